```python
import jax, jax.numpy as jnp
from jax import lax
import numpy as np

D_MODEL = 1024
BATCH = 2
SEQ = 8192
DEPTH = 1

CHUNK = 64
HEAD_DIM = 64
RWKV_HEADS = 8
D_RWKV = RWKV_HEADS * HEAD_DIM
SB_HEADS = 8
D_SB = SB_HEADS * HEAD_DIM
DECAY_LORA = 64
AAA_LORA = 64
GATE_LORA = 128
D_FF = 4 * D_MODEL
Q_BLOCK = 128
DEEPNORM_ALPHA = (2.0 * DEPTH) ** 0.25
DEEPNORM_BETA = (8.0 * DEPTH) ** -0.25
LN_EPS = 1e-5
ADALN_EPS = 1e-6
GN_EPS = 64e-5
RW_SIZES = (D_RWKV, D_RWKV, D_RWKV, DECAY_LORA, AAA_LORA, GATE_LORA)
REST_SIZES = (D_SB, D_SB, D_SB, D_MODEL, D_MODEL)
D_SHIFT = sum(RW_SIZES)
D_IN = D_SHIFT + sum(REST_SIZES)

kernel_name = 'rwkv7_stickbreaking_hybrid_deepnorm_adaln'


def _split(t, sizes):
    return jnp.split(t, np.cumsum(sizes)[:-1].tolist(), axis=-1)


def _layernorm(x, g, b):
    x32 = x.astype(jnp.float32)
    mu = jnp.mean(x32, axis=-1, keepdims=True)
    var = jnp.mean(jnp.square(x32 - mu), axis=-1, keepdims=True)
    return (x32 - mu) * lax.rsqrt(var + LN_EPS) * g + b


def _modulate(x, shift, scale):
    x32 = x.astype(jnp.float32)
    mu = jnp.mean(x32, axis=-1, keepdims=True)
    var = jnp.mean(jnp.square(x32 - mu), axis=-1, keepdims=True)
    return (x32 - mu) * lax.rsqrt(var + ADALN_EPS) * (1.0 + scale) + shift


def _rwkv7_scan(r, w, k, v, a, b):
    def step(state, inp):
        r_t, w_t, k_t, v_t, a_t, b_t = inp
        sa = jnp.einsum('bhvk,bhk->bhv', state, a_t)
        state = (state * w_t[:, :, None, :] + sa[..., None] * b_t[:, :, None, :]
                 + v_t[..., None] * k_t[:, :, None, :])
        return state, jnp.einsum('bhvk,bhk->bhv', state, r_t)
    bsz, _, nh, n = r.shape
    xs = tuple(jnp.moveaxis(t, 1, 0) for t in (r, w, k, v, a, b))
    _, y = lax.scan(step, jnp.zeros((bsz, nh, n, n), jnp.float32), xs)
    return jnp.moveaxis(y, 0, 1)


def _stick_breaking(q, k, v):
    bsz, s_len, nh, d = q.shape
    nb = s_len // Q_BLOCK
    scale = d ** -0.5
    qh = q.astype(jnp.float32).transpose(0, 2, 1, 3)
    kh = k.astype(jnp.float32).transpose(0, 2, 1, 3)
    vh = v.astype(jnp.float32).transpose(0, 2, 1, 3)
    qb = qh.reshape(bsz, nh, nb, Q_BLOCK, d).transpose(2, 0, 1, 3, 4)
    key_pos = jnp.arange(s_len)

    def block(args):
        q_blk, i = args
        q_pos = i * Q_BLOCK + jnp.arange(Q_BLOCK)
        mask = key_pos[None, :] < q_pos[:, None]
        z = jnp.einsum('bhqd,bhkd->bhqk', q_blk, kh) * scale
        log_beta = jax.nn.log_sigmoid(z)
        log_1m = jnp.where(mask, jax.nn.log_sigmoid(-z), 0.0)
        between = lax.cumsum(log_1m, axis=3, reverse=True) - log_1m
        att = jnp.where(mask, jnp.exp(log_beta + between), 0.0)
        return jnp.einsum('bhqk,bhkd->bhqd', att, vh)

    o = lax.map(block, (qb, jnp.arange(nb)))
    return o.transpose(1, 0, 3, 2, 4).reshape(bsz, s_len, nh * d)


def _group_norm(y, g, b):
    mu = jnp.mean(y, axis=-1, keepdims=True)
    var = jnp.mean(jnp.square(y - mu), axis=-1, keepdims=True)
    return (y - mu) * lax.rsqrt(var + GN_EPS) * g.reshape(RWKV_HEADS, HEAD_DIM) + b.reshape(RWKV_HEADS, HEAD_DIM)


def _time_mix(h, w_in, mu_shift, rw_w0, rw_w2, rw_a0, rw_a2, rw_g2, rw_kk, rw_ka, rw_rk,
              rw_gn_g, rw_gn_b, w_branch_a, w_branch_b, w_out):
    bsz, s_len, _ = h.shape
    z = h @ w_in
    z_rw, z_rest = z[..., :D_SHIFT], z[..., D_SHIFT:]
    z_prev = jnp.pad(z_rw, ((0, 0), (1, 0), (0, 0)))[:, :s_len]
    z_rw = z_rw + mu_shift * (z_prev - z_rw)
    r, k, v, zw, za, zg = _split(z_rw, RW_SIZES)
    q_sb, k_sb, v_sb, gate_a, gate_b = _split(z_rest, REST_SIZES)

    heads = lambda t: t.reshape(bsz, s_len, RWKV_HEADS, HEAD_DIM).astype(jnp.float32)
    w_log = -jax.nn.softplus(-(rw_w0 + jnp.tanh(zw) @ rw_w2)) - 0.5
    decay = jnp.exp(-jnp.exp(w_log.astype(jnp.float32)))
    a_rate = jax.nn.sigmoid(rw_a0 + za @ rw_a2)
    g = jax.nn.sigmoid(zg) @ rw_g2
    kk = heads(k * rw_kk)
    kk = kk / jnp.maximum(jnp.sqrt(jnp.sum(kk * kk, axis=-1, keepdims=True)), 1e-12)
    k = k * (1.0 + (a_rate - 1.0) * rw_ka)
    rh, kh, vh, ah = heads(r), heads(k), heads(v), heads(a_rate)
    y = _rwkv7_scan(rh, heads(decay), kh, vh, -kk, kk * ah)
    y = _group_norm(y, rw_gn_g, rw_gn_b)
    y = y + jnp.sum(rh * kh * rw_rk, axis=-1, keepdims=True) * vh
    y_a = (y.reshape(bsz, s_len, D_RWKV) * g) @ w_branch_a

    sb_heads = lambda t: t.reshape(bsz, s_len, SB_HEADS, HEAD_DIM)
    o = _stick_breaking(sb_heads(q_sb), sb_heads(k_sb), sb_heads(v_sb))
    y_b = o @ w_branch_b

    merged = jax.nn.sigmoid(gate_a) * y_a + jax.nn.sigmoid(gate_b) * y_b
    return merged @ w_out


def _layer(x, c, w_ada, b_ada, w_in, mu_shift, rw_w0, rw_w2, rw_a0, rw_a2, rw_g2, rw_kk, rw_ka,
           rw_rk, rw_gn_g, rw_gn_b, w_branch_a, w_branch_b, w_out, ln1_g, ln1_b,
           w_ff1, b_ff1, w_ff2, b_ff2, ln2_g, ln2_b):
    ada = jax.nn.silu(c) @ w_ada + b_ada
    sh1, sc1, g1, sh2, sc2, g2 = jnp.split(ada[:, None, :], 6, axis=-1)
    h = _modulate(x, sh1, sc1)
    mix = _time_mix(h, w_in, mu_shift, rw_w0, rw_w2, rw_a0, rw_a2, rw_g2, rw_kk, rw_ka, rw_rk,
                    rw_gn_g, rw_gn_b, w_branch_a, w_branch_b, w_out)
    x = _layernorm(DEEPNORM_ALPHA * x + g1 * mix, ln1_g, ln1_b)
    h = _modulate(x, sh2, sc2)
    ff = jnp.square(jax.nn.relu(h @ w_ff1 + b_ff1)) @ w_ff2 + b_ff2
    return _layernorm(DEEPNORM_ALPHA * x + g2 * ff, ln2_g, ln2_b)


def setup_inputs(seed: int = 0) -> dict:
    key = jax.random.key(seed)
    ks = jax.random.split(key, 28)
    nrm = lambda i, shape, s: jax.random.normal(ks[i], shape, jnp.float32) * s
    L = DEPTH
    beta = DEEPNORM_BETA
    col_scale = jnp.concatenate([
        jnp.ones((2 * D_RWKV,), jnp.float32), jnp.full((D_RWKV,), beta, jnp.float32),
        jnp.ones((DECAY_LORA + AAA_LORA + GATE_LORA + 2 * D_SB,), jnp.float32),
        jnp.full((D_SB,), beta, jnp.float32), jnp.ones((2 * D_MODEL,), jnp.float32)])
    return {
        'x': nrm(0, (BATCH, SEQ, D_MODEL), 1.0),
        'c': nrm(1, (BATCH, D_MODEL), 1.0),
        'w_ada': nrm(2, (L, D_MODEL, 6 * D_MODEL), D_MODEL ** -0.5),
        'b_ada': nrm(3, (L, 6 * D_MODEL), 0.02),
        'w_in': nrm(4, (L, D_MODEL, D_IN), D_MODEL ** -0.5) * col_scale,
        'mu_shift': jax.random.uniform(ks[5], (L, D_SHIFT), jnp.float32),
        'rw_w0': jax.random.uniform(ks[6], (L, D_RWKV), jnp.float32, -6.0, 1.0),
        'rw_w2': nrm(7, (L, DECAY_LORA, D_RWKV), 0.1 * DECAY_LORA ** -0.5),
        'rw_a0': nrm(8, (L, D_RWKV), 0.5),
        'rw_a2': nrm(9, (L, AAA_LORA, D_RWKV), 0.5 * AAA_LORA ** -0.5),
        'rw_g2': nrm(10, (L, GATE_LORA, D_RWKV), GATE_LORA ** -0.5),
        'rw_kk': 0.85 + nrm(11, (L, D_RWKV), 0.05),
        'rw_ka': 1.0 + nrm(12, (L, D_RWKV), 0.05),
        'rw_rk': nrm(13, (L, RWKV_HEADS, HEAD_DIM), 0.1),
        'rw_gn_g': 1.0 + nrm(14, (L, D_RWKV), 0.05),
        'rw_gn_b': nrm(15, (L, D_RWKV), 0.02),
        'w_branch_a': nrm(16, (L, D_RWKV, D_MODEL), beta * D_RWKV ** -0.5),
        'w_branch_b': nrm(17, (L, D_SB, D_MODEL), beta * D_SB ** -0.5),
        'w_out': nrm(18, (L, D_MODEL, D_MODEL), beta * D_MODEL ** -0.5),
        'ln1_g': 1.0 + nrm(19, (L, D_MODEL), 0.05),
        'ln1_b': nrm(20, (L, D_MODEL), 0.02),
        'w_ff1': nrm(21, (L, D_MODEL, D_FF), beta * D_MODEL ** -0.5),
        'b_ff1': nrm(22, (L, D_FF), 0.02),
        'w_ff2': nrm(23, (L, D_FF, D_MODEL), beta * D_FF ** -0.5),
        'b_ff2': nrm(24, (L, D_MODEL), 0.02),
        'ln2_g': 1.0 + nrm(25, (L, D_MODEL), 0.05),
        'ln2_b': nrm(26, (L, D_MODEL), 0.02),
    }


def reference(x, c, w_ada, b_ada, w_in, mu_shift, rw_w0, rw_w2, rw_a0, rw_a2, rw_g2, rw_kk, rw_ka,
              rw_rk, rw_gn_g, rw_gn_b, w_branch_a, w_branch_b, w_out, ln1_g, ln1_b,
              w_ff1, b_ff1, w_ff2, b_ff2, ln2_g, ln2_b):
    in_dtype = x.dtype
    layer_params = (w_ada, b_ada, w_in, mu_shift, rw_w0, rw_w2, rw_a0, rw_a2, rw_g2, rw_kk, rw_ka,
                    rw_rk, rw_gn_g, rw_gn_b, w_branch_a, w_branch_b, w_out, ln1_g, ln1_b,
                    w_ff1, b_ff1, w_ff2, b_ff2, ln2_g, ln2_b)
    for l in range(DEPTH):
        x = _layer(x, c, *[p[l] for p in layer_params])
    return x.astype(in_dtype)
```

```python
import functools

import jax
import jax.numpy as jnp
from jax import lax
from jax.experimental import pallas as pl
from jax.experimental.pallas import tpu as pltpu

F32 = jnp.float32
BF16 = jnp.bfloat16

HEAD_DIM = 64
LN_EPS = 1e-5
ADALN_EPS = 1e-6
GN_EPS = 64e-5
RWKV_CHUNK = 64
SB_BLOCK = 128
VMEM_LIMIT = 56 * 1024 * 1024


def _split2(a):
    hi = a.astype(BF16)
    lo = (a - hi.astype(F32)).astype(BF16)
    return hi, lo


def _split3(a):
    hi = a.astype(BF16)
    r1 = a - hi.astype(F32)
    mid = r1.astype(BF16)
    lo = (r1 - mid.astype(F32)).astype(BF16)
    return hi, mid, lo


_NN = (((1,), (0,)), ((), ()))
_NT = (((1,), (1,)), ((), ()))
_TN = (((0,), (0,)), ((), ()))


def _dot(a, b, dims=_NN):
    return lax.dot_general(a, b, dims, preferred_element_type=F32)


def _dot3(a, b, dims=_NN):
    ah, al = _split2(a)
    bh, bl = _split2(b)
    return _dot(ah, bh, dims) + (_dot(al, bh, dims) + _dot(ah, bl, dims))


def _dot_exact_rhs(a, b_bf, dims=_NN):
    a0, a1, a2 = _split3(a)
    return _dot(a0, b_bf, dims) + (_dot(a1, b_bf, dims) + _dot(a2, b_bf, dims))


def _dot_exact_lhs(a_bf, b, dims=_NN):
    b0, b1, b2 = _split3(b)
    return _dot(a_bf, b0, dims) + (_dot(a_bf, b1, dims) + _dot(a_bf, b2, dims))


def _softplus(y):
    return jnp.maximum(y, 0.0) + jnp.log(1.0 + jnp.exp(-jnp.abs(y)))


def _sigmoid(y):
    return 1.0 / (1.0 + jnp.exp(-y))


def _norm_rows(x, eps):
    mu = jnp.mean(x, axis=-1, keepdims=True)
    xc = x - mu
    var = jnp.mean(xc * xc, axis=-1, keepdims=True)
    return xc * lax.rsqrt(var + eps)


def _ada_kernel(c_ref, w_ref, b_ref, o_ref):
    c = c_ref[...]
    s = c * _sigmoid(c)
    o_ref[...] = jnp.dot(s, w_ref[...], preferred_element_type=F32,
                         precision=lax.Precision.HIGHEST) + b_ref[...]


def _ada(c, w_ada, b_ada):
    bsz, d = c.shape
    n = w_ada.shape[1]
    return pl.pallas_call(
        _ada_kernel,
        grid=(n // d,),
        in_specs=[pl.BlockSpec((bsz, d), lambda j: (0, 0)),
                  pl.BlockSpec((d, d), lambda j: (0, j)),
                  pl.BlockSpec((1, d), lambda j: (0, j))],
        out_specs=pl.BlockSpec((bsz, d), lambda j: (0, j)),
        out_shape=jax.ShapeDtypeStruct((bsz, n), F32),
        name="ada",
    )(c, w_ada, b_ada.reshape(1, n))


def _inproj_kernel(x_ref, ada_ref, win_ref, mu_ref, w0_ref, w2_ref, a0_ref, a2_ref, g2_ref,
                   kkw_ref, kaw_ref, hsum_ref,
                   r_ref, k_ref, v_ref, lw_ref, kk_ref, b_ref, g_ref,
                   q_ref, ks_ref, vs_ref, ga_ref, gb_ref,
                   carry_ref, *, tiles_per_batch, d_rwkv, d_decay, d_aaa, d_gate, d_sb, d_model):
    i = pl.program_id(0)
    tm = x_ref.shape[0]
    d_shift = 3 * d_rwkv + d_decay + d_aaa + d_gate

    sh = ada_ref[0, 0:1, :]
    sc = ada_ref[0, 1:2, :]
    h = _norm_rows(x_ref[...], ADALN_EPS) * (1.0 + sc) + sh
    hb = h.astype(BF16)

    zrw = _dot(hb, win_ref[:, :d_shift])

    @pl.when(i % tiles_per_batch == 0)
    def _():
        carry_ref[...] = jnp.zeros_like(carry_ref)

    row = lax.broadcasted_iota(jnp.int32, zrw.shape, 0)
    prev = jnp.where(row == 0, carry_ref[...], pltpu.roll(zrw, 1, 0))
    carry_ref[...] = zrw[tm - 1:tm, :]
    zs = zrw + mu_ref[...] * (prev - zrw)

    o = 0
    r = zs[:, o:o + d_rwkv]; o += d_rwkv
    k = zs[:, o:o + d_rwkv]; o += d_rwkv
    v = zs[:, o:o + d_rwkv]; o += d_rwkv
    zw = zs[:, o:o + d_decay]; o += d_decay
    za = zs[:, o:o + d_aaa]; o += d_aaa
    zg = zs[:, o:o + d_gate]

    ww = w0_ref[...] + _dot3(jnp.tanh(zw), w2_ref[...])
    w_log = -_softplus(-ww) - 0.5
    lw_ref[...] = -jnp.exp(w_log)
    arate = _sigmoid(a0_ref[...] + _dot3(za, a2_ref[...]))
    g_ref[...] = _dot3(_sigmoid(zg), g2_ref[...])
    kkraw = k * kkw_ref[...]
    ssq = _dot_exact_rhs(kkraw * kkraw, hsum_ref[...])
    kk = kkraw / jnp.maximum(jnp.sqrt(ssq), 1e-12)
    r_ref[...] = r
    k_ref[...] = k * (1.0 + (arate - 1.0) * kaw_ref[...])
    v_ref[...] = v
    kk_ref[...] = kk
    b_ref[...] = kk * arate

    o = d_shift
    q_ref[...] = (_dot(hb, win_ref[:, o:o + d_sb]) * (HEAD_DIM ** -0.5)).astype(BF16); o += d_sb
    ks_ref[...] = _dot(hb, win_ref[:, o:o + d_sb]).astype(BF16); o += d_sb
    vs_ref[...] = _dot(hb, win_ref[:, o:o + d_sb]).astype(BF16); o += d_sb
    ga_ref[...] = _sigmoid(_dot(hb, win_ref[:, o:o + d_model])).astype(BF16); o += d_model
    gb_ref[...] = _sigmoid(_dot(hb, win_ref[:, o:o + d_model])).astype(BF16)


def _inproj(x2, ada3, win_bf, mu_shift, rw_w0, rw_w2, rw_a0, rw_a2, rw_g2, rw_kk, rw_ka, hsum,
            *, seq, tm):
    m, d_model = x2.shape
    d_rwkv = rw_w0.shape[-1]
    d_decay, d_aaa, d_gate = rw_w2.shape[0], rw_a2.shape[0], rw_g2.shape[0]
    d_shift = 3 * d_rwkv + d_decay + d_aaa + d_gate
    d_sb = (win_bf.shape[1] - d_shift - 2 * d_model) // 3
    tiles_per_batch = seq // tm
    row = lambda i: (i, 0)
    const = lambda i: (0, 0)
    full = lambda a: pl.BlockSpec(a.shape, const)
    vec = lambda a: a.reshape(1, -1)
    ins = [x2, ada3, win_bf, vec(mu_shift), vec(rw_w0), rw_w2, vec(rw_a0), rw_a2, rw_g2,
           vec(rw_kk), vec(rw_ka), hsum]
    in_specs = [pl.BlockSpec((tm, d_model), row),
                pl.BlockSpec((1,) + ada3.shape[1:], lambda i: (i // tiles_per_batch, 0, 0))]
    in_specs += [full(a) for a in ins[2:]]
    out_shape = ([jax.ShapeDtypeStruct((m, d_rwkv), F32)] * 7
                 + [jax.ShapeDtypeStruct((m, d_sb), BF16)] * 3
                 + [jax.ShapeDtypeStruct((m, d_model), BF16)] * 2)
    out_specs = ([pl.BlockSpec((tm, d_rwkv), row)] * 7 + [pl.BlockSpec((tm, d_sb), row)] * 3
                 + [pl.BlockSpec((tm, d_model), row)] * 2)
    kern = functools.partial(_inproj_kernel, tiles_per_batch=tiles_per_batch, d_rwkv=d_rwkv,
                             d_decay=d_decay, d_aaa=d_aaa, d_gate=d_gate, d_sb=d_sb, d_model=d_model)
    return pl.pallas_call(
        kern,
        grid=(m // tm,),
        in_specs=in_specs,
        out_specs=out_specs,
        out_shape=out_shape,
        scratch_shapes=[pltpu.VMEM((1, d_shift), F32)],
        compiler_params=pltpu.CompilerParams(dimension_semantics=("arbitrary",),
                                             vmem_limit_bytes=VMEM_LIMIT),
        name="inproj",
    )(*ins)


def _tri_inverse(m, row, col):
    n = m.shape[0]
    same = lambda s: (row >> s) == (col >> s)
    t = (row == col).astype(F32) + jnp.where(same(1), m, 0.0)
    s = 1
    while (1 << s) < n:
        off = jnp.where(same(s + 1) & jnp.logical_not(same(s)), m, 0.0)
        t = t + _dot3(_dot3(t, off), t)
        s += 1
    return t


def _rwkv_kernel(r_ref, k_ref, v_ref, lw_ref, kk_ref, b_ref, g_ref, rk_ref, gng_ref, gnb_ref, hsum_ref,
                 o_ref, state_ref, *, heads):
    c = pl.program_id(1)
    n = HEAD_DIM
    ch = r_ref.shape[1]

    @pl.when(c == 0)
    def _():
        state_ref[...] = jnp.zeros_like(state_ref)

    row = lax.broadcasted_iota(jnp.int32, (ch, ch), 0)
    col = lax.broadcasted_iota(jnp.int32, (ch, ch), 1)
    strict = row > col
    incl = row >= col
    ltri = jnp.where(incl, 1.0, 0.0).astype(BF16)

    r = r_ref[0]
    k = k_ref[0]
    v = v_ref[0]
    lw = lw_ref[0]
    kk = kk_ref[0]
    b = b_ref[0]

    cum = _dot_exact_lhs(ltri, lw)
    tot = cum[ch - 1:ch, :]
    a_t = -kk * jnp.exp(cum - lw)
    r_t = r * jnp.exp(cum)
    e_neg = jnp.exp(-cum)
    b_t = b * e_neg
    k_t = k * e_neg
    e_bar = jnp.exp(tot - cum)
    b_bar = b * e_bar
    k_bar = k * e_bar
    w_tot = jnp.exp(tot)

    ys = []
    for h in range(heads):
        sl = slice(h * n, (h + 1) * n)
        s0 = state_ref[h]
        lhs = jnp.concatenate([a_t[:, sl], r_t[:, sl]], axis=0)
        rhs = jnp.concatenate([b_t[:, sl], k_t[:, sl]], axis=0)
        gram = _dot3(lhs, rhs, _NT)
        m_ab = jnp.where(strict, gram[:ch, :ch], 0.0)
        m_ak = jnp.where(strict, gram[:ch, ch:], 0.0)
        p_rb = jnp.where(incl, gram[ch:, :ch], 0.0)
        p_rk = jnp.where(incl, gram[ch:, ch:], 0.0)
        vh = v[:, sl]
        t_inv = _tri_inverse(m_ab, row, col)
        a_hat = _dot3(t_inv, a_t[:, sl])
        u0 = _dot3(t_inv, _dot3(m_ak, vh))
        r_hat = r_t[:, sl] + _dot3(p_rb, a_hat)
        y0 = _dot3(p_rk, vh) + _dot3(p_rb, u0)
        g_mat = _dot3(a_hat, b_bar[:, sl], _TN)
        h_mat = _dot3(u0, b_bar[:, sl], _TN) + _dot3(vh, k_bar[:, sl], _TN)
        ys.append(_dot3(r_hat, s0, _NT) + y0)
        state_ref[h] = s0 * w_tot[:, sl] + _dot3(s0, g_mat) + h_mat

    y = jnp.concatenate(ys, axis=-1)
    hsum = hsum_ref[...]
    inv_n = 1.0 / n
    mu = _dot_exact_rhs(y, hsum) * inv_n
    yc = y - mu
    var = _dot_exact_rhs(yc * yc, hsum) * inv_n
    yn = yc * lax.rsqrt(var + GN_EPS) * gng_ref[...] + gnb_ref[...]
    bonus = _dot_exact_rhs(r * k * rk_ref[...], hsum)
    o_ref[0] = ((yn + bonus * v) * g_ref[0]).astype(o_ref.dtype)


def _rwkv(r, k, v, lw, kk, b, g, rw_rk, gn_g, gn_b, hsum, *, chunk):
    bsz, seq, d = r.shape
    heads = d // HEAD_DIM
    blk = pl.BlockSpec((1, chunk, d), lambda bi, ci: (bi, ci, 0))
    const = lambda bi, ci: (0, 0)
    vec = lambda a: a.reshape(1, d)
    return pl.pallas_call(
        functools.partial(_rwkv_kernel, heads=heads),
        grid=(bsz, seq // chunk),
        in_specs=[blk] * 7 + [pl.BlockSpec((1, d), const)] * 3 + [pl.BlockSpec((d, d), const)],
        out_specs=blk,
        out_shape=jax.ShapeDtypeStruct((bsz, seq, d), BF16),
        scratch_shapes=[pltpu.VMEM((heads, HEAD_DIM, HEAD_DIM), F32)],
        compiler_params=pltpu.CompilerParams(dimension_semantics=("arbitrary", "arbitrary")),
        name="rwkv",
    )(r, k, v, lw, kk, b, g, vec(rw_rk), vec(gn_g), vec(gn_b), hsum)


def _sb_kernel(q_ref, k_ref, v_ref, o_ref, *, blk):
    qi = pl.program_id(2)
    q = q_ref[0, 0]
    row = lax.broadcasted_iota(jnp.int32, (blk, blk), 0)
    col = lax.broadcasted_iota(jnp.int32, (blk, blk), 1)
    causal = col < row
    jj = lax.broadcasted_iota(jnp.int32, (blk, 2 * blk), 0)
    ss = lax.broadcasted_iota(jnp.int32, (blk, 2 * blk), 1)
    suffix = jnp.where((jj >= ss) | (ss >= blk), 1.0, 0.0).astype(BF16)

    def body(i, carry):
        acc, run = carry
        start = pl.multiple_of((qi - i) * blk, blk)
        ks = k_ref[0, 0, pl.ds(start, blk), :]
        vs = v_ref[0, 0, pl.ds(start, blk), :]
        z = _dot(q, ks, _NT)
        valid = causal | (i > 0)
        sp = jnp.where(valid, _softplus(z), 0.0)
        hi, lo = _split2(sp)
        cs = _dot(hi, suffix) + _dot(lo, suffix)
        att = jnp.where(valid, jnp.exp(z - cs[:, :blk] - run), 0.0)
        acc = acc + _dot(att.astype(BF16), vs)
        return acc, run + cs[:, blk:]

    acc0 = jnp.zeros((blk, q.shape[1]), F32)
    run0 = jnp.zeros((blk, blk), F32)
    acc, _ = lax.fori_loop(0, qi + 1, body, (acc0, run0))
    o_ref[0, 0] = acc.astype(o_ref.dtype)


def _sb_attention(q, k, v, *, blk):
    bsz, heads, seq, d = q.shape
    qspec = pl.BlockSpec((1, 1, blk, d), lambda b, h, i: (b, h, i, 0))
    kvspec = pl.BlockSpec((1, 1, seq, d), lambda b, h, i: (b, h, 0, 0))
    return pl.pallas_call(
        functools.partial(_sb_kernel, blk=blk),
        grid=(bsz, heads, seq // blk),
        in_specs=[qspec, kvspec, kvspec],
        out_specs=qspec,
        out_shape=jax.ShapeDtypeStruct((bsz, heads, seq, d), BF16),
        compiler_params=pltpu.CompilerParams(
            dimension_semantics=("arbitrary", "arbitrary", "arbitrary")),
        name="sbattn",
    )(q, k, v)


def _merge_kernel(x_ref, ada_ref, ya_ref, ob_ref, ga_ref, gb_ref, wa_ref, wb_ref, wo_ref,
                  lng_ref, lnb_ref, o_ref, *, alpha):
    g1 = ada_ref[0, 2:3, :]
    y_a = _dot(ya_ref[...], wa_ref[...])
    y_b = _dot(ob_ref[...], wb_ref[...])
    merged = ga_ref[...].astype(F32) * y_a + gb_ref[...].astype(F32) * y_b
    mix = _dot(merged.astype(BF16), wo_ref[...])
    o_ref[...] = _norm_rows(alpha * x_ref[...] + g1 * mix, LN_EPS) * lng_ref[...] + lnb_ref[...]


def _merge(x2, ada3, ya, ob, ga, gb, wa_bf, wb_bf, wo_bf, ln_g, ln_b, *, seq, tm, alpha):
    m, d = x2.shape
    tiles_per_batch = seq // tm
    row = lambda i: (i, 0)
    const = lambda i: (0, 0)
    rows = lambda a: pl.BlockSpec((tm, a.shape[1]), row)
    full = lambda a: pl.BlockSpec(a.shape, const)
    vec = lambda a: a.reshape(1, -1)
    ins = [x2, ada3, ya, ob, ga, gb, wa_bf, wb_bf, wo_bf, vec(ln_g), vec(ln_b)]
    in_specs = [rows(x2), pl.BlockSpec((1,) + ada3.shape[1:], lambda i: (i // tiles_per_batch, 0, 0)),
                rows(ya), rows(ob), rows(ga), rows(gb)] + [full(a) for a in ins[6:]]
    return pl.pallas_call(
        functools.partial(_merge_kernel, alpha=alpha),
        grid=(m // tm,),
        in_specs=in_specs,
        out_specs=pl.BlockSpec((tm, d), row),
        out_shape=jax.ShapeDtypeStruct((m, d), F32),
        compiler_params=pltpu.CompilerParams(dimension_semantics=("arbitrary",),
                                             vmem_limit_bytes=VMEM_LIMIT),
        name="merge",
    )(*ins)


def _ffn_kernel(x_ref, ada_ref, w1_ref, b1_ref, w2_ref, b2_ref, lng_ref, lnb_ref, o_ref, *, alpha, fchunk):
    sh = ada_ref[0, 3:4, :]
    sc = ada_ref[0, 4:5, :]
    g2 = ada_ref[0, 5:6, :]
    x = x_ref[...]
    hb = (_norm_rows(x, ADALN_EPS) * (1.0 + sc) + sh).astype(BF16)
    d_ff = w1_ref.shape[1]
    ff = jnp.zeros(x.shape, F32)
    for j in range(d_ff // fchunk):
        sl = slice(j * fchunk, (j + 1) * fchunk)
        t = jnp.maximum(_dot(hb, w1_ref[:, sl]) + b1_ref[:, sl], 0.0)
        ff = ff + _dot((t * t).astype(BF16), w2_ref[sl, :])
    ff = ff + b2_ref[...]
    o_ref[...] = _norm_rows(alpha * x + g2 * ff, LN_EPS) * lng_ref[...] + lnb_ref[...]


def _ffn(x1, ada3, w1_bf, b1, w2_bf, b2, ln_g, ln_b, *, seq, tm, alpha, fchunk):
    m, d = x1.shape
    tiles_per_batch = seq // tm
    row = lambda i: (i, 0)
    const = lambda i: (0, 0)
    full = lambda a: pl.BlockSpec(a.shape, const)
    vec = lambda a: a.reshape(1, -1)
    ins = [x1, ada3, w1_bf, vec(b1), w2_bf, vec(b2), vec(ln_g), vec(ln_b)]
    in_specs = [pl.BlockSpec((tm, d), row),
                pl.BlockSpec((1,) + ada3.shape[1:], lambda i: (i // tiles_per_batch, 0, 0))]
    in_specs += [full(a) for a in ins[2:]]
    return pl.pallas_call(
        functools.partial(_ffn_kernel, alpha=alpha, fchunk=fchunk),
        grid=(m // tm,),
        in_specs=in_specs,
        out_specs=pl.BlockSpec((tm, d), row),
        out_shape=jax.ShapeDtypeStruct((m, d), F32),
        compiler_params=pltpu.CompilerParams(dimension_semantics=("arbitrary",),
                                             vmem_limit_bytes=VMEM_LIMIT),
        name="ffn",
    )(*ins)


def _branches(proj, rw_rk, rw_gn_g, rw_gn_b, hsum, bsz, seq):
    r, k, v, lw, kk, b, g, q, ks, vs = proj
    seq3 = lambda a: a.reshape(bsz, seq, -1)
    ya = _rwkv(seq3(r), seq3(k), seq3(v), seq3(lw), seq3(kk), seq3(b), seq3(g),
               rw_rk, rw_gn_g, rw_gn_b, hsum, chunk=min(RWKV_CHUNK, seq))
    to_heads = lambda a: a.reshape(bsz, seq, -1, HEAD_DIM).transpose(0, 2, 1, 3)
    o = _sb_attention(to_heads(q), to_heads(ks), to_heads(vs), blk=min(SB_BLOCK, seq))
    ob = o.transpose(0, 2, 1, 3).reshape(bsz * seq, -1)
    return ya.reshape(bsz * seq, -1), ob


def _layer(x, c, w_ada, b_ada, w_in, mu_shift, rw_w0, rw_w2, rw_a0, rw_a2, rw_g2, rw_kk, rw_ka,
           rw_rk, rw_gn_g, rw_gn_b, w_branch_a, w_branch_b, w_out, ln1_g, ln1_b,
           w_ff1, b_ff1, w_ff2, b_ff2, ln2_g, ln2_b, *, alpha):
    bsz, seq, d = x.shape
    d_rwkv = rw_w0.shape[-1]
    heads = d_rwkv // HEAD_DIM
    tm = min(256, seq)

    ada3 = _ada(c, w_ada, b_ada).reshape(bsz, 6, d)
    lane = jnp.arange(d_rwkv) // HEAD_DIM
    hsum = (lane[:, None] == lane[None, :]).astype(BF16)

    x2 = x.reshape(bsz * seq, d)
    (r, k, v, lw, kk, b, g, q, ks, vs, ga, gb) = _inproj(
        x2, ada3, w_in.astype(BF16), mu_shift, rw_w0, rw_w2, rw_a0, rw_a2, rw_g2, rw_kk, rw_ka, hsum,
        seq=seq, tm=tm)

    ya, ob = _branches((r, k, v, lw, kk, b, g, q, ks, vs), rw_rk, rw_gn_g, rw_gn_b, hsum, bsz, seq)

    x1 = _merge(x2, ada3, ya, ob, ga, gb,
                w_branch_a.astype(BF16), w_branch_b.astype(BF16), w_out.astype(BF16),
                ln1_g, ln1_b, seq=seq, tm=tm, alpha=alpha)
    out = _ffn(x1, ada3, w_ff1.astype(BF16), b_ff1, w_ff2.astype(BF16), b_ff2, ln2_g, ln2_b,
               seq=seq, tm=tm, alpha=alpha, fchunk=min(1024, w_ff1.shape[-1]))
    return out.reshape(bsz, seq, d)


def kernel(x, c, w_ada, b_ada, w_in, mu_shift, rw_w0, rw_w2, rw_a0, rw_a2, rw_g2, rw_kk, rw_ka, rw_rk,
           rw_gn_g, rw_gn_b, w_branch_a, w_branch_b, w_out, ln1_g, ln1_b, w_ff1, b_ff1, w_ff2, b_ff2,
           ln2_g, ln2_b):
    in_dtype = x.dtype
    layer_params = (w_ada, b_ada, w_in, mu_shift, rw_w0, rw_w2, rw_a0, rw_a2, rw_g2, rw_kk, rw_ka,
                    rw_rk, rw_gn_g, rw_gn_b, w_branch_a, w_branch_b, w_out, ln1_g, ln1_b,
                    w_ff1, b_ff1, w_ff2, b_ff2, ln2_g, ln2_b)
    depth = w_ada.shape[0]
    alpha = (2.0 * depth) ** 0.25
    for l in range(depth):
        x = _layer(x, c, *[p[l] for p in layer_params], alpha=alpha)
    return x.astype(in_dtype)
```

```python
import functools

import jax
import jax.numpy as jnp
from jax import lax
from jax.experimental import pallas as pl
from jax.experimental.pallas import tpu as pltpu

F32 = jnp.float32
BF16 = jnp.bfloat16

HEAD_DIM = 64
LN_EPS = 1e-5
ADALN_EPS = 1e-6
GN_EPS = 64e-5
RWKV_CHUNK = 64
SB_BLOCK = 128
VMEM_LIMIT = 56 * 1024 * 1024


def _split2(a):
    hi = a.astype(BF16)
    lo = (a - hi.astype(F32)).astype(BF16)
    return hi, lo


def _split3(a):
    hi = a.astype(BF16)
    r1 = a - hi.astype(F32)
    mid = r1.astype(BF16)
    lo = (r1 - mid.astype(F32)).astype(BF16)
    return hi, mid, lo


_NN = (((1,), (0,)), ((), ()))
_NT = (((1,), (1,)), ((), ()))
_TN = (((0,), (0,)), ((), ()))


def _dot(a, b, dims=_NN):
    return lax.dot_general(a, b, dims, preferred_element_type=F32)


def _dot3(a, b, dims=_NN):
    ah, al = _split2(a)
    bh, bl = _split2(b)
    return _dot(ah, bh, dims) + (_dot(al, bh, dims) + _dot(ah, bl, dims))


def _dot_exact_rhs(a, b_bf, dims=_NN):
    a0, a1, a2 = _split3(a)
    return _dot(a0, b_bf, dims) + (_dot(a1, b_bf, dims) + _dot(a2, b_bf, dims))


def _dot_exact_lhs(a_bf, b, dims=_NN):
    b0, b1, b2 = _split3(b)
    return _dot(a_bf, b0, dims) + (_dot(a_bf, b1, dims) + _dot(a_bf, b2, dims))


def _softplus(y):
    return jnp.maximum(y, 0.0) + jnp.log(1.0 + jnp.exp(-jnp.abs(y)))


def _sigmoid(y):
    return 1.0 / (1.0 + jnp.exp(-y))


def _norm_rows(x, eps):
    mu = jnp.mean(x, axis=-1, keepdims=True)
    xc = x - mu
    var = jnp.mean(xc * xc, axis=-1, keepdims=True)
    return xc * lax.rsqrt(var + eps)


def _ada_kernel(c_ref, w_ref, b_ref, o_ref):
    c = c_ref[...]
    s = c * _sigmoid(c)
    o_ref[...] = jnp.dot(s, w_ref[...], preferred_element_type=F32,
                         precision=lax.Precision.HIGHEST) + b_ref[...]


def _ada(c, w_ada, b_ada):
    bsz, d = c.shape
    n = w_ada.shape[1]
    return pl.pallas_call(
        _ada_kernel,
        grid=(n // d,),
        in_specs=[pl.BlockSpec((bsz, d), lambda j: (0, 0)),
                  pl.BlockSpec((d, d), lambda j: (0, j)),
                  pl.BlockSpec((1, d), lambda j: (0, j))],
        out_specs=pl.BlockSpec((bsz, d), lambda j: (0, j)),
        out_shape=jax.ShapeDtypeStruct((bsz, n), F32),
        name="ada",
    )(c, w_ada, b_ada.reshape(1, n))


def _inproj_kernel(x_ref, ada_ref, win_ref, mu_ref, w0_ref, w2_ref, a0_ref, a2_ref, g2_ref,
                   kkw_ref, kaw_ref, hsum_ref,
                   r_ref, k_ref, v_ref, lw_ref, kk_ref, b_ref, g_ref,
                   q_ref, ks_ref, vs_ref, ga_ref, gb_ref,
                   carry_ref, *, tiles_per_batch, d_rwkv, d_decay, d_aaa, d_gate, d_sb, d_model):
    i = pl.program_id(0)
    tm = x_ref.shape[0]
    d_shift = 3 * d_rwkv + d_decay + d_aaa + d_gate

    sh = ada_ref[0, 0:1, :]
    sc = ada_ref[0, 1:2, :]
    h = _norm_rows(x_ref[...], ADALN_EPS) * (1.0 + sc) + sh
    hb = h.astype(BF16)

    zrw = _dot(hb, win_ref[:, :d_shift])

    @pl.when(i % tiles_per_batch == 0)
    def _():
        carry_ref[...] = jnp.zeros_like(carry_ref)

    row = lax.broadcasted_iota(jnp.int32, zrw.shape, 0)
    prev = jnp.where(row == 0, carry_ref[...], pltpu.roll(zrw, 1, 0))
    carry_ref[...] = zrw[tm - 1:tm, :]
    zs = zrw + mu_ref[...] * (prev - zrw)

    o = 0
    r = zs[:, o:o + d_rwkv]; o += d_rwkv
    k = zs[:, o:o + d_rwkv]; o += d_rwkv
    v = zs[:, o:o + d_rwkv]; o += d_rwkv
    zw = zs[:, o:o + d_decay]; o += d_decay
    za = zs[:, o:o + d_aaa]; o += d_aaa
    zg = zs[:, o:o + d_gate]

    ww = w0_ref[...] + _dot3(jnp.tanh(zw), w2_ref[...])
    w_log = -_softplus(-ww) - 0.5
    lw_ref[...] = -jnp.exp(w_log)
    arate = _sigmoid(a0_ref[...] + _dot3(za, a2_ref[...]))
    g_ref[...] = _dot3(_sigmoid(zg), g2_ref[...])
    kkraw = k * kkw_ref[...]
    ssq = _dot_exact_rhs(kkraw * kkraw, hsum_ref[...])
    kk = kkraw / jnp.maximum(jnp.sqrt(ssq), 1e-12)
    r_ref[...] = r
    k_ref[...] = k * (1.0 + (arate - 1.0) * kaw_ref[...])
    v_ref[...] = v
    kk_ref[...] = kk
    b_ref[...] = kk * arate

    o = d_shift
    q_ref[...] = (_dot(hb, win_ref[:, o:o + d_sb]) * (HEAD_DIM ** -0.5)).astype(BF16); o += d_sb
    ks_ref[...] = _dot(hb, win_ref[:, o:o + d_sb]).astype(BF16); o += d_sb
    vs_ref[...] = _dot(hb, win_ref[:, o:o + d_sb]).astype(BF16); o += d_sb
    ga_ref[...] = _sigmoid(_dot(hb, win_ref[:, o:o + d_model])).astype(BF16); o += d_model
    gb_ref[...] = _sigmoid(_dot(hb, win_ref[:, o:o + d_model])).astype(BF16)


def _inproj(x2, ada3, win_bf, mu_shift, rw_w0, rw_w2, rw_a0, rw_a2, rw_g2, rw_kk, rw_ka, hsum,
            *, seq, tm):
    m, d_model = x2.shape
    d_rwkv = rw_w0.shape[-1]
    d_decay, d_aaa, d_gate = rw_w2.shape[0], rw_a2.shape[0], rw_g2.shape[0]
    d_shift = 3 * d_rwkv + d_decay + d_aaa + d_gate
    d_sb = (win_bf.shape[1] - d_shift - 2 * d_model) // 3
    tiles_per_batch = seq // tm
    row = lambda i: (i, 0)
    const = lambda i: (0, 0)
    full = lambda a: pl.BlockSpec(a.shape, const)
    vec = lambda a: a.reshape(1, -1)
    ins = [x2, ada3, win_bf, vec(mu_shift), vec(rw_w0), rw_w2, vec(rw_a0), rw_a2, rw_g2,
           vec(rw_kk), vec(rw_ka), hsum]
    in_specs = [pl.BlockSpec((tm, d_model), row),
                pl.BlockSpec((1,) + ada3.shape[1:], lambda i: (i // tiles_per_batch, 0, 0))]
    in_specs += [full(a) for a in ins[2:]]
    out_shape = ([jax.ShapeDtypeStruct((m, d_rwkv), F32)] * 7
                 + [jax.ShapeDtypeStruct((m, d_sb), BF16)] * 3
                 + [jax.ShapeDtypeStruct((m, d_model), BF16)] * 2)
    out_specs = ([pl.BlockSpec((tm, d_rwkv), row)] * 7 + [pl.BlockSpec((tm, d_sb), row)] * 3
                 + [pl.BlockSpec((tm, d_model), row)] * 2)
    kern = functools.partial(_inproj_kernel, tiles_per_batch=tiles_per_batch, d_rwkv=d_rwkv,
                             d_decay=d_decay, d_aaa=d_aaa, d_gate=d_gate, d_sb=d_sb, d_model=d_model)
    return pl.pallas_call(
        kern,
        grid=(m // tm,),
        in_specs=in_specs,
        out_specs=out_specs,
        out_shape=out_shape,
        scratch_shapes=[pltpu.VMEM((1, d_shift), F32)],
        compiler_params=pltpu.CompilerParams(dimension_semantics=("arbitrary",),
                                             vmem_limit_bytes=VMEM_LIMIT),
        name="inproj",
    )(*ins)


def _tri_inverse(m, row, col):
    n = m.shape[0]
    same = lambda s: (row >> s) == (col >> s)
    t = (row == col).astype(F32) + jnp.where(same(1), m, 0.0)
    s = 1
    while (1 << s) < n:
        off = jnp.where(same(s + 1) & jnp.logical_not(same(s)), m, 0.0)
        t = t + _dot3(_dot3(t, off), t)
        s += 1
    return t


def _rwkv_kernel(r_ref, k_ref, v_ref, lw_ref, kk_ref, b_ref, g_ref, rk_ref, gng_ref, gnb_ref, hsum_ref,
                 o_ref, state_ref, *, heads):
    c = pl.program_id(1)
    n = HEAD_DIM
    ch = r_ref.shape[1]

    @pl.when(c == 0)
    def _():
        state_ref[...] = jnp.zeros_like(state_ref)

    row = lax.broadcasted_iota(jnp.int32, (ch, ch), 0)
    col = lax.broadcasted_iota(jnp.int32, (ch, ch), 1)
    strict = row > col
    incl = row >= col
    ltri = jnp.where(incl, 1.0, 0.0).astype(BF16)

    r = r_ref[0]
    k = k_ref[0]
    v = v_ref[0]
    lw = lw_ref[0]
    kk = kk_ref[0]
    b = b_ref[0]

    cum = _dot_exact_lhs(ltri, lw)
    tot = cum[ch - 1:ch, :]
    a_t = -kk * jnp.exp(cum - lw)
    r_t = r * jnp.exp(cum)
    e_neg = jnp.exp(-cum)
    b_t = b * e_neg
    k_t = k * e_neg
    e_bar = jnp.exp(tot - cum)
    b_bar = b * e_bar
    k_bar = k * e_bar
    w_tot = jnp.exp(tot)

    ys = []
    for h in range(heads):
        sl = slice(h * n, (h + 1) * n)
        s0 = state_ref[h]
        lhs = jnp.concatenate([a_t[:, sl], r_t[:, sl]], axis=0)
        rhs = jnp.concatenate([b_t[:, sl], k_t[:, sl]], axis=0)
        gram = _dot3(lhs, rhs, _NT)
        m_ab = jnp.where(strict, gram[:ch, :ch], 0.0)
        m_ak = jnp.where(strict, gram[:ch, ch:], 0.0)
        p_rb = jnp.where(incl, gram[ch:, :ch], 0.0)
        p_rk = jnp.where(incl, gram[ch:, ch:], 0.0)
        vh = v[:, sl]
        t_inv = _tri_inverse(m_ab, row, col)
        a_hat = _dot3(t_inv, a_t[:, sl])
        u0 = _dot3(t_inv, _dot3(m_ak, vh))
        r_hat = r_t[:, sl] + _dot3(p_rb, a_hat)
        y0 = _dot3(p_rk, vh) + _dot3(p_rb, u0)
        g_mat = _dot3(a_hat, b_bar[:, sl], _TN)
        h_mat = _dot3(u0, b_bar[:, sl], _TN) + _dot3(vh, k_bar[:, sl], _TN)
        ys.append(_dot3(r_hat, s0, _NT) + y0)
        state_ref[h] = s0 * w_tot[:, sl] + _dot3(s0, g_mat) + h_mat

    y = jnp.concatenate(ys, axis=-1)
    hsum = hsum_ref[...]
    inv_n = 1.0 / n
    mu = _dot_exact_rhs(y, hsum) * inv_n
    yc = y - mu
    var = _dot_exact_rhs(yc * yc, hsum) * inv_n
    yn = yc * lax.rsqrt(var + GN_EPS) * gng_ref[...] + gnb_ref[...]
    bonus = _dot_exact_rhs(r * k * rk_ref[...], hsum)
    o_ref[0] = ((yn + bonus * v) * g_ref[0]).astype(o_ref.dtype)


def _rwkv(r, k, v, lw, kk, b, g, rw_rk, gn_g, gn_b, hsum, *, chunk):
    bsz, seq, d = r.shape
    heads = d // HEAD_DIM
    blk = pl.BlockSpec((1, chunk, d), lambda bi, ci: (bi, ci, 0))
    const = lambda bi, ci: (0, 0)
    vec = lambda a: a.reshape(1, d)
    return pl.pallas_call(
        functools.partial(_rwkv_kernel, heads=heads),
        grid=(bsz, seq // chunk),
        in_specs=[blk] * 7 + [pl.BlockSpec((1, d), const)] * 3 + [pl.BlockSpec((d, d), const)],
        out_specs=blk,
        out_shape=jax.ShapeDtypeStruct((bsz, seq, d), BF16),
        scratch_shapes=[pltpu.VMEM((heads, HEAD_DIM, HEAD_DIM), F32)],
        compiler_params=pltpu.CompilerParams(dimension_semantics=("arbitrary", "arbitrary")),
        name="rwkv",
    )(r, k, v, lw, kk, b, g, vec(rw_rk), vec(gn_g), vec(gn_b), hsum)


SB_LOG_UNDERFLOW = 105.0


def _sb_kernel(q_ref, k_ref, v_ref, o_ref, acc_ref, run_ref, *, blk):
    qi = pl.program_id(1)
    pairs = q_ref.shape[2] // (2 * HEAD_DIM)
    row = lax.broadcasted_iota(jnp.int32, (blk, blk), 0)
    col = lax.broadcasted_iota(jnp.int32, (blk, blk), 1)
    causal = col < row
    jj = lax.broadcasted_iota(jnp.int32, (2 * blk, 2 * blk), 0)
    ss = lax.broadcasted_iota(jnp.int32, (2 * blk, 2 * blk), 1)
    jj = jnp.where(jj >= blk, jj - blk, jj)
    suffix = jnp.where((jj >= ss) | (ss >= blk), 1.0, 0.0).astype(BF16)
    first_head = lax.broadcasted_iota(jnp.int32, (1, 2 * HEAD_DIM), 1) < HEAD_DIM
    zero = jnp.zeros((), BF16)

    def tile(kb, diagonal):
        start = pl.multiple_of(kb * blk, blk)
        low = None
        for p in range(pairs):
            lanes = slice(p * 2 * HEAD_DIM, (p + 1) * 2 * HEAD_DIM)
            qp = q_ref[0, :, lanes]
            kp = k_ref[0, pl.ds(start, blk), lanes]
            vp = v_ref[0, pl.ds(start, blk), lanes]
            atts = []
            for hh in range(2):
                sel = first_head if hh == 0 else jnp.logical_not(first_head)
                z = _dot(jnp.where(sel, qp, zero), kp, _NT)
                sp = _softplus(z)
                if diagonal:
                    sp = jnp.where(causal, sp, 0.0)
                hi, lo = _split2(sp)
                cs = _dot(jnp.concatenate([hi, lo], axis=1), suffix)
                logw = z - cs[:, :blk]
                if diagonal:
                    att = jnp.where(causal, jnp.exp(logw), 0.0)
                    run = cs[:, blk:]
                else:
                    prev = run_ref[2 * p + hh]
                    att = jnp.exp(logw - prev)
                    run = prev + cs[:, blk:]
                run_ref[2 * p + hh] = run
                low = run if low is None else jnp.minimum(low, run)
                atts.append(att.astype(BF16))
            vv = jnp.concatenate([jnp.where(first_head, vp, zero),
                                  jnp.where(first_head, zero, vp)], axis=0)
            out = _dot(jnp.concatenate(atts, axis=1), vv)
            if diagonal:
                acc_ref[:, lanes] = out
            else:
                acc_ref[:, lanes] += out
        return (jnp.min(low) > SB_LOG_UNDERFLOW).astype(jnp.int32)

    done0 = tile(qi, True)

    def cond(c):
        i, done = c
        return jnp.logical_and(i <= qi, done == 0)

    def body(c):
        i, _ = c
        return i + 1, tile(qi - i, False)

    lax.while_loop(cond, body, (jnp.int32(1), done0))
    o_ref[0] = acc_ref[...].astype(o_ref.dtype)


def _sb_attention(q, k, v, *, blk):
    bsz, seq, d = q.shape
    heads = d // HEAD_DIM
    qspec = pl.BlockSpec((1, blk, d), lambda b, i: (b, i, 0))
    kvspec = pl.BlockSpec((1, seq, d), lambda b, i: (b, 0, 0), pipeline_mode=pl.Buffered(1))
    return pl.pallas_call(
        functools.partial(_sb_kernel, blk=blk),
        grid=(bsz, seq // blk),
        in_specs=[qspec, kvspec, kvspec],
        out_specs=qspec,
        out_shape=jax.ShapeDtypeStruct((bsz, seq, d), BF16),
        scratch_shapes=[pltpu.VMEM((blk, d), F32), pltpu.VMEM((heads, blk, blk), F32)],
        compiler_params=pltpu.CompilerParams(dimension_semantics=("arbitrary", "arbitrary"),
                                             vmem_limit_bytes=VMEM_LIMIT),
        name="sbattn",
    )(q, k, v)


def _merge_kernel(x_ref, ada_ref, ya_ref, ob_ref, ga_ref, gb_ref, wa_ref, wb_ref, wo_ref,
                  lng_ref, lnb_ref, o_ref, *, alpha):
    g1 = ada_ref[0, 2:3, :]
    y_a = _dot(ya_ref[...], wa_ref[...])
    y_b = _dot(ob_ref[...], wb_ref[...])
    merged = ga_ref[...].astype(F32) * y_a + gb_ref[...].astype(F32) * y_b
    mix = _dot(merged.astype(BF16), wo_ref[...])
    o_ref[...] = _norm_rows(alpha * x_ref[...] + g1 * mix, LN_EPS) * lng_ref[...] + lnb_ref[...]


def _merge(x2, ada3, ya, ob, ga, gb, wa_bf, wb_bf, wo_bf, ln_g, ln_b, *, seq, tm, alpha):
    m, d = x2.shape
    tiles_per_batch = seq // tm
    row = lambda i: (i, 0)
    const = lambda i: (0, 0)
    rows = lambda a: pl.BlockSpec((tm, a.shape[1]), row)
    full = lambda a: pl.BlockSpec(a.shape, const)
    vec = lambda a: a.reshape(1, -1)
    ins = [x2, ada3, ya, ob, ga, gb, wa_bf, wb_bf, wo_bf, vec(ln_g), vec(ln_b)]
    in_specs = [rows(x2), pl.BlockSpec((1,) + ada3.shape[1:], lambda i: (i // tiles_per_batch, 0, 0)),
                rows(ya), rows(ob), rows(ga), rows(gb)] + [full(a) for a in ins[6:]]
    return pl.pallas_call(
        functools.partial(_merge_kernel, alpha=alpha),
        grid=(m // tm,),
        in_specs=in_specs,
        out_specs=pl.BlockSpec((tm, d), row),
        out_shape=jax.ShapeDtypeStruct((m, d), F32),
        compiler_params=pltpu.CompilerParams(dimension_semantics=("arbitrary",),
                                             vmem_limit_bytes=VMEM_LIMIT),
        name="merge",
    )(*ins)


def _ffn_kernel(x_ref, ada_ref, w1_ref, b1_ref, w2_ref, b2_ref, lng_ref, lnb_ref, o_ref, *, alpha, fchunk):
    sh = ada_ref[0, 3:4, :]
    sc = ada_ref[0, 4:5, :]
    g2 = ada_ref[0, 5:6, :]
    x = x_ref[...]
    hb = (_norm_rows(x, ADALN_EPS) * (1.0 + sc) + sh).astype(BF16)
    d_ff = w1_ref.shape[1]
    ff = jnp.zeros(x.shape, F32)
    for j in range(d_ff // fchunk):
        sl = slice(j * fchunk, (j + 1) * fchunk)
        t = jnp.maximum(_dot(hb, w1_ref[:, sl]) + b1_ref[:, sl], 0.0)
        ff = ff + _dot((t * t).astype(BF16), w2_ref[sl, :])
    ff = ff + b2_ref[...]
    o_ref[...] = _norm_rows(alpha * x + g2 * ff, LN_EPS) * lng_ref[...] + lnb_ref[...]


def _ffn(x1, ada3, w1_bf, b1, w2_bf, b2, ln_g, ln_b, *, seq, tm, alpha, fchunk):
    m, d = x1.shape
    tiles_per_batch = seq // tm
    row = lambda i: (i, 0)
    const = lambda i: (0, 0)
    full = lambda a: pl.BlockSpec(a.shape, const)
    vec = lambda a: a.reshape(1, -1)
    ins = [x1, ada3, w1_bf, vec(b1), w2_bf, vec(b2), vec(ln_g), vec(ln_b)]
    in_specs = [pl.BlockSpec((tm, d), row),
                pl.BlockSpec((1,) + ada3.shape[1:], lambda i: (i // tiles_per_batch, 0, 0))]
    in_specs += [full(a) for a in ins[2:]]
    return pl.pallas_call(
        functools.partial(_ffn_kernel, alpha=alpha, fchunk=fchunk),
        grid=(m // tm,),
        in_specs=in_specs,
        out_specs=pl.BlockSpec((tm, d), row),
        out_shape=jax.ShapeDtypeStruct((m, d), F32),
        compiler_params=pltpu.CompilerParams(dimension_semantics=("arbitrary",),
                                             vmem_limit_bytes=VMEM_LIMIT),
        name="ffn",
    )(*ins)


def _branches(proj, rw_rk, rw_gn_g, rw_gn_b, hsum, bsz, seq):
    r, k, v, lw, kk, b, g, q, ks, vs = proj
    seq3 = lambda a: a.reshape(bsz, seq, -1)
    ya = _rwkv(seq3(r), seq3(k), seq3(v), seq3(lw), seq3(kk), seq3(b), seq3(g),
               rw_rk, rw_gn_g, rw_gn_b, hsum, chunk=min(RWKV_CHUNK, seq))
    o = _sb_attention(seq3(q), seq3(ks), seq3(vs), blk=min(SB_BLOCK, seq))
    return ya.reshape(bsz * seq, -1), o.reshape(bsz * seq, -1)


def _layer(x, c, w_ada, b_ada, w_in, mu_shift, rw_w0, rw_w2, rw_a0, rw_a2, rw_g2, rw_kk, rw_ka,
           rw_rk, rw_gn_g, rw_gn_b, w_branch_a, w_branch_b, w_out, ln1_g, ln1_b,
           w_ff1, b_ff1, w_ff2, b_ff2, ln2_g, ln2_b, *, alpha):
    bsz, seq, d = x.shape
    d_rwkv = rw_w0.shape[-1]
    heads = d_rwkv // HEAD_DIM
    tm = min(256, seq)

    ada3 = _ada(c, w_ada, b_ada).reshape(bsz, 6, d)
    lane = jnp.arange(d_rwkv) // HEAD_DIM
    hsum = (lane[:, None] == lane[None, :]).astype(BF16)

    x2 = x.reshape(bsz * seq, d)
    (r, k, v, lw, kk, b, g, q, ks, vs, ga, gb) = _inproj(
        x2, ada3, w_in.astype(BF16), mu_shift, rw_w0, rw_w2, rw_a0, rw_a2, rw_g2, rw_kk, rw_ka, hsum,
        seq=seq, tm=tm)

    ya, ob = _branches((r, k, v, lw, kk, b, g, q, ks, vs), rw_rk, rw_gn_g, rw_gn_b, hsum, bsz, seq)

    x1 = _merge(x2, ada3, ya, ob, ga, gb,
                w_branch_a.astype(BF16), w_branch_b.astype(BF16), w_out.astype(BF16),
                ln1_g, ln1_b, seq=seq, tm=tm, alpha=alpha)
    out = _ffn(x1, ada3, w_ff1.astype(BF16), b_ff1, w_ff2.astype(BF16), b_ff2, ln2_g, ln2_b,
               seq=seq, tm=tm, alpha=alpha, fchunk=min(1024, w_ff1.shape[-1]))
    return out.reshape(bsz, seq, d)


def kernel(x, c, w_ada, b_ada, w_in, mu_shift, rw_w0, rw_w2, rw_a0, rw_a2, rw_g2, rw_kk, rw_ka, rw_rk,
           rw_gn_g, rw_gn_b, w_branch_a, w_branch_b, w_out, ln1_g, ln1_b, w_ff1, b_ff1, w_ff2, b_ff2,
           ln2_g, ln2_b):
    in_dtype = x.dtype
    layer_params = (w_ada, b_ada, w_in, mu_shift, rw_w0, rw_w2, rw_a0, rw_a2, rw_g2, rw_kk, rw_ka,
                    rw_rk, rw_gn_g, rw_gn_b, w_branch_a, w_branch_b, w_out, ln1_g, ln1_b,
                    w_ff1, b_ff1, w_ff2, b_ff2, ln2_g, ln2_b)
    depth = w_ada.shape[0]
    alpha = (2.0 * depth) ** 0.25
    for l in range(depth):
        x = _layer(x, c, *[p[l] for p in layer_params], alpha=alpha)
    return x.astype(in_dtype)
```

```python
import functools

import jax
import jax.numpy as jnp
from jax import lax
from jax.experimental import pallas as pl
from jax.experimental.pallas import tpu as pltpu

F32 = jnp.float32
BF16 = jnp.bfloat16

HEAD_DIM = 64
LN_EPS = 1e-5
ADALN_EPS = 1e-6
GN_EPS = 64e-5
RWKV_CHUNK = 64
SB_BLOCK = 128
VMEM_LIMIT = 56 * 1024 * 1024


def _split2(a):
    hi = a.astype(BF16)
    lo = (a - hi.astype(F32)).astype(BF16)
    return hi, lo


def _split3(a):
    hi = a.astype(BF16)
    r1 = a - hi.astype(F32)
    mid = r1.astype(BF16)
    lo = (r1 - mid.astype(F32)).astype(BF16)
    return hi, mid, lo


_NN = (((1,), (0,)), ((), ()))
_NT = (((1,), (1,)), ((), ()))
_TN = (((0,), (0,)), ((), ()))


def _dot(a, b, dims=_NN):
    return lax.dot_general(a, b, dims, preferred_element_type=F32)


def _dot3(a, b, dims=_NN):
    ah, al = _split2(a)
    bh, bl = _split2(b)
    return _dot(ah, bh, dims) + (_dot(al, bh, dims) + _dot(ah, bl, dims))


def _dot_exact_rhs(a, b_bf, dims=_NN):
    a0, a1, a2 = _split3(a)
    return _dot(a0, b_bf, dims) + (_dot(a1, b_bf, dims) + _dot(a2, b_bf, dims))


def _dot_exact_lhs(a_bf, b, dims=_NN):
    b0, b1, b2 = _split3(b)
    return _dot(a_bf, b0, dims) + (_dot(a_bf, b1, dims) + _dot(a_bf, b2, dims))


def _softplus(y):
    return jnp.maximum(y, 0.0) + jnp.log(1.0 + jnp.exp(-jnp.abs(y)))


def _sigmoid(y):
    return 1.0 / (1.0 + jnp.exp(-y))


def _norm_rows(x, eps):
    mu = jnp.mean(x, axis=-1, keepdims=True)
    xc = x - mu
    var = jnp.mean(xc * xc, axis=-1, keepdims=True)
    return xc * lax.rsqrt(var + eps)


def _ada_kernel(c_ref, w_ref, b_ref, o_ref):
    c = c_ref[...]
    s = c * _sigmoid(c)
    o_ref[...] = jnp.dot(s, w_ref[...], preferred_element_type=F32,
                         precision=lax.Precision.HIGHEST) + b_ref[...]


def _ada(c, w_ada, b_ada):
    bsz, d = c.shape
    n = w_ada.shape[1]
    return pl.pallas_call(
        _ada_kernel,
        grid=(n // d,),
        in_specs=[pl.BlockSpec((bsz, d), lambda j: (0, 0)),
                  pl.BlockSpec((d, d), lambda j: (0, j)),
                  pl.BlockSpec((1, d), lambda j: (0, j))],
        out_specs=pl.BlockSpec((bsz, d), lambda j: (0, j)),
        out_shape=jax.ShapeDtypeStruct((bsz, n), F32),
        name="ada",
    )(c, w_ada, b_ada.reshape(1, n))


def _inproj_kernel(x_ref, ada_ref, win_ref, mu_ref, w0_ref, w2_ref, a0_ref, a2_ref, g2_ref,
                   kkw_ref, kaw_ref, hsum_ref,
                   r_ref, k_ref, v_ref, lw_ref, kk_ref, b_ref, g_ref,
                   q_ref, ks_ref, vs_ref, ga_ref, gb_ref,
                   carry_ref, *, tiles_per_batch, d_rwkv, d_decay, d_aaa, d_gate, d_sb, d_model):
    i = pl.program_id(0)
    tm = x_ref.shape[0]
    d_shift = 3 * d_rwkv + d_decay + d_aaa + d_gate

    sh = ada_ref[0, 0:1, :]
    sc = ada_ref[0, 1:2, :]
    h = _norm_rows(x_ref[...], ADALN_EPS) * (1.0 + sc) + sh
    hb = h.astype(BF16)

    zrw = _dot(hb, win_ref[:, :d_shift])

    @pl.when(i % tiles_per_batch == 0)
    def _():
        carry_ref[...] = jnp.zeros_like(carry_ref)

    row = lax.broadcasted_iota(jnp.int32, zrw.shape, 0)
    prev = jnp.where(row == 0, carry_ref[...], pltpu.roll(zrw, 1, 0))
    carry_ref[...] = zrw[tm - 1:tm, :]
    zs = zrw + mu_ref[...] * (prev - zrw)

    o = 0
    r = zs[:, o:o + d_rwkv]; o += d_rwkv
    k = zs[:, o:o + d_rwkv]; o += d_rwkv
    v = zs[:, o:o + d_rwkv]; o += d_rwkv
    zw = zs[:, o:o + d_decay]; o += d_decay
    za = zs[:, o:o + d_aaa]; o += d_aaa
    zg = zs[:, o:o + d_gate]

    ww = w0_ref[...] + _dot3(jnp.tanh(zw), w2_ref[...])
    w_log = -_softplus(-ww) - 0.5
    lw_ref[...] = -jnp.exp(w_log)
    arate = _sigmoid(a0_ref[...] + _dot3(za, a2_ref[...]))
    g_ref[...] = _dot3(_sigmoid(zg), g2_ref[...])
    kkraw = k * kkw_ref[...]
    ssq = _dot_exact_rhs(kkraw * kkraw, hsum_ref[...])
    kk = kkraw / jnp.maximum(jnp.sqrt(ssq), 1e-12)
    r_ref[...] = r
    k_ref[...] = k * (1.0 + (arate - 1.0) * kaw_ref[...])
    v_ref[...] = v
    kk_ref[...] = kk
    b_ref[...] = kk * arate

    o = d_shift
    q_ref[...] = (_dot(hb, win_ref[:, o:o + d_sb]) * (HEAD_DIM ** -0.5)).astype(BF16); o += d_sb
    ks_ref[...] = _dot(hb, win_ref[:, o:o + d_sb]).astype(BF16); o += d_sb
    vs_ref[...] = _dot(hb, win_ref[:, o:o + d_sb]).astype(BF16); o += d_sb
    ga_ref[...] = _sigmoid(_dot(hb, win_ref[:, o:o + d_model])).astype(BF16); o += d_model
    gb_ref[...] = _sigmoid(_dot(hb, win_ref[:, o:o + d_model])).astype(BF16)


def _inproj(x2, ada3, win_bf, mu_shift, rw_w0, rw_w2, rw_a0, rw_a2, rw_g2, rw_kk, rw_ka, hsum,
            *, seq, tm):
    m, d_model = x2.shape
    d_rwkv = rw_w0.shape[-1]
    d_decay, d_aaa, d_gate = rw_w2.shape[0], rw_a2.shape[0], rw_g2.shape[0]
    d_shift = 3 * d_rwkv + d_decay + d_aaa + d_gate
    d_sb = (win_bf.shape[1] - d_shift - 2 * d_model) // 3
    tiles_per_batch = seq // tm
    row = lambda i: (i, 0)
    const = lambda i: (0, 0)
    full = lambda a: pl.BlockSpec(a.shape, const)
    vec = lambda a: a.reshape(1, -1)
    ins = [x2, ada3, win_bf, vec(mu_shift), vec(rw_w0), rw_w2, vec(rw_a0), rw_a2, rw_g2,
           vec(rw_kk), vec(rw_ka), hsum]
    in_specs = [pl.BlockSpec((tm, d_model), row),
                pl.BlockSpec((1,) + ada3.shape[1:], lambda i: (i // tiles_per_batch, 0, 0))]
    in_specs += [full(a) for a in ins[2:]]
    out_shape = ([jax.ShapeDtypeStruct((m, d_rwkv), F32)] * 7
                 + [jax.ShapeDtypeStruct((m, d_sb), BF16)] * 3
                 + [jax.ShapeDtypeStruct((m, d_model), BF16)] * 2)
    out_specs = ([pl.BlockSpec((tm, d_rwkv), row)] * 7 + [pl.BlockSpec((tm, d_sb), row)] * 3
                 + [pl.BlockSpec((tm, d_model), row)] * 2)
    kern = functools.partial(_inproj_kernel, tiles_per_batch=tiles_per_batch, d_rwkv=d_rwkv,
                             d_decay=d_decay, d_aaa=d_aaa, d_gate=d_gate, d_sb=d_sb, d_model=d_model)
    return pl.pallas_call(
        kern,
        grid=(m // tm,),
        in_specs=in_specs,
        out_specs=out_specs,
        out_shape=out_shape,
        scratch_shapes=[pltpu.VMEM((1, d_shift), F32)],
        compiler_params=pltpu.CompilerParams(dimension_semantics=("arbitrary",),
                                             vmem_limit_bytes=VMEM_LIMIT),
        name="inproj",
    )(*ins)


def _tri_inverse(m, row, col):
    n = m.shape[0]
    same = lambda s: (row >> s) == (col >> s)
    t = (row == col).astype(F32) + jnp.where(same(1), m, 0.0)
    s = 1
    while (1 << s) < n:
        off = jnp.where(same(s + 1) & jnp.logical_not(same(s)), m, 0.0)
        t = t + _dot3(_dot3(t, off), t)
        s += 1
    return t


def _rwkv_kernel(r_ref, k_ref, v_ref, lw_ref, kk_ref, b_ref, g_ref, rk_ref, gng_ref, gnb_ref, hsum_ref,
                 o_ref, state_ref, *, heads):
    c = pl.program_id(1)
    n = HEAD_DIM
    ch = r_ref.shape[1]

    @pl.when(c == 0)
    def _():
        state_ref[...] = jnp.zeros_like(state_ref)

    row = lax.broadcasted_iota(jnp.int32, (ch, ch), 0)
    col = lax.broadcasted_iota(jnp.int32, (ch, ch), 1)
    strict = row > col
    incl = row >= col
    ltri = jnp.where(incl, 1.0, 0.0).astype(BF16)

    r = r_ref[0]
    k = k_ref[0]
    v = v_ref[0]
    lw = lw_ref[0]
    kk = kk_ref[0]
    b = b_ref[0]

    cum = _dot_exact_lhs(ltri, lw)
    tot = cum[ch - 1:ch, :]
    a_t = -kk * jnp.exp(cum - lw)
    r_t = r * jnp.exp(cum)
    e_neg = jnp.exp(-cum)
    b_t = b * e_neg
    k_t = k * e_neg
    e_bar = jnp.exp(tot - cum)
    b_bar = b * e_bar
    k_bar = k * e_bar
    w_tot = jnp.exp(tot)

    hs = range(heads)
    sls = [slice(h * n, (h + 1) * n) for h in hs]
    s0s = [state_ref[h] for h in hs]
    grams = [_dot3(jnp.concatenate([a_t[:, sl], r_t[:, sl]], axis=0),
                   jnp.concatenate([b_t[:, sl], k_t[:, sl]], axis=0), _NT) for sl in sls]
    m_ab = [jnp.where(strict, g_[:ch, :ch], 0.0) for g_ in grams]
    m_ak = [jnp.where(strict, g_[:ch, ch:], 0.0) for g_ in grams]
    p_rb = [jnp.where(incl, g_[ch:, :ch], 0.0) for g_ in grams]
    p_rk = [jnp.where(incl, g_[ch:, ch:], 0.0) for g_ in grams]
    same = lambda s: (row >> s) == (col >> s)
    eye = (row == col).astype(F32)
    ts = [eye + jnp.where(same(1), m, 0.0) for m in m_ab]
    s = 1
    while (1 << s) < ch:
        msk = same(s + 1) & jnp.logical_not(same(s))
        offs = [jnp.where(msk, m, 0.0) for m in m_ab]
        mids = [_dot3(t, o) for t, o in zip(ts, offs)]
        ts = [t + _dot3(md, t) for t, md in zip(ts, mids)]
        s += 1
    vhs = [v[:, sl] for sl in sls]
    mv = [_dot3(m, vh) for m, vh in zip(m_ak, vhs)]
    a_hat = [_dot3(t, a_t[:, sl]) for t, sl in zip(ts, sls)]
    u0 = [_dot3(t, x) for t, x in zip(ts, mv)]
    r_hat = [r_t[:, sl] + _dot3(p, a) for sl, p, a in zip(sls, p_rb, a_hat)]
    y0 = [_dot3(pk, vh) + _dot3(pb, u) for pk, vh, pb, u in zip(p_rk, vhs, p_rb, u0)]
    g_mat = [_dot3(a, b_bar[:, sl], _TN) for a, sl in zip(a_hat, sls)]
    h_mat = [_dot3(u, b_bar[:, sl], _TN) + _dot3(vh, k_bar[:, sl], _TN) for u, vh, sl in zip(u0, vhs, sls)]
    ys = [_dot3(rh, s0, _NT) + y_ for rh, s0, y_ in zip(r_hat, s0s, y0)]
    for h in hs:
        state_ref[h] = s0s[h] * w_tot[:, sls[h]] + _dot3(s0s[h], g_mat[h]) + h_mat[h]

    y = jnp.concatenate(ys, axis=-1)
    hsum = hsum_ref[...]
    inv_n = 1.0 / n
    mu = _dot_exact_rhs(y, hsum) * inv_n
    yc = y - mu
    var = _dot_exact_rhs(yc * yc, hsum) * inv_n
    yn = yc * lax.rsqrt(var + GN_EPS) * gng_ref[...] + gnb_ref[...]
    bonus = _dot_exact_rhs(r * k * rk_ref[...], hsum)
    o_ref[0] = ((yn + bonus * v) * g_ref[0]).astype(o_ref.dtype)


def _rwkv(r, k, v, lw, kk, b, g, rw_rk, gn_g, gn_b, hsum, *, chunk):
    bsz, seq, d = r.shape
    heads = d // HEAD_DIM
    blk = pl.BlockSpec((1, chunk, d), lambda bi, ci: (bi, ci, 0))
    const = lambda bi, ci: (0, 0)
    vec = lambda a: a.reshape(1, d)
    return pl.pallas_call(
        functools.partial(_rwkv_kernel, heads=heads),
        grid=(bsz, seq // chunk),
        in_specs=[blk] * 7 + [pl.BlockSpec((1, d), const)] * 3 + [pl.BlockSpec((d, d), const)],
        out_specs=blk,
        out_shape=jax.ShapeDtypeStruct((bsz, seq, d), BF16),
        scratch_shapes=[pltpu.VMEM((heads, HEAD_DIM, HEAD_DIM), F32)],
        compiler_params=pltpu.CompilerParams(dimension_semantics=("arbitrary", "arbitrary")),
        name="rwkv",
    )(r, k, v, lw, kk, b, g, vec(rw_rk), vec(gn_g), vec(gn_b), hsum)


SB_LOG_UNDERFLOW = 105.0


def _sb_kernel(q_ref, k_ref, v_ref, o_ref, acc_ref, run_ref, *, blk):
    qi = pl.program_id(1)
    pairs = q_ref.shape[2] // (2 * HEAD_DIM)
    row = lax.broadcasted_iota(jnp.int32, (blk, blk), 0)
    col = lax.broadcasted_iota(jnp.int32, (blk, blk), 1)
    causal = col < row
    jj = lax.broadcasted_iota(jnp.int32, (2 * blk, 2 * blk), 0)
    ss = lax.broadcasted_iota(jnp.int32, (2 * blk, 2 * blk), 1)
    jj = jnp.where(jj >= blk, jj - blk, jj)
    suffix = jnp.where((jj >= ss) | (ss >= blk), 1.0, 0.0).astype(BF16)
    first_head = lax.broadcasted_iota(jnp.int32, (1, 2 * HEAD_DIM), 1) < HEAD_DIM
    zero = jnp.zeros((), BF16)

    def tile(kb, diagonal):
        start = pl.multiple_of(kb * blk, blk)
        low = None
        for p in range(pairs):
            lanes = slice(p * 2 * HEAD_DIM, (p + 1) * 2 * HEAD_DIM)
            qp = q_ref[0, :, lanes]
            kp = k_ref[0, pl.ds(start, blk), lanes]
            vp = v_ref[0, pl.ds(start, blk), lanes]
            atts = []
            for hh in range(2):
                sel = first_head if hh == 0 else jnp.logical_not(first_head)
                z = _dot(jnp.where(sel, qp, zero), kp, _NT)
                sp = _softplus(z)
                if diagonal:
                    sp = jnp.where(causal, sp, 0.0)
                hi, lo = _split2(sp)
                cs = _dot(jnp.concatenate([hi, lo], axis=1), suffix)
                logw = z - cs[:, :blk]
                if diagonal:
                    att = jnp.where(causal, jnp.exp(logw), 0.0)
                    run = cs[:, blk:]
                else:
                    prev = run_ref[2 * p + hh]
                    att = jnp.exp(logw - prev)
                    run = prev + cs[:, blk:]
                run_ref[2 * p + hh] = run
                low = run if low is None else jnp.minimum(low, run)
                atts.append(att.astype(BF16))
            vv = jnp.concatenate([jnp.where(first_head, vp, zero),
                                  jnp.where(first_head, zero, vp)], axis=0)
            out = _dot(jnp.concatenate(atts, axis=1), vv)
            if diagonal:
                acc_ref[:, lanes] = out
            else:
                acc_ref[:, lanes] += out
        return (jnp.min(low) > SB_LOG_UNDERFLOW).astype(jnp.int32)

    done0 = tile(qi, True)

    def cond(c):
        i, done = c
        return jnp.logical_and(i <= qi, done == 0)

    def body(c):
        i, _ = c
        return i + 1, tile(qi - i, False)

    lax.while_loop(cond, body, (jnp.int32(1), done0))
    o_ref[0] = acc_ref[...].astype(o_ref.dtype)


def _sb_attention(q, k, v, *, blk):
    bsz, seq, d = q.shape
    heads = d // HEAD_DIM
    qspec = pl.BlockSpec((1, blk, d), lambda b, i: (b, i, 0))
    kvspec = pl.BlockSpec((1, seq, d), lambda b, i: (b, 0, 0), pipeline_mode=pl.Buffered(1))
    return pl.pallas_call(
        functools.partial(_sb_kernel, blk=blk),
        grid=(bsz, seq // blk),
        in_specs=[qspec, kvspec, kvspec],
        out_specs=qspec,
        out_shape=jax.ShapeDtypeStruct((bsz, seq, d), BF16),
        scratch_shapes=[pltpu.VMEM((blk, d), F32), pltpu.VMEM((heads, blk, blk), F32)],
        compiler_params=pltpu.CompilerParams(dimension_semantics=("arbitrary", "arbitrary"),
                                             vmem_limit_bytes=VMEM_LIMIT),
        name="sbattn",
    )(q, k, v)


def _merge_kernel(x_ref, ada_ref, ya_ref, ob_ref, ga_ref, gb_ref, wa_ref, wb_ref, wo_ref,
                  lng_ref, lnb_ref, o_ref, *, alpha):
    g1 = ada_ref[0, 2:3, :]
    y_a = _dot(ya_ref[...], wa_ref[...])
    y_b = _dot(ob_ref[...], wb_ref[...])
    merged = ga_ref[...].astype(F32) * y_a + gb_ref[...].astype(F32) * y_b
    mix = _dot(merged.astype(BF16), wo_ref[...])
    o_ref[...] = _norm_rows(alpha * x_ref[...] + g1 * mix, LN_EPS) * lng_ref[...] + lnb_ref[...]


def _merge(x2, ada3, ya, ob, ga, gb, wa_bf, wb_bf, wo_bf, ln_g, ln_b, *, seq, tm, alpha):
    m, d = x2.shape
    tiles_per_batch = seq // tm
    row = lambda i: (i, 0)
    const = lambda i: (0, 0)
    rows = lambda a: pl.BlockSpec((tm, a.shape[1]), row)
    full = lambda a: pl.BlockSpec(a.shape, const)
    vec = lambda a: a.reshape(1, -1)
    ins = [x2, ada3, ya, ob, ga, gb, wa_bf, wb_bf, wo_bf, vec(ln_g), vec(ln_b)]
    in_specs = [rows(x2), pl.BlockSpec((1,) + ada3.shape[1:], lambda i: (i // tiles_per_batch, 0, 0)),
                rows(ya), rows(ob), rows(ga), rows(gb)] + [full(a) for a in ins[6:]]
    return pl.pallas_call(
        functools.partial(_merge_kernel, alpha=alpha),
        grid=(m // tm,),
        in_specs=in_specs,
        out_specs=pl.BlockSpec((tm, d), row),
        out_shape=jax.ShapeDtypeStruct((m, d), F32),
        compiler_params=pltpu.CompilerParams(dimension_semantics=("arbitrary",),
                                             vmem_limit_bytes=VMEM_LIMIT),
        name="merge",
    )(*ins)


def _ffn_kernel(x_ref, ada_ref, w1_ref, b1_ref, w2_ref, b2_ref, lng_ref, lnb_ref, o_ref, *, alpha, fchunk):
    sh = ada_ref[0, 3:4, :]
    sc = ada_ref[0, 4:5, :]
    g2 = ada_ref[0, 5:6, :]
    x = x_ref[...]
    hb = (_norm_rows(x, ADALN_EPS) * (1.0 + sc) + sh).astype(BF16)
    d_ff = w1_ref.shape[1]
    ff = jnp.zeros(x.shape, F32)
    for j in range(d_ff // fchunk):
        sl = slice(j * fchunk, (j + 1) * fchunk)
        t = jnp.maximum(_dot(hb, w1_ref[:, sl]) + b1_ref[:, sl], 0.0)
        ff = ff + _dot((t * t).astype(BF16), w2_ref[sl, :])
    ff = ff + b2_ref[...]
    o_ref[...] = _norm_rows(alpha * x + g2 * ff, LN_EPS) * lng_ref[...] + lnb_ref[...]


def _ffn(x1, ada3, w1_bf, b1, w2_bf, b2, ln_g, ln_b, *, seq, tm, alpha, fchunk):
    m, d = x1.shape
    tiles_per_batch = seq // tm
    row = lambda i: (i, 0)
    const = lambda i: (0, 0)
    full = lambda a: pl.BlockSpec(a.shape, const)
    vec = lambda a: a.reshape(1, -1)
    ins = [x1, ada3, w1_bf, vec(b1), w2_bf, vec(b2), vec(ln_g), vec(ln_b)]
    in_specs = [pl.BlockSpec((tm, d), row),
                pl.BlockSpec((1,) + ada3.shape[1:], lambda i: (i // tiles_per_batch, 0, 0))]
    in_specs += [full(a) for a in ins[2:]]
    return pl.pallas_call(
        functools.partial(_ffn_kernel, alpha=alpha, fchunk=fchunk),
        grid=(m // tm,),
        in_specs=in_specs,
        out_specs=pl.BlockSpec((tm, d), row),
        out_shape=jax.ShapeDtypeStruct((m, d), F32),
        compiler_params=pltpu.CompilerParams(dimension_semantics=("arbitrary",),
                                             vmem_limit_bytes=VMEM_LIMIT),
        name="ffn",
    )(*ins)


def _branches(proj, rw_rk, rw_gn_g, rw_gn_b, hsum, bsz, seq):
    r, k, v, lw, kk, b, g, q, ks, vs = proj
    seq3 = lambda a: a.reshape(bsz, seq, -1)
    ya = _rwkv(seq3(r), seq3(k), seq3(v), seq3(lw), seq3(kk), seq3(b), seq3(g),
               rw_rk, rw_gn_g, rw_gn_b, hsum, chunk=min(RWKV_CHUNK, seq))
    o = _sb_attention(seq3(q), seq3(ks), seq3(vs), blk=min(SB_BLOCK, seq))
    return ya.reshape(bsz * seq, -1), o.reshape(bsz * seq, -1)


def _layer(x, c, w_ada, b_ada, w_in, mu_shift, rw_w0, rw_w2, rw_a0, rw_a2, rw_g2, rw_kk, rw_ka,
           rw_rk, rw_gn_g, rw_gn_b, w_branch_a, w_branch_b, w_out, ln1_g, ln1_b,
           w_ff1, b_ff1, w_ff2, b_ff2, ln2_g, ln2_b, *, alpha):
    bsz, seq, d = x.shape
    d_rwkv = rw_w0.shape[-1]
    heads = d_rwkv // HEAD_DIM
    tm = min(256, seq)

    ada3 = _ada(c, w_ada, b_ada).reshape(bsz, 6, d)
    lane = jnp.arange(d_rwkv) // HEAD_DIM
    hsum = (lane[:, None] == lane[None, :]).astype(BF16)

    x2 = x.reshape(bsz * seq, d)
    (r, k, v, lw, kk, b, g, q, ks, vs, ga, gb) = _inproj(
        x2, ada3, w_in.astype(BF16), mu_shift, rw_w0, rw_w2, rw_a0, rw_a2, rw_g2, rw_kk, rw_ka, hsum,
        seq=seq, tm=tm)

    ya, ob = _branches((r, k, v, lw, kk, b, g, q, ks, vs), rw_rk, rw_gn_g, rw_gn_b, hsum, bsz, seq)

    x1 = _merge(x2, ada3, ya, ob, ga, gb,
                w_branch_a.astype(BF16), w_branch_b.astype(BF16), w_out.astype(BF16),
                ln1_g, ln1_b, seq=seq, tm=tm, alpha=alpha)
    out = _ffn(x1, ada3, w_ff1.astype(BF16), b_ff1, w_ff2.astype(BF16), b_ff2, ln2_g, ln2_b,
               seq=seq, tm=tm, alpha=alpha, fchunk=min(1024, w_ff1.shape[-1]))
    return out.reshape(bsz, seq, d)


def kernel(x, c, w_ada, b_ada, w_in, mu_shift, rw_w0, rw_w2, rw_a0, rw_a2, rw_g2, rw_kk, rw_ka, rw_rk,
           rw_gn_g, rw_gn_b, w_branch_a, w_branch_b, w_out, ln1_g, ln1_b, w_ff1, b_ff1, w_ff2, b_ff2,
           ln2_g, ln2_b):
    in_dtype = x.dtype
    layer_params = (w_ada, b_ada, w_in, mu_shift, rw_w0, rw_w2, rw_a0, rw_a2, rw_g2, rw_kk, rw_ka,
                    rw_rk, rw_gn_g, rw_gn_b, w_branch_a, w_branch_b, w_out, ln1_g, ln1_b,
                    w_ff1, b_ff1, w_ff2, b_ff2, ln2_g, ln2_b)
    depth = w_ada.shape[0]
    alpha = (2.0 * depth) ** 0.25
    for l in range(depth):
        x = _layer(x, c, *[p[l] for p in layer_params], alpha=alpha)
    return x.astype(in_dtype)
```

```python
import functools

import jax
import jax.numpy as jnp
from jax import lax
from jax.experimental import pallas as pl
from jax.experimental.pallas import tpu as pltpu

F32 = jnp.float32
BF16 = jnp.bfloat16

HEAD_DIM = 64
LN_EPS = 1e-5
ADALN_EPS = 1e-6
GN_EPS = 64e-5
RWKV_CHUNK = 64
RWKV_CHUNKS_PER_STEP = 4
SB_BLOCK = 128
VMEM_LIMIT = 56 * 1024 * 1024


def _split2(a):
    hi = a.astype(BF16)
    lo = (a - hi.astype(F32)).astype(BF16)
    return hi, lo


_NN = (((1,), (0,)), ((), ()))
_NT = (((1,), (1,)), ((), ()))
_TN = (((0,), (0,)), ((), ()))


def _dot(a, b, dims=_NN):
    return lax.dot_general(a, b, dims, preferred_element_type=F32)


def _dot3(a, b, dims=_NN):
    ah, al = _split2(a)
    bh, bl = _split2(b)
    return _dot(ah, bh, dims) + (_dot(al, bh, dims) + _dot(ah, bl, dims))


def _head_sums(x, ones_blk):
    m, w = x.shape[0], ones_blk.shape[0]
    hi, lo = _split2(x)
    outs = []
    for g in range(x.shape[1] // w):
        cols = slice(g * w, (g + 1) * w)
        both = _dot(jnp.concatenate([hi[:, cols], lo[:, cols]], axis=0), ones_blk)
        outs.append(both[:m] + both[m:])
    return jnp.concatenate(outs, axis=1)


def _dot_ones_lhs(a_bf, b):
    hi, lo = _split2(b)
    return _dot(a_bf, hi) + _dot(a_bf, lo)


def _softplus(y):
    return jnp.maximum(y, 0.0) + jnp.log(1.0 + jnp.exp(-jnp.abs(y)))


def _sigmoid(y):
    return 1.0 / (1.0 + jnp.exp(-y))


def _norm_rows(x, eps):
    mu = jnp.mean(x, axis=-1, keepdims=True)
    xc = x - mu
    var = jnp.mean(xc * xc, axis=-1, keepdims=True)
    return xc * lax.rsqrt(var + eps)


def _ada_kernel(c_ref, w_ref, b_ref, o_ref):
    c = c_ref[...]
    s = c * _sigmoid(c)
    o_ref[...] = jnp.dot(s, w_ref[...], preferred_element_type=F32,
                         precision=lax.Precision.HIGHEST) + b_ref[...]


def _ada(c, w_ada, b_ada):
    bsz, d = c.shape
    n = w_ada.shape[1]
    return pl.pallas_call(
        _ada_kernel,
        grid=(n // d,),
        in_specs=[pl.BlockSpec((bsz, d), lambda j: (0, 0)),
                  pl.BlockSpec((d, d), lambda j: (0, j)),
                  pl.BlockSpec((1, d), lambda j: (0, j))],
        out_specs=pl.BlockSpec((bsz, d), lambda j: (0, j)),
        out_shape=jax.ShapeDtypeStruct((bsz, n), F32),
        name="ada",
    )(c, w_ada, b_ada.reshape(1, n))


def _inproj_kernel(x_ref, ada_ref, win_ref, mu_ref, w0_ref, w2_ref, a0_ref, a2_ref, g2_ref,
                   kkw_ref, kaw_ref, hsum_ref,
                   r_ref, k_ref, v_ref, lw_ref, kk_ref, b_ref, g_ref,
                   q_ref, ks_ref, vs_ref, ga_ref, gb_ref,
                   carry_ref, *, tiles_per_batch, d_rwkv, d_decay, d_aaa, d_gate, d_sb, d_model):
    i = pl.program_id(0)
    tm = x_ref.shape[0]
    d_shift = 3 * d_rwkv + d_decay + d_aaa + d_gate

    sh = ada_ref[0, 0:1, :]
    sc = ada_ref[0, 1:2, :]
    h = _norm_rows(x_ref[...], ADALN_EPS) * (1.0 + sc) + sh
    hb = h.astype(BF16)

    zrw = _dot(hb, win_ref[:, :d_shift])

    @pl.when(i % tiles_per_batch == 0)
    def _():
        carry_ref[...] = jnp.zeros_like(carry_ref)

    row = lax.broadcasted_iota(jnp.int32, zrw.shape, 0)
    prev = jnp.where(row == 0, carry_ref[...], pltpu.roll(zrw, 1, 0))
    carry_ref[...] = zrw[tm - 1:tm, :]
    zs = zrw + mu_ref[...] * (prev - zrw)

    o = 0
    r = zs[:, o:o + d_rwkv]; o += d_rwkv
    k = zs[:, o:o + d_rwkv]; o += d_rwkv
    v = zs[:, o:o + d_rwkv]; o += d_rwkv
    zw = zs[:, o:o + d_decay]; o += d_decay
    za = zs[:, o:o + d_aaa]; o += d_aaa
    zg = zs[:, o:o + d_gate]

    ww = w0_ref[...] + _dot3(jnp.tanh(zw), w2_ref[...])
    w_log = -_softplus(-ww) - 0.5
    lw_ref[...] = -jnp.exp(w_log)
    arate = _sigmoid(a0_ref[...] + _dot3(za, a2_ref[...]))
    g_ref[...] = _dot3(_sigmoid(zg), g2_ref[...])
    kkraw = k * kkw_ref[...]
    ssq = _head_sums(kkraw * kkraw, hsum_ref[...])
    kk = kkraw / jnp.maximum(jnp.sqrt(ssq), 1e-12)
    r_ref[...] = r
    k_ref[...] = k * (1.0 + (arate - 1.0) * kaw_ref[...])
    v_ref[...] = v
    kk_ref[...] = kk
    b_ref[...] = kk * arate

    o = d_shift
    q_ref[...] = (_dot(hb, win_ref[:, o:o + d_sb]) * (HEAD_DIM ** -0.5)).astype(BF16); o += d_sb
    ks_ref[...] = _dot(hb, win_ref[:, o:o + d_sb]).astype(BF16); o += d_sb
    vs_ref[...] = _dot(hb, win_ref[:, o:o + d_sb]).astype(BF16); o += d_sb
    ga_ref[...] = _sigmoid(_dot(hb, win_ref[:, o:o + d_model])).astype(BF16); o += d_model
    gb_ref[...] = _sigmoid(_dot(hb, win_ref[:, o:o + d_model])).astype(BF16)


def _inproj(x2, ada3, win_bf, mu_shift, rw_w0, rw_w2, rw_a0, rw_a2, rw_g2, rw_kk, rw_ka, hsum,
            *, seq, tm):
    m, d_model = x2.shape
    d_rwkv = rw_w0.shape[-1]
    d_decay, d_aaa, d_gate = rw_w2.shape[0], rw_a2.shape[0], rw_g2.shape[0]
    d_shift = 3 * d_rwkv + d_decay + d_aaa + d_gate
    d_sb = (win_bf.shape[1] - d_shift - 2 * d_model) // 3
    tiles_per_batch = seq // tm
    row = lambda i: (i, 0)
    const = lambda i: (0, 0)
    full = lambda a: pl.BlockSpec(a.shape, const)
    vec = lambda a: a.reshape(1, -1)
    ins = [x2, ada3, win_bf, vec(mu_shift), vec(rw_w0), rw_w2, vec(rw_a0), rw_a2, rw_g2,
           vec(rw_kk), vec(rw_ka), hsum]
    in_specs = [pl.BlockSpec((tm, d_model), row),
                pl.BlockSpec((1,) + ada3.shape[1:], lambda i: (i // tiles_per_batch, 0, 0))]
    in_specs += [full(a) for a in ins[2:]]
    out_shape = ([jax.ShapeDtypeStruct((m, d_rwkv), F32)] * 7
                 + [jax.ShapeDtypeStruct((m, d_sb), BF16)] * 3
                 + [jax.ShapeDtypeStruct((m, d_model), BF16)] * 2)
    out_specs = ([pl.BlockSpec((tm, d_rwkv), row)] * 7 + [pl.BlockSpec((tm, d_sb), row)] * 3
                 + [pl.BlockSpec((tm, d_model), row)] * 2)
    kern = functools.partial(_inproj_kernel, tiles_per_batch=tiles_per_batch, d_rwkv=d_rwkv,
                             d_decay=d_decay, d_aaa=d_aaa, d_gate=d_gate, d_sb=d_sb, d_model=d_model)
    return pl.pallas_call(
        kern,
        grid=(m // tm,),
        in_specs=in_specs,
        out_specs=out_specs,
        out_shape=out_shape,
        scratch_shapes=[pltpu.VMEM((1, d_shift), F32)],
        compiler_params=pltpu.CompilerParams(dimension_semantics=("arbitrary",),
                                             vmem_limit_bytes=VMEM_LIMIT),
        name="inproj",
    )(*ins)


RWKV_GROUP_HEADS = 4


def _block_diag(q, head_masks):
    zero = jnp.zeros((), q.dtype)
    return jnp.concatenate([jnp.where(m, q, zero) for m in head_masks], axis=0)


def _packed_dot1(lhs, q, head_masks, dims=_NN):
    return _dot(lhs.astype(BF16), _block_diag(q.astype(BF16), head_masks), dims)


def _packed_dot3(lhs, q, head_masks, dims=_NN):
    m = lhs.shape[0]
    lh, ll = _split2(lhs)
    qh, ql = _split2(q)
    top = _dot(jnp.concatenate([lh, ll], axis=0), _block_diag(qh, head_masks), dims)
    return top[:m] + (top[m:] + _dot(lh, _block_diag(ql, head_masks), dims))


def _diag_blocks(full, head_masks, n):
    out = None
    for h, m in enumerate(head_masks):
        part = jnp.where(m, full[h * n:(h + 1) * n, :], 0.0)
        out = part if out is None else out + part
    return out


def _rwkv_kernel(r_ref, k_ref, v_ref, lw_ref, kk_ref, b_ref, g_ref, rk_ref, gng_ref, gnb_ref, hsum_ref,
                 o_ref, state_ref, *, chunk):
    c = pl.program_id(1)
    n = HEAD_DIM
    rows, d = r_ref.shape[1], r_ref.shape[2]
    gw = RWKV_GROUP_HEADS * n
    n_groups = d // gw
    n_chunks = rows // chunk

    @pl.when(c == 0)
    def _():
        state_ref[...] = jnp.zeros_like(state_ref)

    t_idx = lax.broadcasted_iota(jnp.int32, (chunk, gw), 0)
    lane = lax.broadcasted_iota(jnp.int32, (chunk, gw), 1)
    i_idx = lane & (n - 1)
    strict = t_idx > i_idx
    incl = t_idx >= i_idx
    lane1 = lax.broadcasted_iota(jnp.int32, (1, gw), 1)
    hm = [(lane1 >> (n.bit_length() - 1)) == h for h in range(RWKV_GROUP_HEADS)]
    same = lambda s: (t_idx >> s) == (i_idx >> s)

    rr = lax.broadcasted_iota(jnp.int32, (rows, rows), 0)
    cc = lax.broadcasted_iota(jnp.int32, (rows, rows), 1)
    cs = chunk.bit_length() - 1
    ltri = jnp.where((rr >= cc) & ((rr >> cs) == (cc >> cs)), 1.0, 0.0).astype(BF16)

    r = r_ref[0]
    k = k_ref[0]
    v = v_ref[0]
    lw = lw_ref[0]
    kk = kk_ref[0]
    b = b_ref[0]

    cum = _dot_ones_lhs(ltri, lw)
    a_t = -kk * jnp.exp(cum - lw)
    r_t = r * jnp.exp(cum)
    e_neg = jnp.exp(-cum)
    b_t = b * e_neg
    k_t = k * e_neg

    chains = [(ci, gi) for ci in range(n_chunks) for gi in range(n_groups)]
    win = lambda x, ci, gi: x[ci * chunk:(ci + 1) * chunk, gi * gw:(gi + 1) * gw]
    tots = [cum[(ci + 1) * chunk - 1:(ci + 1) * chunk, :] for ci in range(n_chunks)]
    e_bars = [jnp.exp(tots[ci] - cum[ci * chunk:(ci + 1) * chunk, :]) for ci in range(n_chunks)]
    b_bar = [win(b, ci, gi) * e_bars[ci][:, gi * gw:(gi + 1) * gw] for ci, gi in chains]
    k_bar = [win(k, ci, gi) * e_bars[ci][:, gi * gw:(gi + 1) * gw] for ci, gi in chains]
    at = [win(a_t, ci, gi) for ci, gi in chains]
    rt = [win(r_t, ci, gi) for ci, gi in chains]
    vv = [win(v, ci, gi) for ci, gi in chains]

    ar = [jnp.concatenate([a_, r_], axis=0) for a_, r_ in zip(at, rt)]
    gram_b = [_packed_dot3(x, win(b_t, ci, gi), hm, _NT) for x, (ci, gi) in zip(ar, chains)]
    gram_k = [_packed_dot3(x, win(k_t, ci, gi), hm, _NT) for x, (ci, gi) in zip(ar, chains)]
    m_ab = [jnp.where(strict, g_[:chunk], 0.0) for g_ in gram_b]
    p_rb = [jnp.where(incl, g_[chunk:], 0.0) for g_ in gram_b]
    m_ak = [jnp.where(strict, g_[:chunk], 0.0) for g_ in gram_k]
    p_rk = [jnp.where(incl, g_[chunk:], 0.0) for g_ in gram_k]

    eye = jnp.where(t_idx == i_idx, 1.0, 0.0)
    ts = [eye + jnp.where(same(1), m, 0.0) for m in m_ab]
    s = 1
    while (1 << s) < chunk:
        level = same(s + 1) & jnp.logical_not(same(s))
        offs = [jnp.where(level, m, 0.0) for m in m_ab]
        mids = [_packed_dot1(t, o, hm) for t, o in zip(ts, offs)]
        ts = [t + _packed_dot1(md, t, hm) for t, md in zip(ts, mids)]
        s += 1

    mv = [_packed_dot1(m, x, hm) for m, x in zip(m_ak, vv)]
    a_hat = [_packed_dot1(t, x, hm) for t, x in zip(ts, at)]
    u0 = [_packed_dot1(t, x, hm) for t, x in zip(ts, mv)]
    r_hat = [x + _packed_dot1(p, a, hm) for x, p, a in zip(rt, p_rb, a_hat)]
    y0 = [_packed_dot1(pk, x, hm) + _packed_dot1(pb, u, hm) for pk, x, pb, u in zip(p_rk, vv, p_rb, u0)]
    g_mat = [_diag_blocks(_dot(a.astype(BF16), bb.astype(BF16), _TN), hm, n)
             for a, bb in zip(a_hat, b_bar)]
    h_mat = [_diag_blocks(_dot(jnp.concatenate([u, x], axis=0).astype(BF16),
                               jnp.concatenate([bb, kb], axis=0).astype(BF16), _TN), hm, n)
             for u, x, bb, kb in zip(u0, vv, b_bar, k_bar)]

    y_rows = []
    for ci in range(n_chunks):
        y_groups = []
        for gi in range(n_groups):
            j = ci * n_groups + gi
            s0 = state_ref[gi]
            y_groups.append(_packed_dot3(r_hat[j], s0, hm, _NT) + y0[j])
            w_tot = jnp.exp(tots[ci][:, gi * gw:(gi + 1) * gw])
            state_ref[gi] = s0 * w_tot + _packed_dot3(s0, g_mat[j], hm) + h_mat[j]
        y_rows.append(jnp.concatenate(y_groups, axis=1))
    y = jnp.concatenate(y_rows, axis=0)

    hsum = hsum_ref[...]
    inv_n = 1.0 / n
    mu = _head_sums(y, hsum) * inv_n
    yc = y - mu
    var = _head_sums(yc * yc, hsum) * inv_n
    yn = yc * lax.rsqrt(var + GN_EPS) * gng_ref[...] + gnb_ref[...]
    bonus = _head_sums(r * k * rk_ref[...], hsum)
    o_ref[0] = ((yn + bonus * v) * g_ref[0]).astype(o_ref.dtype)


def _rwkv(r, k, v, lw, kk, b, g, rw_rk, gn_g, gn_b, hsum, *, chunk, rows):
    bsz, seq, d = r.shape
    n_groups = d // (RWKV_GROUP_HEADS * HEAD_DIM)
    blk = pl.BlockSpec((1, rows, d), lambda bi, ci: (bi, ci, 0))
    const = lambda bi, ci: (0, 0)
    vec = lambda a: a.reshape(1, d)
    return pl.pallas_call(
        functools.partial(_rwkv_kernel, chunk=chunk),
        grid=(bsz, seq // rows),
        in_specs=[blk] * 7 + [pl.BlockSpec((1, d), const)] * 3 + [pl.BlockSpec(hsum.shape, const)],
        out_specs=blk,
        out_shape=jax.ShapeDtypeStruct((bsz, seq, d), BF16),
        scratch_shapes=[pltpu.VMEM((n_groups, HEAD_DIM, RWKV_GROUP_HEADS * HEAD_DIM), F32)],
        compiler_params=pltpu.CompilerParams(dimension_semantics=("arbitrary", "arbitrary"),
                                             vmem_limit_bytes=VMEM_LIMIT),
        name="rwkv",
    )(r, k, v, lw, kk, b, g, vec(rw_rk), vec(gn_g), vec(gn_b), hsum)


SB_LOG_UNDERFLOW = 105.0


def _sb_kernel(q_ref, k_ref, v_ref, o_ref, acc_ref, run_ref, *, blk):
    qi = pl.program_id(1)
    pairs = q_ref.shape[2] // (2 * HEAD_DIM)
    row = lax.broadcasted_iota(jnp.int32, (blk, blk), 0)
    col = lax.broadcasted_iota(jnp.int32, (blk, blk), 1)
    causal = col < row
    jj = lax.broadcasted_iota(jnp.int32, (2 * blk, 2 * blk), 0)
    ss = lax.broadcasted_iota(jnp.int32, (2 * blk, 2 * blk), 1)
    jj = jnp.where(jj >= blk, jj - blk, jj)
    suffix = jnp.where((jj >= ss) | (ss >= blk), 1.0, 0.0).astype(BF16)
    first_head = lax.broadcasted_iota(jnp.int32, (1, 2 * HEAD_DIM), 1) < HEAD_DIM
    zero = jnp.zeros((), BF16)

    def tile(kb, diagonal):
        start = pl.multiple_of(kb * blk, blk)
        lanes = [slice(p * 2 * HEAD_DIM, (p + 1) * 2 * HEAD_DIM) for p in range(pairs)]
        heads = [(p, hh) for p in range(pairs) for hh in range(2)]
        zs = []
        for p, hh in heads:
            sel = first_head if hh == 0 else jnp.logical_not(first_head)
            zs.append(_dot(jnp.where(sel, q_ref[0, :, lanes[p]], zero),
                           k_ref[0, pl.ds(start, blk), lanes[p]], _NT))
        sps = [_softplus(z) for z in zs]
        if diagonal:
            sps = [jnp.where(causal, sp, 0.0) for sp in sps]
        css = []
        for sp in sps:
            hi, lo = _split2(sp)
            css.append(_dot(jnp.concatenate([hi, lo], axis=1), suffix))
        atts = []
        low = None
        for (p, hh), z, cs in zip(heads, zs, css):
            logw = z - cs[:, :blk]
            if diagonal:
                att = jnp.where(causal, jnp.exp(logw), 0.0)
                run = cs[:, blk:]
            else:
                prev = run_ref[2 * p + hh]
                att = jnp.exp(logw - prev)
                run = prev + cs[:, blk:]
            run_ref[2 * p + hh] = run
            low = run if low is None else jnp.minimum(low, run)
            atts.append(att.astype(BF16))
        for p in range(pairs):
            vp = v_ref[0, pl.ds(start, blk), lanes[p]]
            vv = jnp.concatenate([jnp.where(first_head, vp, zero),
                                  jnp.where(first_head, zero, vp)], axis=0)
            out = _dot(jnp.concatenate(atts[2 * p:2 * p + 2], axis=1), vv)
            if diagonal:
                acc_ref[:, lanes[p]] = out
            else:
                acc_ref[:, lanes[p]] += out
        return (jnp.min(low) > SB_LOG_UNDERFLOW).astype(jnp.int32)

    done0 = tile(qi, True)

    def cond(c):
        i, done = c
        return jnp.logical_and(i <= qi, done == 0)

    def body(c):
        i, _ = c
        return i + 1, tile(qi - i, False)

    lax.while_loop(cond, body, (jnp.int32(1), done0))
    o_ref[0] = acc_ref[...].astype(o_ref.dtype)


def _sb_attention(q, k, v, *, blk):
    bsz, seq, d = q.shape
    heads = d // HEAD_DIM
    qspec = pl.BlockSpec((1, blk, d), lambda b, i: (b, i, 0))
    kvspec = pl.BlockSpec((1, seq, d), lambda b, i: (b, 0, 0), pipeline_mode=pl.Buffered(1))
    return pl.pallas_call(
        functools.partial(_sb_kernel, blk=blk),
        grid=(bsz, seq // blk),
        in_specs=[qspec, kvspec, kvspec],
        out_specs=qspec,
        out_shape=jax.ShapeDtypeStruct((bsz, seq, d), BF16),
        scratch_shapes=[pltpu.VMEM((blk, d), F32), pltpu.VMEM((heads, blk, blk), F32)],
        compiler_params=pltpu.CompilerParams(dimension_semantics=("arbitrary", "arbitrary"),
                                             vmem_limit_bytes=VMEM_LIMIT),
        name="sbattn",
    )(q, k, v)


def _merge_kernel(x_ref, ada_ref, ya_ref, ob_ref, ga_ref, gb_ref, wa_ref, wb_ref, wo_ref,
                  lng_ref, lnb_ref, o_ref, *, alpha):
    g1 = ada_ref[0, 2:3, :]
    y_a = _dot(ya_ref[...], wa_ref[...])
    y_b = _dot(ob_ref[...], wb_ref[...])
    merged = ga_ref[...].astype(F32) * y_a + gb_ref[...].astype(F32) * y_b
    mix = _dot(merged.astype(BF16), wo_ref[...])
    o_ref[...] = _norm_rows(alpha * x_ref[...] + g1 * mix, LN_EPS) * lng_ref[...] + lnb_ref[...]


def _merge(x2, ada3, ya, ob, ga, gb, wa_bf, wb_bf, wo_bf, ln_g, ln_b, *, seq, tm, alpha):
    m, d = x2.shape
    tiles_per_batch = seq // tm
    row = lambda i: (i, 0)
    const = lambda i: (0, 0)
    rows = lambda a: pl.BlockSpec((tm, a.shape[1]), row)
    full = lambda a: pl.BlockSpec(a.shape, const)
    vec = lambda a: a.reshape(1, -1)
    ins = [x2, ada3, ya, ob, ga, gb, wa_bf, wb_bf, wo_bf, vec(ln_g), vec(ln_b)]
    in_specs = [rows(x2), pl.BlockSpec((1,) + ada3.shape[1:], lambda i: (i // tiles_per_batch, 0, 0)),
                rows(ya), rows(ob), rows(ga), rows(gb)] + [full(a) for a in ins[6:]]
    return pl.pallas_call(
        functools.partial(_merge_kernel, alpha=alpha),
        grid=(m // tm,),
        in_specs=in_specs,
        out_specs=pl.BlockSpec((tm, d), row),
        out_shape=jax.ShapeDtypeStruct((m, d), F32),
        compiler_params=pltpu.CompilerParams(dimension_semantics=("arbitrary",),
                                             vmem_limit_bytes=VMEM_LIMIT),
        name="merge",
    )(*ins)


def _ffn_kernel(x_ref, ada_ref, w1_ref, b1_ref, w2_ref, b2_ref, lng_ref, lnb_ref, o_ref, *, alpha, fchunk):
    sh = ada_ref[0, 3:4, :]
    sc = ada_ref[0, 4:5, :]
    g2 = ada_ref[0, 5:6, :]
    x = x_ref[...]
    hb = (_norm_rows(x, ADALN_EPS) * (1.0 + sc) + sh).astype(BF16)
    d_ff = w1_ref.shape[1]
    ff = jnp.zeros(x.shape, F32)
    for j in range(d_ff // fchunk):
        sl = slice(j * fchunk, (j + 1) * fchunk)
        t = jnp.maximum(_dot(hb, w1_ref[:, sl]) + b1_ref[:, sl], 0.0)
        ff = ff + _dot((t * t).astype(BF16), w2_ref[sl, :])
    ff = ff + b2_ref[...]
    o_ref[...] = _norm_rows(alpha * x + g2 * ff, LN_EPS) * lng_ref[...] + lnb_ref[...]


def _ffn(x1, ada3, w1_bf, b1, w2_bf, b2, ln_g, ln_b, *, seq, tm, alpha, fchunk):
    m, d = x1.shape
    tiles_per_batch = seq // tm
    row = lambda i: (i, 0)
    const = lambda i: (0, 0)
    full = lambda a: pl.BlockSpec(a.shape, const)
    vec = lambda a: a.reshape(1, -1)
    ins = [x1, ada3, w1_bf, vec(b1), w2_bf, vec(b2), vec(ln_g), vec(ln_b)]
    in_specs = [pl.BlockSpec((tm, d), row),
                pl.BlockSpec((1,) + ada3.shape[1:], lambda i: (i // tiles_per_batch, 0, 0))]
    in_specs += [full(a) for a in ins[2:]]
    return pl.pallas_call(
        functools.partial(_ffn_kernel, alpha=alpha, fchunk=fchunk),
        grid=(m // tm,),
        in_specs=in_specs,
        out_specs=pl.BlockSpec((tm, d), row),
        out_shape=jax.ShapeDtypeStruct((m, d), F32),
        compiler_params=pltpu.CompilerParams(dimension_semantics=("arbitrary",),
                                             vmem_limit_bytes=VMEM_LIMIT),
        name="ffn",
    )(*ins)


def _branches(proj, rw_rk, rw_gn_g, rw_gn_b, hsum, bsz, seq):
    r, k, v, lw, kk, b, g, q, ks, vs = proj
    seq3 = lambda a: a.reshape(bsz, seq, -1)
    ya = _rwkv(seq3(r), seq3(k), seq3(v), seq3(lw), seq3(kk), seq3(b), seq3(g),
               rw_rk, rw_gn_g, rw_gn_b, hsum, chunk=min(RWKV_CHUNK, seq),
               rows=min(RWKV_CHUNKS_PER_STEP * RWKV_CHUNK, seq))
    o = _sb_attention(seq3(q), seq3(ks), seq3(vs), blk=min(SB_BLOCK, seq))
    return ya.reshape(bsz * seq, -1), o.reshape(bsz * seq, -1)


def _layer(x, c, w_ada, b_ada, w_in, mu_shift, rw_w0, rw_w2, rw_a0, rw_a2, rw_g2, rw_kk, rw_ka,
           rw_rk, rw_gn_g, rw_gn_b, w_branch_a, w_branch_b, w_out, ln1_g, ln1_b,
           w_ff1, b_ff1, w_ff2, b_ff2, ln2_g, ln2_b, *, alpha):
    bsz, seq, d = x.shape
    d_rwkv = rw_w0.shape[-1]
    heads = d_rwkv // HEAD_DIM
    tm = min(256, seq)

    ada3 = _ada(c, w_ada, b_ada).reshape(bsz, 6, d)
    lane = jnp.arange(RWKV_GROUP_HEADS * HEAD_DIM) // HEAD_DIM
    hsum = (lane[:, None] == lane[None, :]).astype(BF16)

    x2 = x.reshape(bsz * seq, d)
    (r, k, v, lw, kk, b, g, q, ks, vs, ga, gb) = _inproj(
        x2, ada3, w_in.astype(BF16), mu_shift, rw_w0, rw_w2, rw_a0, rw_a2, rw_g2, rw_kk, rw_ka, hsum,
        seq=seq, tm=tm)

    ya, ob = _branches((r, k, v, lw, kk, b, g, q, ks, vs), rw_rk, rw_gn_g, rw_gn_b, hsum, bsz, seq)

    x1 = _merge(x2, ada3, ya, ob, ga, gb,
                w_branch_a.astype(BF16), w_branch_b.astype(BF16), w_out.astype(BF16),
                ln1_g, ln1_b, seq=seq, tm=tm, alpha=alpha)
    out = _ffn(x1, ada3, w_ff1.astype(BF16), b_ff1, w_ff2.astype(BF16), b_ff2, ln2_g, ln2_b,
               seq=seq, tm=tm, alpha=alpha, fchunk=min(1024, w_ff1.shape[-1]))
    return out.reshape(bsz, seq, d)


def kernel(x, c, w_ada, b_ada, w_in, mu_shift, rw_w0, rw_w2, rw_a0, rw_a2, rw_g2, rw_kk, rw_ka, rw_rk,
           rw_gn_g, rw_gn_b, w_branch_a, w_branch_b, w_out, ln1_g, ln1_b, w_ff1, b_ff1, w_ff2, b_ff2,
           ln2_g, ln2_b):
    in_dtype = x.dtype
    layer_params = (w_ada, b_ada, w_in, mu_shift, rw_w0, rw_w2, rw_a0, rw_a2, rw_g2, rw_kk, rw_ka,
                    rw_rk, rw_gn_g, rw_gn_b, w_branch_a, w_branch_b, w_out, ln1_g, ln1_b,
                    w_ff1, b_ff1, w_ff2, b_ff2, ln2_g, ln2_b)
    depth = w_ada.shape[0]
    alpha = (2.0 * depth) ** 0.25
    for l in range(depth):
        x = _layer(x, c, *[p[l] for p in layer_params], alpha=alpha)
    return x.astype(in_dtype)
```

```python
import functools

import jax
import jax.numpy as jnp
from jax import lax
from jax.experimental import pallas as pl
from jax.experimental.pallas import tpu as pltpu

F32 = jnp.float32
BF16 = jnp.bfloat16

HEAD_DIM = 64
LN_EPS = 1e-5
ADALN_EPS = 1e-6
GN_EPS = 64e-5
RWKV_CHUNK = 64
RWKV_CHUNKS_PER_STEP = 4
SB_BLOCK = 128
VMEM_LIMIT = 56 * 1024 * 1024


def _split2(a):
    hi = a.astype(BF16)
    lo = (a - hi.astype(F32)).astype(BF16)
    return hi, lo


_NN = (((1,), (0,)), ((), ()))
_NT = (((1,), (1,)), ((), ()))
_TN = (((0,), (0,)), ((), ()))


def _dot(a, b, dims=_NN):
    return lax.dot_general(a, b, dims, preferred_element_type=F32)


def _dot3(a, b, dims=_NN):
    ah, al = _split2(a)
    bh, bl = _split2(b)
    return _dot(ah, bh, dims) + (_dot(al, bh, dims) + _dot(ah, bl, dims))


def _head_sums(x, ones_blk):
    m, w = x.shape[0], ones_blk.shape[0]
    hi, lo = _split2(x)
    outs = []
    for g in range(x.shape[1] // w):
        cols = slice(g * w, (g + 1) * w)
        both = _dot(jnp.concatenate([hi[:, cols], lo[:, cols]], axis=0), ones_blk)
        outs.append(both[:m] + both[m:])
    return jnp.concatenate(outs, axis=1)


def _dot_ones_lhs(a_bf, b):
    hi, lo = _split2(b)
    return _dot(a_bf, hi) + _dot(a_bf, lo)


def _softplus(y):
    return jnp.maximum(y, 0.0) + jnp.log(1.0 + jnp.exp(-jnp.abs(y)))


def _sigmoid(y):
    return 1.0 / (1.0 + jnp.exp(-y))


def _norm_rows(x, eps):
    mu = jnp.mean(x, axis=-1, keepdims=True)
    xc = x - mu
    var = jnp.mean(xc * xc, axis=-1, keepdims=True)
    return xc * lax.rsqrt(var + eps)


def _ada_kernel(c_ref, w_ref, b_ref, o_ref):
    c = c_ref[...]
    s = c * _sigmoid(c)
    o_ref[...] = jnp.dot(s, w_ref[...], preferred_element_type=F32,
                         precision=lax.Precision.HIGHEST) + b_ref[...]


def _ada(c, w_ada, b_ada):
    bsz, d = c.shape
    n = w_ada.shape[1]
    return pl.pallas_call(
        _ada_kernel,
        grid=(n // d,),
        in_specs=[pl.BlockSpec((bsz, d), lambda j: (0, 0)),
                  pl.BlockSpec((d, d), lambda j: (0, j)),
                  pl.BlockSpec((1, d), lambda j: (0, j))],
        out_specs=pl.BlockSpec((bsz, d), lambda j: (0, j)),
        out_shape=jax.ShapeDtypeStruct((bsz, n), F32),
        name="ada",
    )(c, w_ada, b_ada.reshape(1, n))


def _inproj_kernel(x_ref, ada_ref, win_ref, mu_ref, w0_ref, w2_ref, a0_ref, a2_ref, g2_ref,
                   kkw_ref, kaw_ref, hsum_ref,
                   r_ref, k_ref, v_ref, lw_ref, kk_ref, b_ref, g_ref,
                   q_ref, ks_ref, vs_ref, ga_ref, gb_ref,
                   carry_ref, *, tiles_per_batch, d_rwkv, d_decay, d_aaa, d_gate, d_sb, d_model):
    i = pl.program_id(0)
    tm = x_ref.shape[0]
    d_shift = 3 * d_rwkv + d_decay + d_aaa + d_gate

    sh = ada_ref[0, 0:1, :]
    sc = ada_ref[0, 1:2, :]
    h = _norm_rows(x_ref[...], ADALN_EPS) * (1.0 + sc) + sh
    hb = h.astype(BF16)

    @pl.when(i % tiles_per_batch == 0)
    def _():
        carry_ref[...] = jnp.zeros_like(carry_ref)

    row0 = lax.broadcasted_iota(jnp.int32, (tm, 1), 0) == 0

    def shifted(lo, width):
        cols = slice(lo, lo + width)
        z = _dot(hb, win_ref[:, cols])
        prev = jnp.where(row0, carry_ref[:, cols], pltpu.roll(z, 1, 0))
        carry_ref[:, cols] = z[tm - 1:tm, :]
        return z + mu_ref[:, cols] * (prev - z)

    half = d_model // 2
    rest = [(q_ref, slice(None), d_sb, lambda z: z * (HEAD_DIM ** -0.5)),
            (ks_ref, slice(None), d_sb, lambda z: z),
            (vs_ref, slice(None), d_sb, lambda z: z),
            (ga_ref, slice(0, half), half, _sigmoid), (ga_ref, slice(half, d_model), half, _sigmoid),
            (gb_ref, slice(0, half), half, _sigmoid), (gb_ref, slice(half, d_model), half, _sigmoid)]
    rest_col = [d_shift]

    def project_next():
        ref, cols, width, fn = rest.pop(0)
        z = _dot(hb, win_ref[:, rest_col[0]:rest_col[0] + width])
        ref[:, cols] = fn(z).astype(BF16)
        rest_col[0] += width

    zl = shifted(3 * d_rwkv, d_decay + d_aaa + d_gate)
    zw = zl[:, :d_decay]
    za = zl[:, d_decay:d_decay + d_aaa]
    zg = zl[:, d_decay + d_aaa:]
    k = shifted(d_rwkv, d_rwkv)
    ww = w0_ref[...] + _dot(jnp.tanh(zw).astype(BF16), w2_ref[...])
    w_log = -_softplus(-ww) - 0.5
    lw_ref[...] = -jnp.exp(w_log)
    project_next()
    arate = _sigmoid(a0_ref[...] + _dot(za.astype(BF16), a2_ref[...]))
    g_ref[...] = _dot(_sigmoid(zg).astype(BF16), g2_ref[...])
    r_ref[...] = shifted(0, d_rwkv)
    kkraw = k * kkw_ref[...]
    ssq = _head_sums(kkraw * kkraw, hsum_ref[...])
    kk = kkraw / jnp.maximum(jnp.sqrt(ssq), 1e-12)
    v_ref[...] = shifted(2 * d_rwkv, d_rwkv)
    k_ref[...] = k * (1.0 + (arate - 1.0) * kaw_ref[...])
    project_next()
    kk_ref[...] = kk
    b_ref[...] = kk * arate
    while rest:
        project_next()


def _inproj(x2, ada3, win_bf, mu_shift, rw_w0, rw_w2, rw_a0, rw_a2, rw_g2, rw_kk, rw_ka, hsum,
            *, seq, tm):
    m, d_model = x2.shape
    d_rwkv = rw_w0.shape[-1]
    d_decay, d_aaa, d_gate = rw_w2.shape[0], rw_a2.shape[0], rw_g2.shape[0]
    d_shift = 3 * d_rwkv + d_decay + d_aaa + d_gate
    d_sb = (win_bf.shape[1] - d_shift - 2 * d_model) // 3
    tiles_per_batch = seq // tm
    row = lambda i: (i, 0)
    const = lambda i: (0, 0)
    full = lambda a: pl.BlockSpec(a.shape, const)
    vec = lambda a: a.reshape(1, -1)
    ins = [x2, ada3, win_bf, vec(mu_shift), vec(rw_w0), rw_w2.astype(BF16), vec(rw_a0),
           rw_a2.astype(BF16), rw_g2.astype(BF16), vec(rw_kk), vec(rw_ka), hsum]
    in_specs = [pl.BlockSpec((tm, d_model), row),
                pl.BlockSpec((1,) + ada3.shape[1:], lambda i: (i // tiles_per_batch, 0, 0))]
    in_specs += [full(a) for a in ins[2:]]
    out_shape = ([jax.ShapeDtypeStruct((m, d_rwkv), F32)] * 7
                 + [jax.ShapeDtypeStruct((m, d_sb), BF16)] * 3
                 + [jax.ShapeDtypeStruct((m, d_model), BF16)] * 2)
    out_specs = ([pl.BlockSpec((tm, d_rwkv), row)] * 7 + [pl.BlockSpec((tm, d_sb), row)] * 3
                 + [pl.BlockSpec((tm, d_model), row)] * 2)
    kern = functools.partial(_inproj_kernel, tiles_per_batch=tiles_per_batch, d_rwkv=d_rwkv,
                             d_decay=d_decay, d_aaa=d_aaa, d_gate=d_gate, d_sb=d_sb, d_model=d_model)
    return pl.pallas_call(
        kern,
        grid=(m // tm,),
        in_specs=in_specs,
        out_specs=out_specs,
        out_shape=out_shape,
        scratch_shapes=[pltpu.VMEM((1, d_shift), F32)],
        compiler_params=pltpu.CompilerParams(dimension_semantics=("arbitrary",),
                                             vmem_limit_bytes=VMEM_LIMIT),
        name="inproj",
    )(*ins)


RWKV_GROUP_HEADS = 4


def _block_diag(q, head_masks):
    zero = jnp.zeros((), q.dtype)
    return jnp.concatenate([jnp.where(m, q, zero) for m in head_masks], axis=0)


def _packed_dot1(lhs, q, head_masks, dims=_NN):
    return _dot(lhs.astype(BF16), _block_diag(q.astype(BF16), head_masks), dims)


def _packed_dot3(lhs, q, head_masks, dims=_NN):
    m = lhs.shape[0]
    lh, ll = _split2(lhs)
    qh, ql = _split2(q)
    top = _dot(jnp.concatenate([lh, ll], axis=0), _block_diag(qh, head_masks), dims)
    return top[:m] + (top[m:] + _dot(lh, _block_diag(ql, head_masks), dims))


def _rwkv_kernel(r_ref, k_ref, v_ref, lw_ref, kk_ref, b_ref, g_ref, rk_ref, gng_ref, gnb_ref, hsum_ref,
                 o_ref, state_ref, *, chunk):
    c = pl.program_id(1)
    n = HEAD_DIM
    rows, d = r_ref.shape[1], r_ref.shape[2]
    gw = RWKV_GROUP_HEADS * n
    n_groups = d // gw
    n_chunks = rows // chunk

    @pl.when(c == 0)
    def _():
        state_ref[...] = jnp.zeros_like(state_ref)

    t_idx = lax.broadcasted_iota(jnp.int32, (chunk, gw), 0)
    lane = lax.broadcasted_iota(jnp.int32, (chunk, gw), 1)
    i_idx = lane & (n - 1)
    strict = t_idx > i_idx
    incl = t_idx >= i_idx
    lane1 = lax.broadcasted_iota(jnp.int32, (1, gw), 1)
    hm = [(lane1 >> (n.bit_length() - 1)) == h for h in range(RWKV_GROUP_HEADS)]
    same = lambda s: (t_idx >> s) == (i_idx >> s)

    rr = lax.broadcasted_iota(jnp.int32, (rows, rows), 0)
    cc = lax.broadcasted_iota(jnp.int32, (rows, rows), 1)
    cs = chunk.bit_length() - 1
    ltri = jnp.where((rr >= cc) & ((rr >> cs) == (cc >> cs)), 1.0, 0.0).astype(BF16)

    r = r_ref[0]
    k = k_ref[0]
    v = v_ref[0]
    lw = lw_ref[0]
    kk = kk_ref[0]
    b = b_ref[0]

    cum = _dot_ones_lhs(ltri, lw)
    a_t = -kk * jnp.exp(cum - lw)
    r_t = r * jnp.exp(cum)
    e_neg = jnp.exp(-cum)
    b_t = b * e_neg
    k_t = k * e_neg

    chains = [(ci, gi) for ci in range(n_chunks) for gi in range(n_groups)]
    win = lambda x, ci, gi: x[ci * chunk:(ci + 1) * chunk, gi * gw:(gi + 1) * gw]
    tots = [cum[(ci + 1) * chunk - 1:(ci + 1) * chunk, :] for ci in range(n_chunks)]
    e_bars = [jnp.exp(tots[ci] - cum[ci * chunk:(ci + 1) * chunk, :]) for ci in range(n_chunks)]
    b_bar = [win(b, ci, gi) * e_bars[ci][:, gi * gw:(gi + 1) * gw] for ci, gi in chains]
    k_bar = [win(k, ci, gi) * e_bars[ci][:, gi * gw:(gi + 1) * gw] for ci, gi in chains]
    at = [win(a_t, ci, gi) for ci, gi in chains]
    rt = [win(r_t, ci, gi) for ci, gi in chains]
    vv = [win(v, ci, gi) for ci, gi in chains]

    ar = [jnp.concatenate([a_, r_], axis=0) for a_, r_ in zip(at, rt)]
    gram_b = [_packed_dot3(x, win(b_t, ci, gi), hm, _NT) for x, (ci, gi) in zip(ar, chains)]
    gram_k = [_packed_dot3(x, win(k_t, ci, gi), hm, _NT) for x, (ci, gi) in zip(ar, chains)]
    m_ab = [jnp.where(strict, g_[:chunk], 0.0) for g_ in gram_b]
    p_rb = [jnp.where(incl, g_[chunk:], 0.0) for g_ in gram_b]
    m_ak = [jnp.where(strict, g_[:chunk], 0.0) for g_ in gram_k]
    p_rk = [jnp.where(incl, g_[chunk:], 0.0) for g_ in gram_k]

    eye = jnp.where(t_idx == i_idx, 1.0, 0.0)
    ts = [eye + jnp.where(same(1), m, 0.0) for m in m_ab]
    s = 1
    while (1 << s) < chunk:
        level = same(s + 1) & jnp.logical_not(same(s))
        offs = [jnp.where(level, m, 0.0) for m in m_ab]
        mids = [_packed_dot1(t, o, hm) for t, o in zip(ts, offs)]
        ts = [t + _packed_dot1(md, t, hm) for t, md in zip(ts, mids)]
        s += 1

    mv = [_packed_dot1(m, x, hm) for m, x in zip(m_ak, vv)]
    a_hat = [_packed_dot1(t, x, hm) for t, x in zip(ts, at)]
    u0 = [_packed_dot1(t, x, hm) for t, x in zip(ts, mv)]
    r_hat = [x + _packed_dot1(p, a, hm) for x, p, a in zip(rt, p_rb, a_hat)]
    y0 = [_packed_dot1(pk, x, hm) + _packed_dot1(pb, u, hm) for pk, x, pb, u in zip(p_rk, vv, p_rb, u0)]
    half_w = gw // 2
    lane_half = [(lane1 >> (half_w.bit_length() - 1)) == j for j in range(2)]
    second_in_half = ((lane >> (n.bit_length() - 1)) & 1) == 1

    def halves_rows(x):
        return jnp.concatenate([x[:, :half_w], x[:, half_w:]], axis=0).astype(BF16)

    def halves_masked(w):
        return _block_diag(w.astype(BF16), lane_half)

    def same_head(out):
        return jnp.where(second_in_half, out[n:], out[:n])

    g_mat = [same_head(_dot(halves_rows(a), halves_masked(bb), _TN))
             for a, bb in zip(a_hat, b_bar)]
    h_mat = [same_head(_dot(jnp.concatenate([halves_rows(u), halves_rows(x)], axis=0),
                            jnp.concatenate([halves_masked(bb), halves_masked(kb)], axis=0), _TN))
             for u, x, bb, kb in zip(u0, vv, b_bar, k_bar)]

    y_rows = []
    for ci in range(n_chunks):
        y_groups = []
        for gi in range(n_groups):
            j = ci * n_groups + gi
            s0 = state_ref[gi]
            y_groups.append(_packed_dot3(r_hat[j], s0, hm, _NT) + y0[j])
            w_tot = jnp.exp(tots[ci][:, gi * gw:(gi + 1) * gw])
            state_ref[gi] = s0 * w_tot + _packed_dot3(s0, g_mat[j], hm) + h_mat[j]
        y_rows.append(jnp.concatenate(y_groups, axis=1))
    y = jnp.concatenate(y_rows, axis=0)

    hsum = hsum_ref[...]
    inv_n = 1.0 / n
    mu = _head_sums(y, hsum) * inv_n
    yc = y - mu
    var = _head_sums(yc * yc, hsum) * inv_n
    yn = yc * lax.rsqrt(var + GN_EPS) * gng_ref[...] + gnb_ref[...]
    bonus = _head_sums(r * k * rk_ref[...], hsum)
    o_ref[0] = ((yn + bonus * v) * g_ref[0]).astype(o_ref.dtype)


def _rwkv(r, k, v, lw, kk, b, g, rw_rk, gn_g, gn_b, hsum, *, chunk, rows):
    bsz, seq, d = r.shape
    n_groups = d // (RWKV_GROUP_HEADS * HEAD_DIM)
    blk = pl.BlockSpec((1, rows, d), lambda bi, ci: (bi, ci, 0))
    const = lambda bi, ci: (0, 0)
    vec = lambda a: a.reshape(1, d)
    return pl.pallas_call(
        functools.partial(_rwkv_kernel, chunk=chunk),
        grid=(bsz, seq // rows),
        in_specs=[blk] * 7 + [pl.BlockSpec((1, d), const)] * 3 + [pl.BlockSpec(hsum.shape, const)],
        out_specs=blk,
        out_shape=jax.ShapeDtypeStruct((bsz, seq, d), BF16),
        scratch_shapes=[pltpu.VMEM((n_groups, HEAD_DIM, RWKV_GROUP_HEADS * HEAD_DIM), F32)],
        compiler_params=pltpu.CompilerParams(dimension_semantics=("arbitrary", "arbitrary"),
                                             vmem_limit_bytes=VMEM_LIMIT),
        name="rwkv",
    )(r, k, v, lw, kk, b, g, vec(rw_rk), vec(gn_g), vec(gn_b), hsum)


SB_LOG_UNDERFLOW = 105.0


def _sb_kernel(q_ref, k_ref, v_ref, o_ref, acc_ref, run_ref, *, blk):
    qi = pl.program_id(1)
    pairs = q_ref.shape[2] // (2 * HEAD_DIM)
    row = lax.broadcasted_iota(jnp.int32, (blk, blk), 0)
    col = lax.broadcasted_iota(jnp.int32, (blk, blk), 1)
    causal = col < row
    jj = lax.broadcasted_iota(jnp.int32, (blk, 2 * blk), 0)
    ss = lax.broadcasted_iota(jnp.int32, (blk, 2 * blk), 1)
    suffix = jnp.where((jj >= ss) | (ss >= blk), 1.0, 0.0).astype(BF16)
    first_head = lax.broadcasted_iota(jnp.int32, (1, 2 * HEAD_DIM), 1) < HEAD_DIM
    zero = jnp.zeros((), BF16)

    lanes = [slice(p * 2 * HEAD_DIM, (p + 1) * 2 * HEAD_DIM) for p in range(pairs)]
    heads = [(p, hh) for p in range(pairs) for hh in range(2)]

    def tile(kb, diagonal):
        start = pl.multiple_of(kb * blk, blk)
        zs = []
        for p, hh in heads:
            sel = first_head if hh == 0 else jnp.logical_not(first_head)
            zs.append(_dot(jnp.where(sel, q_ref[0, :, lanes[p]], zero),
                           k_ref[0, pl.ds(start, blk), lanes[p]], _NT))
        sps = [_softplus(z) for z in zs]
        if diagonal:
            sps = [jnp.where(causal, sp, 0.0) for sp in sps]
        css = [_dot(sp.astype(BF16), suffix) for sp in sps]
        atts = []
        low = None
        for n_, (z, cs) in enumerate(zip(zs, css)):
            logw = z - cs[:, :blk]
            if diagonal:
                att = jnp.where(causal, jnp.exp(logw), 0.0)
                run = cs[:, blk:]
            else:
                prev = run_ref[n_]
                att = jnp.exp(logw - prev)
                run = prev + cs[:, blk:]
            run_ref[n_] = run
            low = run if low is None else jnp.minimum(low, run)
            atts.append(att.astype(BF16))
        for p in range(pairs):
            vp = v_ref[0, pl.ds(start, blk), lanes[p]]
            vv = jnp.concatenate([jnp.where(first_head, vp, zero),
                                  jnp.where(first_head, zero, vp)], axis=0)
            out = _dot(jnp.concatenate(atts[2 * p:2 * p + 2], axis=1), vv)
            if diagonal:
                acc_ref[:, lanes[p]] = out
            else:
                acc_ref[:, lanes[p]] += out
        return (jnp.min(low) > SB_LOG_UNDERFLOW).astype(jnp.int32)

    done0 = tile(qi, True)

    def cond(c):
        i, done = c
        return jnp.logical_and(i <= qi, done == 0)

    def body(c):
        i, _ = c
        return i + 1, tile(qi - i, False)

    lax.while_loop(cond, body, (jnp.int32(1), done0))
    o_ref[0] = acc_ref[...].astype(o_ref.dtype)


def _sb_attention(q, k, v, *, blk):
    bsz, seq, d = q.shape
    heads = d // HEAD_DIM
    qspec = pl.BlockSpec((1, blk, d), lambda b, i: (b, i, 0))
    kvspec = pl.BlockSpec((1, seq, d), lambda b, i: (b, 0, 0), pipeline_mode=pl.Buffered(1))
    return pl.pallas_call(
        functools.partial(_sb_kernel, blk=blk),
        grid=(bsz, seq // blk),
        in_specs=[qspec, kvspec, kvspec],
        out_specs=qspec,
        out_shape=jax.ShapeDtypeStruct((bsz, seq, d), BF16),
        scratch_shapes=[pltpu.VMEM((blk, d), F32), pltpu.VMEM((heads, blk, blk), F32)],
        compiler_params=pltpu.CompilerParams(dimension_semantics=("arbitrary", "arbitrary"),
                                             vmem_limit_bytes=VMEM_LIMIT),
        name="sbattn",
    )(q, k, v)


def _merge_kernel(x_ref, ada_ref, ya_ref, ob_ref, ga_ref, gb_ref, wa_ref, wb_ref, wo_ref,
                  lng_ref, lnb_ref, o_ref, *, alpha):
    g1 = ada_ref[0, 2:3, :]
    y_a = _dot(ya_ref[...], wa_ref[...])
    y_b = _dot(ob_ref[...], wb_ref[...])
    merged = ga_ref[...].astype(F32) * y_a + gb_ref[...].astype(F32) * y_b
    mix = _dot(merged.astype(BF16), wo_ref[...])
    o_ref[...] = _norm_rows(alpha * x_ref[...] + g1 * mix, LN_EPS) * lng_ref[...] + lnb_ref[...]


def _merge(x2, ada3, ya, ob, ga, gb, wa_bf, wb_bf, wo_bf, ln_g, ln_b, *, seq, tm, alpha):
    m, d = x2.shape
    tiles_per_batch = seq // tm
    row = lambda i: (i, 0)
    const = lambda i: (0, 0)
    rows = lambda a: pl.BlockSpec((tm, a.shape[1]), row)
    full = lambda a: pl.BlockSpec(a.shape, const)
    vec = lambda a: a.reshape(1, -1)
    ins = [x2, ada3, ya, ob, ga, gb, wa_bf, wb_bf, wo_bf, vec(ln_g), vec(ln_b)]
    in_specs = [rows(x2), pl.BlockSpec((1,) + ada3.shape[1:], lambda i: (i // tiles_per_batch, 0, 0)),
                rows(ya), rows(ob), rows(ga), rows(gb)] + [full(a) for a in ins[6:]]
    return pl.pallas_call(
        functools.partial(_merge_kernel, alpha=alpha),
        grid=(m // tm,),
        in_specs=in_specs,
        out_specs=pl.BlockSpec((tm, d), row),
        out_shape=jax.ShapeDtypeStruct((m, d), F32),
        compiler_params=pltpu.CompilerParams(dimension_semantics=("arbitrary",),
                                             vmem_limit_bytes=VMEM_LIMIT),
        name="merge",
    )(*ins)


def _ffn_kernel(x_ref, ada_ref, w1_ref, b1_ref, w2_ref, b2_ref, lng_ref, lnb_ref, o_ref, *, alpha, fchunk):
    sh = ada_ref[0, 3:4, :]
    sc = ada_ref[0, 4:5, :]
    g2 = ada_ref[0, 5:6, :]
    x = x_ref[...]
    hb = (_norm_rows(x, ADALN_EPS) * (1.0 + sc) + sh).astype(BF16)
    d_ff = w1_ref.shape[1]
    ff = jnp.zeros(x.shape, F32)
    for j in range(d_ff // fchunk):
        sl = slice(j * fchunk, (j + 1) * fchunk)
        t = jnp.maximum(_dot(hb, w1_ref[:, sl]) + b1_ref[:, sl], 0.0)
        ff = ff + _dot((t * t).astype(BF16), w2_ref[sl, :])
    ff = ff + b2_ref[...]
    o_ref[...] = _norm_rows(alpha * x + g2 * ff, LN_EPS) * lng_ref[...] + lnb_ref[...]


def _ffn(x1, ada3, w1_bf, b1, w2_bf, b2, ln_g, ln_b, *, seq, tm, alpha, fchunk):
    m, d = x1.shape
    tiles_per_batch = seq // tm
    row = lambda i: (i, 0)
    const = lambda i: (0, 0)
    full = lambda a: pl.BlockSpec(a.shape, const)
    vec = lambda a: a.reshape(1, -1)
    ins = [x1, ada3, w1_bf, vec(b1), w2_bf, vec(b2), vec(ln_g), vec(ln_b)]
    in_specs = [pl.BlockSpec((tm, d), row),
                pl.BlockSpec((1,) + ada3.shape[1:], lambda i: (i // tiles_per_batch, 0, 0))]
    in_specs += [full(a) for a in ins[2:]]
    return pl.pallas_call(
        functools.partial(_ffn_kernel, alpha=alpha, fchunk=fchunk),
        grid=(m // tm,),
        in_specs=in_specs,
        out_specs=pl.BlockSpec((tm, d), row),
        out_shape=jax.ShapeDtypeStruct((m, d), F32),
        compiler_params=pltpu.CompilerParams(dimension_semantics=("arbitrary",),
                                             vmem_limit_bytes=VMEM_LIMIT),
        name="ffn",
    )(*ins)


def _branches(proj, rw_rk, rw_gn_g, rw_gn_b, hsum, bsz, seq):
    r, k, v, lw, kk, b, g, q, ks, vs = proj
    seq3 = lambda a: a.reshape(bsz, seq, -1)
    ya = _rwkv(seq3(r), seq3(k), seq3(v), seq3(lw), seq3(kk), seq3(b), seq3(g),
               rw_rk, rw_gn_g, rw_gn_b, hsum, chunk=min(RWKV_CHUNK, seq),
               rows=min(RWKV_CHUNKS_PER_STEP * RWKV_CHUNK, seq))
    o = _sb_attention(seq3(q), seq3(ks), seq3(vs), blk=min(SB_BLOCK, seq))
    return ya.reshape(bsz * seq, -1), o.reshape(bsz * seq, -1)


def _layer(x, c, w_ada, b_ada, w_in, mu_shift, rw_w0, rw_w2, rw_a0, rw_a2, rw_g2, rw_kk, rw_ka,
           rw_rk, rw_gn_g, rw_gn_b, w_branch_a, w_branch_b, w_out, ln1_g, ln1_b,
           w_ff1, b_ff1, w_ff2, b_ff2, ln2_g, ln2_b, *, alpha):
    bsz, seq, d = x.shape
    d_rwkv = rw_w0.shape[-1]
    heads = d_rwkv // HEAD_DIM
    tm = min(256, seq)

    ada3 = _ada(c, w_ada, b_ada).reshape(bsz, 6, d)
    lane = jnp.arange(RWKV_GROUP_HEADS * HEAD_DIM) // HEAD_DIM
    hsum = (lane[:, None] == lane[None, :]).astype(BF16)

    x2 = x.reshape(bsz * seq, d)
    (r, k, v, lw, kk, b, g, q, ks, vs, ga, gb) = _inproj(
        x2, ada3, w_in.astype(BF16), mu_shift, rw_w0, rw_w2, rw_a0, rw_a2, rw_g2, rw_kk, rw_ka, hsum,
        seq=seq, tm=tm)

    ya, ob = _branches((r, k, v, lw, kk, b, g, q, ks, vs), rw_rk, rw_gn_g, rw_gn_b, hsum, bsz, seq)

    x1 = _merge(x2, ada3, ya, ob, ga, gb,
                w_branch_a.astype(BF16), w_branch_b.astype(BF16), w_out.astype(BF16),
                ln1_g, ln1_b, seq=seq, tm=tm, alpha=alpha)
    out = _ffn(x1, ada3, w_ff1.astype(BF16), b_ff1, w_ff2.astype(BF16), b_ff2, ln2_g, ln2_b,
               seq=seq, tm=tm, alpha=alpha, fchunk=min(1024, w_ff1.shape[-1]))
    return out.reshape(bsz, seq, d)


def kernel(x, c, w_ada, b_ada, w_in, mu_shift, rw_w0, rw_w2, rw_a0, rw_a2, rw_g2, rw_kk, rw_ka, rw_rk,
           rw_gn_g, rw_gn_b, w_branch_a, w_branch_b, w_out, ln1_g, ln1_b, w_ff1, b_ff1, w_ff2, b_ff2,
           ln2_g, ln2_b):
    in_dtype = x.dtype
    layer_params = (w_ada, b_ada, w_in, mu_shift, rw_w0, rw_w2, rw_a0, rw_a2, rw_g2, rw_kk, rw_ka,
                    rw_rk, rw_gn_g, rw_gn_b, w_branch_a, w_branch_b, w_out, ln1_g, ln1_b,
                    w_ff1, b_ff1, w_ff2, b_ff2, ln2_g, ln2_b)
    depth = w_ada.shape[0]
    alpha = (2.0 * depth) ** 0.25
    for l in range(depth):
        x = _layer(x, c, *[p[l] for p in layer_params], alpha=alpha)
    return x.astype(in_dtype)
```

```python
import functools

import jax
import jax.numpy as jnp
from jax import lax
from jax.experimental import pallas as pl
from jax.experimental.pallas import tpu as pltpu

F32 = jnp.float32
BF16 = jnp.bfloat16

HEAD_DIM = 64
LN_EPS = 1e-5
ADALN_EPS = 1e-6
GN_EPS = 64e-5
RWKV_CHUNK = 64
RWKV_SECTIONS = 2
RWKV_SECTION_CHUNKS = 4
RWKV_SECTION_LAG = 12
RWKV_CHUNKS_PER_STEP = RWKV_SECTIONS * RWKV_SECTION_CHUNKS
SB_BLOCK = 128
VMEM_LIMIT = 56 * 1024 * 1024


def _split2(a):
    hi = a.astype(BF16)
    lo = (a - hi.astype(F32)).astype(BF16)
    return hi, lo


_NN = (((1,), (0,)), ((), ()))
_NT = (((1,), (1,)), ((), ()))
_TN = (((0,), (0,)), ((), ()))


def _dot(a, b, dims=_NN):
    return lax.dot_general(a, b, dims, preferred_element_type=F32)


def _dot3(a, b, dims=_NN):
    ah, al = _split2(a)
    bh, bl = _split2(b)
    return _dot(ah, bh, dims) + (_dot(al, bh, dims) + _dot(ah, bl, dims))


def _head_sums(x, ones_blk):
    m, w = x.shape[0], ones_blk.shape[0]
    hi, lo = _split2(x)
    outs = []
    for g in range(x.shape[1] // w):
        cols = slice(g * w, (g + 1) * w)
        both = _dot(jnp.concatenate([hi[:, cols], lo[:, cols]], axis=0), ones_blk)
        outs.append(both[:m] + both[m:])
    return jnp.concatenate(outs, axis=1)


def _dot_ones_lhs(a_bf, b):
    hi, lo = _split2(b)
    return _dot(a_bf, hi) + _dot(a_bf, lo)


def _softplus(y):
    return jnp.maximum(y, 0.0) + jnp.log(1.0 + jnp.exp(-jnp.abs(y)))


def _sigmoid(y):
    return 1.0 / (1.0 + jnp.exp(-y))


def _norm_rows(x, eps):
    mu = jnp.mean(x, axis=-1, keepdims=True)
    xc = x - mu
    var = jnp.mean(xc * xc, axis=-1, keepdims=True)
    return xc * lax.rsqrt(var + eps)


def _ada_kernel(c_ref, w_ref, b_ref, o_ref):
    c = c_ref[...]
    s = c * _sigmoid(c)
    o_ref[...] = jnp.dot(s, w_ref[...], preferred_element_type=F32,
                         precision=lax.Precision.HIGHEST) + b_ref[...]


def _ada(c, w_ada, b_ada):
    bsz, d = c.shape
    n = w_ada.shape[1]
    return pl.pallas_call(
        _ada_kernel,
        grid=(n // d,),
        in_specs=[pl.BlockSpec((bsz, d), lambda j: (0, 0)),
                  pl.BlockSpec((d, d), lambda j: (0, j)),
                  pl.BlockSpec((1, d), lambda j: (0, j))],
        out_specs=pl.BlockSpec((bsz, d), lambda j: (0, j)),
        out_shape=jax.ShapeDtypeStruct((bsz, n), F32),
        name="ada",
    )(c, w_ada, b_ada.reshape(1, n))


def _inproj_kernel(x_ref, ada_ref, win_ref, mu_ref, w0_ref, w2_ref, a0_ref, a2_ref, g2_ref,
                   kkw_ref, kaw_ref, hsum_ref,
                   r_ref, k_ref, v_ref, lw_ref, kk_ref, b_ref, g_ref,
                   q_ref, ks_ref, vs_ref, ga_ref, gb_ref,
                   carry_ref, *, tiles_per_batch, d_rwkv, d_decay, d_aaa, d_gate, d_sb, d_model):
    i = pl.program_id(0)
    tm = x_ref.shape[0]
    d_shift = 3 * d_rwkv + d_decay + d_aaa + d_gate

    sh = ada_ref[0, 0:1, :]
    sc = ada_ref[0, 1:2, :]
    h = _norm_rows(x_ref[...], ADALN_EPS) * (1.0 + sc) + sh
    hb = h.astype(BF16)

    @pl.when(i % tiles_per_batch == 0)
    def _():
        carry_ref[...] = jnp.zeros_like(carry_ref)

    row0 = lax.broadcasted_iota(jnp.int32, (tm, 1), 0) == 0

    def shifted(lo, width):
        cols = slice(lo, lo + width)
        z = _dot(hb, win_ref[:, cols])
        prev = jnp.where(row0, carry_ref[:, cols], pltpu.roll(z, 1, 0))
        carry_ref[:, cols] = z[tm - 1:tm, :]
        return z + mu_ref[:, cols] * (prev - z)

    half = d_model // 2
    rest = [(q_ref, slice(None), d_sb, lambda z: z * (HEAD_DIM ** -0.5)),
            (ks_ref, slice(None), d_sb, lambda z: z),
            (vs_ref, slice(None), d_sb, lambda z: z),
            (ga_ref, slice(0, half), half, _sigmoid), (ga_ref, slice(half, d_model), half, _sigmoid),
            (gb_ref, slice(0, half), half, _sigmoid), (gb_ref, slice(half, d_model), half, _sigmoid)]
    rest_col = [d_shift]

    def project_next():
        ref, cols, width, fn = rest.pop(0)
        z = _dot(hb, win_ref[:, rest_col[0]:rest_col[0] + width])
        ref[:, cols] = fn(z).astype(BF16)
        rest_col[0] += width

    zl = shifted(3 * d_rwkv, d_decay + d_aaa + d_gate)
    zw = zl[:, :d_decay]
    za = zl[:, d_decay:d_decay + d_aaa]
    zg = zl[:, d_decay + d_aaa:]
    k = shifted(d_rwkv, d_rwkv)
    ww = w0_ref[...] + _dot(jnp.tanh(zw).astype(BF16), w2_ref[...])
    w_log = -_softplus(-ww) - 0.5
    lw_ref[...] = -jnp.exp(w_log)
    project_next()
    arate = _sigmoid(a0_ref[...] + _dot(za.astype(BF16), a2_ref[...]))
    g_ref[...] = _dot(_sigmoid(zg).astype(BF16), g2_ref[...])
    r_ref[...] = shifted(0, d_rwkv)
    kkraw = k * kkw_ref[...]
    ssq = _head_sums(kkraw * kkraw, hsum_ref[...])
    kk = kkraw / jnp.maximum(jnp.sqrt(ssq), 1e-12)
    v_ref[...] = shifted(2 * d_rwkv, d_rwkv)
    k_ref[...] = k * (1.0 + (arate - 1.0) * kaw_ref[...])
    project_next()
    kk_ref[...] = kk
    b_ref[...] = kk * arate
    while rest:
        project_next()


def _inproj(x2, ada3, win_bf, mu_shift, rw_w0, rw_w2, rw_a0, rw_a2, rw_g2, rw_kk, rw_ka, hsum,
            *, seq, tm):
    m, d_model = x2.shape
    d_rwkv = rw_w0.shape[-1]
    d_decay, d_aaa, d_gate = rw_w2.shape[0], rw_a2.shape[0], rw_g2.shape[0]
    d_shift = 3 * d_rwkv + d_decay + d_aaa + d_gate
    d_sb = (win_bf.shape[1] - d_shift - 2 * d_model) // 3
    tiles_per_batch = seq // tm
    row = lambda i: (i, 0)
    const = lambda i: (0, 0)
    full = lambda a: pl.BlockSpec(a.shape, const)
    vec = lambda a: a.reshape(1, -1)
    ins = [x2, ada3, win_bf, vec(mu_shift), vec(rw_w0), rw_w2.astype(BF16), vec(rw_a0),
           rw_a2.astype(BF16), rw_g2.astype(BF16), vec(rw_kk), vec(rw_ka), hsum]
    in_specs = [pl.BlockSpec((tm, d_model), row),
                pl.BlockSpec((1,) + ada3.shape[1:], lambda i: (i // tiles_per_batch, 0, 0))]
    in_specs += [full(a) for a in ins[2:]]
    out_shape = ([jax.ShapeDtypeStruct((m, d_rwkv), F32)] * 7
                 + [jax.ShapeDtypeStruct((m, d_sb), BF16)] * 3
                 + [jax.ShapeDtypeStruct((m, d_model), BF16)] * 2)
    out_specs = ([pl.BlockSpec((tm, d_rwkv), row)] * 7 + [pl.BlockSpec((tm, d_sb), row)] * 3
                 + [pl.BlockSpec((tm, d_model), row)] * 2)
    kern = functools.partial(_inproj_kernel, tiles_per_batch=tiles_per_batch, d_rwkv=d_rwkv,
                             d_decay=d_decay, d_aaa=d_aaa, d_gate=d_gate, d_sb=d_sb, d_model=d_model)
    return pl.pallas_call(
        kern,
        grid=(m // tm,),
        in_specs=in_specs,
        out_specs=out_specs,
        out_shape=out_shape,
        scratch_shapes=[pltpu.VMEM((1, d_shift), F32)],
        compiler_params=pltpu.CompilerParams(dimension_semantics=("arbitrary",),
                                             vmem_limit_bytes=VMEM_LIMIT),
        name="inproj",
    )(*ins)


RWKV_GROUP_HEADS = 4


def _block_diag(q, head_masks):
    zero = jnp.zeros((), q.dtype)
    return jnp.concatenate([jnp.where(m, q, zero) for m in head_masks], axis=0)


def _packed_dot1(lhs, q, head_masks, dims=_NN):
    return _dot(lhs.astype(BF16), _block_diag(q.astype(BF16), head_masks), dims)


def _packed_dot_wide_lhs(lhs, q, head_masks, dims=_NN):
    m = lhs.shape[0]
    lh, ll = _split2(lhs)
    both = _dot(jnp.concatenate([lh, ll], axis=0), _block_diag(q.astype(BF16), head_masks), dims)
    return both[:m] + both[m:]


def _rwkv_kernel(r_ref, k_ref, v_ref, lw_ref, kk_ref, b_ref, g_ref, rk_ref, gng_ref, gnb_ref, hsum_ref,
                 o_ref, state_ref, *, chunk):
    c = pl.program_id(1)
    n = HEAD_DIM
    rows, d = r_ref.shape[1], r_ref.shape[2]
    gw = RWKV_GROUP_HEADS * n
    n_groups = d // gw

    @pl.when(c == 0)
    def _():
        state_ref[...] = jnp.zeros_like(state_ref)

    t_idx = lax.broadcasted_iota(jnp.int32, (chunk, gw), 0)
    lane = lax.broadcasted_iota(jnp.int32, (chunk, gw), 1)
    i_idx = lane & (n - 1)
    strict = t_idx > i_idx
    incl = t_idx >= i_idx
    lane1 = lax.broadcasted_iota(jnp.int32, (1, gw), 1)
    hm = [(lane1 >> (n.bit_length() - 1)) == h for h in range(RWKV_GROUP_HEADS)]
    same = lambda s: (t_idx >> s) == (i_idx >> s)

    half_w = gw // 2
    lane_half = [(lane1 >> (half_w.bit_length() - 1)) == j for j in range(2)]
    second_in_half = ((lane >> (n.bit_length() - 1)) & 1) == 1

    def halves_rows(x):
        return jnp.concatenate([x[:, :half_w], x[:, half_w:]], axis=0).astype(BF16)

    def halves_masked(w):
        return _block_diag(w.astype(BF16), lane_half)

    def same_head(out):
        return jnp.where(second_in_half, out[n:], out[:n])

    sec_rows = rows // RWKV_SECTIONS
    sec_chunks = sec_rows // chunk
    rr = lax.broadcasted_iota(jnp.int32, (sec_rows, sec_rows), 0)
    cc = lax.broadcasted_iota(jnp.int32, (sec_rows, sec_rows), 1)
    cs = chunk.bit_length() - 1
    ltri = jnp.where((rr >= cc) & ((rr >> cs) == (cc >> cs)), 1.0, 0.0).astype(BF16)
    eye = jnp.where(t_idx == i_idx, 1.0, 0.0)
    chains = [(ci, gi) for ci in range(sec_chunks) for gi in range(n_groups)]
    win = lambda x, ci, gi: x[ci * chunk:(ci + 1) * chunk, gi * gw:(gi + 1) * gw]

    def section(row0):
        rs = slice(row0, row0 + sec_rows)
        r, k, v, lw = r_ref[0, rs, :], k_ref[0, rs, :], v_ref[0, rs, :], lw_ref[0, rs, :]
        kk, b = kk_ref[0, rs, :], b_ref[0, rs, :]
        cum = _dot_ones_lhs(ltri, lw)
        a_t = -kk * jnp.exp(cum - lw)
        r_t = r * jnp.exp(cum)
        e_neg = jnp.exp(-cum)
        b_t = b * e_neg
        k_t = k * e_neg
        tots = [cum[(ci + 1) * chunk - 1:(ci + 1) * chunk, :] for ci in range(sec_chunks)]
        e_bars = [jnp.exp(tots[ci] - cum[ci * chunk:(ci + 1) * chunk, :]) for ci in range(sec_chunks)]
        b_bar = [win(b, ci, gi) * e_bars[ci][:, gi * gw:(gi + 1) * gw] for ci, gi in chains]
        k_bar = [win(k, ci, gi) * e_bars[ci][:, gi * gw:(gi + 1) * gw] for ci, gi in chains]
        at = [win(a_t, ci, gi) for ci, gi in chains]
        rt = [win(r_t, ci, gi) for ci, gi in chains]
        vv = [win(v, ci, gi) for ci, gi in chains]
        yield

        ar = [jnp.concatenate([a_, r_], axis=0) for a_, r_ in zip(at, rt)]
        gram_b = [_packed_dot1(x, win(b_t, ci, gi), hm, _NT) for x, (ci, gi) in zip(ar, chains)]
        yield
        gram_k = [_packed_dot1(x, win(k_t, ci, gi), hm, _NT) for x, (ci, gi) in zip(ar, chains)]
        m_ab = [jnp.where(strict, g_[:chunk], 0.0) for g_ in gram_b]
        p_rb = [jnp.where(incl, g_[chunk:], 0.0) for g_ in gram_b]
        m_ak = [jnp.where(strict, g_[:chunk], 0.0) for g_ in gram_k]
        p_rk = [jnp.where(incl, g_[chunk:], 0.0) for g_ in gram_k]
        yield

        ts = [eye + jnp.where(same(1), m, 0.0) for m in m_ab]
        s = 1
        while (1 << s) < chunk:
            level = same(s + 1) & jnp.logical_not(same(s))
            offs = [jnp.where(level, m, 0.0) for m in m_ab]
            mids = [_packed_dot1(t, o, hm) for t, o in zip(ts, offs)]
            yield
            ts = [t + _packed_dot1(md, t, hm) for t, md in zip(ts, mids)]
            yield
            s += 1

        mv = [_packed_dot1(m, x, hm) for m, x in zip(m_ak, vv)]
        a_hat = [_packed_dot1(t, x, hm) for t, x in zip(ts, at)]
        yield
        u0 = [_packed_dot1(t, x, hm) for t, x in zip(ts, mv)]
        r_hat = [x + _packed_dot1(p, a, hm) for x, p, a in zip(rt, p_rb, a_hat)]
        yield
        y0 = [_packed_dot1(pk, x, hm) + _packed_dot1(pb, u, hm) for pk, x, pb, u in zip(p_rk, vv, p_rb, u0)]
        yield
        g_mat = [same_head(_dot(halves_rows(a), halves_masked(bb), _TN))
                 for a, bb in zip(a_hat, b_bar)]
        h_mat = [same_head(_dot(jnp.concatenate([halves_rows(u), halves_rows(x)], axis=0),
                                jnp.concatenate([halves_masked(bb), halves_masked(kb)], axis=0), _TN))
                 for u, x, bb, kb in zip(u0, vv, b_bar, k_bar)]
        yield

        y_rows = []
        for ci in range(sec_chunks):
            y_groups = []
            for gi in range(n_groups):
                j = ci * n_groups + gi
                s0 = state_ref[gi]
                y_groups.append(_packed_dot1(r_hat[j], s0, hm, _NT) + y0[j])
                w_tot = jnp.exp(tots[ci][:, gi * gw:(gi + 1) * gw])
                state_ref[gi] = s0 * w_tot + _packed_dot_wide_lhs(s0, g_mat[j], hm) + h_mat[j]
            y_rows.append(jnp.concatenate(y_groups, axis=1))
            yield
        y = jnp.concatenate(y_rows, axis=0)

        hsum = hsum_ref[...]
        inv_n = 1.0 / n
        mu = _head_sums(y, hsum) * inv_n
        yc = y - mu
        var = _head_sums(yc * yc, hsum) * inv_n
        yn = yc * lax.rsqrt(var + GN_EPS) * gng_ref[...] + gnb_ref[...]
        bonus = _head_sums(r * k * rk_ref[...], hsum)
        o_ref[0, rs, :] = ((yn + bonus * v) * g_ref[0, rs, :]).astype(o_ref.dtype)

    assert RWKV_SECTION_LAG >= sec_chunks
    gens = [section(si * sec_rows) for si in range(RWKV_SECTIONS)]
    live = [True] * RWKV_SECTIONS
    tick = 0
    while any(live):
        for si, gen in enumerate(gens):
            if live[si] and tick >= si * RWKV_SECTION_LAG:
                live[si] = next(gen, "done") != "done"
        tick += 1


def _rwkv(r, k, v, lw, kk, b, g, rw_rk, gn_g, gn_b, hsum, *, chunk, rows):
    bsz, seq, d = r.shape
    n_groups = d // (RWKV_GROUP_HEADS * HEAD_DIM)
    blk = pl.BlockSpec((1, rows, d), lambda bi, ci: (bi, ci, 0))
    const = lambda bi, ci: (0, 0)
    vec = lambda a: a.reshape(1, d)
    return pl.pallas_call(
        functools.partial(_rwkv_kernel, chunk=chunk),
        grid=(bsz, seq // rows),
        in_specs=[blk] * 7 + [pl.BlockSpec((1, d), const)] * 3 + [pl.BlockSpec(hsum.shape, const)],
        out_specs=blk,
        out_shape=jax.ShapeDtypeStruct((bsz, seq, d), BF16),
        scratch_shapes=[pltpu.VMEM((n_groups, HEAD_DIM, RWKV_GROUP_HEADS * HEAD_DIM), F32)],
        compiler_params=pltpu.CompilerParams(dimension_semantics=("arbitrary", "arbitrary"),
                                             vmem_limit_bytes=VMEM_LIMIT),
        name="rwkv",
    )(r, k, v, lw, kk, b, g, vec(rw_rk), vec(gn_g), vec(gn_b), hsum)


SB_LOG_UNDERFLOW = 105.0


def _sb_kernel(q_ref, k_ref, v_ref, o_ref, acc_ref, run_ref, *, blk):
    qi = pl.program_id(1)
    pairs = q_ref.shape[2] // (2 * HEAD_DIM)
    row = lax.broadcasted_iota(jnp.int32, (blk, blk), 0)
    col = lax.broadcasted_iota(jnp.int32, (blk, blk), 1)
    causal = col < row
    jj = lax.broadcasted_iota(jnp.int32, (blk, 2 * blk), 0)
    ss = lax.broadcasted_iota(jnp.int32, (blk, 2 * blk), 1)
    suffix = jnp.where((jj >= ss) | (ss >= blk), 1.0, 0.0).astype(BF16)
    first_head = lax.broadcasted_iota(jnp.int32, (1, 2 * HEAD_DIM), 1) < HEAD_DIM
    zero = jnp.zeros((), BF16)

    lanes = [slice(p * 2 * HEAD_DIM, (p + 1) * 2 * HEAD_DIM) for p in range(pairs)]
    heads = [(p, hh) for p in range(pairs) for hh in range(2)]

    def tile(kb, diagonal):
        start = pl.multiple_of(kb * blk, blk)
        zs = []
        for p, hh in heads:
            sel = first_head if hh == 0 else jnp.logical_not(first_head)
            zs.append(_dot(jnp.where(sel, q_ref[0, :, lanes[p]], zero),
                           k_ref[0, pl.ds(start, blk), lanes[p]], _NT))
        sps = [_softplus(z) for z in zs]
        if diagonal:
            sps = [jnp.where(causal, sp, 0.0) for sp in sps]
        css = [_dot(sp.astype(BF16), suffix) for sp in sps]
        atts = []
        low = None
        for n_, (z, cs) in enumerate(zip(zs, css)):
            logw = z - cs[:, :blk]
            if diagonal:
                att = jnp.where(causal, jnp.exp(logw), 0.0)
                run = cs[:, blk:]
            else:
                prev = run_ref[n_]
                att = jnp.exp(logw - prev)
                run = prev + cs[:, blk:]
            run_ref[n_] = run
            low = run if low is None else jnp.minimum(low, run)
            atts.append(att.astype(BF16))
        for p in range(pairs):
            vp = v_ref[0, pl.ds(start, blk), lanes[p]]
            vv = jnp.concatenate([jnp.where(first_head, vp, zero),
                                  jnp.where(first_head, zero, vp)], axis=0)
            out = _dot(jnp.concatenate(atts[2 * p:2 * p + 2], axis=1), vv)
            if diagonal:
                acc_ref[:, lanes[p]] = out
            else:
                acc_ref[:, lanes[p]] += out
        return (jnp.min(low) > SB_LOG_UNDERFLOW).astype(jnp.int32)

    done0 = tile(qi, True)

    def cond(c):
        i, done = c
        return jnp.logical_and(i <= qi, done == 0)

    def body(c):
        i, _ = c
        return i + 1, tile(qi - i, False)

    lax.while_loop(cond, body, (jnp.int32(1), done0))
    o_ref[0] = acc_ref[...].astype(o_ref.dtype)


def _sb_attention(q, k, v, *, blk):
    bsz, seq, d = q.shape
    heads = d // HEAD_DIM
    qspec = pl.BlockSpec((1, blk, d), lambda b, i: (b, i, 0))
    kvspec = pl.BlockSpec((1, seq, d), lambda b, i: (b, 0, 0), pipeline_mode=pl.Buffered(1))
    return pl.pallas_call(
        functools.partial(_sb_kernel, blk=blk),
        grid=(bsz, seq // blk),
        in_specs=[qspec, kvspec, kvspec],
        out_specs=qspec,
        out_shape=jax.ShapeDtypeStruct((bsz, seq, d), BF16),
        scratch_shapes=[pltpu.VMEM((blk, d), F32), pltpu.VMEM((heads, blk, blk), F32)],
        compiler_params=pltpu.CompilerParams(dimension_semantics=("arbitrary", "arbitrary"),
                                             vmem_limit_bytes=VMEM_LIMIT),
        name="sbattn",
    )(q, k, v)


def _merge_kernel(x_ref, ada_ref, ya_ref, ob_ref, ga_ref, gb_ref, wa_ref, wb_ref, wo_ref,
                  lng_ref, lnb_ref, o_ref, *, alpha):
    g1 = ada_ref[0, 2:3, :]
    y_a = _dot(ya_ref[...], wa_ref[...])
    y_b = _dot(ob_ref[...], wb_ref[...])
    merged = ga_ref[...].astype(F32) * y_a + gb_ref[...].astype(F32) * y_b
    mix = _dot(merged.astype(BF16), wo_ref[...])
    o_ref[...] = _norm_rows(alpha * x_ref[...] + g1 * mix, LN_EPS) * lng_ref[...] + lnb_ref[...]


def _merge(x2, ada3, ya, ob, ga, gb, wa_bf, wb_bf, wo_bf, ln_g, ln_b, *, seq, tm, alpha):
    m, d = x2.shape
    tiles_per_batch = seq // tm
    row = lambda i: (i, 0)
    const = lambda i: (0, 0)
    rows = lambda a: pl.BlockSpec((tm, a.shape[1]), row)
    full = lambda a: pl.BlockSpec(a.shape, const)
    vec = lambda a: a.reshape(1, -1)
    ins = [x2, ada3, ya, ob, ga, gb, wa_bf, wb_bf, wo_bf, vec(ln_g), vec(ln_b)]
    in_specs = [rows(x2), pl.BlockSpec((1,) + ada3.shape[1:], lambda i: (i // tiles_per_batch, 0, 0)),
                rows(ya), rows(ob), rows(ga), rows(gb)] + [full(a) for a in ins[6:]]
    return pl.pallas_call(
        functools.partial(_merge_kernel, alpha=alpha),
        grid=(m // tm,),
        in_specs=in_specs,
        out_specs=pl.BlockSpec((tm, d), row),
        out_shape=jax.ShapeDtypeStruct((m, d), F32),
        compiler_params=pltpu.CompilerParams(dimension_semantics=("arbitrary",),
                                             vmem_limit_bytes=VMEM_LIMIT),
        name="merge",
    )(*ins)


def _ffn_kernel(x_ref, ada_ref, w1_ref, b1_ref, w2_ref, b2_ref, lng_ref, lnb_ref, o_ref, *, alpha, fchunk):
    sh = ada_ref[0, 3:4, :]
    sc = ada_ref[0, 4:5, :]
    g2 = ada_ref[0, 5:6, :]
    x = x_ref[...]
    hb = (_norm_rows(x, ADALN_EPS) * (1.0 + sc) + sh).astype(BF16)
    d_ff = w1_ref.shape[1]
    ff = jnp.zeros(x.shape, F32)
    for j in range(d_ff // fchunk):
        sl = slice(j * fchunk, (j + 1) * fchunk)
        t = jnp.maximum(_dot(hb, w1_ref[:, sl]) + b1_ref[:, sl], 0.0)
        ff = ff + _dot((t * t).astype(BF16), w2_ref[sl, :])
    ff = ff + b2_ref[...]
    o_ref[...] = _norm_rows(alpha * x + g2 * ff, LN_EPS) * lng_ref[...] + lnb_ref[...]


def _ffn(x1, ada3, w1_bf, b1, w2_bf, b2, ln_g, ln_b, *, seq, tm, alpha, fchunk):
    m, d = x1.shape
    tiles_per_batch = seq // tm
    row = lambda i: (i, 0)
    const = lambda i: (0, 0)
    full = lambda a: pl.BlockSpec(a.shape, const)
    vec = lambda a: a.reshape(1, -1)
    ins = [x1, ada3, w1_bf, vec(b1), w2_bf, vec(b2), vec(ln_g), vec(ln_b)]
    in_specs = [pl.BlockSpec((tm, d), row),
                pl.BlockSpec((1,) + ada3.shape[1:], lambda i: (i // tiles_per_batch, 0, 0))]
    in_specs += [full(a) for a in ins[2:]]
    return pl.pallas_call(
        functools.partial(_ffn_kernel, alpha=alpha, fchunk=fchunk),
        grid=(m // tm,),
        in_specs=in_specs,
        out_specs=pl.BlockSpec((tm, d), row),
        out_shape=jax.ShapeDtypeStruct((m, d), F32),
        compiler_params=pltpu.CompilerParams(dimension_semantics=("arbitrary",),
                                             vmem_limit_bytes=VMEM_LIMIT),
        name="ffn",
    )(*ins)


def _branches(proj, rw_rk, rw_gn_g, rw_gn_b, hsum, bsz, seq):
    r, k, v, lw, kk, b, g, q, ks, vs = proj
    seq3 = lambda a: a.reshape(bsz, seq, -1)
    ya = _rwkv(seq3(r), seq3(k), seq3(v), seq3(lw), seq3(kk), seq3(b), seq3(g),
               rw_rk, rw_gn_g, rw_gn_b, hsum, chunk=min(RWKV_CHUNK, seq),
               rows=min(RWKV_CHUNKS_PER_STEP * RWKV_CHUNK, seq))
    o = _sb_attention(seq3(q), seq3(ks), seq3(vs), blk=min(SB_BLOCK, seq))
    return ya.reshape(bsz * seq, -1), o.reshape(bsz * seq, -1)


def _layer(x, c, w_ada, b_ada, w_in, mu_shift, rw_w0, rw_w2, rw_a0, rw_a2, rw_g2, rw_kk, rw_ka,
           rw_rk, rw_gn_g, rw_gn_b, w_branch_a, w_branch_b, w_out, ln1_g, ln1_b,
           w_ff1, b_ff1, w_ff2, b_ff2, ln2_g, ln2_b, *, alpha):
    bsz, seq, d = x.shape
    d_rwkv = rw_w0.shape[-1]
    heads = d_rwkv // HEAD_DIM
    tm = min(256, seq)

    ada3 = _ada(c, w_ada, b_ada).reshape(bsz, 6, d)
    lane = jnp.arange(RWKV_GROUP_HEADS * HEAD_DIM) // HEAD_DIM
    hsum = (lane[:, None] == lane[None, :]).astype(BF16)

    x2 = x.reshape(bsz * seq, d)
    (r, k, v, lw, kk, b, g, q, ks, vs, ga, gb) = _inproj(
        x2, ada3, w_in.astype(BF16), mu_shift, rw_w0, rw_w2, rw_a0, rw_a2, rw_g2, rw_kk, rw_ka, hsum,
        seq=seq, tm=tm)

    ya, ob = _branches((r, k, v, lw, kk, b, g, q, ks, vs), rw_rk, rw_gn_g, rw_gn_b, hsum, bsz, seq)

    x1 = _merge(x2, ada3, ya, ob, ga, gb,
                w_branch_a.astype(BF16), w_branch_b.astype(BF16), w_out.astype(BF16),
                ln1_g, ln1_b, seq=seq, tm=tm, alpha=alpha)
    out = _ffn(x1, ada3, w_ff1.astype(BF16), b_ff1, w_ff2.astype(BF16), b_ff2, ln2_g, ln2_b,
               seq=seq, tm=tm, alpha=alpha, fchunk=min(1024, w_ff1.shape[-1]))
    return out.reshape(bsz, seq, d)


def kernel(x, c, w_ada, b_ada, w_in, mu_shift, rw_w0, rw_w2, rw_a0, rw_a2, rw_g2, rw_kk, rw_ka, rw_rk,
           rw_gn_g, rw_gn_b, w_branch_a, w_branch_b, w_out, ln1_g, ln1_b, w_ff1, b_ff1, w_ff2, b_ff2,
           ln2_g, ln2_b):
    in_dtype = x.dtype
    layer_params = (w_ada, b_ada, w_in, mu_shift, rw_w0, rw_w2, rw_a0, rw_a2, rw_g2, rw_kk, rw_ka,
                    rw_rk, rw_gn_g, rw_gn_b, w_branch_a, w_branch_b, w_out, ln1_g, ln1_b,
                    w_ff1, b_ff1, w_ff2, b_ff2, ln2_g, ln2_b)
    depth = w_ada.shape[0]
    alpha = (2.0 * depth) ** 0.25
    for l in range(depth):
        x = _layer(x, c, *[p[l] for p in layer_params], alpha=alpha)
    return x.astype(in_dtype)
```

```python
import functools

import jax
import jax.numpy as jnp
from jax import lax
from jax.experimental import pallas as pl
from jax.experimental.pallas import tpu as pltpu

F32 = jnp.float32
BF16 = jnp.bfloat16

HEAD_DIM = 64
LN_EPS = 1e-5
ADALN_EPS = 1e-6
GN_EPS = 64e-5
RWKV_CHUNK = 64
RWKV_SECTIONS = 2
RWKV_SECTION_CHUNKS = 4
RWKV_SECTION_LAG = 12
RWKV_CHUNKS_PER_STEP = RWKV_SECTIONS * RWKV_SECTION_CHUNKS
SB_BLOCK = 128
INPROJ_ROWS = 256
MLP_ROWS = 512
VMEM_LIMIT = 56 * 1024 * 1024


def _split2(a):
    hi = a.astype(BF16)
    lo = (a - hi.astype(F32)).astype(BF16)
    return hi, lo


_NN = (((1,), (0,)), ((), ()))
_NT = (((1,), (1,)), ((), ()))
_TN = (((0,), (0,)), ((), ()))


def _dot(a, b, dims=_NN):
    return lax.dot_general(a, b, dims, preferred_element_type=F32)


def _dot3(a, b, dims=_NN):
    ah, al = _split2(a)
    bh, bl = _split2(b)
    return _dot(ah, bh, dims) + (_dot(al, bh, dims) + _dot(ah, bl, dims))


def _head_sums(x, ones_blk):
    m, w = x.shape[0], ones_blk.shape[0]
    hi, lo = _split2(x)
    outs = []
    for g in range(x.shape[1] // w):
        cols = slice(g * w, (g + 1) * w)
        both = _dot(jnp.concatenate([hi[:, cols], lo[:, cols]], axis=0), ones_blk)
        outs.append(both[:m] + both[m:])
    return jnp.concatenate(outs, axis=1)


def _dot_ones_lhs(a_bf, b):
    hi, lo = _split2(b)
    return _dot(a_bf, hi) + _dot(a_bf, lo)


_DONE = object()


def _run_skewed(programs, lag):
    live = [True] * len(programs)
    tick = 0
    while any(live):
        for i, prog in enumerate(programs):
            if live[i] and tick >= i * lag:
                live[i] = next(prog, _DONE) is not _DONE
        tick += 1


_NEG_LOG2_E = -1.4426950408889634


def _softplus(y):
    return jnp.maximum(y, 0.0) + jnp.log(1.0 + jnp.exp2(jnp.abs(y) * _NEG_LOG2_E))


def _sigmoid(y):
    return 1.0 / (1.0 + jnp.exp(-y))


def _norm_rows(x, eps):
    mu = jnp.mean(x, axis=-1, keepdims=True)
    xc = x - mu
    var = jnp.mean(xc * xc, axis=-1, keepdims=True)
    return xc * lax.rsqrt(var + eps)


def _ada_kernel(c_ref, w_ref, b_ref, o_ref):
    c = c_ref[...]
    s = c * _sigmoid(c)
    o_ref[...] = jnp.dot(s, w_ref[...], preferred_element_type=F32,
                         precision=lax.Precision.HIGHEST) + b_ref[...]


def _ada(c, w_ada, b_ada):
    bsz, d = c.shape
    n = w_ada.shape[1]
    return pl.pallas_call(
        _ada_kernel,
        grid=(n // d,),
        in_specs=[pl.BlockSpec((bsz, d), lambda j: (0, 0)),
                  pl.BlockSpec((d, d), lambda j: (0, j)),
                  pl.BlockSpec((1, d), lambda j: (0, j))],
        out_specs=pl.BlockSpec((bsz, d), lambda j: (0, j)),
        out_shape=jax.ShapeDtypeStruct((bsz, n), F32),
        name="ada",
    )(c, w_ada, b_ada.reshape(1, n))


def _inproj_kernel(x_ref, ada_ref, win_ref, mu_ref, w0_ref, w2_ref, a0_ref, a2_ref, g2_ref,
                   kkw_ref, kaw_ref, hsum_ref,
                   r_ref, k_ref, v_ref, lw_ref, kk_ref, b_ref, g_ref,
                   q_ref, ks_ref, vs_ref, ga_ref, gb_ref,
                   carry_ref, *, tiles_per_batch, d_rwkv, d_decay, d_aaa, d_gate, d_sb, d_model):
    i = pl.program_id(0)
    tm = x_ref.shape[0]
    d_shift = 3 * d_rwkv + d_decay + d_aaa + d_gate

    sh = ada_ref[0, 0:1, :]
    sc = ada_ref[0, 1:2, :]
    h = _norm_rows(x_ref[...], ADALN_EPS) * (1.0 + sc) + sh
    hb = h.astype(BF16)

    @pl.when(i % tiles_per_batch == 0)
    def _():
        carry_ref[...] = jnp.zeros_like(carry_ref)

    row0 = lax.broadcasted_iota(jnp.int32, (tm, 1), 0) == 0

    def shifted(lo, width):
        cols = slice(lo, lo + width)
        z = _dot(hb, win_ref[:, cols])
        prev = jnp.where(row0, carry_ref[:, cols], pltpu.roll(z, 1, 0))
        carry_ref[:, cols] = z[tm - 1:tm, :]
        return z + mu_ref[:, cols] * (prev - z)

    half = d_model // 2
    pair_w = 2 * HEAD_DIM
    first_head = lax.broadcasted_iota(jnp.int32, (1, pair_w), 1) < HEAD_DIM

    def store_queries(z):
        zq = (z * (HEAD_DIM ** -0.5)).astype(BF16)
        zero = jnp.zeros((), BF16)
        for p in range(d_sb // pair_w):
            qp = zq[:, p * pair_w:(p + 1) * pair_w]
            q_ref[:, (2 * p) * pair_w:(2 * p + 1) * pair_w] = jnp.where(first_head, qp, zero)
            q_ref[:, (2 * p + 1) * pair_w:(2 * p + 2) * pair_w] = jnp.where(first_head, zero, qp)

    def plain(ref, cols, fn=lambda z: z):
        def store(z):
            ref[:, cols] = fn(z).astype(BF16)
        return store

    rest = [(store_queries, d_sb), (plain(ks_ref, slice(None)), d_sb), (plain(vs_ref, slice(None)), d_sb),
            (plain(ga_ref, slice(0, half), _sigmoid), half), (plain(ga_ref, slice(half, d_model), _sigmoid), half),
            (plain(gb_ref, slice(0, half), _sigmoid), half), (plain(gb_ref, slice(half, d_model), _sigmoid), half)]
    rest_col = [d_shift]

    def project_next():
        store, width = rest.pop(0)
        store(_dot(hb, win_ref[:, rest_col[0]:rest_col[0] + width]))
        rest_col[0] += width

    zl = shifted(3 * d_rwkv, d_decay + d_aaa + d_gate)
    zw = zl[:, :d_decay]
    za = zl[:, d_decay:d_decay + d_aaa]
    zg = zl[:, d_decay + d_aaa:]
    k = shifted(d_rwkv, d_rwkv)
    ww = w0_ref[...] + _dot(jnp.tanh(zw).astype(BF16), w2_ref[...])
    w_log = -_softplus(-ww) - 0.5
    lw_ref[...] = -jnp.exp(w_log)
    project_next()
    arate = _sigmoid(a0_ref[...] + _dot(za.astype(BF16), a2_ref[...]))
    g_ref[...] = _dot(_sigmoid(zg).astype(BF16), g2_ref[...])
    r_ref[...] = shifted(0, d_rwkv)
    kkraw = k * kkw_ref[...]
    ssq = _head_sums(kkraw * kkraw, hsum_ref[...])
    kk = kkraw / jnp.maximum(jnp.sqrt(ssq), 1e-12)
    v_ref[...] = shifted(2 * d_rwkv, d_rwkv)
    k_ref[...] = k * (1.0 + (arate - 1.0) * kaw_ref[...])
    project_next()
    kk_ref[...] = kk
    b_ref[...] = kk * arate
    while rest:
        project_next()


def _inproj(x2, ada3, win_bf, mu_shift, rw_w0, rw_w2, rw_a0, rw_a2, rw_g2, rw_kk, rw_ka, hsum,
            *, seq, tm):
    m, d_model = x2.shape
    d_rwkv = rw_w0.shape[-1]
    d_decay, d_aaa, d_gate = rw_w2.shape[0], rw_a2.shape[0], rw_g2.shape[0]
    d_shift = 3 * d_rwkv + d_decay + d_aaa + d_gate
    d_sb = (win_bf.shape[1] - d_shift - 2 * d_model) // 3
    tiles_per_batch = seq // tm
    row = lambda i: (i, 0)
    const = lambda i: (0, 0)
    full = lambda a: pl.BlockSpec(a.shape, const, pipeline_mode=pl.Buffered(1))
    vec = lambda a: a.reshape(1, -1)
    ins = [x2, ada3, win_bf, vec(mu_shift), vec(rw_w0), rw_w2.astype(BF16), vec(rw_a0),
           rw_a2.astype(BF16), rw_g2.astype(BF16), vec(rw_kk), vec(rw_ka), hsum]
    in_specs = [pl.BlockSpec((tm, d_model), row),
                pl.BlockSpec((1,) + ada3.shape[1:], lambda i: (i // tiles_per_batch, 0, 0))]
    in_specs += [full(a) for a in ins[2:]]
    widths = [d_rwkv] * 7 + [2 * d_sb, d_sb, d_sb] + [d_model] * 2
    dtypes = [F32] * 7 + [BF16] * 5
    out_shape = [jax.ShapeDtypeStruct((m, w), dt) for w, dt in zip(widths, dtypes)]
    out_specs = [pl.BlockSpec((tm, w), row) for w in widths]
    kern = functools.partial(_inproj_kernel, tiles_per_batch=tiles_per_batch, d_rwkv=d_rwkv,
                             d_decay=d_decay, d_aaa=d_aaa, d_gate=d_gate, d_sb=d_sb, d_model=d_model)
    return pl.pallas_call(
        kern,
        grid=(m // tm,),
        in_specs=in_specs,
        out_specs=out_specs,
        out_shape=out_shape,
        scratch_shapes=[pltpu.VMEM((1, d_shift), F32)],
        compiler_params=pltpu.CompilerParams(dimension_semantics=("arbitrary",),
                                             vmem_limit_bytes=VMEM_LIMIT),
        name="inproj",
    )(*ins)


RWKV_GROUP_HEADS = 4


def _block_diag(q, head_masks):
    zero = jnp.zeros((), q.dtype)
    return jnp.concatenate([jnp.where(m, q, zero) for m in head_masks], axis=0)


def _packed_dot1(lhs, q, head_masks, dims=_NN):
    return _dot(lhs.astype(BF16), _block_diag(q.astype(BF16), head_masks), dims)


def _packed_dot_wide_lhs(lhs, q, head_masks, dims=_NN):
    m = lhs.shape[0]
    lh, ll = _split2(lhs)
    both = _dot(jnp.concatenate([lh, ll], axis=0), _block_diag(q.astype(BF16), head_masks), dims)
    return both[:m] + both[m:]


def _rwkv_kernel(r_ref, k_ref, v_ref, lw_ref, kk_ref, b_ref, g_ref, rk_ref, gng_ref, gnb_ref, hsum_ref,
                 o_ref, state_ref, *, chunk):
    c = pl.program_id(1)
    n = HEAD_DIM
    rows, d = r_ref.shape[1], r_ref.shape[2]
    gw = RWKV_GROUP_HEADS * n
    n_groups = d // gw

    @pl.when(c == 0)
    def _():
        state_ref[...] = jnp.zeros_like(state_ref)

    t_idx = lax.broadcasted_iota(jnp.int32, (chunk, gw), 0)
    lane = lax.broadcasted_iota(jnp.int32, (chunk, gw), 1)
    i_idx = lane & (n - 1)
    strict = t_idx > i_idx
    incl = t_idx >= i_idx
    lane1 = lax.broadcasted_iota(jnp.int32, (1, gw), 1)
    hm = [(lane1 >> (n.bit_length() - 1)) == h for h in range(RWKV_GROUP_HEADS)]
    same = lambda s: (t_idx >> s) == (i_idx >> s)

    half_w = gw // 2
    lane_half = [(lane1 >> (half_w.bit_length() - 1)) == j for j in range(2)]
    second_in_half = ((lane >> (n.bit_length() - 1)) & 1) == 1

    def halves_rows(x):
        return jnp.concatenate([x[:, :half_w], x[:, half_w:]], axis=0).astype(BF16)

    def halves_masked(w):
        return _block_diag(w.astype(BF16), lane_half)

    def same_head(out):
        return jnp.where(second_in_half, out[n:], out[:n])

    sec_rows = rows // RWKV_SECTIONS
    sec_chunks = sec_rows // chunk
    rr = lax.broadcasted_iota(jnp.int32, (sec_rows, sec_rows), 0)
    cc = lax.broadcasted_iota(jnp.int32, (sec_rows, sec_rows), 1)
    cs = chunk.bit_length() - 1
    ltri = jnp.where((rr >= cc) & ((rr >> cs) == (cc >> cs)), 1.0, 0.0).astype(BF16)
    eye = jnp.where(t_idx == i_idx, 1.0, 0.0)
    chains = [(ci, gi) for ci in range(sec_chunks) for gi in range(n_groups)]
    win = lambda x, ci, gi: x[ci * chunk:(ci + 1) * chunk, gi * gw:(gi + 1) * gw]

    def section(row0):
        rs = slice(row0, row0 + sec_rows)
        r, k, v, lw = r_ref[0, rs, :], k_ref[0, rs, :], v_ref[0, rs, :], lw_ref[0, rs, :]
        kk, b = kk_ref[0, rs, :], b_ref[0, rs, :]
        cum = _dot_ones_lhs(ltri, lw)
        a_t = -kk * jnp.exp(cum - lw)
        r_t = r * jnp.exp(cum)
        e_neg = jnp.exp(-cum)
        b_t = b * e_neg
        k_t = k * e_neg
        tots = [cum[(ci + 1) * chunk - 1:(ci + 1) * chunk, :] for ci in range(sec_chunks)]
        e_bars = [jnp.exp(tots[ci] - cum[ci * chunk:(ci + 1) * chunk, :]) for ci in range(sec_chunks)]
        b_bar = [win(b, ci, gi) * e_bars[ci][:, gi * gw:(gi + 1) * gw] for ci, gi in chains]
        k_bar = [win(k, ci, gi) * e_bars[ci][:, gi * gw:(gi + 1) * gw] for ci, gi in chains]
        at = [win(a_t, ci, gi) for ci, gi in chains]
        rt = [win(r_t, ci, gi) for ci, gi in chains]
        vv = [win(v, ci, gi) for ci, gi in chains]
        yield

        ar = [jnp.concatenate([a_, r_], axis=0) for a_, r_ in zip(at, rt)]
        gram_b = [_packed_dot1(x, win(b_t, ci, gi), hm, _NT) for x, (ci, gi) in zip(ar, chains)]
        yield
        gram_k = [_packed_dot1(x, win(k_t, ci, gi), hm, _NT) for x, (ci, gi) in zip(ar, chains)]
        m_ab = [jnp.where(strict, g_[:chunk], 0.0) for g_ in gram_b]
        p_rb = [jnp.where(incl, g_[chunk:], 0.0) for g_ in gram_b]
        m_ak = [jnp.where(strict, g_[:chunk], 0.0) for g_ in gram_k]
        p_rk = [jnp.where(incl, g_[chunk:], 0.0) for g_ in gram_k]
        yield

        ts = [eye + jnp.where(same(1), m, 0.0) for m in m_ab]
        s = 1
        while (1 << s) < chunk:
            level = same(s + 1) & jnp.logical_not(same(s))
            offs = [jnp.where(level, m, 0.0) for m in m_ab]
            mids = [_packed_dot1(t, o, hm) for t, o in zip(ts, offs)]
            yield
            ts = [t + _packed_dot1(md, t, hm) for t, md in zip(ts, mids)]
            yield
            s += 1

        mv = [_packed_dot1(m, x, hm) for m, x in zip(m_ak, vv)]
        a_hat = [_packed_dot1(t, x, hm) for t, x in zip(ts, at)]
        yield
        u0 = [_packed_dot1(t, x, hm) for t, x in zip(ts, mv)]
        r_hat = [x + _packed_dot1(p, a, hm) for x, p, a in zip(rt, p_rb, a_hat)]
        yield
        y0 = [_packed_dot1(pk, x, hm) + _packed_dot1(pb, u, hm) for pk, x, pb, u in zip(p_rk, vv, p_rb, u0)]
        yield
        g_mat = [same_head(_dot(halves_rows(a), halves_masked(bb), _TN))
                 for a, bb in zip(a_hat, b_bar)]
        h_mat = [same_head(_dot(jnp.concatenate([halves_rows(u), halves_rows(x)], axis=0),
                                jnp.concatenate([halves_masked(bb), halves_masked(kb)], axis=0), _TN))
                 for u, x, bb, kb in zip(u0, vv, b_bar, k_bar)]
        yield

        y_rows = []
        for ci in range(sec_chunks):
            y_groups = []
            for gi in range(n_groups):
                j = ci * n_groups + gi
                s0 = state_ref[gi]
                y_groups.append(_packed_dot1(r_hat[j], s0, hm, _NT) + y0[j])
                w_tot = jnp.exp(tots[ci][:, gi * gw:(gi + 1) * gw])
                state_ref[gi] = s0 * w_tot + _packed_dot_wide_lhs(s0, g_mat[j], hm) + h_mat[j]
            y_rows.append(jnp.concatenate(y_groups, axis=1))
            yield
        y = jnp.concatenate(y_rows, axis=0)

        hsum = hsum_ref[...]
        inv_n = 1.0 / n
        mu = _head_sums(y, hsum) * inv_n
        yc = y - mu
        var = _head_sums(yc * yc, hsum) * inv_n
        yn = yc * lax.rsqrt(var + GN_EPS) * gng_ref[...] + gnb_ref[...]
        bonus = _head_sums(r * k * rk_ref[...], hsum)
        o_ref[0, rs, :] = ((yn + bonus * v) * g_ref[0, rs, :]).astype(o_ref.dtype)

    assert RWKV_SECTION_LAG >= sec_chunks
    _run_skewed([section(si * sec_rows) for si in range(RWKV_SECTIONS)], lag=RWKV_SECTION_LAG)


def _rwkv(r, k, v, lw, kk, b, g, rw_rk, gn_g, gn_b, hsum, *, chunk, rows):
    bsz, seq, d = r.shape
    n_groups = d // (RWKV_GROUP_HEADS * HEAD_DIM)
    blk = pl.BlockSpec((1, rows, d), lambda bi, ci: (bi, ci, 0))
    const = lambda bi, ci: (0, 0)
    vec = lambda a: a.reshape(1, d)
    return pl.pallas_call(
        functools.partial(_rwkv_kernel, chunk=chunk),
        grid=(bsz, seq // rows),
        in_specs=[blk] * 7 + [pl.BlockSpec((1, d), const)] * 3 + [pl.BlockSpec(hsum.shape, const)],
        out_specs=blk,
        out_shape=jax.ShapeDtypeStruct((bsz, seq, d), BF16),
        scratch_shapes=[pltpu.VMEM((n_groups, HEAD_DIM, RWKV_GROUP_HEADS * HEAD_DIM), F32)],
        compiler_params=pltpu.CompilerParams(dimension_semantics=("arbitrary", "arbitrary"),
                                             vmem_limit_bytes=VMEM_LIMIT),
        name="rwkv",
    )(r, k, v, lw, kk, b, g, vec(rw_rk), vec(gn_g), vec(gn_b), hsum)


SB_LOG_UNDERFLOW = 105.0


def _sb_kernel(q_ref, k_ref, v_ref, o_ref, acc_ref, run_ref, *, blk):
    qi = pl.program_id(1)
    pairs = k_ref.shape[2] // (2 * HEAD_DIM)
    row = lax.broadcasted_iota(jnp.int32, (blk, blk), 0)
    col = lax.broadcasted_iota(jnp.int32, (blk, blk), 1)
    causal = col < row
    jj = lax.broadcasted_iota(jnp.int32, (blk, 2 * blk), 0)
    ss = lax.broadcasted_iota(jnp.int32, (blk, 2 * blk), 1)
    suffix = jnp.where((jj >= ss) | (ss >= blk), 1.0, 0.0).astype(BF16)
    first_head = lax.broadcasted_iota(jnp.int32, (1, 2 * HEAD_DIM), 1) < HEAD_DIM
    zero = jnp.zeros((), BF16)

    lanes = [slice(p * 2 * HEAD_DIM, (p + 1) * 2 * HEAD_DIM) for p in range(pairs)]
    heads = [(p, hh) for p in range(pairs) for hh in range(2)]

    def tile(kb, diagonal):
        start = pl.multiple_of(kb * blk, blk)
        zs = []
        for n_, (p, hh) in enumerate(heads):
            zs.append(_dot(q_ref[0, :, n_ * 2 * HEAD_DIM:(n_ + 1) * 2 * HEAD_DIM],
                           k_ref[0, pl.ds(start, blk), lanes[p]], _NT))
        sps = [_softplus(z) for z in zs]
        if diagonal:
            sps = [jnp.where(causal, sp, 0.0) for sp in sps]
        css = [_dot(sp.astype(BF16), suffix) for sp in sps]
        atts = []
        low = None
        for n_, (z, cs) in enumerate(zip(zs, css)):
            logw = z - cs[:, :blk]
            if diagonal:
                att = jnp.where(causal, jnp.exp(logw), 0.0)
                run = cs[:, blk:]
            else:
                prev = run_ref[n_]
                att = jnp.exp(logw - prev)
                run = prev + cs[:, blk:]
            run_ref[n_] = run
            low = run if low is None else jnp.minimum(low, run)
            atts.append(att.astype(BF16))
        for p in range(pairs):
            vp = v_ref[0, pl.ds(start, blk), lanes[p]]
            vv = jnp.concatenate([jnp.where(first_head, vp, zero),
                                  jnp.where(first_head, zero, vp)], axis=0)
            out = _dot(jnp.concatenate(atts[2 * p:2 * p + 2], axis=1), vv)
            if diagonal:
                acc_ref[:, lanes[p]] = out
            else:
                acc_ref[:, lanes[p]] += out
        return (jnp.min(low) > SB_LOG_UNDERFLOW).astype(jnp.int32)

    done0 = tile(qi, True)

    def cond(c):
        i, done = c
        return jnp.logical_and(i <= qi, done == 0)

    def body(c):
        i, _ = c
        return i + 1, tile(qi - i, False)

    lax.while_loop(cond, body, (jnp.int32(1), done0))
    o_ref[0] = acc_ref[...].astype(o_ref.dtype)


def _sb_attention(q, k, v, *, blk):
    bsz, seq, d = k.shape
    heads = d // HEAD_DIM
    qspec = pl.BlockSpec((1, blk, 2 * d), lambda b, i: (b, i, 0))
    ospec = pl.BlockSpec((1, blk, d), lambda b, i: (b, i, 0))
    kvspec = pl.BlockSpec((1, seq, d), lambda b, i: (b, 0, 0), pipeline_mode=pl.Buffered(1))
    return pl.pallas_call(
        functools.partial(_sb_kernel, blk=blk),
        grid=(bsz, seq // blk),
        in_specs=[qspec, kvspec, kvspec],
        out_specs=ospec,
        out_shape=jax.ShapeDtypeStruct((bsz, seq, d), BF16),
        scratch_shapes=[pltpu.VMEM((blk, d), F32), pltpu.VMEM((heads, blk, blk), F32)],
        compiler_params=pltpu.CompilerParams(dimension_semantics=("arbitrary", "arbitrary"),
                                             vmem_limit_bytes=VMEM_LIMIT),
        name="sbattn",
    )(q, k, v)


def _merge_kernel(x_ref, ada_ref, ya_ref, ob_ref, ga_ref, gb_ref, wa_ref, wb_ref, wo_ref,
                  lng_ref, lnb_ref, o_ref, *, alpha):
    g1 = ada_ref[0, 2:3, :]
    half = x_ref.shape[0] // 2

    def rows_program(rows):
        y_a = _dot(ya_ref[rows, :], wa_ref[...])
        y_b = _dot(ob_ref[rows, :], wb_ref[...])
        yield
        merged = ga_ref[rows, :].astype(F32) * y_a + gb_ref[rows, :].astype(F32) * y_b
        mix = _dot(merged.astype(BF16), wo_ref[...])
        yield
        o_ref[rows, :] = _norm_rows(alpha * x_ref[rows, :] + g1 * mix, LN_EPS) * lng_ref[...] + lnb_ref[...]

    _run_skewed([rows_program(slice(0, half)), rows_program(slice(half, 2 * half))], lag=1)


def _merge(x2, ada3, ya, ob, ga, gb, wa_bf, wb_bf, wo_bf, ln_g, ln_b, *, seq, tm, alpha):
    m, d = x2.shape
    tiles_per_batch = seq // tm
    row = lambda i: (i, 0)
    const = lambda i: (0, 0)
    rows = lambda a: pl.BlockSpec((tm, a.shape[1]), row)
    full = lambda a: pl.BlockSpec(a.shape, const, pipeline_mode=pl.Buffered(1))
    vec = lambda a: a.reshape(1, -1)
    ins = [x2, ada3, ya, ob, ga, gb, wa_bf, wb_bf, wo_bf, vec(ln_g), vec(ln_b)]
    in_specs = [rows(x2), pl.BlockSpec((1,) + ada3.shape[1:], lambda i: (i // tiles_per_batch, 0, 0)),
                rows(ya), rows(ob), rows(ga), rows(gb)] + [full(a) for a in ins[6:]]
    return pl.pallas_call(
        functools.partial(_merge_kernel, alpha=alpha),
        grid=(m // tm,),
        in_specs=in_specs,
        out_specs=pl.BlockSpec((tm, d), row),
        out_shape=jax.ShapeDtypeStruct((m, d), F32),
        compiler_params=pltpu.CompilerParams(dimension_semantics=("arbitrary",),
                                             vmem_limit_bytes=VMEM_LIMIT),
        name="merge",
    )(*ins)


def _ffn_kernel(x_ref, ada_ref, w1_ref, b1_ref, w2_ref, b2_ref, lng_ref, lnb_ref, o_ref, *, alpha, fchunk):
    sh = ada_ref[0, 3:4, :]
    sc = ada_ref[0, 4:5, :]
    g2 = ada_ref[0, 5:6, :]
    d_ff = w1_ref.shape[1]
    n_chunks = d_ff // fchunk
    half = x_ref.shape[0] // 2

    def rows_program(rows):
        x = x_ref[rows, :]
        hb = (_norm_rows(x, ADALN_EPS) * (1.0 + sc) + sh).astype(BF16)
        yield
        ff = None
        for j in range(n_chunks):
            sl = slice(j * fchunk, (j + 1) * fchunk)
            t = jnp.maximum(_dot(hb, w1_ref[:, sl]) + b1_ref[:, sl], 0.0)
            part = _dot((t * t).astype(BF16), w2_ref[sl, :])
            ff = part if ff is None else ff + part
            yield
        ff = ff + b2_ref[...]
        o_ref[rows, :] = _norm_rows(alpha * x + g2 * ff, LN_EPS) * lng_ref[...] + lnb_ref[...]

    _run_skewed([rows_program(slice(0, half)), rows_program(slice(half, 2 * half))], lag=2)


def _ffn(x1, ada3, w1_bf, b1, w2_bf, b2, ln_g, ln_b, *, seq, tm, alpha, fchunk):
    m, d = x1.shape
    tiles_per_batch = seq // tm
    row = lambda i: (i, 0)
    const = lambda i: (0, 0)
    full = lambda a: pl.BlockSpec(a.shape, const, pipeline_mode=pl.Buffered(1))
    vec = lambda a: a.reshape(1, -1)
    ins = [x1, ada3, w1_bf, vec(b1), w2_bf, vec(b2), vec(ln_g), vec(ln_b)]
    in_specs = [pl.BlockSpec((tm, d), row),
                pl.BlockSpec((1,) + ada3.shape[1:], lambda i: (i // tiles_per_batch, 0, 0))]
    in_specs += [full(a) for a in ins[2:]]
    return pl.pallas_call(
        functools.partial(_ffn_kernel, alpha=alpha, fchunk=fchunk),
        grid=(m // tm,),
        in_specs=in_specs,
        out_specs=pl.BlockSpec((tm, d), row),
        out_shape=jax.ShapeDtypeStruct((m, d), F32),
        compiler_params=pltpu.CompilerParams(dimension_semantics=("arbitrary",),
                                             vmem_limit_bytes=VMEM_LIMIT),
        name="ffn",
    )(*ins)


def _branches(proj, rw_rk, rw_gn_g, rw_gn_b, hsum, bsz, seq):
    r, k, v, lw, kk, b, g, q, ks, vs = proj
    seq3 = lambda a: a.reshape(bsz, seq, -1)
    ya = _rwkv(seq3(r), seq3(k), seq3(v), seq3(lw), seq3(kk), seq3(b), seq3(g),
               rw_rk, rw_gn_g, rw_gn_b, hsum, chunk=min(RWKV_CHUNK, seq),
               rows=min(RWKV_CHUNKS_PER_STEP * RWKV_CHUNK, seq))
    o = _sb_attention(seq3(q), seq3(ks), seq3(vs), blk=min(SB_BLOCK, seq))
    return ya.reshape(bsz * seq, -1), o.reshape(bsz * seq, -1)


def _layer(x, c, w_ada, b_ada, w_in, mu_shift, rw_w0, rw_w2, rw_a0, rw_a2, rw_g2, rw_kk, rw_ka,
           rw_rk, rw_gn_g, rw_gn_b, w_branch_a, w_branch_b, w_out, ln1_g, ln1_b,
           w_ff1, b_ff1, w_ff2, b_ff2, ln2_g, ln2_b, *, alpha):
    bsz, seq, d = x.shape
    d_rwkv = rw_w0.shape[-1]
    tm = min(INPROJ_ROWS, seq)
    tm2 = min(MLP_ROWS, seq)

    ada3 = _ada(c, w_ada, b_ada).reshape(bsz, 6, d)
    lane = jnp.arange(RWKV_GROUP_HEADS * HEAD_DIM) // HEAD_DIM
    hsum = (lane[:, None] == lane[None, :]).astype(BF16)

    x2 = x.reshape(bsz * seq, d)
    (r, k, v, lw, kk, b, g, q, ks, vs, ga, gb) = _inproj(
        x2, ada3, w_in.astype(BF16), mu_shift, rw_w0, rw_w2, rw_a0, rw_a2, rw_g2, rw_kk, rw_ka, hsum,
        seq=seq, tm=tm)

    ya, ob = _branches((r, k, v, lw, kk, b, g, q, ks, vs), rw_rk, rw_gn_g, rw_gn_b, hsum, bsz, seq)

    x1 = _merge(x2, ada3, ya, ob, ga, gb,
                w_branch_a.astype(BF16), w_branch_b.astype(BF16), w_out.astype(BF16),
                ln1_g, ln1_b, seq=seq, tm=tm2, alpha=alpha)
    out = _ffn(x1, ada3, w_ff1.astype(BF16), b_ff1, w_ff2.astype(BF16), b_ff2, ln2_g, ln2_b,
               seq=seq, tm=tm2, alpha=alpha, fchunk=min(1024, w_ff1.shape[-1]))
    return out.reshape(bsz, seq, d)


def kernel(x, c, w_ada, b_ada, w_in, mu_shift, rw_w0, rw_w2, rw_a0, rw_a2, rw_g2, rw_kk, rw_ka, rw_rk,
           rw_gn_g, rw_gn_b, w_branch_a, w_branch_b, w_out, ln1_g, ln1_b, w_ff1, b_ff1, w_ff2, b_ff2,
           ln2_g, ln2_b):
    in_dtype = x.dtype
    layer_params = (w_ada, b_ada, w_in, mu_shift, rw_w0, rw_w2, rw_a0, rw_a2, rw_g2, rw_kk, rw_ka,
                    rw_rk, rw_gn_g, rw_gn_b, w_branch_a, w_branch_b, w_out, ln1_g, ln1_b,
                    w_ff1, b_ff1, w_ff2, b_ff2, ln2_g, ln2_b)
    depth = w_ada.shape[0]
    alpha = (2.0 * depth) ** 0.25
    for l in range(depth):
        x = _layer(x, c, *[p[l] for p in layer_params], alpha=alpha)
    return x.astype(in_dtype)
```

```python
import functools

import jax
import jax.numpy as jnp
from jax import lax
from jax.experimental import pallas as pl
from jax.experimental.pallas import tpu as pltpu

F32 = jnp.float32
BF16 = jnp.bfloat16

HEAD_DIM = 64
LN_EPS = 1e-5
ADALN_EPS = 1e-6
GN_EPS = 64e-5
RWKV_CHUNK = 64
RWKV_SECTIONS = 2
RWKV_SECTION_CHUNKS = 4
RWKV_SECTION_LAG = 12
RWKV_CHUNKS_PER_STEP = RWKV_SECTIONS * RWKV_SECTION_CHUNKS
SB_BLOCK = 128
INPROJ_ROWS = 256
MLP_ROWS = 512
VMEM_LIMIT = 56 * 1024 * 1024


def _split2(a):
    hi = a.astype(BF16)
    lo = (a - hi.astype(F32)).astype(BF16)
    return hi, lo


_NN = (((1,), (0,)), ((), ()))
_NT = (((1,), (1,)), ((), ()))
_TN = (((0,), (0,)), ((), ()))


def _dot(a, b, dims=_NN):
    return lax.dot_general(a, b, dims, preferred_element_type=F32)


def _dot3(a, b, dims=_NN):
    ah, al = _split2(a)
    bh, bl = _split2(b)
    return _dot(ah, bh, dims) + (_dot(al, bh, dims) + _dot(ah, bl, dims))


def _head_sums(x, ones_blk):
    m, w = x.shape[0], ones_blk.shape[0]
    hi, lo = _split2(x)
    outs = []
    for g in range(x.shape[1] // w):
        cols = slice(g * w, (g + 1) * w)
        both = _dot(jnp.concatenate([hi[:, cols], lo[:, cols]], axis=0), ones_blk)
        outs.append(both[:m] + both[m:])
    return jnp.concatenate(outs, axis=1)


def _dot_ones_lhs(a_bf, b):
    hi, lo = _split2(b)
    return _dot(a_bf, hi) + _dot(a_bf, lo)


_DONE = object()


def _run_skewed(programs, lag):
    live = [True] * len(programs)
    tick = 0
    while any(live):
        for i, prog in enumerate(programs):
            if live[i] and tick >= i * lag:
                live[i] = next(prog, _DONE) is not _DONE
        tick += 1


_NEG_LOG2_E = -1.4426950408889634


def _softplus(y):
    return jnp.maximum(y, 0.0) + jnp.log(1.0 + jnp.exp2(jnp.abs(y) * _NEG_LOG2_E))


def _sigmoid(y):
    return 1.0 / (1.0 + jnp.exp(-y))


def _norm_rows(x, eps):
    mu = jnp.mean(x, axis=-1, keepdims=True)
    xc = x - mu
    var = jnp.mean(xc * xc, axis=-1, keepdims=True)
    return xc * lax.rsqrt(var + eps)


def _ada_kernel(c_ref, w_ref, b_ref, o_ref):
    c = c_ref[...]
    s = c * _sigmoid(c)
    o_ref[...] = jnp.dot(s, w_ref[...], preferred_element_type=F32,
                         precision=lax.Precision.HIGHEST) + b_ref[...]


def _ada(c, w_ada, b_ada):
    bsz, d = c.shape
    n = w_ada.shape[1]
    return pl.pallas_call(
        _ada_kernel,
        grid=(n // d,),
        in_specs=[pl.BlockSpec((bsz, d), lambda j: (0, 0)),
                  pl.BlockSpec((d, d), lambda j: (0, j)),
                  pl.BlockSpec((1, d), lambda j: (0, j))],
        out_specs=pl.BlockSpec((bsz, d), lambda j: (0, j)),
        out_shape=jax.ShapeDtypeStruct((bsz, n), F32),
        name="ada",
    )(c, w_ada, b_ada.reshape(1, n))


def _inproj_kernel(x_ref, ada_ref, win_ref, mu_ref, w0_ref, w2_ref, a0_ref, a2_ref, g2_ref,
                   kkw_ref, kaw_ref, hsum_ref,
                   r_ref, k_ref, v_ref, lw_ref, kk_ref, b_ref, g_ref,
                   q_ref, ks_ref, vs_ref, ga_ref, gb_ref,
                   carry_ref, *, tiles_per_batch, d_rwkv, d_decay, d_aaa, d_gate, d_sb, d_model):
    i = pl.program_id(0)
    tm = x_ref.shape[0]
    d_shift = 3 * d_rwkv + d_decay + d_aaa + d_gate

    sh = ada_ref[0, 0:1, :]
    sc = ada_ref[0, 1:2, :]
    h = _norm_rows(x_ref[...], ADALN_EPS) * (1.0 + sc) + sh
    hb = h.astype(BF16)

    @pl.when(i % tiles_per_batch == 0)
    def _():
        carry_ref[...] = jnp.zeros_like(carry_ref)

    row0 = lax.broadcasted_iota(jnp.int32, (tm, 1), 0) == 0

    def shifted(lo, width):
        cols = slice(lo, lo + width)
        z = _dot(hb, win_ref[:, cols])
        prev = jnp.where(row0, carry_ref[:, cols], pltpu.roll(z, 1, 0))
        carry_ref[:, cols] = z[tm - 1:tm, :]
        return z + mu_ref[:, cols] * (prev - z)

    half = d_model // 2
    pair_w = 2 * HEAD_DIM
    first_head = lax.broadcasted_iota(jnp.int32, (1, pair_w), 1) < HEAD_DIM

    def store_queries(z):
        zq = (z * (HEAD_DIM ** -0.5)).astype(BF16)
        zero = jnp.zeros((), BF16)
        for p in range(d_sb // pair_w):
            qp = zq[:, p * pair_w:(p + 1) * pair_w]
            q_ref[:, (2 * p) * pair_w:(2 * p + 1) * pair_w] = jnp.where(first_head, qp, zero)
            q_ref[:, (2 * p + 1) * pair_w:(2 * p + 2) * pair_w] = jnp.where(first_head, zero, qp)

    def plain(ref, cols, fn=lambda z: z):
        def store(z):
            ref[:, cols] = fn(z).astype(BF16)
        return store

    rest = [(store_queries, d_sb), (plain(ks_ref, slice(None)), d_sb), (plain(vs_ref, slice(None)), d_sb),
            (plain(ga_ref, slice(0, half), _sigmoid), half), (plain(ga_ref, slice(half, d_model), _sigmoid), half),
            (plain(gb_ref, slice(0, half), _sigmoid), half), (plain(gb_ref, slice(half, d_model), _sigmoid), half)]
    rest_col = [d_shift]

    def project_next():
        store, width = rest.pop(0)
        store(_dot(hb, win_ref[:, rest_col[0]:rest_col[0] + width]))
        rest_col[0] += width

    zl = shifted(3 * d_rwkv, d_decay + d_aaa + d_gate)
    zw = zl[:, :d_decay]
    za = zl[:, d_decay:d_decay + d_aaa]
    zg = zl[:, d_decay + d_aaa:]
    k = shifted(d_rwkv, d_rwkv)
    ww = w0_ref[...] + _dot(jnp.tanh(zw).astype(BF16), w2_ref[...])
    w_log = -_softplus(-ww) - 0.5
    lw_ref[...] = -jnp.exp(w_log)
    project_next()
    arate = _sigmoid(a0_ref[...] + _dot(za.astype(BF16), a2_ref[...]))
    g_ref[...] = _dot(_sigmoid(zg).astype(BF16), g2_ref[...])
    r_ref[...] = shifted(0, d_rwkv)
    kkraw = k * kkw_ref[...]
    ssq = _head_sums(kkraw * kkraw, hsum_ref[...])
    kk = kkraw / jnp.maximum(jnp.sqrt(ssq), 1e-12)
    v_ref[...] = shifted(2 * d_rwkv, d_rwkv)
    k_ref[...] = k * (1.0 + (arate - 1.0) * kaw_ref[...])
    project_next()
    kk_ref[...] = kk
    b_ref[...] = kk * arate
    while rest:
        project_next()


def _inproj(x2, ada3, win_bf, mu_shift, rw_w0, rw_w2, rw_a0, rw_a2, rw_g2, rw_kk, rw_ka, hsum,
            *, seq, tm):
    m, d_model = x2.shape
    d_rwkv = rw_w0.shape[-1]
    d_decay, d_aaa, d_gate = rw_w2.shape[0], rw_a2.shape[0], rw_g2.shape[0]
    d_shift = 3 * d_rwkv + d_decay + d_aaa + d_gate
    d_sb = (win_bf.shape[1] - d_shift - 2 * d_model) // 3
    tiles_per_batch = seq // tm
    row = lambda i: (i, 0)
    const = lambda i: (0, 0)
    full = lambda a: pl.BlockSpec(a.shape, const, pipeline_mode=pl.Buffered(1))
    vec = lambda a: a.reshape(1, -1)
    ins = [x2, ada3, win_bf, vec(mu_shift), vec(rw_w0), rw_w2.astype(BF16), vec(rw_a0),
           rw_a2.astype(BF16), rw_g2.astype(BF16), vec(rw_kk), vec(rw_ka), hsum]
    in_specs = [pl.BlockSpec((tm, d_model), row),
                pl.BlockSpec((1,) + ada3.shape[1:], lambda i: (i // tiles_per_batch, 0, 0))]
    in_specs += [full(a) for a in ins[2:]]
    widths = [d_rwkv] * 7 + [2 * d_sb, d_sb, d_sb] + [d_model] * 2
    dtypes = [F32] * 7 + [BF16] * 5
    out_shape = [jax.ShapeDtypeStruct((m, w), dt) for w, dt in zip(widths, dtypes)]
    out_specs = [pl.BlockSpec((tm, w), row) for w in widths]
    kern = functools.partial(_inproj_kernel, tiles_per_batch=tiles_per_batch, d_rwkv=d_rwkv,
                             d_decay=d_decay, d_aaa=d_aaa, d_gate=d_gate, d_sb=d_sb, d_model=d_model)
    return pl.pallas_call(
        kern,
        grid=(m // tm,),
        in_specs=in_specs,
        out_specs=out_specs,
        out_shape=out_shape,
        scratch_shapes=[pltpu.VMEM((1, d_shift), F32)],
        compiler_params=pltpu.CompilerParams(dimension_semantics=("arbitrary",),
                                             vmem_limit_bytes=VMEM_LIMIT),
        name="inproj",
    )(*ins)


RWKV_GROUP_HEADS = 4


def _block_diag(q, head_masks):
    zero = jnp.zeros((), q.dtype)
    return jnp.concatenate([jnp.where(m, q, zero) for m in head_masks], axis=0)


def _packed_dot1(lhs, q, head_masks, dims=_NN):
    return _dot(lhs.astype(BF16), _block_diag(q.astype(BF16), head_masks), dims)


def _packed_dot_wide_lhs(lhs, q, head_masks, dims=_NN):
    m = lhs.shape[0]
    lh, ll = _split2(lhs)
    both = _dot(jnp.concatenate([lh, ll], axis=0), _block_diag(q.astype(BF16), head_masks), dims)
    return both[:m] + both[m:]


def _rwkv_kernel(r_ref, k_ref, v_ref, lw_ref, kk_ref, b_ref, g_ref, rk_ref, gng_ref, gnb_ref, hsum_ref,
                 o_ref, state_ref, *, chunk):
    c = pl.program_id(1)
    n = HEAD_DIM
    rows, d = r_ref.shape[1], r_ref.shape[2]
    gw = RWKV_GROUP_HEADS * n
    n_groups = d // gw

    @pl.when(c == 0)
    def _():
        state_ref[...] = jnp.zeros_like(state_ref)

    t_idx = lax.broadcasted_iota(jnp.int32, (chunk, gw), 0)
    lane = lax.broadcasted_iota(jnp.int32, (chunk, gw), 1)
    i_idx = lane & (n - 1)
    strict = t_idx > i_idx
    incl = t_idx >= i_idx
    lane1 = lax.broadcasted_iota(jnp.int32, (1, gw), 1)
    hm = [(lane1 >> (n.bit_length() - 1)) == h for h in range(RWKV_GROUP_HEADS)]
    same = lambda s: (t_idx >> s) == (i_idx >> s)

    half_w = gw // 2
    lane_half = [(lane1 >> (half_w.bit_length() - 1)) == j for j in range(2)]
    second_in_half = ((lane >> (n.bit_length() - 1)) & 1) == 1

    def halves_rows(x):
        return jnp.concatenate([x[:, :half_w], x[:, half_w:]], axis=0).astype(BF16)

    def halves_masked(w):
        return _block_diag(w.astype(BF16), lane_half)

    def same_head(out):
        return jnp.where(second_in_half, out[n:], out[:n])

    lane_h = lax.broadcasted_iota(jnp.int32, (1, half_w), 1)
    in_half = [(lane_h >> (n.bit_length() - 1)) == j for j in range(half_w // n)]

    def two_rhs(lhs, x, y, dims=_NN):
        lb, xb, yb = lhs.astype(BF16), x.astype(BF16), y.astype(BF16)
        out_x, out_y = [], []
        for j in range(gw // half_w):
            cols = slice(j * half_w, (j + 1) * half_w)
            w = jnp.concatenate([_block_diag(xb[:, cols], in_half), _block_diag(yb[:, cols], in_half)],
                                axis=1 if dims == _NN else 0)
            o = _dot(lb[:, cols], w, dims)
            out_x.append(o[:, :half_w])
            out_y.append(o[:, half_w:])
        return jnp.concatenate(out_x, axis=1), jnp.concatenate(out_y, axis=1)

    sec_rows = rows // RWKV_SECTIONS
    sec_chunks = sec_rows // chunk
    rr = lax.broadcasted_iota(jnp.int32, (sec_rows, sec_rows), 0)
    cc = lax.broadcasted_iota(jnp.int32, (sec_rows, sec_rows), 1)
    cs = chunk.bit_length() - 1
    ltri = jnp.where((rr >= cc) & ((rr >> cs) == (cc >> cs)), 1.0, 0.0).astype(BF16)
    eye = jnp.where(t_idx == i_idx, 1.0, 0.0)
    chains = [(ci, gi) for ci in range(sec_chunks) for gi in range(n_groups)]
    win = lambda x, ci, gi: x[ci * chunk:(ci + 1) * chunk, gi * gw:(gi + 1) * gw]

    def section(row0):
        rs = slice(row0, row0 + sec_rows)
        r, k, v, lw = r_ref[0, rs, :], k_ref[0, rs, :], v_ref[0, rs, :], lw_ref[0, rs, :]
        kk, b = kk_ref[0, rs, :], b_ref[0, rs, :]
        cum = _dot_ones_lhs(ltri, lw)
        a_t = -kk * jnp.exp(cum - lw)
        r_t = r * jnp.exp(cum)
        e_neg = jnp.exp(-cum)
        b_t = b * e_neg
        k_t = k * e_neg
        tots = [cum[(ci + 1) * chunk - 1:(ci + 1) * chunk, :] for ci in range(sec_chunks)]
        e_bars = [jnp.exp(tots[ci] - cum[ci * chunk:(ci + 1) * chunk, :]) for ci in range(sec_chunks)]
        b_bar = [win(b, ci, gi) * e_bars[ci][:, gi * gw:(gi + 1) * gw] for ci, gi in chains]
        k_bar = [win(k, ci, gi) * e_bars[ci][:, gi * gw:(gi + 1) * gw] for ci, gi in chains]
        at = [win(a_t, ci, gi) for ci, gi in chains]
        rt = [win(r_t, ci, gi) for ci, gi in chains]
        vv = [win(v, ci, gi) for ci, gi in chains]
        yield

        ar = [jnp.concatenate([a_, r_], axis=0) for a_, r_ in zip(at, rt)]
        grams = [two_rhs(x, win(b_t, ci, gi), win(k_t, ci, gi), _NT) for x, (ci, gi) in zip(ar, chains)]
        gram_b = [x[0] for x in grams]
        gram_k = [x[1] for x in grams]
        yield
        m_ab = [jnp.where(strict, g_[:chunk], 0.0) for g_ in gram_b]
        p_rb = [jnp.where(incl, g_[chunk:], 0.0) for g_ in gram_b]
        m_ak = [jnp.where(strict, g_[:chunk], 0.0) for g_ in gram_k]
        p_rk = [jnp.where(incl, g_[chunk:], 0.0) for g_ in gram_k]
        yield

        ts = [eye + jnp.where(same(1), m, 0.0) for m in m_ab]
        s = 1
        while (1 << s) < chunk:
            level = same(s + 1) & jnp.logical_not(same(s))
            offs = [jnp.where(level, m, 0.0) for m in m_ab]
            mids = [_packed_dot1(t, o, hm) for t, o in zip(ts, offs)]
            yield
            ts = [t + _packed_dot1(md, t, hm) for t, md in zip(ts, mids)]
            yield
            s += 1

        both = [_packed_dot1(jnp.concatenate([m, pk], axis=0), x, hm) for m, pk, x in zip(m_ak, p_rk, vv)]
        mv = [x[:chunk] for x in both]
        pkv = [x[chunk:] for x in both]
        yield
        hat = [two_rhs(t, x, y) for t, x, y in zip(ts, at, mv)]
        a_hat = [x[0] for x in hat]
        u0 = [x[1] for x in hat]
        yield
        pb = [two_rhs(p, a, u) for p, a, u in zip(p_rb, a_hat, u0)]
        r_hat = [x + y[0] for x, y in zip(rt, pb)]
        y0 = [x + y[1] for x, y in zip(pkv, pb)]
        yield
        g_mat = [same_head(_dot(halves_rows(a), halves_masked(bb), _TN))
                 for a, bb in zip(a_hat, b_bar)]
        h_mat = [same_head(_dot(jnp.concatenate([halves_rows(u), halves_rows(x)], axis=0),
                                jnp.concatenate([halves_masked(bb), halves_masked(kb)], axis=0), _TN))
                 for u, x, bb, kb in zip(u0, vv, b_bar, k_bar)]
        yield

        y_rows = []
        for ci in range(sec_chunks):
            y_groups = []
            for gi in range(n_groups):
                j = ci * n_groups + gi
                s0 = state_ref[gi]
                y_groups.append(_packed_dot1(r_hat[j], s0, hm, _NT) + y0[j])
                w_tot = jnp.exp(tots[ci][:, gi * gw:(gi + 1) * gw])
                state_ref[gi] = s0 * w_tot + _packed_dot_wide_lhs(s0, g_mat[j], hm) + h_mat[j]
            y_rows.append(jnp.concatenate(y_groups, axis=1))
            yield
        y = jnp.concatenate(y_rows, axis=0)

        hsum = hsum_ref[...]
        inv_n = 1.0 / n
        mu = _head_sums(y, hsum) * inv_n
        yc = y - mu
        var = _head_sums(yc * yc, hsum) * inv_n
        yn = yc * lax.rsqrt(var + GN_EPS) * gng_ref[...] + gnb_ref[...]
        bonus = _head_sums(r * k * rk_ref[...], hsum)
        o_ref[0, rs, :] = ((yn + bonus * v) * g_ref[0, rs, :]).astype(o_ref.dtype)

    assert RWKV_SECTION_LAG >= sec_chunks
    _run_skewed([section(si * sec_rows) for si in range(RWKV_SECTIONS)], lag=RWKV_SECTION_LAG)


def _rwkv(r, k, v, lw, kk, b, g, rw_rk, gn_g, gn_b, hsum, *, chunk, rows):
    bsz, seq, d = r.shape
    n_groups = d // (RWKV_GROUP_HEADS * HEAD_DIM)
    blk = pl.BlockSpec((1, rows, d), lambda bi, ci: (bi, ci, 0))
    const = lambda bi, ci: (0, 0)
    vec = lambda a: a.reshape(1, d)
    return pl.pallas_call(
        functools.partial(_rwkv_kernel, chunk=chunk),
        grid=(bsz, seq // rows),
        in_specs=[blk] * 7 + [pl.BlockSpec((1, d), const)] * 3 + [pl.BlockSpec(hsum.shape, const)],
        out_specs=blk,
        out_shape=jax.ShapeDtypeStruct((bsz, seq, d), BF16),
        scratch_shapes=[pltpu.VMEM((n_groups, HEAD_DIM, RWKV_GROUP_HEADS * HEAD_DIM), F32)],
        compiler_params=pltpu.CompilerParams(dimension_semantics=("arbitrary", "arbitrary"),
                                             vmem_limit_bytes=VMEM_LIMIT),
        name="rwkv",
    )(r, k, v, lw, kk, b, g, vec(rw_rk), vec(gn_g), vec(gn_b), hsum)


SB_LOG_UNDERFLOW = 105.0


def _sb_kernel(q_ref, k_ref, v_ref, o_ref, acc_ref, run_ref, *, blk):
    qi = pl.program_id(1)
    pairs = k_ref.shape[2] // (2 * HEAD_DIM)
    row = lax.broadcasted_iota(jnp.int32, (blk, blk), 0)
    col = lax.broadcasted_iota(jnp.int32, (blk, blk), 1)
    causal = col < row
    jj = lax.broadcasted_iota(jnp.int32, (blk, 2 * blk), 0)
    ss = lax.broadcasted_iota(jnp.int32, (blk, 2 * blk), 1)
    suffix = jnp.where((jj >= ss) | (ss >= blk), 1.0, 0.0).astype(BF16)
    first_head = lax.broadcasted_iota(jnp.int32, (1, 2 * HEAD_DIM), 1) < HEAD_DIM
    zero = jnp.zeros((), BF16)

    lanes = [slice(p * 2 * HEAD_DIM, (p + 1) * 2 * HEAD_DIM) for p in range(pairs)]
    heads = [(p, hh) for p in range(pairs) for hh in range(2)]

    def tile(kb, diagonal):
        start = pl.multiple_of(kb * blk, blk)
        zs = []
        for n_, (p, hh) in enumerate(heads):
            zs.append(_dot(q_ref[0, :, n_ * 2 * HEAD_DIM:(n_ + 1) * 2 * HEAD_DIM],
                           k_ref[0, pl.ds(start, blk), lanes[p]], _NT))
        sps = [_softplus(z) for z in zs]
        if diagonal:
            sps = [jnp.where(causal, sp, 0.0) for sp in sps]
        css = [_dot(sp.astype(BF16), suffix) for sp in sps]
        atts = []
        low = None
        for n_, (z, cs) in enumerate(zip(zs, css)):
            logw = z - cs[:, :blk]
            if diagonal:
                att = jnp.where(causal, jnp.exp(logw), 0.0)
                run = cs[:, blk:]
            else:
                prev = run_ref[n_]
                att = jnp.exp(logw - prev)
                run = prev + cs[:, blk:]
            run_ref[n_] = run
            low = run if low is None else jnp.minimum(low, run)
            atts.append(att.astype(BF16))
        done = (jnp.min(low) > SB_LOG_UNDERFLOW).astype(jnp.int32)
        for p in range(pairs):
            vp = v_ref[0, pl.ds(start, blk), lanes[p]]
            vv = jnp.concatenate([jnp.where(first_head, vp, zero),
                                  jnp.where(first_head, zero, vp)], axis=0)
            out = _dot(jnp.concatenate(atts[2 * p:2 * p + 2], axis=1), vv)
            if diagonal:
                acc_ref[:, lanes[p]] = out
            else:
                acc_ref[:, lanes[p]] += out
        return done

    done0 = tile(qi, True)

    def cond(c):
        i, done = c
        return jnp.logical_and(i <= qi, done == 0)

    def body(c):
        i, _ = c
        return i + 1, tile(qi - i, False)

    lax.while_loop(cond, body, (jnp.int32(1), done0))
    o_ref[0] = acc_ref[...].astype(o_ref.dtype)


def _sb_attention(q, k, v, *, blk):
    bsz, seq, d = k.shape
    heads = d // HEAD_DIM
    qspec = pl.BlockSpec((1, blk, 2 * d), lambda b, i: (b, i, 0))
    ospec = pl.BlockSpec((1, blk, d), lambda b, i: (b, i, 0))
    kvspec = pl.BlockSpec((1, seq, d), lambda b, i: (b, 0, 0), pipeline_mode=pl.Buffered(1))
    return pl.pallas_call(
        functools.partial(_sb_kernel, blk=blk),
        grid=(bsz, seq // blk),
        in_specs=[qspec, kvspec, kvspec],
        out_specs=ospec,
        out_shape=jax.ShapeDtypeStruct((bsz, seq, d), BF16),
        scratch_shapes=[pltpu.VMEM((blk, d), F32), pltpu.VMEM((heads, blk, blk), F32)],
        compiler_params=pltpu.CompilerParams(dimension_semantics=("arbitrary", "arbitrary"),
                                             vmem_limit_bytes=VMEM_LIMIT),
        name="sbattn",
    )(q, k, v)


def _merge_kernel(x_ref, ada_ref, ya_ref, ob_ref, ga_ref, gb_ref, wa_ref, wb_ref, wo_ref,
                  lng_ref, lnb_ref, o_ref, *, alpha):
    g1 = ada_ref[0, 2:3, :]
    half = x_ref.shape[0] // 2

    def rows_program(rows):
        y_a = _dot(ya_ref[rows, :], wa_ref[...])
        y_b = _dot(ob_ref[rows, :], wb_ref[...])
        yield
        merged = ga_ref[rows, :].astype(F32) * y_a + gb_ref[rows, :].astype(F32) * y_b
        mix = _dot(merged.astype(BF16), wo_ref[...])
        yield
        o_ref[rows, :] = _norm_rows(alpha * x_ref[rows, :] + g1 * mix, LN_EPS) * lng_ref[...] + lnb_ref[...]

    _run_skewed([rows_program(slice(0, half)), rows_program(slice(half, 2 * half))], lag=1)


def _merge(x2, ada3, ya, ob, ga, gb, wa_bf, wb_bf, wo_bf, ln_g, ln_b, *, seq, tm, alpha):
    m, d = x2.shape
    tiles_per_batch = seq // tm
    row = lambda i: (i, 0)
    const = lambda i: (0, 0)
    rows = lambda a: pl.BlockSpec((tm, a.shape[1]), row)
    full = lambda a: pl.BlockSpec(a.shape, const, pipeline_mode=pl.Buffered(1))
    vec = lambda a: a.reshape(1, -1)
    ins = [x2, ada3, ya, ob, ga, gb, wa_bf, wb_bf, wo_bf, vec(ln_g), vec(ln_b)]
    in_specs = [rows(x2), pl.BlockSpec((1,) + ada3.shape[1:], lambda i: (i // tiles_per_batch, 0, 0)),
                rows(ya), rows(ob), rows(ga), rows(gb)] + [full(a) for a in ins[6:]]
    return pl.pallas_call(
        functools.partial(_merge_kernel, alpha=alpha),
        grid=(m // tm,),
        in_specs=in_specs,
        out_specs=pl.BlockSpec((tm, d), row),
        out_shape=jax.ShapeDtypeStruct((m, d), F32),
        compiler_params=pltpu.CompilerParams(dimension_semantics=("arbitrary",),
                                             vmem_limit_bytes=VMEM_LIMIT),
        name="merge",
    )(*ins)


def _ffn_kernel(x_ref, ada_ref, w1_ref, b1_ref, w2_ref, b2_ref, lng_ref, lnb_ref, o_ref, *, alpha, fchunk):
    sh = ada_ref[0, 3:4, :]
    sc = ada_ref[0, 4:5, :]
    g2 = ada_ref[0, 5:6, :]
    d_ff = w1_ref.shape[1]
    n_chunks = d_ff // fchunk
    half = x_ref.shape[0] // 2

    def rows_program(rows):
        x = x_ref[rows, :]
        hb = (_norm_rows(x, ADALN_EPS) * (1.0 + sc) + sh).astype(BF16)
        yield
        ff = None
        for j in range(n_chunks):
            sl = slice(j * fchunk, (j + 1) * fchunk)
            t = jnp.maximum(_dot(hb, w1_ref[:, sl]) + b1_ref[:, sl], 0.0)
            part = _dot((t * t).astype(BF16), w2_ref[sl, :])
            ff = part if ff is None else ff + part
            yield
        ff = ff + b2_ref[...]
        o_ref[rows, :] = _norm_rows(alpha * x + g2 * ff, LN_EPS) * lng_ref[...] + lnb_ref[...]

    _run_skewed([rows_program(slice(0, half)), rows_program(slice(half, 2 * half))], lag=2)


def _ffn(x1, ada3, w1_bf, b1, w2_bf, b2, ln_g, ln_b, *, seq, tm, alpha, fchunk):
    m, d = x1.shape
    tiles_per_batch = seq // tm
    row = lambda i: (i, 0)
    const = lambda i: (0, 0)
    full = lambda a: pl.BlockSpec(a.shape, const, pipeline_mode=pl.Buffered(1))
    vec = lambda a: a.reshape(1, -1)
    ins = [x1, ada3, w1_bf, vec(b1), w2_bf, vec(b2), vec(ln_g), vec(ln_b)]
    in_specs = [pl.BlockSpec((tm, d), row),
                pl.BlockSpec((1,) + ada3.shape[1:], lambda i: (i // tiles_per_batch, 0, 0))]
    in_specs += [full(a) for a in ins[2:]]
    return pl.pallas_call(
        functools.partial(_ffn_kernel, alpha=alpha, fchunk=fchunk),
        grid=(m // tm,),
        in_specs=in_specs,
        out_specs=pl.BlockSpec((tm, d), row),
        out_shape=jax.ShapeDtypeStruct((m, d), F32),
        compiler_params=pltpu.CompilerParams(dimension_semantics=("arbitrary",),
                                             vmem_limit_bytes=VMEM_LIMIT),
        name="ffn",
    )(*ins)


def _branches(proj, rw_rk, rw_gn_g, rw_gn_b, hsum, bsz, seq):
    r, k, v, lw, kk, b, g, q, ks, vs = proj
    seq3 = lambda a: a.reshape(bsz, seq, -1)
    ya = _rwkv(seq3(r), seq3(k), seq3(v), seq3(lw), seq3(kk), seq3(b), seq3(g),
               rw_rk, rw_gn_g, rw_gn_b, hsum, chunk=min(RWKV_CHUNK, seq),
               rows=min(RWKV_CHUNKS_PER_STEP * RWKV_CHUNK, seq))
    o = _sb_attention(seq3(q), seq3(ks), seq3(vs), blk=min(SB_BLOCK, seq))
    return ya.reshape(bsz * seq, -1), o.reshape(bsz * seq, -1)


def _layer(x, c, w_ada, b_ada, w_in, mu_shift, rw_w0, rw_w2, rw_a0, rw_a2, rw_g2, rw_kk, rw_ka,
           rw_rk, rw_gn_g, rw_gn_b, w_branch_a, w_branch_b, w_out, ln1_g, ln1_b,
           w_ff1, b_ff1, w_ff2, b_ff2, ln2_g, ln2_b, *, alpha):
    bsz, seq, d = x.shape
    d_rwkv = rw_w0.shape[-1]
    tm = min(INPROJ_ROWS, seq)
    tm2 = min(MLP_ROWS, seq)

    ada3 = _ada(c, w_ada, b_ada).reshape(bsz, 6, d)
    lane = jnp.arange(RWKV_GROUP_HEADS * HEAD_DIM) // HEAD_DIM
    hsum = (lane[:, None] == lane[None, :]).astype(BF16)

    x2 = x.reshape(bsz * seq, d)
    (r, k, v, lw, kk, b, g, q, ks, vs, ga, gb) = _inproj(
        x2, ada3, w_in.astype(BF16), mu_shift, rw_w0, rw_w2, rw_a0, rw_a2, rw_g2, rw_kk, rw_ka, hsum,
        seq=seq, tm=tm)

    ya, ob = _branches((r, k, v, lw, kk, b, g, q, ks, vs), rw_rk, rw_gn_g, rw_gn_b, hsum, bsz, seq)

    x1 = _merge(x2, ada3, ya, ob, ga, gb,
                w_branch_a.astype(BF16), w_branch_b.astype(BF16), w_out.astype(BF16),
                ln1_g, ln1_b, seq=seq, tm=tm2, alpha=alpha)
    out = _ffn(x1, ada3, w_ff1.astype(BF16), b_ff1, w_ff2.astype(BF16), b_ff2, ln2_g, ln2_b,
               seq=seq, tm=tm2, alpha=alpha, fchunk=min(1024, w_ff1.shape[-1]))
    return out.reshape(bsz, seq, d)


def kernel(x, c, w_ada, b_ada, w_in, mu_shift, rw_w0, rw_w2, rw_a0, rw_a2, rw_g2, rw_kk, rw_ka, rw_rk,
           rw_gn_g, rw_gn_b, w_branch_a, w_branch_b, w_out, ln1_g, ln1_b, w_ff1, b_ff1, w_ff2, b_ff2,
           ln2_g, ln2_b):
    in_dtype = x.dtype
    layer_params = (w_ada, b_ada, w_in, mu_shift, rw_w0, rw_w2, rw_a0, rw_a2, rw_g2, rw_kk, rw_ka,
                    rw_rk, rw_gn_g, rw_gn_b, w_branch_a, w_branch_b, w_out, ln1_g, ln1_b,
                    w_ff1, b_ff1, w_ff2, b_ff2, ln2_g, ln2_b)
    depth = w_ada.shape[0]
    alpha = (2.0 * depth) ** 0.25
    for l in range(depth):
        x = _layer(x, c, *[p[l] for p in layer_params], alpha=alpha)
    return x.astype(in_dtype)
```

```python
import functools

import jax
import jax.numpy as jnp
from jax import lax
from jax.experimental import pallas as pl
from jax.experimental.pallas import tpu as pltpu

F32 = jnp.float32
BF16 = jnp.bfloat16

HEAD_DIM = 64
LN_EPS = 1e-5
ADALN_EPS = 1e-6
GN_EPS = 64e-5
RWKV_CHUNK = 64
RWKV_SECTIONS = 2
RWKV_SECTION_CHUNKS = 4
RWKV_SECTION_LAG = 12
RWKV_CHUNKS_PER_STEP = RWKV_SECTIONS * RWKV_SECTION_CHUNKS
SB_BLOCK = 128
INPROJ_ROWS = 256
MLP_ROWS = 512
VMEM_LIMIT = 56 * 1024 * 1024


def _split2(a):
    hi = a.astype(BF16)
    lo = (a - hi.astype(F32)).astype(BF16)
    return hi, lo


_NN = (((1,), (0,)), ((), ()))
_NT = (((1,), (1,)), ((), ()))
_TN = (((0,), (0,)), ((), ()))


def _dot(a, b, dims=_NN):
    return lax.dot_general(a, b, dims, preferred_element_type=F32)


def _dot3(a, b, dims=_NN):
    ah, al = _split2(a)
    bh, bl = _split2(b)
    return _dot(ah, bh, dims) + (_dot(al, bh, dims) + _dot(ah, bl, dims))


def _head_sums(x, ones_blk):
    m, w = x.shape[0], ones_blk.shape[0]
    hi, lo = _split2(x)
    outs = []
    for g in range(x.shape[1] // w):
        cols = slice(g * w, (g + 1) * w)
        both = _dot(jnp.concatenate([hi[:, cols], lo[:, cols]], axis=0), ones_blk)
        outs.append(both[:m] + both[m:])
    return jnp.concatenate(outs, axis=1)


def _dot_ones_lhs(a_bf, b):
    hi, lo = _split2(b)
    return _dot(a_bf, hi) + _dot(a_bf, lo)


_DONE = object()


def _run_skewed(programs, lag):
    live = [True] * len(programs)
    tick = 0
    while any(live):
        for i, prog in enumerate(programs):
            if live[i] and tick >= i * lag:
                live[i] = next(prog, _DONE) is not _DONE
        tick += 1


_NEG_LOG2_E = -1.4426950408889634


def _softplus(y):
    return jnp.maximum(y, 0.0) + jnp.log(1.0 + jnp.exp2(jnp.abs(y) * _NEG_LOG2_E))


def _sigmoid(y):
    return 1.0 / (1.0 + jnp.exp(-y))


def _norm_rows(x, eps):
    mu = jnp.mean(x, axis=-1, keepdims=True)
    xc = x - mu
    var = jnp.mean(xc * xc, axis=-1, keepdims=True)
    return xc * lax.rsqrt(var + eps)


def _ada_kernel(c_ref, w_ref, b_ref, o_ref):
    c = c_ref[...]
    s = c * _sigmoid(c)
    o_ref[...] = jnp.dot(s, w_ref[...], preferred_element_type=F32,
                         precision=lax.Precision.HIGHEST) + b_ref[...]


def _ada(c, w_ada, b_ada):
    bsz, d = c.shape
    n = w_ada.shape[1]
    return pl.pallas_call(
        _ada_kernel,
        grid=(n // d,),
        in_specs=[pl.BlockSpec((bsz, d), lambda j: (0, 0)),
                  pl.BlockSpec((d, d), lambda j: (0, j)),
                  pl.BlockSpec((1, d), lambda j: (0, j))],
        out_specs=pl.BlockSpec((bsz, d), lambda j: (0, j)),
        out_shape=jax.ShapeDtypeStruct((bsz, n), F32),
        name="ada",
    )(c, w_ada, b_ada.reshape(1, n))


def _inproj_kernel(x_ref, ada_ref, win_ref, mu_ref, w0_ref, w2_ref, a0_ref, a2_ref, g2_ref,
                   kkw_ref, kaw_ref, hsum_ref,
                   r_ref, k_ref, v_ref, lw_ref, kk_ref, b_ref, g_ref,
                   q_ref, ks_ref, vs_ref, ga_ref, gb_ref,
                   carry_ref, *, tiles_per_batch, d_rwkv, d_decay, d_aaa, d_gate, d_sb, d_model):
    i = pl.program_id(0)
    tm = x_ref.shape[0]
    d_shift = 3 * d_rwkv + d_decay + d_aaa + d_gate

    sh = ada_ref[0, 0:1, :]
    sc = ada_ref[0, 1:2, :]
    h = _norm_rows(x_ref[...], ADALN_EPS) * (1.0 + sc) + sh
    hb = h.astype(BF16)

    @pl.when(i % tiles_per_batch == 0)
    def _():
        carry_ref[...] = jnp.zeros_like(carry_ref)

    row0 = lax.broadcasted_iota(jnp.int32, (tm, 1), 0) == 0

    def shifted(lo, width):
        cols = slice(lo, lo + width)
        z = _dot(hb, win_ref[:, cols])
        prev = jnp.where(row0, carry_ref[:, cols], pltpu.roll(z, 1, 0))
        carry_ref[:, cols] = z[tm - 1:tm, :]
        return z + mu_ref[:, cols] * (prev - z)

    half = d_model // 2
    pair_w = 2 * HEAD_DIM
    first_head = lax.broadcasted_iota(jnp.int32, (1, pair_w), 1) < HEAD_DIM

    def store_queries(z):
        zq = (z * (HEAD_DIM ** -0.5)).astype(BF16)
        zero = jnp.zeros((), BF16)
        for p in range(d_sb // pair_w):
            qp = zq[:, p * pair_w:(p + 1) * pair_w]
            q_ref[:, (2 * p) * pair_w:(2 * p + 1) * pair_w] = jnp.where(first_head, qp, zero)
            q_ref[:, (2 * p + 1) * pair_w:(2 * p + 2) * pair_w] = jnp.where(first_head, zero, qp)

    def plain(ref, cols, fn=lambda z: z):
        def store(z):
            ref[:, cols] = fn(z).astype(BF16)
        return store

    rest = [(store_queries, d_sb), (plain(ks_ref, slice(None)), d_sb), (plain(vs_ref, slice(None)), d_sb),
            (plain(ga_ref, slice(0, half), _sigmoid), half), (plain(ga_ref, slice(half, d_model), _sigmoid), half),
            (plain(gb_ref, slice(0, half), _sigmoid), half), (plain(gb_ref, slice(half, d_model), _sigmoid), half)]
    rest_col = [d_shift]

    def project_next():
        store, width = rest.pop(0)
        store(_dot(hb, win_ref[:, rest_col[0]:rest_col[0] + width]))
        rest_col[0] += width

    zl = shifted(3 * d_rwkv, d_decay + d_aaa + d_gate)
    zw = zl[:, :d_decay]
    za = zl[:, d_decay:d_decay + d_aaa]
    zg = zl[:, d_decay + d_aaa:]
    k = shifted(d_rwkv, d_rwkv)
    ww = w0_ref[...] + _dot(jnp.tanh(zw).astype(BF16), w2_ref[...])
    w_log = -_softplus(-ww) - 0.5
    lw_ref[...] = -jnp.exp(w_log)
    project_next()
    arate = _sigmoid(a0_ref[...] + _dot(za.astype(BF16), a2_ref[...]))
    g_ref[...] = _dot(_sigmoid(zg).astype(BF16), g2_ref[...])
    r_ref[...] = shifted(0, d_rwkv)
    kkraw = k * kkw_ref[...]
    ssq = _head_sums(kkraw * kkraw, hsum_ref[...])
    kk = kkraw / jnp.maximum(jnp.sqrt(ssq), 1e-12)
    v_ref[...] = shifted(2 * d_rwkv, d_rwkv)
    k_ref[...] = k * (1.0 + (arate - 1.0) * kaw_ref[...])
    project_next()
    kk_ref[...] = kk
    b_ref[...] = kk * arate
    while rest:
        project_next()


def _inproj(x2, ada3, win_bf, mu_shift, rw_w0, rw_w2, rw_a0, rw_a2, rw_g2, rw_kk, rw_ka, hsum,
            *, seq, tm):
    m, d_model = x2.shape
    d_rwkv = rw_w0.shape[-1]
    d_decay, d_aaa, d_gate = rw_w2.shape[0], rw_a2.shape[0], rw_g2.shape[0]
    d_shift = 3 * d_rwkv + d_decay + d_aaa + d_gate
    d_sb = (win_bf.shape[1] - d_shift - 2 * d_model) // 3
    tiles_per_batch = seq // tm
    row = lambda i: (i, 0)
    const = lambda i: (0, 0)
    full = lambda a: pl.BlockSpec(a.shape, const, pipeline_mode=pl.Buffered(1))
    vec = lambda a: a.reshape(1, -1)
    ins = [x2, ada3, win_bf, vec(mu_shift), vec(rw_w0), rw_w2.astype(BF16), vec(rw_a0),
           rw_a2.astype(BF16), rw_g2.astype(BF16), vec(rw_kk), vec(rw_ka), hsum]
    in_specs = [pl.BlockSpec((tm, d_model), row),
                pl.BlockSpec((1,) + ada3.shape[1:], lambda i: (i // tiles_per_batch, 0, 0))]
    in_specs += [full(a) for a in ins[2:]]
    widths = [d_rwkv] * 7 + [2 * d_sb, d_sb, d_sb] + [d_model] * 2
    dtypes = [F32] * 7 + [BF16] * 5
    out_shape = [jax.ShapeDtypeStruct((m, w), dt) for w, dt in zip(widths, dtypes)]
    out_specs = [pl.BlockSpec((tm, w), row) for w in widths]
    kern = functools.partial(_inproj_kernel, tiles_per_batch=tiles_per_batch, d_rwkv=d_rwkv,
                             d_decay=d_decay, d_aaa=d_aaa, d_gate=d_gate, d_sb=d_sb, d_model=d_model)
    return pl.pallas_call(
        kern,
        grid=(m // tm,),
        in_specs=in_specs,
        out_specs=out_specs,
        out_shape=out_shape,
        scratch_shapes=[pltpu.VMEM((1, d_shift), F32)],
        compiler_params=pltpu.CompilerParams(dimension_semantics=("arbitrary",),
                                             vmem_limit_bytes=VMEM_LIMIT),
        name="inproj",
    )(*ins)


RWKV_GROUP_HEADS = 4


def _block_diag(q, head_masks):
    zero = jnp.zeros((), q.dtype)
    return jnp.concatenate([jnp.where(m, q, zero) for m in head_masks], axis=0)


def _rwkv_kernel(r_ref, k_ref, v_ref, lw_ref, kk_ref, b_ref, g_ref, rk_ref, gng_ref, gnb_ref, hsum_ref,
                 o_ref, state_ref, *, chunk):
    c = pl.program_id(1)
    n = HEAD_DIM
    rows, d = r_ref.shape[1], r_ref.shape[2]
    gw = RWKV_GROUP_HEADS * n
    n_groups = d // gw

    @pl.when(c == 0)
    def _():
        state_ref[...] = jnp.zeros_like(state_ref)

    t_idx = lax.broadcasted_iota(jnp.int32, (chunk, gw), 0)
    lane = lax.broadcasted_iota(jnp.int32, (chunk, gw), 1)
    i_idx = lane & (n - 1)
    strict = t_idx > i_idx
    incl = t_idx >= i_idx
    lane1 = lax.broadcasted_iota(jnp.int32, (1, gw), 1)
    same = lambda s: (t_idx >> s) == (i_idx >> s)

    half_w = gw // 2
    lane_half = [(lane1 >> (half_w.bit_length() - 1)) == j for j in range(2)]
    second_in_half = ((lane >> (n.bit_length() - 1)) & 1) == 1

    def halves_rows(x):
        return jnp.concatenate([x[:, :half_w], x[:, half_w:]], axis=0).astype(BF16)

    def halves_masked(w):
        return _block_diag(w.astype(BF16), lane_half)

    def same_head(out):
        return jnp.where(second_in_half, out[n:], out[:n])

    lane_h = lax.broadcasted_iota(jnp.int32, (1, half_w), 1)
    in_half = [(lane_h >> (n.bit_length() - 1)) == j for j in range(half_w // n)]

    half_cols = [slice(j * half_w, (j + 1) * half_w) for j in range(gw // half_w)]

    def half_dot(lhs, q, dims=_NN):
        lb, qb = lhs.astype(BF16), q.astype(BF16)
        return jnp.concatenate([_dot(lb[:, c_], _block_diag(qb[:, c_], in_half), dims) for c_ in half_cols],
                               axis=1)

    def half_dot_wide_lhs(lhs, q):
        m = lhs.shape[0]
        both = half_dot(jnp.concatenate(_split2(lhs), axis=0), q)
        return both[:m] + both[m:]

    def two_rhs(lhs, x, y, dims=_NN):
        lb, xb, yb = lhs.astype(BF16), x.astype(BF16), y.astype(BF16)
        out_x, out_y = [], []
        for j in range(gw // half_w):
            cols = slice(j * half_w, (j + 1) * half_w)
            w = jnp.concatenate([_block_diag(xb[:, cols], in_half), _block_diag(yb[:, cols], in_half)],
                                axis=1 if dims == _NN else 0)
            o = _dot(lb[:, cols], w, dims)
            out_x.append(o[:, :half_w])
            out_y.append(o[:, half_w:])
        return jnp.concatenate(out_x, axis=1), jnp.concatenate(out_y, axis=1)

    sec_rows = rows // RWKV_SECTIONS
    sec_chunks = sec_rows // chunk
    rr = lax.broadcasted_iota(jnp.int32, (sec_rows, sec_rows), 0)
    cc = lax.broadcasted_iota(jnp.int32, (sec_rows, sec_rows), 1)
    cs = chunk.bit_length() - 1
    ltri = jnp.where((rr >= cc) & ((rr >> cs) == (cc >> cs)), 1.0, 0.0).astype(BF16)
    eye = jnp.where(t_idx == i_idx, 1.0, 0.0)
    chains = [(ci, gi) for ci in range(sec_chunks) for gi in range(n_groups)]
    win = lambda x, ci, gi: x[ci * chunk:(ci + 1) * chunk, gi * gw:(gi + 1) * gw]

    def section(row0):
        rs = slice(row0, row0 + sec_rows)
        r, k, v, lw = r_ref[0, rs, :], k_ref[0, rs, :], v_ref[0, rs, :], lw_ref[0, rs, :]
        kk, b = kk_ref[0, rs, :], b_ref[0, rs, :]
        cum = _dot_ones_lhs(ltri, lw)
        a_t = -kk * jnp.exp(cum - lw)
        r_t = r * jnp.exp(cum)
        e_neg = jnp.exp(-cum)
        b_t = b * e_neg
        k_t = k * e_neg
        tots = [cum[(ci + 1) * chunk - 1:(ci + 1) * chunk, :] for ci in range(sec_chunks)]
        e_bars = [jnp.exp(tots[ci] - cum[ci * chunk:(ci + 1) * chunk, :]) for ci in range(sec_chunks)]
        b_bar = [win(b, ci, gi) * e_bars[ci][:, gi * gw:(gi + 1) * gw] for ci, gi in chains]
        k_bar = [win(k, ci, gi) * e_bars[ci][:, gi * gw:(gi + 1) * gw] for ci, gi in chains]
        at = [win(a_t, ci, gi) for ci, gi in chains]
        rt = [win(r_t, ci, gi) for ci, gi in chains]
        vv = [win(v, ci, gi) for ci, gi in chains]
        yield

        ar = [jnp.concatenate([a_, r_], axis=0) for a_, r_ in zip(at, rt)]
        grams = [two_rhs(x, win(b_t, ci, gi), win(k_t, ci, gi), _NT) for x, (ci, gi) in zip(ar, chains)]
        gram_b = [x[0] for x in grams]
        gram_k = [x[1] for x in grams]
        yield
        m_ab = [jnp.where(strict, g_[:chunk], 0.0) for g_ in gram_b]
        p_rb = [jnp.where(incl, g_[chunk:], 0.0) for g_ in gram_b]
        m_ak = [jnp.where(strict, g_[:chunk], 0.0) for g_ in gram_k]
        p_rk = [jnp.where(incl, g_[chunk:], 0.0) for g_ in gram_k]
        yield

        ts = [eye + jnp.where(same(1), m, 0.0) for m in m_ab]
        s = 1
        while (1 << s) < chunk:
            level = same(s + 1) & jnp.logical_not(same(s))
            offs = [jnp.where(level, m, 0.0) for m in m_ab]
            mids = [half_dot(t, o) for t, o in zip(ts, offs)]
            yield
            ts = [t + half_dot(md, t) for t, md in zip(ts, mids)]
            yield
            s += 1

        both = [half_dot(jnp.concatenate([m, pk], axis=0), x) for m, pk, x in zip(m_ak, p_rk, vv)]
        mv = [x[:chunk] for x in both]
        pkv = [x[chunk:] for x in both]
        yield
        hat = [two_rhs(t, x, y) for t, x, y in zip(ts, at, mv)]
        a_hat = [x[0] for x in hat]
        u0 = [x[1] for x in hat]
        yield
        pb = [two_rhs(p, a, u) for p, a, u in zip(p_rb, a_hat, u0)]
        r_hat = [x + y[0] for x, y in zip(rt, pb)]
        y0 = [x + y[1] for x, y in zip(pkv, pb)]
        yield
        g_mat = [same_head(_dot(halves_rows(a), halves_masked(bb), _TN))
                 for a, bb in zip(a_hat, b_bar)]
        h_mat = [same_head(_dot(jnp.concatenate([halves_rows(u), halves_rows(x)], axis=0),
                                jnp.concatenate([halves_masked(bb), halves_masked(kb)], axis=0), _TN))
                 for u, x, bb, kb in zip(u0, vv, b_bar, k_bar)]
        yield

        y_rows = []
        for ci in range(sec_chunks):
            y_groups = []
            for gi in range(n_groups):
                j = ci * n_groups + gi
                s0 = state_ref[gi]
                y_groups.append(half_dot(r_hat[j], s0, _NT) + y0[j])
                w_tot = jnp.exp(tots[ci][:, gi * gw:(gi + 1) * gw])
                state_ref[gi] = s0 * w_tot + half_dot_wide_lhs(s0, g_mat[j]) + h_mat[j]
            y_rows.append(jnp.concatenate(y_groups, axis=1))
            yield
        y = jnp.concatenate(y_rows, axis=0)

        hsum = hsum_ref[...]
        inv_n = 1.0 / n
        mu = _head_sums(y, hsum) * inv_n
        yc = y - mu
        var = _head_sums(yc * yc, hsum) * inv_n
        yn = yc * lax.rsqrt(var + GN_EPS) * gng_ref[...] + gnb_ref[...]
        bonus = _head_sums(r * k * rk_ref[...], hsum)
        o_ref[0, rs, :] = ((yn + bonus * v) * g_ref[0, rs, :]).astype(o_ref.dtype)

    assert RWKV_SECTION_LAG >= sec_chunks
    _run_skewed([section(si * sec_rows) for si in range(RWKV_SECTIONS)], lag=RWKV_SECTION_LAG)


def _rwkv(r, k, v, lw, kk, b, g, rw_rk, gn_g, gn_b, hsum, *, chunk, rows):
    bsz, seq, d = r.shape
    n_groups = d // (RWKV_GROUP_HEADS * HEAD_DIM)
    blk = pl.BlockSpec((1, rows, d), lambda bi, ci: (bi, ci, 0))
    const = lambda bi, ci: (0, 0)
    vec = lambda a: a.reshape(1, d)
    return pl.pallas_call(
        functools.partial(_rwkv_kernel, chunk=chunk),
        grid=(bsz, seq // rows),
        in_specs=[blk] * 7 + [pl.BlockSpec((1, d), const)] * 3 + [pl.BlockSpec(hsum.shape, const)],
        out_specs=blk,
        out_shape=jax.ShapeDtypeStruct((bsz, seq, d), BF16),
        scratch_shapes=[pltpu.VMEM((n_groups, HEAD_DIM, RWKV_GROUP_HEADS * HEAD_DIM), F32)],
        compiler_params=pltpu.CompilerParams(dimension_semantics=("arbitrary", "arbitrary"),
                                             vmem_limit_bytes=VMEM_LIMIT),
        name="rwkv",
    )(r, k, v, lw, kk, b, g, vec(rw_rk), vec(gn_g), vec(gn_b), hsum)


SB_LOG_UNDERFLOW = 105.0


def _sb_kernel(q_ref, k_ref, v_ref, o_ref, acc_ref, run_ref, *, blk):
    qi = pl.program_id(1)
    pairs = k_ref.shape[2] // (2 * HEAD_DIM)
    row = lax.broadcasted_iota(jnp.int32, (blk, blk), 0)
    col = lax.broadcasted_iota(jnp.int32, (blk, blk), 1)
    causal = col < row
    jj = lax.broadcasted_iota(jnp.int32, (blk, 2 * blk), 0)
    ss = lax.broadcasted_iota(jnp.int32, (blk, 2 * blk), 1)
    suffix = jnp.where((jj >= ss) | (ss >= blk), 1.0, 0.0).astype(BF16)
    first_head = lax.broadcasted_iota(jnp.int32, (1, 2 * HEAD_DIM), 1) < HEAD_DIM
    zero = jnp.zeros((), BF16)

    lanes = [slice(p * 2 * HEAD_DIM, (p + 1) * 2 * HEAD_DIM) for p in range(pairs)]
    heads = [(p, hh) for p in range(pairs) for hh in range(2)]

    def tile(kb, diagonal):
        start = pl.multiple_of(kb * blk, blk)
        zs = []
        for n_, (p, hh) in enumerate(heads):
            zs.append(_dot(q_ref[0, :, n_ * 2 * HEAD_DIM:(n_ + 1) * 2 * HEAD_DIM],
                           k_ref[0, pl.ds(start, blk), lanes[p]], _NT))
        sps = [_softplus(z) for z in zs]
        if diagonal:
            sps = [jnp.where(causal, sp, 0.0) for sp in sps]
        css = [_dot(sp.astype(BF16), suffix) for sp in sps]
        atts = []
        low = None
        for n_, (z, cs) in enumerate(zip(zs, css)):
            logw = z - cs[:, :blk]
            if diagonal:
                att = jnp.where(causal, jnp.exp(logw), 0.0)
                run = cs[:, blk:]
            else:
                prev = run_ref[n_]
                att = jnp.exp(logw - prev)
                run = prev + cs[:, blk:]
            run_ref[n_] = run
            low = run if low is None else jnp.minimum(low, run)
            atts.append(att.astype(BF16))
        done = (jnp.min(low) > SB_LOG_UNDERFLOW).astype(jnp.int32)
        for p in range(pairs):
            vp = v_ref[0, pl.ds(start, blk), lanes[p]]
            vv = jnp.concatenate([jnp.where(first_head, vp, zero),
                                  jnp.where(first_head, zero, vp)], axis=0)
            out = _dot(jnp.concatenate(atts[2 * p:2 * p + 2], axis=1), vv)
            if diagonal:
                acc_ref[:, lanes[p]] = out
            else:
                acc_ref[:, lanes[p]] += out
        return done

    done0 = tile(qi, True)

    def cond(c):
        i, done = c
        return jnp.logical_and(i <= qi, done == 0)

    def body(c):
        i, _ = c
        return i + 1, tile(qi - i, False)

    lax.while_loop(cond, body, (jnp.int32(1), done0))
    o_ref[0] = acc_ref[...].astype(o_ref.dtype)


def _sb_attention(q, k, v, *, blk):
    bsz, seq, d = k.shape
    heads = d // HEAD_DIM
    qspec = pl.BlockSpec((1, blk, 2 * d), lambda b, i: (b, i, 0))
    ospec = pl.BlockSpec((1, blk, d), lambda b, i: (b, i, 0))
    kvspec = pl.BlockSpec((1, seq, d), lambda b, i: (b, 0, 0), pipeline_mode=pl.Buffered(1))
    return pl.pallas_call(
        functools.partial(_sb_kernel, blk=blk),
        grid=(bsz, seq // blk),
        in_specs=[qspec, kvspec, kvspec],
        out_specs=ospec,
        out_shape=jax.ShapeDtypeStruct((bsz, seq, d), BF16),
        scratch_shapes=[pltpu.VMEM((blk, d), F32), pltpu.VMEM((heads, blk, blk), F32)],
        compiler_params=pltpu.CompilerParams(dimension_semantics=("arbitrary", "arbitrary"),
                                             vmem_limit_bytes=VMEM_LIMIT),
        name="sbattn",
    )(q, k, v)


def _merge_kernel(x_ref, ada_ref, ya_ref, ob_ref, ga_ref, gb_ref, wa_ref, wb_ref, wo_ref,
                  lng_ref, lnb_ref, o_ref, *, alpha):
    g1 = ada_ref[0, 2:3, :]
    half = x_ref.shape[0] // 2

    def rows_program(rows):
        y_a = _dot(ya_ref[rows, :], wa_ref[...])
        y_b = _dot(ob_ref[rows, :], wb_ref[...])
        yield
        merged = ga_ref[rows, :].astype(F32) * y_a + gb_ref[rows, :].astype(F32) * y_b
        mix = _dot(merged.astype(BF16), wo_ref[...])
        yield
        o_ref[rows, :] = _norm_rows(alpha * x_ref[rows, :] + g1 * mix, LN_EPS) * lng_ref[...] + lnb_ref[...]

    _run_skewed([rows_program(slice(0, half)), rows_program(slice(half, 2 * half))], lag=1)


def _merge(x2, ada3, ya, ob, ga, gb, wa_bf, wb_bf, wo_bf, ln_g, ln_b, *, seq, tm, alpha):
    m, d = x2.shape
    tiles_per_batch = seq // tm
    row = lambda i: (i, 0)
    const = lambda i: (0, 0)
    rows = lambda a: pl.BlockSpec((tm, a.shape[1]), row)
    full = lambda a: pl.BlockSpec(a.shape, const, pipeline_mode=pl.Buffered(1))
    vec = lambda a: a.reshape(1, -1)
    ins = [x2, ada3, ya, ob, ga, gb, wa_bf, wb_bf, wo_bf, vec(ln_g), vec(ln_b)]
    in_specs = [rows(x2), pl.BlockSpec((1,) + ada3.shape[1:], lambda i: (i // tiles_per_batch, 0, 0)),
                rows(ya), rows(ob), rows(ga), rows(gb)] + [full(a) for a in ins[6:]]
    return pl.pallas_call(
        functools.partial(_merge_kernel, alpha=alpha),
        grid=(m // tm,),
        in_specs=in_specs,
        out_specs=pl.BlockSpec((tm, d), row),
        out_shape=jax.ShapeDtypeStruct((m, d), F32),
        compiler_params=pltpu.CompilerParams(dimension_semantics=("arbitrary",),
                                             vmem_limit_bytes=VMEM_LIMIT),
        name="merge",
    )(*ins)


def _ffn_kernel(x_ref, ada_ref, w1_ref, b1_ref, w2_ref, b2_ref, lng_ref, lnb_ref, o_ref, *, alpha, fchunk):
    sh = ada_ref[0, 3:4, :]
    sc = ada_ref[0, 4:5, :]
    g2 = ada_ref[0, 5:6, :]
    d_ff = w1_ref.shape[1]
    n_chunks = d_ff // fchunk
    half = x_ref.shape[0] // 2

    def rows_program(rows):
        x = x_ref[rows, :]
        hb = (_norm_rows(x, ADALN_EPS) * (1.0 + sc) + sh).astype(BF16)
        yield
        ff = None
        for j in range(n_chunks):
            sl = slice(j * fchunk, (j + 1) * fchunk)
            t = jnp.maximum(_dot(hb, w1_ref[:, sl]) + b1_ref[:, sl], 0.0)
            part = _dot((t * t).astype(BF16), w2_ref[sl, :])
            ff = part if ff is None else ff + part
            yield
        ff = ff + b2_ref[...]
        o_ref[rows, :] = _norm_rows(alpha * x + g2 * ff, LN_EPS) * lng_ref[...] + lnb_ref[...]

    _run_skewed([rows_program(slice(0, half)), rows_program(slice(half, 2 * half))], lag=2)


def _ffn(x1, ada3, w1_bf, b1, w2_bf, b2, ln_g, ln_b, *, seq, tm, alpha, fchunk):
    m, d = x1.shape
    tiles_per_batch = seq // tm
    row = lambda i: (i, 0)
    const = lambda i: (0, 0)
    full = lambda a: pl.BlockSpec(a.shape, const, pipeline_mode=pl.Buffered(1))
    vec = lambda a: a.reshape(1, -1)
    ins = [x1, ada3, w1_bf, vec(b1), w2_bf, vec(b2), vec(ln_g), vec(ln_b)]
    in_specs = [pl.BlockSpec((tm, d), row),
                pl.BlockSpec((1,) + ada3.shape[1:], lambda i: (i // tiles_per_batch, 0, 0))]
    in_specs += [full(a) for a in ins[2:]]
    return pl.pallas_call(
        functools.partial(_ffn_kernel, alpha=alpha, fchunk=fchunk),
        grid=(m // tm,),
        in_specs=in_specs,
        out_specs=pl.BlockSpec((tm, d), row),
        out_shape=jax.ShapeDtypeStruct((m, d), F32),
        compiler_params=pltpu.CompilerParams(dimension_semantics=("arbitrary",),
                                             vmem_limit_bytes=VMEM_LIMIT),
        name="ffn",
    )(*ins)


def _branches(proj, rw_rk, rw_gn_g, rw_gn_b, hsum, bsz, seq):
    r, k, v, lw, kk, b, g, q, ks, vs = proj
    seq3 = lambda a: a.reshape(bsz, seq, -1)
    ya = _rwkv(seq3(r), seq3(k), seq3(v), seq3(lw), seq3(kk), seq3(b), seq3(g),
               rw_rk, rw_gn_g, rw_gn_b, hsum, chunk=min(RWKV_CHUNK, seq),
               rows=min(RWKV_CHUNKS_PER_STEP * RWKV_CHUNK, seq))
    o = _sb_attention(seq3(q), seq3(ks), seq3(vs), blk=min(SB_BLOCK, seq))
    return ya.reshape(bsz * seq, -1), o.reshape(bsz * seq, -1)


def _layer(x, c, w_ada, b_ada, w_in, mu_shift, rw_w0, rw_w2, rw_a0, rw_a2, rw_g2, rw_kk, rw_ka,
           rw_rk, rw_gn_g, rw_gn_b, w_branch_a, w_branch_b, w_out, ln1_g, ln1_b,
           w_ff1, b_ff1, w_ff2, b_ff2, ln2_g, ln2_b, *, alpha):
    bsz, seq, d = x.shape
    d_rwkv = rw_w0.shape[-1]
    tm = min(INPROJ_ROWS, seq)
    tm2 = min(MLP_ROWS, seq)

    ada3 = _ada(c, w_ada, b_ada).reshape(bsz, 6, d)
    lane = jnp.arange(RWKV_GROUP_HEADS * HEAD_DIM) // HEAD_DIM
    hsum = (lane[:, None] == lane[None, :]).astype(BF16)

    x2 = x.reshape(bsz * seq, d)
    (r, k, v, lw, kk, b, g, q, ks, vs, ga, gb) = _inproj(
        x2, ada3, w_in.astype(BF16), mu_shift, rw_w0, rw_w2, rw_a0, rw_a2, rw_g2, rw_kk, rw_ka, hsum,
        seq=seq, tm=tm)

    ya, ob = _branches((r, k, v, lw, kk, b, g, q, ks, vs), rw_rk, rw_gn_g, rw_gn_b, hsum, bsz, seq)

    x1 = _merge(x2, ada3, ya, ob, ga, gb,
                w_branch_a.astype(BF16), w_branch_b.astype(BF16), w_out.astype(BF16),
                ln1_g, ln1_b, seq=seq, tm=tm2, alpha=alpha)
    out = _ffn(x1, ada3, w_ff1.astype(BF16), b_ff1, w_ff2.astype(BF16), b_ff2, ln2_g, ln2_b,
               seq=seq, tm=tm2, alpha=alpha, fchunk=min(1024, w_ff1.shape[-1]))
    return out.reshape(bsz, seq, d)


def kernel(x, c, w_ada, b_ada, w_in, mu_shift, rw_w0, rw_w2, rw_a0, rw_a2, rw_g2, rw_kk, rw_ka, rw_rk,
           rw_gn_g, rw_gn_b, w_branch_a, w_branch_b, w_out, ln1_g, ln1_b, w_ff1, b_ff1, w_ff2, b_ff2,
           ln2_g, ln2_b):
    in_dtype = x.dtype
    layer_params = (w_ada, b_ada, w_in, mu_shift, rw_w0, rw_w2, rw_a0, rw_a2, rw_g2, rw_kk, rw_ka,
                    rw_rk, rw_gn_g, rw_gn_b, w_branch_a, w_branch_b, w_out, ln1_g, ln1_b,
                    w_ff1, b_ff1, w_ff2, b_ff2, ln2_g, ln2_b)
    depth = w_ada.shape[0]
    alpha = (2.0 * depth) ** 0.25
    for l in range(depth):
        x = _layer(x, c, *[p[l] for p in layer_params], alpha=alpha)
    return x.astype(in_dtype)
```

```python
import functools

import jax
import jax.numpy as jnp
from jax import lax
from jax.experimental import pallas as pl
from jax.experimental.pallas import tpu as pltpu

F32 = jnp.float32
BF16 = jnp.bfloat16

HEAD_DIM = 64
LN_EPS = 1e-5
ADALN_EPS = 1e-6
GN_EPS = 64e-5
RWKV_CHUNK = 64
RWKV_SECTIONS = 2
RWKV_SECTION_CHUNKS = 4
RWKV_SECTION_LAG = 12
RWKV_CHUNKS_PER_STEP = RWKV_SECTIONS * RWKV_SECTION_CHUNKS
SB_BLOCK = 128
INPROJ_ROWS = 256
MLP_ROWS = 1024
VMEM_LIMIT = 56 * 1024 * 1024


def _split2(a):
    hi = a.astype(BF16)
    lo = (a - hi.astype(F32)).astype(BF16)
    return hi, lo


_NN = (((1,), (0,)), ((), ()))
_NT = (((1,), (1,)), ((), ()))
_TN = (((0,), (0,)), ((), ()))


def _dot(a, b, dims=_NN):
    return lax.dot_general(a, b, dims, preferred_element_type=F32)


def _dot3(a, b, dims=_NN):
    ah, al = _split2(a)
    bh, bl = _split2(b)
    return _dot(ah, bh, dims) + (_dot(al, bh, dims) + _dot(ah, bl, dims))


def _head_sums(x, ones_blk):
    m, w = x.shape[0], ones_blk.shape[0]
    hi, lo = _split2(x)
    outs = []
    for g in range(x.shape[1] // w):
        cols = slice(g * w, (g + 1) * w)
        both = _dot(jnp.concatenate([hi[:, cols], lo[:, cols]], axis=0), ones_blk)
        outs.append(both[:m] + both[m:])
    return jnp.concatenate(outs, axis=1)


def _dot_ones_lhs(a_bf, b):
    hi, lo = _split2(b)
    return _dot(a_bf, hi) + _dot(a_bf, lo)


_DONE = object()


def _run_skewed(programs, lag):
    live = [True] * len(programs)
    tick = 0
    while any(live):
        for i, prog in enumerate(programs):
            if live[i] and tick >= i * lag:
                live[i] = next(prog, _DONE) is not _DONE
        tick += 1


_NEG_LOG2_E = -1.4426950408889634


def _softplus(y):
    return jnp.maximum(y, 0.0) + jnp.log(1.0 + jnp.exp2(jnp.abs(y) * _NEG_LOG2_E))


def _sigmoid(y):
    return 1.0 / (1.0 + jnp.exp(-y))


def _norm_rows(x, eps):
    mu = jnp.mean(x, axis=-1, keepdims=True)
    xc = x - mu
    var = jnp.mean(xc * xc, axis=-1, keepdims=True)
    return xc * lax.rsqrt(var + eps)


def _ada_kernel(c_ref, w_ref, b_ref, o_ref):
    c = c_ref[...]
    s = c * _sigmoid(c)
    o_ref[...] = jnp.dot(s, w_ref[...], preferred_element_type=F32,
                         precision=lax.Precision.HIGHEST) + b_ref[...]


def _ada(c, w_ada, b_ada):
    bsz, d = c.shape
    n = w_ada.shape[1]
    return pl.pallas_call(
        _ada_kernel,
        grid=(n // d,),
        in_specs=[pl.BlockSpec((bsz, d), lambda j: (0, 0)),
                  pl.BlockSpec((d, d), lambda j: (0, j)),
                  pl.BlockSpec((1, d), lambda j: (0, j))],
        out_specs=pl.BlockSpec((bsz, d), lambda j: (0, j)),
        out_shape=jax.ShapeDtypeStruct((bsz, n), F32),
        name="ada",
    )(c, w_ada, b_ada.reshape(1, n))


def _inproj_kernel(x_ref, ada_ref, win_ref, mu_ref, w0_ref, w2_ref, a0_ref, a2_ref, g2_ref,
                   kkw_ref, kaw_ref, hsum_ref,
                   r_ref, k_ref, v_ref, lw_ref, kk_ref, b_ref, g_ref,
                   q_ref, ks_ref, vs_ref, ga_ref, gb_ref,
                   carry_ref, *, tiles_per_batch, d_rwkv, d_decay, d_aaa, d_gate, d_sb, d_model):
    i = pl.program_id(0)
    tm = x_ref.shape[0]
    d_shift = 3 * d_rwkv + d_decay + d_aaa + d_gate

    sh = ada_ref[0, 0:1, :]
    sc = ada_ref[0, 1:2, :]
    h = _norm_rows(x_ref[...], ADALN_EPS) * (1.0 + sc) + sh
    hb = h.astype(BF16)

    @pl.when(i % tiles_per_batch == 0)
    def _():
        carry_ref[...] = jnp.zeros_like(carry_ref)

    row0 = lax.broadcasted_iota(jnp.int32, (tm, 1), 0) == 0

    def shifted(lo, width):
        cols = slice(lo, lo + width)
        z = _dot(hb, win_ref[:, cols])
        prev = jnp.where(row0, carry_ref[:, cols], pltpu.roll(z, 1, 0))
        carry_ref[:, cols] = z[tm - 1:tm, :]
        return z + mu_ref[:, cols] * (prev - z)

    half = d_model // 2
    pair_w = 2 * HEAD_DIM
    first_head = lax.broadcasted_iota(jnp.int32, (1, pair_w), 1) < HEAD_DIM

    def store_queries(z):
        zq = (z * (HEAD_DIM ** -0.5)).astype(BF16)
        zero = jnp.zeros((), BF16)
        for p in range(d_sb // pair_w):
            qp = zq[:, p * pair_w:(p + 1) * pair_w]
            q_ref[:, (2 * p) * pair_w:(2 * p + 1) * pair_w] = jnp.where(first_head, qp, zero)
            q_ref[:, (2 * p + 1) * pair_w:(2 * p + 2) * pair_w] = jnp.where(first_head, zero, qp)

    def plain(ref, cols, fn=lambda z: z):
        def store(z):
            ref[:, cols] = fn(z).astype(BF16)
        return store

    rest = [(store_queries, d_sb), (plain(ks_ref, slice(None)), d_sb), (plain(vs_ref, slice(None)), d_sb),
            (plain(ga_ref, slice(0, half), _sigmoid), half), (plain(ga_ref, slice(half, d_model), _sigmoid), half),
            (plain(gb_ref, slice(0, half), _sigmoid), half), (plain(gb_ref, slice(half, d_model), _sigmoid), half)]
    rest_col = [d_shift]

    def project_next():
        store, width = rest.pop(0)
        store(_dot(hb, win_ref[:, rest_col[0]:rest_col[0] + width]))
        rest_col[0] += width

    zl = shifted(3 * d_rwkv, d_decay + d_aaa + d_gate)
    zw = zl[:, :d_decay]
    za = zl[:, d_decay:d_decay + d_aaa]
    zg = zl[:, d_decay + d_aaa:]
    k = shifted(d_rwkv, d_rwkv)
    ww = w0_ref[...] + _dot(jnp.tanh(zw).astype(BF16), w2_ref[...])
    w_log = -_softplus(-ww) - 0.5
    lw_ref[...] = -jnp.exp(w_log)
    project_next()
    arate = _sigmoid(a0_ref[...] + _dot(za.astype(BF16), a2_ref[...]))
    g_ref[...] = _dot(_sigmoid(zg).astype(BF16), g2_ref[...])
    r_ref[...] = shifted(0, d_rwkv)
    kkraw = k * kkw_ref[...]
    ssq = _head_sums(kkraw * kkraw, hsum_ref[...])
    kk = kkraw / jnp.maximum(jnp.sqrt(ssq), 1e-12)
    v_ref[...] = shifted(2 * d_rwkv, d_rwkv)
    k_ref[...] = k * (1.0 + (arate - 1.0) * kaw_ref[...])
    project_next()
    kk_ref[...] = kk
    b_ref[...] = kk * arate
    while rest:
        project_next()


def _inproj(x2, ada3, win_bf, mu_shift, rw_w0, rw_w2, rw_a0, rw_a2, rw_g2, rw_kk, rw_ka, hsum,
            *, seq, tm):
    m, d_model = x2.shape
    d_rwkv = rw_w0.shape[-1]
    d_decay, d_aaa, d_gate = rw_w2.shape[0], rw_a2.shape[0], rw_g2.shape[0]
    d_shift = 3 * d_rwkv + d_decay + d_aaa + d_gate
    d_sb = (win_bf.shape[1] - d_shift - 2 * d_model) // 3
    tiles_per_batch = seq // tm
    row = lambda i: (i, 0)
    const = lambda i: (0, 0)
    full = lambda a: pl.BlockSpec(a.shape, const, pipeline_mode=pl.Buffered(1))
    vec = lambda a: a.reshape(1, -1)
    ins = [x2, ada3, win_bf, vec(mu_shift), vec(rw_w0), rw_w2.astype(BF16), vec(rw_a0),
           rw_a2.astype(BF16), rw_g2.astype(BF16), vec(rw_kk), vec(rw_ka), hsum]
    in_specs = [pl.BlockSpec((tm, d_model), row),
                pl.BlockSpec((1,) + ada3.shape[1:], lambda i: (i // tiles_per_batch, 0, 0))]
    in_specs += [full(a) for a in ins[2:]]
    widths = [d_rwkv] * 7 + [2 * d_sb, d_sb, d_sb] + [d_model] * 2
    dtypes = [F32] * 7 + [BF16] * 5
    out_shape = [jax.ShapeDtypeStruct((m, w), dt) for w, dt in zip(widths, dtypes)]
    out_specs = [pl.BlockSpec((tm, w), row) for w in widths]
    kern = functools.partial(_inproj_kernel, tiles_per_batch=tiles_per_batch, d_rwkv=d_rwkv,
                             d_decay=d_decay, d_aaa=d_aaa, d_gate=d_gate, d_sb=d_sb, d_model=d_model)
    return pl.pallas_call(
        kern,
        grid=(m // tm,),
        in_specs=in_specs,
        out_specs=out_specs,
        out_shape=out_shape,
        scratch_shapes=[pltpu.VMEM((1, d_shift), F32)],
        compiler_params=pltpu.CompilerParams(dimension_semantics=("arbitrary",),
                                             vmem_limit_bytes=VMEM_LIMIT),
        name="inproj",
    )(*ins)


RWKV_GROUP_HEADS = 4


def _block_diag(q, head_masks):
    zero = jnp.zeros((), q.dtype)
    return jnp.concatenate([jnp.where(m, q, zero) for m in head_masks], axis=0)


def _packed_dot1(lhs, q, head_masks, dims=_NN):
    return _dot(lhs.astype(BF16), _block_diag(q.astype(BF16), head_masks), dims)


def _packed_dot_wide_lhs(lhs, q, head_masks, dims=_NN):
    m = lhs.shape[0]
    lh, ll = _split2(lhs)
    both = _dot(jnp.concatenate([lh, ll], axis=0), _block_diag(q.astype(BF16), head_masks), dims)
    return both[:m] + both[m:]


def _rwkv_kernel(r_ref, k_ref, v_ref, lw_ref, kk_ref, b_ref, g_ref, rk_ref, gng_ref, gnb_ref, hsum_ref,
                 o_ref, state_ref, *, chunk):
    c = pl.program_id(1)
    n = HEAD_DIM
    rows, d = r_ref.shape[1], r_ref.shape[2]
    gw = RWKV_GROUP_HEADS * n
    n_groups = d // gw

    @pl.when(c == 0)
    def _():
        state_ref[...] = jnp.zeros_like(state_ref)

    t_idx = lax.broadcasted_iota(jnp.int32, (chunk, gw), 0)
    lane = lax.broadcasted_iota(jnp.int32, (chunk, gw), 1)
    i_idx = lane & (n - 1)
    strict = t_idx > i_idx
    incl = t_idx >= i_idx
    lane1 = lax.broadcasted_iota(jnp.int32, (1, gw), 1)
    hm = [(lane1 >> (n.bit_length() - 1)) == h for h in range(RWKV_GROUP_HEADS)]
    same = lambda s: (t_idx >> s) == (i_idx >> s)

    half_w = gw // 2
    lane_half = [(lane1 >> (half_w.bit_length() - 1)) == j for j in range(2)]
    second_in_half = ((lane >> (n.bit_length() - 1)) & 1) == 1

    def halves_rows(x):
        return jnp.concatenate([x[:, :half_w], x[:, half_w:]], axis=0).astype(BF16)

    def halves_masked(w):
        return _block_diag(w.astype(BF16), lane_half)

    def same_head(out):
        return jnp.where(second_in_half, out[n:], out[:n])

    lane_h = lax.broadcasted_iota(jnp.int32, (1, half_w), 1)
    in_half = [(lane_h >> (n.bit_length() - 1)) == j for j in range(half_w // n)]

    def two_rhs(lhs, x, y, dims=_NN):
        lb, xb, yb = lhs.astype(BF16), x.astype(BF16), y.astype(BF16)
        out_x, out_y = [], []
        for j in range(gw // half_w):
            cols = slice(j * half_w, (j + 1) * half_w)
            w = jnp.concatenate([_block_diag(xb[:, cols], in_half), _block_diag(yb[:, cols], in_half)],
                                axis=1 if dims == _NN else 0)
            o = _dot(lb[:, cols], w, dims)
            out_x.append(o[:, :half_w])
            out_y.append(o[:, half_w:])
        return jnp.concatenate(out_x, axis=1), jnp.concatenate(out_y, axis=1)

    sec_rows = rows // RWKV_SECTIONS
    sec_chunks = sec_rows // chunk
    rr = lax.broadcasted_iota(jnp.int32, (sec_rows, sec_rows), 0)
    cc = lax.broadcasted_iota(jnp.int32, (sec_rows, sec_rows), 1)
    cs = chunk.bit_length() - 1
    ltri = jnp.where((rr >= cc) & ((rr >> cs) == (cc >> cs)), 1.0, 0.0).astype(BF16)
    eye = jnp.where(t_idx == i_idx, 1.0, 0.0)
    chains = [(ci, gi) for ci in range(sec_chunks) for gi in range(n_groups)]
    win = lambda x, ci, gi: x[ci * chunk:(ci + 1) * chunk, gi * gw:(gi + 1) * gw]

    def section(row0):
        rs = slice(row0, row0 + sec_rows)
        r, k, v, lw = r_ref[0, rs, :], k_ref[0, rs, :], v_ref[0, rs, :], lw_ref[0, rs, :]
        kk, b = kk_ref[0, rs, :], b_ref[0, rs, :]
        cum = _dot_ones_lhs(ltri, lw)
        a_t = -kk * jnp.exp(cum - lw)
        r_t = r * jnp.exp(cum)
        e_neg = jnp.exp(-cum)
        b_t = b * e_neg
        k_t = k * e_neg
        tots = [cum[(ci + 1) * chunk - 1:(ci + 1) * chunk, :] for ci in range(sec_chunks)]
        e_bars = [jnp.exp(tots[ci] - cum[ci * chunk:(ci + 1) * chunk, :]) for ci in range(sec_chunks)]
        b_bar = [win(b, ci, gi) * e_bars[ci][:, gi * gw:(gi + 1) * gw] for ci, gi in chains]
        k_bar = [win(k, ci, gi) * e_bars[ci][:, gi * gw:(gi + 1) * gw] for ci, gi in chains]
        at = [win(a_t, ci, gi) for ci, gi in chains]
        rt = [win(r_t, ci, gi) for ci, gi in chains]
        vv = [win(v, ci, gi) for ci, gi in chains]
        yield

        ar = [jnp.concatenate([a_, r_], axis=0) for a_, r_ in zip(at, rt)]
        grams = [two_rhs(x, win(b_t, ci, gi), win(k_t, ci, gi), _NT) for x, (ci, gi) in zip(ar, chains)]
        gram_b = [x[0] for x in grams]
        gram_k = [x[1] for x in grams]
        yield
        m_ab = [jnp.where(strict, g_[:chunk], 0.0) for g_ in gram_b]
        p_rb = [jnp.where(incl, g_[chunk:], 0.0) for g_ in gram_b]
        m_ak = [jnp.where(strict, g_[:chunk], 0.0) for g_ in gram_k]
        p_rk = [jnp.where(incl, g_[chunk:], 0.0) for g_ in gram_k]
        yield

        ts = [eye + jnp.where(same(1), m, 0.0) for m in m_ab]
        s = 1
        while (1 << s) < chunk:
            level = same(s + 1) & jnp.logical_not(same(s))
            offs = [jnp.where(level, m, 0.0) for m in m_ab]
            mids = [_packed_dot1(t, o, hm) for t, o in zip(ts, offs)]
            yield
            ts = [t + _packed_dot1(md, t, hm) for t, md in zip(ts, mids)]
            yield
            s += 1

        both = [_packed_dot1(jnp.concatenate([m, pk], axis=0), x, hm) for m, pk, x in zip(m_ak, p_rk, vv)]
        mv = [x[:chunk] for x in both]
        pkv = [x[chunk:] for x in both]
        yield
        hat = [two_rhs(t, x, y) for t, x, y in zip(ts, at, mv)]
        a_hat = [x[0] for x in hat]
        u0 = [x[1] for x in hat]
        yield
        pb = [two_rhs(p, a, u) for p, a, u in zip(p_rb, a_hat, u0)]
        r_hat = [x + y[0] for x, y in zip(rt, pb)]
        y0 = [x + y[1] for x, y in zip(pkv, pb)]
        yield
        g_mat = [same_head(_dot(halves_rows(a), halves_masked(bb), _TN))
                 for a, bb in zip(a_hat, b_bar)]
        h_mat = [same_head(_dot(jnp.concatenate([halves_rows(u), halves_rows(x)], axis=0),
                                jnp.concatenate([halves_masked(bb), halves_masked(kb)], axis=0), _TN))
                 for u, x, bb, kb in zip(u0, vv, b_bar, k_bar)]
        yield

        y_rows = []
        for ci in range(sec_chunks):
            y_groups = []
            for gi in range(n_groups):
                j = ci * n_groups + gi
                s0 = state_ref[gi]
                y_groups.append(_packed_dot1(r_hat[j], s0, hm, _NT) + y0[j])
                w_tot = jnp.exp(tots[ci][:, gi * gw:(gi + 1) * gw])
                state_ref[gi] = s0 * w_tot + _packed_dot_wide_lhs(s0, g_mat[j], hm) + h_mat[j]
            y_rows.append(jnp.concatenate(y_groups, axis=1))
            yield
        y = jnp.concatenate(y_rows, axis=0)

        hsum = hsum_ref[...]
        inv_n = 1.0 / n
        mu = _head_sums(y, hsum) * inv_n
        yc = y - mu
        var = _head_sums(yc * yc, hsum) * inv_n
        yn = yc * lax.rsqrt(var + GN_EPS) * gng_ref[...] + gnb_ref[...]
        bonus = _head_sums(r * k * rk_ref[...], hsum)
        o_ref[0, rs, :] = ((yn + bonus * v) * g_ref[0, rs, :]).astype(o_ref.dtype)

    assert RWKV_SECTION_LAG >= sec_chunks
    _run_skewed([section(si * sec_rows) for si in range(RWKV_SECTIONS)], lag=RWKV_SECTION_LAG)


def _rwkv(r, k, v, lw, kk, b, g, rw_rk, gn_g, gn_b, hsum, *, chunk, rows):
    bsz, seq, d = r.shape
    n_groups = d // (RWKV_GROUP_HEADS * HEAD_DIM)
    blk = pl.BlockSpec((1, rows, d), lambda bi, ci: (bi, ci, 0))
    const = lambda bi, ci: (0, 0)
    vec = lambda a: a.reshape(1, d)
    return pl.pallas_call(
        functools.partial(_rwkv_kernel, chunk=chunk),
        grid=(bsz, seq // rows),
        in_specs=[blk] * 7 + [pl.BlockSpec((1, d), const)] * 3 + [pl.BlockSpec(hsum.shape, const)],
        out_specs=blk,
        out_shape=jax.ShapeDtypeStruct((bsz, seq, d), BF16),
        scratch_shapes=[pltpu.VMEM((n_groups, HEAD_DIM, RWKV_GROUP_HEADS * HEAD_DIM), F32)],
        compiler_params=pltpu.CompilerParams(dimension_semantics=("arbitrary", "arbitrary"),
                                             vmem_limit_bytes=VMEM_LIMIT),
        name="rwkv",
    )(r, k, v, lw, kk, b, g, vec(rw_rk), vec(gn_g), vec(gn_b), hsum)


SB_LOG_UNDERFLOW = 105.0


def _sb_kernel(q_ref, k_ref, v_ref, o_ref, acc_ref, run_ref, *, blk):
    qi = pl.program_id(1)
    pairs = k_ref.shape[2] // (2 * HEAD_DIM)
    row = lax.broadcasted_iota(jnp.int32, (blk, blk), 0)
    col = lax.broadcasted_iota(jnp.int32, (blk, blk), 1)
    causal = col < row
    jj = lax.broadcasted_iota(jnp.int32, (blk, 2 * blk), 0)
    ss = lax.broadcasted_iota(jnp.int32, (blk, 2 * blk), 1)
    suffix = jnp.where((jj >= ss) | (ss >= blk), 1.0, 0.0).astype(BF16)
    first_head = lax.broadcasted_iota(jnp.int32, (1, 2 * HEAD_DIM), 1) < HEAD_DIM
    zero = jnp.zeros((), BF16)

    lanes = [slice(p * 2 * HEAD_DIM, (p + 1) * 2 * HEAD_DIM) for p in range(pairs)]
    heads = [(p, hh) for p in range(pairs) for hh in range(2)]

    def tile(kb, diagonal):
        start = pl.multiple_of(kb * blk, blk)
        pw = 2 * HEAD_DIM
        zs = []
        for p in range(pairs):
            q2 = jnp.concatenate([q_ref[0, :, (2 * p + hh) * pw:(2 * p + hh + 1) * pw] for hh in range(2)],
                                 axis=0)
            z2 = _dot(q2, k_ref[0, pl.ds(start, blk), lanes[p]], _NT)
            zs += [z2[:blk], z2[blk:]]
        sps = [_softplus(z) for z in zs]
        if diagonal:
            sps = [jnp.where(causal, sp, 0.0) for sp in sps]
        css = []
        for p in range(pairs):
            c2 = _dot(jnp.concatenate(sps[2 * p:2 * p + 2], axis=0).astype(BF16), suffix)
            css += [c2[:blk], c2[blk:]]
        atts = []
        low = None
        for n_, (z, cs) in enumerate(zip(zs, css)):
            logw = z - cs[:, :blk]
            if diagonal:
                att = jnp.where(causal, jnp.exp(logw), 0.0)
                run = cs[:, blk:]
            else:
                prev = run_ref[n_]
                att = jnp.exp(logw - prev)
                run = prev + cs[:, blk:]
            run_ref[n_] = run
            low = run if low is None else jnp.minimum(low, run)
            atts.append(att.astype(BF16))
        done = (jnp.min(low) > SB_LOG_UNDERFLOW).astype(jnp.int32)
        for p in range(pairs):
            vp = v_ref[0, pl.ds(start, blk), lanes[p]]
            vv = jnp.concatenate([jnp.where(first_head, vp, zero),
                                  jnp.where(first_head, zero, vp)], axis=0)
            out = _dot(jnp.concatenate(atts[2 * p:2 * p + 2], axis=1), vv)
            if diagonal:
                acc_ref[:, lanes[p]] = out
            else:
                acc_ref[:, lanes[p]] += out
        return done

    done0 = tile(qi, True)

    def cond(c):
        i, done = c
        return jnp.logical_and(i <= qi, done == 0)

    def body(c):
        i, _ = c
        return i + 1, tile(qi - i, False)

    lax.while_loop(cond, body, (jnp.int32(1), done0))
    o_ref[0] = acc_ref[...].astype(o_ref.dtype)


def _sb_attention(q, k, v, *, blk):
    bsz, seq, d = k.shape
    heads = d // HEAD_DIM
    qspec = pl.BlockSpec((1, blk, 2 * d), lambda b, i: (b, i, 0))
    ospec = pl.BlockSpec((1, blk, d), lambda b, i: (b, i, 0))
    kvspec = pl.BlockSpec((1, seq, d), lambda b, i: (b, 0, 0), pipeline_mode=pl.Buffered(1))
    return pl.pallas_call(
        functools.partial(_sb_kernel, blk=blk),
        grid=(bsz, seq // blk),
        in_specs=[qspec, kvspec, kvspec],
        out_specs=ospec,
        out_shape=jax.ShapeDtypeStruct((bsz, seq, d), BF16),
        scratch_shapes=[pltpu.VMEM((blk, d), F32), pltpu.VMEM((heads, blk, blk), F32)],
        compiler_params=pltpu.CompilerParams(dimension_semantics=("arbitrary", "arbitrary"),
                                             vmem_limit_bytes=VMEM_LIMIT),
        name="sbattn",
    )(q, k, v)


def _merge_kernel(x_ref, ada_ref, ya_ref, ob_ref, ga_ref, gb_ref, wa_ref, wb_ref, wo_ref,
                  lng_ref, lnb_ref, o_ref, *, alpha):
    g1 = ada_ref[0, 2:3, :]
    half = x_ref.shape[0] // 2

    def rows_program(rows):
        y_a = _dot(ya_ref[rows, :], wa_ref[...])
        y_b = _dot(ob_ref[rows, :], wb_ref[...])
        yield
        merged = ga_ref[rows, :].astype(F32) * y_a + gb_ref[rows, :].astype(F32) * y_b
        mix = _dot(merged.astype(BF16), wo_ref[...])
        yield
        o_ref[rows, :] = _norm_rows(alpha * x_ref[rows, :] + g1 * mix, LN_EPS) * lng_ref[...] + lnb_ref[...]

    _run_skewed([rows_program(slice(0, half)), rows_program(slice(half, 2 * half))], lag=1)


def _merge(x2, ada3, ya, ob, ga, gb, wa_bf, wb_bf, wo_bf, ln_g, ln_b, *, seq, tm, alpha):
    m, d = x2.shape
    tiles_per_batch = seq // tm
    row = lambda i: (i, 0)
    const = lambda i: (0, 0)
    rows = lambda a: pl.BlockSpec((tm, a.shape[1]), row)
    full = lambda a: pl.BlockSpec(a.shape, const, pipeline_mode=pl.Buffered(1))
    vec = lambda a: a.reshape(1, -1)
    ins = [x2, ada3, ya, ob, ga, gb, wa_bf, wb_bf, wo_bf, vec(ln_g), vec(ln_b)]
    in_specs = [rows(x2), pl.BlockSpec((1,) + ada3.shape[1:], lambda i: (i // tiles_per_batch, 0, 0)),
                rows(ya), rows(ob), rows(ga), rows(gb)] + [full(a) for a in ins[6:]]
    return pl.pallas_call(
        functools.partial(_merge_kernel, alpha=alpha),
        grid=(m // tm,),
        in_specs=in_specs,
        out_specs=pl.BlockSpec((tm, d), row),
        out_shape=jax.ShapeDtypeStruct((m, d), F32),
        compiler_params=pltpu.CompilerParams(dimension_semantics=("arbitrary",),
                                             vmem_limit_bytes=VMEM_LIMIT),
        name="merge",
    )(*ins)


def _ffn_kernel(x_ref, ada_ref, w1_ref, b1_ref, w2_ref, b2_ref, lng_ref, lnb_ref, o_ref, *, alpha, fchunk):
    sh = ada_ref[0, 3:4, :]
    sc = ada_ref[0, 4:5, :]
    g2 = ada_ref[0, 5:6, :]
    d_ff = w1_ref.shape[1]
    n_chunks = d_ff // fchunk
    half = x_ref.shape[0] // 2

    def rows_program(rows):
        x = x_ref[rows, :]
        hb = (_norm_rows(x, ADALN_EPS) * (1.0 + sc) + sh).astype(BF16)
        yield
        ff = None
        for j in range(n_chunks):
            sl = slice(j * fchunk, (j + 1) * fchunk)
            t = jnp.maximum(_dot(hb, w1_ref[:, sl]) + b1_ref[:, sl], 0.0)
            part = _dot((t * t).astype(BF16), w2_ref[sl, :])
            ff = part if ff is None else ff + part
            yield
        ff = ff + b2_ref[...]
        o_ref[rows, :] = _norm_rows(alpha * x + g2 * ff, LN_EPS) * lng_ref[...] + lnb_ref[...]

    _run_skewed([rows_program(slice(0, half)), rows_program(slice(half, 2 * half))], lag=2)


def _ffn(x1, ada3, w1_bf, b1, w2_bf, b2, ln_g, ln_b, *, seq, tm, alpha, fchunk):
    m, d = x1.shape
    tiles_per_batch = seq // tm
    row = lambda i: (i, 0)
    const = lambda i: (0, 0)
    full = lambda a: pl.BlockSpec(a.shape, const, pipeline_mode=pl.Buffered(1))
    vec = lambda a: a.reshape(1, -1)
    ins = [x1, ada3, w1_bf, vec(b1), w2_bf, vec(b2), vec(ln_g), vec(ln_b)]
    in_specs = [pl.BlockSpec((tm, d), row),
                pl.BlockSpec((1,) + ada3.shape[1:], lambda i: (i // tiles_per_batch, 0, 0))]
    in_specs += [full(a) for a in ins[2:]]
    return pl.pallas_call(
        functools.partial(_ffn_kernel, alpha=alpha, fchunk=fchunk),
        grid=(m // tm,),
        in_specs=in_specs,
        out_specs=pl.BlockSpec((tm, d), row),
        out_shape=jax.ShapeDtypeStruct((m, d), F32),
        compiler_params=pltpu.CompilerParams(dimension_semantics=("arbitrary",),
                                             vmem_limit_bytes=VMEM_LIMIT),
        name="ffn",
    )(*ins)


def _branches(proj, rw_rk, rw_gn_g, rw_gn_b, hsum, bsz, seq):
    r, k, v, lw, kk, b, g, q, ks, vs = proj
    seq3 = lambda a: a.reshape(bsz, seq, -1)
    ya = _rwkv(seq3(r), seq3(k), seq3(v), seq3(lw), seq3(kk), seq3(b), seq3(g),
               rw_rk, rw_gn_g, rw_gn_b, hsum, chunk=min(RWKV_CHUNK, seq),
               rows=min(RWKV_CHUNKS_PER_STEP * RWKV_CHUNK, seq))
    o = _sb_attention(seq3(q), seq3(ks), seq3(vs), blk=min(SB_BLOCK, seq))
    return ya.reshape(bsz * seq, -1), o.reshape(bsz * seq, -1)


def _layer(x, c, w_ada, b_ada, w_in, mu_shift, rw_w0, rw_w2, rw_a0, rw_a2, rw_g2, rw_kk, rw_ka,
           rw_rk, rw_gn_g, rw_gn_b, w_branch_a, w_branch_b, w_out, ln1_g, ln1_b,
           w_ff1, b_ff1, w_ff2, b_ff2, ln2_g, ln2_b, *, alpha):
    bsz, seq, d = x.shape
    d_rwkv = rw_w0.shape[-1]
    tm = min(INPROJ_ROWS, seq)
    tm2 = min(MLP_ROWS, seq)

    ada3 = _ada(c, w_ada, b_ada).reshape(bsz, 6, d)
    lane = jnp.arange(RWKV_GROUP_HEADS * HEAD_DIM) // HEAD_DIM
    hsum = (lane[:, None] == lane[None, :]).astype(BF16)

    x2 = x.reshape(bsz * seq, d)
    (r, k, v, lw, kk, b, g, q, ks, vs, ga, gb) = _inproj(
        x2, ada3, w_in.astype(BF16), mu_shift, rw_w0, rw_w2, rw_a0, rw_a2, rw_g2, rw_kk, rw_ka, hsum,
        seq=seq, tm=tm)

    ya, ob = _branches((r, k, v, lw, kk, b, g, q, ks, vs), rw_rk, rw_gn_g, rw_gn_b, hsum, bsz, seq)

    x1 = _merge(x2, ada3, ya, ob, ga, gb,
                w_branch_a.astype(BF16), w_branch_b.astype(BF16), w_out.astype(BF16),
                ln1_g, ln1_b, seq=seq, tm=tm2, alpha=alpha)
    out = _ffn(x1, ada3, w_ff1.astype(BF16), b_ff1, w_ff2.astype(BF16), b_ff2, ln2_g, ln2_b,
               seq=seq, tm=tm2, alpha=alpha, fchunk=min(1024, w_ff1.shape[-1]))
    return out.reshape(bsz, seq, d)


def kernel(x, c, w_ada, b_ada, w_in, mu_shift, rw_w0, rw_w2, rw_a0, rw_a2, rw_g2, rw_kk, rw_ka, rw_rk,
           rw_gn_g, rw_gn_b, w_branch_a, w_branch_b, w_out, ln1_g, ln1_b, w_ff1, b_ff1, w_ff2, b_ff2,
           ln2_g, ln2_b):
    in_dtype = x.dtype
    layer_params = (w_ada, b_ada, w_in, mu_shift, rw_w0, rw_w2, rw_a0, rw_a2, rw_g2, rw_kk, rw_ka,
                    rw_rk, rw_gn_g, rw_gn_b, w_branch_a, w_branch_b, w_out, ln1_g, ln1_b,
                    w_ff1, b_ff1, w_ff2, b_ff2, ln2_g, ln2_b)
    depth = w_ada.shape[0]
    alpha = (2.0 * depth) ** 0.25
    for l in range(depth):
        x = _layer(x, c, *[p[l] for p in layer_params], alpha=alpha)
    return x.astype(in_dtype)
```

```python
import functools

import jax
import jax.numpy as jnp
from jax import lax
from jax.experimental import pallas as pl
from jax.experimental.pallas import tpu as pltpu

F32 = jnp.float32
BF16 = jnp.bfloat16

HEAD_DIM = 64
LN_EPS = 1e-5
ADALN_EPS = 1e-6
GN_EPS = 64e-5
RWKV_CHUNK = 64
RWKV_SECTIONS = 2
RWKV_SECTION_CHUNKS = 4
RWKV_SECTION_LAG = 12
RWKV_CHUNKS_PER_STEP = RWKV_SECTIONS * RWKV_SECTION_CHUNKS
SB_BLOCK = 128
INPROJ_ROWS = 512
MLP_ROWS = 1024
VMEM_LIMIT = 56 * 1024 * 1024


def _split2(a):
    hi = a.astype(BF16)
    lo = (a - hi.astype(F32)).astype(BF16)
    return hi, lo


_NN = (((1,), (0,)), ((), ()))
_NT = (((1,), (1,)), ((), ()))
_TN = (((0,), (0,)), ((), ()))


def _dot(a, b, dims=_NN):
    return lax.dot_general(a, b, dims, preferred_element_type=F32)


def _dot3(a, b, dims=_NN):
    ah, al = _split2(a)
    bh, bl = _split2(b)
    return _dot(ah, bh, dims) + (_dot(al, bh, dims) + _dot(ah, bl, dims))


def _head_sums(x, ones_blk):
    m, w = x.shape[0], ones_blk.shape[0]
    hi, lo = _split2(x)
    outs = []
    for g in range(x.shape[1] // w):
        cols = slice(g * w, (g + 1) * w)
        both = _dot(jnp.concatenate([hi[:, cols], lo[:, cols]], axis=0), ones_blk)
        outs.append(both[:m] + both[m:])
    return jnp.concatenate(outs, axis=1)


def _dot_ones_lhs(a_bf, b):
    hi, lo = _split2(b)
    return _dot(a_bf, hi) + _dot(a_bf, lo)


_DONE = object()


def _run_skewed(programs, lag):
    live = [True] * len(programs)
    tick = 0
    while any(live):
        for i, prog in enumerate(programs):
            if live[i] and tick >= i * lag:
                live[i] = next(prog, _DONE) is not _DONE
        tick += 1


_NEG_LOG2_E = -1.4426950408889634


def _softplus(y):
    return jnp.maximum(y, 0.0) + jnp.log(1.0 + jnp.exp2(jnp.abs(y) * _NEG_LOG2_E))


def _sigmoid(y):
    return 1.0 / (1.0 + jnp.exp(-y))


def _norm_rows(x, eps):
    mu = jnp.mean(x, axis=-1, keepdims=True)
    xc = x - mu
    var = jnp.mean(xc * xc, axis=-1, keepdims=True)
    return xc * lax.rsqrt(var + eps)


def _ada_kernel(c_ref, w_ref, b_ref, o_ref):
    c = c_ref[...]
    s = c * _sigmoid(c)
    o_ref[...] = jnp.dot(s, w_ref[...], preferred_element_type=F32,
                         precision=lax.Precision.HIGHEST) + b_ref[...]


def _ada(c, w_ada, b_ada):
    bsz, d = c.shape
    n = w_ada.shape[1]
    return pl.pallas_call(
        _ada_kernel,
        grid=(n // d,),
        in_specs=[pl.BlockSpec((bsz, d), lambda j: (0, 0)),
                  pl.BlockSpec((d, d), lambda j: (0, j)),
                  pl.BlockSpec((1, d), lambda j: (0, j))],
        out_specs=pl.BlockSpec((bsz, d), lambda j: (0, j)),
        out_shape=jax.ShapeDtypeStruct((bsz, n), F32),
        name="ada",
    )(c, w_ada, b_ada.reshape(1, n))


def _inproj_kernel(x_ref, ada_ref, win_ref, mu_ref, w0_ref, w2_ref, a0_ref, a2_ref, g2_ref,
                   kkw_ref, kaw_ref, hsum_ref,
                   r_ref, k_ref, v_ref, lw_ref, kk_ref, b_ref, g_ref,
                   q_ref, ks_ref, vs_ref, ga_ref, gb_ref,
                   carry_ref, *, tiles_per_batch, d_rwkv, d_decay, d_aaa, d_gate, d_sb, d_model):
    i = pl.program_id(0)
    tm = x_ref.shape[0]
    d_shift = 3 * d_rwkv + d_decay + d_aaa + d_gate

    sh = ada_ref[0, 0:1, :]
    sc = ada_ref[0, 1:2, :]
    h = _norm_rows(x_ref[...], ADALN_EPS) * (1.0 + sc) + sh
    hb = h.astype(BF16)

    @pl.when(i % tiles_per_batch == 0)
    def _():
        carry_ref[...] = jnp.zeros_like(carry_ref)

    row0 = lax.broadcasted_iota(jnp.int32, (tm, 1), 0) == 0

    def shifted(lo, width):
        cols = slice(lo, lo + width)
        z = _dot(hb, win_ref[:, cols])
        prev = jnp.where(row0, carry_ref[:, cols], pltpu.roll(z, 1, 0))
        carry_ref[:, cols] = z[tm - 1:tm, :]
        return z + mu_ref[:, cols] * (prev - z)

    half = d_model // 2
    pair_w = 2 * HEAD_DIM
    first_head = lax.broadcasted_iota(jnp.int32, (1, pair_w), 1) < HEAD_DIM

    def store_queries(z):
        zq = (z * (HEAD_DIM ** -0.5)).astype(BF16)
        zero = jnp.zeros((), BF16)
        for p in range(d_sb // pair_w):
            qp = zq[:, p * pair_w:(p + 1) * pair_w]
            q_ref[:, (2 * p) * pair_w:(2 * p + 1) * pair_w] = jnp.where(first_head, qp, zero)
            q_ref[:, (2 * p + 1) * pair_w:(2 * p + 2) * pair_w] = jnp.where(first_head, zero, qp)

    def plain(ref, cols, fn=lambda z: z):
        def store(z):
            ref[:, cols] = fn(z).astype(BF16)
        return store

    rest = [(store_queries, d_sb), (plain(ks_ref, slice(None)), d_sb), (plain(vs_ref, slice(None)), d_sb),
            (plain(ga_ref, slice(0, half), _sigmoid), half), (plain(ga_ref, slice(half, d_model), _sigmoid), half),
            (plain(gb_ref, slice(0, half), _sigmoid), half), (plain(gb_ref, slice(half, d_model), _sigmoid), half)]
    rest_col = [d_shift]

    def project_next():
        store, width = rest.pop(0)
        store(_dot(hb, win_ref[:, rest_col[0]:rest_col[0] + width]))
        rest_col[0] += width

    zl = shifted(3 * d_rwkv, d_decay + d_aaa + d_gate)
    zw = zl[:, :d_decay]
    za = zl[:, d_decay:d_decay + d_aaa]
    zg = zl[:, d_decay + d_aaa:]
    k = shifted(d_rwkv, d_rwkv)
    ww = w0_ref[...] + _dot(jnp.tanh(zw).astype(BF16), w2_ref[...])
    w_log = -_softplus(-ww) - 0.5
    lw_ref[...] = -jnp.exp(w_log)
    project_next()
    arate = _sigmoid(a0_ref[...] + _dot(za.astype(BF16), a2_ref[...]))
    g_ref[...] = _dot(_sigmoid(zg).astype(BF16), g2_ref[...])
    r_ref[...] = shifted(0, d_rwkv)
    kkraw = k * kkw_ref[...]
    ssq = _head_sums(kkraw * kkraw, hsum_ref[...])
    kk = kkraw / jnp.maximum(jnp.sqrt(ssq), 1e-12)
    v_ref[...] = shifted(2 * d_rwkv, d_rwkv)
    k_ref[...] = k * (1.0 + (arate - 1.0) * kaw_ref[...])
    project_next()
    kk_ref[...] = kk
    b_ref[...] = kk * arate
    while rest:
        project_next()


def _inproj(x2, ada3, win_bf, mu_shift, rw_w0, rw_w2, rw_a0, rw_a2, rw_g2, rw_kk, rw_ka, hsum,
            *, seq, tm):
    m, d_model = x2.shape
    d_rwkv = rw_w0.shape[-1]
    d_decay, d_aaa, d_gate = rw_w2.shape[0], rw_a2.shape[0], rw_g2.shape[0]
    d_shift = 3 * d_rwkv + d_decay + d_aaa + d_gate
    d_sb = (win_bf.shape[1] - d_shift - 2 * d_model) // 3
    tiles_per_batch = seq // tm
    row = lambda i: (i, 0)
    const = lambda i: (0, 0)
    full = lambda a: pl.BlockSpec(a.shape, const, pipeline_mode=pl.Buffered(1))
    vec = lambda a: a.reshape(1, -1)
    ins = [x2, ada3, win_bf, vec(mu_shift), vec(rw_w0), rw_w2.astype(BF16), vec(rw_a0),
           rw_a2.astype(BF16), rw_g2.astype(BF16), vec(rw_kk), vec(rw_ka), hsum]
    in_specs = [pl.BlockSpec((tm, d_model), row),
                pl.BlockSpec((1,) + ada3.shape[1:], lambda i: (i // tiles_per_batch, 0, 0))]
    in_specs += [full(a) for a in ins[2:]]
    widths = [d_rwkv] * 7 + [2 * d_sb, d_sb, d_sb] + [d_model] * 2
    dtypes = [F32] * 7 + [BF16] * 5
    out_shape = [jax.ShapeDtypeStruct((m, w), dt) for w, dt in zip(widths, dtypes)]
    out_specs = [pl.BlockSpec((tm, w), row) for w in widths]
    kern = functools.partial(_inproj_kernel, tiles_per_batch=tiles_per_batch, d_rwkv=d_rwkv,
                             d_decay=d_decay, d_aaa=d_aaa, d_gate=d_gate, d_sb=d_sb, d_model=d_model)
    return pl.pallas_call(
        kern,
        grid=(m // tm,),
        in_specs=in_specs,
        out_specs=out_specs,
        out_shape=out_shape,
        scratch_shapes=[pltpu.VMEM((1, d_shift), F32)],
        compiler_params=pltpu.CompilerParams(dimension_semantics=("arbitrary",),
                                             vmem_limit_bytes=VMEM_LIMIT),
        name="inproj",
    )(*ins)


RWKV_GROUP_HEADS = 4


def _block_diag(q, head_masks):
    zero = jnp.zeros((), q.dtype)
    return jnp.concatenate([jnp.where(m, q, zero) for m in head_masks], axis=0)


def _packed_dot1(lhs, q, head_masks, dims=_NN):
    return _dot(lhs.astype(BF16), _block_diag(q.astype(BF16), head_masks), dims)


def _packed_dot_wide_lhs(lhs, q, head_masks, dims=_NN):
    m = lhs.shape[0]
    lh, ll = _split2(lhs)
    both = _dot(jnp.concatenate([lh, ll], axis=0), _block_diag(q.astype(BF16), head_masks), dims)
    return both[:m] + both[m:]


def _rwkv_kernel(r_ref, k_ref, v_ref, lw_ref, kk_ref, b_ref, g_ref, rk_ref, gng_ref, gnb_ref, hsum_ref,
                 o_ref, state_ref, *, chunk):
    c = pl.program_id(1)
    n = HEAD_DIM
    rows, d = r_ref.shape[1], r_ref.shape[2]
    gw = RWKV_GROUP_HEADS * n
    n_groups = d // gw

    @pl.when(c == 0)
    def _():
        state_ref[...] = jnp.zeros_like(state_ref)

    t_idx = lax.broadcasted_iota(jnp.int32, (chunk, gw), 0)
    lane = lax.broadcasted_iota(jnp.int32, (chunk, gw), 1)
    i_idx = lane & (n - 1)
    strict = t_idx > i_idx
    incl = t_idx >= i_idx
    lane1 = lax.broadcasted_iota(jnp.int32, (1, gw), 1)
    hm = [(lane1 >> (n.bit_length() - 1)) == h for h in range(RWKV_GROUP_HEADS)]
    same = lambda s: (t_idx >> s) == (i_idx >> s)

    half_w = gw // 2
    lane_half = [(lane1 >> (half_w.bit_length() - 1)) == j for j in range(2)]
    second_in_half = ((lane >> (n.bit_length() - 1)) & 1) == 1

    def halves_rows(x):
        return jnp.concatenate([x[:, :half_w], x[:, half_w:]], axis=0).astype(BF16)

    def halves_masked(w):
        return _block_diag(w.astype(BF16), lane_half)

    def same_head(out):
        return jnp.where(second_in_half, out[n:], out[:n])

    lane_h = lax.broadcasted_iota(jnp.int32, (1, half_w), 1)
    in_half = [(lane_h >> (n.bit_length() - 1)) == j for j in range(half_w // n)]

    def two_rhs(lhs, x, y, dims=_NN):
        lb, xb, yb = lhs.astype(BF16), x.astype(BF16), y.astype(BF16)
        out_x, out_y = [], []
        for j in range(gw // half_w):
            cols = slice(j * half_w, (j + 1) * half_w)
            w = jnp.concatenate([_block_diag(xb[:, cols], in_half), _block_diag(yb[:, cols], in_half)],
                                axis=1 if dims == _NN else 0)
            o = _dot(lb[:, cols], w, dims)
            out_x.append(o[:, :half_w])
            out_y.append(o[:, half_w:])
        return jnp.concatenate(out_x, axis=1), jnp.concatenate(out_y, axis=1)

    sec_rows = rows // RWKV_SECTIONS
    sec_chunks = sec_rows // chunk
    rr = lax.broadcasted_iota(jnp.int32, (sec_rows, sec_rows), 0)
    cc = lax.broadcasted_iota(jnp.int32, (sec_rows, sec_rows), 1)
    cs = chunk.bit_length() - 1
    ltri = jnp.where((rr >= cc) & ((rr >> cs) == (cc >> cs)), 1.0, 0.0).astype(BF16)
    eye = jnp.where(t_idx == i_idx, 1.0, 0.0)
    chains = [(ci, gi) for ci in range(sec_chunks) for gi in range(n_groups)]
    win = lambda x, ci, gi: x[ci * chunk:(ci + 1) * chunk, gi * gw:(gi + 1) * gw]

    def section(row0):
        rs = slice(row0, row0 + sec_rows)
        r, k, v, lw = r_ref[0, rs, :], k_ref[0, rs, :], v_ref[0, rs, :], lw_ref[0, rs, :]
        kk, b = kk_ref[0, rs, :], b_ref[0, rs, :]
        cum = _dot_ones_lhs(ltri, lw)
        a_t = -kk * jnp.exp(cum - lw)
        r_t = r * jnp.exp(cum)
        e_neg = jnp.exp(-cum)
        b_t = b * e_neg
        k_t = k * e_neg
        tots = [cum[(ci + 1) * chunk - 1:(ci + 1) * chunk, :] for ci in range(sec_chunks)]
        e_bars = [jnp.exp(tots[ci] - cum[ci * chunk:(ci + 1) * chunk, :]) for ci in range(sec_chunks)]
        b_bar = [win(b, ci, gi) * e_bars[ci][:, gi * gw:(gi + 1) * gw] for ci, gi in chains]
        k_bar = [win(k, ci, gi) * e_bars[ci][:, gi * gw:(gi + 1) * gw] for ci, gi in chains]
        at = [win(a_t, ci, gi) for ci, gi in chains]
        rt = [win(r_t, ci, gi) for ci, gi in chains]
        vv = [win(v, ci, gi) for ci, gi in chains]
        yield

        ar = [jnp.concatenate([a_, r_], axis=0) for a_, r_ in zip(at, rt)]
        grams = [two_rhs(x, win(b_t, ci, gi), win(k_t, ci, gi), _NT) for x, (ci, gi) in zip(ar, chains)]
        gram_b = [x[0] for x in grams]
        gram_k = [x[1] for x in grams]
        yield
        m_ab = [jnp.where(strict, g_[:chunk], 0.0) for g_ in gram_b]
        p_rb = [jnp.where(incl, g_[chunk:], 0.0) for g_ in gram_b]
        m_ak = [jnp.where(strict, g_[:chunk], 0.0) for g_ in gram_k]
        p_rk = [jnp.where(incl, g_[chunk:], 0.0) for g_ in gram_k]
        yield

        ts = [eye + jnp.where(same(1), m, 0.0) for m in m_ab]
        s = 1
        while (1 << s) < chunk:
            level = same(s + 1) & jnp.logical_not(same(s))
            offs = [jnp.where(level, m, 0.0) for m in m_ab]
            mids = [_packed_dot1(t, o, hm) for t, o in zip(ts, offs)]
            yield
            ts = [t + _packed_dot1(md, t, hm) for t, md in zip(ts, mids)]
            yield
            s += 1

        both = [_packed_dot1(jnp.concatenate([m, pk], axis=0), x, hm) for m, pk, x in zip(m_ak, p_rk, vv)]
        mv = [x[:chunk] for x in both]
        pkv = [x[chunk:] for x in both]
        yield
        hat = [two_rhs(t, x, y) for t, x, y in zip(ts, at, mv)]
        a_hat = [x[0] for x in hat]
        u0 = [x[1] for x in hat]
        yield
        pb = [two_rhs(p, a, u) for p, a, u in zip(p_rb, a_hat, u0)]
        r_hat = [x + y[0] for x, y in zip(rt, pb)]
        y0 = [x + y[1] for x, y in zip(pkv, pb)]
        yield
        g_mat = [same_head(_dot(halves_rows(a), halves_masked(bb), _TN))
                 for a, bb in zip(a_hat, b_bar)]
        h_mat = [same_head(_dot(jnp.concatenate([halves_rows(u), halves_rows(x)], axis=0),
                                jnp.concatenate([halves_masked(bb), halves_masked(kb)], axis=0), _TN))
                 for u, x, bb, kb in zip(u0, vv, b_bar, k_bar)]
        yield

        y_rows = []
        for ci in range(sec_chunks):
            y_groups = []
            for gi in range(n_groups):
                j = ci * n_groups + gi
                s0 = state_ref[gi]
                y_groups.append(_packed_dot1(r_hat[j], s0, hm, _NT) + y0[j])
                w_tot = jnp.exp(tots[ci][:, gi * gw:(gi + 1) * gw])
                state_ref[gi] = s0 * w_tot + _packed_dot_wide_lhs(s0, g_mat[j], hm) + h_mat[j]
            y_rows.append(jnp.concatenate(y_groups, axis=1))
            yield
        y = jnp.concatenate(y_rows, axis=0)

        hsum = hsum_ref[...]
        inv_n = 1.0 / n
        mu = _head_sums(y, hsum) * inv_n
        yc = y - mu
        var = _head_sums(yc * yc, hsum) * inv_n
        yn = yc * lax.rsqrt(var + GN_EPS) * gng_ref[...] + gnb_ref[...]
        bonus = _head_sums(r * k * rk_ref[...], hsum)
        o_ref[0, rs, :] = ((yn + bonus * v) * g_ref[0, rs, :]).astype(o_ref.dtype)

    assert RWKV_SECTION_LAG >= sec_chunks
    _run_skewed([section(si * sec_rows) for si in range(RWKV_SECTIONS)], lag=RWKV_SECTION_LAG)


def _rwkv(r, k, v, lw, kk, b, g, rw_rk, gn_g, gn_b, hsum, *, chunk, rows):
    bsz, seq, d = r.shape
    n_groups = d // (RWKV_GROUP_HEADS * HEAD_DIM)
    blk = pl.BlockSpec((1, rows, d), lambda bi, ci: (bi, ci, 0))
    const = lambda bi, ci: (0, 0)
    vec = lambda a: a.reshape(1, d)
    return pl.pallas_call(
        functools.partial(_rwkv_kernel, chunk=chunk),
        grid=(bsz, seq // rows),
        in_specs=[blk] * 7 + [pl.BlockSpec((1, d), const)] * 3 + [pl.BlockSpec(hsum.shape, const)],
        out_specs=blk,
        out_shape=jax.ShapeDtypeStruct((bsz, seq, d), BF16),
        scratch_shapes=[pltpu.VMEM((n_groups, HEAD_DIM, RWKV_GROUP_HEADS * HEAD_DIM), F32)],
        compiler_params=pltpu.CompilerParams(dimension_semantics=("arbitrary", "arbitrary"),
                                             vmem_limit_bytes=VMEM_LIMIT),
        name="rwkv",
    )(r, k, v, lw, kk, b, g, vec(rw_rk), vec(gn_g), vec(gn_b), hsum)


SB_LOG_UNDERFLOW = 105.0


def _sb_kernel(q_ref, k_ref, v_ref, o_ref, acc_ref, run_ref, *, blk):
    qi = pl.program_id(1)
    pairs = k_ref.shape[2] // (2 * HEAD_DIM)
    row = lax.broadcasted_iota(jnp.int32, (blk, blk), 0)
    col = lax.broadcasted_iota(jnp.int32, (blk, blk), 1)
    causal = col < row
    jj = lax.broadcasted_iota(jnp.int32, (blk, 2 * blk), 0)
    ss = lax.broadcasted_iota(jnp.int32, (blk, 2 * blk), 1)
    suffix = jnp.where((jj >= ss) | (ss >= blk), 1.0, 0.0).astype(BF16)
    first_head = lax.broadcasted_iota(jnp.int32, (1, 2 * HEAD_DIM), 1) < HEAD_DIM
    zero = jnp.zeros((), BF16)

    lanes = [slice(p * 2 * HEAD_DIM, (p + 1) * 2 * HEAD_DIM) for p in range(pairs)]
    heads = [(p, hh) for p in range(pairs) for hh in range(2)]

    def tile(kb, diagonal):
        start = pl.multiple_of(kb * blk, blk)
        pw = 2 * HEAD_DIM
        zs = []
        for p in range(pairs):
            q2 = jnp.concatenate([q_ref[0, :, (2 * p + hh) * pw:(2 * p + hh + 1) * pw] for hh in range(2)],
                                 axis=0)
            z2 = _dot(q2, k_ref[0, pl.ds(start, blk), lanes[p]], _NT)
            zs += [z2[:blk], z2[blk:]]
        sps = [_softplus(z) for z in zs]
        if diagonal:
            sps = [jnp.where(causal, sp, 0.0) for sp in sps]
        c_all = _dot(jnp.concatenate(sps, axis=0).astype(BF16), suffix)
        css = [c_all[n_ * blk:(n_ + 1) * blk] for n_ in range(len(heads))]
        atts = []
        low = None
        for n_, (z, cs) in enumerate(zip(zs, css)):
            logw = z - cs[:, :blk]
            if diagonal:
                att = jnp.where(causal, jnp.exp(logw), 0.0)
                run = cs[:, blk:]
            else:
                prev = run_ref[n_]
                att = jnp.exp(logw - prev)
                run = prev + cs[:, blk:]
            run_ref[n_] = run
            low = run if low is None else jnp.minimum(low, run)
            atts.append(att.astype(BF16))
        done = (jnp.min(low) > SB_LOG_UNDERFLOW).astype(jnp.int32)
        for p in range(pairs):
            vp = v_ref[0, pl.ds(start, blk), lanes[p]]
            vv = jnp.concatenate([jnp.where(first_head, vp, zero),
                                  jnp.where(first_head, zero, vp)], axis=0)
            out = _dot(jnp.concatenate(atts[2 * p:2 * p + 2], axis=1), vv)
            if diagonal:
                acc_ref[:, lanes[p]] = out
            else:
                acc_ref[:, lanes[p]] += out
        return done

    done0 = tile(qi, True)

    def cond(c):
        i, done = c
        return jnp.logical_and(i <= qi, done == 0)

    def body(c):
        i, _ = c
        return i + 1, tile(qi - i, False)

    lax.while_loop(cond, body, (jnp.int32(1), done0))
    o_ref[0] = acc_ref[...].astype(o_ref.dtype)


def _sb_attention(q, k, v, *, blk):
    bsz, seq, d = k.shape
    heads = d // HEAD_DIM
    qspec = pl.BlockSpec((1, blk, 2 * d), lambda b, i: (b, i, 0))
    ospec = pl.BlockSpec((1, blk, d), lambda b, i: (b, i, 0))
    kvspec = pl.BlockSpec((1, seq, d), lambda b, i: (b, 0, 0), pipeline_mode=pl.Buffered(1))
    return pl.pallas_call(
        functools.partial(_sb_kernel, blk=blk),
        grid=(bsz, seq // blk),
        in_specs=[qspec, kvspec, kvspec],
        out_specs=ospec,
        out_shape=jax.ShapeDtypeStruct((bsz, seq, d), BF16),
        scratch_shapes=[pltpu.VMEM((blk, d), F32), pltpu.VMEM((heads, blk, blk), F32)],
        compiler_params=pltpu.CompilerParams(dimension_semantics=("arbitrary", "arbitrary"),
                                             vmem_limit_bytes=VMEM_LIMIT),
        name="sbattn",
    )(q, k, v)


def _merge_kernel(x_ref, ada_ref, ya_ref, ob_ref, ga_ref, gb_ref, wa_ref, wb_ref, wo_ref,
                  lng_ref, lnb_ref, o_ref, *, alpha):
    g1 = ada_ref[0, 2:3, :]
    half = x_ref.shape[0] // 2

    def rows_program(rows):
        y_a = _dot(ya_ref[rows, :], wa_ref[...])
        y_b = _dot(ob_ref[rows, :], wb_ref[...])
        yield
        merged = ga_ref[rows, :].astype(F32) * y_a + gb_ref[rows, :].astype(F32) * y_b
        mix = _dot(merged.astype(BF16), wo_ref[...])
        yield
        o_ref[rows, :] = _norm_rows(alpha * x_ref[rows, :] + g1 * mix, LN_EPS) * lng_ref[...] + lnb_ref[...]

    _run_skewed([rows_program(slice(0, half)), rows_program(slice(half, 2 * half))], lag=1)


def _merge(x2, ada3, ya, ob, ga, gb, wa_bf, wb_bf, wo_bf, ln_g, ln_b, *, seq, tm, alpha):
    m, d = x2.shape
    tiles_per_batch = seq // tm
    row = lambda i: (i, 0)
    const = lambda i: (0, 0)
    rows = lambda a: pl.BlockSpec((tm, a.shape[1]), row)
    full = lambda a: pl.BlockSpec(a.shape, const, pipeline_mode=pl.Buffered(1))
    vec = lambda a: a.reshape(1, -1)
    ins = [x2, ada3, ya, ob, ga, gb, wa_bf, wb_bf, wo_bf, vec(ln_g), vec(ln_b)]
    in_specs = [rows(x2), pl.BlockSpec((1,) + ada3.shape[1:], lambda i: (i // tiles_per_batch, 0, 0)),
                rows(ya), rows(ob), rows(ga), rows(gb)] + [full(a) for a in ins[6:]]
    return pl.pallas_call(
        functools.partial(_merge_kernel, alpha=alpha),
        grid=(m // tm,),
        in_specs=in_specs,
        out_specs=pl.BlockSpec((tm, d), row),
        out_shape=jax.ShapeDtypeStruct((m, d), F32),
        compiler_params=pltpu.CompilerParams(dimension_semantics=("arbitrary",),
                                             vmem_limit_bytes=VMEM_LIMIT),
        name="merge",
    )(*ins)


def _ffn_kernel(x_ref, ada_ref, w1_ref, b1_ref, w2_ref, b2_ref, lng_ref, lnb_ref, o_ref, *, alpha, fchunk):
    sh = ada_ref[0, 3:4, :]
    sc = ada_ref[0, 4:5, :]
    g2 = ada_ref[0, 5:6, :]
    d_ff = w1_ref.shape[1]
    n_chunks = d_ff // fchunk
    half = x_ref.shape[0] // 2

    def rows_program(rows):
        x = x_ref[rows, :]
        hb = (_norm_rows(x, ADALN_EPS) * (1.0 + sc) + sh).astype(BF16)
        yield
        ff = None
        for j in range(n_chunks):
            sl = slice(j * fchunk, (j + 1) * fchunk)
            t = jnp.maximum(_dot(hb, w1_ref[:, sl]) + b1_ref[:, sl], 0.0)
            part = _dot((t * t).astype(BF16), w2_ref[sl, :])
            ff = part if ff is None else ff + part
            yield
        ff = ff + b2_ref[...]
        o_ref[rows, :] = _norm_rows(alpha * x + g2 * ff, LN_EPS) * lng_ref[...] + lnb_ref[...]

    _run_skewed([rows_program(slice(0, half)), rows_program(slice(half, 2 * half))], lag=2)


def _ffn(x1, ada3, w1_bf, b1, w2_bf, b2, ln_g, ln_b, *, seq, tm, alpha, fchunk):
    m, d = x1.shape
    tiles_per_batch = seq // tm
    row = lambda i: (i, 0)
    const = lambda i: (0, 0)
    full = lambda a: pl.BlockSpec(a.shape, const, pipeline_mode=pl.Buffered(1))
    vec = lambda a: a.reshape(1, -1)
    ins = [x1, ada3, w1_bf, vec(b1), w2_bf, vec(b2), vec(ln_g), vec(ln_b)]
    in_specs = [pl.BlockSpec((tm, d), row),
                pl.BlockSpec((1,) + ada3.shape[1:], lambda i: (i // tiles_per_batch, 0, 0))]
    in_specs += [full(a) for a in ins[2:]]
    return pl.pallas_call(
        functools.partial(_ffn_kernel, alpha=alpha, fchunk=fchunk),
        grid=(m // tm,),
        in_specs=in_specs,
        out_specs=pl.BlockSpec((tm, d), row),
        out_shape=jax.ShapeDtypeStruct((m, d), F32),
        compiler_params=pltpu.CompilerParams(dimension_semantics=("arbitrary",),
                                             vmem_limit_bytes=VMEM_LIMIT),
        name="ffn",
    )(*ins)


def _branches(proj, rw_rk, rw_gn_g, rw_gn_b, hsum, bsz, seq):
    r, k, v, lw, kk, b, g, q, ks, vs = proj
    seq3 = lambda a: a.reshape(bsz, seq, -1)
    ya = _rwkv(seq3(r), seq3(k), seq3(v), seq3(lw), seq3(kk), seq3(b), seq3(g),
               rw_rk, rw_gn_g, rw_gn_b, hsum, chunk=min(RWKV_CHUNK, seq),
               rows=min(RWKV_CHUNKS_PER_STEP * RWKV_CHUNK, seq))
    o = _sb_attention(seq3(q), seq3(ks), seq3(vs), blk=min(SB_BLOCK, seq))
    return ya.reshape(bsz * seq, -1), o.reshape(bsz * seq, -1)


def _layer(x, c, w_ada, b_ada, w_in, mu_shift, rw_w0, rw_w2, rw_a0, rw_a2, rw_g2, rw_kk, rw_ka,
           rw_rk, rw_gn_g, rw_gn_b, w_branch_a, w_branch_b, w_out, ln1_g, ln1_b,
           w_ff1, b_ff1, w_ff2, b_ff2, ln2_g, ln2_b, *, alpha):
    bsz, seq, d = x.shape
    d_rwkv = rw_w0.shape[-1]
    tm = min(INPROJ_ROWS, seq)
    tm2 = min(MLP_ROWS, seq)

    ada3 = _ada(c, w_ada, b_ada).reshape(bsz, 6, d)
    lane = jnp.arange(RWKV_GROUP_HEADS * HEAD_DIM) // HEAD_DIM
    hsum = (lane[:, None] == lane[None, :]).astype(BF16)

    x2 = x.reshape(bsz * seq, d)
    (r, k, v, lw, kk, b, g, q, ks, vs, ga, gb) = _inproj(
        x2, ada3, w_in.astype(BF16), mu_shift, rw_w0, rw_w2, rw_a0, rw_a2, rw_g2, rw_kk, rw_ka, hsum,
        seq=seq, tm=tm)

    ya, ob = _branches((r, k, v, lw, kk, b, g, q, ks, vs), rw_rk, rw_gn_g, rw_gn_b, hsum, bsz, seq)

    x1 = _merge(x2, ada3, ya, ob, ga, gb,
                w_branch_a.astype(BF16), w_branch_b.astype(BF16), w_out.astype(BF16),
                ln1_g, ln1_b, seq=seq, tm=tm2, alpha=alpha)
    out = _ffn(x1, ada3, w_ff1.astype(BF16), b_ff1, w_ff2.astype(BF16), b_ff2, ln2_g, ln2_b,
               seq=seq, tm=tm2, alpha=alpha, fchunk=min(1024, w_ff1.shape[-1]))
    return out.reshape(bsz, seq, d)


def kernel(x, c, w_ada, b_ada, w_in, mu_shift, rw_w0, rw_w2, rw_a0, rw_a2, rw_g2, rw_kk, rw_ka, rw_rk,
           rw_gn_g, rw_gn_b, w_branch_a, w_branch_b, w_out, ln1_g, ln1_b, w_ff1, b_ff1, w_ff2, b_ff2,
           ln2_g, ln2_b):
    in_dtype = x.dtype
    layer_params = (w_ada, b_ada, w_in, mu_shift, rw_w0, rw_w2, rw_a0, rw_a2, rw_g2, rw_kk, rw_ka,
                    rw_rk, rw_gn_g, rw_gn_b, w_branch_a, w_branch_b, w_out, ln1_g, ln1_b,
                    w_ff1, b_ff1, w_ff2, b_ff2, ln2_g, ln2_b)
    depth = w_ada.shape[0]
    alpha = (2.0 * depth) ** 0.25
    for l in range(depth):
        x = _layer(x, c, *[p[l] for p in layer_params], alpha=alpha)
    return x.astype(in_dtype)
```

```python
import functools

import jax
import jax.numpy as jnp
from jax import lax
from jax.experimental import pallas as pl
from jax.experimental.pallas import tpu as pltpu

F32 = jnp.float32
BF16 = jnp.bfloat16

HEAD_DIM = 64
LN_EPS = 1e-5
ADALN_EPS = 1e-6
GN_EPS = 64e-5
RWKV_CHUNK = 64
RWKV_SECTIONS = 2
RWKV_SECTION_CHUNKS = 4
RWKV_SECTION_LAG = 12
RWKV_CHUNKS_PER_STEP = RWKV_SECTIONS * RWKV_SECTION_CHUNKS
SB_BLOCK = 128
INPROJ_ROWS = 512
MLP_ROWS = 1024
VMEM_LIMIT = 56 * 1024 * 1024


def _split2(a):
    hi = a.astype(BF16)
    lo = (a - hi.astype(F32)).astype(BF16)
    return hi, lo


_NN = (((1,), (0,)), ((), ()))
_NT = (((1,), (1,)), ((), ()))
_TN = (((0,), (0,)), ((), ()))


def _dot(a, b, dims=_NN):
    return lax.dot_general(a, b, dims, preferred_element_type=F32)


def _dot3(a, b, dims=_NN):
    ah, al = _split2(a)
    bh, bl = _split2(b)
    return _dot(ah, bh, dims) + (_dot(al, bh, dims) + _dot(ah, bl, dims))


def _head_sums(x, ones_blk):
    m, w = x.shape[0], ones_blk.shape[0]
    hi, lo = _split2(x)
    outs = []
    for g in range(x.shape[1] // w):
        cols = slice(g * w, (g + 1) * w)
        both = _dot(jnp.concatenate([hi[:, cols], lo[:, cols]], axis=0), ones_blk)
        outs.append(both[:m] + both[m:])
    return jnp.concatenate(outs, axis=1)


def _dot_ones_lhs(a_bf, b):
    hi, lo = _split2(b)
    return _dot(a_bf, hi) + _dot(a_bf, lo)


_DONE = object()


def _run_skewed(programs, lag):
    live = [True] * len(programs)
    tick = 0
    while any(live):
        for i, prog in enumerate(programs):
            if live[i] and tick >= i * lag:
                live[i] = next(prog, _DONE) is not _DONE
        tick += 1


_NEG_LOG2_E = -1.4426950408889634


def _softplus(y):
    return jnp.maximum(y, 0.0) + jnp.log(1.0 + jnp.exp2(jnp.abs(y) * _NEG_LOG2_E))


def _sigmoid(y):
    return 1.0 / (1.0 + jnp.exp(-y))


def _norm_rows(x, eps):
    mu = jnp.mean(x, axis=-1, keepdims=True)
    xc = x - mu
    var = jnp.mean(xc * xc, axis=-1, keepdims=True)
    return xc * lax.rsqrt(var + eps)


def _ada_kernel(c_ref, w_ref, b_ref, o_ref):
    c = c_ref[...]
    s = c * _sigmoid(c)
    o_ref[...] = jnp.dot(s, w_ref[...], preferred_element_type=F32,
                         precision=lax.Precision.HIGHEST) + b_ref[...]


def _ada(c, w_ada, b_ada):
    bsz, d = c.shape
    n = w_ada.shape[1]
    return pl.pallas_call(
        _ada_kernel,
        grid=(n // d,),
        in_specs=[pl.BlockSpec((bsz, d), lambda j: (0, 0)),
                  pl.BlockSpec((d, d), lambda j: (0, j)),
                  pl.BlockSpec((1, d), lambda j: (0, j))],
        out_specs=pl.BlockSpec((bsz, d), lambda j: (0, j)),
        out_shape=jax.ShapeDtypeStruct((bsz, n), F32),
        name="ada",
    )(c, w_ada, b_ada.reshape(1, n))


def _inproj_kernel(x_ref, ada_ref, win_ref, mu_ref, w0_ref, w2_ref, a0_ref, a2_ref, g2_ref,
                   kkw_ref, kaw_ref, hsum_ref,
                   r_ref, k_ref, v_ref, lw_ref, kk_ref, b_ref, g_ref,
                   q_ref, ks_ref, vs_ref, ga_ref, gb_ref,
                   carry_ref, *, tiles_per_batch, d_rwkv, d_decay, d_aaa, d_gate, d_sb, d_model):
    i = pl.program_id(0)
    tm = x_ref.shape[0]
    d_shift = 3 * d_rwkv + d_decay + d_aaa + d_gate

    sh = ada_ref[0, 0:1, :]
    sc = ada_ref[0, 1:2, :]
    h = _norm_rows(x_ref[...], ADALN_EPS) * (1.0 + sc) + sh
    hb = h.astype(BF16)

    @pl.when(i % tiles_per_batch == 0)
    def _():
        carry_ref[...] = jnp.zeros_like(carry_ref)

    row0 = lax.broadcasted_iota(jnp.int32, (tm, 1), 0) == 0

    def shifted(lo, width):
        cols = slice(lo, lo + width)
        z = _dot(hb, win_ref[:, cols])
        prev = jnp.where(row0, carry_ref[:, cols], pltpu.roll(z, 1, 0))
        carry_ref[:, cols] = z[tm - 1:tm, :]
        return z + mu_ref[:, cols] * (prev - z)

    half = d_model // 2
    pair_w = 2 * HEAD_DIM
    first_head = lax.broadcasted_iota(jnp.int32, (1, pair_w), 1) < HEAD_DIM

    def store_queries(z):
        zq = (z * (HEAD_DIM ** -0.5)).astype(BF16)
        zero = jnp.zeros((), BF16)
        for p in range(d_sb // pair_w):
            qp = zq[:, p * pair_w:(p + 1) * pair_w]
            q_ref[:, (2 * p) * pair_w:(2 * p + 1) * pair_w] = jnp.where(first_head, qp, zero)
            q_ref[:, (2 * p + 1) * pair_w:(2 * p + 2) * pair_w] = jnp.where(first_head, zero, qp)

    def plain(ref, cols, fn=lambda z: z):
        def store(z):
            ref[:, cols] = fn(z).astype(BF16)
        return store

    rest = [(store_queries, d_sb), (plain(ks_ref, slice(None)), d_sb), (plain(vs_ref, slice(None)), d_sb),
            (plain(ga_ref, slice(0, half), _sigmoid), half), (plain(ga_ref, slice(half, d_model), _sigmoid), half),
            (plain(gb_ref, slice(0, half), _sigmoid), half), (plain(gb_ref, slice(half, d_model), _sigmoid), half)]
    rest_col = [d_shift]

    def project_next():
        store, width = rest.pop(0)
        store(_dot(hb, win_ref[:, rest_col[0]:rest_col[0] + width]))
        rest_col[0] += width

    zl = shifted(3 * d_rwkv, d_decay + d_aaa + d_gate)
    zw = zl[:, :d_decay]
    za = zl[:, d_decay:d_decay + d_aaa]
    zg = zl[:, d_decay + d_aaa:]
    k = shifted(d_rwkv, d_rwkv)
    ww = w0_ref[...] + _dot(jnp.tanh(zw).astype(BF16), w2_ref[...])
    w_log = -_softplus(-ww) - 0.5
    lw_ref[...] = -jnp.exp(w_log)
    project_next()
    arate = _sigmoid(a0_ref[...] + _dot(za.astype(BF16), a2_ref[...]))
    g_ref[...] = _dot(_sigmoid(zg).astype(BF16), g2_ref[...])
    r_ref[...] = shifted(0, d_rwkv)
    kkraw = k * kkw_ref[...]
    ssq = _head_sums(kkraw * kkraw, hsum_ref[...])
    kk = kkraw / jnp.maximum(jnp.sqrt(ssq), 1e-12)
    v_ref[...] = shifted(2 * d_rwkv, d_rwkv)
    k_ref[...] = k * (1.0 + (arate - 1.0) * kaw_ref[...])
    project_next()
    kk_ref[...] = kk
    b_ref[...] = kk * arate
    while rest:
        project_next()


def _inproj(x2, ada3, win_bf, mu_shift, rw_w0, rw_w2, rw_a0, rw_a2, rw_g2, rw_kk, rw_ka, hsum,
            *, seq, tm):
    m, d_model = x2.shape
    d_rwkv = rw_w0.shape[-1]
    d_decay, d_aaa, d_gate = rw_w2.shape[0], rw_a2.shape[0], rw_g2.shape[0]
    d_shift = 3 * d_rwkv + d_decay + d_aaa + d_gate
    d_sb = (win_bf.shape[1] - d_shift - 2 * d_model) // 3
    tiles_per_batch = seq // tm
    row = lambda i: (i, 0)
    const = lambda i: (0, 0)
    full = lambda a: pl.BlockSpec(a.shape, const, pipeline_mode=pl.Buffered(1))
    vec = lambda a: a.reshape(1, -1)
    ins = [x2, ada3, win_bf, vec(mu_shift), vec(rw_w0), rw_w2.astype(BF16), vec(rw_a0),
           rw_a2.astype(BF16), rw_g2.astype(BF16), vec(rw_kk), vec(rw_ka), hsum]
    in_specs = [pl.BlockSpec((tm, d_model), row),
                pl.BlockSpec((1,) + ada3.shape[1:], lambda i: (i // tiles_per_batch, 0, 0))]
    in_specs += [full(a) for a in ins[2:]]
    widths = [d_rwkv] * 7 + [2 * d_sb, d_sb, d_sb] + [d_model] * 2
    dtypes = [F32] * 7 + [BF16] * 5
    out_shape = [jax.ShapeDtypeStruct((m, w), dt) for w, dt in zip(widths, dtypes)]
    out_specs = [pl.BlockSpec((tm, w), row) for w in widths]
    kern = functools.partial(_inproj_kernel, tiles_per_batch=tiles_per_batch, d_rwkv=d_rwkv,
                             d_decay=d_decay, d_aaa=d_aaa, d_gate=d_gate, d_sb=d_sb, d_model=d_model)
    return pl.pallas_call(
        kern,
        grid=(m // tm,),
        in_specs=in_specs,
        out_specs=out_specs,
        out_shape=out_shape,
        scratch_shapes=[pltpu.VMEM((1, d_shift), F32)],
        compiler_params=pltpu.CompilerParams(dimension_semantics=("arbitrary",),
                                             vmem_limit_bytes=VMEM_LIMIT),
        name="inproj",
    )(*ins)


RWKV_GROUP_HEADS = 4


def _block_diag(q, head_masks):
    zero = jnp.zeros((), q.dtype)
    return jnp.concatenate([jnp.where(m, q, zero) for m in head_masks], axis=0)


def _packed_dot1(lhs, q, head_masks, dims=_NN):
    return _dot(lhs.astype(BF16), _block_diag(q.astype(BF16), head_masks), dims)


def _packed_dot_wide_lhs(lhs, q, head_masks, dims=_NN):
    m = lhs.shape[0]
    lh, ll = _split2(lhs)
    both = _dot(jnp.concatenate([lh, ll], axis=0), _block_diag(q.astype(BF16), head_masks), dims)
    return both[:m] + both[m:]


def _rwkv_kernel(r_ref, k_ref, v_ref, lw_ref, kk_ref, b_ref, g_ref, rk_ref, gng_ref, gnb_ref, hsum_ref,
                 o_ref, state_ref, *, chunk):
    c = pl.program_id(1)
    n = HEAD_DIM
    rows, d = r_ref.shape[1], r_ref.shape[2]
    gw = RWKV_GROUP_HEADS * n
    n_groups = d // gw

    @pl.when(c == 0)
    def _():
        state_ref[...] = jnp.zeros_like(state_ref)

    t_idx = lax.broadcasted_iota(jnp.int32, (chunk, gw), 0)
    lane = lax.broadcasted_iota(jnp.int32, (chunk, gw), 1)
    i_idx = lane & (n - 1)
    strict = t_idx > i_idx
    incl = t_idx >= i_idx
    lane1 = lax.broadcasted_iota(jnp.int32, (1, gw), 1)
    hm = [(lane1 >> (n.bit_length() - 1)) == h for h in range(RWKV_GROUP_HEADS)]
    same = lambda s: (t_idx >> s) == (i_idx >> s)

    half_w = gw // 2
    lane_half = [(lane1 >> (half_w.bit_length() - 1)) == j for j in range(2)]
    second_in_half = ((lane >> (n.bit_length() - 1)) & 1) == 1

    def halves_rows(x):
        return jnp.concatenate([x[:, :half_w], x[:, half_w:]], axis=0).astype(BF16)

    def halves_masked(w):
        return _block_diag(w.astype(BF16), lane_half)

    def same_head(out):
        return jnp.where(second_in_half, out[n:], out[:n])

    lane_h = lax.broadcasted_iota(jnp.int32, (1, half_w), 1)
    in_half = [(lane_h >> (n.bit_length() - 1)) == j for j in range(half_w // n)]

    def two_rhs(lhs, x, y, dims=_NN):
        lb, xb, yb = lhs.astype(BF16), x.astype(BF16), y.astype(BF16)
        out_x, out_y = [], []
        for j in range(gw // half_w):
            cols = slice(j * half_w, (j + 1) * half_w)
            w = jnp.concatenate([_block_diag(xb[:, cols], in_half), _block_diag(yb[:, cols], in_half)],
                                axis=1 if dims == _NN else 0)
            o = _dot(lb[:, cols], w, dims)
            out_x.append(o[:, :half_w])
            out_y.append(o[:, half_w:])
        return jnp.concatenate(out_x, axis=1), jnp.concatenate(out_y, axis=1)

    sec_rows = rows // RWKV_SECTIONS
    sec_chunks = sec_rows // chunk
    rr = lax.broadcasted_iota(jnp.int32, (sec_rows, sec_rows), 0)
    cc = lax.broadcasted_iota(jnp.int32, (sec_rows, sec_rows), 1)
    cs = chunk.bit_length() - 1
    ltri = jnp.where((rr >= cc) & ((rr >> cs) == (cc >> cs)), 1.0, 0.0).astype(BF16)
    eye = jnp.where(t_idx == i_idx, 1.0, 0.0)
    chains = [(ci, gi) for ci in range(sec_chunks) for gi in range(n_groups)]
    win = lambda x, ci, gi: x[ci * chunk:(ci + 1) * chunk, gi * gw:(gi + 1) * gw]

    def section(row0):
        rs = slice(row0, row0 + sec_rows)
        r, k, v, lw = r_ref[0, rs, :], k_ref[0, rs, :], v_ref[0, rs, :], lw_ref[0, rs, :]
        kk, b = kk_ref[0, rs, :], b_ref[0, rs, :]
        cum = _dot_ones_lhs(ltri, lw)
        a_t = -kk * jnp.exp(cum - lw)
        r_t = r * jnp.exp(cum)
        e_neg = jnp.exp(-cum)
        b_t = b * e_neg
        k_t = k * e_neg
        tots = [cum[(ci + 1) * chunk - 1:(ci + 1) * chunk, :] for ci in range(sec_chunks)]
        e_bars = [jnp.exp(tots[ci] - cum[ci * chunk:(ci + 1) * chunk, :]) for ci in range(sec_chunks)]
        b_bar = [win(b, ci, gi) * e_bars[ci][:, gi * gw:(gi + 1) * gw] for ci, gi in chains]
        k_bar = [win(k, ci, gi) * e_bars[ci][:, gi * gw:(gi + 1) * gw] for ci, gi in chains]
        at = [win(a_t, ci, gi) for ci, gi in chains]
        rt = [win(r_t, ci, gi) for ci, gi in chains]
        vv = [win(v, ci, gi) for ci, gi in chains]
        yield

        ar = [jnp.concatenate([a_, r_], axis=0) for a_, r_ in zip(at, rt)]
        grams = [two_rhs(x, win(b_t, ci, gi), win(k_t, ci, gi), _NT) for x, (ci, gi) in zip(ar, chains)]
        gram_b = [x[0] for x in grams]
        gram_k = [x[1] for x in grams]
        yield
        m_ab = [jnp.where(strict, g_[:chunk], 0.0) for g_ in gram_b]
        p_rb = [jnp.where(incl, g_[chunk:], 0.0) for g_ in gram_b]
        m_ak = [jnp.where(strict, g_[:chunk], 0.0) for g_ in gram_k]
        p_rk = [jnp.where(incl, g_[chunk:], 0.0) for g_ in gram_k]
        yield

        ts = [eye + jnp.where(same(1), m, 0.0) for m in m_ab]
        s = 1
        while (1 << s) < chunk:
            level = same(s + 1) & jnp.logical_not(same(s))
            offs = [jnp.where(level, m, 0.0) for m in m_ab]
            mids = [_packed_dot1(t, o, hm) for t, o in zip(ts, offs)]
            yield
            ts = [t + _packed_dot1(md, t, hm) for t, md in zip(ts, mids)]
            yield
            s += 1

        both = [_packed_dot1(jnp.concatenate([m, pk], axis=0), x, hm) for m, pk, x in zip(m_ak, p_rk, vv)]
        mv = [x[:chunk] for x in both]
        pkv = [x[chunk:] for x in both]
        yield
        hat = [two_rhs(t, x, y) for t, x, y in zip(ts, at, mv)]
        a_hat = [x[0] for x in hat]
        u0 = [x[1] for x in hat]
        yield
        pb = [two_rhs(p, a, u) for p, a, u in zip(p_rb, a_hat, u0)]
        r_hat = [x + y[0] for x, y in zip(rt, pb)]
        y0 = [x + y[1] for x, y in zip(pkv, pb)]
        yield
        g_mat = [same_head(_dot(halves_rows(a), halves_masked(bb), _TN))
                 for a, bb in zip(a_hat, b_bar)]
        h_mat = [same_head(_dot(jnp.concatenate([halves_rows(u), halves_rows(x)], axis=0),
                                jnp.concatenate([halves_masked(bb), halves_masked(kb)], axis=0), _TN))
                 for u, x, bb, kb in zip(u0, vv, b_bar, k_bar)]
        yield

        y_rows = []
        for ci in range(sec_chunks):
            y_groups = []
            for gi in range(n_groups):
                j = ci * n_groups + gi
                s0 = state_ref[gi]
                y_groups.append(_packed_dot1(r_hat[j], s0, hm, _NT) + y0[j])
                w_tot = jnp.exp(tots[ci][:, gi * gw:(gi + 1) * gw])
                state_ref[gi] = s0 * w_tot + _packed_dot_wide_lhs(s0, g_mat[j], hm) + h_mat[j]
            y_rows.append(jnp.concatenate(y_groups, axis=1))
            yield
        y = jnp.concatenate(y_rows, axis=0)

        hsum = hsum_ref[...]
        inv_n = 1.0 / n
        mu = _head_sums(y, hsum) * inv_n
        yc = y - mu
        var = _head_sums(yc * yc, hsum) * inv_n
        yn = yc * lax.rsqrt(var + GN_EPS) * gng_ref[...] + gnb_ref[...]
        bonus = _head_sums(r * k * rk_ref[...], hsum)
        o_ref[0, rs, :] = ((yn + bonus * v) * g_ref[0, rs, :]).astype(o_ref.dtype)

    assert RWKV_SECTION_LAG >= sec_chunks
    _run_skewed([section(si * sec_rows) for si in range(RWKV_SECTIONS)], lag=RWKV_SECTION_LAG)


def _rwkv(r, k, v, lw, kk, b, g, rw_rk, gn_g, gn_b, hsum, *, chunk, rows):
    bsz, seq, d = r.shape
    n_groups = d // (RWKV_GROUP_HEADS * HEAD_DIM)
    blk = pl.BlockSpec((1, rows, d), lambda bi, ci: (bi, ci, 0))
    const = lambda bi, ci: (0, 0)
    vec = lambda a: a.reshape(1, d)
    return pl.pallas_call(
        functools.partial(_rwkv_kernel, chunk=chunk),
        grid=(bsz, seq // rows),
        in_specs=[blk] * 7 + [pl.BlockSpec((1, d), const)] * 3 + [pl.BlockSpec(hsum.shape, const)],
        out_specs=blk,
        out_shape=jax.ShapeDtypeStruct((bsz, seq, d), BF16),
        scratch_shapes=[pltpu.VMEM((n_groups, HEAD_DIM, RWKV_GROUP_HEADS * HEAD_DIM), F32)],
        compiler_params=pltpu.CompilerParams(dimension_semantics=("arbitrary", "arbitrary"),
                                             vmem_limit_bytes=VMEM_LIMIT),
        name="rwkv",
    )(r, k, v, lw, kk, b, g, vec(rw_rk), vec(gn_g), vec(gn_b), hsum)


SB_LOG_UNDERFLOW = 105.0


def _sb_kernel(q_ref, k_ref, v_ref, o_ref, acc_ref, run_ref, *, blk):
    qi = pl.program_id(1)
    pairs = k_ref.shape[2] // (2 * HEAD_DIM)
    row = lax.broadcasted_iota(jnp.int32, (blk, blk), 0)
    col = lax.broadcasted_iota(jnp.int32, (blk, blk), 1)
    causal = col < row
    jj = lax.broadcasted_iota(jnp.int32, (blk, 2 * blk), 0)
    ss = lax.broadcasted_iota(jnp.int32, (blk, 2 * blk), 1)
    suffix = jnp.where((jj >= ss) | (ss >= blk), 1.0, 0.0).astype(BF16)
    first_head = lax.broadcasted_iota(jnp.int32, (1, 2 * HEAD_DIM), 1) < HEAD_DIM
    zero = jnp.zeros((), BF16)

    lanes = [slice(p * 2 * HEAD_DIM, (p + 1) * 2 * HEAD_DIM) for p in range(pairs)]
    heads = [(p, hh) for p in range(pairs) for hh in range(2)]

    nh = len(heads)
    pw = 2 * HEAD_DIM

    def tiles(kbs, diagonal, absent=None):
        starts = [pl.multiple_of(kb * blk, blk) for kb in kbs]
        zs = []
        for start in starts:
            for p in range(pairs):
                q2 = jnp.concatenate([q_ref[0, :, (2 * p + hh) * pw:(2 * p + hh + 1) * pw] for hh in range(2)],
                                     axis=0)
                z2 = _dot(q2, k_ref[0, pl.ds(start, blk), lanes[p]], _NT)
                zs += [z2[:blk], z2[blk:]]
        sps = [_softplus(z) for z in zs]
        if diagonal:
            sps = [jnp.where(causal, sp, 0.0) for sp in sps]
        css = []
        for j in range(len(zs) // 2):
            c2 = _dot(jnp.concatenate(sps[2 * j:2 * j + 2], axis=0).astype(BF16), suffix)
            css += [c2[:blk], c2[blk:]]
        atts = [[None] * nh for _ in kbs]
        low = None
        for n_ in range(nh):
            run = None if diagonal else run_ref[n_]
            for t_ in range(len(kbs)):
                z, cs = zs[t_ * nh + n_], css[t_ * nh + n_]
                logw = z - cs[:, :blk]
                if run is None:
                    att = jnp.where(causal, jnp.exp(logw), 0.0)
                    run = cs[:, blk:]
                else:
                    att = jnp.exp(logw - run)
                    run = run + cs[:, blk:]
                if t_ == 0 and absent is not None:
                    run = run + absent
                atts[t_][n_] = att.astype(BF16)
            run_ref[n_] = run
            low = run if low is None else jnp.minimum(low, run)
        done = (jnp.min(low) > SB_LOG_UNDERFLOW).astype(jnp.int32)
        for p in range(pairs):
            vvs = []
            for start in starts:
                vp = v_ref[0, pl.ds(start, blk), lanes[p]]
                vvs += [jnp.where(first_head, vp, zero), jnp.where(first_head, zero, vp)]
            lhs = jnp.concatenate([atts[t_][2 * p + hh] for t_ in range(len(kbs)) for hh in range(2)], axis=1)
            out = _dot(lhs, jnp.concatenate(vvs, axis=0))
            if diagonal:
                acc_ref[:, lanes[p]] = out
            else:
                acc_ref[:, lanes[p]] += out
        return done

    done0 = tiles([qi], True)

    def cond(c):
        i, done = c
        return jnp.logical_and(i <= qi, done == 0)

    def body(c):
        i, _ = c
        second = qi - i - 1
        absent = jnp.where(second < 0, 1e4, 0.0).astype(F32)
        return i + 2, tiles([qi - i, jnp.maximum(second, 0)], False, absent)

    lax.while_loop(cond, body, (jnp.int32(1), done0))
    o_ref[0] = acc_ref[...].astype(o_ref.dtype)


def _sb_attention(q, k, v, *, blk):
    bsz, seq, d = k.shape
    heads = d // HEAD_DIM
    qspec = pl.BlockSpec((1, blk, 2 * d), lambda b, i: (b, i, 0))
    ospec = pl.BlockSpec((1, blk, d), lambda b, i: (b, i, 0))
    kvspec = pl.BlockSpec((1, seq, d), lambda b, i: (b, 0, 0), pipeline_mode=pl.Buffered(1))
    return pl.pallas_call(
        functools.partial(_sb_kernel, blk=blk),
        grid=(bsz, seq // blk),
        in_specs=[qspec, kvspec, kvspec],
        out_specs=ospec,
        out_shape=jax.ShapeDtypeStruct((bsz, seq, d), BF16),
        scratch_shapes=[pltpu.VMEM((blk, d), F32), pltpu.VMEM((heads, blk, blk), F32)],
        compiler_params=pltpu.CompilerParams(dimension_semantics=("arbitrary", "arbitrary"),
                                             vmem_limit_bytes=VMEM_LIMIT),
        name="sbattn",
    )(q, k, v)


def _merge_kernel(x_ref, ada_ref, ya_ref, ob_ref, ga_ref, gb_ref, wa_ref, wb_ref, wo_ref,
                  lng_ref, lnb_ref, o_ref, *, alpha):
    g1 = ada_ref[0, 2:3, :]
    half = x_ref.shape[0] // 2

    def rows_program(rows):
        y_a = _dot(ya_ref[rows, :], wa_ref[...])
        y_b = _dot(ob_ref[rows, :], wb_ref[...])
        yield
        merged = ga_ref[rows, :].astype(F32) * y_a + gb_ref[rows, :].astype(F32) * y_b
        mix = _dot(merged.astype(BF16), wo_ref[...])
        yield
        o_ref[rows, :] = _norm_rows(alpha * x_ref[rows, :] + g1 * mix, LN_EPS) * lng_ref[...] + lnb_ref[...]

    _run_skewed([rows_program(slice(0, half)), rows_program(slice(half, 2 * half))], lag=1)


def _merge(x2, ada3, ya, ob, ga, gb, wa_bf, wb_bf, wo_bf, ln_g, ln_b, *, seq, tm, alpha):
    m, d = x2.shape
    tiles_per_batch = seq // tm
    row = lambda i: (i, 0)
    const = lambda i: (0, 0)
    rows = lambda a: pl.BlockSpec((tm, a.shape[1]), row)
    full = lambda a: pl.BlockSpec(a.shape, const, pipeline_mode=pl.Buffered(1))
    vec = lambda a: a.reshape(1, -1)
    ins = [x2, ada3, ya, ob, ga, gb, wa_bf, wb_bf, wo_bf, vec(ln_g), vec(ln_b)]
    in_specs = [rows(x2), pl.BlockSpec((1,) + ada3.shape[1:], lambda i: (i // tiles_per_batch, 0, 0)),
                rows(ya), rows(ob), rows(ga), rows(gb)] + [full(a) for a in ins[6:]]
    return pl.pallas_call(
        functools.partial(_merge_kernel, alpha=alpha),
        grid=(m // tm,),
        in_specs=in_specs,
        out_specs=pl.BlockSpec((tm, d), row),
        out_shape=jax.ShapeDtypeStruct((m, d), F32),
        compiler_params=pltpu.CompilerParams(dimension_semantics=("arbitrary",),
                                             vmem_limit_bytes=VMEM_LIMIT),
        name="merge",
    )(*ins)


def _ffn_kernel(x_ref, ada_ref, w1_ref, b1_ref, w2_ref, b2_ref, lng_ref, lnb_ref, o_ref, *, alpha, fchunk):
    sh = ada_ref[0, 3:4, :]
    sc = ada_ref[0, 4:5, :]
    g2 = ada_ref[0, 5:6, :]
    d_ff = w1_ref.shape[1]
    n_chunks = d_ff // fchunk
    half = x_ref.shape[0] // 2

    def rows_program(rows):
        x = x_ref[rows, :]
        hb = (_norm_rows(x, ADALN_EPS) * (1.0 + sc) + sh).astype(BF16)
        yield
        ff = None
        for j in range(n_chunks):
            sl = slice(j * fchunk, (j + 1) * fchunk)
            t = jnp.maximum(_dot(hb, w1_ref[:, sl]) + b1_ref[:, sl], 0.0)
            part = _dot((t * t).astype(BF16), w2_ref[sl, :])
            ff = part if ff is None else ff + part
            yield
        ff = ff + b2_ref[...]
        o_ref[rows, :] = _norm_rows(alpha * x + g2 * ff, LN_EPS) * lng_ref[...] + lnb_ref[...]

    _run_skewed([rows_program(slice(0, half)), rows_program(slice(half, 2 * half))], lag=2)


def _ffn(x1, ada3, w1_bf, b1, w2_bf, b2, ln_g, ln_b, *, seq, tm, alpha, fchunk):
    m, d = x1.shape
    tiles_per_batch = seq // tm
    row = lambda i: (i, 0)
    const = lambda i: (0, 0)
    full = lambda a: pl.BlockSpec(a.shape, const, pipeline_mode=pl.Buffered(1))
    vec = lambda a: a.reshape(1, -1)
    ins = [x1, ada3, w1_bf, vec(b1), w2_bf, vec(b2), vec(ln_g), vec(ln_b)]
    in_specs = [pl.BlockSpec((tm, d), row),
                pl.BlockSpec((1,) + ada3.shape[1:], lambda i: (i // tiles_per_batch, 0, 0))]
    in_specs += [full(a) for a in ins[2:]]
    return pl.pallas_call(
        functools.partial(_ffn_kernel, alpha=alpha, fchunk=fchunk),
        grid=(m // tm,),
        in_specs=in_specs,
        out_specs=pl.BlockSpec((tm, d), row),
        out_shape=jax.ShapeDtypeStruct((m, d), F32),
        compiler_params=pltpu.CompilerParams(dimension_semantics=("arbitrary",),
                                             vmem_limit_bytes=VMEM_LIMIT),
        name="ffn",
    )(*ins)


def _branches(proj, rw_rk, rw_gn_g, rw_gn_b, hsum, bsz, seq):
    r, k, v, lw, kk, b, g, q, ks, vs = proj
    seq3 = lambda a: a.reshape(bsz, seq, -1)
    ya = _rwkv(seq3(r), seq3(k), seq3(v), seq3(lw), seq3(kk), seq3(b), seq3(g),
               rw_rk, rw_gn_g, rw_gn_b, hsum, chunk=min(RWKV_CHUNK, seq),
               rows=min(RWKV_CHUNKS_PER_STEP * RWKV_CHUNK, seq))
    o = _sb_attention(seq3(q), seq3(ks), seq3(vs), blk=min(SB_BLOCK, seq))
    return ya.reshape(bsz * seq, -1), o.reshape(bsz * seq, -1)


def _layer(x, c, w_ada, b_ada, w_in, mu_shift, rw_w0, rw_w2, rw_a0, rw_a2, rw_g2, rw_kk, rw_ka,
           rw_rk, rw_gn_g, rw_gn_b, w_branch_a, w_branch_b, w_out, ln1_g, ln1_b,
           w_ff1, b_ff1, w_ff2, b_ff2, ln2_g, ln2_b, *, alpha):
    bsz, seq, d = x.shape
    d_rwkv = rw_w0.shape[-1]
    tm = min(INPROJ_ROWS, seq)
    tm2 = min(MLP_ROWS, seq)

    ada3 = _ada(c, w_ada, b_ada).reshape(bsz, 6, d)
    lane = jnp.arange(RWKV_GROUP_HEADS * HEAD_DIM) // HEAD_DIM
    hsum = (lane[:, None] == lane[None, :]).astype(BF16)

    x2 = x.reshape(bsz * seq, d)
    (r, k, v, lw, kk, b, g, q, ks, vs, ga, gb) = _inproj(
        x2, ada3, w_in.astype(BF16), mu_shift, rw_w0, rw_w2, rw_a0, rw_a2, rw_g2, rw_kk, rw_ka, hsum,
        seq=seq, tm=tm)

    ya, ob = _branches((r, k, v, lw, kk, b, g, q, ks, vs), rw_rk, rw_gn_g, rw_gn_b, hsum, bsz, seq)

    x1 = _merge(x2, ada3, ya, ob, ga, gb,
                w_branch_a.astype(BF16), w_branch_b.astype(BF16), w_out.astype(BF16),
                ln1_g, ln1_b, seq=seq, tm=tm2, alpha=alpha)
    out = _ffn(x1, ada3, w_ff1.astype(BF16), b_ff1, w_ff2.astype(BF16), b_ff2, ln2_g, ln2_b,
               seq=seq, tm=tm2, alpha=alpha, fchunk=min(1024, w_ff1.shape[-1]))
    return out.reshape(bsz, seq, d)


def kernel(x, c, w_ada, b_ada, w_in, mu_shift, rw_w0, rw_w2, rw_a0, rw_a2, rw_g2, rw_kk, rw_ka, rw_rk,
           rw_gn_g, rw_gn_b, w_branch_a, w_branch_b, w_out, ln1_g, ln1_b, w_ff1, b_ff1, w_ff2, b_ff2,
           ln2_g, ln2_b):
    in_dtype = x.dtype
    layer_params = (w_ada, b_ada, w_in, mu_shift, rw_w0, rw_w2, rw_a0, rw_a2, rw_g2, rw_kk, rw_ka,
                    rw_rk, rw_gn_g, rw_gn_b, w_branch_a, w_branch_b, w_out, ln1_g, ln1_b,
                    w_ff1, b_ff1, w_ff2, b_ff2, ln2_g, ln2_b)
    depth = w_ada.shape[0]
    alpha = (2.0 * depth) ** 0.25
    for l in range(depth):
        x = _layer(x, c, *[p[l] for p in layer_params], alpha=alpha)
    return x.astype(in_dtype)
```

```python
import functools

import jax
import jax.numpy as jnp
from jax import lax
from jax.experimental import pallas as pl
from jax.experimental.pallas import tpu as pltpu

F32 = jnp.float32
BF16 = jnp.bfloat16

HEAD_DIM = 64
LN_EPS = 1e-5
ADALN_EPS = 1e-6
GN_EPS = 64e-5
RWKV_CHUNK = 64
RWKV_SECTIONS = 2
RWKV_SECTION_CHUNKS = 4
RWKV_SECTION_LAG = 12
RWKV_CHUNKS_PER_STEP = RWKV_SECTIONS * RWKV_SECTION_CHUNKS
SB_BLOCK = 128
INPROJ_ROWS = 512
MLP_ROWS = 1024
VMEM_LIMIT = 56 * 1024 * 1024


def _split2(a):
    hi = a.astype(BF16)
    lo = (a - hi.astype(F32)).astype(BF16)
    return hi, lo


_NN = (((1,), (0,)), ((), ()))
_NT = (((1,), (1,)), ((), ()))
_TN = (((0,), (0,)), ((), ()))


def _dot(a, b, dims=_NN):
    return lax.dot_general(a, b, dims, preferred_element_type=F32)


def _dot3(a, b, dims=_NN):
    ah, al = _split2(a)
    bh, bl = _split2(b)
    return _dot(ah, bh, dims) + (_dot(al, bh, dims) + _dot(ah, bl, dims))


def _head_sums(x, ones_blk):
    m, w = x.shape[0], ones_blk.shape[0]
    hi, lo = _split2(x)
    outs = []
    for g in range(x.shape[1] // w):
        cols = slice(g * w, (g + 1) * w)
        both = _dot(jnp.concatenate([hi[:, cols], lo[:, cols]], axis=0), ones_blk)
        outs.append(both[:m] + both[m:])
    return jnp.concatenate(outs, axis=1)


def _dot_ones_lhs(a_bf, b):
    hi, lo = _split2(b)
    return _dot(a_bf, hi) + _dot(a_bf, lo)


_DONE = object()


def _run_skewed(programs, lag):
    live = [True] * len(programs)
    tick = 0
    while any(live):
        for i, prog in enumerate(programs):
            if live[i] and tick >= i * lag:
                live[i] = next(prog, _DONE) is not _DONE
        tick += 1


_NEG_LOG2_E = -1.4426950408889634


def _softplus(y):
    return jnp.maximum(y, 0.0) + jnp.log(1.0 + jnp.exp2(jnp.abs(y) * _NEG_LOG2_E))


def _sigmoid(y):
    return 1.0 / (1.0 + jnp.exp(-y))


def _norm_rows(x, eps):
    mu = jnp.mean(x, axis=-1, keepdims=True)
    xc = x - mu
    var = jnp.mean(xc * xc, axis=-1, keepdims=True)
    return xc * lax.rsqrt(var + eps)


def _ada_kernel(c_ref, w_ref, b_ref, o_ref):
    c = c_ref[...]
    s = c * _sigmoid(c)
    o_ref[...] = jnp.dot(s, w_ref[...], preferred_element_type=F32,
                         precision=lax.Precision.HIGHEST) + b_ref[...]


def _ada(c, w_ada, b_ada):
    bsz, d = c.shape
    n = w_ada.shape[1]
    return pl.pallas_call(
        _ada_kernel,
        grid=(n // d,),
        in_specs=[pl.BlockSpec((bsz, d), lambda j: (0, 0)),
                  pl.BlockSpec((d, d), lambda j: (0, j)),
                  pl.BlockSpec((1, d), lambda j: (0, j))],
        out_specs=pl.BlockSpec((bsz, d), lambda j: (0, j)),
        out_shape=jax.ShapeDtypeStruct((bsz, n), F32),
        name="ada",
    )(c, w_ada, b_ada.reshape(1, n))


def _inproj_kernel(x_ref, ada_ref, win_ref, mu_ref, w0_ref, w2_ref, a0_ref, a2_ref, g2_ref,
                   kkw_ref, kaw_ref, hsum_ref,
                   r_ref, k_ref, v_ref, lw_ref, kk_ref, b_ref, g_ref,
                   q_ref, ks_ref, vs_ref, ga_ref, gb_ref,
                   carry_ref, *, tiles_per_batch, d_rwkv, d_decay, d_aaa, d_gate, d_sb, d_model):
    i = pl.program_id(0)
    tm = x_ref.shape[0]
    d_shift = 3 * d_rwkv + d_decay + d_aaa + d_gate

    sh = ada_ref[0, 0:1, :]
    sc = ada_ref[0, 1:2, :]
    h = _norm_rows(x_ref[...], ADALN_EPS) * (1.0 + sc) + sh
    hb = h.astype(BF16)

    @pl.when(i % tiles_per_batch == 0)
    def _():
        carry_ref[...] = jnp.zeros_like(carry_ref)

    row0 = lax.broadcasted_iota(jnp.int32, (tm, 1), 0) == 0

    def shifted(lo, width):
        cols = slice(lo, lo + width)
        z = _dot(hb, win_ref[:, cols])
        prev = jnp.where(row0, carry_ref[:, cols], pltpu.roll(z, 1, 0))
        carry_ref[:, cols] = z[tm - 1:tm, :]
        return z + mu_ref[:, cols] * (prev - z)

    half = d_model // 2
    pair_w = 2 * HEAD_DIM
    first_head = lax.broadcasted_iota(jnp.int32, (1, pair_w), 1) < HEAD_DIM

    def store_queries(z):
        zq = (z * (HEAD_DIM ** -0.5)).astype(BF16)
        zero = jnp.zeros((), BF16)
        for p in range(d_sb // pair_w):
            qp = zq[:, p * pair_w:(p + 1) * pair_w]
            q_ref[:, (2 * p) * pair_w:(2 * p + 1) * pair_w] = jnp.where(first_head, qp, zero)
            q_ref[:, (2 * p + 1) * pair_w:(2 * p + 2) * pair_w] = jnp.where(first_head, zero, qp)

    def plain(ref, cols, fn=lambda z: z):
        def store(z):
            ref[:, cols] = fn(z).astype(BF16)
        return store

    rest = [(store_queries, d_sb), (plain(ks_ref, slice(None)), d_sb), (plain(vs_ref, slice(None)), d_sb),
            (plain(ga_ref, slice(0, half), _sigmoid), half), (plain(ga_ref, slice(half, d_model), _sigmoid), half),
            (plain(gb_ref, slice(0, half), _sigmoid), half), (plain(gb_ref, slice(half, d_model), _sigmoid), half)]
    rest_col = [d_shift]

    def project_next():
        store, width = rest.pop(0)
        store(_dot(hb, win_ref[:, rest_col[0]:rest_col[0] + width]))
        rest_col[0] += width

    zl = shifted(3 * d_rwkv, d_decay + d_aaa + d_gate)
    zw = zl[:, :d_decay]
    za = zl[:, d_decay:d_decay + d_aaa]
    zg = zl[:, d_decay + d_aaa:]
    k = shifted(d_rwkv, d_rwkv)
    ww = w0_ref[...] + _dot(jnp.tanh(zw).astype(BF16), w2_ref[...])
    w_log = -_softplus(-ww) - 0.5
    lw_ref[...] = -jnp.exp(w_log)
    project_next()
    arate = _sigmoid(a0_ref[...] + _dot(za.astype(BF16), a2_ref[...]))
    g_ref[...] = _dot(_sigmoid(zg).astype(BF16), g2_ref[...])
    r_ref[...] = shifted(0, d_rwkv)
    kkraw = k * kkw_ref[...]
    ssq = _head_sums(kkraw * kkraw, hsum_ref[...])
    kk = kkraw / jnp.maximum(jnp.sqrt(ssq), 1e-12)
    v_ref[...] = shifted(2 * d_rwkv, d_rwkv)
    k_ref[...] = k * (1.0 + (arate - 1.0) * kaw_ref[...])
    project_next()
    kk_ref[...] = kk
    b_ref[...] = kk * arate
    while rest:
        project_next()


def _inproj(x2, ada3, win_bf, mu_shift, rw_w0, rw_w2, rw_a0, rw_a2, rw_g2, rw_kk, rw_ka, hsum,
            *, seq, tm):
    m, d_model = x2.shape
    d_rwkv = rw_w0.shape[-1]
    d_decay, d_aaa, d_gate = rw_w2.shape[0], rw_a2.shape[0], rw_g2.shape[0]
    d_shift = 3 * d_rwkv + d_decay + d_aaa + d_gate
    d_sb = (win_bf.shape[1] - d_shift - 2 * d_model) // 3
    tiles_per_batch = seq // tm
    row = lambda i: (i, 0)
    const = lambda i: (0, 0)
    full = lambda a: pl.BlockSpec(a.shape, const, pipeline_mode=pl.Buffered(1))
    vec = lambda a: a.reshape(1, -1)
    ins = [x2, ada3, win_bf, vec(mu_shift), vec(rw_w0), rw_w2.astype(BF16), vec(rw_a0),
           rw_a2.astype(BF16), rw_g2.astype(BF16), vec(rw_kk), vec(rw_ka), hsum]
    in_specs = [pl.BlockSpec((tm, d_model), row),
                pl.BlockSpec((1,) + ada3.shape[1:], lambda i: (i // tiles_per_batch, 0, 0))]
    in_specs += [full(a) for a in ins[2:]]
    widths = [d_rwkv] * 7 + [2 * d_sb, d_sb, d_sb] + [d_model] * 2
    dtypes = [F32] * 7 + [BF16] * 5
    out_shape = [jax.ShapeDtypeStruct((m, w), dt) for w, dt in zip(widths, dtypes)]
    out_specs = [pl.BlockSpec((tm, w), row) for w in widths]
    kern = functools.partial(_inproj_kernel, tiles_per_batch=tiles_per_batch, d_rwkv=d_rwkv,
                             d_decay=d_decay, d_aaa=d_aaa, d_gate=d_gate, d_sb=d_sb, d_model=d_model)
    return pl.pallas_call(
        kern,
        grid=(m // tm,),
        in_specs=in_specs,
        out_specs=out_specs,
        out_shape=out_shape,
        scratch_shapes=[pltpu.VMEM((1, d_shift), F32)],
        compiler_params=pltpu.CompilerParams(dimension_semantics=("arbitrary",),
                                             vmem_limit_bytes=VMEM_LIMIT),
        name="inproj",
    )(*ins)


RWKV_GROUP_HEADS = 4


def _block_diag(q, head_masks):
    zero = jnp.zeros((), q.dtype)
    return jnp.concatenate([jnp.where(m, q, zero) for m in head_masks], axis=0)


def _packed_dot1(lhs, q, head_masks, dims=_NN):
    return _dot(lhs.astype(BF16), _block_diag(q.astype(BF16), head_masks), dims)


def _packed_dot_wide_lhs(lhs, q, head_masks, dims=_NN):
    m = lhs.shape[0]
    lh, ll = _split2(lhs)
    both = _dot(jnp.concatenate([lh, ll], axis=0), _block_diag(q.astype(BF16), head_masks), dims)
    return both[:m] + both[m:]


def _rwkv_kernel(r_ref, k_ref, v_ref, lw_ref, kk_ref, b_ref, g_ref, rk_ref, gng_ref, gnb_ref, hsum_ref,
                 o_ref, state_ref, *, chunk):
    c = pl.program_id(1)
    n = HEAD_DIM
    rows, d = r_ref.shape[1], r_ref.shape[2]
    gw = RWKV_GROUP_HEADS * n
    n_groups = d // gw

    @pl.when(c == 0)
    def _():
        state_ref[...] = jnp.zeros_like(state_ref)

    t_idx = lax.broadcasted_iota(jnp.int32, (chunk, gw), 0)
    lane = lax.broadcasted_iota(jnp.int32, (chunk, gw), 1)
    i_idx = lane & (n - 1)
    strict = t_idx > i_idx
    incl = t_idx >= i_idx
    lane1 = lax.broadcasted_iota(jnp.int32, (1, gw), 1)
    hm = [(lane1 >> (n.bit_length() - 1)) == h for h in range(RWKV_GROUP_HEADS)]
    same = lambda s: (t_idx >> s) == (i_idx >> s)

    half_w = gw // 2
    lane_half = [(lane1 >> (half_w.bit_length() - 1)) == j for j in range(2)]
    second_in_half = ((lane >> (n.bit_length() - 1)) & 1) == 1

    def halves_rows(x):
        return jnp.concatenate([x[:, :half_w], x[:, half_w:]], axis=0).astype(BF16)

    def halves_masked(w):
        return _block_diag(w.astype(BF16), lane_half)

    def same_head(out):
        return jnp.where(second_in_half, out[n:], out[:n])

    lane_h = lax.broadcasted_iota(jnp.int32, (1, half_w), 1)
    in_half = [(lane_h >> (n.bit_length() - 1)) == j for j in range(half_w // n)]

    def two_rhs(lhs, x, y, dims=_NN):
        lb, xb, yb = lhs.astype(BF16), x.astype(BF16), y.astype(BF16)
        out_x, out_y = [], []
        for j in range(gw // half_w):
            cols = slice(j * half_w, (j + 1) * half_w)
            w = jnp.concatenate([_block_diag(xb[:, cols], in_half), _block_diag(yb[:, cols], in_half)],
                                axis=1 if dims == _NN else 0)
            o = _dot(lb[:, cols], w, dims)
            out_x.append(o[:, :half_w])
            out_y.append(o[:, half_w:])
        return jnp.concatenate(out_x, axis=1), jnp.concatenate(out_y, axis=1)

    sec_rows = rows // RWKV_SECTIONS
    sec_chunks = sec_rows // chunk
    rr = lax.broadcasted_iota(jnp.int32, (sec_rows, sec_rows), 0)
    cc = lax.broadcasted_iota(jnp.int32, (sec_rows, sec_rows), 1)
    cs = chunk.bit_length() - 1
    ltri = jnp.where((rr >= cc) & ((rr >> cs) == (cc >> cs)), 1.0, 0.0).astype(BF16)
    eye = jnp.where(t_idx == i_idx, 1.0, 0.0)
    chains = [(ci, gi) for ci in range(sec_chunks) for gi in range(n_groups)]
    win = lambda x, ci, gi: x[ci * chunk:(ci + 1) * chunk, gi * gw:(gi + 1) * gw]

    def section(row0):
        rs = slice(row0, row0 + sec_rows)
        r, k, v, lw = r_ref[0, rs, :], k_ref[0, rs, :], v_ref[0, rs, :], lw_ref[0, rs, :]
        kk, b = kk_ref[0, rs, :], b_ref[0, rs, :]
        cum = _dot_ones_lhs(ltri, lw)
        a_t = -kk * jnp.exp(cum - lw)
        r_t = r * jnp.exp(cum)
        e_neg = jnp.exp(-cum)
        b_t = b * e_neg
        k_t = k * e_neg
        tots = [cum[(ci + 1) * chunk - 1:(ci + 1) * chunk, :] for ci in range(sec_chunks)]
        e_bars = [jnp.exp(tots[ci] - cum[ci * chunk:(ci + 1) * chunk, :]) for ci in range(sec_chunks)]
        b_bar = [win(b, ci, gi) * e_bars[ci][:, gi * gw:(gi + 1) * gw] for ci, gi in chains]
        k_bar = [win(k, ci, gi) * e_bars[ci][:, gi * gw:(gi + 1) * gw] for ci, gi in chains]
        at = [win(a_t, ci, gi) for ci, gi in chains]
        rt = [win(r_t, ci, gi) for ci, gi in chains]
        vv = [win(v, ci, gi) for ci, gi in chains]
        yield

        ar = [jnp.concatenate([a_, r_], axis=0) for a_, r_ in zip(at, rt)]
        grams = [two_rhs(x, win(b_t, ci, gi), win(k_t, ci, gi), _NT) for x, (ci, gi) in zip(ar, chains)]
        gram_b = [x[0] for x in grams]
        gram_k = [x[1] for x in grams]
        yield
        m_ab = [jnp.where(strict, g_[:chunk], 0.0) for g_ in gram_b]
        p_rb = [jnp.where(incl, g_[chunk:], 0.0) for g_ in gram_b]
        m_ak = [jnp.where(strict, g_[:chunk], 0.0) for g_ in gram_k]
        p_rk = [jnp.where(incl, g_[chunk:], 0.0) for g_ in gram_k]
        yield

        ts = [eye + jnp.where(same(1), m, 0.0) for m in m_ab]
        s = 1
        while (1 << s) < chunk:
            level = same(s + 1) & jnp.logical_not(same(s))
            offs = [jnp.where(level, m, 0.0) for m in m_ab]
            mids = [_packed_dot1(t, o, hm) for t, o in zip(ts, offs)]
            yield
            ts = [t + _packed_dot1(md, t, hm) for t, md in zip(ts, mids)]
            yield
            s += 1

        both = [_packed_dot1(jnp.concatenate([m, pk], axis=0), x, hm) for m, pk, x in zip(m_ak, p_rk, vv)]
        mv = [x[:chunk] for x in both]
        pkv = [x[chunk:] for x in both]
        yield
        hat = [two_rhs(t, x, y) for t, x, y in zip(ts, at, mv)]
        a_hat = [x[0] for x in hat]
        u0 = [x[1] for x in hat]
        yield
        pb = [two_rhs(p, a, u) for p, a, u in zip(p_rb, a_hat, u0)]
        r_hat = [x + y[0] for x, y in zip(rt, pb)]
        y0 = [x + y[1] for x, y in zip(pkv, pb)]
        yield
        g_mat = [same_head(_dot(halves_rows(a), halves_masked(bb), _TN))
                 for a, bb in zip(a_hat, b_bar)]
        h_mat = [same_head(_dot(jnp.concatenate([halves_rows(u), halves_rows(x)], axis=0),
                                jnp.concatenate([halves_masked(bb), halves_masked(kb)], axis=0), _TN))
                 for u, x, bb, kb in zip(u0, vv, b_bar, k_bar)]
        yield

        y_rows = []
        for ci in range(sec_chunks):
            y_groups = []
            for gi in range(n_groups):
                j = ci * n_groups + gi
                s0 = state_ref[gi]
                y_groups.append(_packed_dot1(r_hat[j], s0, hm, _NT) + y0[j])
                w_tot = jnp.exp(tots[ci][:, gi * gw:(gi + 1) * gw])
                state_ref[gi] = s0 * w_tot + _packed_dot_wide_lhs(s0, g_mat[j], hm) + h_mat[j]
            y_rows.append(jnp.concatenate(y_groups, axis=1))
            yield
        y = jnp.concatenate(y_rows, axis=0)

        hsum = hsum_ref[...]
        inv_n = 1.0 / n
        mu = _head_sums(y, hsum) * inv_n
        yc = y - mu
        var = _head_sums(yc * yc, hsum) * inv_n
        yn = yc * lax.rsqrt(var + GN_EPS) * gng_ref[...] + gnb_ref[...]
        bonus = _head_sums(r * k * rk_ref[...], hsum)
        o_ref[0, rs, :] = ((yn + bonus * v) * g_ref[0, rs, :]).astype(o_ref.dtype)

    assert RWKV_SECTION_LAG >= sec_chunks
    _run_skewed([section(si * sec_rows) for si in range(RWKV_SECTIONS)], lag=RWKV_SECTION_LAG)


def _rwkv(r, k, v, lw, kk, b, g, rw_rk, gn_g, gn_b, hsum, *, chunk, rows):
    bsz, seq, d = r.shape
    n_groups = d // (RWKV_GROUP_HEADS * HEAD_DIM)
    blk = pl.BlockSpec((1, rows, d), lambda bi, ci: (bi, ci, 0))
    const = lambda bi, ci: (0, 0)
    vec = lambda a: a.reshape(1, d)
    return pl.pallas_call(
        functools.partial(_rwkv_kernel, chunk=chunk),
        grid=(bsz, seq // rows),
        in_specs=[blk] * 7 + [pl.BlockSpec((1, d), const)] * 3 + [pl.BlockSpec(hsum.shape, const)],
        out_specs=blk,
        out_shape=jax.ShapeDtypeStruct((bsz, seq, d), BF16),
        scratch_shapes=[pltpu.VMEM((n_groups, HEAD_DIM, RWKV_GROUP_HEADS * HEAD_DIM), F32)],
        compiler_params=pltpu.CompilerParams(dimension_semantics=("arbitrary", "arbitrary"),
                                             vmem_limit_bytes=VMEM_LIMIT),
        name="rwkv",
    )(r, k, v, lw, kk, b, g, vec(rw_rk), vec(gn_g), vec(gn_b), hsum)


SB_LOG_UNDERFLOW = 105.0


def _sb_kernel(q_ref, k_ref, v_ref, o_ref, acc_ref, run_ref, *, blk):
    qi = pl.program_id(1)
    pairs = k_ref.shape[2] // (2 * HEAD_DIM)
    row = lax.broadcasted_iota(jnp.int32, (blk, blk), 0)
    col = lax.broadcasted_iota(jnp.int32, (blk, blk), 1)
    causal = col < row
    jj = lax.broadcasted_iota(jnp.int32, (blk, 2 * blk), 0)
    ss = lax.broadcasted_iota(jnp.int32, (blk, 2 * blk), 1)
    suffix = jnp.where((jj >= ss) | (ss >= blk), 1.0, 0.0).astype(BF16)
    first_head = lax.broadcasted_iota(jnp.int32, (1, 2 * HEAD_DIM), 1) < HEAD_DIM
    zero = jnp.zeros((), BF16)

    lanes = [slice(p * 2 * HEAD_DIM, (p + 1) * 2 * HEAD_DIM) for p in range(pairs)]
    heads = [(p, hh) for p in range(pairs) for hh in range(2)]

    nh = len(heads)
    pw = 2 * HEAD_DIM

    def tiles(kbs, diagonal, absent=None):
        starts = [pl.multiple_of(kb * blk, blk) for kb in kbs]
        zs = []
        for start in starts:
            for p in range(pairs):
                q2 = jnp.concatenate([q_ref[0, :, (2 * p + hh) * pw:(2 * p + hh + 1) * pw] for hh in range(2)],
                                     axis=0)
                z2 = _dot(q2, k_ref[0, pl.ds(start, blk), lanes[p]], _NT)
                zs += [z2[:blk], z2[blk:]]
        sps = [_softplus(z) for z in zs]
        if diagonal:
            sps = [jnp.where(causal, sp, 0.0) for sp in sps[:nh]] + sps[nh:]
        css = []
        for j in range(len(zs) // 2):
            c2 = _dot(jnp.concatenate(sps[2 * j:2 * j + 2], axis=0).astype(BF16), suffix)
            css += [c2[:blk], c2[blk:]]
        atts = [[None] * nh for _ in kbs]
        low = None
        for n_ in range(nh):
            run = None if diagonal else run_ref[n_]
            for t_ in range(len(kbs)):
                z, cs = zs[t_ * nh + n_], css[t_ * nh + n_]
                logw = z - cs[:, :blk]
                if run is None:
                    att = jnp.where(causal, jnp.exp(logw), 0.0)
                    run = cs[:, blk:]
                else:
                    att = jnp.exp(logw - run)
                    run = run + cs[:, blk:]
                if t_ == 0 and absent is not None:
                    run = run + absent
                atts[t_][n_] = att.astype(BF16)
            run_ref[n_] = run
            low = run if low is None else jnp.minimum(low, run)
        done = (jnp.min(low) > SB_LOG_UNDERFLOW).astype(jnp.int32)
        for p in range(pairs):
            vvs = []
            for start in starts:
                vp = v_ref[0, pl.ds(start, blk), lanes[p]]
                vvs += [jnp.where(first_head, vp, zero), jnp.where(first_head, zero, vp)]
            lhs = jnp.concatenate([atts[t_][2 * p + hh] for t_ in range(len(kbs)) for hh in range(2)], axis=1)
            out = _dot(lhs, jnp.concatenate(vvs, axis=0))
            if diagonal:
                acc_ref[:, lanes[p]] = out
            else:
                acc_ref[:, lanes[p]] += out
        return done

    def pair_of_tiles(i, diagonal):
        second = qi - i - 1
        absent = jnp.where(second < 0, 1e4, 0.0).astype(F32)
        return tiles([qi - i, jnp.maximum(second, 0)], diagonal, absent)

    done0 = pair_of_tiles(0, True)

    def cond(c):
        i, done = c
        return jnp.logical_and(i <= qi, done == 0)

    def body(c):
        i, _ = c
        return i + 2, pair_of_tiles(i, False)

    lax.while_loop(cond, body, (jnp.int32(2), done0))
    o_ref[0] = acc_ref[...].astype(o_ref.dtype)


def _sb_attention(q, k, v, *, blk):
    bsz, seq, d = k.shape
    heads = d // HEAD_DIM
    qspec = pl.BlockSpec((1, blk, 2 * d), lambda b, i: (b, i, 0))
    ospec = pl.BlockSpec((1, blk, d), lambda b, i: (b, i, 0))
    kvspec = pl.BlockSpec((1, seq, d), lambda b, i: (b, 0, 0), pipeline_mode=pl.Buffered(1))
    return pl.pallas_call(
        functools.partial(_sb_kernel, blk=blk),
        grid=(bsz, seq // blk),
        in_specs=[qspec, kvspec, kvspec],
        out_specs=ospec,
        out_shape=jax.ShapeDtypeStruct((bsz, seq, d), BF16),
        scratch_shapes=[pltpu.VMEM((blk, d), F32), pltpu.VMEM((heads, blk, blk), F32)],
        compiler_params=pltpu.CompilerParams(dimension_semantics=("arbitrary", "arbitrary"),
                                             vmem_limit_bytes=VMEM_LIMIT),
        name="sbattn",
    )(q, k, v)


def _merge_kernel(x_ref, ada_ref, ya_ref, ob_ref, ga_ref, gb_ref, wa_ref, wb_ref, wo_ref,
                  lng_ref, lnb_ref, o_ref, *, alpha):
    g1 = ada_ref[0, 2:3, :]
    half = x_ref.shape[0] // 2

    def rows_program(rows):
        y_a = _dot(ya_ref[rows, :], wa_ref[...])
        y_b = _dot(ob_ref[rows, :], wb_ref[...])
        yield
        merged = ga_ref[rows, :].astype(F32) * y_a + gb_ref[rows, :].astype(F32) * y_b
        mix = _dot(merged.astype(BF16), wo_ref[...])
        yield
        o_ref[rows, :] = _norm_rows(alpha * x_ref[rows, :] + g1 * mix, LN_EPS) * lng_ref[...] + lnb_ref[...]

    _run_skewed([rows_program(slice(0, half)), rows_program(slice(half, 2 * half))], lag=1)


def _merge(x2, ada3, ya, ob, ga, gb, wa_bf, wb_bf, wo_bf, ln_g, ln_b, *, seq, tm, alpha):
    m, d = x2.shape
    tiles_per_batch = seq // tm
    row = lambda i: (i, 0)
    const = lambda i: (0, 0)
    rows = lambda a: pl.BlockSpec((tm, a.shape[1]), row)
    full = lambda a: pl.BlockSpec(a.shape, const, pipeline_mode=pl.Buffered(1))
    vec = lambda a: a.reshape(1, -1)
    ins = [x2, ada3, ya, ob, ga, gb, wa_bf, wb_bf, wo_bf, vec(ln_g), vec(ln_b)]
    in_specs = [rows(x2), pl.BlockSpec((1,) + ada3.shape[1:], lambda i: (i // tiles_per_batch, 0, 0)),
                rows(ya), rows(ob), rows(ga), rows(gb)] + [full(a) for a in ins[6:]]
    return pl.pallas_call(
        functools.partial(_merge_kernel, alpha=alpha),
        grid=(m // tm,),
        in_specs=in_specs,
        out_specs=pl.BlockSpec((tm, d), row),
        out_shape=jax.ShapeDtypeStruct((m, d), F32),
        compiler_params=pltpu.CompilerParams(dimension_semantics=("arbitrary",),
                                             vmem_limit_bytes=VMEM_LIMIT),
        name="merge",
    )(*ins)


def _ffn_kernel(x_ref, ada_ref, w1_ref, b1_ref, w2_ref, b2_ref, lng_ref, lnb_ref, o_ref, *, alpha, fchunk):
    sh = ada_ref[0, 3:4, :]
    sc = ada_ref[0, 4:5, :]
    g2 = ada_ref[0, 5:6, :]
    d_ff = w1_ref.shape[1]
    n_chunks = d_ff // fchunk
    half = x_ref.shape[0] // 2

    def rows_program(rows):
        x = x_ref[rows, :]
        hb = (_norm_rows(x, ADALN_EPS) * (1.0 + sc) + sh).astype(BF16)
        yield
        ff = None
        for j in range(n_chunks):
            sl = slice(j * fchunk, (j + 1) * fchunk)
            t = jnp.maximum(_dot(hb, w1_ref[:, sl]) + b1_ref[:, sl], 0.0)
            part = _dot((t * t).astype(BF16), w2_ref[sl, :])
            ff = part if ff is None else ff + part
            yield
        ff = ff + b2_ref[...]
        o_ref[rows, :] = _norm_rows(alpha * x + g2 * ff, LN_EPS) * lng_ref[...] + lnb_ref[...]

    _run_skewed([rows_program(slice(0, half)), rows_program(slice(half, 2 * half))], lag=2)


def _ffn(x1, ada3, w1_bf, b1, w2_bf, b2, ln_g, ln_b, *, seq, tm, alpha, fchunk):
    m, d = x1.shape
    tiles_per_batch = seq // tm
    row = lambda i: (i, 0)
    const = lambda i: (0, 0)
    full = lambda a: pl.BlockSpec(a.shape, const, pipeline_mode=pl.Buffered(1))
    vec = lambda a: a.reshape(1, -1)
    ins = [x1, ada3, w1_bf, vec(b1), w2_bf, vec(b2), vec(ln_g), vec(ln_b)]
    in_specs = [pl.BlockSpec((tm, d), row),
                pl.BlockSpec((1,) + ada3.shape[1:], lambda i: (i // tiles_per_batch, 0, 0))]
    in_specs += [full(a) for a in ins[2:]]
    return pl.pallas_call(
        functools.partial(_ffn_kernel, alpha=alpha, fchunk=fchunk),
        grid=(m // tm,),
        in_specs=in_specs,
        out_specs=pl.BlockSpec((tm, d), row),
        out_shape=jax.ShapeDtypeStruct((m, d), F32),
        compiler_params=pltpu.CompilerParams(dimension_semantics=("arbitrary",),
                                             vmem_limit_bytes=VMEM_LIMIT),
        name="ffn",
    )(*ins)


def _branches(proj, rw_rk, rw_gn_g, rw_gn_b, hsum, bsz, seq):
    r, k, v, lw, kk, b, g, q, ks, vs = proj
    seq3 = lambda a: a.reshape(bsz, seq, -1)
    ya = _rwkv(seq3(r), seq3(k), seq3(v), seq3(lw), seq3(kk), seq3(b), seq3(g),
               rw_rk, rw_gn_g, rw_gn_b, hsum, chunk=min(RWKV_CHUNK, seq),
               rows=min(RWKV_CHUNKS_PER_STEP * RWKV_CHUNK, seq))
    o = _sb_attention(seq3(q), seq3(ks), seq3(vs), blk=min(SB_BLOCK, seq))
    return ya.reshape(bsz * seq, -1), o.reshape(bsz * seq, -1)


def _layer(x, c, w_ada, b_ada, w_in, mu_shift, rw_w0, rw_w2, rw_a0, rw_a2, rw_g2, rw_kk, rw_ka,
           rw_rk, rw_gn_g, rw_gn_b, w_branch_a, w_branch_b, w_out, ln1_g, ln1_b,
           w_ff1, b_ff1, w_ff2, b_ff2, ln2_g, ln2_b, *, alpha):
    bsz, seq, d = x.shape
    d_rwkv = rw_w0.shape[-1]
    tm = min(INPROJ_ROWS, seq)
    tm2 = min(MLP_ROWS, seq)

    ada3 = _ada(c, w_ada, b_ada).reshape(bsz, 6, d)
    lane = jnp.arange(RWKV_GROUP_HEADS * HEAD_DIM) // HEAD_DIM
    hsum = (lane[:, None] == lane[None, :]).astype(BF16)

    x2 = x.reshape(bsz * seq, d)
    (r, k, v, lw, kk, b, g, q, ks, vs, ga, gb) = _inproj(
        x2, ada3, w_in.astype(BF16), mu_shift, rw_w0, rw_w2, rw_a0, rw_a2, rw_g2, rw_kk, rw_ka, hsum,
        seq=seq, tm=tm)

    ya, ob = _branches((r, k, v, lw, kk, b, g, q, ks, vs), rw_rk, rw_gn_g, rw_gn_b, hsum, bsz, seq)

    x1 = _merge(x2, ada3, ya, ob, ga, gb,
                w_branch_a.astype(BF16), w_branch_b.astype(BF16), w_out.astype(BF16),
                ln1_g, ln1_b, seq=seq, tm=tm2, alpha=alpha)
    out = _ffn(x1, ada3, w_ff1.astype(BF16), b_ff1, w_ff2.astype(BF16), b_ff2, ln2_g, ln2_b,
               seq=seq, tm=tm2, alpha=alpha, fchunk=min(1024, w_ff1.shape[-1]))
    return out.reshape(bsz, seq, d)


def kernel(x, c, w_ada, b_ada, w_in, mu_shift, rw_w0, rw_w2, rw_a0, rw_a2, rw_g2, rw_kk, rw_ka, rw_rk,
           rw_gn_g, rw_gn_b, w_branch_a, w_branch_b, w_out, ln1_g, ln1_b, w_ff1, b_ff1, w_ff2, b_ff2,
           ln2_g, ln2_b):
    in_dtype = x.dtype
    layer_params = (w_ada, b_ada, w_in, mu_shift, rw_w0, rw_w2, rw_a0, rw_a2, rw_g2, rw_kk, rw_ka,
                    rw_rk, rw_gn_g, rw_gn_b, w_branch_a, w_branch_b, w_out, ln1_g, ln1_b,
                    w_ff1, b_ff1, w_ff2, b_ff2, ln2_g, ln2_b)
    depth = w_ada.shape[0]
    alpha = (2.0 * depth) ** 0.25
    for l in range(depth):
        x = _layer(x, c, *[p[l] for p in layer_params], alpha=alpha)
    return x.astype(in_dtype)
```

```python
import functools

import jax
import jax.numpy as jnp
from jax import lax
from jax.experimental import pallas as pl
from jax.experimental.pallas import tpu as pltpu

F32 = jnp.float32
BF16 = jnp.bfloat16

HEAD_DIM = 64
LN_EPS = 1e-5
ADALN_EPS = 1e-6
GN_EPS = 64e-5
RWKV_CHUNK = 64
RWKV_SECTIONS = 2
RWKV_SECTION_CHUNKS = 4
RWKV_SECTION_LAG = 12
RWKV_CHUNKS_PER_STEP = RWKV_SECTIONS * RWKV_SECTION_CHUNKS
SB_BLOCK = 128
INPROJ_ROWS = 512
MLP_ROWS = 1024
VMEM_LIMIT = 56 * 1024 * 1024


def _split2(a):
    hi = a.astype(BF16)
    lo = (a - hi.astype(F32)).astype(BF16)
    return hi, lo


_NN = (((1,), (0,)), ((), ()))
_NT = (((1,), (1,)), ((), ()))
_TN = (((0,), (0,)), ((), ()))


def _dot(a, b, dims=_NN):
    return lax.dot_general(a, b, dims, preferred_element_type=F32)


def _dot3(a, b, dims=_NN):
    ah, al = _split2(a)
    bh, bl = _split2(b)
    return _dot(ah, bh, dims) + (_dot(al, bh, dims) + _dot(ah, bl, dims))


def _head_sums(x, ones_blk):
    m, w = x.shape[0], ones_blk.shape[0]
    hi, lo = _split2(x)
    outs = []
    for g in range(x.shape[1] // w):
        cols = slice(g * w, (g + 1) * w)
        both = _dot(jnp.concatenate([hi[:, cols], lo[:, cols]], axis=0), ones_blk)
        outs.append(both[:m] + both[m:])
    return jnp.concatenate(outs, axis=1)


def _dot_ones_lhs(a_bf, b):
    hi, lo = _split2(b)
    return _dot(a_bf, hi) + _dot(a_bf, lo)


_DONE = object()


def _run_skewed(programs, lag):
    live = [True] * len(programs)
    tick = 0
    while any(live):
        for i, prog in enumerate(programs):
            if live[i] and tick >= i * lag:
                live[i] = next(prog, _DONE) is not _DONE
        tick += 1


_NEG_LOG2_E = -1.4426950408889634


def _softplus(y):
    return jnp.maximum(y, 0.0) + jnp.log(1.0 + jnp.exp2(jnp.abs(y) * _NEG_LOG2_E))


def _sigmoid(y):
    return 1.0 / (1.0 + jnp.exp(-y))


def _norm_rows(x, eps):
    mu = jnp.mean(x, axis=-1, keepdims=True)
    xc = x - mu
    var = jnp.mean(xc * xc, axis=-1, keepdims=True)
    return xc * lax.rsqrt(var + eps)


def _ada_kernel(c_ref, w_ref, b_ref, o_ref):
    c = c_ref[...]
    s = c * _sigmoid(c)
    o_ref[...] = jnp.dot(s, w_ref[...], preferred_element_type=F32,
                         precision=lax.Precision.HIGHEST) + b_ref[...]


def _ada(c, w_ada, b_ada):
    bsz, d = c.shape
    n = w_ada.shape[1]
    return pl.pallas_call(
        _ada_kernel,
        grid=(n // d,),
        in_specs=[pl.BlockSpec((bsz, d), lambda j: (0, 0)),
                  pl.BlockSpec((d, d), lambda j: (0, j)),
                  pl.BlockSpec((1, d), lambda j: (0, j))],
        out_specs=pl.BlockSpec((bsz, d), lambda j: (0, j)),
        out_shape=jax.ShapeDtypeStruct((bsz, n), F32),
        name="ada",
    )(c, w_ada, b_ada.reshape(1, n))


def _inproj_kernel(x_ref, ada_ref, win_ref, mu_ref, w0_ref, w2_ref, a0_ref, a2_ref, g2_ref,
                   kkw_ref, kaw_ref, hsum_ref,
                   r_ref, k_ref, v_ref, lw_ref, kk_ref, b_ref, g_ref,
                   q_ref, ks_ref, vs_ref, ga_ref, gb_ref,
                   carry_ref, *, tiles_per_batch, d_rwkv, d_decay, d_aaa, d_gate, d_sb, d_model):
    i = pl.program_id(0)
    tm = x_ref.shape[0]
    d_shift = 3 * d_rwkv + d_decay + d_aaa + d_gate

    sh = ada_ref[0, 0:1, :]
    sc = ada_ref[0, 1:2, :]
    h = _norm_rows(x_ref[...], ADALN_EPS) * (1.0 + sc) + sh
    hb = h.astype(BF16)

    @pl.when(i % tiles_per_batch == 0)
    def _():
        carry_ref[...] = jnp.zeros_like(carry_ref)

    row0 = lax.broadcasted_iota(jnp.int32, (tm, 1), 0) == 0

    def shifted(lo, width):
        cols = slice(lo, lo + width)
        z = _dot(hb, win_ref[:, cols])
        prev = jnp.where(row0, carry_ref[:, cols], pltpu.roll(z, 1, 0))
        carry_ref[:, cols] = z[tm - 1:tm, :]
        return z + mu_ref[:, cols] * (prev - z)

    half = d_model // 2
    pair_w = 2 * HEAD_DIM
    first_head = lax.broadcasted_iota(jnp.int32, (1, pair_w), 1) < HEAD_DIM

    def store_queries(z):
        zq = (z * (HEAD_DIM ** -0.5)).astype(BF16)
        zero = jnp.zeros((), BF16)
        for p in range(d_sb // pair_w):
            qp = zq[:, p * pair_w:(p + 1) * pair_w]
            q_ref[:, (2 * p) * pair_w:(2 * p + 1) * pair_w] = jnp.where(first_head, qp, zero)
            q_ref[:, (2 * p + 1) * pair_w:(2 * p + 2) * pair_w] = jnp.where(first_head, zero, qp)

    def plain(ref, cols, fn=lambda z: z):
        def store(z):
            ref[:, cols] = fn(z).astype(BF16)
        return store

    rest = [(store_queries, d_sb), (plain(ks_ref, slice(None)), d_sb), (plain(vs_ref, slice(None)), d_sb),
            (plain(ga_ref, slice(0, half), _sigmoid), half), (plain(ga_ref, slice(half, d_model), _sigmoid), half),
            (plain(gb_ref, slice(0, half), _sigmoid), half), (plain(gb_ref, slice(half, d_model), _sigmoid), half)]
    rest_col = [d_shift]

    def project_next():
        store, width = rest.pop(0)
        store(_dot(hb, win_ref[:, rest_col[0]:rest_col[0] + width]))
        rest_col[0] += width

    zl = shifted(3 * d_rwkv, d_decay + d_aaa + d_gate)
    zw = zl[:, :d_decay]
    za = zl[:, d_decay:d_decay + d_aaa]
    zg = zl[:, d_decay + d_aaa:]
    k = shifted(d_rwkv, d_rwkv)
    ww = w0_ref[...] + _dot(jnp.tanh(zw).astype(BF16), w2_ref[...])
    w_log = -_softplus(-ww) - 0.5
    lw_ref[...] = -jnp.exp(w_log)
    project_next()
    arate = _sigmoid(a0_ref[...] + _dot(za.astype(BF16), a2_ref[...]))
    g_ref[...] = _dot(_sigmoid(zg).astype(BF16), g2_ref[...])
    r_ref[...] = shifted(0, d_rwkv)
    kkraw = k * kkw_ref[...]
    ssq = _head_sums(kkraw * kkraw, hsum_ref[...])
    kk = kkraw / jnp.maximum(jnp.sqrt(ssq), 1e-12)
    v_ref[...] = shifted(2 * d_rwkv, d_rwkv)
    k_ref[...] = k * (1.0 + (arate - 1.0) * kaw_ref[...])
    project_next()
    kk_ref[...] = kk
    b_ref[...] = kk * arate
    while rest:
        project_next()


def _inproj(x2, ada3, win_bf, mu_shift, rw_w0, rw_w2, rw_a0, rw_a2, rw_g2, rw_kk, rw_ka, hsum,
            *, seq, tm):
    m, d_model = x2.shape
    d_rwkv = rw_w0.shape[-1]
    d_decay, d_aaa, d_gate = rw_w2.shape[0], rw_a2.shape[0], rw_g2.shape[0]
    d_shift = 3 * d_rwkv + d_decay + d_aaa + d_gate
    d_sb = (win_bf.shape[1] - d_shift - 2 * d_model) // 3
    tiles_per_batch = seq // tm
    row = lambda i: (i, 0)
    const = lambda i: (0, 0)
    full = lambda a: pl.BlockSpec(a.shape, const, pipeline_mode=pl.Buffered(1))
    vec = lambda a: a.reshape(1, -1)
    ins = [x2, ada3, win_bf, vec(mu_shift), vec(rw_w0), rw_w2.astype(BF16), vec(rw_a0),
           rw_a2.astype(BF16), rw_g2.astype(BF16), vec(rw_kk), vec(rw_ka), hsum]
    in_specs = [pl.BlockSpec((tm, d_model), row),
                pl.BlockSpec((1,) + ada3.shape[1:], lambda i: (i // tiles_per_batch, 0, 0))]
    in_specs += [full(a) for a in ins[2:]]
    widths = [d_rwkv] * 7 + [2 * d_sb, d_sb, d_sb] + [d_model] * 2
    dtypes = [F32] * 7 + [BF16] * 5
    out_shape = [jax.ShapeDtypeStruct((m, w), dt) for w, dt in zip(widths, dtypes)]
    out_specs = [pl.BlockSpec((tm, w), row) for w in widths]
    kern = functools.partial(_inproj_kernel, tiles_per_batch=tiles_per_batch, d_rwkv=d_rwkv,
                             d_decay=d_decay, d_aaa=d_aaa, d_gate=d_gate, d_sb=d_sb, d_model=d_model)
    return pl.pallas_call(
        kern,
        grid=(m // tm,),
        in_specs=in_specs,
        out_specs=out_specs,
        out_shape=out_shape,
        scratch_shapes=[pltpu.VMEM((1, d_shift), F32)],
        compiler_params=pltpu.CompilerParams(dimension_semantics=("arbitrary",),
                                             vmem_limit_bytes=VMEM_LIMIT),
        name="inproj",
    )(*ins)


RWKV_GROUP_HEADS = 4


def _block_diag(q, head_masks):
    zero = jnp.zeros((), q.dtype)
    return jnp.concatenate([jnp.where(m, q, zero) for m in head_masks], axis=0)


def _packed_dot1(lhs, q, head_masks, dims=_NN):
    return _dot(lhs.astype(BF16), _block_diag(q.astype(BF16), head_masks), dims)


def _packed_dot_wide_lhs(lhs, q, head_masks, dims=_NN):
    m = lhs.shape[0]
    lh, ll = _split2(lhs)
    both = _dot(jnp.concatenate([lh, ll], axis=0), _block_diag(q.astype(BF16), head_masks), dims)
    return both[:m] + both[m:]


def _rwkv_kernel(r_ref, k_ref, v_ref, lw_ref, kk_ref, b_ref, g_ref, rk_ref, gng_ref, gnb_ref, hsum_ref,
                 o_ref, state_ref, *, chunk):
    c = pl.program_id(1)
    n = HEAD_DIM
    rows, d = r_ref.shape[1], r_ref.shape[2]
    gw = RWKV_GROUP_HEADS * n
    n_groups = d // gw

    @pl.when(c == 0)
    def _():
        state_ref[...] = jnp.zeros_like(state_ref)

    t_idx = lax.broadcasted_iota(jnp.int32, (chunk, gw), 0)
    lane = lax.broadcasted_iota(jnp.int32, (chunk, gw), 1)
    i_idx = lane & (n - 1)
    strict = t_idx > i_idx
    incl = t_idx >= i_idx
    lane1 = lax.broadcasted_iota(jnp.int32, (1, gw), 1)
    hm = [(lane1 >> (n.bit_length() - 1)) == h for h in range(RWKV_GROUP_HEADS)]
    same = lambda s: (t_idx >> s) == (i_idx >> s)

    half_w = gw // 2
    lane_half = [(lane1 >> (half_w.bit_length() - 1)) == j for j in range(2)]
    second_in_half = ((lane >> (n.bit_length() - 1)) & 1) == 1

    def halves_rows(x):
        return jnp.concatenate([x[:, :half_w], x[:, half_w:]], axis=0).astype(BF16)

    def halves_masked(w):
        return _block_diag(w.astype(BF16), lane_half)

    def same_head(out):
        return jnp.where(second_in_half, out[n:], out[:n])

    lane_h = lax.broadcasted_iota(jnp.int32, (1, half_w), 1)
    in_half = [(lane_h >> (n.bit_length() - 1)) == j for j in range(half_w // n)]

    def two_rhs(lhs, x, y, dims=_NN):
        lb, xb, yb = lhs.astype(BF16), x.astype(BF16), y.astype(BF16)
        out_x, out_y = [], []
        for j in range(gw // half_w):
            cols = slice(j * half_w, (j + 1) * half_w)
            w = jnp.concatenate([_block_diag(xb[:, cols], in_half), _block_diag(yb[:, cols], in_half)],
                                axis=1 if dims == _NN else 0)
            o = _dot(lb[:, cols], w, dims)
            out_x.append(o[:, :half_w])
            out_y.append(o[:, half_w:])
        return jnp.concatenate(out_x, axis=1), jnp.concatenate(out_y, axis=1)

    sec_rows = rows // RWKV_SECTIONS
    sec_chunks = sec_rows // chunk
    rr = lax.broadcasted_iota(jnp.int32, (sec_rows, sec_rows), 0)
    cc = lax.broadcasted_iota(jnp.int32, (sec_rows, sec_rows), 1)
    cs = chunk.bit_length() - 1
    ltri = jnp.where((rr >= cc) & ((rr >> cs) == (cc >> cs)), 1.0, 0.0).astype(BF16)
    eye = jnp.where(t_idx == i_idx, 1.0, 0.0)
    chains = [(ci, gi) for ci in range(sec_chunks) for gi in range(n_groups)]
    win = lambda x, ci, gi: x[ci * chunk:(ci + 1) * chunk, gi * gw:(gi + 1) * gw]

    def section(row0):
        rs = slice(row0, row0 + sec_rows)
        r, k, v, lw = r_ref[0, rs, :], k_ref[0, rs, :], v_ref[0, rs, :], lw_ref[0, rs, :]
        kk, b = kk_ref[0, rs, :], b_ref[0, rs, :]
        cum = _dot_ones_lhs(ltri, lw)
        a_t = -kk * jnp.exp(cum - lw)
        r_t = r * jnp.exp(cum)
        e_neg = jnp.exp(-cum)
        b_t = b * e_neg
        k_t = k * e_neg
        tots = [cum[(ci + 1) * chunk - 1:(ci + 1) * chunk, :] for ci in range(sec_chunks)]
        e_bars = [jnp.exp(tots[ci] - cum[ci * chunk:(ci + 1) * chunk, :]) for ci in range(sec_chunks)]
        b_bar = [win(b, ci, gi) * e_bars[ci][:, gi * gw:(gi + 1) * gw] for ci, gi in chains]
        k_bar = [win(k, ci, gi) * e_bars[ci][:, gi * gw:(gi + 1) * gw] for ci, gi in chains]
        at = [win(a_t, ci, gi) for ci, gi in chains]
        rt = [win(r_t, ci, gi) for ci, gi in chains]
        vv = [win(v, ci, gi) for ci, gi in chains]
        yield

        ar = [jnp.concatenate([a_, r_], axis=0) for a_, r_ in zip(at, rt)]
        grams = [two_rhs(x, win(b_t, ci, gi), win(k_t, ci, gi), _NT) for x, (ci, gi) in zip(ar, chains)]
        gram_b = [x[0] for x in grams]
        gram_k = [x[1] for x in grams]
        yield
        m_ab = [jnp.where(strict, g_[:chunk], 0.0) for g_ in gram_b]
        p_rb = [jnp.where(incl, g_[chunk:], 0.0) for g_ in gram_b]
        m_ak = [jnp.where(strict, g_[:chunk], 0.0) for g_ in gram_k]
        p_rk = [jnp.where(incl, g_[chunk:], 0.0) for g_ in gram_k]
        yield

        ts = [eye + jnp.where(same(1), m, 0.0) for m in m_ab]
        s = 1
        while (1 << s) < chunk:
            level = same(s + 1) & jnp.logical_not(same(s))
            offs = [jnp.where(level, m, 0.0) for m in m_ab]
            mids = [_packed_dot1(t, o, hm) for t, o in zip(ts, offs)]
            yield
            ts = [t + _packed_dot1(md, t, hm) for t, md in zip(ts, mids)]
            yield
            s += 1

        both = [_packed_dot1(jnp.concatenate([m, pk], axis=0), x, hm) for m, pk, x in zip(m_ak, p_rk, vv)]
        mv = [x[:chunk] for x in both]
        pkv = [x[chunk:] for x in both]
        yield
        hat = [two_rhs(t, x, y) for t, x, y in zip(ts, at, mv)]
        a_hat = [x[0] for x in hat]
        u0 = [x[1] for x in hat]
        yield
        pb = [two_rhs(p, a, u) for p, a, u in zip(p_rb, a_hat, u0)]
        r_hat = [x + y[0] for x, y in zip(rt, pb)]
        y0 = [x + y[1] for x, y in zip(pkv, pb)]
        yield
        g_mat = [same_head(_dot(halves_rows(a), halves_masked(bb), _TN))
                 for a, bb in zip(a_hat, b_bar)]
        h_mat = [same_head(_dot(jnp.concatenate([halves_rows(u), halves_rows(x)], axis=0),
                                jnp.concatenate([halves_masked(bb), halves_masked(kb)], axis=0), _TN))
                 for u, x, bb, kb in zip(u0, vv, b_bar, k_bar)]
        yield

        y_rows = []
        for ci in range(sec_chunks):
            y_groups = []
            for gi in range(n_groups):
                j = ci * n_groups + gi
                s0 = state_ref[gi]
                y_groups.append(_packed_dot1(r_hat[j], s0, hm, _NT) + y0[j])
                w_tot = jnp.exp(tots[ci][:, gi * gw:(gi + 1) * gw])
                state_ref[gi] = s0 * w_tot + _packed_dot_wide_lhs(s0, g_mat[j], hm) + h_mat[j]
            y_rows.append(jnp.concatenate(y_groups, axis=1))
            yield
        y = jnp.concatenate(y_rows, axis=0)

        hsum = hsum_ref[...]
        inv_n = 1.0 / n
        mu = _head_sums(y, hsum) * inv_n
        yc = y - mu
        var = _head_sums(yc * yc, hsum) * inv_n
        yn = yc * lax.rsqrt(var + GN_EPS) * gng_ref[...] + gnb_ref[...]
        bonus = _head_sums(r * k * rk_ref[...], hsum)
        o_ref[0, rs, :] = ((yn + bonus * v) * g_ref[0, rs, :]).astype(o_ref.dtype)

    assert RWKV_SECTION_LAG >= sec_chunks
    _run_skewed([section(si * sec_rows) for si in range(RWKV_SECTIONS)], lag=RWKV_SECTION_LAG)


def _rwkv(r, k, v, lw, kk, b, g, rw_rk, gn_g, gn_b, hsum, *, chunk, rows):
    bsz, seq, d = r.shape
    n_groups = d // (RWKV_GROUP_HEADS * HEAD_DIM)
    blk = pl.BlockSpec((1, rows, d), lambda bi, ci: (bi, ci, 0))
    const = lambda bi, ci: (0, 0)
    vec = lambda a: a.reshape(1, d)
    return pl.pallas_call(
        functools.partial(_rwkv_kernel, chunk=chunk),
        grid=(bsz, seq // rows),
        in_specs=[blk] * 7 + [pl.BlockSpec((1, d), const)] * 3 + [pl.BlockSpec(hsum.shape, const)],
        out_specs=blk,
        out_shape=jax.ShapeDtypeStruct((bsz, seq, d), BF16),
        scratch_shapes=[pltpu.VMEM((n_groups, HEAD_DIM, RWKV_GROUP_HEADS * HEAD_DIM), F32)],
        compiler_params=pltpu.CompilerParams(dimension_semantics=("arbitrary", "arbitrary"),
                                             vmem_limit_bytes=VMEM_LIMIT),
        name="rwkv",
    )(r, k, v, lw, kk, b, g, vec(rw_rk), vec(gn_g), vec(gn_b), hsum)


SB_LOG_UNDERFLOW = 105.0


def _sb_kernel(q_ref, k_ref, v_ref, o_ref, acc_ref, run_ref, *, blk):
    qi = pl.program_id(1)
    pairs = k_ref.shape[2] // (2 * HEAD_DIM)
    row = lax.broadcasted_iota(jnp.int32, (blk, blk), 0)
    col = lax.broadcasted_iota(jnp.int32, (blk, blk), 1)
    causal = col < row
    jj = lax.broadcasted_iota(jnp.int32, (blk, 2 * blk), 0)
    ss = lax.broadcasted_iota(jnp.int32, (blk, 2 * blk), 1)
    suffix = jnp.where((jj >= ss) | (ss >= blk), 1.0, 0.0).astype(BF16)
    first_head = lax.broadcasted_iota(jnp.int32, (1, 2 * HEAD_DIM), 1) < HEAD_DIM
    zero = jnp.zeros((), BF16)

    lanes = [slice(p * 2 * HEAD_DIM, (p + 1) * 2 * HEAD_DIM) for p in range(pairs)]
    heads = [(p, hh) for p in range(pairs) for hh in range(2)]

    nh = len(heads)
    pw = 2 * HEAD_DIM

    def tiles(kbs, diagonal, absent=None):
        starts = [pl.multiple_of(kb * blk, blk) for kb in kbs]
        zs = [None] * (len(kbs) * nh)
        for p in range(pairs):
            q2 = jnp.concatenate([q_ref[0, :, (2 * p + hh) * pw:(2 * p + hh + 1) * pw] for hh in range(2)],
                                 axis=0)
            keys = jnp.concatenate([k_ref[0, pl.ds(start, blk), lanes[p]] for start in starts], axis=0)
            z2 = _dot(q2, keys, _NT)
            for t_ in range(len(kbs)):
                for hh in range(2):
                    zs[t_ * nh + 2 * p + hh] = z2[hh * blk:(hh + 1) * blk, t_ * blk:(t_ + 1) * blk]
        sps = [_softplus(z) for z in zs]
        if diagonal:
            sps = [jnp.where(causal, sp, 0.0) for sp in sps[:nh]] + sps[nh:]
        css = [None] * len(zs)
        for p in range(pairs):
            idx = [t_ * nh + 2 * p + hh for t_ in range(len(kbs)) for hh in range(2)]
            c_ = _dot(jnp.concatenate([sps[i_] for i_ in idx], axis=0).astype(BF16), suffix)
            for j, i_ in enumerate(idx):
                css[i_] = c_[j * blk:(j + 1) * blk]
        atts = [[None] * nh for _ in kbs]
        low = None
        for n_ in range(nh):
            run = None if diagonal else run_ref[n_]
            for t_ in range(len(kbs)):
                z, cs = zs[t_ * nh + n_], css[t_ * nh + n_]
                logw = z - cs[:, :blk]
                if run is None:
                    att = jnp.where(causal, jnp.exp(logw), 0.0)
                    run = cs[:, blk:]
                else:
                    att = jnp.exp(logw - run)
                    run = run + cs[:, blk:]
                if t_ == 0 and absent is not None:
                    run = run + absent
                atts[t_][n_] = att.astype(BF16)
            run_ref[n_] = run
            low = run if low is None else jnp.minimum(low, run)
        done = (jnp.min(low) > SB_LOG_UNDERFLOW).astype(jnp.int32)
        for p in range(pairs):
            vvs = []
            for start in starts:
                vp = v_ref[0, pl.ds(start, blk), lanes[p]]
                vvs += [jnp.where(first_head, vp, zero), jnp.where(first_head, zero, vp)]
            lhs = jnp.concatenate([atts[t_][2 * p + hh] for t_ in range(len(kbs)) for hh in range(2)], axis=1)
            out = _dot(lhs, jnp.concatenate(vvs, axis=0))
            if diagonal:
                acc_ref[:, lanes[p]] = out
            else:
                acc_ref[:, lanes[p]] += out
        return done

    def pair_of_tiles(i, diagonal):
        second = qi - i - 1
        absent = jnp.where(second < 0, 1e4, 0.0).astype(F32)
        return tiles([qi - i, jnp.maximum(second, 0)], diagonal, absent)

    done0 = pair_of_tiles(0, True)

    def cond(c):
        i, done = c
        return jnp.logical_and(i <= qi, done == 0)

    def body(c):
        i, _ = c
        return i + 2, pair_of_tiles(i, False)

    lax.while_loop(cond, body, (jnp.int32(2), done0))
    o_ref[0] = acc_ref[...].astype(o_ref.dtype)


def _sb_attention(q, k, v, *, blk):
    bsz, seq, d = k.shape
    heads = d // HEAD_DIM
    qspec = pl.BlockSpec((1, blk, 2 * d), lambda b, i: (b, i, 0))
    ospec = pl.BlockSpec((1, blk, d), lambda b, i: (b, i, 0))
    kvspec = pl.BlockSpec((1, seq, d), lambda b, i: (b, 0, 0), pipeline_mode=pl.Buffered(1))
    return pl.pallas_call(
        functools.partial(_sb_kernel, blk=blk),
        grid=(bsz, seq // blk),
        in_specs=[qspec, kvspec, kvspec],
        out_specs=ospec,
        out_shape=jax.ShapeDtypeStruct((bsz, seq, d), BF16),
        scratch_shapes=[pltpu.VMEM((blk, d), F32), pltpu.VMEM((heads, blk, blk), F32)],
        compiler_params=pltpu.CompilerParams(dimension_semantics=("arbitrary", "arbitrary"),
                                             vmem_limit_bytes=VMEM_LIMIT),
        name="sbattn",
    )(q, k, v)


def _merge_kernel(x_ref, ada_ref, ya_ref, ob_ref, ga_ref, gb_ref, wa_ref, wb_ref, wo_ref,
                  lng_ref, lnb_ref, o_ref, *, alpha):
    g1 = ada_ref[0, 2:3, :]
    half = x_ref.shape[0] // 2

    def rows_program(rows):
        y_a = _dot(ya_ref[rows, :], wa_ref[...])
        y_b = _dot(ob_ref[rows, :], wb_ref[...])
        yield
        merged = ga_ref[rows, :].astype(F32) * y_a + gb_ref[rows, :].astype(F32) * y_b
        mix = _dot(merged.astype(BF16), wo_ref[...])
        yield
        o_ref[rows, :] = _norm_rows(alpha * x_ref[rows, :] + g1 * mix, LN_EPS) * lng_ref[...] + lnb_ref[...]

    _run_skewed([rows_program(slice(0, half)), rows_program(slice(half, 2 * half))], lag=1)


def _merge(x2, ada3, ya, ob, ga, gb, wa_bf, wb_bf, wo_bf, ln_g, ln_b, *, seq, tm, alpha):
    m, d = x2.shape
    tiles_per_batch = seq // tm
    row = lambda i: (i, 0)
    const = lambda i: (0, 0)
    rows = lambda a: pl.BlockSpec((tm, a.shape[1]), row)
    full = lambda a: pl.BlockSpec(a.shape, const, pipeline_mode=pl.Buffered(1))
    vec = lambda a: a.reshape(1, -1)
    ins = [x2, ada3, ya, ob, ga, gb, wa_bf, wb_bf, wo_bf, vec(ln_g), vec(ln_b)]
    in_specs = [rows(x2), pl.BlockSpec((1,) + ada3.shape[1:], lambda i: (i // tiles_per_batch, 0, 0)),
                rows(ya), rows(ob), rows(ga), rows(gb)] + [full(a) for a in ins[6:]]
    return pl.pallas_call(
        functools.partial(_merge_kernel, alpha=alpha),
        grid=(m // tm,),
        in_specs=in_specs,
        out_specs=pl.BlockSpec((tm, d), row),
        out_shape=jax.ShapeDtypeStruct((m, d), F32),
        compiler_params=pltpu.CompilerParams(dimension_semantics=("arbitrary",),
                                             vmem_limit_bytes=VMEM_LIMIT),
        name="merge",
    )(*ins)


def _ffn_kernel(x_ref, ada_ref, w1_ref, b1_ref, w2_ref, b2_ref, lng_ref, lnb_ref, o_ref, *, alpha, fchunk):
    sh = ada_ref[0, 3:4, :]
    sc = ada_ref[0, 4:5, :]
    g2 = ada_ref[0, 5:6, :]
    d_ff = w1_ref.shape[1]
    n_chunks = d_ff // fchunk
    half = x_ref.shape[0] // 2

    def rows_program(rows):
        x = x_ref[rows, :]
        hb = (_norm_rows(x, ADALN_EPS) * (1.0 + sc) + sh).astype(BF16)
        yield
        ff = None
        for j in range(n_chunks):
            sl = slice(j * fchunk, (j + 1) * fchunk)
            t = jnp.maximum(_dot(hb, w1_ref[:, sl]) + b1_ref[:, sl], 0.0)
            part = _dot((t * t).astype(BF16), w2_ref[sl, :])
            ff = part if ff is None else ff + part
            yield
        ff = ff + b2_ref[...]
        o_ref[rows, :] = _norm_rows(alpha * x + g2 * ff, LN_EPS) * lng_ref[...] + lnb_ref[...]

    _run_skewed([rows_program(slice(0, half)), rows_program(slice(half, 2 * half))], lag=2)


def _ffn(x1, ada3, w1_bf, b1, w2_bf, b2, ln_g, ln_b, *, seq, tm, alpha, fchunk):
    m, d = x1.shape
    tiles_per_batch = seq // tm
    row = lambda i: (i, 0)
    const = lambda i: (0, 0)
    full = lambda a: pl.BlockSpec(a.shape, const, pipeline_mode=pl.Buffered(1))
    vec = lambda a: a.reshape(1, -1)
    ins = [x1, ada3, w1_bf, vec(b1), w2_bf, vec(b2), vec(ln_g), vec(ln_b)]
    in_specs = [pl.BlockSpec((tm, d), row),
                pl.BlockSpec((1,) + ada3.shape[1:], lambda i: (i // tiles_per_batch, 0, 0))]
    in_specs += [full(a) for a in ins[2:]]
    return pl.pallas_call(
        functools.partial(_ffn_kernel, alpha=alpha, fchunk=fchunk),
        grid=(m // tm,),
        in_specs=in_specs,
        out_specs=pl.BlockSpec((tm, d), row),
        out_shape=jax.ShapeDtypeStruct((m, d), F32),
        compiler_params=pltpu.CompilerParams(dimension_semantics=("arbitrary",),
                                             vmem_limit_bytes=VMEM_LIMIT),
        name="ffn",
    )(*ins)


def _branches(proj, rw_rk, rw_gn_g, rw_gn_b, hsum, bsz, seq):
    r, k, v, lw, kk, b, g, q, ks, vs = proj
    seq3 = lambda a: a.reshape(bsz, seq, -1)
    ya = _rwkv(seq3(r), seq3(k), seq3(v), seq3(lw), seq3(kk), seq3(b), seq3(g),
               rw_rk, rw_gn_g, rw_gn_b, hsum, chunk=min(RWKV_CHUNK, seq),
               rows=min(RWKV_CHUNKS_PER_STEP * RWKV_CHUNK, seq))
    o = _sb_attention(seq3(q), seq3(ks), seq3(vs), blk=min(SB_BLOCK, seq))
    return ya.reshape(bsz * seq, -1), o.reshape(bsz * seq, -1)


def _layer(x, c, w_ada, b_ada, w_in, mu_shift, rw_w0, rw_w2, rw_a0, rw_a2, rw_g2, rw_kk, rw_ka,
           rw_rk, rw_gn_g, rw_gn_b, w_branch_a, w_branch_b, w_out, ln1_g, ln1_b,
           w_ff1, b_ff1, w_ff2, b_ff2, ln2_g, ln2_b, *, alpha):
    bsz, seq, d = x.shape
    d_rwkv = rw_w0.shape[-1]
    tm = min(INPROJ_ROWS, seq)
    tm2 = min(MLP_ROWS, seq)

    ada3 = _ada(c, w_ada, b_ada).reshape(bsz, 6, d)
    lane = jnp.arange(RWKV_GROUP_HEADS * HEAD_DIM) // HEAD_DIM
    hsum = (lane[:, None] == lane[None, :]).astype(BF16)

    x2 = x.reshape(bsz * seq, d)
    (r, k, v, lw, kk, b, g, q, ks, vs, ga, gb) = _inproj(
        x2, ada3, w_in.astype(BF16), mu_shift, rw_w0, rw_w2, rw_a0, rw_a2, rw_g2, rw_kk, rw_ka, hsum,
        seq=seq, tm=tm)

    ya, ob = _branches((r, k, v, lw, kk, b, g, q, ks, vs), rw_rk, rw_gn_g, rw_gn_b, hsum, bsz, seq)

    x1 = _merge(x2, ada3, ya, ob, ga, gb,
                w_branch_a.astype(BF16), w_branch_b.astype(BF16), w_out.astype(BF16),
                ln1_g, ln1_b, seq=seq, tm=tm2, alpha=alpha)
    out = _ffn(x1, ada3, w_ff1.astype(BF16), b_ff1, w_ff2.astype(BF16), b_ff2, ln2_g, ln2_b,
               seq=seq, tm=tm2, alpha=alpha, fchunk=min(1024, w_ff1.shape[-1]))
    return out.reshape(bsz, seq, d)


def kernel(x, c, w_ada, b_ada, w_in, mu_shift, rw_w0, rw_w2, rw_a0, rw_a2, rw_g2, rw_kk, rw_ka, rw_rk,
           rw_gn_g, rw_gn_b, w_branch_a, w_branch_b, w_out, ln1_g, ln1_b, w_ff1, b_ff1, w_ff2, b_ff2,
           ln2_g, ln2_b):
    in_dtype = x.dtype
    layer_params = (w_ada, b_ada, w_in, mu_shift, rw_w0, rw_w2, rw_a0, rw_a2, rw_g2, rw_kk, rw_ka,
                    rw_rk, rw_gn_g, rw_gn_b, w_branch_a, w_branch_b, w_out, ln1_g, ln1_b,
                    w_ff1, b_ff1, w_ff2, b_ff2, ln2_g, ln2_b)
    depth = w_ada.shape[0]
    alpha = (2.0 * depth) ** 0.25
    for l in range(depth):
        x = _layer(x, c, *[p[l] for p in layer_params], alpha=alpha)
    return x.astype(in_dtype)
```

```python
import functools

import jax
import jax.numpy as jnp
from jax import lax
from jax.experimental import pallas as pl
from jax.experimental.pallas import tpu as pltpu

F32 = jnp.float32
BF16 = jnp.bfloat16

HEAD_DIM = 64
LN_EPS = 1e-5
ADALN_EPS = 1e-6
GN_EPS = 64e-5
RWKV_CHUNK = 64
RWKV_SECTIONS = 1
RWKV_SECTION_CHUNKS = 8
RWKV_SECTION_LAG = 12
RWKV_CHUNKS_PER_STEP = RWKV_SECTIONS * RWKV_SECTION_CHUNKS
SB_BLOCK = 128
INPROJ_ROWS = 512
MLP_ROWS = 1024
VMEM_LIMIT = 56 * 1024 * 1024


def _split2(a):
    hi = a.astype(BF16)
    lo = (a - hi.astype(F32)).astype(BF16)
    return hi, lo


_NN = (((1,), (0,)), ((), ()))
_NT = (((1,), (1,)), ((), ()))
_TN = (((0,), (0,)), ((), ()))


def _dot(a, b, dims=_NN):
    return lax.dot_general(a, b, dims, preferred_element_type=F32)


def _dot3(a, b, dims=_NN):
    ah, al = _split2(a)
    bh, bl = _split2(b)
    return _dot(ah, bh, dims) + (_dot(al, bh, dims) + _dot(ah, bl, dims))


def _head_sums(x, ones_blk):
    m, w = x.shape[0], ones_blk.shape[0]
    hi, lo = _split2(x)
    outs = []
    for g in range(x.shape[1] // w):
        cols = slice(g * w, (g + 1) * w)
        both = _dot(jnp.concatenate([hi[:, cols], lo[:, cols]], axis=0), ones_blk)
        outs.append(both[:m] + both[m:])
    return jnp.concatenate(outs, axis=1)


def _dot_ones_lhs(a_bf, b):
    hi, lo = _split2(b)
    return _dot(a_bf, hi) + _dot(a_bf, lo)


_DONE = object()


def _run_skewed(programs, lag):
    live = [True] * len(programs)
    tick = 0
    while any(live):
        for i, prog in enumerate(programs):
            if live[i] and tick >= i * lag:
                live[i] = next(prog, _DONE) is not _DONE
        tick += 1


_NEG_LOG2_E = -1.4426950408889634


def _softplus(y):
    return jnp.maximum(y, 0.0) + jnp.log(1.0 + jnp.exp2(jnp.abs(y) * _NEG_LOG2_E))


def _sigmoid(y):
    return 1.0 / (1.0 + jnp.exp(-y))


def _norm_rows(x, eps):
    mu = jnp.mean(x, axis=-1, keepdims=True)
    xc = x - mu
    var = jnp.mean(xc * xc, axis=-1, keepdims=True)
    return xc * lax.rsqrt(var + eps)


def _ada_kernel(c_ref, w_ref, b_ref, o_ref):
    c = c_ref[...]
    s = c * _sigmoid(c)
    o_ref[...] = jnp.dot(s, w_ref[...], preferred_element_type=F32,
                         precision=lax.Precision.HIGHEST) + b_ref[...]


def _ada(c, w_ada, b_ada):
    bsz, d = c.shape
    n = w_ada.shape[1]
    return pl.pallas_call(
        _ada_kernel,
        grid=(n // d,),
        in_specs=[pl.BlockSpec((bsz, d), lambda j: (0, 0)),
                  pl.BlockSpec((d, d), lambda j: (0, j)),
                  pl.BlockSpec((1, d), lambda j: (0, j))],
        out_specs=pl.BlockSpec((bsz, d), lambda j: (0, j)),
        out_shape=jax.ShapeDtypeStruct((bsz, n), F32),
        name="ada",
    )(c, w_ada, b_ada.reshape(1, n))


def _inproj_kernel(x_ref, ada_ref, win_ref, mu_ref, w0_ref, w2_ref, a0_ref, a2_ref, g2_ref,
                   kkw_ref, kaw_ref, hsum_ref,
                   r_ref, k_ref, v_ref, lw_ref, kk_ref, b_ref, g_ref,
                   q_ref, ks_ref, vs_ref, ga_ref, gb_ref,
                   carry_ref, *, tiles_per_batch, d_rwkv, d_decay, d_aaa, d_gate, d_sb, d_model):
    i = pl.program_id(0)
    tm = x_ref.shape[0]
    d_shift = 3 * d_rwkv + d_decay + d_aaa + d_gate

    sh = ada_ref[0, 0:1, :]
    sc = ada_ref[0, 1:2, :]

    @pl.when(i % tiles_per_batch == 0)
    def _():
        carry_ref[...] = jnp.zeros_like(carry_ref)

    half = d_model // 2
    pair_w = 2 * HEAD_DIM
    first_head = lax.broadcasted_iota(jnp.int32, (1, pair_w), 1) < HEAD_DIM

    def rows_program(rows):
        hm = rows.stop - rows.start
        hb = (_norm_rows(x_ref[rows, :], ADALN_EPS) * (1.0 + sc) + sh).astype(BF16)
        row0 = lax.broadcasted_iota(jnp.int32, (hm, 1), 0) == 0
        yield

        def shifted(lo, width):
            cols = slice(lo, lo + width)
            z = _dot(hb, win_ref[:, cols])
            prev = jnp.where(row0, carry_ref[:, cols], pltpu.roll(z, 1, 0))
            carry_ref[:, cols] = z[hm - 1:hm, :]
            return z + mu_ref[:, cols] * (prev - z)

        def store_queries(z):
            zq = (z * (HEAD_DIM ** -0.5)).astype(BF16)
            zero = jnp.zeros((), BF16)
            for p in range(d_sb // pair_w):
                qp = zq[:, p * pair_w:(p + 1) * pair_w]
                q_ref[rows, (2 * p) * pair_w:(2 * p + 1) * pair_w] = jnp.where(first_head, qp, zero)
                q_ref[rows, (2 * p + 1) * pair_w:(2 * p + 2) * pair_w] = jnp.where(first_head, zero, qp)

        def plain(ref, cols, fn=lambda z: z):
            def store(z):
                ref[rows, cols] = fn(z).astype(BF16)
            return store

        all_cols = slice(0, d_sb)
        rest = [(store_queries, d_sb), (plain(ks_ref, all_cols), d_sb), (plain(vs_ref, all_cols), d_sb),
                (plain(ga_ref, slice(0, half), _sigmoid), half),
                (plain(ga_ref, slice(half, d_model), _sigmoid), half),
                (plain(gb_ref, slice(0, half), _sigmoid), half),
                (plain(gb_ref, slice(half, d_model), _sigmoid), half)]
        rest_col = [d_shift]

        def project_next():
            store, width = rest.pop(0)
            store(_dot(hb, win_ref[:, rest_col[0]:rest_col[0] + width]))
            rest_col[0] += width

        zl = shifted(3 * d_rwkv, d_decay + d_aaa + d_gate)
        zw = zl[:, :d_decay]
        za = zl[:, d_decay:d_decay + d_aaa]
        zg = zl[:, d_decay + d_aaa:]
        k = shifted(d_rwkv, d_rwkv)
        yield
        ww = w0_ref[...] + _dot(jnp.tanh(zw).astype(BF16), w2_ref[...])
        w_log = -_softplus(-ww) - 0.5
        lw_ref[rows, :] = -jnp.exp(w_log)
        project_next()
        yield
        arate = _sigmoid(a0_ref[...] + _dot(za.astype(BF16), a2_ref[...]))
        g_ref[rows, :] = _dot(_sigmoid(zg).astype(BF16), g2_ref[...])
        r_ref[rows, :] = shifted(0, d_rwkv)
        yield
        kkraw = k * kkw_ref[...]
        ssq = _head_sums(kkraw * kkraw, hsum_ref[...])
        kk = kkraw / jnp.maximum(jnp.sqrt(ssq), 1e-12)
        v_ref[rows, :] = shifted(2 * d_rwkv, d_rwkv)
        k_ref[rows, :] = k * (1.0 + (arate - 1.0) * kaw_ref[...])
        yield
        project_next()
        kk_ref[rows, :] = kk
        b_ref[rows, :] = kk * arate
        yield
        while rest:
            project_next()
            yield

    hm = tm // 2
    _run_skewed([rows_program(slice(0, hm)), rows_program(slice(hm, tm))], lag=2)


def _inproj(x2, ada3, win_bf, mu_shift, rw_w0, rw_w2, rw_a0, rw_a2, rw_g2, rw_kk, rw_ka, hsum,
            *, seq, tm):
    m, d_model = x2.shape
    d_rwkv = rw_w0.shape[-1]
    d_decay, d_aaa, d_gate = rw_w2.shape[0], rw_a2.shape[0], rw_g2.shape[0]
    d_shift = 3 * d_rwkv + d_decay + d_aaa + d_gate
    d_sb = (win_bf.shape[1] - d_shift - 2 * d_model) // 3
    tiles_per_batch = seq // tm
    row = lambda i: (i, 0)
    const = lambda i: (0, 0)
    full = lambda a: pl.BlockSpec(a.shape, const, pipeline_mode=pl.Buffered(1))
    vec = lambda a: a.reshape(1, -1)
    ins = [x2, ada3, win_bf, vec(mu_shift), vec(rw_w0), rw_w2.astype(BF16), vec(rw_a0),
           rw_a2.astype(BF16), rw_g2.astype(BF16), vec(rw_kk), vec(rw_ka), hsum]
    in_specs = [pl.BlockSpec((tm, d_model), row),
                pl.BlockSpec((1,) + ada3.shape[1:], lambda i: (i // tiles_per_batch, 0, 0))]
    in_specs += [full(a) for a in ins[2:]]
    widths = [d_rwkv] * 7 + [2 * d_sb, d_sb, d_sb] + [d_model] * 2
    dtypes = [F32] * 7 + [BF16] * 5
    out_shape = [jax.ShapeDtypeStruct((m, w), dt) for w, dt in zip(widths, dtypes)]
    out_specs = [pl.BlockSpec((tm, w), row) for w in widths]
    kern = functools.partial(_inproj_kernel, tiles_per_batch=tiles_per_batch, d_rwkv=d_rwkv,
                             d_decay=d_decay, d_aaa=d_aaa, d_gate=d_gate, d_sb=d_sb, d_model=d_model)
    return pl.pallas_call(
        kern,
        grid=(m // tm,),
        in_specs=in_specs,
        out_specs=out_specs,
        out_shape=out_shape,
        scratch_shapes=[pltpu.VMEM((1, d_shift), F32)],
        compiler_params=pltpu.CompilerParams(dimension_semantics=("arbitrary",),
                                             vmem_limit_bytes=VMEM_LIMIT),
        name="inproj",
    )(*ins)


RWKV_GROUP_HEADS = 4


def _block_diag(q, head_masks):
    zero = jnp.zeros((), q.dtype)
    return jnp.concatenate([jnp.where(m, q, zero) for m in head_masks], axis=0)


def _packed_dot1(lhs, q, head_masks, dims=_NN):
    return _dot(lhs.astype(BF16), _block_diag(q.astype(BF16), head_masks), dims)


def _packed_dot_wide_lhs(lhs, q, head_masks, dims=_NN):
    m = lhs.shape[0]
    lh, ll = _split2(lhs)
    both = _dot(jnp.concatenate([lh, ll], axis=0), _block_diag(q.astype(BF16), head_masks), dims)
    return both[:m] + both[m:]


def _rwkv_kernel(r_ref, k_ref, v_ref, lw_ref, kk_ref, b_ref, g_ref, rk_ref, gng_ref, gnb_ref, hsum_ref,
                 o_ref, state_ref, *, chunk):
    c = pl.program_id(1)
    n = HEAD_DIM
    rows, d = r_ref.shape[1], r_ref.shape[2]
    gw = RWKV_GROUP_HEADS * n
    n_groups = d // gw

    @pl.when(c == 0)
    def _():
        state_ref[...] = jnp.zeros_like(state_ref)

    t_idx = lax.broadcasted_iota(jnp.int32, (chunk, gw), 0)
    lane = lax.broadcasted_iota(jnp.int32, (chunk, gw), 1)
    i_idx = lane & (n - 1)
    strict = t_idx > i_idx
    incl = t_idx >= i_idx
    lane1 = lax.broadcasted_iota(jnp.int32, (1, gw), 1)
    hm = [(lane1 >> (n.bit_length() - 1)) == h for h in range(RWKV_GROUP_HEADS)]
    same = lambda s: (t_idx >> s) == (i_idx >> s)

    half_w = gw // 2
    lane_half = [(lane1 >> (half_w.bit_length() - 1)) == j for j in range(2)]
    second_in_half = ((lane >> (n.bit_length() - 1)) & 1) == 1

    def halves_rows(x):
        return jnp.concatenate([x[:, :half_w], x[:, half_w:]], axis=0).astype(BF16)

    def halves_masked(w):
        return _block_diag(w.astype(BF16), lane_half)

    def same_head(out):
        return jnp.where(second_in_half, out[n:], out[:n])

    lane_h = lax.broadcasted_iota(jnp.int32, (1, half_w), 1)
    in_half = [(lane_h >> (n.bit_length() - 1)) == j for j in range(half_w // n)]

    def two_rhs(lhs, x, y, dims=_NN):
        lb, xb, yb = lhs.astype(BF16), x.astype(BF16), y.astype(BF16)
        out_x, out_y = [], []
        for j in range(gw // half_w):
            cols = slice(j * half_w, (j + 1) * half_w)
            w = jnp.concatenate([_block_diag(xb[:, cols], in_half), _block_diag(yb[:, cols], in_half)],
                                axis=1 if dims == _NN else 0)
            o = _dot(lb[:, cols], w, dims)
            out_x.append(o[:, :half_w])
            out_y.append(o[:, half_w:])
        return jnp.concatenate(out_x, axis=1), jnp.concatenate(out_y, axis=1)

    sec_rows = rows // RWKV_SECTIONS
    sec_chunks = sec_rows // chunk
    rr = lax.broadcasted_iota(jnp.int32, (sec_rows, sec_rows), 0)
    cc = lax.broadcasted_iota(jnp.int32, (sec_rows, sec_rows), 1)
    cs = chunk.bit_length() - 1
    ltri = jnp.where((rr >= cc) & ((rr >> cs) == (cc >> cs)), 1.0, 0.0).astype(BF16)
    eye = jnp.where(t_idx == i_idx, 1.0, 0.0)
    chains = [(ci, gi) for ci in range(sec_chunks) for gi in range(n_groups)]
    win = lambda x, ci, gi: x[ci * chunk:(ci + 1) * chunk, gi * gw:(gi + 1) * gw]

    def section(row0):
        rs = slice(row0, row0 + sec_rows)
        r, k, v, lw = r_ref[0, rs, :], k_ref[0, rs, :], v_ref[0, rs, :], lw_ref[0, rs, :]
        kk, b = kk_ref[0, rs, :], b_ref[0, rs, :]
        cum = _dot_ones_lhs(ltri, lw)
        a_t = -kk * jnp.exp(cum - lw)
        r_t = r * jnp.exp(cum)
        e_neg = jnp.exp(-cum)
        b_t = b * e_neg
        k_t = k * e_neg
        tots = [cum[(ci + 1) * chunk - 1:(ci + 1) * chunk, :] for ci in range(sec_chunks)]
        e_bars = [jnp.exp(tots[ci] - cum[ci * chunk:(ci + 1) * chunk, :]) for ci in range(sec_chunks)]
        b_bar = [win(b, ci, gi) * e_bars[ci][:, gi * gw:(gi + 1) * gw] for ci, gi in chains]
        k_bar = [win(k, ci, gi) * e_bars[ci][:, gi * gw:(gi + 1) * gw] for ci, gi in chains]
        at = [win(a_t, ci, gi) for ci, gi in chains]
        rt = [win(r_t, ci, gi) for ci, gi in chains]
        vv = [win(v, ci, gi) for ci, gi in chains]
        yield

        ar = [jnp.concatenate([a_, r_], axis=0) for a_, r_ in zip(at, rt)]
        grams = [two_rhs(x, win(b_t, ci, gi), win(k_t, ci, gi), _NT) for x, (ci, gi) in zip(ar, chains)]
        gram_b = [x[0] for x in grams]
        gram_k = [x[1] for x in grams]
        yield
        m_ab = [jnp.where(strict, g_[:chunk], 0.0) for g_ in gram_b]
        p_rb = [jnp.where(incl, g_[chunk:], 0.0) for g_ in gram_b]
        m_ak = [jnp.where(strict, g_[:chunk], 0.0) for g_ in gram_k]
        p_rk = [jnp.where(incl, g_[chunk:], 0.0) for g_ in gram_k]
        yield

        ts = [eye + jnp.where(same(1), m, 0.0) for m in m_ab]
        s = 1
        while (1 << s) < chunk:
            level = same(s + 1) & jnp.logical_not(same(s))
            offs = [jnp.where(level, m, 0.0) for m in m_ab]
            mids = [_packed_dot1(t, o, hm) for t, o in zip(ts, offs)]
            yield
            ts = [t + _packed_dot1(md, t, hm) for t, md in zip(ts, mids)]
            yield
            s += 1

        both = [_packed_dot1(jnp.concatenate([m, pk], axis=0), x, hm) for m, pk, x in zip(m_ak, p_rk, vv)]
        mv = [x[:chunk] for x in both]
        pkv = [x[chunk:] for x in both]
        yield
        hat = [two_rhs(t, x, y) for t, x, y in zip(ts, at, mv)]
        a_hat = [x[0] for x in hat]
        u0 = [x[1] for x in hat]
        yield
        pb = [two_rhs(p, a, u) for p, a, u in zip(p_rb, a_hat, u0)]
        r_hat = [x + y[0] for x, y in zip(rt, pb)]
        y0 = [x + y[1] for x, y in zip(pkv, pb)]
        yield
        g_mat = [same_head(_dot(halves_rows(a), halves_masked(bb), _TN))
                 for a, bb in zip(a_hat, b_bar)]
        h_mat = [same_head(_dot(jnp.concatenate([halves_rows(u), halves_rows(x)], axis=0),
                                jnp.concatenate([halves_masked(bb), halves_masked(kb)], axis=0), _TN))
                 for u, x, bb, kb in zip(u0, vv, b_bar, k_bar)]
        yield

        y_rows = []
        for ci in range(sec_chunks):
            y_groups = []
            for gi in range(n_groups):
                j = ci * n_groups + gi
                s0 = state_ref[gi]
                y_groups.append(_packed_dot1(r_hat[j], s0, hm, _NT) + y0[j])
                w_tot = jnp.exp(tots[ci][:, gi * gw:(gi + 1) * gw])
                state_ref[gi] = s0 * w_tot + _packed_dot_wide_lhs(s0, g_mat[j], hm) + h_mat[j]
            y_rows.append(jnp.concatenate(y_groups, axis=1))
            yield
        y = jnp.concatenate(y_rows, axis=0)

        hsum = hsum_ref[...]
        inv_n = 1.0 / n
        mu = _head_sums(y, hsum) * inv_n
        yc = y - mu
        var = _head_sums(yc * yc, hsum) * inv_n
        yn = yc * lax.rsqrt(var + GN_EPS) * gng_ref[...] + gnb_ref[...]
        bonus = _head_sums(r * k * rk_ref[...], hsum)
        o_ref[0, rs, :] = ((yn + bonus * v) * g_ref[0, rs, :]).astype(o_ref.dtype)

    assert RWKV_SECTION_LAG >= sec_chunks
    _run_skewed([section(si * sec_rows) for si in range(RWKV_SECTIONS)], lag=RWKV_SECTION_LAG)


def _rwkv(r, k, v, lw, kk, b, g, rw_rk, gn_g, gn_b, hsum, *, chunk, rows):
    bsz, seq, d = r.shape
    n_groups = d // (RWKV_GROUP_HEADS * HEAD_DIM)
    blk = pl.BlockSpec((1, rows, d), lambda bi, ci: (bi, ci, 0))
    const = lambda bi, ci: (0, 0)
    vec = lambda a: a.reshape(1, d)
    return pl.pallas_call(
        functools.partial(_rwkv_kernel, chunk=chunk),
        grid=(bsz, seq // rows),
        in_specs=[blk] * 7 + [pl.BlockSpec((1, d), const)] * 3 + [pl.BlockSpec(hsum.shape, const)],
        out_specs=blk,
        out_shape=jax.ShapeDtypeStruct((bsz, seq, d), BF16),
        scratch_shapes=[pltpu.VMEM((n_groups, HEAD_DIM, RWKV_GROUP_HEADS * HEAD_DIM), F32)],
        compiler_params=pltpu.CompilerParams(dimension_semantics=("arbitrary", "arbitrary"),
                                             vmem_limit_bytes=VMEM_LIMIT),
        name="rwkv",
    )(r, k, v, lw, kk, b, g, vec(rw_rk), vec(gn_g), vec(gn_b), hsum)


SB_LOG_UNDERFLOW = 105.0


def _sb_kernel(q_ref, k_ref, v_ref, o_ref, acc_ref, run_ref, *, blk):
    qi = pl.program_id(1)
    pairs = k_ref.shape[2] // (2 * HEAD_DIM)
    row = lax.broadcasted_iota(jnp.int32, (blk, blk), 0)
    col = lax.broadcasted_iota(jnp.int32, (blk, blk), 1)
    causal = col < row
    jj = lax.broadcasted_iota(jnp.int32, (blk, 2 * blk), 0)
    ss = lax.broadcasted_iota(jnp.int32, (blk, 2 * blk), 1)
    suffix = jnp.where((jj >= ss) | (ss >= blk), 1.0, 0.0).astype(BF16)
    first_head = lax.broadcasted_iota(jnp.int32, (1, 2 * HEAD_DIM), 1) < HEAD_DIM
    zero = jnp.zeros((), BF16)

    lanes = [slice(p * 2 * HEAD_DIM, (p + 1) * 2 * HEAD_DIM) for p in range(pairs)]
    heads = [(p, hh) for p in range(pairs) for hh in range(2)]

    nh = len(heads)
    pw = 2 * HEAD_DIM

    def tiles(kbs, diagonal, absent=None):
        starts = [pl.multiple_of(kb * blk, blk) for kb in kbs]
        zs = [None] * (len(kbs) * nh)
        for p in range(pairs):
            q2 = jnp.concatenate([q_ref[0, :, (2 * p + hh) * pw:(2 * p + hh + 1) * pw] for hh in range(2)],
                                 axis=0)
            keys = jnp.concatenate([k_ref[0, pl.ds(start, blk), lanes[p]] for start in starts], axis=0)
            z2 = _dot(q2, keys, _NT)
            for t_ in range(len(kbs)):
                for hh in range(2):
                    zs[t_ * nh + 2 * p + hh] = z2[hh * blk:(hh + 1) * blk, t_ * blk:(t_ + 1) * blk]
        sps = [_softplus(z) for z in zs]
        if diagonal:
            sps = [jnp.where(causal, sp, 0.0) for sp in sps[:nh]] + sps[nh:]
        css = [None] * len(zs)
        for p in range(pairs):
            idx = [t_ * nh + 2 * p + hh for t_ in range(len(kbs)) for hh in range(2)]
            c_ = _dot(jnp.concatenate([sps[i_] for i_ in idx], axis=0).astype(BF16), suffix)
            for j, i_ in enumerate(idx):
                css[i_] = c_[j * blk:(j + 1) * blk]
        atts = [[None] * nh for _ in kbs]
        low = None
        for n_ in range(nh):
            run = None if diagonal else run_ref[n_]
            for t_ in range(len(kbs)):
                z, cs = zs[t_ * nh + n_], css[t_ * nh + n_]
                logw = z - cs[:, :blk]
                if run is None:
                    att = jnp.where(causal, jnp.exp(logw), 0.0)
                    run = cs[:, blk:]
                else:
                    att = jnp.exp(logw - run)
                    run = run + cs[:, blk:]
                if t_ == 0 and absent is not None:
                    run = run + absent
                atts[t_][n_] = att.astype(BF16)
            run_ref[n_] = run
            low = run if low is None else jnp.minimum(low, run)
        done = (jnp.min(low) > SB_LOG_UNDERFLOW).astype(jnp.int32)
        for p in range(pairs):
            vvs = []
            for start in starts:
                vp = v_ref[0, pl.ds(start, blk), lanes[p]]
                vvs += [jnp.where(first_head, vp, zero), jnp.where(first_head, zero, vp)]
            lhs = jnp.concatenate([atts[t_][2 * p + hh] for t_ in range(len(kbs)) for hh in range(2)], axis=1)
            out = _dot(lhs, jnp.concatenate(vvs, axis=0))
            if diagonal:
                acc_ref[:, lanes[p]] = out
            else:
                acc_ref[:, lanes[p]] += out
        return done

    def pair_of_tiles(i, diagonal):
        second = qi - i - 1
        absent = jnp.where(second < 0, 1e4, 0.0).astype(F32)
        return tiles([qi - i, jnp.maximum(second, 0)], diagonal, absent)

    done0 = pair_of_tiles(0, True)

    def cond(c):
        i, done = c
        return jnp.logical_and(i <= qi, done == 0)

    def body(c):
        i, _ = c
        return i + 2, pair_of_tiles(i, False)

    lax.while_loop(cond, body, (jnp.int32(2), done0))
    o_ref[0] = acc_ref[...].astype(o_ref.dtype)


def _sb_attention(q, k, v, *, blk):
    bsz, seq, d = k.shape
    heads = d // HEAD_DIM
    qspec = pl.BlockSpec((1, blk, 2 * d), lambda b, i: (b, i, 0))
    ospec = pl.BlockSpec((1, blk, d), lambda b, i: (b, i, 0))
    kvspec = pl.BlockSpec((1, seq, d), lambda b, i: (b, 0, 0), pipeline_mode=pl.Buffered(1))
    return pl.pallas_call(
        functools.partial(_sb_kernel, blk=blk),
        grid=(bsz, seq // blk),
        in_specs=[qspec, kvspec, kvspec],
        out_specs=ospec,
        out_shape=jax.ShapeDtypeStruct((bsz, seq, d), BF16),
        scratch_shapes=[pltpu.VMEM((blk, d), F32), pltpu.VMEM((heads, blk, blk), F32)],
        compiler_params=pltpu.CompilerParams(dimension_semantics=("arbitrary", "arbitrary"),
                                             vmem_limit_bytes=VMEM_LIMIT),
        name="sbattn",
    )(q, k, v)


def _merge_kernel(x_ref, ada_ref, ya_ref, ob_ref, ga_ref, gb_ref, wa_ref, wb_ref, wo_ref,
                  lng_ref, lnb_ref, o_ref, *, alpha):
    g1 = ada_ref[0, 2:3, :]
    half = x_ref.shape[0] // 2

    def rows_program(rows):
        y_a = _dot(ya_ref[rows, :], wa_ref[...])
        y_b = _dot(ob_ref[rows, :], wb_ref[...])
        yield
        merged = ga_ref[rows, :].astype(F32) * y_a + gb_ref[rows, :].astype(F32) * y_b
        mix = _dot(merged.astype(BF16), wo_ref[...])
        yield
        o_ref[rows, :] = _norm_rows(alpha * x_ref[rows, :] + g1 * mix, LN_EPS) * lng_ref[...] + lnb_ref[...]

    _run_skewed([rows_program(slice(0, half)), rows_program(slice(half, 2 * half))], lag=1)


def _merge(x2, ada3, ya, ob, ga, gb, wa_bf, wb_bf, wo_bf, ln_g, ln_b, *, seq, tm, alpha):
    m, d = x2.shape
    tiles_per_batch = seq // tm
    row = lambda i: (i, 0)
    const = lambda i: (0, 0)
    rows = lambda a: pl.BlockSpec((tm, a.shape[1]), row)
    full = lambda a: pl.BlockSpec(a.shape, const, pipeline_mode=pl.Buffered(1))
    vec = lambda a: a.reshape(1, -1)
    ins = [x2, ada3, ya, ob, ga, gb, wa_bf, wb_bf, wo_bf, vec(ln_g), vec(ln_b)]
    in_specs = [rows(x2), pl.BlockSpec((1,) + ada3.shape[1:], lambda i: (i // tiles_per_batch, 0, 0)),
                rows(ya), rows(ob), rows(ga), rows(gb)] + [full(a) for a in ins[6:]]
    return pl.pallas_call(
        functools.partial(_merge_kernel, alpha=alpha),
        grid=(m // tm,),
        in_specs=in_specs,
        out_specs=pl.BlockSpec((tm, d), row),
        out_shape=jax.ShapeDtypeStruct((m, d), F32),
        compiler_params=pltpu.CompilerParams(dimension_semantics=("arbitrary",),
                                             vmem_limit_bytes=VMEM_LIMIT),
        name="merge",
    )(*ins)


def _ffn_kernel(x_ref, ada_ref, w1_ref, b1_ref, w2_ref, b2_ref, lng_ref, lnb_ref, o_ref, *, alpha, fchunk):
    sh = ada_ref[0, 3:4, :]
    sc = ada_ref[0, 4:5, :]
    g2 = ada_ref[0, 5:6, :]
    d_ff = w1_ref.shape[1]
    n_chunks = d_ff // fchunk
    half = x_ref.shape[0] // 2

    def rows_program(rows):
        x = x_ref[rows, :]
        hb = (_norm_rows(x, ADALN_EPS) * (1.0 + sc) + sh).astype(BF16)
        yield
        ff = None
        for j in range(n_chunks):
            sl = slice(j * fchunk, (j + 1) * fchunk)
            t = jnp.maximum(_dot(hb, w1_ref[:, sl]) + b1_ref[:, sl], 0.0)
            part = _dot((t * t).astype(BF16), w2_ref[sl, :])
            ff = part if ff is None else ff + part
            yield
        ff = ff + b2_ref[...]
        o_ref[rows, :] = _norm_rows(alpha * x + g2 * ff, LN_EPS) * lng_ref[...] + lnb_ref[...]

    _run_skewed([rows_program(slice(0, half)), rows_program(slice(half, 2 * half))], lag=2)


def _ffn(x1, ada3, w1_bf, b1, w2_bf, b2, ln_g, ln_b, *, seq, tm, alpha, fchunk):
    m, d = x1.shape
    tiles_per_batch = seq // tm
    row = lambda i: (i, 0)
    const = lambda i: (0, 0)
    full = lambda a: pl.BlockSpec(a.shape, const, pipeline_mode=pl.Buffered(1))
    vec = lambda a: a.reshape(1, -1)
    ins = [x1, ada3, w1_bf, vec(b1), w2_bf, vec(b2), vec(ln_g), vec(ln_b)]
    in_specs = [pl.BlockSpec((tm, d), row),
                pl.BlockSpec((1,) + ada3.shape[1:], lambda i: (i // tiles_per_batch, 0, 0))]
    in_specs += [full(a) for a in ins[2:]]
    return pl.pallas_call(
        functools.partial(_ffn_kernel, alpha=alpha, fchunk=fchunk),
        grid=(m // tm,),
        in_specs=in_specs,
        out_specs=pl.BlockSpec((tm, d), row),
        out_shape=jax.ShapeDtypeStruct((m, d), F32),
        compiler_params=pltpu.CompilerParams(dimension_semantics=("arbitrary",),
                                             vmem_limit_bytes=VMEM_LIMIT),
        name="ffn",
    )(*ins)


def _branches(proj, rw_rk, rw_gn_g, rw_gn_b, hsum, bsz, seq):
    r, k, v, lw, kk, b, g, q, ks, vs = proj
    seq3 = lambda a: a.reshape(bsz, seq, -1)
    ya = _rwkv(seq3(r), seq3(k), seq3(v), seq3(lw), seq3(kk), seq3(b), seq3(g),
               rw_rk, rw_gn_g, rw_gn_b, hsum, chunk=min(RWKV_CHUNK, seq),
               rows=min(RWKV_CHUNKS_PER_STEP * RWKV_CHUNK, seq))
    o = _sb_attention(seq3(q), seq3(ks), seq3(vs), blk=min(SB_BLOCK, seq))
    return ya.reshape(bsz * seq, -1), o.reshape(bsz * seq, -1)


def _layer(x, c, w_ada, b_ada, w_in, mu_shift, rw_w0, rw_w2, rw_a0, rw_a2, rw_g2, rw_kk, rw_ka,
           rw_rk, rw_gn_g, rw_gn_b, w_branch_a, w_branch_b, w_out, ln1_g, ln1_b,
           w_ff1, b_ff1, w_ff2, b_ff2, ln2_g, ln2_b, *, alpha):
    bsz, seq, d = x.shape
    d_rwkv = rw_w0.shape[-1]
    tm = min(INPROJ_ROWS, seq)
    tm2 = min(MLP_ROWS, seq)

    ada3 = _ada(c, w_ada, b_ada).reshape(bsz, 6, d)
    lane = jnp.arange(RWKV_GROUP_HEADS * HEAD_DIM) // HEAD_DIM
    hsum = (lane[:, None] == lane[None, :]).astype(BF16)

    x2 = x.reshape(bsz * seq, d)
    (r, k, v, lw, kk, b, g, q, ks, vs, ga, gb) = _inproj(
        x2, ada3, w_in.astype(BF16), mu_shift, rw_w0, rw_w2, rw_a0, rw_a2, rw_g2, rw_kk, rw_ka, hsum,
        seq=seq, tm=tm)

    ya, ob = _branches((r, k, v, lw, kk, b, g, q, ks, vs), rw_rk, rw_gn_g, rw_gn_b, hsum, bsz, seq)

    x1 = _merge(x2, ada3, ya, ob, ga, gb,
                w_branch_a.astype(BF16), w_branch_b.astype(BF16), w_out.astype(BF16),
                ln1_g, ln1_b, seq=seq, tm=tm2, alpha=alpha)
    out = _ffn(x1, ada3, w_ff1.astype(BF16), b_ff1, w_ff2.astype(BF16), b_ff2, ln2_g, ln2_b,
               seq=seq, tm=tm2, alpha=alpha, fchunk=min(1024, w_ff1.shape[-1]))
    return out.reshape(bsz, seq, d)


def kernel(x, c, w_ada, b_ada, w_in, mu_shift, rw_w0, rw_w2, rw_a0, rw_a2, rw_g2, rw_kk, rw_ka, rw_rk,
           rw_gn_g, rw_gn_b, w_branch_a, w_branch_b, w_out, ln1_g, ln1_b, w_ff1, b_ff1, w_ff2, b_ff2,
           ln2_g, ln2_b):
    in_dtype = x.dtype
    layer_params = (w_ada, b_ada, w_in, mu_shift, rw_w0, rw_w2, rw_a0, rw_a2, rw_g2, rw_kk, rw_ka,
                    rw_rk, rw_gn_g, rw_gn_b, w_branch_a, w_branch_b, w_out, ln1_g, ln1_b,
                    w_ff1, b_ff1, w_ff2, b_ff2, ln2_g, ln2_b)
    depth = w_ada.shape[0]
    alpha = (2.0 * depth) ** 0.25
    for l in range(depth):
        x = _layer(x, c, *[p[l] for p in layer_params], alpha=alpha)
    return x.astype(in_dtype)
```

```python
import functools

import jax
import jax.numpy as jnp
from jax import lax
from jax.experimental import pallas as pl
from jax.experimental.pallas import tpu as pltpu

F32 = jnp.float32
BF16 = jnp.bfloat16

HEAD_DIM = 64
LN_EPS = 1e-5
ADALN_EPS = 1e-6
GN_EPS = 64e-5
RWKV_CHUNK = 64
RWKV_SECTIONS = 4
RWKV_SECTION_CHUNKS = 2
RWKV_SECTION_LAG = 6
RWKV_CHUNKS_PER_STEP = RWKV_SECTIONS * RWKV_SECTION_CHUNKS
SB_BLOCK = 128
INPROJ_ROWS = 512
MLP_ROWS = 1024
VMEM_LIMIT = 56 * 1024 * 1024


def _split2(a):
    hi = a.astype(BF16)
    lo = (a - hi.astype(F32)).astype(BF16)
    return hi, lo


_NN = (((1,), (0,)), ((), ()))
_NT = (((1,), (1,)), ((), ()))
_TN = (((0,), (0,)), ((), ()))


def _dot(a, b, dims=_NN):
    return lax.dot_general(a, b, dims, preferred_element_type=F32)


def _dot3(a, b, dims=_NN):
    ah, al = _split2(a)
    bh, bl = _split2(b)
    return _dot(ah, bh, dims) + (_dot(al, bh, dims) + _dot(ah, bl, dims))


def _head_sums(x, ones_blk):
    m, w = x.shape[0], ones_blk.shape[0]
    hi, lo = _split2(x)
    outs = []
    for g in range(x.shape[1] // w):
        cols = slice(g * w, (g + 1) * w)
        both = _dot(jnp.concatenate([hi[:, cols], lo[:, cols]], axis=0), ones_blk)
        outs.append(both[:m] + both[m:])
    return jnp.concatenate(outs, axis=1)


def _dot_ones_lhs(a_bf, b):
    hi, lo = _split2(b)
    return _dot(a_bf, hi) + _dot(a_bf, lo)


_DONE = object()


def _run_skewed(programs, lag):
    live = [True] * len(programs)
    tick = 0
    while any(live):
        for i, prog in enumerate(programs):
            if live[i] and tick >= i * lag:
                live[i] = next(prog, _DONE) is not _DONE
        tick += 1


_NEG_LOG2_E = -1.4426950408889634


def _softplus(y):
    return jnp.maximum(y, 0.0) + jnp.log(1.0 + jnp.exp2(jnp.abs(y) * _NEG_LOG2_E))


def _sigmoid(y):
    return 1.0 / (1.0 + jnp.exp(-y))


def _norm_rows(x, eps):
    mu = jnp.mean(x, axis=-1, keepdims=True)
    xc = x - mu
    var = jnp.mean(xc * xc, axis=-1, keepdims=True)
    return xc * lax.rsqrt(var + eps)


def _ada_kernel(c_ref, w_ref, b_ref, o_ref):
    c = c_ref[...]
    s = c * _sigmoid(c)
    o_ref[...] = jnp.dot(s, w_ref[...], preferred_element_type=F32,
                         precision=lax.Precision.HIGHEST) + b_ref[...]


def _ada(c, w_ada, b_ada):
    bsz, d = c.shape
    n = w_ada.shape[1]
    return pl.pallas_call(
        _ada_kernel,
        grid=(n // d,),
        in_specs=[pl.BlockSpec((bsz, d), lambda j: (0, 0)),
                  pl.BlockSpec((d, d), lambda j: (0, j)),
                  pl.BlockSpec((1, d), lambda j: (0, j))],
        out_specs=pl.BlockSpec((bsz, d), lambda j: (0, j)),
        out_shape=jax.ShapeDtypeStruct((bsz, n), F32),
        name="ada",
    )(c, w_ada, b_ada.reshape(1, n))


def _inproj_kernel(x_ref, ada_ref, win_ref, mu_ref, w0_ref, w2_ref, a0_ref, a2_ref, g2_ref,
                   kkw_ref, kaw_ref, hsum_ref,
                   r_ref, k_ref, v_ref, lw_ref, kk_ref, b_ref, g_ref,
                   q_ref, ks_ref, vs_ref, ga_ref, gb_ref,
                   carry_ref, *, tiles_per_batch, d_rwkv, d_decay, d_aaa, d_gate, d_sb, d_model):
    i = pl.program_id(0)
    tm = x_ref.shape[0]
    d_shift = 3 * d_rwkv + d_decay + d_aaa + d_gate

    sh = ada_ref[0, 0:1, :]
    sc = ada_ref[0, 1:2, :]

    @pl.when(i % tiles_per_batch == 0)
    def _():
        carry_ref[...] = jnp.zeros_like(carry_ref)

    half = d_model // 2
    pair_w = 2 * HEAD_DIM
    first_head = lax.broadcasted_iota(jnp.int32, (1, pair_w), 1) < HEAD_DIM

    def rows_program(rows):
        hm = rows.stop - rows.start
        hb = (_norm_rows(x_ref[rows, :], ADALN_EPS) * (1.0 + sc) + sh).astype(BF16)
        row0 = lax.broadcasted_iota(jnp.int32, (hm, 1), 0) == 0
        yield

        def shifted(lo, width):
            cols = slice(lo, lo + width)
            z = _dot(hb, win_ref[:, cols])
            prev = jnp.where(row0, carry_ref[:, cols], pltpu.roll(z, 1, 0))
            carry_ref[:, cols] = z[hm - 1:hm, :]
            return z + mu_ref[:, cols] * (prev - z)

        def store_queries(z):
            zq = (z * (HEAD_DIM ** -0.5)).astype(BF16)
            zero = jnp.zeros((), BF16)
            for p in range(d_sb // pair_w):
                qp = zq[:, p * pair_w:(p + 1) * pair_w]
                q_ref[rows, (2 * p) * pair_w:(2 * p + 1) * pair_w] = jnp.where(first_head, qp, zero)
                q_ref[rows, (2 * p + 1) * pair_w:(2 * p + 2) * pair_w] = jnp.where(first_head, zero, qp)

        def plain(ref, cols, fn=lambda z: z):
            def store(z):
                ref[rows, cols] = fn(z).astype(BF16)
            return store

        all_cols = slice(0, d_sb)
        rest = [(store_queries, d_sb), (plain(ks_ref, all_cols), d_sb), (plain(vs_ref, all_cols), d_sb),
                (plain(ga_ref, slice(0, half), _sigmoid), half),
                (plain(ga_ref, slice(half, d_model), _sigmoid), half),
                (plain(gb_ref, slice(0, half), _sigmoid), half),
                (plain(gb_ref, slice(half, d_model), _sigmoid), half)]
        rest_col = [d_shift]

        def project_next():
            store, width = rest.pop(0)
            store(_dot(hb, win_ref[:, rest_col[0]:rest_col[0] + width]))
            rest_col[0] += width

        zl = shifted(3 * d_rwkv, d_decay + d_aaa + d_gate)
        zw = zl[:, :d_decay]
        za = zl[:, d_decay:d_decay + d_aaa]
        zg = zl[:, d_decay + d_aaa:]
        k = shifted(d_rwkv, d_rwkv)
        yield
        ww = w0_ref[...] + _dot(jnp.tanh(zw).astype(BF16), w2_ref[...])
        w_log = -_softplus(-ww) - 0.5
        lw_ref[rows, :] = -jnp.exp(w_log)
        project_next()
        yield
        arate = _sigmoid(a0_ref[...] + _dot(za.astype(BF16), a2_ref[...]))
        g_ref[rows, :] = _dot(_sigmoid(zg).astype(BF16), g2_ref[...])
        r_ref[rows, :] = shifted(0, d_rwkv)
        yield
        kkraw = k * kkw_ref[...]
        ssq = _head_sums(kkraw * kkraw, hsum_ref[...])
        kk = kkraw / jnp.maximum(jnp.sqrt(ssq), 1e-12)
        v_ref[rows, :] = shifted(2 * d_rwkv, d_rwkv)
        k_ref[rows, :] = k * (1.0 + (arate - 1.0) * kaw_ref[...])
        yield
        project_next()
        kk_ref[rows, :] = kk
        b_ref[rows, :] = kk * arate
        yield
        while rest:
            project_next()
            yield

    hm = tm // 2
    _run_skewed([rows_program(slice(0, hm)), rows_program(slice(hm, tm))], lag=2)


def _inproj(x2, ada3, win_bf, mu_shift, rw_w0, rw_w2, rw_a0, rw_a2, rw_g2, rw_kk, rw_ka, hsum,
            *, seq, tm):
    m, d_model = x2.shape
    d_rwkv = rw_w0.shape[-1]
    d_decay, d_aaa, d_gate = rw_w2.shape[0], rw_a2.shape[0], rw_g2.shape[0]
    d_shift = 3 * d_rwkv + d_decay + d_aaa + d_gate
    d_sb = (win_bf.shape[1] - d_shift - 2 * d_model) // 3
    tiles_per_batch = seq // tm
    row = lambda i: (i, 0)
    const = lambda i: (0, 0)
    full = lambda a: pl.BlockSpec(a.shape, const, pipeline_mode=pl.Buffered(1))
    vec = lambda a: a.reshape(1, -1)
    ins = [x2, ada3, win_bf, vec(mu_shift), vec(rw_w0), rw_w2.astype(BF16), vec(rw_a0),
           rw_a2.astype(BF16), rw_g2.astype(BF16), vec(rw_kk), vec(rw_ka), hsum]
    in_specs = [pl.BlockSpec((tm, d_model), row),
                pl.BlockSpec((1,) + ada3.shape[1:], lambda i: (i // tiles_per_batch, 0, 0))]
    in_specs += [full(a) for a in ins[2:]]
    widths = [d_rwkv] * 7 + [2 * d_sb, d_sb, d_sb] + [d_model] * 2
    dtypes = [F32] * 7 + [BF16] * 5
    out_shape = [jax.ShapeDtypeStruct((m, w), dt) for w, dt in zip(widths, dtypes)]
    out_specs = [pl.BlockSpec((tm, w), row) for w in widths]
    kern = functools.partial(_inproj_kernel, tiles_per_batch=tiles_per_batch, d_rwkv=d_rwkv,
                             d_decay=d_decay, d_aaa=d_aaa, d_gate=d_gate, d_sb=d_sb, d_model=d_model)
    return pl.pallas_call(
        kern,
        grid=(m // tm,),
        in_specs=in_specs,
        out_specs=out_specs,
        out_shape=out_shape,
        scratch_shapes=[pltpu.VMEM((1, d_shift), F32)],
        compiler_params=pltpu.CompilerParams(dimension_semantics=("arbitrary",),
                                             vmem_limit_bytes=VMEM_LIMIT),
        name="inproj",
    )(*ins)


RWKV_GROUP_HEADS = 4


def _block_diag(q, head_masks):
    zero = jnp.zeros((), q.dtype)
    return jnp.concatenate([jnp.where(m, q, zero) for m in head_masks], axis=0)


def _packed_dot1(lhs, q, head_masks, dims=_NN):
    return _dot(lhs.astype(BF16), _block_diag(q.astype(BF16), head_masks), dims)


def _packed_dot_wide_lhs(lhs, q, head_masks, dims=_NN):
    m = lhs.shape[0]
    lh, ll = _split2(lhs)
    both = _dot(jnp.concatenate([lh, ll], axis=0), _block_diag(q.astype(BF16), head_masks), dims)
    return both[:m] + both[m:]


def _rwkv_kernel(r_ref, k_ref, v_ref, lw_ref, kk_ref, b_ref, g_ref, rk_ref, gng_ref, gnb_ref, hsum_ref,
                 o_ref, state_ref, *, chunk):
    c = pl.program_id(1)
    n = HEAD_DIM
    rows, d = r_ref.shape[1], r_ref.shape[2]
    gw = RWKV_GROUP_HEADS * n
    n_groups = d // gw

    @pl.when(c == 0)
    def _():
        state_ref[...] = jnp.zeros_like(state_ref)

    t_idx = lax.broadcasted_iota(jnp.int32, (chunk, gw), 0)
    lane = lax.broadcasted_iota(jnp.int32, (chunk, gw), 1)
    i_idx = lane & (n - 1)
    strict = t_idx > i_idx
    incl = t_idx >= i_idx
    lane1 = lax.broadcasted_iota(jnp.int32, (1, gw), 1)
    hm = [(lane1 >> (n.bit_length() - 1)) == h for h in range(RWKV_GROUP_HEADS)]
    same = lambda s: (t_idx >> s) == (i_idx >> s)

    half_w = gw // 2
    lane_half = [(lane1 >> (half_w.bit_length() - 1)) == j for j in range(2)]
    second_in_half = ((lane >> (n.bit_length() - 1)) & 1) == 1

    def halves_rows(x):
        return jnp.concatenate([x[:, :half_w], x[:, half_w:]], axis=0).astype(BF16)

    def halves_masked(w):
        return _block_diag(w.astype(BF16), lane_half)

    def same_head(out):
        return jnp.where(second_in_half, out[n:], out[:n])

    lane_h = lax.broadcasted_iota(jnp.int32, (1, half_w), 1)
    in_half = [(lane_h >> (n.bit_length() - 1)) == j for j in range(half_w // n)]

    def two_rhs(lhs, x, y, dims=_NN):
        lb, xb, yb = lhs.astype(BF16), x.astype(BF16), y.astype(BF16)
        out_x, out_y = [], []
        for j in range(gw // half_w):
            cols = slice(j * half_w, (j + 1) * half_w)
            w = jnp.concatenate([_block_diag(xb[:, cols], in_half), _block_diag(yb[:, cols], in_half)],
                                axis=1 if dims == _NN else 0)
            o = _dot(lb[:, cols], w, dims)
            out_x.append(o[:, :half_w])
            out_y.append(o[:, half_w:])
        return jnp.concatenate(out_x, axis=1), jnp.concatenate(out_y, axis=1)

    sec_rows = rows // RWKV_SECTIONS
    sec_chunks = sec_rows // chunk
    rr = lax.broadcasted_iota(jnp.int32, (sec_rows, sec_rows), 0)
    cc = lax.broadcasted_iota(jnp.int32, (sec_rows, sec_rows), 1)
    cs = chunk.bit_length() - 1
    ltri = jnp.where((rr >= cc) & ((rr >> cs) == (cc >> cs)), 1.0, 0.0).astype(BF16)
    eye = jnp.where(t_idx == i_idx, 1.0, 0.0)
    chains = [(ci, gi) for ci in range(sec_chunks) for gi in range(n_groups)]
    win = lambda x, ci, gi: x[ci * chunk:(ci + 1) * chunk, gi * gw:(gi + 1) * gw]

    def section(row0):
        rs = slice(row0, row0 + sec_rows)
        r, k, v, lw = r_ref[0, rs, :], k_ref[0, rs, :], v_ref[0, rs, :], lw_ref[0, rs, :]
        kk, b = kk_ref[0, rs, :], b_ref[0, rs, :]
        cum = _dot_ones_lhs(ltri, lw)
        a_t = -kk * jnp.exp(cum - lw)
        r_t = r * jnp.exp(cum)
        e_neg = jnp.exp(-cum)
        b_t = b * e_neg
        k_t = k * e_neg
        tots = [cum[(ci + 1) * chunk - 1:(ci + 1) * chunk, :] for ci in range(sec_chunks)]
        e_bars = [jnp.exp(tots[ci] - cum[ci * chunk:(ci + 1) * chunk, :]) for ci in range(sec_chunks)]
        b_bar = [win(b, ci, gi) * e_bars[ci][:, gi * gw:(gi + 1) * gw] for ci, gi in chains]
        k_bar = [win(k, ci, gi) * e_bars[ci][:, gi * gw:(gi + 1) * gw] for ci, gi in chains]
        at = [win(a_t, ci, gi) for ci, gi in chains]
        rt = [win(r_t, ci, gi) for ci, gi in chains]
        vv = [win(v, ci, gi) for ci, gi in chains]
        yield

        ar = [jnp.concatenate([a_, r_], axis=0) for a_, r_ in zip(at, rt)]
        grams = [two_rhs(x, win(b_t, ci, gi), win(k_t, ci, gi), _NT) for x, (ci, gi) in zip(ar, chains)]
        gram_b = [x[0] for x in grams]
        gram_k = [x[1] for x in grams]
        yield
        m_ab = [jnp.where(strict, g_[:chunk], 0.0) for g_ in gram_b]
        p_rb = [jnp.where(incl, g_[chunk:], 0.0) for g_ in gram_b]
        m_ak = [jnp.where(strict, g_[:chunk], 0.0) for g_ in gram_k]
        p_rk = [jnp.where(incl, g_[chunk:], 0.0) for g_ in gram_k]
        yield

        ts = [eye + jnp.where(same(1), m, 0.0) for m in m_ab]
        s = 1
        while (1 << s) < chunk:
            level = same(s + 1) & jnp.logical_not(same(s))
            offs = [jnp.where(level, m, 0.0) for m in m_ab]
            mids = [_packed_dot1(t, o, hm) for t, o in zip(ts, offs)]
            yield
            ts = [t + _packed_dot1(md, t, hm) for t, md in zip(ts, mids)]
            yield
            s += 1

        both = [_packed_dot1(jnp.concatenate([m, pk], axis=0), x, hm) for m, pk, x in zip(m_ak, p_rk, vv)]
        mv = [x[:chunk] for x in both]
        pkv = [x[chunk:] for x in both]
        yield
        hat = [two_rhs(t, x, y) for t, x, y in zip(ts, at, mv)]
        a_hat = [x[0] for x in hat]
        u0 = [x[1] for x in hat]
        yield
        pb = [two_rhs(p, a, u) for p, a, u in zip(p_rb, a_hat, u0)]
        r_hat = [x + y[0] for x, y in zip(rt, pb)]
        y0 = [x + y[1] for x, y in zip(pkv, pb)]
        yield
        g_mat = [same_head(_dot(halves_rows(a), halves_masked(bb), _TN))
                 for a, bb in zip(a_hat, b_bar)]
        h_mat = [same_head(_dot(jnp.concatenate([halves_rows(u), halves_rows(x)], axis=0),
                                jnp.concatenate([halves_masked(bb), halves_masked(kb)], axis=0), _TN))
                 for u, x, bb, kb in zip(u0, vv, b_bar, k_bar)]
        yield

        y_rows = []
        for ci in range(sec_chunks):
            y_groups = []
            for gi in range(n_groups):
                j = ci * n_groups + gi
                s0 = state_ref[gi]
                y_groups.append(_packed_dot1(r_hat[j], s0, hm, _NT) + y0[j])
                w_tot = jnp.exp(tots[ci][:, gi * gw:(gi + 1) * gw])
                state_ref[gi] = s0 * w_tot + _packed_dot_wide_lhs(s0, g_mat[j], hm) + h_mat[j]
            y_rows.append(jnp.concatenate(y_groups, axis=1))
            yield
        y = jnp.concatenate(y_rows, axis=0)

        hsum = hsum_ref[...]
        inv_n = 1.0 / n
        mu = _head_sums(y, hsum) * inv_n
        yc = y - mu
        var = _head_sums(yc * yc, hsum) * inv_n
        yn = yc * lax.rsqrt(var + GN_EPS) * gng_ref[...] + gnb_ref[...]
        bonus = _head_sums(r * k * rk_ref[...], hsum)
        o_ref[0, rs, :] = ((yn + bonus * v) * g_ref[0, rs, :]).astype(o_ref.dtype)

    assert RWKV_SECTION_LAG >= sec_chunks
    _run_skewed([section(si * sec_rows) for si in range(RWKV_SECTIONS)], lag=RWKV_SECTION_LAG)


def _rwkv(r, k, v, lw, kk, b, g, rw_rk, gn_g, gn_b, hsum, *, chunk, rows):
    bsz, seq, d = r.shape
    n_groups = d // (RWKV_GROUP_HEADS * HEAD_DIM)
    blk = pl.BlockSpec((1, rows, d), lambda bi, ci: (bi, ci, 0))
    const = lambda bi, ci: (0, 0)
    vec = lambda a: a.reshape(1, d)
    return pl.pallas_call(
        functools.partial(_rwkv_kernel, chunk=chunk),
        grid=(bsz, seq // rows),
        in_specs=[blk] * 7 + [pl.BlockSpec((1, d), const)] * 3 + [pl.BlockSpec(hsum.shape, const)],
        out_specs=blk,
        out_shape=jax.ShapeDtypeStruct((bsz, seq, d), BF16),
        scratch_shapes=[pltpu.VMEM((n_groups, HEAD_DIM, RWKV_GROUP_HEADS * HEAD_DIM), F32)],
        compiler_params=pltpu.CompilerParams(dimension_semantics=("arbitrary", "arbitrary"),
                                             vmem_limit_bytes=VMEM_LIMIT),
        name="rwkv",
    )(r, k, v, lw, kk, b, g, vec(rw_rk), vec(gn_g), vec(gn_b), hsum)


SB_LOG_UNDERFLOW = 105.0


def _sb_kernel(q_ref, k_ref, v_ref, o_ref, acc_ref, run_ref, *, blk):
    qi = pl.program_id(1)
    pairs = k_ref.shape[2] // (2 * HEAD_DIM)
    row = lax.broadcasted_iota(jnp.int32, (blk, blk), 0)
    col = lax.broadcasted_iota(jnp.int32, (blk, blk), 1)
    causal = col < row
    jj = lax.broadcasted_iota(jnp.int32, (blk, 2 * blk), 0)
    ss = lax.broadcasted_iota(jnp.int32, (blk, 2 * blk), 1)
    suffix = jnp.where((jj >= ss) | (ss >= blk), 1.0, 0.0).astype(BF16)
    first_head = lax.broadcasted_iota(jnp.int32, (1, 2 * HEAD_DIM), 1) < HEAD_DIM
    zero = jnp.zeros((), BF16)

    lanes = [slice(p * 2 * HEAD_DIM, (p + 1) * 2 * HEAD_DIM) for p in range(pairs)]
    heads = [(p, hh) for p in range(pairs) for hh in range(2)]

    nh = len(heads)
    pw = 2 * HEAD_DIM

    def tiles(kbs, diagonal, absent=None):
        starts = [pl.multiple_of(kb * blk, blk) for kb in kbs]
        zs = [None] * (len(kbs) * nh)
        for p in range(pairs):
            q2 = jnp.concatenate([q_ref[0, :, (2 * p + hh) * pw:(2 * p + hh + 1) * pw] for hh in range(2)],
                                 axis=0)
            keys = jnp.concatenate([k_ref[0, pl.ds(start, blk), lanes[p]] for start in starts], axis=0)
            z2 = _dot(q2, keys, _NT)
            for t_ in range(len(kbs)):
                for hh in range(2):
                    zs[t_ * nh + 2 * p + hh] = z2[hh * blk:(hh + 1) * blk, t_ * blk:(t_ + 1) * blk]
        sps = [_softplus(z) for z in zs]
        if diagonal:
            sps = [jnp.where(causal, sp, 0.0) for sp in sps[:nh]] + sps[nh:]
        css = [None] * len(zs)
        for p in range(pairs):
            idx = [t_ * nh + 2 * p + hh for t_ in range(len(kbs)) for hh in range(2)]
            c_ = _dot(jnp.concatenate([sps[i_] for i_ in idx], axis=0).astype(BF16), suffix)
            for j, i_ in enumerate(idx):
                css[i_] = c_[j * blk:(j + 1) * blk]
        atts = [[None] * nh for _ in kbs]
        low = None
        for n_ in range(nh):
            run = None if diagonal else run_ref[n_]
            for t_ in range(len(kbs)):
                z, cs = zs[t_ * nh + n_], css[t_ * nh + n_]
                logw = z - cs[:, :blk]
                if run is None:
                    att = jnp.where(causal, jnp.exp(logw), 0.0)
                    run = cs[:, blk:]
                else:
                    att = jnp.exp(logw - run)
                    run = run + cs[:, blk:]
                if t_ == 0 and absent is not None:
                    run = run + absent
                atts[t_][n_] = att.astype(BF16)
            run_ref[n_] = run
            low = run if low is None else jnp.minimum(low, run)
        done = (jnp.min(low) > SB_LOG_UNDERFLOW).astype(jnp.int32)
        for p in range(pairs):
            vvs = []
            for start in starts:
                vp = v_ref[0, pl.ds(start, blk), lanes[p]]
                vvs += [jnp.where(first_head, vp, zero), jnp.where(first_head, zero, vp)]
            lhs = jnp.concatenate([atts[t_][2 * p + hh] for t_ in range(len(kbs)) for hh in range(2)], axis=1)
            out = _dot(lhs, jnp.concatenate(vvs, axis=0))
            if diagonal:
                acc_ref[:, lanes[p]] = out
            else:
                acc_ref[:, lanes[p]] += out
        return done

    def pair_of_tiles(i, diagonal):
        second = qi - i - 1
        absent = jnp.where(second < 0, 1e4, 0.0).astype(F32)
        return tiles([qi - i, jnp.maximum(second, 0)], diagonal, absent)

    done0 = pair_of_tiles(0, True)

    def cond(c):
        i, done = c
        return jnp.logical_and(i <= qi, done == 0)

    def body(c):
        i, _ = c
        return i + 2, pair_of_tiles(i, False)

    lax.while_loop(cond, body, (jnp.int32(2), done0))
    o_ref[0] = acc_ref[...].astype(o_ref.dtype)


def _sb_attention(q, k, v, *, blk):
    bsz, seq, d = k.shape
    heads = d // HEAD_DIM
    qspec = pl.BlockSpec((1, blk, 2 * d), lambda b, i: (b, i, 0))
    ospec = pl.BlockSpec((1, blk, d), lambda b, i: (b, i, 0))
    kvspec = pl.BlockSpec((1, seq, d), lambda b, i: (b, 0, 0), pipeline_mode=pl.Buffered(1))
    return pl.pallas_call(
        functools.partial(_sb_kernel, blk=blk),
        grid=(bsz, seq // blk),
        in_specs=[qspec, kvspec, kvspec],
        out_specs=ospec,
        out_shape=jax.ShapeDtypeStruct((bsz, seq, d), BF16),
        scratch_shapes=[pltpu.VMEM((blk, d), F32), pltpu.VMEM((heads, blk, blk), F32)],
        compiler_params=pltpu.CompilerParams(dimension_semantics=("arbitrary", "arbitrary"),
                                             vmem_limit_bytes=VMEM_LIMIT),
        name="sbattn",
    )(q, k, v)


def _merge_kernel(x_ref, ada_ref, ya_ref, ob_ref, ga_ref, gb_ref, wa_ref, wb_ref, wo_ref,
                  lng_ref, lnb_ref, o_ref, *, alpha):
    g1 = ada_ref[0, 2:3, :]
    half = x_ref.shape[0] // 2

    def rows_program(rows):
        y_a = _dot(ya_ref[rows, :], wa_ref[...])
        y_b = _dot(ob_ref[rows, :], wb_ref[...])
        yield
        merged = ga_ref[rows, :].astype(F32) * y_a + gb_ref[rows, :].astype(F32) * y_b
        mix = _dot(merged.astype(BF16), wo_ref[...])
        yield
        o_ref[rows, :] = _norm_rows(alpha * x_ref[rows, :] + g1 * mix, LN_EPS) * lng_ref[...] + lnb_ref[...]

    _run_skewed([rows_program(slice(0, half)), rows_program(slice(half, 2 * half))], lag=1)


def _merge(x2, ada3, ya, ob, ga, gb, wa_bf, wb_bf, wo_bf, ln_g, ln_b, *, seq, tm, alpha):
    m, d = x2.shape
    tiles_per_batch = seq // tm
    row = lambda i: (i, 0)
    const = lambda i: (0, 0)
    rows = lambda a: pl.BlockSpec((tm, a.shape[1]), row)
    full = lambda a: pl.BlockSpec(a.shape, const, pipeline_mode=pl.Buffered(1))
    vec = lambda a: a.reshape(1, -1)
    ins = [x2, ada3, ya, ob, ga, gb, wa_bf, wb_bf, wo_bf, vec(ln_g), vec(ln_b)]
    in_specs = [rows(x2), pl.BlockSpec((1,) + ada3.shape[1:], lambda i: (i // tiles_per_batch, 0, 0)),
                rows(ya), rows(ob), rows(ga), rows(gb)] + [full(a) for a in ins[6:]]
    return pl.pallas_call(
        functools.partial(_merge_kernel, alpha=alpha),
        grid=(m // tm,),
        in_specs=in_specs,
        out_specs=pl.BlockSpec((tm, d), row),
        out_shape=jax.ShapeDtypeStruct((m, d), F32),
        compiler_params=pltpu.CompilerParams(dimension_semantics=("arbitrary",),
                                             vmem_limit_bytes=VMEM_LIMIT),
        name="merge",
    )(*ins)


def _ffn_kernel(x_ref, ada_ref, w1_ref, b1_ref, w2_ref, b2_ref, lng_ref, lnb_ref, o_ref, *, alpha, fchunk):
    sh = ada_ref[0, 3:4, :]
    sc = ada_ref[0, 4:5, :]
    g2 = ada_ref[0, 5:6, :]
    d_ff = w1_ref.shape[1]
    n_chunks = d_ff // fchunk
    half = x_ref.shape[0] // 2

    def rows_program(rows):
        x = x_ref[rows, :]
        hb = (_norm_rows(x, ADALN_EPS) * (1.0 + sc) + sh).astype(BF16)
        yield
        ff = None
        for j in range(n_chunks):
            sl = slice(j * fchunk, (j + 1) * fchunk)
            t = jnp.maximum(_dot(hb, w1_ref[:, sl]) + b1_ref[:, sl], 0.0)
            part = _dot((t * t).astype(BF16), w2_ref[sl, :])
            ff = part if ff is None else ff + part
            yield
        ff = ff + b2_ref[...]
        o_ref[rows, :] = _norm_rows(alpha * x + g2 * ff, LN_EPS) * lng_ref[...] + lnb_ref[...]

    _run_skewed([rows_program(slice(0, half)), rows_program(slice(half, 2 * half))], lag=2)


def _ffn(x1, ada3, w1_bf, b1, w2_bf, b2, ln_g, ln_b, *, seq, tm, alpha, fchunk):
    m, d = x1.shape
    tiles_per_batch = seq // tm
    row = lambda i: (i, 0)
    const = lambda i: (0, 0)
    full = lambda a: pl.BlockSpec(a.shape, const, pipeline_mode=pl.Buffered(1))
    vec = lambda a: a.reshape(1, -1)
    ins = [x1, ada3, w1_bf, vec(b1), w2_bf, vec(b2), vec(ln_g), vec(ln_b)]
    in_specs = [pl.BlockSpec((tm, d), row),
                pl.BlockSpec((1,) + ada3.shape[1:], lambda i: (i // tiles_per_batch, 0, 0))]
    in_specs += [full(a) for a in ins[2:]]
    return pl.pallas_call(
        functools.partial(_ffn_kernel, alpha=alpha, fchunk=fchunk),
        grid=(m // tm,),
        in_specs=in_specs,
        out_specs=pl.BlockSpec((tm, d), row),
        out_shape=jax.ShapeDtypeStruct((m, d), F32),
        compiler_params=pltpu.CompilerParams(dimension_semantics=("arbitrary",),
                                             vmem_limit_bytes=VMEM_LIMIT),
        name="ffn",
    )(*ins)


def _branches(proj, rw_rk, rw_gn_g, rw_gn_b, hsum, bsz, seq):
    r, k, v, lw, kk, b, g, q, ks, vs = proj
    seq3 = lambda a: a.reshape(bsz, seq, -1)
    ya = _rwkv(seq3(r), seq3(k), seq3(v), seq3(lw), seq3(kk), seq3(b), seq3(g),
               rw_rk, rw_gn_g, rw_gn_b, hsum, chunk=min(RWKV_CHUNK, seq),
               rows=min(RWKV_CHUNKS_PER_STEP * RWKV_CHUNK, seq))
    o = _sb_attention(seq3(q), seq3(ks), seq3(vs), blk=min(SB_BLOCK, seq))
    return ya.reshape(bsz * seq, -1), o.reshape(bsz * seq, -1)


def _layer(x, c, w_ada, b_ada, w_in, mu_shift, rw_w0, rw_w2, rw_a0, rw_a2, rw_g2, rw_kk, rw_ka,
           rw_rk, rw_gn_g, rw_gn_b, w_branch_a, w_branch_b, w_out, ln1_g, ln1_b,
           w_ff1, b_ff1, w_ff2, b_ff2, ln2_g, ln2_b, *, alpha):
    bsz, seq, d = x.shape
    d_rwkv = rw_w0.shape[-1]
    tm = min(INPROJ_ROWS, seq)
    tm2 = min(MLP_ROWS, seq)

    ada3 = _ada(c, w_ada, b_ada).reshape(bsz, 6, d)
    lane = jnp.arange(RWKV_GROUP_HEADS * HEAD_DIM) // HEAD_DIM
    hsum = (lane[:, None] == lane[None, :]).astype(BF16)

    x2 = x.reshape(bsz * seq, d)
    (r, k, v, lw, kk, b, g, q, ks, vs, ga, gb) = _inproj(
        x2, ada3, w_in.astype(BF16), mu_shift, rw_w0, rw_w2, rw_a0, rw_a2, rw_g2, rw_kk, rw_ka, hsum,
        seq=seq, tm=tm)

    ya, ob = _branches((r, k, v, lw, kk, b, g, q, ks, vs), rw_rk, rw_gn_g, rw_gn_b, hsum, bsz, seq)

    x1 = _merge(x2, ada3, ya, ob, ga, gb,
                w_branch_a.astype(BF16), w_branch_b.astype(BF16), w_out.astype(BF16),
                ln1_g, ln1_b, seq=seq, tm=tm2, alpha=alpha)
    out = _ffn(x1, ada3, w_ff1.astype(BF16), b_ff1, w_ff2.astype(BF16), b_ff2, ln2_g, ln2_b,
               seq=seq, tm=tm2, alpha=alpha, fchunk=min(1024, w_ff1.shape[-1]))
    return out.reshape(bsz, seq, d)


def kernel(x, c, w_ada, b_ada, w_in, mu_shift, rw_w0, rw_w2, rw_a0, rw_a2, rw_g2, rw_kk, rw_ka, rw_rk,
           rw_gn_g, rw_gn_b, w_branch_a, w_branch_b, w_out, ln1_g, ln1_b, w_ff1, b_ff1, w_ff2, b_ff2,
           ln2_g, ln2_b):
    in_dtype = x.dtype
    layer_params = (w_ada, b_ada, w_in, mu_shift, rw_w0, rw_w2, rw_a0, rw_a2, rw_g2, rw_kk, rw_ka,
                    rw_rk, rw_gn_g, rw_gn_b, w_branch_a, w_branch_b, w_out, ln1_g, ln1_b,
                    w_ff1, b_ff1, w_ff2, b_ff2, ln2_g, ln2_b)
    depth = w_ada.shape[0]
    alpha = (2.0 * depth) ** 0.25
    for l in range(depth):
        x = _layer(x, c, *[p[l] for p in layer_params], alpha=alpha)
    return x.astype(in_dtype)
```

```python
import functools

import jax
import jax.numpy as jnp
from jax import lax
from jax.experimental import pallas as pl
from jax.experimental.pallas import tpu as pltpu

F32 = jnp.float32
BF16 = jnp.bfloat16

HEAD_DIM = 64
LN_EPS = 1e-5
ADALN_EPS = 1e-6
GN_EPS = 64e-5
RWKV_CHUNK = 64
RWKV_SECTIONS = 2
RWKV_SECTION_CHUNKS = 4
RWKV_SECTION_LAG = 16
RWKV_CHUNKS_PER_STEP = RWKV_SECTIONS * RWKV_SECTION_CHUNKS
SB_BLOCK = 128
INPROJ_ROWS = 512
MLP_ROWS = 1024
VMEM_LIMIT = 56 * 1024 * 1024


def _split2(a):
    hi = a.astype(BF16)
    lo = (a - hi.astype(F32)).astype(BF16)
    return hi, lo


_NN = (((1,), (0,)), ((), ()))
_NT = (((1,), (1,)), ((), ()))
_TN = (((0,), (0,)), ((), ()))


def _dot(a, b, dims=_NN):
    return lax.dot_general(a, b, dims, preferred_element_type=F32)


def _dot3(a, b, dims=_NN):
    ah, al = _split2(a)
    bh, bl = _split2(b)
    return _dot(ah, bh, dims) + (_dot(al, bh, dims) + _dot(ah, bl, dims))


def _head_sums(x, ones_blk):
    m, w = x.shape[0], ones_blk.shape[0]
    hi, lo = _split2(x)
    outs = []
    for g in range(x.shape[1] // w):
        cols = slice(g * w, (g + 1) * w)
        both = _dot(jnp.concatenate([hi[:, cols], lo[:, cols]], axis=0), ones_blk)
        outs.append(both[:m] + both[m:])
    return jnp.concatenate(outs, axis=1)


def _dot_ones_lhs(a_bf, b):
    hi, lo = _split2(b)
    return _dot(a_bf, hi) + _dot(a_bf, lo)


_DONE = object()


def _run_skewed(programs, lag):
    live = [True] * len(programs)
    tick = 0
    while any(live):
        for i, prog in enumerate(programs):
            if live[i] and tick >= i * lag:
                live[i] = next(prog, _DONE) is not _DONE
        tick += 1


_NEG_LOG2_E = -1.4426950408889634


def _softplus(y):
    return jnp.maximum(y, 0.0) + jnp.log(1.0 + jnp.exp2(jnp.abs(y) * _NEG_LOG2_E))


def _sigmoid(y):
    return 1.0 / (1.0 + jnp.exp(-y))


def _norm_rows(x, eps):
    mu = jnp.mean(x, axis=-1, keepdims=True)
    xc = x - mu
    var = jnp.mean(xc * xc, axis=-1, keepdims=True)
    return xc * lax.rsqrt(var + eps)


def _ada_kernel(c_ref, w_ref, b_ref, o_ref):
    c = c_ref[...]
    s = c * _sigmoid(c)
    o_ref[...] = jnp.dot(s, w_ref[...], preferred_element_type=F32,
                         precision=lax.Precision.HIGHEST) + b_ref[...]


def _ada(c, w_ada, b_ada):
    bsz, d = c.shape
    n = w_ada.shape[1]
    return pl.pallas_call(
        _ada_kernel,
        grid=(n // d,),
        in_specs=[pl.BlockSpec((bsz, d), lambda j: (0, 0)),
                  pl.BlockSpec((d, d), lambda j: (0, j)),
                  pl.BlockSpec((1, d), lambda j: (0, j))],
        out_specs=pl.BlockSpec((bsz, d), lambda j: (0, j)),
        out_shape=jax.ShapeDtypeStruct((bsz, n), F32),
        name="ada",
    )(c, w_ada, b_ada.reshape(1, n))


def _inproj_kernel(x_ref, ada_ref, win_ref, mu_ref, w0_ref, w2_ref, a0_ref, a2_ref, g2_ref,
                   kkw_ref, kaw_ref, hsum_ref,
                   r_ref, k_ref, v_ref, lw_ref, kk_ref, b_ref, g_ref,
                   q_ref, ks_ref, vs_ref, ga_ref, gb_ref,
                   carry_ref, *, tiles_per_batch, d_rwkv, d_decay, d_aaa, d_gate, d_sb, d_model):
    i = pl.program_id(0)
    tm = x_ref.shape[0]
    d_shift = 3 * d_rwkv + d_decay + d_aaa + d_gate

    sh = ada_ref[0, 0:1, :]
    sc = ada_ref[0, 1:2, :]

    @pl.when(i % tiles_per_batch == 0)
    def _():
        carry_ref[...] = jnp.zeros_like(carry_ref)

    half = d_model // 2
    pair_w = 2 * HEAD_DIM
    first_head = lax.broadcasted_iota(jnp.int32, (1, pair_w), 1) < HEAD_DIM

    def rows_program(rows):
        hm = rows.stop - rows.start
        hb = (_norm_rows(x_ref[rows, :], ADALN_EPS) * (1.0 + sc) + sh).astype(BF16)
        row0 = lax.broadcasted_iota(jnp.int32, (hm, 1), 0) == 0
        yield

        def shifted(lo, width):
            cols = slice(lo, lo + width)
            z = _dot(hb, win_ref[:, cols])
            prev = jnp.where(row0, carry_ref[:, cols], pltpu.roll(z, 1, 0))
            carry_ref[:, cols] = z[hm - 1:hm, :]
            return z + mu_ref[:, cols] * (prev - z)

        def store_queries(z):
            zq = (z * (HEAD_DIM ** -0.5)).astype(BF16)
            zero = jnp.zeros((), BF16)
            for p in range(d_sb // pair_w):
                qp = zq[:, p * pair_w:(p + 1) * pair_w]
                q_ref[rows, (2 * p) * pair_w:(2 * p + 1) * pair_w] = jnp.where(first_head, qp, zero)
                q_ref[rows, (2 * p + 1) * pair_w:(2 * p + 2) * pair_w] = jnp.where(first_head, zero, qp)

        def plain(ref, cols, fn=lambda z: z):
            def store(z):
                ref[rows, cols] = fn(z).astype(BF16)
            return store

        all_cols = slice(0, d_sb)
        rest = [(store_queries, d_sb), (plain(ks_ref, all_cols), d_sb), (plain(vs_ref, all_cols), d_sb),
                (plain(ga_ref, slice(0, half), _sigmoid), half),
                (plain(ga_ref, slice(half, d_model), _sigmoid), half),
                (plain(gb_ref, slice(0, half), _sigmoid), half),
                (plain(gb_ref, slice(half, d_model), _sigmoid), half)]
        rest_col = [d_shift]

        def project_next():
            store, width = rest.pop(0)
            store(_dot(hb, win_ref[:, rest_col[0]:rest_col[0] + width]))
            rest_col[0] += width

        zl = shifted(3 * d_rwkv, d_decay + d_aaa + d_gate)
        zw = zl[:, :d_decay]
        za = zl[:, d_decay:d_decay + d_aaa]
        zg = zl[:, d_decay + d_aaa:]
        k = shifted(d_rwkv, d_rwkv)
        yield
        ww = w0_ref[...] + _dot(jnp.tanh(zw).astype(BF16), w2_ref[...])
        w_log = -_softplus(-ww) - 0.5
        lw_ref[rows, :] = -jnp.exp(w_log)
        project_next()
        yield
        arate = _sigmoid(a0_ref[...] + _dot(za.astype(BF16), a2_ref[...]))
        g_ref[rows, :] = _dot(_sigmoid(zg).astype(BF16), g2_ref[...])
        r_ref[rows, :] = shifted(0, d_rwkv)
        yield
        kkraw = k * kkw_ref[...]
        ssq = _head_sums(kkraw * kkraw, hsum_ref[...])
        kk = kkraw / jnp.maximum(jnp.sqrt(ssq), 1e-12)
        v_ref[rows, :] = shifted(2 * d_rwkv, d_rwkv)
        k_ref[rows, :] = k * (1.0 + (arate - 1.0) * kaw_ref[...])
        yield
        project_next()
        kk_ref[rows, :] = kk
        b_ref[rows, :] = kk * arate
        yield
        while rest:
            project_next()
            yield

    hm = tm // 2
    _run_skewed([rows_program(slice(0, hm)), rows_program(slice(hm, tm))], lag=2)


def _inproj(x2, ada3, win_bf, mu_shift, rw_w0, rw_w2, rw_a0, rw_a2, rw_g2, rw_kk, rw_ka, hsum,
            *, seq, tm):
    m, d_model = x2.shape
    d_rwkv = rw_w0.shape[-1]
    d_decay, d_aaa, d_gate = rw_w2.shape[0], rw_a2.shape[0], rw_g2.shape[0]
    d_shift = 3 * d_rwkv + d_decay + d_aaa + d_gate
    d_sb = (win_bf.shape[1] - d_shift - 2 * d_model) // 3
    tiles_per_batch = seq // tm
    row = lambda i: (i, 0)
    const = lambda i: (0, 0)
    full = lambda a: pl.BlockSpec(a.shape, const, pipeline_mode=pl.Buffered(1))
    vec = lambda a: a.reshape(1, -1)
    ins = [x2, ada3, win_bf, vec(mu_shift), vec(rw_w0), rw_w2.astype(BF16), vec(rw_a0),
           rw_a2.astype(BF16), rw_g2.astype(BF16), vec(rw_kk), vec(rw_ka), hsum]
    in_specs = [pl.BlockSpec((tm, d_model), row),
                pl.BlockSpec((1,) + ada3.shape[1:], lambda i: (i // tiles_per_batch, 0, 0))]
    in_specs += [full(a) for a in ins[2:]]
    widths = [d_rwkv] * 7 + [2 * d_sb, d_sb, d_sb] + [d_model] * 2
    dtypes = [F32] * 7 + [BF16] * 5
    out_shape = [jax.ShapeDtypeStruct((m, w), dt) for w, dt in zip(widths, dtypes)]
    out_specs = [pl.BlockSpec((tm, w), row) for w in widths]
    kern = functools.partial(_inproj_kernel, tiles_per_batch=tiles_per_batch, d_rwkv=d_rwkv,
                             d_decay=d_decay, d_aaa=d_aaa, d_gate=d_gate, d_sb=d_sb, d_model=d_model)
    return pl.pallas_call(
        kern,
        grid=(m // tm,),
        in_specs=in_specs,
        out_specs=out_specs,
        out_shape=out_shape,
        scratch_shapes=[pltpu.VMEM((1, d_shift), F32)],
        compiler_params=pltpu.CompilerParams(dimension_semantics=("arbitrary",),
                                             vmem_limit_bytes=VMEM_LIMIT),
        name="inproj",
    )(*ins)


RWKV_GROUP_HEADS = 4


def _block_diag(q, head_masks):
    zero = jnp.zeros((), q.dtype)
    return jnp.concatenate([jnp.where(m, q, zero) for m in head_masks], axis=0)


def _packed_dot1(lhs, q, head_masks, dims=_NN):
    return _dot(lhs.astype(BF16), _block_diag(q.astype(BF16), head_masks), dims)


def _packed_dot_wide_lhs(lhs, q, head_masks, dims=_NN):
    m = lhs.shape[0]
    lh, ll = _split2(lhs)
    both = _dot(jnp.concatenate([lh, ll], axis=0), _block_diag(q.astype(BF16), head_masks), dims)
    return both[:m] + both[m:]


def _rwkv_kernel(r_ref, k_ref, v_ref, lw_ref, kk_ref, b_ref, g_ref, rk_ref, gng_ref, gnb_ref, hsum_ref,
                 o_ref, state_ref, *, chunk):
    c = pl.program_id(1)
    n = HEAD_DIM
    rows, d = r_ref.shape[1], r_ref.shape[2]
    gw = RWKV_GROUP_HEADS * n
    n_groups = d // gw

    @pl.when(c == 0)
    def _():
        state_ref[...] = jnp.zeros_like(state_ref)

    t_idx = lax.broadcasted_iota(jnp.int32, (chunk, gw), 0)
    lane = lax.broadcasted_iota(jnp.int32, (chunk, gw), 1)
    i_idx = lane & (n - 1)
    strict = t_idx > i_idx
    incl = t_idx >= i_idx
    lane1 = lax.broadcasted_iota(jnp.int32, (1, gw), 1)
    hm = [(lane1 >> (n.bit_length() - 1)) == h for h in range(RWKV_GROUP_HEADS)]
    same = lambda s: (t_idx >> s) == (i_idx >> s)

    half_w = gw // 2
    lane_half = [(lane1 >> (half_w.bit_length() - 1)) == j for j in range(2)]
    second_in_half = ((lane >> (n.bit_length() - 1)) & 1) == 1

    def halves_rows(x):
        return jnp.concatenate([x[:, :half_w], x[:, half_w:]], axis=0).astype(BF16)

    def halves_masked(w):
        return _block_diag(w.astype(BF16), lane_half)

    def same_head(out):
        return jnp.where(second_in_half, out[n:], out[:n])

    lane_h = lax.broadcasted_iota(jnp.int32, (1, half_w), 1)
    in_half = [(lane_h >> (n.bit_length() - 1)) == j for j in range(half_w // n)]

    def two_rhs(lhs, x, y, dims=_NN):
        lb, xb, yb = lhs.astype(BF16), x.astype(BF16), y.astype(BF16)
        out_x, out_y = [], []
        for j in range(gw // half_w):
            cols = slice(j * half_w, (j + 1) * half_w)
            w = jnp.concatenate([_block_diag(xb[:, cols], in_half), _block_diag(yb[:, cols], in_half)],
                                axis=1 if dims == _NN else 0)
            o = _dot(lb[:, cols], w, dims)
            out_x.append(o[:, :half_w])
            out_y.append(o[:, half_w:])
        return jnp.concatenate(out_x, axis=1), jnp.concatenate(out_y, axis=1)

    sec_rows = rows // RWKV_SECTIONS
    sec_chunks = sec_rows // chunk
    rr = lax.broadcasted_iota(jnp.int32, (sec_rows, sec_rows), 0)
    cc = lax.broadcasted_iota(jnp.int32, (sec_rows, sec_rows), 1)
    cs = chunk.bit_length() - 1
    ltri = jnp.where((rr >= cc) & ((rr >> cs) == (cc >> cs)), 1.0, 0.0).astype(BF16)
    eye = jnp.where(t_idx == i_idx, 1.0, 0.0)
    chains = [(ci, gi) for ci in range(sec_chunks) for gi in range(n_groups)]
    win = lambda x, ci, gi: x[ci * chunk:(ci + 1) * chunk, gi * gw:(gi + 1) * gw]

    def section(row0):
        rs = slice(row0, row0 + sec_rows)
        r, k, v, lw = r_ref[0, rs, :], k_ref[0, rs, :], v_ref[0, rs, :], lw_ref[0, rs, :]
        kk, b = kk_ref[0, rs, :], b_ref[0, rs, :]
        cum = _dot_ones_lhs(ltri, lw)
        a_t = -kk * jnp.exp(cum - lw)
        r_t = r * jnp.exp(cum)
        e_neg = jnp.exp(-cum)
        b_t = b * e_neg
        k_t = k * e_neg
        tots = [cum[(ci + 1) * chunk - 1:(ci + 1) * chunk, :] for ci in range(sec_chunks)]
        e_bars = [jnp.exp(tots[ci] - cum[ci * chunk:(ci + 1) * chunk, :]) for ci in range(sec_chunks)]
        b_bar = [win(b, ci, gi) * e_bars[ci][:, gi * gw:(gi + 1) * gw] for ci, gi in chains]
        k_bar = [win(k, ci, gi) * e_bars[ci][:, gi * gw:(gi + 1) * gw] for ci, gi in chains]
        at = [win(a_t, ci, gi) for ci, gi in chains]
        rt = [win(r_t, ci, gi) for ci, gi in chains]
        vv = [win(v, ci, gi) for ci, gi in chains]
        yield

        ar = [jnp.concatenate([a_, r_], axis=0) for a_, r_ in zip(at, rt)]
        grams = [two_rhs(x, win(b_t, ci, gi), win(k_t, ci, gi), _NT) for x, (ci, gi) in zip(ar, chains)]
        gram_b = [x[0] for x in grams]
        gram_k = [x[1] for x in grams]
        yield
        m_ab = [jnp.where(strict, g_[:chunk], 0.0) for g_ in gram_b]
        p_rb = [jnp.where(incl, g_[chunk:], 0.0) for g_ in gram_b]
        m_ak = [jnp.where(strict, g_[:chunk], 0.0) for g_ in gram_k]
        p_rk = [jnp.where(incl, g_[chunk:], 0.0) for g_ in gram_k]
        yield

        ts = [eye + jnp.where(same(1), m, 0.0) for m in m_ab]
        s = 1
        while (1 << s) < chunk:
            level = same(s + 1) & jnp.logical_not(same(s))
            offs = [jnp.where(level, m, 0.0) for m in m_ab]
            mids = [_packed_dot1(t, o, hm) for t, o in zip(ts, offs)]
            yield
            ts = [t + _packed_dot1(md, t, hm) for t, md in zip(ts, mids)]
            yield
            s += 1

        both = [_packed_dot1(jnp.concatenate([m, pk], axis=0), x, hm) for m, pk, x in zip(m_ak, p_rk, vv)]
        mv = [x[:chunk] for x in both]
        pkv = [x[chunk:] for x in both]
        yield
        hat = [two_rhs(t, x, y) for t, x, y in zip(ts, at, mv)]
        a_hat = [x[0] for x in hat]
        u0 = [x[1] for x in hat]
        yield
        pb = [two_rhs(p, a, u) for p, a, u in zip(p_rb, a_hat, u0)]
        r_hat = [x + y[0] for x, y in zip(rt, pb)]
        y0 = [x + y[1] for x, y in zip(pkv, pb)]
        yield
        g_mat = [same_head(_dot(halves_rows(a), halves_masked(bb), _TN))
                 for a, bb in zip(a_hat, b_bar)]
        h_mat = [same_head(_dot(jnp.concatenate([halves_rows(u), halves_rows(x)], axis=0),
                                jnp.concatenate([halves_masked(bb), halves_masked(kb)], axis=0), _TN))
                 for u, x, bb, kb in zip(u0, vv, b_bar, k_bar)]
        yield

        y_rows = []
        for ci in range(sec_chunks):
            y_groups = []
            for gi in range(n_groups):
                j = ci * n_groups + gi
                s0 = state_ref[gi]
                y_groups.append(_packed_dot1(r_hat[j], s0, hm, _NT) + y0[j])
                w_tot = jnp.exp(tots[ci][:, gi * gw:(gi + 1) * gw])
                state_ref[gi] = s0 * w_tot + _packed_dot_wide_lhs(s0, g_mat[j], hm) + h_mat[j]
            y_rows.append(jnp.concatenate(y_groups, axis=1))
            yield
        y = jnp.concatenate(y_rows, axis=0)

        hsum = hsum_ref[...]
        inv_n = 1.0 / n
        mu = _head_sums(y, hsum) * inv_n
        yc = y - mu
        var = _head_sums(yc * yc, hsum) * inv_n
        yn = yc * lax.rsqrt(var + GN_EPS) * gng_ref[...] + gnb_ref[...]
        bonus = _head_sums(r * k * rk_ref[...], hsum)
        o_ref[0, rs, :] = ((yn + bonus * v) * g_ref[0, rs, :]).astype(o_ref.dtype)

    assert RWKV_SECTION_LAG >= sec_chunks
    _run_skewed([section(si * sec_rows) for si in range(RWKV_SECTIONS)], lag=RWKV_SECTION_LAG)


def _rwkv(r, k, v, lw, kk, b, g, rw_rk, gn_g, gn_b, hsum, *, chunk, rows):
    bsz, seq, d = r.shape
    n_groups = d // (RWKV_GROUP_HEADS * HEAD_DIM)
    blk = pl.BlockSpec((1, rows, d), lambda bi, ci: (bi, ci, 0))
    const = lambda bi, ci: (0, 0)
    vec = lambda a: a.reshape(1, d)
    return pl.pallas_call(
        functools.partial(_rwkv_kernel, chunk=chunk),
        grid=(bsz, seq // rows),
        in_specs=[blk] * 7 + [pl.BlockSpec((1, d), const)] * 3 + [pl.BlockSpec(hsum.shape, const)],
        out_specs=blk,
        out_shape=jax.ShapeDtypeStruct((bsz, seq, d), BF16),
        scratch_shapes=[pltpu.VMEM((n_groups, HEAD_DIM, RWKV_GROUP_HEADS * HEAD_DIM), F32)],
        compiler_params=pltpu.CompilerParams(dimension_semantics=("arbitrary", "arbitrary"),
                                             vmem_limit_bytes=VMEM_LIMIT),
        name="rwkv",
    )(r, k, v, lw, kk, b, g, vec(rw_rk), vec(gn_g), vec(gn_b), hsum)


SB_LOG_UNDERFLOW = 105.0


def _sb_kernel(q_ref, k_ref, v_ref, o_ref, acc_ref, run_ref, *, blk):
    qi = pl.program_id(1)
    pairs = k_ref.shape[2] // (2 * HEAD_DIM)
    row = lax.broadcasted_iota(jnp.int32, (blk, blk), 0)
    col = lax.broadcasted_iota(jnp.int32, (blk, blk), 1)
    causal = col < row
    jj = lax.broadcasted_iota(jnp.int32, (blk, 2 * blk), 0)
    ss = lax.broadcasted_iota(jnp.int32, (blk, 2 * blk), 1)
    suffix = jnp.where((jj >= ss) | (ss >= blk), 1.0, 0.0).astype(BF16)
    first_head = lax.broadcasted_iota(jnp.int32, (1, 2 * HEAD_DIM), 1) < HEAD_DIM
    zero = jnp.zeros((), BF16)

    lanes = [slice(p * 2 * HEAD_DIM, (p + 1) * 2 * HEAD_DIM) for p in range(pairs)]
    heads = [(p, hh) for p in range(pairs) for hh in range(2)]

    nh = len(heads)
    pw = 2 * HEAD_DIM

    def tiles(kbs, diagonal, absent=None):
        starts = [pl.multiple_of(kb * blk, blk) for kb in kbs]
        zs = [None] * (len(kbs) * nh)
        for p in range(pairs):
            q2 = jnp.concatenate([q_ref[0, :, (2 * p + hh) * pw:(2 * p + hh + 1) * pw] for hh in range(2)],
                                 axis=0)
            keys = jnp.concatenate([k_ref[0, pl.ds(start, blk), lanes[p]] for start in starts], axis=0)
            z2 = _dot(q2, keys, _NT)
            for t_ in range(len(kbs)):
                for hh in range(2):
                    zs[t_ * nh + 2 * p + hh] = z2[hh * blk:(hh + 1) * blk, t_ * blk:(t_ + 1) * blk]
        sps = [_softplus(z) for z in zs]
        if diagonal:
            sps = [jnp.where(causal, sp, 0.0) for sp in sps[:nh]] + sps[nh:]
        css = [None] * len(zs)
        for p in range(pairs):
            idx = [t_ * nh + 2 * p + hh for t_ in range(len(kbs)) for hh in range(2)]
            c_ = _dot(jnp.concatenate([sps[i_] for i_ in idx], axis=0).astype(BF16), suffix)
            for j, i_ in enumerate(idx):
                css[i_] = c_[j * blk:(j + 1) * blk]
        atts = [[None] * nh for _ in kbs]
        low = None
        for n_ in range(nh):
            run = None if diagonal else run_ref[n_]
            for t_ in range(len(kbs)):
                z, cs = zs[t_ * nh + n_], css[t_ * nh + n_]
                logw = z - cs[:, :blk]
                if run is None:
                    att = jnp.where(causal, jnp.exp(logw), 0.0)
                    run = cs[:, blk:]
                else:
                    att = jnp.exp(logw - run)
                    run = run + cs[:, blk:]
                if t_ == 0 and absent is not None:
                    run = run + absent
                atts[t_][n_] = att.astype(BF16)
            run_ref[n_] = run
            low = run if low is None else jnp.minimum(low, run)
        done = (jnp.min(low) > SB_LOG_UNDERFLOW).astype(jnp.int32)
        for p in range(pairs):
            vvs = []
            for start in starts:
                vp = v_ref[0, pl.ds(start, blk), lanes[p]]
                vvs += [jnp.where(first_head, vp, zero), jnp.where(first_head, zero, vp)]
            lhs = jnp.concatenate([atts[t_][2 * p + hh] for t_ in range(len(kbs)) for hh in range(2)], axis=1)
            out = _dot(lhs, jnp.concatenate(vvs, axis=0))
            if diagonal:
                acc_ref[:, lanes[p]] = out
            else:
                acc_ref[:, lanes[p]] += out
        return done

    def pair_of_tiles(i, diagonal):
        second = qi - i - 1
        absent = jnp.where(second < 0, 1e4, 0.0).astype(F32)
        return tiles([qi - i, jnp.maximum(second, 0)], diagonal, absent)

    done0 = pair_of_tiles(0, True)

    def cond(c):
        i, done = c
        return jnp.logical_and(i <= qi, done == 0)

    def body(c):
        i, _ = c
        return i + 2, pair_of_tiles(i, False)

    lax.while_loop(cond, body, (jnp.int32(2), done0))
    o_ref[0] = acc_ref[...].astype(o_ref.dtype)


def _sb_attention(q, k, v, *, blk):
    bsz, seq, d = k.shape
    heads = d // HEAD_DIM
    qspec = pl.BlockSpec((1, blk, 2 * d), lambda b, i: (b, i, 0))
    ospec = pl.BlockSpec((1, blk, d), lambda b, i: (b, i, 0))
    kvspec = pl.BlockSpec((1, seq, d), lambda b, i: (b, 0, 0), pipeline_mode=pl.Buffered(1))
    return pl.pallas_call(
        functools.partial(_sb_kernel, blk=blk),
        grid=(bsz, seq // blk),
        in_specs=[qspec, kvspec, kvspec],
        out_specs=ospec,
        out_shape=jax.ShapeDtypeStruct((bsz, seq, d), BF16),
        scratch_shapes=[pltpu.VMEM((blk, d), F32), pltpu.VMEM((heads, blk, blk), F32)],
        compiler_params=pltpu.CompilerParams(dimension_semantics=("arbitrary", "arbitrary"),
                                             vmem_limit_bytes=VMEM_LIMIT),
        name="sbattn",
    )(q, k, v)


def _merge_kernel(x_ref, ada_ref, ya_ref, ob_ref, ga_ref, gb_ref, wa_ref, wb_ref, wo_ref,
                  lng_ref, lnb_ref, o_ref, *, alpha):
    g1 = ada_ref[0, 2:3, :]
    half = x_ref.shape[0] // 2

    def rows_program(rows):
        y_a = _dot(ya_ref[rows, :], wa_ref[...])
        y_b = _dot(ob_ref[rows, :], wb_ref[...])
        yield
        merged = ga_ref[rows, :].astype(F32) * y_a + gb_ref[rows, :].astype(F32) * y_b
        mix = _dot(merged.astype(BF16), wo_ref[...])
        yield
        o_ref[rows, :] = _norm_rows(alpha * x_ref[rows, :] + g1 * mix, LN_EPS) * lng_ref[...] + lnb_ref[...]

    _run_skewed([rows_program(slice(0, half)), rows_program(slice(half, 2 * half))], lag=1)


def _merge(x2, ada3, ya, ob, ga, gb, wa_bf, wb_bf, wo_bf, ln_g, ln_b, *, seq, tm, alpha):
    m, d = x2.shape
    tiles_per_batch = seq // tm
    row = lambda i: (i, 0)
    const = lambda i: (0, 0)
    rows = lambda a: pl.BlockSpec((tm, a.shape[1]), row)
    full = lambda a: pl.BlockSpec(a.shape, const, pipeline_mode=pl.Buffered(1))
    vec = lambda a: a.reshape(1, -1)
    ins = [x2, ada3, ya, ob, ga, gb, wa_bf, wb_bf, wo_bf, vec(ln_g), vec(ln_b)]
    in_specs = [rows(x2), pl.BlockSpec((1,) + ada3.shape[1:], lambda i: (i // tiles_per_batch, 0, 0)),
                rows(ya), rows(ob), rows(ga), rows(gb)] + [full(a) for a in ins[6:]]
    return pl.pallas_call(
        functools.partial(_merge_kernel, alpha=alpha),
        grid=(m // tm,),
        in_specs=in_specs,
        out_specs=pl.BlockSpec((tm, d), row),
        out_shape=jax.ShapeDtypeStruct((m, d), F32),
        compiler_params=pltpu.CompilerParams(dimension_semantics=("arbitrary",),
                                             vmem_limit_bytes=VMEM_LIMIT),
        name="merge",
    )(*ins)


def _ffn_kernel(x_ref, ada_ref, w1_ref, b1_ref, w2_ref, b2_ref, lng_ref, lnb_ref, o_ref, *, alpha, fchunk):
    sh = ada_ref[0, 3:4, :]
    sc = ada_ref[0, 4:5, :]
    g2 = ada_ref[0, 5:6, :]
    d_ff = w1_ref.shape[1]
    n_chunks = d_ff // fchunk
    half = x_ref.shape[0] // 2

    def rows_program(rows):
        x = x_ref[rows, :]
        hb = (_norm_rows(x, ADALN_EPS) * (1.0 + sc) + sh).astype(BF16)
        yield
        ff = None
        for j in range(n_chunks):
            sl = slice(j * fchunk, (j + 1) * fchunk)
            t = jnp.maximum(_dot(hb, w1_ref[:, sl]) + b1_ref[:, sl], 0.0)
            part = _dot((t * t).astype(BF16), w2_ref[sl, :])
            ff = part if ff is None else ff + part
            yield
        ff = ff + b2_ref[...]
        o_ref[rows, :] = _norm_rows(alpha * x + g2 * ff, LN_EPS) * lng_ref[...] + lnb_ref[...]

    _run_skewed([rows_program(slice(0, half)), rows_program(slice(half, 2 * half))], lag=2)


def _ffn(x1, ada3, w1_bf, b1, w2_bf, b2, ln_g, ln_b, *, seq, tm, alpha, fchunk):
    m, d = x1.shape
    tiles_per_batch = seq // tm
    row = lambda i: (i, 0)
    const = lambda i: (0, 0)
    full = lambda a: pl.BlockSpec(a.shape, const, pipeline_mode=pl.Buffered(1))
    vec = lambda a: a.reshape(1, -1)
    ins = [x1, ada3, w1_bf, vec(b1), w2_bf, vec(b2), vec(ln_g), vec(ln_b)]
    in_specs = [pl.BlockSpec((tm, d), row),
                pl.BlockSpec((1,) + ada3.shape[1:], lambda i: (i // tiles_per_batch, 0, 0))]
    in_specs += [full(a) for a in ins[2:]]
    return pl.pallas_call(
        functools.partial(_ffn_kernel, alpha=alpha, fchunk=fchunk),
        grid=(m // tm,),
        in_specs=in_specs,
        out_specs=pl.BlockSpec((tm, d), row),
        out_shape=jax.ShapeDtypeStruct((m, d), F32),
        compiler_params=pltpu.CompilerParams(dimension_semantics=("arbitrary",),
                                             vmem_limit_bytes=VMEM_LIMIT),
        name="ffn",
    )(*ins)


def _branches(proj, rw_rk, rw_gn_g, rw_gn_b, hsum, bsz, seq):
    r, k, v, lw, kk, b, g, q, ks, vs = proj
    seq3 = lambda a: a.reshape(bsz, seq, -1)
    ya = _rwkv(seq3(r), seq3(k), seq3(v), seq3(lw), seq3(kk), seq3(b), seq3(g),
               rw_rk, rw_gn_g, rw_gn_b, hsum, chunk=min(RWKV_CHUNK, seq),
               rows=min(RWKV_CHUNKS_PER_STEP * RWKV_CHUNK, seq))
    o = _sb_attention(seq3(q), seq3(ks), seq3(vs), blk=min(SB_BLOCK, seq))
    return ya.reshape(bsz * seq, -1), o.reshape(bsz * seq, -1)


def _layer(x, c, w_ada, b_ada, w_in, mu_shift, rw_w0, rw_w2, rw_a0, rw_a2, rw_g2, rw_kk, rw_ka,
           rw_rk, rw_gn_g, rw_gn_b, w_branch_a, w_branch_b, w_out, ln1_g, ln1_b,
           w_ff1, b_ff1, w_ff2, b_ff2, ln2_g, ln2_b, *, alpha):
    bsz, seq, d = x.shape
    d_rwkv = rw_w0.shape[-1]
    tm = min(INPROJ_ROWS, seq)
    tm2 = min(MLP_ROWS, seq)

    ada3 = _ada(c, w_ada, b_ada).reshape(bsz, 6, d)
    lane = jnp.arange(RWKV_GROUP_HEADS * HEAD_DIM) // HEAD_DIM
    hsum = (lane[:, None] == lane[None, :]).astype(BF16)

    x2 = x.reshape(bsz * seq, d)
    (r, k, v, lw, kk, b, g, q, ks, vs, ga, gb) = _inproj(
        x2, ada3, w_in.astype(BF16), mu_shift, rw_w0, rw_w2, rw_a0, rw_a2, rw_g2, rw_kk, rw_ka, hsum,
        seq=seq, tm=tm)

    ya, ob = _branches((r, k, v, lw, kk, b, g, q, ks, vs), rw_rk, rw_gn_g, rw_gn_b, hsum, bsz, seq)

    x1 = _merge(x2, ada3, ya, ob, ga, gb,
                w_branch_a.astype(BF16), w_branch_b.astype(BF16), w_out.astype(BF16),
                ln1_g, ln1_b, seq=seq, tm=tm2, alpha=alpha)
    out = _ffn(x1, ada3, w_ff1.astype(BF16), b_ff1, w_ff2.astype(BF16), b_ff2, ln2_g, ln2_b,
               seq=seq, tm=tm2, alpha=alpha, fchunk=min(1024, w_ff1.shape[-1]))
    return out.reshape(bsz, seq, d)


def kernel(x, c, w_ada, b_ada, w_in, mu_shift, rw_w0, rw_w2, rw_a0, rw_a2, rw_g2, rw_kk, rw_ka, rw_rk,
           rw_gn_g, rw_gn_b, w_branch_a, w_branch_b, w_out, ln1_g, ln1_b, w_ff1, b_ff1, w_ff2, b_ff2,
           ln2_g, ln2_b):
    in_dtype = x.dtype
    layer_params = (w_ada, b_ada, w_in, mu_shift, rw_w0, rw_w2, rw_a0, rw_a2, rw_g2, rw_kk, rw_ka,
                    rw_rk, rw_gn_g, rw_gn_b, w_branch_a, w_branch_b, w_out, ln1_g, ln1_b,
                    w_ff1, b_ff1, w_ff2, b_ff2, ln2_g, ln2_b)
    depth = w_ada.shape[0]
    alpha = (2.0 * depth) ** 0.25
    for l in range(depth):
        x = _layer(x, c, *[p[l] for p in layer_params], alpha=alpha)
    return x.astype(in_dtype)
```

```python
import functools

import jax
import jax.numpy as jnp
from jax import lax
from jax.experimental import pallas as pl
from jax.experimental.pallas import tpu as pltpu

F32 = jnp.float32
BF16 = jnp.bfloat16

HEAD_DIM = 64
LN_EPS = 1e-5
ADALN_EPS = 1e-6
GN_EPS = 64e-5
RWKV_CHUNK = 64
RWKV_SECTIONS = 2
RWKV_SECTION_CHUNKS = 4
RWKV_SECTION_LAG = 12
RWKV_CHUNKS_PER_STEP = RWKV_SECTIONS * RWKV_SECTION_CHUNKS
SB_BLOCK = 128
INPROJ_ROWS = 512
MLP_ROWS = 1024
VMEM_LIMIT = 56 * 1024 * 1024


def _split2(a):
    hi = a.astype(BF16)
    lo = (a - hi.astype(F32)).astype(BF16)
    return hi, lo


_NN = (((1,), (0,)), ((), ()))
_NT = (((1,), (1,)), ((), ()))
_TN = (((0,), (0,)), ((), ()))


def _dot(a, b, dims=_NN):
    return lax.dot_general(a, b, dims, preferred_element_type=F32)


def _head_sums(x, ones_blk):
    m, w = x.shape[0], ones_blk.shape[0]
    hi, lo = _split2(x)
    outs = []
    for g in range(x.shape[1] // w):
        cols = slice(g * w, (g + 1) * w)
        both = _dot(jnp.concatenate([hi[:, cols], lo[:, cols]], axis=0), ones_blk)
        outs.append(both[:m] + both[m:])
    return jnp.concatenate(outs, axis=1)


def _dot_ones_lhs(a_bf, b):
    hi, lo = _split2(b)
    return _dot(a_bf, hi) + _dot(a_bf, lo)


_DONE = object()


def _run_skewed(programs, lag):
    live = [True] * len(programs)
    tick = 0
    while any(live):
        for i, prog in enumerate(programs):
            if live[i] and tick >= i * lag:
                live[i] = next(prog, _DONE) is not _DONE
        tick += 1


_NEG_LOG2_E = -1.4426950408889634


def _softplus(y):
    return jnp.maximum(y, 0.0) + jnp.log(1.0 + jnp.exp2(jnp.abs(y) * _NEG_LOG2_E))


def _sigmoid(y):
    return 1.0 / (1.0 + jnp.exp(-y))


def _norm_rows(x, eps):
    mu = jnp.mean(x, axis=-1, keepdims=True)
    xc = x - mu
    var = jnp.mean(xc * xc, axis=-1, keepdims=True)
    return xc * lax.rsqrt(var + eps)


def _ada_kernel(c_ref, w_ref, b_ref, o_ref):
    c = c_ref[...]
    s = c * _sigmoid(c)
    o_ref[...] = jnp.dot(s, w_ref[...], preferred_element_type=F32,
                         precision=lax.Precision.HIGHEST) + b_ref[...]


def _ada(c, w_ada, b_ada):
    bsz, d = c.shape
    n = w_ada.shape[1]
    return pl.pallas_call(
        _ada_kernel,
        grid=(n // d,),
        in_specs=[pl.BlockSpec((bsz, d), lambda j: (0, 0)),
                  pl.BlockSpec((d, d), lambda j: (0, j)),
                  pl.BlockSpec((1, d), lambda j: (0, j))],
        out_specs=pl.BlockSpec((bsz, d), lambda j: (0, j)),
        out_shape=jax.ShapeDtypeStruct((bsz, n), F32),
        name="ada",
    )(c, w_ada, b_ada.reshape(1, n))


def _inproj_kernel(x_ref, ada_ref, win_ref, mu_ref, w0_ref, w2_ref, a0_ref, a2_ref, g2_ref,
                   kkw_ref, kaw_ref, hsum_ref,
                   r_ref, k_ref, v_ref, lw_ref, kk_ref, b_ref, g_ref,
                   q_ref, ks_ref, vs_ref, ga_ref, gb_ref,
                   carry_ref, *, tiles_per_batch, d_rwkv, d_decay, d_aaa, d_gate, d_sb, d_model):
    i = pl.program_id(0)
    tm = x_ref.shape[0]
    d_shift = 3 * d_rwkv + d_decay + d_aaa + d_gate

    sh = ada_ref[0, 0:1, :]
    sc = ada_ref[0, 1:2, :]

    @pl.when(i % tiles_per_batch == 0)
    def _():
        carry_ref[...] = jnp.zeros_like(carry_ref)

    half = d_model // 2
    pair_w = 2 * HEAD_DIM
    first_head = lax.broadcasted_iota(jnp.int32, (1, pair_w), 1) < HEAD_DIM

    def rows_program(rows):
        hm = rows.stop - rows.start
        hb = (_norm_rows(x_ref[rows, :], ADALN_EPS) * (1.0 + sc) + sh).astype(BF16)
        row0 = lax.broadcasted_iota(jnp.int32, (hm, 1), 0) == 0
        yield

        def shifted(lo, width):
            cols = slice(lo, lo + width)
            z = _dot(hb, win_ref[:, cols])
            prev = jnp.where(row0, carry_ref[:, cols], pltpu.roll(z, 1, 0))
            carry_ref[:, cols] = z[hm - 1:hm, :]
            return z + mu_ref[:, cols] * (prev - z)

        def store_queries(z):
            zq = (z * (HEAD_DIM ** -0.5)).astype(BF16)
            zero = jnp.zeros((), BF16)
            for p in range(d_sb // pair_w):
                qp = zq[:, p * pair_w:(p + 1) * pair_w]
                q_ref[rows, (2 * p) * pair_w:(2 * p + 1) * pair_w] = jnp.where(first_head, qp, zero)
                q_ref[rows, (2 * p + 1) * pair_w:(2 * p + 2) * pair_w] = jnp.where(first_head, zero, qp)

        def plain(ref, cols, fn=lambda z: z):
            def store(z):
                ref[rows, cols] = fn(z).astype(BF16)
            return store

        all_cols = slice(0, d_sb)
        rest = [(store_queries, d_sb), (plain(ks_ref, all_cols), d_sb), (plain(vs_ref, all_cols), d_sb),
                (plain(ga_ref, slice(0, half), _sigmoid), half),
                (plain(ga_ref, slice(half, d_model), _sigmoid), half),
                (plain(gb_ref, slice(0, half), _sigmoid), half),
                (plain(gb_ref, slice(half, d_model), _sigmoid), half)]
        rest_col = [d_shift]

        def project_next():
            store, width = rest.pop(0)
            store(_dot(hb, win_ref[:, rest_col[0]:rest_col[0] + width]))
            rest_col[0] += width

        zl = shifted(3 * d_rwkv, d_decay + d_aaa + d_gate)
        zw = zl[:, :d_decay]
        za = zl[:, d_decay:d_decay + d_aaa]
        zg = zl[:, d_decay + d_aaa:]
        k = shifted(d_rwkv, d_rwkv)
        yield
        ww = w0_ref[...] + _dot(jnp.tanh(zw).astype(BF16), w2_ref[...])
        w_log = -_softplus(-ww) - 0.5
        lw_ref[rows, :] = -jnp.exp(w_log)
        project_next()
        yield
        arate = _sigmoid(a0_ref[...] + _dot(za.astype(BF16), a2_ref[...]))
        g_ref[rows, :] = _dot(_sigmoid(zg).astype(BF16), g2_ref[...])
        r_ref[rows, :] = shifted(0, d_rwkv)
        yield
        kkraw = k * kkw_ref[...]
        ssq = _head_sums(kkraw * kkraw, hsum_ref[...])
        kk = kkraw / jnp.maximum(jnp.sqrt(ssq), 1e-12)
        v_ref[rows, :] = shifted(2 * d_rwkv, d_rwkv)
        k_ref[rows, :] = k * (1.0 + (arate - 1.0) * kaw_ref[...])
        yield
        project_next()
        kk_ref[rows, :] = kk
        b_ref[rows, :] = kk * arate
        yield
        while rest:
            project_next()
            yield

    hm = tm // 2
    _run_skewed([rows_program(slice(0, hm)), rows_program(slice(hm, tm))], lag=2)


def _inproj(x2, ada3, win_bf, mu_shift, rw_w0, rw_w2, rw_a0, rw_a2, rw_g2, rw_kk, rw_ka, hsum,
            *, seq, tm):
    m, d_model = x2.shape
    d_rwkv = rw_w0.shape[-1]
    d_decay, d_aaa, d_gate = rw_w2.shape[0], rw_a2.shape[0], rw_g2.shape[0]
    d_shift = 3 * d_rwkv + d_decay + d_aaa + d_gate
    d_sb = (win_bf.shape[1] - d_shift - 2 * d_model) // 3
    tiles_per_batch = seq // tm
    row = lambda i: (i, 0)
    const = lambda i: (0, 0)
    full = lambda a: pl.BlockSpec(a.shape, const, pipeline_mode=pl.Buffered(1))
    vec = lambda a: a.reshape(1, -1)
    ins = [x2, ada3, win_bf, vec(mu_shift), vec(rw_w0), rw_w2.astype(BF16), vec(rw_a0),
           rw_a2.astype(BF16), rw_g2.astype(BF16), vec(rw_kk), vec(rw_ka), hsum]
    in_specs = [pl.BlockSpec((tm, d_model), row),
                pl.BlockSpec((1,) + ada3.shape[1:], lambda i: (i // tiles_per_batch, 0, 0))]
    in_specs += [full(a) for a in ins[2:]]
    widths = [d_rwkv] * 7 + [2 * d_sb, d_sb, d_sb] + [d_model] * 2
    dtypes = [F32] * 7 + [BF16] * 5
    out_shape = [jax.ShapeDtypeStruct((m, w), dt) for w, dt in zip(widths, dtypes)]
    out_specs = [pl.BlockSpec((tm, w), row) for w in widths]
    kern = functools.partial(_inproj_kernel, tiles_per_batch=tiles_per_batch, d_rwkv=d_rwkv,
                             d_decay=d_decay, d_aaa=d_aaa, d_gate=d_gate, d_sb=d_sb, d_model=d_model)
    return pl.pallas_call(
        kern,
        grid=(m // tm,),
        in_specs=in_specs,
        out_specs=out_specs,
        out_shape=out_shape,
        scratch_shapes=[pltpu.VMEM((1, d_shift), F32)],
        compiler_params=pltpu.CompilerParams(dimension_semantics=("arbitrary",),
                                             vmem_limit_bytes=VMEM_LIMIT),
        name="inproj",
    )(*ins)


RWKV_GROUP_HEADS = 4


def _block_diag(q, head_masks):
    zero = jnp.zeros((), q.dtype)
    return jnp.concatenate([jnp.where(m, q, zero) for m in head_masks], axis=0)


def _packed_dot1(lhs, q, head_masks, dims=_NN):
    return _dot(lhs.astype(BF16), _block_diag(q.astype(BF16), head_masks), dims)


def _packed_dot_wide_lhs(lhs, q, head_masks, dims=_NN):
    m = lhs.shape[0]
    lh, ll = _split2(lhs)
    both = _dot(jnp.concatenate([lh, ll], axis=0), _block_diag(q.astype(BF16), head_masks), dims)
    return both[:m] + both[m:]


def _rwkv_kernel(r_ref, k_ref, v_ref, lw_ref, kk_ref, b_ref, g_ref, rk_ref, gng_ref, gnb_ref, hsum_ref,
                 o_ref, state_ref, *, chunk):
    c = pl.program_id(1)
    n = HEAD_DIM
    rows, d = r_ref.shape[1], r_ref.shape[2]
    gw = RWKV_GROUP_HEADS * n
    n_groups = d // gw

    @pl.when(c == 0)
    def _():
        state_ref[...] = jnp.zeros_like(state_ref)

    t_idx = lax.broadcasted_iota(jnp.int32, (chunk, gw), 0)
    lane = lax.broadcasted_iota(jnp.int32, (chunk, gw), 1)
    i_idx = lane & (n - 1)
    strict = t_idx > i_idx
    incl = t_idx >= i_idx
    lane1 = lax.broadcasted_iota(jnp.int32, (1, gw), 1)
    hm = [(lane1 >> (n.bit_length() - 1)) == h for h in range(RWKV_GROUP_HEADS)]
    same = lambda s: (t_idx >> s) == (i_idx >> s)

    half_w = gw // 2
    lane_half = [(lane1 >> (half_w.bit_length() - 1)) == j for j in range(2)]
    second_in_half = ((lane >> (n.bit_length() - 1)) & 1) == 1

    def halves_rows(x):
        return jnp.concatenate([x[:, :half_w], x[:, half_w:]], axis=0).astype(BF16)

    def halves_masked(w):
        return _block_diag(w.astype(BF16), lane_half)

    def same_head(out):
        return jnp.where(second_in_half, out[n:], out[:n])

    lane_h = lax.broadcasted_iota(jnp.int32, (1, half_w), 1)
    in_half = [(lane_h >> (n.bit_length() - 1)) == j for j in range(half_w // n)]

    def two_rhs(lhs, x, y, dims=_NN):
        lb, xb, yb = lhs.astype(BF16), x.astype(BF16), y.astype(BF16)
        out_x, out_y = [], []
        for j in range(gw // half_w):
            cols = slice(j * half_w, (j + 1) * half_w)
            w = jnp.concatenate([_block_diag(xb[:, cols], in_half), _block_diag(yb[:, cols], in_half)],
                                axis=1 if dims == _NN else 0)
            o = _dot(lb[:, cols], w, dims)
            out_x.append(o[:, :half_w])
            out_y.append(o[:, half_w:])
        return jnp.concatenate(out_x, axis=1), jnp.concatenate(out_y, axis=1)

    sec_rows = rows // RWKV_SECTIONS
    sec_chunks = sec_rows // chunk
    rr = lax.broadcasted_iota(jnp.int32, (sec_rows, sec_rows), 0)
    cc = lax.broadcasted_iota(jnp.int32, (sec_rows, sec_rows), 1)
    cs = chunk.bit_length() - 1
    ltri = jnp.where((rr >= cc) & ((rr >> cs) == (cc >> cs)), 1.0, 0.0).astype(BF16)
    eye = jnp.where(t_idx == i_idx, 1.0, 0.0)
    chains = [(ci, gi) for ci in range(sec_chunks) for gi in range(n_groups)]
    win = lambda x, ci, gi: x[ci * chunk:(ci + 1) * chunk, gi * gw:(gi + 1) * gw]

    def section(row0):
        rs = slice(row0, row0 + sec_rows)
        r, k, v, lw = r_ref[0, rs, :], k_ref[0, rs, :], v_ref[0, rs, :], lw_ref[0, rs, :]
        kk, b = kk_ref[0, rs, :], b_ref[0, rs, :]
        cum = _dot_ones_lhs(ltri, lw)
        a_t = -kk * jnp.exp(cum - lw)
        r_t = r * jnp.exp(cum)
        e_neg = jnp.exp(-cum)
        b_t = b * e_neg
        k_t = k * e_neg
        tots = [cum[(ci + 1) * chunk - 1:(ci + 1) * chunk, :] for ci in range(sec_chunks)]
        e_bars = [jnp.exp(tots[ci] - cum[ci * chunk:(ci + 1) * chunk, :]) for ci in range(sec_chunks)]
        b_bar = [win(b, ci, gi) * e_bars[ci][:, gi * gw:(gi + 1) * gw] for ci, gi in chains]
        k_bar = [win(k, ci, gi) * e_bars[ci][:, gi * gw:(gi + 1) * gw] for ci, gi in chains]
        at = [win(a_t, ci, gi) for ci, gi in chains]
        rt = [win(r_t, ci, gi) for ci, gi in chains]
        vv = [win(v, ci, gi) for ci, gi in chains]
        yield

        ar = [jnp.concatenate([a_, r_], axis=0) for a_, r_ in zip(at, rt)]
        grams = [two_rhs(x, win(b_t, ci, gi), win(k_t, ci, gi), _NT) for x, (ci, gi) in zip(ar, chains)]
        gram_b = [x[0] for x in grams]
        gram_k = [x[1] for x in grams]
        yield
        m_ab = [jnp.where(strict, g_[:chunk], 0.0) for g_ in gram_b]
        p_rb = [jnp.where(incl, g_[chunk:], 0.0) for g_ in gram_b]
        m_ak = [jnp.where(strict, g_[:chunk], 0.0) for g_ in gram_k]
        p_rk = [jnp.where(incl, g_[chunk:], 0.0) for g_ in gram_k]
        yield

        ts = [eye + jnp.where(same(1), m, 0.0) for m in m_ab]
        s = 1
        while (1 << s) < chunk:
            level = same(s + 1) & jnp.logical_not(same(s))
            offs = [jnp.where(level, m, 0.0) for m in m_ab]
            mids = [_packed_dot1(t, o, hm) for t, o in zip(ts, offs)]
            yield
            ts = [t + _packed_dot1(md, t, hm) for t, md in zip(ts, mids)]
            yield
            s += 1

        both = [_packed_dot1(jnp.concatenate([m, pk], axis=0), x, hm) for m, pk, x in zip(m_ak, p_rk, vv)]
        mv = [x[:chunk] for x in both]
        pkv = [x[chunk:] for x in both]
        yield
        hat = [two_rhs(t, x, y) for t, x, y in zip(ts, at, mv)]
        a_hat = [x[0] for x in hat]
        u0 = [x[1] for x in hat]
        yield
        pb = [two_rhs(p, a, u) for p, a, u in zip(p_rb, a_hat, u0)]
        r_hat = [x + y[0] for x, y in zip(rt, pb)]
        y0 = [x + y[1] for x, y in zip(pkv, pb)]
        yield
        g_mat = [same_head(_dot(halves_rows(a), halves_masked(bb), _TN))
                 for a, bb in zip(a_hat, b_bar)]
        h_mat = [same_head(_dot(jnp.concatenate([halves_rows(u), halves_rows(x)], axis=0),
                                jnp.concatenate([halves_masked(bb), halves_masked(kb)], axis=0), _TN))
                 for u, x, bb, kb in zip(u0, vv, b_bar, k_bar)]
        yield

        y_rows = []
        for ci in range(sec_chunks):
            y_groups = []
            for gi in range(n_groups):
                j = ci * n_groups + gi
                s0 = state_ref[gi]
                y_groups.append(_packed_dot1(r_hat[j], s0, hm, _NT) + y0[j])
                w_tot = jnp.exp(tots[ci][:, gi * gw:(gi + 1) * gw])
                state_ref[gi] = s0 * w_tot + _packed_dot_wide_lhs(s0, g_mat[j], hm) + h_mat[j]
            y_rows.append(jnp.concatenate(y_groups, axis=1))
            yield
        y = jnp.concatenate(y_rows, axis=0)

        hsum = hsum_ref[...]
        inv_n = 1.0 / n
        mu = _head_sums(y, hsum) * inv_n
        yc = y - mu
        var = _head_sums(yc * yc, hsum) * inv_n
        yn = yc * lax.rsqrt(var + GN_EPS) * gng_ref[...] + gnb_ref[...]
        bonus = _head_sums(r * k * rk_ref[...], hsum)
        o_ref[0, rs, :] = ((yn + bonus * v) * g_ref[0, rs, :]).astype(o_ref.dtype)

    assert RWKV_SECTION_LAG >= sec_chunks
    _run_skewed([section(si * sec_rows) for si in range(RWKV_SECTIONS)], lag=RWKV_SECTION_LAG)


def _rwkv(r, k, v, lw, kk, b, g, rw_rk, gn_g, gn_b, hsum, *, chunk, rows):
    bsz, seq, d = r.shape
    n_groups = d // (RWKV_GROUP_HEADS * HEAD_DIM)
    blk = pl.BlockSpec((1, rows, d), lambda bi, ci: (bi, ci, 0))
    const = lambda bi, ci: (0, 0)
    vec = lambda a: a.reshape(1, d)
    return pl.pallas_call(
        functools.partial(_rwkv_kernel, chunk=chunk),
        grid=(bsz, seq // rows),
        in_specs=[blk] * 7 + [pl.BlockSpec((1, d), const)] * 3 + [pl.BlockSpec(hsum.shape, const)],
        out_specs=blk,
        out_shape=jax.ShapeDtypeStruct((bsz, seq, d), BF16),
        scratch_shapes=[pltpu.VMEM((n_groups, HEAD_DIM, RWKV_GROUP_HEADS * HEAD_DIM), F32)],
        compiler_params=pltpu.CompilerParams(dimension_semantics=("arbitrary", "arbitrary"),
                                             vmem_limit_bytes=VMEM_LIMIT),
        name="rwkv",
    )(r, k, v, lw, kk, b, g, vec(rw_rk), vec(gn_g), vec(gn_b), hsum)


SB_LOG_UNDERFLOW = 105.0


def _sb_kernel(q_ref, k_ref, v_ref, o_ref, acc_ref, run_ref, *, blk):
    qi = pl.program_id(1)
    pairs = k_ref.shape[2] // (2 * HEAD_DIM)
    row = lax.broadcasted_iota(jnp.int32, (blk, blk), 0)
    col = lax.broadcasted_iota(jnp.int32, (blk, blk), 1)
    causal = col < row
    jj = lax.broadcasted_iota(jnp.int32, (blk, 2 * blk), 0)
    ss = lax.broadcasted_iota(jnp.int32, (blk, 2 * blk), 1)
    suffix = jnp.where((jj >= ss) | (ss >= blk), 1.0, 0.0).astype(BF16)
    first_head = lax.broadcasted_iota(jnp.int32, (1, 2 * HEAD_DIM), 1) < HEAD_DIM
    zero = jnp.zeros((), BF16)

    lanes = [slice(p * 2 * HEAD_DIM, (p + 1) * 2 * HEAD_DIM) for p in range(pairs)]
    heads = [(p, hh) for p in range(pairs) for hh in range(2)]

    nh = len(heads)
    pw = 2 * HEAD_DIM

    def tiles(kbs, diagonal, absent=None):
        starts = [pl.multiple_of(kb * blk, blk) for kb in kbs]
        zs = [None] * (len(kbs) * nh)
        for p in range(pairs):
            q2 = jnp.concatenate([q_ref[0, :, (2 * p + hh) * pw:(2 * p + hh + 1) * pw] for hh in range(2)],
                                 axis=0)
            keys = jnp.concatenate([k_ref[0, pl.ds(start, blk), lanes[p]] for start in starts], axis=0)
            z2 = _dot(q2, keys, _NT)
            for t_ in range(len(kbs)):
                for hh in range(2):
                    zs[t_ * nh + 2 * p + hh] = z2[hh * blk:(hh + 1) * blk, t_ * blk:(t_ + 1) * blk]
        sps = [_softplus(z) for z in zs]
        if diagonal:
            sps = [jnp.where(causal, sp, 0.0) for sp in sps[:nh]] + sps[nh:]
        css = [None] * len(zs)
        for p in range(pairs):
            idx = [t_ * nh + 2 * p + hh for t_ in range(len(kbs)) for hh in range(2)]
            c_ = _dot(jnp.concatenate([sps[i_] for i_ in idx], axis=0).astype(BF16), suffix)
            for j, i_ in enumerate(idx):
                css[i_] = c_[j * blk:(j + 1) * blk]
        atts = [[None] * nh for _ in kbs]
        low = None
        for n_ in range(nh):
            run = None if diagonal else run_ref[n_]
            for t_ in range(len(kbs)):
                z, cs = zs[t_ * nh + n_], css[t_ * nh + n_]
                logw = z - cs[:, :blk]
                if run is None:
                    att = jnp.where(causal, jnp.exp(logw), 0.0)
                    run = cs[:, blk:]
                else:
                    att = jnp.exp(logw - run)
                    run = run + cs[:, blk:]
                if t_ == 0 and absent is not None:
                    run = run + absent
                atts[t_][n_] = att.astype(BF16)
            run_ref[n_] = run
            low = run if low is None else jnp.minimum(low, run)
        done = (jnp.min(low) > SB_LOG_UNDERFLOW).astype(jnp.int32)
        for p in range(pairs):
            vvs = []
            for start in starts:
                vp = v_ref[0, pl.ds(start, blk), lanes[p]]
                vvs += [jnp.where(first_head, vp, zero), jnp.where(first_head, zero, vp)]
            lhs = jnp.concatenate([atts[t_][2 * p + hh] for t_ in range(len(kbs)) for hh in range(2)], axis=1)
            out = _dot(lhs, jnp.concatenate(vvs, axis=0))
            if diagonal:
                acc_ref[:, lanes[p]] = out
            else:
                acc_ref[:, lanes[p]] += out
        return done

    def pair_of_tiles(i, diagonal):
        second = qi - i - 1
        absent = jnp.where(second < 0, 1e4, 0.0).astype(F32)
        return tiles([qi - i, jnp.maximum(second, 0)], diagonal, absent)

    done0 = pair_of_tiles(0, True)

    def cond(c):
        i, done = c
        return jnp.logical_and(i <= qi, done == 0)

    def body(c):
        i, _ = c
        return i + 2, pair_of_tiles(i, False)

    lax.while_loop(cond, body, (jnp.int32(2), done0))
    o_ref[0] = acc_ref[...].astype(o_ref.dtype)


def _sb_attention(q, k, v, *, blk):
    bsz, seq, d = k.shape
    heads = d // HEAD_DIM
    qspec = pl.BlockSpec((1, blk, 2 * d), lambda b, i: (b, i, 0))
    ospec = pl.BlockSpec((1, blk, d), lambda b, i: (b, i, 0))
    kvspec = pl.BlockSpec((1, seq, d), lambda b, i: (b, 0, 0), pipeline_mode=pl.Buffered(1))
    return pl.pallas_call(
        functools.partial(_sb_kernel, blk=blk),
        grid=(bsz, seq // blk),
        in_specs=[qspec, kvspec, kvspec],
        out_specs=ospec,
        out_shape=jax.ShapeDtypeStruct((bsz, seq, d), BF16),
        scratch_shapes=[pltpu.VMEM((blk, d), F32), pltpu.VMEM((heads, blk, blk), F32)],
        compiler_params=pltpu.CompilerParams(dimension_semantics=("arbitrary", "arbitrary"),
                                             vmem_limit_bytes=VMEM_LIMIT),
        name="sbattn",
    )(q, k, v)


def _merge_kernel(x_ref, ada_ref, ya_ref, ob_ref, ga_ref, gb_ref, wa_ref, wb_ref, wo_ref,
                  lng_ref, lnb_ref, o_ref, *, alpha):
    g1 = ada_ref[0, 2:3, :]
    half = x_ref.shape[0] // 2

    def rows_program(rows):
        y_a = _dot(ya_ref[rows, :], wa_ref[...])
        y_b = _dot(ob_ref[rows, :], wb_ref[...])
        yield
        merged = ga_ref[rows, :].astype(F32) * y_a + gb_ref[rows, :].astype(F32) * y_b
        mix = _dot(merged.astype(BF16), wo_ref[...])
        yield
        o_ref[rows, :] = _norm_rows(alpha * x_ref[rows, :] + g1 * mix, LN_EPS) * lng_ref[...] + lnb_ref[...]

    _run_skewed([rows_program(slice(0, half)), rows_program(slice(half, 2 * half))], lag=1)


def _merge(x2, ada3, ya, ob, ga, gb, wa_bf, wb_bf, wo_bf, ln_g, ln_b, *, seq, tm, alpha):
    m, d = x2.shape
    tiles_per_batch = seq // tm
    row = lambda i: (i, 0)
    const = lambda i: (0, 0)
    rows = lambda a: pl.BlockSpec((tm, a.shape[1]), row)
    full = lambda a: pl.BlockSpec(a.shape, const, pipeline_mode=pl.Buffered(1))
    vec = lambda a: a.reshape(1, -1)
    ins = [x2, ada3, ya, ob, ga, gb, wa_bf, wb_bf, wo_bf, vec(ln_g), vec(ln_b)]
    in_specs = [rows(x2), pl.BlockSpec((1,) + ada3.shape[1:], lambda i: (i // tiles_per_batch, 0, 0)),
                rows(ya), rows(ob), rows(ga), rows(gb)] + [full(a) for a in ins[6:]]
    return pl.pallas_call(
        functools.partial(_merge_kernel, alpha=alpha),
        grid=(m // tm,),
        in_specs=in_specs,
        out_specs=pl.BlockSpec((tm, d), row),
        out_shape=jax.ShapeDtypeStruct((m, d), F32),
        compiler_params=pltpu.CompilerParams(dimension_semantics=("arbitrary",),
                                             vmem_limit_bytes=VMEM_LIMIT),
        name="merge",
    )(*ins)


def _ffn_kernel(x_ref, ada_ref, w1_ref, b1_ref, w2_ref, b2_ref, lng_ref, lnb_ref, o_ref, *, alpha, fchunk):
    sh = ada_ref[0, 3:4, :]
    sc = ada_ref[0, 4:5, :]
    g2 = ada_ref[0, 5:6, :]
    d_ff = w1_ref.shape[1]
    n_chunks = d_ff // fchunk
    half = x_ref.shape[0] // 2

    def rows_program(rows):
        x = x_ref[rows, :]
        hb = (_norm_rows(x, ADALN_EPS) * (1.0 + sc) + sh).astype(BF16)
        yield
        ff = None
        for j in range(n_chunks):
            sl = slice(j * fchunk, (j + 1) * fchunk)
            t = jnp.maximum(_dot(hb, w1_ref[:, sl]) + b1_ref[:, sl], 0.0)
            part = _dot((t * t).astype(BF16), w2_ref[sl, :])
            ff = part if ff is None else ff + part
            yield
        ff = ff + b2_ref[...]
        o_ref[rows, :] = _norm_rows(alpha * x + g2 * ff, LN_EPS) * lng_ref[...] + lnb_ref[...]

    _run_skewed([rows_program(slice(0, half)), rows_program(slice(half, 2 * half))], lag=2)


def _ffn(x1, ada3, w1_bf, b1, w2_bf, b2, ln_g, ln_b, *, seq, tm, alpha, fchunk):
    m, d = x1.shape
    tiles_per_batch = seq // tm
    row = lambda i: (i, 0)
    const = lambda i: (0, 0)
    full = lambda a: pl.BlockSpec(a.shape, const, pipeline_mode=pl.Buffered(1))
    vec = lambda a: a.reshape(1, -1)
    ins = [x1, ada3, w1_bf, vec(b1), w2_bf, vec(b2), vec(ln_g), vec(ln_b)]
    in_specs = [pl.BlockSpec((tm, d), row),
                pl.BlockSpec((1,) + ada3.shape[1:], lambda i: (i // tiles_per_batch, 0, 0))]
    in_specs += [full(a) for a in ins[2:]]
    return pl.pallas_call(
        functools.partial(_ffn_kernel, alpha=alpha, fchunk=fchunk),
        grid=(m // tm,),
        in_specs=in_specs,
        out_specs=pl.BlockSpec((tm, d), row),
        out_shape=jax.ShapeDtypeStruct((m, d), F32),
        compiler_params=pltpu.CompilerParams(dimension_semantics=("arbitrary",),
                                             vmem_limit_bytes=VMEM_LIMIT),
        name="ffn",
    )(*ins)


def _branches(proj, rw_rk, rw_gn_g, rw_gn_b, hsum, bsz, seq):
    r, k, v, lw, kk, b, g, q, ks, vs = proj
    seq3 = lambda a: a.reshape(bsz, seq, -1)
    ya = _rwkv(seq3(r), seq3(k), seq3(v), seq3(lw), seq3(kk), seq3(b), seq3(g),
               rw_rk, rw_gn_g, rw_gn_b, hsum, chunk=min(RWKV_CHUNK, seq),
               rows=min(RWKV_CHUNKS_PER_STEP * RWKV_CHUNK, seq))
    o = _sb_attention(seq3(q), seq3(ks), seq3(vs), blk=min(SB_BLOCK, seq))
    return ya.reshape(bsz * seq, -1), o.reshape(bsz * seq, -1)


def _layer(x, c, w_ada, b_ada, w_in, mu_shift, rw_w0, rw_w2, rw_a0, rw_a2, rw_g2, rw_kk, rw_ka,
           rw_rk, rw_gn_g, rw_gn_b, w_branch_a, w_branch_b, w_out, ln1_g, ln1_b,
           w_ff1, b_ff1, w_ff2, b_ff2, ln2_g, ln2_b, *, alpha):
    bsz, seq, d = x.shape
    d_rwkv = rw_w0.shape[-1]
    tm = min(INPROJ_ROWS, seq)
    tm2 = min(MLP_ROWS, seq)
    rwkv_rows = min(RWKV_CHUNKS_PER_STEP * RWKV_CHUNK, seq)
    group_w = RWKV_GROUP_HEADS * HEAD_DIM
    assert seq % tm == 0 and seq % tm2 == 0 and seq % rwkv_rows == 0 and seq % min(SB_BLOCK, seq) == 0, seq
    assert rwkv_rows % (RWKV_SECTIONS * RWKV_CHUNK) == 0, (rwkv_rows, RWKV_SECTIONS)
    assert d_rwkv % group_w == 0 and w_branch_b.shape[0] % (2 * HEAD_DIM) == 0, (d_rwkv, w_branch_b.shape)
    assert HEAD_DIM & (HEAD_DIM - 1) == 0 and RWKV_CHUNK & (RWKV_CHUNK - 1) == 0

    ada3 = _ada(c, w_ada, b_ada).reshape(bsz, 6, d)
    lane = jnp.arange(RWKV_GROUP_HEADS * HEAD_DIM) // HEAD_DIM
    hsum = (lane[:, None] == lane[None, :]).astype(BF16)

    x2 = x.reshape(bsz * seq, d)
    (r, k, v, lw, kk, b, g, q, ks, vs, ga, gb) = _inproj(
        x2, ada3, w_in.astype(BF16), mu_shift, rw_w0, rw_w2, rw_a0, rw_a2, rw_g2, rw_kk, rw_ka, hsum,
        seq=seq, tm=tm)

    ya, ob = _branches((r, k, v, lw, kk, b, g, q, ks, vs), rw_rk, rw_gn_g, rw_gn_b, hsum, bsz, seq)

    x1 = _merge(x2, ada3, ya, ob, ga, gb,
                w_branch_a.astype(BF16), w_branch_b.astype(BF16), w_out.astype(BF16),
                ln1_g, ln1_b, seq=seq, tm=tm2, alpha=alpha)
    out = _ffn(x1, ada3, w_ff1.astype(BF16), b_ff1, w_ff2.astype(BF16), b_ff2, ln2_g, ln2_b,
               seq=seq, tm=tm2, alpha=alpha, fchunk=min(1024, w_ff1.shape[-1]))
    return out.reshape(bsz, seq, d)


def kernel(x, c, w_ada, b_ada, w_in, mu_shift, rw_w0, rw_w2, rw_a0, rw_a2, rw_g2, rw_kk, rw_ka, rw_rk,
           rw_gn_g, rw_gn_b, w_branch_a, w_branch_b, w_out, ln1_g, ln1_b, w_ff1, b_ff1, w_ff2, b_ff2,
           ln2_g, ln2_b):
    in_dtype = x.dtype
    layer_params = (w_ada, b_ada, w_in, mu_shift, rw_w0, rw_w2, rw_a0, rw_a2, rw_g2, rw_kk, rw_ka,
                    rw_rk, rw_gn_g, rw_gn_b, w_branch_a, w_branch_b, w_out, ln1_g, ln1_b,
                    w_ff1, b_ff1, w_ff2, b_ff2, ln2_g, ln2_b)
    depth = w_ada.shape[0]
    alpha = (2.0 * depth) ** 0.25
    for l in range(depth):
        x = _layer(x, c, *[p[l] for p in layer_params], alpha=alpha)
    return x.astype(in_dtype)
```

```python
import functools

import jax
import jax.numpy as jnp
from jax import lax
from jax.experimental import pallas as pl
from jax.experimental.pallas import tpu as pltpu

F32 = jnp.float32
BF16 = jnp.bfloat16

HEAD_DIM = 64
LN_EPS = 1e-5
ADALN_EPS = 1e-6
GN_EPS = 64e-5
RWKV_CHUNK = 64
RWKV_SECTIONS = 2
RWKV_SECTION_CHUNKS = 4
RWKV_SECTION_LAG = 12
RWKV_CHUNKS_PER_STEP = RWKV_SECTIONS * RWKV_SECTION_CHUNKS
SB_BLOCK = 256
SB_TILES_PER_STEP = 2
INPROJ_ROWS = 512
MLP_ROWS = 1024
VMEM_LIMIT = 56 * 1024 * 1024


def _split2(a):
    hi = a.astype(BF16)
    lo = (a - hi.astype(F32)).astype(BF16)
    return hi, lo


_NN = (((1,), (0,)), ((), ()))
_NT = (((1,), (1,)), ((), ()))
_TN = (((0,), (0,)), ((), ()))


def _dot(a, b, dims=_NN):
    return lax.dot_general(a, b, dims, preferred_element_type=F32)


def _head_sums(x, ones_blk):
    m, w = x.shape[0], ones_blk.shape[0]
    hi, lo = _split2(x)
    outs = []
    for g in range(x.shape[1] // w):
        cols = slice(g * w, (g + 1) * w)
        both = _dot(jnp.concatenate([hi[:, cols], lo[:, cols]], axis=0), ones_blk)
        outs.append(both[:m] + both[m:])
    return jnp.concatenate(outs, axis=1)


def _dot_ones_lhs(a_bf, b):
    hi, lo = _split2(b)
    return _dot(a_bf, hi) + _dot(a_bf, lo)


_DONE = object()


def _run_skewed(programs, lag):
    live = [True] * len(programs)
    tick = 0
    while any(live):
        for i, prog in enumerate(programs):
            if live[i] and tick >= i * lag:
                live[i] = next(prog, _DONE) is not _DONE
        tick += 1


_NEG_LOG2_E = -1.4426950408889634


def _softplus(y):
    return jnp.maximum(y, 0.0) + jnp.log(1.0 + jnp.exp2(jnp.abs(y) * _NEG_LOG2_E))


def _sigmoid(y):
    return 1.0 / (1.0 + jnp.exp(-y))


def _norm_rows(x, eps):
    mu = jnp.mean(x, axis=-1, keepdims=True)
    xc = x - mu
    var = jnp.mean(xc * xc, axis=-1, keepdims=True)
    return xc * lax.rsqrt(var + eps)


def _ada_kernel(c_ref, w_ref, b_ref, o_ref):
    c = c_ref[...]
    s = c * _sigmoid(c)
    o_ref[...] = jnp.dot(s, w_ref[...], preferred_element_type=F32,
                         precision=lax.Precision.HIGHEST) + b_ref[...]


def _ada(c, w_ada, b_ada):
    bsz, d = c.shape
    n = w_ada.shape[1]
    return pl.pallas_call(
        _ada_kernel,
        grid=(n // d,),
        in_specs=[pl.BlockSpec((bsz, d), lambda j: (0, 0)),
                  pl.BlockSpec((d, d), lambda j: (0, j)),
                  pl.BlockSpec((1, d), lambda j: (0, j))],
        out_specs=pl.BlockSpec((bsz, d), lambda j: (0, j)),
        out_shape=jax.ShapeDtypeStruct((bsz, n), F32),
        name="ada",
    )(c, w_ada, b_ada.reshape(1, n))


def _inproj_kernel(x_ref, ada_ref, win_ref, mu_ref, w0_ref, w2_ref, a0_ref, a2_ref, g2_ref,
                   kkw_ref, kaw_ref, hsum_ref,
                   r_ref, k_ref, v_ref, lw_ref, kk_ref, b_ref, g_ref,
                   q_ref, ks_ref, vs_ref, ga_ref, gb_ref,
                   carry_ref, *, tiles_per_batch, d_rwkv, d_decay, d_aaa, d_gate, d_sb, d_model):
    i = pl.program_id(0)
    tm = x_ref.shape[0]
    d_shift = 3 * d_rwkv + d_decay + d_aaa + d_gate

    sh = ada_ref[0, 0:1, :]
    sc = ada_ref[0, 1:2, :]

    @pl.when(i % tiles_per_batch == 0)
    def _():
        carry_ref[...] = jnp.zeros_like(carry_ref)

    half = d_model // 2
    pair_w = 2 * HEAD_DIM
    first_head = lax.broadcasted_iota(jnp.int32, (1, pair_w), 1) < HEAD_DIM

    def rows_program(rows):
        hm = rows.stop - rows.start
        hb = (_norm_rows(x_ref[rows, :], ADALN_EPS) * (1.0 + sc) + sh).astype(BF16)
        row0 = lax.broadcasted_iota(jnp.int32, (hm, 1), 0) == 0
        yield

        def shifted(lo, width):
            cols = slice(lo, lo + width)
            z = _dot(hb, win_ref[:, cols])
            prev = jnp.where(row0, carry_ref[:, cols], pltpu.roll(z, 1, 0))
            carry_ref[:, cols] = z[hm - 1:hm, :]
            return z + mu_ref[:, cols] * (prev - z)

        def store_queries(z):
            zq = (z * (HEAD_DIM ** -0.5)).astype(BF16)
            zero = jnp.zeros((), BF16)
            for p in range(d_sb // pair_w):
                qp = zq[:, p * pair_w:(p + 1) * pair_w]
                q_ref[rows, (2 * p) * pair_w:(2 * p + 1) * pair_w] = jnp.where(first_head, qp, zero)
                q_ref[rows, (2 * p + 1) * pair_w:(2 * p + 2) * pair_w] = jnp.where(first_head, zero, qp)

        def plain(ref, cols, fn=lambda z: z):
            def store(z):
                ref[rows, cols] = fn(z).astype(BF16)
            return store

        all_cols = slice(0, d_sb)
        rest = [(store_queries, d_sb), (plain(ks_ref, all_cols), d_sb), (plain(vs_ref, all_cols), d_sb),
                (plain(ga_ref, slice(0, half), _sigmoid), half),
                (plain(ga_ref, slice(half, d_model), _sigmoid), half),
                (plain(gb_ref, slice(0, half), _sigmoid), half),
                (plain(gb_ref, slice(half, d_model), _sigmoid), half)]
        rest_col = [d_shift]

        def project_next():
            store, width = rest.pop(0)
            store(_dot(hb, win_ref[:, rest_col[0]:rest_col[0] + width]))
            rest_col[0] += width

        zl = shifted(3 * d_rwkv, d_decay + d_aaa + d_gate)
        zw = zl[:, :d_decay]
        za = zl[:, d_decay:d_decay + d_aaa]
        zg = zl[:, d_decay + d_aaa:]
        k = shifted(d_rwkv, d_rwkv)
        yield
        ww = w0_ref[...] + _dot(jnp.tanh(zw).astype(BF16), w2_ref[...])
        w_log = -_softplus(-ww) - 0.5
        lw_ref[rows, :] = -jnp.exp(w_log)
        project_next()
        yield
        arate = _sigmoid(a0_ref[...] + _dot(za.astype(BF16), a2_ref[...]))
        g_ref[rows, :] = _dot(_sigmoid(zg).astype(BF16), g2_ref[...])
        r_ref[rows, :] = shifted(0, d_rwkv)
        yield
        kkraw = k * kkw_ref[...]
        ssq = _head_sums(kkraw * kkraw, hsum_ref[...])
        kk = kkraw / jnp.maximum(jnp.sqrt(ssq), 1e-12)
        v_ref[rows, :] = shifted(2 * d_rwkv, d_rwkv)
        k_ref[rows, :] = k * (1.0 + (arate - 1.0) * kaw_ref[...])
        yield
        project_next()
        kk_ref[rows, :] = kk
        b_ref[rows, :] = kk * arate
        yield
        while rest:
            project_next()
            yield

    hm = tm // 2
    _run_skewed([rows_program(slice(0, hm)), rows_program(slice(hm, tm))], lag=2)


def _inproj(x2, ada3, win_bf, mu_shift, rw_w0, rw_w2, rw_a0, rw_a2, rw_g2, rw_kk, rw_ka, hsum,
            *, seq, tm):
    m, d_model = x2.shape
    d_rwkv = rw_w0.shape[-1]
    d_decay, d_aaa, d_gate = rw_w2.shape[0], rw_a2.shape[0], rw_g2.shape[0]
    d_shift = 3 * d_rwkv + d_decay + d_aaa + d_gate
    d_sb = (win_bf.shape[1] - d_shift - 2 * d_model) // 3
    tiles_per_batch = seq // tm
    row = lambda i: (i, 0)
    const = lambda i: (0, 0)
    full = lambda a: pl.BlockSpec(a.shape, const, pipeline_mode=pl.Buffered(1))
    vec = lambda a: a.reshape(1, -1)
    ins = [x2, ada3, win_bf, vec(mu_shift), vec(rw_w0), rw_w2.astype(BF16), vec(rw_a0),
           rw_a2.astype(BF16), rw_g2.astype(BF16), vec(rw_kk), vec(rw_ka), hsum]
    in_specs = [pl.BlockSpec((tm, d_model), row),
                pl.BlockSpec((1,) + ada3.shape[1:], lambda i: (i // tiles_per_batch, 0, 0))]
    in_specs += [full(a) for a in ins[2:]]
    widths = [d_rwkv] * 7 + [2 * d_sb, d_sb, d_sb] + [d_model] * 2
    dtypes = [F32] * 7 + [BF16] * 5
    out_shape = [jax.ShapeDtypeStruct((m, w), dt) for w, dt in zip(widths, dtypes)]
    out_specs = [pl.BlockSpec((tm, w), row) for w in widths]
    kern = functools.partial(_inproj_kernel, tiles_per_batch=tiles_per_batch, d_rwkv=d_rwkv,
                             d_decay=d_decay, d_aaa=d_aaa, d_gate=d_gate, d_sb=d_sb, d_model=d_model)
    return pl.pallas_call(
        kern,
        grid=(m // tm,),
        in_specs=in_specs,
        out_specs=out_specs,
        out_shape=out_shape,
        scratch_shapes=[pltpu.VMEM((1, d_shift), F32)],
        compiler_params=pltpu.CompilerParams(dimension_semantics=("arbitrary",),
                                             vmem_limit_bytes=VMEM_LIMIT),
        name="inproj",
    )(*ins)


RWKV_GROUP_HEADS = 4


def _block_diag(q, head_masks):
    zero = jnp.zeros((), q.dtype)
    return jnp.concatenate([jnp.where(m, q, zero) for m in head_masks], axis=0)


def _packed_dot1(lhs, q, head_masks, dims=_NN):
    return _dot(lhs.astype(BF16), _block_diag(q.astype(BF16), head_masks), dims)


def _packed_dot_wide_lhs(lhs, q, head_masks, dims=_NN):
    m = lhs.shape[0]
    lh, ll = _split2(lhs)
    both = _dot(jnp.concatenate([lh, ll], axis=0), _block_diag(q.astype(BF16), head_masks), dims)
    return both[:m] + both[m:]


def _rwkv_kernel(r_ref, k_ref, v_ref, lw_ref, kk_ref, b_ref, g_ref, rk_ref, gng_ref, gnb_ref, hsum_ref,
                 o_ref, state_ref, *, chunk):
    c = pl.program_id(1)
    n = HEAD_DIM
    rows, d = r_ref.shape[1], r_ref.shape[2]
    gw = RWKV_GROUP_HEADS * n
    n_groups = d // gw

    @pl.when(c == 0)
    def _():
        state_ref[...] = jnp.zeros_like(state_ref)

    t_idx = lax.broadcasted_iota(jnp.int32, (chunk, gw), 0)
    lane = lax.broadcasted_iota(jnp.int32, (chunk, gw), 1)
    i_idx = lane & (n - 1)
    strict = t_idx > i_idx
    incl = t_idx >= i_idx
    lane1 = lax.broadcasted_iota(jnp.int32, (1, gw), 1)
    hm = [(lane1 >> (n.bit_length() - 1)) == h for h in range(RWKV_GROUP_HEADS)]
    same = lambda s: (t_idx >> s) == (i_idx >> s)

    half_w = gw // 2
    lane_half = [(lane1 >> (half_w.bit_length() - 1)) == j for j in range(2)]
    second_in_half = ((lane >> (n.bit_length() - 1)) & 1) == 1

    def halves_rows(x):
        return jnp.concatenate([x[:, :half_w], x[:, half_w:]], axis=0).astype(BF16)

    def halves_masked(w):
        return _block_diag(w.astype(BF16), lane_half)

    def same_head(out):
        return jnp.where(second_in_half, out[n:], out[:n])

    lane_h = lax.broadcasted_iota(jnp.int32, (1, half_w), 1)
    in_half = [(lane_h >> (n.bit_length() - 1)) == j for j in range(half_w // n)]

    def two_rhs(lhs, x, y, dims=_NN):
        lb, xb, yb = lhs.astype(BF16), x.astype(BF16), y.astype(BF16)
        out_x, out_y = [], []
        for j in range(gw // half_w):
            cols = slice(j * half_w, (j + 1) * half_w)
            w = jnp.concatenate([_block_diag(xb[:, cols], in_half), _block_diag(yb[:, cols], in_half)],
                                axis=1 if dims == _NN else 0)
            o = _dot(lb[:, cols], w, dims)
            out_x.append(o[:, :half_w])
            out_y.append(o[:, half_w:])
        return jnp.concatenate(out_x, axis=1), jnp.concatenate(out_y, axis=1)

    sec_rows = rows // RWKV_SECTIONS
    sec_chunks = sec_rows // chunk
    rr = lax.broadcasted_iota(jnp.int32, (sec_rows, sec_rows), 0)
    cc = lax.broadcasted_iota(jnp.int32, (sec_rows, sec_rows), 1)
    cs = chunk.bit_length() - 1
    ltri = jnp.where((rr >= cc) & ((rr >> cs) == (cc >> cs)), 1.0, 0.0).astype(BF16)
    eye = jnp.where(t_idx == i_idx, 1.0, 0.0)
    chains = [(ci, gi) for ci in range(sec_chunks) for gi in range(n_groups)]
    win = lambda x, ci, gi: x[ci * chunk:(ci + 1) * chunk, gi * gw:(gi + 1) * gw]

    def section(row0):
        rs = slice(row0, row0 + sec_rows)
        r, k, v, lw = r_ref[0, rs, :], k_ref[0, rs, :], v_ref[0, rs, :], lw_ref[0, rs, :]
        kk, b = kk_ref[0, rs, :], b_ref[0, rs, :]
        cum = _dot_ones_lhs(ltri, lw)
        a_t = -kk * jnp.exp(cum - lw)
        r_t = r * jnp.exp(cum)
        e_neg = jnp.exp(-cum)
        b_t = b * e_neg
        k_t = k * e_neg
        tots = [cum[(ci + 1) * chunk - 1:(ci + 1) * chunk, :] for ci in range(sec_chunks)]
        e_bars = [jnp.exp(tots[ci] - cum[ci * chunk:(ci + 1) * chunk, :]) for ci in range(sec_chunks)]
        b_bar = [win(b, ci, gi) * e_bars[ci][:, gi * gw:(gi + 1) * gw] for ci, gi in chains]
        k_bar = [win(k, ci, gi) * e_bars[ci][:, gi * gw:(gi + 1) * gw] for ci, gi in chains]
        at = [win(a_t, ci, gi) for ci, gi in chains]
        rt = [win(r_t, ci, gi) for ci, gi in chains]
        vv = [win(v, ci, gi) for ci, gi in chains]
        yield

        ar = [jnp.concatenate([a_, r_], axis=0) for a_, r_ in zip(at, rt)]
        grams = [two_rhs(x, win(b_t, ci, gi), win(k_t, ci, gi), _NT) for x, (ci, gi) in zip(ar, chains)]
        gram_b = [x[0] for x in grams]
        gram_k = [x[1] for x in grams]
        yield
        m_ab = [jnp.where(strict, g_[:chunk], 0.0) for g_ in gram_b]
        p_rb = [jnp.where(incl, g_[chunk:], 0.0) for g_ in gram_b]
        m_ak = [jnp.where(strict, g_[:chunk], 0.0) for g_ in gram_k]
        p_rk = [jnp.where(incl, g_[chunk:], 0.0) for g_ in gram_k]
        yield

        ts = [eye + jnp.where(same(1), m, 0.0) for m in m_ab]
        s = 1
        while (1 << s) < chunk:
            level = same(s + 1) & jnp.logical_not(same(s))
            offs = [jnp.where(level, m, 0.0) for m in m_ab]
            mids = [_packed_dot1(t, o, hm) for t, o in zip(ts, offs)]
            yield
            ts = [t + _packed_dot1(md, t, hm) for t, md in zip(ts, mids)]
            yield
            s += 1

        both = [_packed_dot1(jnp.concatenate([m, pk], axis=0), x, hm) for m, pk, x in zip(m_ak, p_rk, vv)]
        mv = [x[:chunk] for x in both]
        pkv = [x[chunk:] for x in both]
        yield
        hat = [two_rhs(t, x, y) for t, x, y in zip(ts, at, mv)]
        a_hat = [x[0] for x in hat]
        u0 = [x[1] for x in hat]
        yield
        pb = [two_rhs(p, a, u) for p, a, u in zip(p_rb, a_hat, u0)]
        r_hat = [x + y[0] for x, y in zip(rt, pb)]
        y0 = [x + y[1] for x, y in zip(pkv, pb)]
        yield
        g_mat = [same_head(_dot(halves_rows(a), halves_masked(bb), _TN))
                 for a, bb in zip(a_hat, b_bar)]
        h_mat = [same_head(_dot(jnp.concatenate([halves_rows(u), halves_rows(x)], axis=0),
                                jnp.concatenate([halves_masked(bb), halves_masked(kb)], axis=0), _TN))
                 for u, x, bb, kb in zip(u0, vv, b_bar, k_bar)]
        yield

        y_rows = []
        for ci in range(sec_chunks):
            y_groups = []
            for gi in range(n_groups):
                j = ci * n_groups + gi
                s0 = state_ref[gi]
                y_groups.append(_packed_dot1(r_hat[j], s0, hm, _NT) + y0[j])
                w_tot = jnp.exp(tots[ci][:, gi * gw:(gi + 1) * gw])
                state_ref[gi] = s0 * w_tot + _packed_dot_wide_lhs(s0, g_mat[j], hm) + h_mat[j]
            y_rows.append(jnp.concatenate(y_groups, axis=1))
            yield
        y = jnp.concatenate(y_rows, axis=0)

        hsum = hsum_ref[...]
        inv_n = 1.0 / n
        mu = _head_sums(y, hsum) * inv_n
        yc = y - mu
        var = _head_sums(yc * yc, hsum) * inv_n
        yn = yc * lax.rsqrt(var + GN_EPS) * gng_ref[...] + gnb_ref[...]
        bonus = _head_sums(r * k * rk_ref[...], hsum)
        o_ref[0, rs, :] = ((yn + bonus * v) * g_ref[0, rs, :]).astype(o_ref.dtype)

    assert RWKV_SECTION_LAG >= sec_chunks
    _run_skewed([section(si * sec_rows) for si in range(RWKV_SECTIONS)], lag=RWKV_SECTION_LAG)


def _rwkv(r, k, v, lw, kk, b, g, rw_rk, gn_g, gn_b, hsum, *, chunk, rows):
    bsz, seq, d = r.shape
    n_groups = d // (RWKV_GROUP_HEADS * HEAD_DIM)
    blk = pl.BlockSpec((1, rows, d), lambda bi, ci: (bi, ci, 0))
    const = lambda bi, ci: (0, 0)
    vec = lambda a: a.reshape(1, d)
    return pl.pallas_call(
        functools.partial(_rwkv_kernel, chunk=chunk),
        grid=(bsz, seq // rows),
        in_specs=[blk] * 7 + [pl.BlockSpec((1, d), const)] * 3 + [pl.BlockSpec(hsum.shape, const)],
        out_specs=blk,
        out_shape=jax.ShapeDtypeStruct((bsz, seq, d), BF16),
        scratch_shapes=[pltpu.VMEM((n_groups, HEAD_DIM, RWKV_GROUP_HEADS * HEAD_DIM), F32)],
        compiler_params=pltpu.CompilerParams(dimension_semantics=("arbitrary", "arbitrary"),
                                             vmem_limit_bytes=VMEM_LIMIT),
        name="rwkv",
    )(r, k, v, lw, kk, b, g, vec(rw_rk), vec(gn_g), vec(gn_b), hsum)


SB_LOG_UNDERFLOW = 105.0


def _sb_kernel(q_ref, k_ref, v_ref, o_ref, acc_ref, run_ref, *, blk):
    qi = pl.program_id(1)
    pairs = k_ref.shape[2] // (2 * HEAD_DIM)
    row = lax.broadcasted_iota(jnp.int32, (blk, blk), 0)
    col = lax.broadcasted_iota(jnp.int32, (blk, blk), 1)
    causal = col < row
    jj = lax.broadcasted_iota(jnp.int32, (blk, 2 * blk), 0)
    ss = lax.broadcasted_iota(jnp.int32, (blk, 2 * blk), 1)
    suffix = jnp.where((jj >= ss) | (ss >= blk), 1.0, 0.0).astype(BF16)
    first_head = lax.broadcasted_iota(jnp.int32, (1, 2 * HEAD_DIM), 1) < HEAD_DIM
    zero = jnp.zeros((), BF16)

    lanes = [slice(p * 2 * HEAD_DIM, (p + 1) * 2 * HEAD_DIM) for p in range(pairs)]
    heads = [(p, hh) for p in range(pairs) for hh in range(2)]

    nh = len(heads)
    pw = 2 * HEAD_DIM

    def tiles(kbs, diagonal, absent=None):
        starts = [pl.multiple_of(kb * blk, blk) for kb in kbs]
        zs = [None] * (len(kbs) * nh)
        for p in range(pairs):
            q2 = jnp.concatenate([q_ref[0, :, (2 * p + hh) * pw:(2 * p + hh + 1) * pw] for hh in range(2)],
                                 axis=0)
            keys = jnp.concatenate([k_ref[0, pl.ds(start, blk), lanes[p]] for start in starts], axis=0)
            z2 = _dot(q2, keys, _NT)
            for t_ in range(len(kbs)):
                for hh in range(2):
                    zs[t_ * nh + 2 * p + hh] = z2[hh * blk:(hh + 1) * blk, t_ * blk:(t_ + 1) * blk]
        sps = [_softplus(z) for z in zs]
        if diagonal:
            sps = [jnp.where(causal, sp, 0.0) for sp in sps[:nh]] + sps[nh:]
        css = [None] * len(zs)
        for p in range(pairs):
            idx = [t_ * nh + 2 * p + hh for t_ in range(len(kbs)) for hh in range(2)]
            c_ = _dot(jnp.concatenate([sps[i_] for i_ in idx], axis=0).astype(BF16), suffix)
            for j, i_ in enumerate(idx):
                css[i_] = c_[j * blk:(j + 1) * blk]
        atts = [[None] * nh for _ in kbs]
        low = None
        for n_ in range(nh):
            run = None if diagonal else run_ref[n_]
            for t_ in range(len(kbs)):
                z, cs = zs[t_ * nh + n_], css[t_ * nh + n_]
                logw = z - cs[:, :blk]
                if run is None:
                    att = jnp.where(causal, jnp.exp(logw), 0.0)
                    run = cs[:, blk:]
                else:
                    att = jnp.exp(logw - run)
                    run = run + cs[:, blk:]
                if t_ == 0 and absent is not None:
                    run = run + absent
                atts[t_][n_] = att.astype(BF16)
            run_ref[n_] = run
            low = run if low is None else jnp.minimum(low, run)
        done = (jnp.min(low) > SB_LOG_UNDERFLOW).astype(jnp.int32)
        for p in range(pairs):
            vvs = []
            for start in starts:
                vp = v_ref[0, pl.ds(start, blk), lanes[p]]
                vvs += [jnp.where(first_head, vp, zero), jnp.where(first_head, zero, vp)]
            lhs = jnp.concatenate([atts[t_][2 * p + hh] for t_ in range(len(kbs)) for hh in range(2)], axis=1)
            out = _dot(lhs, jnp.concatenate(vvs, axis=0))
            if diagonal:
                acc_ref[:, lanes[p]] = out
            else:
                acc_ref[:, lanes[p]] += out
        return done

    def pair_of_tiles(i, diagonal):
        if SB_TILES_PER_STEP == 1:
            return tiles([qi - i], diagonal)
        second = qi - i - 1
        absent = jnp.where(second < 0, 1e4, 0.0).astype(F32)
        return tiles([qi - i, jnp.maximum(second, 0)], diagonal, absent)

    done0 = pair_of_tiles(0, True)

    def cond(c):
        i, done = c
        return jnp.logical_and(i <= qi, done == 0)

    def body(c):
        i, _ = c
        return i + SB_TILES_PER_STEP, pair_of_tiles(i, False)

    lax.while_loop(cond, body, (jnp.int32(SB_TILES_PER_STEP), done0))
    o_ref[0] = acc_ref[...].astype(o_ref.dtype)


def _sb_attention(q, k, v, *, blk):
    bsz, seq, d = k.shape
    heads = d // HEAD_DIM
    qspec = pl.BlockSpec((1, blk, 2 * d), lambda b, i: (b, i, 0))
    ospec = pl.BlockSpec((1, blk, d), lambda b, i: (b, i, 0))
    kvspec = pl.BlockSpec((1, seq, d), lambda b, i: (b, 0, 0), pipeline_mode=pl.Buffered(1))
    return pl.pallas_call(
        functools.partial(_sb_kernel, blk=blk),
        grid=(bsz, seq // blk),
        in_specs=[qspec, kvspec, kvspec],
        out_specs=ospec,
        out_shape=jax.ShapeDtypeStruct((bsz, seq, d), BF16),
        scratch_shapes=[pltpu.VMEM((blk, d), F32), pltpu.VMEM((heads, blk, blk), F32)],
        compiler_params=pltpu.CompilerParams(dimension_semantics=("arbitrary", "arbitrary"),
                                             vmem_limit_bytes=VMEM_LIMIT),
        name="sbattn",
    )(q, k, v)


def _merge_kernel(x_ref, ada_ref, ya_ref, ob_ref, ga_ref, gb_ref, wa_ref, wb_ref, wo_ref,
                  lng_ref, lnb_ref, o_ref, *, alpha):
    g1 = ada_ref[0, 2:3, :]
    half = x_ref.shape[0] // 2

    def rows_program(rows):
        y_a = _dot(ya_ref[rows, :], wa_ref[...])
        y_b = _dot(ob_ref[rows, :], wb_ref[...])
        yield
        merged = ga_ref[rows, :].astype(F32) * y_a + gb_ref[rows, :].astype(F32) * y_b
        mix = _dot(merged.astype(BF16), wo_ref[...])
        yield
        o_ref[rows, :] = _norm_rows(alpha * x_ref[rows, :] + g1 * mix, LN_EPS) * lng_ref[...] + lnb_ref[...]

    _run_skewed([rows_program(slice(0, half)), rows_program(slice(half, 2 * half))], lag=1)


def _merge(x2, ada3, ya, ob, ga, gb, wa_bf, wb_bf, wo_bf, ln_g, ln_b, *, seq, tm, alpha):
    m, d = x2.shape
    tiles_per_batch = seq // tm
    row = lambda i: (i, 0)
    const = lambda i: (0, 0)
    rows = lambda a: pl.BlockSpec((tm, a.shape[1]), row)
    full = lambda a: pl.BlockSpec(a.shape, const, pipeline_mode=pl.Buffered(1))
    vec = lambda a: a.reshape(1, -1)
    ins = [x2, ada3, ya, ob, ga, gb, wa_bf, wb_bf, wo_bf, vec(ln_g), vec(ln_b)]
    in_specs = [rows(x2), pl.BlockSpec((1,) + ada3.shape[1:], lambda i: (i // tiles_per_batch, 0, 0)),
                rows(ya), rows(ob), rows(ga), rows(gb)] + [full(a) for a in ins[6:]]
    return pl.pallas_call(
        functools.partial(_merge_kernel, alpha=alpha),
        grid=(m // tm,),
        in_specs=in_specs,
        out_specs=pl.BlockSpec((tm, d), row),
        out_shape=jax.ShapeDtypeStruct((m, d), F32),
        compiler_params=pltpu.CompilerParams(dimension_semantics=("arbitrary",),
                                             vmem_limit_bytes=VMEM_LIMIT),
        name="merge",
    )(*ins)


def _ffn_kernel(x_ref, ada_ref, w1_ref, b1_ref, w2_ref, b2_ref, lng_ref, lnb_ref, o_ref, *, alpha, fchunk):
    sh = ada_ref[0, 3:4, :]
    sc = ada_ref[0, 4:5, :]
    g2 = ada_ref[0, 5:6, :]
    d_ff = w1_ref.shape[1]
    n_chunks = d_ff // fchunk
    half = x_ref.shape[0] // 2

    def rows_program(rows):
        x = x_ref[rows, :]
        hb = (_norm_rows(x, ADALN_EPS) * (1.0 + sc) + sh).astype(BF16)
        yield
        ff = None
        for j in range(n_chunks):
            sl = slice(j * fchunk, (j + 1) * fchunk)
            t = jnp.maximum(_dot(hb, w1_ref[:, sl]) + b1_ref[:, sl], 0.0)
            part = _dot((t * t).astype(BF16), w2_ref[sl, :])
            ff = part if ff is None else ff + part
            yield
        ff = ff + b2_ref[...]
        o_ref[rows, :] = _norm_rows(alpha * x + g2 * ff, LN_EPS) * lng_ref[...] + lnb_ref[...]

    _run_skewed([rows_program(slice(0, half)), rows_program(slice(half, 2 * half))], lag=2)


def _ffn(x1, ada3, w1_bf, b1, w2_bf, b2, ln_g, ln_b, *, seq, tm, alpha, fchunk):
    m, d = x1.shape
    tiles_per_batch = seq // tm
    row = lambda i: (i, 0)
    const = lambda i: (0, 0)
    full = lambda a: pl.BlockSpec(a.shape, const, pipeline_mode=pl.Buffered(1))
    vec = lambda a: a.reshape(1, -1)
    ins = [x1, ada3, w1_bf, vec(b1), w2_bf, vec(b2), vec(ln_g), vec(ln_b)]
    in_specs = [pl.BlockSpec((tm, d), row),
                pl.BlockSpec((1,) + ada3.shape[1:], lambda i: (i // tiles_per_batch, 0, 0))]
    in_specs += [full(a) for a in ins[2:]]
    return pl.pallas_call(
        functools.partial(_ffn_kernel, alpha=alpha, fchunk=fchunk),
        grid=(m // tm,),
        in_specs=in_specs,
        out_specs=pl.BlockSpec((tm, d), row),
        out_shape=jax.ShapeDtypeStruct((m, d), F32),
        compiler_params=pltpu.CompilerParams(dimension_semantics=("arbitrary",),
                                             vmem_limit_bytes=VMEM_LIMIT),
        name="ffn",
    )(*ins)


def _branches(proj, rw_rk, rw_gn_g, rw_gn_b, hsum, bsz, seq):
    r, k, v, lw, kk, b, g, q, ks, vs = proj
    seq3 = lambda a: a.reshape(bsz, seq, -1)
    ya = _rwkv(seq3(r), seq3(k), seq3(v), seq3(lw), seq3(kk), seq3(b), seq3(g),
               rw_rk, rw_gn_g, rw_gn_b, hsum, chunk=min(RWKV_CHUNK, seq),
               rows=min(RWKV_CHUNKS_PER_STEP * RWKV_CHUNK, seq))
    o = _sb_attention(seq3(q), seq3(ks), seq3(vs), blk=min(SB_BLOCK, seq))
    return ya.reshape(bsz * seq, -1), o.reshape(bsz * seq, -1)


def _layer(x, c, w_ada, b_ada, w_in, mu_shift, rw_w0, rw_w2, rw_a0, rw_a2, rw_g2, rw_kk, rw_ka,
           rw_rk, rw_gn_g, rw_gn_b, w_branch_a, w_branch_b, w_out, ln1_g, ln1_b,
           w_ff1, b_ff1, w_ff2, b_ff2, ln2_g, ln2_b, *, alpha):
    bsz, seq, d = x.shape
    d_rwkv = rw_w0.shape[-1]
    tm = min(INPROJ_ROWS, seq)
    tm2 = min(MLP_ROWS, seq)
    rwkv_rows = min(RWKV_CHUNKS_PER_STEP * RWKV_CHUNK, seq)
    group_w = RWKV_GROUP_HEADS * HEAD_DIM
    assert seq % tm == 0 and seq % tm2 == 0 and seq % rwkv_rows == 0 and seq % min(SB_BLOCK, seq) == 0, seq
    assert rwkv_rows % (RWKV_SECTIONS * RWKV_CHUNK) == 0, (rwkv_rows, RWKV_SECTIONS)
    assert d_rwkv % group_w == 0 and w_branch_b.shape[0] % (2 * HEAD_DIM) == 0, (d_rwkv, w_branch_b.shape)
    assert HEAD_DIM & (HEAD_DIM - 1) == 0 and RWKV_CHUNK & (RWKV_CHUNK - 1) == 0

    ada3 = _ada(c, w_ada, b_ada).reshape(bsz, 6, d)
    lane = jnp.arange(RWKV_GROUP_HEADS * HEAD_DIM) // HEAD_DIM
    hsum = (lane[:, None] == lane[None, :]).astype(BF16)

    x2 = x.reshape(bsz * seq, d)
    (r, k, v, lw, kk, b, g, q, ks, vs, ga, gb) = _inproj(
        x2, ada3, w_in.astype(BF16), mu_shift, rw_w0, rw_w2, rw_a0, rw_a2, rw_g2, rw_kk, rw_ka, hsum,
        seq=seq, tm=tm)

    ya, ob = _branches((r, k, v, lw, kk, b, g, q, ks, vs), rw_rk, rw_gn_g, rw_gn_b, hsum, bsz, seq)

    x1 = _merge(x2, ada3, ya, ob, ga, gb,
                w_branch_a.astype(BF16), w_branch_b.astype(BF16), w_out.astype(BF16),
                ln1_g, ln1_b, seq=seq, tm=tm2, alpha=alpha)
    out = _ffn(x1, ada3, w_ff1.astype(BF16), b_ff1, w_ff2.astype(BF16), b_ff2, ln2_g, ln2_b,
               seq=seq, tm=tm2, alpha=alpha, fchunk=min(1024, w_ff1.shape[-1]))
    return out.reshape(bsz, seq, d)


def kernel(x, c, w_ada, b_ada, w_in, mu_shift, rw_w0, rw_w2, rw_a0, rw_a2, rw_g2, rw_kk, rw_ka, rw_rk,
           rw_gn_g, rw_gn_b, w_branch_a, w_branch_b, w_out, ln1_g, ln1_b, w_ff1, b_ff1, w_ff2, b_ff2,
           ln2_g, ln2_b):
    in_dtype = x.dtype
    layer_params = (w_ada, b_ada, w_in, mu_shift, rw_w0, rw_w2, rw_a0, rw_a2, rw_g2, rw_kk, rw_ka,
                    rw_rk, rw_gn_g, rw_gn_b, w_branch_a, w_branch_b, w_out, ln1_g, ln1_b,
                    w_ff1, b_ff1, w_ff2, b_ff2, ln2_g, ln2_b)
    depth = w_ada.shape[0]
    alpha = (2.0 * depth) ** 0.25
    for l in range(depth):
        x = _layer(x, c, *[p[l] for p in layer_params], alpha=alpha)
    return x.astype(in_dtype)
```

```python
import functools

import jax
import jax.numpy as jnp
from jax import lax
from jax.experimental import pallas as pl
from jax.experimental.pallas import tpu as pltpu

F32 = jnp.float32
BF16 = jnp.bfloat16

HEAD_DIM = 64
LN_EPS = 1e-5
ADALN_EPS = 1e-6
GN_EPS = 64e-5
RWKV_CHUNK = 64
RWKV_SECTIONS = 2
RWKV_SECTION_CHUNKS = 4
RWKV_SECTION_LAG = 12
RWKV_CHUNKS_PER_STEP = RWKV_SECTIONS * RWKV_SECTION_CHUNKS
SB_BLOCK = 256
INPROJ_ROWS = 512
MLP_ROWS = 1024
VMEM_LIMIT = 56 * 1024 * 1024


def _split2(a):
    hi = a.astype(BF16)
    lo = (a - hi.astype(F32)).astype(BF16)
    return hi, lo


_NN = (((1,), (0,)), ((), ()))
_NT = (((1,), (1,)), ((), ()))
_TN = (((0,), (0,)), ((), ()))


def _dot(a, b, dims=_NN):
    return lax.dot_general(a, b, dims, preferred_element_type=F32)


def _head_sums(x, ones_blk):
    m, w = x.shape[0], ones_blk.shape[0]
    hi, lo = _split2(x)
    outs = []
    for g in range(x.shape[1] // w):
        cols = slice(g * w, (g + 1) * w)
        both = _dot(jnp.concatenate([hi[:, cols], lo[:, cols]], axis=0), ones_blk)
        outs.append(both[:m] + both[m:])
    return jnp.concatenate(outs, axis=1)


def _dot_ones_lhs(a_bf, b):
    hi, lo = _split2(b)
    return _dot(a_bf, hi) + _dot(a_bf, lo)


_DONE = object()


def _run_skewed(programs, lag):
    live = [True] * len(programs)
    tick = 0
    while any(live):
        for i, prog in enumerate(programs):
            if live[i] and tick >= i * lag:
                live[i] = next(prog, _DONE) is not _DONE
        tick += 1


_NEG_LOG2_E = -1.4426950408889634


def _softplus(y):
    return jnp.maximum(y, 0.0) + jnp.log(1.0 + jnp.exp2(jnp.abs(y) * _NEG_LOG2_E))


def _sigmoid(y):
    return 1.0 / (1.0 + jnp.exp(-y))


def _norm_rows(x, eps):
    mu = jnp.mean(x, axis=-1, keepdims=True)
    xc = x - mu
    var = jnp.mean(xc * xc, axis=-1, keepdims=True)
    return xc * lax.rsqrt(var + eps)


def _ada_kernel(c_ref, w_ref, b_ref, o_ref):
    c = c_ref[...]
    s = c * _sigmoid(c)
    o_ref[...] = jnp.dot(s, w_ref[...], preferred_element_type=F32,
                         precision=lax.Precision.HIGHEST) + b_ref[...]


def _ada(c, w_ada, b_ada):
    bsz, d = c.shape
    n = w_ada.shape[1]
    return pl.pallas_call(
        _ada_kernel,
        grid=(n // d,),
        in_specs=[pl.BlockSpec((bsz, d), lambda j: (0, 0)),
                  pl.BlockSpec((d, d), lambda j: (0, j)),
                  pl.BlockSpec((1, d), lambda j: (0, j))],
        out_specs=pl.BlockSpec((bsz, d), lambda j: (0, j)),
        out_shape=jax.ShapeDtypeStruct((bsz, n), F32),
        name="ada",
    )(c, w_ada, b_ada.reshape(1, n))


def _inproj_kernel(x_ref, ada_ref, win_ref, mu_ref, w0_ref, w2_ref, a0_ref, a2_ref, g2_ref,
                   kkw_ref, kaw_ref, hsum_ref,
                   r_ref, k_ref, v_ref, lw_ref, kk_ref, b_ref, g_ref,
                   q_ref, ks_ref, vs_ref, ga_ref, gb_ref,
                   carry_ref, *, tiles_per_batch, d_rwkv, d_decay, d_aaa, d_gate, d_sb, d_model):
    i = pl.program_id(0)
    tm = x_ref.shape[0]
    d_shift = 3 * d_rwkv + d_decay + d_aaa + d_gate

    sh = ada_ref[0, 0:1, :]
    sc = ada_ref[0, 1:2, :]

    @pl.when(i % tiles_per_batch == 0)
    def _():
        carry_ref[...] = jnp.zeros_like(carry_ref)

    half = d_model // 2
    pair_w = 2 * HEAD_DIM
    first_head = lax.broadcasted_iota(jnp.int32, (1, pair_w), 1) < HEAD_DIM

    def rows_program(rows):
        hm = rows.stop - rows.start
        hb = (_norm_rows(x_ref[rows, :], ADALN_EPS) * (1.0 + sc) + sh).astype(BF16)
        row0 = lax.broadcasted_iota(jnp.int32, (hm, 1), 0) == 0
        yield

        def shifted(lo, width):
            cols = slice(lo, lo + width)
            z = _dot(hb, win_ref[:, cols])
            prev = jnp.where(row0, carry_ref[:, cols], pltpu.roll(z, 1, 0))
            carry_ref[:, cols] = z[hm - 1:hm, :]
            return z + mu_ref[:, cols] * (prev - z)

        def store_queries(z):
            zq = (z * (HEAD_DIM ** -0.5)).astype(BF16)
            zero = jnp.zeros((), BF16)
            for p in range(d_sb // pair_w):
                qp = zq[:, p * pair_w:(p + 1) * pair_w]
                q_ref[rows, (2 * p) * pair_w:(2 * p + 1) * pair_w] = jnp.where(first_head, qp, zero)
                q_ref[rows, (2 * p + 1) * pair_w:(2 * p + 2) * pair_w] = jnp.where(first_head, zero, qp)

        def plain(ref, cols, fn=lambda z: z):
            def store(z):
                ref[rows, cols] = fn(z).astype(BF16)
            return store

        all_cols = slice(0, d_sb)
        rest = [(store_queries, d_sb), (plain(ks_ref, all_cols), d_sb), (plain(vs_ref, all_cols), d_sb),
                (plain(ga_ref, slice(0, half), _sigmoid), half),
                (plain(ga_ref, slice(half, d_model), _sigmoid), half),
                (plain(gb_ref, slice(0, half), _sigmoid), half),
                (plain(gb_ref, slice(half, d_model), _sigmoid), half)]
        rest_col = [d_shift]

        def project_next():
            store, width = rest.pop(0)
            store(_dot(hb, win_ref[:, rest_col[0]:rest_col[0] + width]))
            rest_col[0] += width

        zl = shifted(3 * d_rwkv, d_decay + d_aaa + d_gate)
        zw = zl[:, :d_decay]
        za = zl[:, d_decay:d_decay + d_aaa]
        zg = zl[:, d_decay + d_aaa:]
        k = shifted(d_rwkv, d_rwkv)
        yield
        ww = w0_ref[...] + _dot(jnp.tanh(zw).astype(BF16), w2_ref[...])
        w_log = -_softplus(-ww) - 0.5
        lw_ref[rows, :] = -jnp.exp(w_log)
        project_next()
        yield
        arate = _sigmoid(a0_ref[...] + _dot(za.astype(BF16), a2_ref[...]))
        g_ref[rows, :] = _dot(_sigmoid(zg).astype(BF16), g2_ref[...])
        r_ref[rows, :] = shifted(0, d_rwkv)
        yield
        kkraw = k * kkw_ref[...]
        ssq = _head_sums(kkraw * kkraw, hsum_ref[...])
        kk = kkraw / jnp.maximum(jnp.sqrt(ssq), 1e-12)
        v_ref[rows, :] = shifted(2 * d_rwkv, d_rwkv)
        k_ref[rows, :] = k * (1.0 + (arate - 1.0) * kaw_ref[...])
        yield
        project_next()
        kk_ref[rows, :] = kk
        b_ref[rows, :] = kk * arate
        yield
        while rest:
            project_next()
            yield

    hm = tm // 2
    _run_skewed([rows_program(slice(0, hm)), rows_program(slice(hm, tm))], lag=2)


def _inproj(x2, ada3, win_bf, mu_shift, rw_w0, rw_w2, rw_a0, rw_a2, rw_g2, rw_kk, rw_ka, hsum,
            *, seq, tm):
    m, d_model = x2.shape
    d_rwkv = rw_w0.shape[-1]
    d_decay, d_aaa, d_gate = rw_w2.shape[0], rw_a2.shape[0], rw_g2.shape[0]
    d_shift = 3 * d_rwkv + d_decay + d_aaa + d_gate
    d_sb = (win_bf.shape[1] - d_shift - 2 * d_model) // 3
    tiles_per_batch = seq // tm
    row = lambda i: (i, 0)
    const = lambda i: (0, 0)
    full = lambda a: pl.BlockSpec(a.shape, const, pipeline_mode=pl.Buffered(1))
    vec = lambda a: a.reshape(1, -1)
    ins = [x2, ada3, win_bf, vec(mu_shift), vec(rw_w0), rw_w2.astype(BF16), vec(rw_a0),
           rw_a2.astype(BF16), rw_g2.astype(BF16), vec(rw_kk), vec(rw_ka), hsum]
    in_specs = [pl.BlockSpec((tm, d_model), row),
                pl.BlockSpec((1,) + ada3.shape[1:], lambda i: (i // tiles_per_batch, 0, 0))]
    in_specs += [full(a) for a in ins[2:]]
    widths = [d_rwkv] * 7 + [2 * d_sb, d_sb, d_sb] + [d_model] * 2
    dtypes = [F32] * 7 + [BF16] * 5
    out_shape = [jax.ShapeDtypeStruct((m, w), dt) for w, dt in zip(widths, dtypes)]
    out_specs = [pl.BlockSpec((tm, w), row) for w in widths]
    kern = functools.partial(_inproj_kernel, tiles_per_batch=tiles_per_batch, d_rwkv=d_rwkv,
                             d_decay=d_decay, d_aaa=d_aaa, d_gate=d_gate, d_sb=d_sb, d_model=d_model)
    return pl.pallas_call(
        kern,
        grid=(m // tm,),
        in_specs=in_specs,
        out_specs=out_specs,
        out_shape=out_shape,
        scratch_shapes=[pltpu.VMEM((1, d_shift), F32)],
        compiler_params=pltpu.CompilerParams(dimension_semantics=("arbitrary",),
                                             vmem_limit_bytes=VMEM_LIMIT),
        name="inproj",
    )(*ins)


RWKV_GROUP_HEADS = 4


def _block_diag(q, head_masks):
    zero = jnp.zeros((), q.dtype)
    return jnp.concatenate([jnp.where(m, q, zero) for m in head_masks], axis=0)


def _packed_dot1(lhs, q, head_masks, dims=_NN):
    return _dot(lhs.astype(BF16), _block_diag(q.astype(BF16), head_masks), dims)


def _packed_dot_wide_lhs(lhs, q, head_masks, dims=_NN):
    m = lhs.shape[0]
    lh, ll = _split2(lhs)
    both = _dot(jnp.concatenate([lh, ll], axis=0), _block_diag(q.astype(BF16), head_masks), dims)
    return both[:m] + both[m:]


def _rwkv_kernel(r_ref, k_ref, v_ref, lw_ref, kk_ref, b_ref, g_ref, rk_ref, gng_ref, gnb_ref, hsum_ref,
                 o_ref, state_ref, *, chunk):
    c = pl.program_id(1)
    n = HEAD_DIM
    rows, d = r_ref.shape[1], r_ref.shape[2]
    gw = RWKV_GROUP_HEADS * n
    n_groups = d // gw

    @pl.when(c == 0)
    def _():
        state_ref[...] = jnp.zeros_like(state_ref)

    t_idx = lax.broadcasted_iota(jnp.int32, (chunk, gw), 0)
    lane = lax.broadcasted_iota(jnp.int32, (chunk, gw), 1)
    i_idx = lane & (n - 1)
    strict = t_idx > i_idx
    incl = t_idx >= i_idx
    lane1 = lax.broadcasted_iota(jnp.int32, (1, gw), 1)
    hm = [(lane1 >> (n.bit_length() - 1)) == h for h in range(RWKV_GROUP_HEADS)]
    same = lambda s: (t_idx >> s) == (i_idx >> s)

    half_w = gw // 2
    lane_half = [(lane1 >> (half_w.bit_length() - 1)) == j for j in range(2)]
    second_in_half = ((lane >> (n.bit_length() - 1)) & 1) == 1

    def halves_rows(x):
        return jnp.concatenate([x[:, :half_w], x[:, half_w:]], axis=0).astype(BF16)

    def halves_masked(w):
        return _block_diag(w.astype(BF16), lane_half)

    def same_head(out):
        return jnp.where(second_in_half, out[n:], out[:n])

    lane_h = lax.broadcasted_iota(jnp.int32, (1, half_w), 1)
    in_half = [(lane_h >> (n.bit_length() - 1)) == j for j in range(half_w // n)]

    def two_rhs(lhs, x, y, dims=_NN):
        lb, xb, yb = lhs.astype(BF16), x.astype(BF16), y.astype(BF16)
        out_x, out_y = [], []
        for j in range(gw // half_w):
            cols = slice(j * half_w, (j + 1) * half_w)
            w = jnp.concatenate([_block_diag(xb[:, cols], in_half), _block_diag(yb[:, cols], in_half)],
                                axis=1 if dims == _NN else 0)
            o = _dot(lb[:, cols], w, dims)
            out_x.append(o[:, :half_w])
            out_y.append(o[:, half_w:])
        return jnp.concatenate(out_x, axis=1), jnp.concatenate(out_y, axis=1)

    sec_rows = rows // RWKV_SECTIONS
    sec_chunks = sec_rows // chunk
    rr = lax.broadcasted_iota(jnp.int32, (sec_rows, sec_rows), 0)
    cc = lax.broadcasted_iota(jnp.int32, (sec_rows, sec_rows), 1)
    cs = chunk.bit_length() - 1
    ltri = jnp.where((rr >= cc) & ((rr >> cs) == (cc >> cs)), 1.0, 0.0).astype(BF16)
    eye = jnp.where(t_idx == i_idx, 1.0, 0.0)
    chains = [(ci, gi) for ci in range(sec_chunks) for gi in range(n_groups)]
    win = lambda x, ci, gi: x[ci * chunk:(ci + 1) * chunk, gi * gw:(gi + 1) * gw]

    def section(row0):
        rs = slice(row0, row0 + sec_rows)
        r, k, v, lw = r_ref[0, rs, :], k_ref[0, rs, :], v_ref[0, rs, :], lw_ref[0, rs, :]
        kk, b = kk_ref[0, rs, :], b_ref[0, rs, :]
        cum = _dot_ones_lhs(ltri, lw)
        a_t = -kk * jnp.exp(cum - lw)
        r_t = r * jnp.exp(cum)
        e_neg = jnp.exp(-cum)
        b_t = b * e_neg
        k_t = k * e_neg
        tots = [cum[(ci + 1) * chunk - 1:(ci + 1) * chunk, :] for ci in range(sec_chunks)]
        e_bars = [jnp.exp(tots[ci] - cum[ci * chunk:(ci + 1) * chunk, :]) for ci in range(sec_chunks)]
        b_bar = [win(b, ci, gi) * e_bars[ci][:, gi * gw:(gi + 1) * gw] for ci, gi in chains]
        k_bar = [win(k, ci, gi) * e_bars[ci][:, gi * gw:(gi + 1) * gw] for ci, gi in chains]
        at = [win(a_t, ci, gi) for ci, gi in chains]
        rt = [win(r_t, ci, gi) for ci, gi in chains]
        vv = [win(v, ci, gi) for ci, gi in chains]
        yield

        ar = [jnp.concatenate([a_, r_], axis=0) for a_, r_ in zip(at, rt)]
        grams = [two_rhs(x, win(b_t, ci, gi), win(k_t, ci, gi), _NT) for x, (ci, gi) in zip(ar, chains)]
        gram_b = [x[0] for x in grams]
        gram_k = [x[1] for x in grams]
        yield
        m_ab = [jnp.where(strict, g_[:chunk], 0.0) for g_ in gram_b]
        p_rb = [jnp.where(incl, g_[chunk:], 0.0) for g_ in gram_b]
        m_ak = [jnp.where(strict, g_[:chunk], 0.0) for g_ in gram_k]
        p_rk = [jnp.where(incl, g_[chunk:], 0.0) for g_ in gram_k]
        yield

        ts = [eye + jnp.where(same(1), m, 0.0) for m in m_ab]
        s = 1
        while (1 << s) < chunk:
            level = same(s + 1) & jnp.logical_not(same(s))
            offs = [jnp.where(level, m, 0.0) for m in m_ab]
            mids = [_packed_dot1(t, o, hm) for t, o in zip(ts, offs)]
            yield
            ts = [t + _packed_dot1(md, t, hm) for t, md in zip(ts, mids)]
            yield
            s += 1

        both = [_packed_dot1(jnp.concatenate([m, pk], axis=0), x, hm) for m, pk, x in zip(m_ak, p_rk, vv)]
        mv = [x[:chunk] for x in both]
        pkv = [x[chunk:] for x in both]
        yield
        hat = [two_rhs(t, x, y) for t, x, y in zip(ts, at, mv)]
        a_hat = [x[0] for x in hat]
        u0 = [x[1] for x in hat]
        yield
        pb = [two_rhs(p, a, u) for p, a, u in zip(p_rb, a_hat, u0)]
        r_hat = [x + y[0] for x, y in zip(rt, pb)]
        y0 = [x + y[1] for x, y in zip(pkv, pb)]
        yield
        g_mat = [same_head(_dot(halves_rows(a), halves_masked(bb), _TN))
                 for a, bb in zip(a_hat, b_bar)]
        h_mat = [same_head(_dot(jnp.concatenate([halves_rows(u), halves_rows(x)], axis=0),
                                jnp.concatenate([halves_masked(bb), halves_masked(kb)], axis=0), _TN))
                 for u, x, bb, kb in zip(u0, vv, b_bar, k_bar)]
        yield

        y_rows = []
        for ci in range(sec_chunks):
            y_groups = []
            for gi in range(n_groups):
                j = ci * n_groups + gi
                s0 = state_ref[gi]
                y_groups.append(_packed_dot1(r_hat[j], s0, hm, _NT) + y0[j])
                w_tot = jnp.exp(tots[ci][:, gi * gw:(gi + 1) * gw])
                state_ref[gi] = s0 * w_tot + _packed_dot_wide_lhs(s0, g_mat[j], hm) + h_mat[j]
            y_rows.append(jnp.concatenate(y_groups, axis=1))
            yield
        y = jnp.concatenate(y_rows, axis=0)

        hsum = hsum_ref[...]
        inv_n = 1.0 / n
        mu = _head_sums(y, hsum) * inv_n
        yc = y - mu
        var = _head_sums(yc * yc, hsum) * inv_n
        yn = yc * lax.rsqrt(var + GN_EPS) * gng_ref[...] + gnb_ref[...]
        bonus = _head_sums(r * k * rk_ref[...], hsum)
        o_ref[0, rs, :] = ((yn + bonus * v) * g_ref[0, rs, :]).astype(o_ref.dtype)

    assert RWKV_SECTION_LAG >= sec_chunks
    _run_skewed([section(si * sec_rows) for si in range(RWKV_SECTIONS)], lag=RWKV_SECTION_LAG)


def _rwkv(r, k, v, lw, kk, b, g, rw_rk, gn_g, gn_b, hsum, *, chunk, rows):
    bsz, seq, d = r.shape
    n_groups = d // (RWKV_GROUP_HEADS * HEAD_DIM)
    blk = pl.BlockSpec((1, rows, d), lambda bi, ci: (bi, ci, 0))
    const = lambda bi, ci: (0, 0)
    vec = lambda a: a.reshape(1, d)
    return pl.pallas_call(
        functools.partial(_rwkv_kernel, chunk=chunk),
        grid=(bsz, seq // rows),
        in_specs=[blk] * 7 + [pl.BlockSpec((1, d), const)] * 3 + [pl.BlockSpec(hsum.shape, const)],
        out_specs=blk,
        out_shape=jax.ShapeDtypeStruct((bsz, seq, d), BF16),
        scratch_shapes=[pltpu.VMEM((n_groups, HEAD_DIM, RWKV_GROUP_HEADS * HEAD_DIM), F32)],
        compiler_params=pltpu.CompilerParams(dimension_semantics=("arbitrary", "arbitrary"),
                                             vmem_limit_bytes=VMEM_LIMIT),
        name="rwkv",
    )(r, k, v, lw, kk, b, g, vec(rw_rk), vec(gn_g), vec(gn_b), hsum)


SB_LOG_UNDERFLOW = 105.0


def _sb_kernel(q_ref, k_ref, v_ref, o_ref, acc_ref, run_ref, *, blk):
    qi = pl.program_id(1)
    pairs = k_ref.shape[2] // (2 * HEAD_DIM)
    row = lax.broadcasted_iota(jnp.int32, (blk, blk), 0)
    col = lax.broadcasted_iota(jnp.int32, (blk, blk), 1)
    causal = col < row
    jj = lax.broadcasted_iota(jnp.int32, (blk, 2 * blk), 0)
    ss = lax.broadcasted_iota(jnp.int32, (blk, 2 * blk), 1)
    suffix = jnp.where((jj >= ss) | (ss >= blk), 1.0, 0.0).astype(BF16)
    first_head = lax.broadcasted_iota(jnp.int32, (1, 2 * HEAD_DIM), 1) < HEAD_DIM
    zero = jnp.zeros((), BF16)

    lanes = [slice(p * 2 * HEAD_DIM, (p + 1) * 2 * HEAD_DIM) for p in range(pairs)]

    pw = 2 * HEAD_DIM

    def tile(kb, diagonal):
        start = pl.multiple_of(kb * blk, blk)
        zs = []
        for p in range(pairs):
            q2 = jnp.concatenate([q_ref[0, :, (2 * p + hh) * pw:(2 * p + hh + 1) * pw] for hh in range(2)],
                                 axis=0)
            z2 = _dot(q2, k_ref[0, pl.ds(start, blk), lanes[p]], _NT)
            zs += [z2[:blk], z2[blk:]]
        sps = [_softplus(z) for z in zs]
        if diagonal:
            sps = [jnp.where(causal, sp, 0.0) for sp in sps]
        css = []
        for p in range(pairs):
            c2 = _dot(jnp.concatenate(sps[2 * p:2 * p + 2], axis=0).astype(BF16), suffix)
            css += [c2[:blk], c2[blk:]]
        atts = []
        low = None
        for n_, (z, cs) in enumerate(zip(zs, css)):
            logw = z - cs[:, :blk]
            if diagonal:
                att = jnp.where(causal, jnp.exp(logw), 0.0)
                run = cs[:, blk:]
            else:
                prev = run_ref[n_]
                att = jnp.exp(logw - prev)
                run = prev + cs[:, blk:]
            run_ref[n_] = run
            low = run if low is None else jnp.minimum(low, run)
            atts.append(att.astype(BF16))
        done = (jnp.min(low) > SB_LOG_UNDERFLOW).astype(jnp.int32)
        for p in range(pairs):
            vp = v_ref[0, pl.ds(start, blk), lanes[p]]
            vv = jnp.concatenate([jnp.where(first_head, vp, zero),
                                  jnp.where(first_head, zero, vp)], axis=0)
            out = _dot(jnp.concatenate(atts[2 * p:2 * p + 2], axis=1), vv)
            if diagonal:
                acc_ref[:, lanes[p]] = out
            else:
                acc_ref[:, lanes[p]] += out
        return done

    done0 = tile(qi, True)

    def cond(c):
        i, done = c
        return jnp.logical_and(i <= qi, done == 0)

    def body(c):
        i, _ = c
        return i + 1, tile(qi - i, False)

    lax.while_loop(cond, body, (jnp.int32(1), done0))
    o_ref[0] = acc_ref[...].astype(o_ref.dtype)


def _sb_attention(q, k, v, *, blk):
    bsz, seq, d = k.shape
    heads = d // HEAD_DIM
    qspec = pl.BlockSpec((1, blk, 2 * d), lambda b, i: (b, i, 0))
    ospec = pl.BlockSpec((1, blk, d), lambda b, i: (b, i, 0))
    kvspec = pl.BlockSpec((1, seq, d), lambda b, i: (b, 0, 0), pipeline_mode=pl.Buffered(1))
    return pl.pallas_call(
        functools.partial(_sb_kernel, blk=blk),
        grid=(bsz, seq // blk),
        in_specs=[qspec, kvspec, kvspec],
        out_specs=ospec,
        out_shape=jax.ShapeDtypeStruct((bsz, seq, d), BF16),
        scratch_shapes=[pltpu.VMEM((blk, d), F32), pltpu.VMEM((heads, blk, blk), F32)],
        compiler_params=pltpu.CompilerParams(dimension_semantics=("arbitrary", "arbitrary"),
                                             vmem_limit_bytes=VMEM_LIMIT),
        name="sbattn",
    )(q, k, v)


def _merge_kernel(x_ref, ada_ref, ya_ref, ob_ref, ga_ref, gb_ref, wa_ref, wb_ref, wo_ref,
                  lng_ref, lnb_ref, o_ref, *, alpha):
    g1 = ada_ref[0, 2:3, :]
    half = x_ref.shape[0] // 2

    def rows_program(rows):
        y_a = _dot(ya_ref[rows, :], wa_ref[...])
        y_b = _dot(ob_ref[rows, :], wb_ref[...])
        yield
        merged = ga_ref[rows, :].astype(F32) * y_a + gb_ref[rows, :].astype(F32) * y_b
        mix = _dot(merged.astype(BF16), wo_ref[...])
        yield
        o_ref[rows, :] = _norm_rows(alpha * x_ref[rows, :] + g1 * mix, LN_EPS) * lng_ref[...] + lnb_ref[...]

    _run_skewed([rows_program(slice(0, half)), rows_program(slice(half, 2 * half))], lag=1)


def _merge(x2, ada3, ya, ob, ga, gb, wa_bf, wb_bf, wo_bf, ln_g, ln_b, *, seq, tm, alpha):
    m, d = x2.shape
    tiles_per_batch = seq // tm
    row = lambda i: (i, 0)
    const = lambda i: (0, 0)
    rows = lambda a: pl.BlockSpec((tm, a.shape[1]), row)
    full = lambda a: pl.BlockSpec(a.shape, const, pipeline_mode=pl.Buffered(1))
    vec = lambda a: a.reshape(1, -1)
    ins = [x2, ada3, ya, ob, ga, gb, wa_bf, wb_bf, wo_bf, vec(ln_g), vec(ln_b)]
    in_specs = [rows(x2), pl.BlockSpec((1,) + ada3.shape[1:], lambda i: (i // tiles_per_batch, 0, 0)),
                rows(ya), rows(ob), rows(ga), rows(gb)] + [full(a) for a in ins[6:]]
    return pl.pallas_call(
        functools.partial(_merge_kernel, alpha=alpha),
        grid=(m // tm,),
        in_specs=in_specs,
        out_specs=pl.BlockSpec((tm, d), row),
        out_shape=jax.ShapeDtypeStruct((m, d), F32),
        compiler_params=pltpu.CompilerParams(dimension_semantics=("arbitrary",),
                                             vmem_limit_bytes=VMEM_LIMIT),
        name="merge",
    )(*ins)


def _ffn_kernel(x_ref, ada_ref, w1_ref, b1_ref, w2_ref, b2_ref, lng_ref, lnb_ref, o_ref, *, alpha, fchunk):
    sh = ada_ref[0, 3:4, :]
    sc = ada_ref[0, 4:5, :]
    g2 = ada_ref[0, 5:6, :]
    d_ff = w1_ref.shape[1]
    n_chunks = d_ff // fchunk
    half = x_ref.shape[0] // 2

    def rows_program(rows):
        x = x_ref[rows, :]
        hb = (_norm_rows(x, ADALN_EPS) * (1.0 + sc) + sh).astype(BF16)
        yield
        ff = None
        for j in range(n_chunks):
            sl = slice(j * fchunk, (j + 1) * fchunk)
            t = jnp.maximum(_dot(hb, w1_ref[:, sl]) + b1_ref[:, sl], 0.0)
            part = _dot((t * t).astype(BF16), w2_ref[sl, :])
            ff = part if ff is None else ff + part
            yield
        ff = ff + b2_ref[...]
        o_ref[rows, :] = _norm_rows(alpha * x + g2 * ff, LN_EPS) * lng_ref[...] + lnb_ref[...]

    _run_skewed([rows_program(slice(0, half)), rows_program(slice(half, 2 * half))], lag=2)


def _ffn(x1, ada3, w1_bf, b1, w2_bf, b2, ln_g, ln_b, *, seq, tm, alpha, fchunk):
    m, d = x1.shape
    tiles_per_batch = seq // tm
    row = lambda i: (i, 0)
    const = lambda i: (0, 0)
    full = lambda a: pl.BlockSpec(a.shape, const, pipeline_mode=pl.Buffered(1))
    vec = lambda a: a.reshape(1, -1)
    ins = [x1, ada3, w1_bf, vec(b1), w2_bf, vec(b2), vec(ln_g), vec(ln_b)]
    in_specs = [pl.BlockSpec((tm, d), row),
                pl.BlockSpec((1,) + ada3.shape[1:], lambda i: (i // tiles_per_batch, 0, 0))]
    in_specs += [full(a) for a in ins[2:]]
    return pl.pallas_call(
        functools.partial(_ffn_kernel, alpha=alpha, fchunk=fchunk),
        grid=(m // tm,),
        in_specs=in_specs,
        out_specs=pl.BlockSpec((tm, d), row),
        out_shape=jax.ShapeDtypeStruct((m, d), F32),
        compiler_params=pltpu.CompilerParams(dimension_semantics=("arbitrary",),
                                             vmem_limit_bytes=VMEM_LIMIT),
        name="ffn",
    )(*ins)


def _branches(proj, rw_rk, rw_gn_g, rw_gn_b, hsum, bsz, seq):
    r, k, v, lw, kk, b, g, q, ks, vs = proj
    seq3 = lambda a: a.reshape(bsz, seq, -1)
    ya = _rwkv(seq3(r), seq3(k), seq3(v), seq3(lw), seq3(kk), seq3(b), seq3(g),
               rw_rk, rw_gn_g, rw_gn_b, hsum, chunk=min(RWKV_CHUNK, seq),
               rows=min(RWKV_CHUNKS_PER_STEP * RWKV_CHUNK, seq))
    o = _sb_attention(seq3(q), seq3(ks), seq3(vs), blk=min(SB_BLOCK, seq))
    return ya.reshape(bsz * seq, -1), o.reshape(bsz * seq, -1)


def _layer(x, c, w_ada, b_ada, w_in, mu_shift, rw_w0, rw_w2, rw_a0, rw_a2, rw_g2, rw_kk, rw_ka,
           rw_rk, rw_gn_g, rw_gn_b, w_branch_a, w_branch_b, w_out, ln1_g, ln1_b,
           w_ff1, b_ff1, w_ff2, b_ff2, ln2_g, ln2_b, *, alpha):
    bsz, seq, d = x.shape
    d_rwkv = rw_w0.shape[-1]
    tm = min(INPROJ_ROWS, seq)
    tm2 = min(MLP_ROWS, seq)
    rwkv_rows = min(RWKV_CHUNKS_PER_STEP * RWKV_CHUNK, seq)
    group_w = RWKV_GROUP_HEADS * HEAD_DIM
    assert seq % tm == 0 and seq % tm2 == 0 and seq % rwkv_rows == 0 and seq % min(SB_BLOCK, seq) == 0, seq
    assert rwkv_rows % (RWKV_SECTIONS * RWKV_CHUNK) == 0, (rwkv_rows, RWKV_SECTIONS)
    assert d_rwkv % group_w == 0 and w_branch_b.shape[0] % (2 * HEAD_DIM) == 0, (d_rwkv, w_branch_b.shape)
    assert HEAD_DIM & (HEAD_DIM - 1) == 0 and RWKV_CHUNK & (RWKV_CHUNK - 1) == 0

    ada3 = _ada(c, w_ada, b_ada).reshape(bsz, 6, d)
    lane = jnp.arange(RWKV_GROUP_HEADS * HEAD_DIM) // HEAD_DIM
    hsum = (lane[:, None] == lane[None, :]).astype(BF16)

    x2 = x.reshape(bsz * seq, d)
    (r, k, v, lw, kk, b, g, q, ks, vs, ga, gb) = _inproj(
        x2, ada3, w_in.astype(BF16), mu_shift, rw_w0, rw_w2, rw_a0, rw_a2, rw_g2, rw_kk, rw_ka, hsum,
        seq=seq, tm=tm)

    ya, ob = _branches((r, k, v, lw, kk, b, g, q, ks, vs), rw_rk, rw_gn_g, rw_gn_b, hsum, bsz, seq)

    x1 = _merge(x2, ada3, ya, ob, ga, gb,
                w_branch_a.astype(BF16), w_branch_b.astype(BF16), w_out.astype(BF16),
                ln1_g, ln1_b, seq=seq, tm=tm2, alpha=alpha)
    out = _ffn(x1, ada3, w_ff1.astype(BF16), b_ff1, w_ff2.astype(BF16), b_ff2, ln2_g, ln2_b,
               seq=seq, tm=tm2, alpha=alpha, fchunk=min(2048, w_ff1.shape[-1]))
    return out.reshape(bsz, seq, d)


def kernel(x, c, w_ada, b_ada, w_in, mu_shift, rw_w0, rw_w2, rw_a0, rw_a2, rw_g2, rw_kk, rw_ka, rw_rk,
           rw_gn_g, rw_gn_b, w_branch_a, w_branch_b, w_out, ln1_g, ln1_b, w_ff1, b_ff1, w_ff2, b_ff2,
           ln2_g, ln2_b):
    in_dtype = x.dtype
    layer_params = (w_ada, b_ada, w_in, mu_shift, rw_w0, rw_w2, rw_a0, rw_a2, rw_g2, rw_kk, rw_ka,
                    rw_rk, rw_gn_g, rw_gn_b, w_branch_a, w_branch_b, w_out, ln1_g, ln1_b,
                    w_ff1, b_ff1, w_ff2, b_ff2, ln2_g, ln2_b)
    depth = w_ada.shape[0]
    alpha = (2.0 * depth) ** 0.25
    for l in range(depth):
        x = _layer(x, c, *[p[l] for p in layer_params], alpha=alpha)
    return x.astype(in_dtype)
```

```python
import functools

import jax
import jax.numpy as jnp
from jax import lax
from jax.experimental import pallas as pl
from jax.experimental.pallas import tpu as pltpu

F32 = jnp.float32
BF16 = jnp.bfloat16

HEAD_DIM = 64
LN_EPS = 1e-5
ADALN_EPS = 1e-6
GN_EPS = 64e-5
RWKV_CHUNK = 64
RWKV_SECTIONS = 2
RWKV_SECTION_CHUNKS = 4
RWKV_SECTION_LAG = 12
RWKV_CHUNKS_PER_STEP = RWKV_SECTIONS * RWKV_SECTION_CHUNKS
SB_BLOCK = 256
INPROJ_ROWS = 512
MLP_ROWS = 1024
VMEM_LIMIT = 56 * 1024 * 1024


def _split2(a):
    hi = a.astype(BF16)
    lo = (a - hi.astype(F32)).astype(BF16)
    return hi, lo


_NN = (((1,), (0,)), ((), ()))
_NT = (((1,), (1,)), ((), ()))
_TN = (((0,), (0,)), ((), ()))


def _dot(a, b, dims=_NN):
    return lax.dot_general(a, b, dims, preferred_element_type=F32)


def _head_sums(x, ones_blk):
    m, w = x.shape[0], ones_blk.shape[0]
    hi, lo = _split2(x)
    outs = []
    for g in range(x.shape[1] // w):
        cols = slice(g * w, (g + 1) * w)
        both = _dot(jnp.concatenate([hi[:, cols], lo[:, cols]], axis=0), ones_blk)
        outs.append(both[:m] + both[m:])
    return jnp.concatenate(outs, axis=1)


def _dot_ones_lhs(a_bf, b):
    hi, lo = _split2(b)
    return _dot(a_bf, hi) + _dot(a_bf, lo)


_DONE = object()


def _run_skewed(programs, lag):
    live = [True] * len(programs)
    tick = 0
    while any(live):
        for i, prog in enumerate(programs):
            if live[i] and tick >= i * lag:
                live[i] = next(prog, _DONE) is not _DONE
        tick += 1


_NEG_LOG2_E = -1.4426950408889634


def _softplus(y):
    return jnp.maximum(y, 0.0) + jnp.log(1.0 + jnp.exp2(jnp.abs(y) * _NEG_LOG2_E))


def _sigmoid(y):
    return 1.0 / (1.0 + jnp.exp(-y))


def _norm_rows(x, eps):
    mu = jnp.mean(x, axis=-1, keepdims=True)
    xc = x - mu
    var = jnp.mean(xc * xc, axis=-1, keepdims=True)
    return xc * lax.rsqrt(var + eps)


def _ada_kernel(c_ref, w_ref, b_ref, o_ref):
    c = c_ref[...]
    s = c * _sigmoid(c)
    o_ref[...] = jnp.dot(s, w_ref[...], preferred_element_type=F32,
                         precision=lax.Precision.HIGHEST) + b_ref[...]


def _ada(c, w_ada, b_ada):
    bsz, d = c.shape
    n = w_ada.shape[1]
    return pl.pallas_call(
        _ada_kernel,
        grid=(n // d,),
        in_specs=[pl.BlockSpec((bsz, d), lambda j: (0, 0)),
                  pl.BlockSpec((d, d), lambda j: (0, j)),
                  pl.BlockSpec((1, d), lambda j: (0, j))],
        out_specs=pl.BlockSpec((bsz, d), lambda j: (0, j)),
        out_shape=jax.ShapeDtypeStruct((bsz, n), F32),
        name="ada",
    )(c, w_ada, b_ada.reshape(1, n))


def _inproj_kernel(x_ref, ada_ref, win_ref, mu_ref, w0_ref, w2_ref, a0_ref, a2_ref, g2_ref,
                   kkw_ref, kaw_ref, hsum_ref,
                   r_ref, k_ref, v_ref, lw_ref, kk_ref, b_ref, g_ref,
                   q_ref, ks_ref, vs_ref, ga_ref, gb_ref,
                   carry_ref, *, tiles_per_batch, d_rwkv, d_decay, d_aaa, d_gate, d_sb, d_model):
    i = pl.program_id(0)
    tm = x_ref.shape[0]
    d_shift = 3 * d_rwkv + d_decay + d_aaa + d_gate

    sh = ada_ref[0, 0:1, :]
    sc = ada_ref[0, 1:2, :]

    @pl.when(i % tiles_per_batch == 0)
    def _():
        carry_ref[...] = jnp.zeros_like(carry_ref)

    half = d_model // 2
    pair_w = 2 * HEAD_DIM
    first_head = lax.broadcasted_iota(jnp.int32, (1, pair_w), 1) < HEAD_DIM

    def rows_program(rows):
        hm = rows.stop - rows.start
        hb = (_norm_rows(x_ref[rows, :], ADALN_EPS) * (1.0 + sc) + sh).astype(BF16)
        row0 = lax.broadcasted_iota(jnp.int32, (hm, 1), 0) == 0
        yield

        def shifted(lo, width):
            cols = slice(lo, lo + width)
            z = _dot(hb, win_ref[:, cols])
            prev = jnp.where(row0, carry_ref[:, cols], pltpu.roll(z, 1, 0))
            carry_ref[:, cols] = z[hm - 1:hm, :]
            return z + mu_ref[:, cols] * (prev - z)

        def store_queries(z):
            zq = (z * (HEAD_DIM ** -0.5)).astype(BF16)
            zero = jnp.zeros((), BF16)
            for p in range(d_sb // pair_w):
                qp = zq[:, p * pair_w:(p + 1) * pair_w]
                q_ref[rows, (2 * p) * pair_w:(2 * p + 1) * pair_w] = jnp.where(first_head, qp, zero)
                q_ref[rows, (2 * p + 1) * pair_w:(2 * p + 2) * pair_w] = jnp.where(first_head, zero, qp)

        def plain(ref, cols, fn=lambda z: z):
            def store(z):
                ref[rows, cols] = fn(z).astype(BF16)
            return store

        all_cols = slice(0, d_sb)
        rest = [(store_queries, d_sb), (plain(ks_ref, all_cols), d_sb), (plain(vs_ref, all_cols), d_sb),
                (plain(ga_ref, slice(0, half), _sigmoid), half),
                (plain(ga_ref, slice(half, d_model), _sigmoid), half),
                (plain(gb_ref, slice(0, half), _sigmoid), half),
                (plain(gb_ref, slice(half, d_model), _sigmoid), half)]
        rest_col = [d_shift]

        def project_next():
            store, width = rest.pop(0)
            store(_dot(hb, win_ref[:, rest_col[0]:rest_col[0] + width]))
            rest_col[0] += width

        zl = shifted(3 * d_rwkv, d_decay + d_aaa + d_gate)
        zw = zl[:, :d_decay]
        za = zl[:, d_decay:d_decay + d_aaa]
        zg = zl[:, d_decay + d_aaa:]
        k = shifted(d_rwkv, d_rwkv)
        yield
        ww = w0_ref[...] + _dot(jnp.tanh(zw).astype(BF16), w2_ref[...])
        w_log = -_softplus(-ww) - 0.5
        lw_ref[rows, :] = -jnp.exp(w_log)
        project_next()
        yield
        arate = _sigmoid(a0_ref[...] + _dot(za.astype(BF16), a2_ref[...]))
        g_ref[rows, :] = _dot(_sigmoid(zg).astype(BF16), g2_ref[...])
        r_ref[rows, :] = shifted(0, d_rwkv)
        yield
        kkraw = k * kkw_ref[...]
        ssq = _head_sums(kkraw * kkraw, hsum_ref[...])
        kk = kkraw / jnp.maximum(jnp.sqrt(ssq), 1e-12)
        v_ref[rows, :] = shifted(2 * d_rwkv, d_rwkv)
        k_ref[rows, :] = k * (1.0 + (arate - 1.0) * kaw_ref[...])
        yield
        project_next()
        kk_ref[rows, :] = kk
        b_ref[rows, :] = kk * arate
        yield
        while rest:
            project_next()
            yield

    hm = tm // 2
    _run_skewed([rows_program(slice(0, hm)), rows_program(slice(hm, tm))], lag=2)


def _inproj(x2, ada3, win_bf, mu_shift, rw_w0, rw_w2, rw_a0, rw_a2, rw_g2, rw_kk, rw_ka, hsum,
            *, seq, tm):
    m, d_model = x2.shape
    d_rwkv = rw_w0.shape[-1]
    d_decay, d_aaa, d_gate = rw_w2.shape[0], rw_a2.shape[0], rw_g2.shape[0]
    d_shift = 3 * d_rwkv + d_decay + d_aaa + d_gate
    d_sb = (win_bf.shape[1] - d_shift - 2 * d_model) // 3
    tiles_per_batch = seq // tm
    row = lambda i: (i, 0)
    const = lambda i: (0, 0)
    full = lambda a: pl.BlockSpec(a.shape, const, pipeline_mode=pl.Buffered(1))
    vec = lambda a: a.reshape(1, -1)
    ins = [x2, ada3, win_bf, vec(mu_shift), vec(rw_w0), rw_w2.astype(BF16), vec(rw_a0),
           rw_a2.astype(BF16), rw_g2.astype(BF16), vec(rw_kk), vec(rw_ka), hsum]
    in_specs = [pl.BlockSpec((tm, d_model), row),
                pl.BlockSpec((1,) + ada3.shape[1:], lambda i: (i // tiles_per_batch, 0, 0))]
    in_specs += [full(a) for a in ins[2:]]
    widths = [d_rwkv] * 7 + [2 * d_sb, d_sb, d_sb] + [d_model] * 2
    dtypes = [F32] * 7 + [BF16] * 5
    out_shape = [jax.ShapeDtypeStruct((m, w), dt) for w, dt in zip(widths, dtypes)]
    out_specs = [pl.BlockSpec((tm, w), row) for w in widths]
    kern = functools.partial(_inproj_kernel, tiles_per_batch=tiles_per_batch, d_rwkv=d_rwkv,
                             d_decay=d_decay, d_aaa=d_aaa, d_gate=d_gate, d_sb=d_sb, d_model=d_model)
    return pl.pallas_call(
        kern,
        grid=(m // tm,),
        in_specs=in_specs,
        out_specs=out_specs,
        out_shape=out_shape,
        scratch_shapes=[pltpu.VMEM((1, d_shift), F32)],
        compiler_params=pltpu.CompilerParams(dimension_semantics=("arbitrary",),
                                             vmem_limit_bytes=VMEM_LIMIT),
        name="inproj",
    )(*ins)


RWKV_GROUP_HEADS = 4


def _block_diag(q, head_masks):
    zero = jnp.zeros((), q.dtype)
    return jnp.concatenate([jnp.where(m, q, zero) for m in head_masks], axis=0)


def _packed_dot1(lhs, q, head_masks, dims=_NN):
    return _dot(lhs.astype(BF16), _block_diag(q.astype(BF16), head_masks), dims)


def _packed_dot_wide_lhs(lhs, q, head_masks, dims=_NN):
    m = lhs.shape[0]
    lh, ll = _split2(lhs)
    both = _dot(jnp.concatenate([lh, ll], axis=0), _block_diag(q.astype(BF16), head_masks), dims)
    return both[:m] + both[m:]


def _rwkv_kernel(r_ref, k_ref, v_ref, lw_ref, kk_ref, b_ref, g_ref, rk_ref, gng_ref, gnb_ref, hsum_ref,
                 o_ref, state_ref, *, chunk):
    c = pl.program_id(1)
    n = HEAD_DIM
    rows, d = r_ref.shape[1], r_ref.shape[2]
    gw = RWKV_GROUP_HEADS * n
    n_groups = d // gw

    @pl.when(c == 0)
    def _():
        state_ref[...] = jnp.zeros_like(state_ref)

    t_idx = lax.broadcasted_iota(jnp.int32, (chunk, gw), 0)
    lane = lax.broadcasted_iota(jnp.int32, (chunk, gw), 1)
    i_idx = lane & (n - 1)
    strict = t_idx > i_idx
    incl = t_idx >= i_idx
    lane1 = lax.broadcasted_iota(jnp.int32, (1, gw), 1)
    hm = [(lane1 >> (n.bit_length() - 1)) == h for h in range(RWKV_GROUP_HEADS)]
    same = lambda s: (t_idx >> s) == (i_idx >> s)

    half_w = gw // 2
    lane_half = [(lane1 >> (half_w.bit_length() - 1)) == j for j in range(2)]
    second_in_half = ((lane >> (n.bit_length() - 1)) & 1) == 1

    def halves_rows(x):
        return jnp.concatenate([x[:, :half_w], x[:, half_w:]], axis=0).astype(BF16)

    def halves_masked(w):
        return _block_diag(w.astype(BF16), lane_half)

    def same_head(out):
        return jnp.where(second_in_half, out[n:], out[:n])

    lane_h = lax.broadcasted_iota(jnp.int32, (1, half_w), 1)
    in_half = [(lane_h >> (n.bit_length() - 1)) == j for j in range(half_w // n)]

    def two_rhs(lhs, x, y, dims=_NN):
        lb, xb, yb = lhs.astype(BF16), x.astype(BF16), y.astype(BF16)
        out_x, out_y = [], []
        for j in range(gw // half_w):
            cols = slice(j * half_w, (j + 1) * half_w)
            w = jnp.concatenate([_block_diag(xb[:, cols], in_half), _block_diag(yb[:, cols], in_half)],
                                axis=1 if dims == _NN else 0)
            o = _dot(lb[:, cols], w, dims)
            out_x.append(o[:, :half_w])
            out_y.append(o[:, half_w:])
        return jnp.concatenate(out_x, axis=1), jnp.concatenate(out_y, axis=1)

    sec_rows = rows // RWKV_SECTIONS
    sec_chunks = sec_rows // chunk
    rr = lax.broadcasted_iota(jnp.int32, (sec_rows, sec_rows), 0)
    cc = lax.broadcasted_iota(jnp.int32, (sec_rows, sec_rows), 1)
    cs = chunk.bit_length() - 1
    ltri = jnp.where((rr >= cc) & ((rr >> cs) == (cc >> cs)), 1.0, 0.0).astype(BF16)
    eye = jnp.where(t_idx == i_idx, 1.0, 0.0)
    chains = [(ci, gi) for ci in range(sec_chunks) for gi in range(n_groups)]
    win = lambda x, ci, gi: x[ci * chunk:(ci + 1) * chunk, gi * gw:(gi + 1) * gw]

    def section(row0):
        rs = slice(row0, row0 + sec_rows)
        r, k, v, lw = r_ref[0, rs, :], k_ref[0, rs, :], v_ref[0, rs, :], lw_ref[0, rs, :]
        kk, b = kk_ref[0, rs, :], b_ref[0, rs, :]
        cum = _dot_ones_lhs(ltri, lw)
        a_t = -kk * jnp.exp(cum - lw)
        r_t = r * jnp.exp(cum)
        e_neg = jnp.exp(-cum)
        b_t = b * e_neg
        k_t = k * e_neg
        tots = [cum[(ci + 1) * chunk - 1:(ci + 1) * chunk, :] for ci in range(sec_chunks)]
        e_bars = [jnp.exp(tots[ci] - cum[ci * chunk:(ci + 1) * chunk, :]) for ci in range(sec_chunks)]
        b_bar = [win(b, ci, gi) * e_bars[ci][:, gi * gw:(gi + 1) * gw] for ci, gi in chains]
        k_bar = [win(k, ci, gi) * e_bars[ci][:, gi * gw:(gi + 1) * gw] for ci, gi in chains]
        at = [win(a_t, ci, gi) for ci, gi in chains]
        rt = [win(r_t, ci, gi) for ci, gi in chains]
        vv = [win(v, ci, gi) for ci, gi in chains]
        yield

        ar = [jnp.concatenate([a_, r_], axis=0) for a_, r_ in zip(at, rt)]
        grams = [two_rhs(x, win(b_t, ci, gi), win(k_t, ci, gi), _NT) for x, (ci, gi) in zip(ar, chains)]
        gram_b = [x[0] for x in grams]
        gram_k = [x[1] for x in grams]
        yield
        m_ab = [jnp.where(strict, g_[:chunk], 0.0) for g_ in gram_b]
        p_rb = [jnp.where(incl, g_[chunk:], 0.0) for g_ in gram_b]
        m_ak = [jnp.where(strict, g_[:chunk], 0.0) for g_ in gram_k]
        p_rk = [jnp.where(incl, g_[chunk:], 0.0) for g_ in gram_k]
        yield

        ts = [eye + jnp.where(same(1), m, 0.0) for m in m_ab]
        s = 1
        while (1 << s) < chunk:
            level = same(s + 1) & jnp.logical_not(same(s))
            offs = [jnp.where(level, m, 0.0) for m in m_ab]
            mids = [_packed_dot1(t, o, hm) for t, o in zip(ts, offs)]
            yield
            ts = [t + _packed_dot1(md, t, hm) for t, md in zip(ts, mids)]
            yield
            s += 1

        both = [_packed_dot1(jnp.concatenate([m, pk], axis=0), x, hm) for m, pk, x in zip(m_ak, p_rk, vv)]
        mv = [x[:chunk] for x in both]
        pkv = [x[chunk:] for x in both]
        yield
        hat = [two_rhs(t, x, y) for t, x, y in zip(ts, at, mv)]
        a_hat = [x[0] for x in hat]
        u0 = [x[1] for x in hat]
        yield
        pb = [two_rhs(p, a, u) for p, a, u in zip(p_rb, a_hat, u0)]
        r_hat = [x + y[0] for x, y in zip(rt, pb)]
        y0 = [x + y[1] for x, y in zip(pkv, pb)]
        yield
        g_mat = [same_head(_dot(halves_rows(a), halves_masked(bb), _TN))
                 for a, bb in zip(a_hat, b_bar)]
        h_mat = [same_head(_dot(jnp.concatenate([halves_rows(u), halves_rows(x)], axis=0),
                                jnp.concatenate([halves_masked(bb), halves_masked(kb)], axis=0), _TN))
                 for u, x, bb, kb in zip(u0, vv, b_bar, k_bar)]
        yield

        y_rows = []
        for ci in range(sec_chunks):
            y_groups = []
            for gi in range(n_groups):
                j = ci * n_groups + gi
                s0 = state_ref[gi]
                y_groups.append(_packed_dot1(r_hat[j], s0, hm, _NT) + y0[j])
                w_tot = jnp.exp(tots[ci][:, gi * gw:(gi + 1) * gw])
                state_ref[gi] = s0 * w_tot + _packed_dot_wide_lhs(s0, g_mat[j], hm) + h_mat[j]
            y_rows.append(jnp.concatenate(y_groups, axis=1))
            yield
        y = jnp.concatenate(y_rows, axis=0)

        hsum = hsum_ref[...]
        inv_n = 1.0 / n
        mu = _head_sums(y, hsum) * inv_n
        yc = y - mu
        var = _head_sums(yc * yc, hsum) * inv_n
        yn = yc * lax.rsqrt(var + GN_EPS) * gng_ref[...] + gnb_ref[...]
        bonus = _head_sums(r * k * rk_ref[...], hsum)
        o_ref[0, rs, :] = ((yn + bonus * v) * g_ref[0, rs, :]).astype(o_ref.dtype)

    assert RWKV_SECTION_LAG >= sec_chunks
    _run_skewed([section(si * sec_rows) for si in range(RWKV_SECTIONS)], lag=RWKV_SECTION_LAG)


def _rwkv(r, k, v, lw, kk, b, g, rw_rk, gn_g, gn_b, hsum, *, chunk, rows):
    bsz, seq, d = r.shape
    n_groups = d // (RWKV_GROUP_HEADS * HEAD_DIM)
    blk = pl.BlockSpec((1, rows, d), lambda bi, ci: (bi, ci, 0))
    const = lambda bi, ci: (0, 0)
    vec = lambda a: a.reshape(1, d)
    return pl.pallas_call(
        functools.partial(_rwkv_kernel, chunk=chunk),
        grid=(bsz, seq // rows),
        in_specs=[blk] * 7 + [pl.BlockSpec((1, d), const)] * 3 + [pl.BlockSpec(hsum.shape, const)],
        out_specs=blk,
        out_shape=jax.ShapeDtypeStruct((bsz, seq, d), BF16),
        scratch_shapes=[pltpu.VMEM((n_groups, HEAD_DIM, RWKV_GROUP_HEADS * HEAD_DIM), F32)],
        compiler_params=pltpu.CompilerParams(dimension_semantics=("arbitrary", "arbitrary"),
                                             vmem_limit_bytes=VMEM_LIMIT),
        name="rwkv",
    )(r, k, v, lw, kk, b, g, vec(rw_rk), vec(gn_g), vec(gn_b), hsum)


SB_LOG_UNDERFLOW = 105.0


def _sb_kernel(q_ref, k_ref, v_ref, o_ref, acc_ref, run_ref, *, blk):
    qi = pl.program_id(1)
    pairs = k_ref.shape[2] // (2 * HEAD_DIM)
    row = lax.broadcasted_iota(jnp.int32, (blk, blk), 0)
    col = lax.broadcasted_iota(jnp.int32, (blk, blk), 1)
    causal = col < row
    jj = lax.broadcasted_iota(jnp.int32, (blk, 2 * blk), 0)
    ss = lax.broadcasted_iota(jnp.int32, (blk, 2 * blk), 1)
    suffix = jnp.where((jj >= ss) | (ss >= blk), 1.0, 0.0).astype(BF16)
    first_head = lax.broadcasted_iota(jnp.int32, (1, 2 * HEAD_DIM), 1) < HEAD_DIM
    zero = jnp.zeros((), BF16)

    lanes = [slice(p * 2 * HEAD_DIM, (p + 1) * 2 * HEAD_DIM) for p in range(pairs)]

    pw = 2 * HEAD_DIM

    def tile(kb, diagonal):
        start = pl.multiple_of(kb * blk, blk)
        zs = []
        for p in range(pairs):
            q2 = jnp.concatenate([q_ref[0, :, (2 * p + hh) * pw:(2 * p + hh + 1) * pw] for hh in range(2)],
                                 axis=0)
            z2 = _dot(q2, k_ref[0, pl.ds(start, blk), lanes[p]], _NT)
            zs += [z2[:blk], z2[blk:]]
        sps = [_softplus(z) for z in zs]
        if diagonal:
            sps = [jnp.where(causal, sp, 0.0) for sp in sps]
        css = []
        for p in range(pairs):
            c2 = _dot(jnp.concatenate(sps[2 * p:2 * p + 2], axis=0).astype(BF16), suffix)
            css += [c2[:blk], c2[blk:]]
        atts = []
        low = None
        for n_, (z, cs) in enumerate(zip(zs, css)):
            logw = z - cs[:, :blk]
            if diagonal:
                att = jnp.where(causal, jnp.exp(logw), 0.0)
                run = cs[:, blk:]
            else:
                prev = run_ref[n_]
                att = jnp.exp(logw - prev)
                run = prev + cs[:, blk:]
            run_ref[n_] = run
            low = run if low is None else jnp.minimum(low, run)
            atts.append(att.astype(BF16))
        done = (jnp.min(low) > SB_LOG_UNDERFLOW).astype(jnp.int32)
        for p in range(pairs):
            vp = v_ref[0, pl.ds(start, blk), lanes[p]]
            vv = jnp.concatenate([jnp.where(first_head, vp, zero),
                                  jnp.where(first_head, zero, vp)], axis=0)
            out = _dot(jnp.concatenate(atts[2 * p:2 * p + 2], axis=1), vv)
            if diagonal:
                acc_ref[:, lanes[p]] = out
            else:
                acc_ref[:, lanes[p]] += out
        return done

    done0 = tile(qi, True)

    def cond(c):
        i, done = c
        return jnp.logical_and(i <= qi, done == 0)

    def body(c):
        i, _ = c
        return i + 1, tile(qi - i, False)

    lax.while_loop(cond, body, (jnp.int32(1), done0))
    o_ref[0] = acc_ref[...].astype(o_ref.dtype)


def _sb_attention(q, k, v, *, blk):
    bsz, seq, d = k.shape
    heads = d // HEAD_DIM
    qspec = pl.BlockSpec((1, blk, 2 * d), lambda b, i: (b, i, 0))
    ospec = pl.BlockSpec((1, blk, d), lambda b, i: (b, i, 0))
    kvspec = pl.BlockSpec((1, seq, d), lambda b, i: (b, 0, 0), pipeline_mode=pl.Buffered(1))
    return pl.pallas_call(
        functools.partial(_sb_kernel, blk=blk),
        grid=(bsz, seq // blk),
        in_specs=[qspec, kvspec, kvspec],
        out_specs=ospec,
        out_shape=jax.ShapeDtypeStruct((bsz, seq, d), BF16),
        scratch_shapes=[pltpu.VMEM((blk, d), F32), pltpu.VMEM((heads, blk, blk), F32)],
        compiler_params=pltpu.CompilerParams(dimension_semantics=("arbitrary", "arbitrary"),
                                             vmem_limit_bytes=VMEM_LIMIT),
        name="sbattn",
    )(q, k, v)


def _merge_kernel(x_ref, ada_ref, ya_ref, ob_ref, ga_ref, gb_ref, wa_ref, wb_ref, wo_ref,
                  lng_ref, lnb_ref, o_ref, *, alpha):
    g1 = ada_ref[0, 2:3, :]
    half = x_ref.shape[0] // 2

    def rows_program(rows):
        y_a = _dot(ya_ref[rows, :], wa_ref[...])
        y_b = _dot(ob_ref[rows, :], wb_ref[...])
        yield
        merged = ga_ref[rows, :].astype(F32) * y_a + gb_ref[rows, :].astype(F32) * y_b
        mix = _dot(merged.astype(BF16), wo_ref[...])
        yield
        o_ref[rows, :] = _norm_rows(alpha * x_ref[rows, :] + g1 * mix, LN_EPS) * lng_ref[...] + lnb_ref[...]

    _run_skewed([rows_program(slice(0, half)), rows_program(slice(half, 2 * half))], lag=1)


def _merge(x2, ada3, ya, ob, ga, gb, wa_bf, wb_bf, wo_bf, ln_g, ln_b, *, seq, tm, alpha):
    m, d = x2.shape
    tiles_per_batch = seq // tm
    row = lambda i: (i, 0)
    const = lambda i: (0, 0)
    rows = lambda a: pl.BlockSpec((tm, a.shape[1]), row)
    full = lambda a: pl.BlockSpec(a.shape, const, pipeline_mode=pl.Buffered(1))
    vec = lambda a: a.reshape(1, -1)
    ins = [x2, ada3, ya, ob, ga, gb, wa_bf, wb_bf, wo_bf, vec(ln_g), vec(ln_b)]
    in_specs = [rows(x2), pl.BlockSpec((1,) + ada3.shape[1:], lambda i: (i // tiles_per_batch, 0, 0)),
                rows(ya), rows(ob), rows(ga), rows(gb)] + [full(a) for a in ins[6:]]
    return pl.pallas_call(
        functools.partial(_merge_kernel, alpha=alpha),
        grid=(m // tm,),
        in_specs=in_specs,
        out_specs=pl.BlockSpec((tm, d), row),
        out_shape=jax.ShapeDtypeStruct((m, d), F32),
        compiler_params=pltpu.CompilerParams(dimension_semantics=("arbitrary",),
                                             vmem_limit_bytes=VMEM_LIMIT),
        name="merge",
    )(*ins)


def _ffn_kernel(x_ref, ada_ref, w1_ref, b1_ref, w2_ref, b2_ref, lng_ref, lnb_ref, o_ref, *, alpha, fchunk):
    sh = ada_ref[0, 3:4, :]
    sc = ada_ref[0, 4:5, :]
    g2 = ada_ref[0, 5:6, :]
    d_ff = w1_ref.shape[1]
    n_chunks = d_ff // fchunk
    half = x_ref.shape[0] // 2

    def rows_program(rows):
        x = x_ref[rows, :]
        hb = (_norm_rows(x, ADALN_EPS) * (1.0 + sc) + sh).astype(BF16)
        yield
        ff = None
        for j in range(n_chunks):
            sl = slice(j * fchunk, (j + 1) * fchunk)
            t = jnp.maximum(_dot(hb, w1_ref[:, sl]) + b1_ref[:, sl], 0.0)
            part = _dot((t * t).astype(BF16), w2_ref[sl, :])
            ff = part if ff is None else ff + part
            yield
        ff = ff + b2_ref[...]
        o_ref[rows, :] = _norm_rows(alpha * x + g2 * ff, LN_EPS) * lng_ref[...] + lnb_ref[...]

    _run_skewed([rows_program(slice(0, half)), rows_program(slice(half, 2 * half))], lag=2)


def _ffn(x1, ada3, w1_bf, b1, w2_bf, b2, ln_g, ln_b, *, seq, tm, alpha, fchunk):
    m, d = x1.shape
    tiles_per_batch = seq // tm
    row = lambda i: (i, 0)
    const = lambda i: (0, 0)
    full = lambda a: pl.BlockSpec(a.shape, const, pipeline_mode=pl.Buffered(1))
    vec = lambda a: a.reshape(1, -1)
    ins = [x1, ada3, w1_bf, vec(b1), w2_bf, vec(b2), vec(ln_g), vec(ln_b)]
    in_specs = [pl.BlockSpec((tm, d), row),
                pl.BlockSpec((1,) + ada3.shape[1:], lambda i: (i // tiles_per_batch, 0, 0))]
    in_specs += [full(a) for a in ins[2:]]
    return pl.pallas_call(
        functools.partial(_ffn_kernel, alpha=alpha, fchunk=fchunk),
        grid=(m // tm,),
        in_specs=in_specs,
        out_specs=pl.BlockSpec((tm, d), row),
        out_shape=jax.ShapeDtypeStruct((m, d), F32),
        compiler_params=pltpu.CompilerParams(dimension_semantics=("arbitrary",),
                                             vmem_limit_bytes=VMEM_LIMIT),
        name="ffn",
    )(*ins)


def _branches(proj, rw_rk, rw_gn_g, rw_gn_b, hsum, bsz, seq):
    r, k, v, lw, kk, b, g, q, ks, vs = proj
    seq3 = lambda a: a.reshape(bsz, seq, -1)
    ya = _rwkv(seq3(r), seq3(k), seq3(v), seq3(lw), seq3(kk), seq3(b), seq3(g),
               rw_rk, rw_gn_g, rw_gn_b, hsum, chunk=min(RWKV_CHUNK, seq),
               rows=min(RWKV_CHUNKS_PER_STEP * RWKV_CHUNK, seq))
    o = _sb_attention(seq3(q), seq3(ks), seq3(vs), blk=min(SB_BLOCK, seq))
    return ya.reshape(bsz * seq, -1), o.reshape(bsz * seq, -1)


def _layer(x, c, w_ada, b_ada, w_in, mu_shift, rw_w0, rw_w2, rw_a0, rw_a2, rw_g2, rw_kk, rw_ka,
           rw_rk, rw_gn_g, rw_gn_b, w_branch_a, w_branch_b, w_out, ln1_g, ln1_b,
           w_ff1, b_ff1, w_ff2, b_ff2, ln2_g, ln2_b, *, alpha):
    bsz, seq, d = x.shape
    d_rwkv = rw_w0.shape[-1]
    tm = min(INPROJ_ROWS, seq)
    tm2 = min(MLP_ROWS, seq)
    rwkv_rows = min(RWKV_CHUNKS_PER_STEP * RWKV_CHUNK, seq)
    group_w = RWKV_GROUP_HEADS * HEAD_DIM
    assert seq % tm == 0 and seq % tm2 == 0 and seq % rwkv_rows == 0 and seq % min(SB_BLOCK, seq) == 0, seq
    assert rwkv_rows % (RWKV_SECTIONS * RWKV_CHUNK) == 0, (rwkv_rows, RWKV_SECTIONS)
    assert d_rwkv % group_w == 0 and w_branch_b.shape[0] % (2 * HEAD_DIM) == 0, (d_rwkv, w_branch_b.shape)
    assert HEAD_DIM & (HEAD_DIM - 1) == 0 and RWKV_CHUNK & (RWKV_CHUNK - 1) == 0

    ada3 = _ada(c, w_ada, b_ada).reshape(bsz, 6, d)
    lane = jnp.arange(RWKV_GROUP_HEADS * HEAD_DIM) // HEAD_DIM
    hsum = (lane[:, None] == lane[None, :]).astype(BF16)

    x2 = x.reshape(bsz * seq, d)
    (r, k, v, lw, kk, b, g, q, ks, vs, ga, gb) = _inproj(
        x2, ada3, w_in.astype(BF16), mu_shift, rw_w0, rw_w2, rw_a0, rw_a2, rw_g2, rw_kk, rw_ka, hsum,
        seq=seq, tm=tm)

    ya, ob = _branches((r, k, v, lw, kk, b, g, q, ks, vs), rw_rk, rw_gn_g, rw_gn_b, hsum, bsz, seq)

    x1 = _merge(x2, ada3, ya, ob, ga, gb,
                w_branch_a.astype(BF16), w_branch_b.astype(BF16), w_out.astype(BF16),
                ln1_g, ln1_b, seq=seq, tm=tm2, alpha=alpha)
    out = _ffn(x1, ada3, w_ff1.astype(BF16), b_ff1, w_ff2.astype(BF16), b_ff2, ln2_g, ln2_b,
               seq=seq, tm=tm2, alpha=alpha, fchunk=min(512, w_ff1.shape[-1]))
    return out.reshape(bsz, seq, d)


def kernel(x, c, w_ada, b_ada, w_in, mu_shift, rw_w0, rw_w2, rw_a0, rw_a2, rw_g2, rw_kk, rw_ka, rw_rk,
           rw_gn_g, rw_gn_b, w_branch_a, w_branch_b, w_out, ln1_g, ln1_b, w_ff1, b_ff1, w_ff2, b_ff2,
           ln2_g, ln2_b):
    in_dtype = x.dtype
    layer_params = (w_ada, b_ada, w_in, mu_shift, rw_w0, rw_w2, rw_a0, rw_a2, rw_g2, rw_kk, rw_ka,
                    rw_rk, rw_gn_g, rw_gn_b, w_branch_a, w_branch_b, w_out, ln1_g, ln1_b,
                    w_ff1, b_ff1, w_ff2, b_ff2, ln2_g, ln2_b)
    depth = w_ada.shape[0]
    alpha = (2.0 * depth) ** 0.25
    for l in range(depth):
        x = _layer(x, c, *[p[l] for p in layer_params], alpha=alpha)
    return x.astype(in_dtype)
```

```python
import functools

import jax
import jax.numpy as jnp
from jax import lax
from jax.experimental import pallas as pl
from jax.experimental.pallas import tpu as pltpu

F32 = jnp.float32
BF16 = jnp.bfloat16

HEAD_DIM = 64
LN_EPS = 1e-5
ADALN_EPS = 1e-6
GN_EPS = 64e-5
RWKV_CHUNK = 64
RWKV_SECTIONS = 2
RWKV_SECTION_CHUNKS = 4
RWKV_SECTION_LAG = 12
RWKV_CHUNKS_PER_STEP = RWKV_SECTIONS * RWKV_SECTION_CHUNKS
SB_BLOCK = 256
INPROJ_ROWS = 512
MLP_ROWS = 1024
VMEM_LIMIT = 56 * 1024 * 1024


def _split2(a):
    hi = a.astype(BF16)
    lo = (a - hi.astype(F32)).astype(BF16)
    return hi, lo


_NN = (((1,), (0,)), ((), ()))
_NT = (((1,), (1,)), ((), ()))
_TN = (((0,), (0,)), ((), ()))


def _dot(a, b, dims=_NN):
    return lax.dot_general(a, b, dims, preferred_element_type=F32)


def _head_sums(x, ones_blk):
    m, w = x.shape[0], ones_blk.shape[0]
    hi, lo = _split2(x)
    outs = []
    for g in range(x.shape[1] // w):
        cols = slice(g * w, (g + 1) * w)
        both = _dot(jnp.concatenate([hi[:, cols], lo[:, cols]], axis=0), ones_blk)
        outs.append(both[:m] + both[m:])
    return jnp.concatenate(outs, axis=1)


def _dot_ones_lhs(a_bf, b):
    hi, lo = _split2(b)
    return _dot(a_bf, hi) + _dot(a_bf, lo)


_DONE = object()


def _run_skewed(programs, lag):
    live = [True] * len(programs)
    tick = 0
    while any(live):
        for i, prog in enumerate(programs):
            if live[i] and tick >= i * lag:
                live[i] = next(prog, _DONE) is not _DONE
        tick += 1


_NEG_LOG2_E = -1.4426950408889634


def _softplus(y):
    return jnp.maximum(y, 0.0) + jnp.log(1.0 + jnp.exp2(jnp.abs(y) * _NEG_LOG2_E))


def _sigmoid(y):
    return 1.0 / (1.0 + jnp.exp(-y))


def _norm_rows(x, eps):
    mu = jnp.mean(x, axis=-1, keepdims=True)
    xc = x - mu
    var = jnp.mean(xc * xc, axis=-1, keepdims=True)
    return xc * lax.rsqrt(var + eps)


def _ada_kernel(c_ref, w_ref, b_ref, o_ref):
    c = c_ref[...]
    s = c * _sigmoid(c)
    m = s.shape[0]
    sh, sl = _split2(s)
    wh, wl = _split2(w_ref[...])
    top = _dot(jnp.concatenate([sh, sl], axis=0), wh)
    o_ref[...] = top[:m] + (top[m:] + _dot(sh, wl)) + b_ref[...]


def _ada(c, w_ada, b_ada):
    bsz, d = c.shape
    n = w_ada.shape[1]
    return pl.pallas_call(
        _ada_kernel,
        grid=(n // d,),
        in_specs=[pl.BlockSpec((bsz, d), lambda j: (0, 0)),
                  pl.BlockSpec((d, d), lambda j: (0, j)),
                  pl.BlockSpec((1, d), lambda j: (0, j))],
        out_specs=pl.BlockSpec((bsz, d), lambda j: (0, j)),
        out_shape=jax.ShapeDtypeStruct((bsz, n), F32),
        name="ada",
    )(c, w_ada, b_ada.reshape(1, n))


def _inproj_kernel(x_ref, ada_ref, win_ref, mu_ref, w0_ref, w2_ref, a0_ref, a2_ref, g2_ref,
                   kkw_ref, kaw_ref, hsum_ref,
                   r_ref, k_ref, v_ref, lw_ref, kk_ref, b_ref, g_ref,
                   q_ref, ks_ref, vs_ref, ga_ref, gb_ref,
                   carry_ref, *, tiles_per_batch, d_rwkv, d_decay, d_aaa, d_gate, d_sb, d_model):
    i = pl.program_id(0)
    tm = x_ref.shape[0]
    d_shift = 3 * d_rwkv + d_decay + d_aaa + d_gate

    sh = ada_ref[0, 0:1, :]
    sc = ada_ref[0, 1:2, :]

    @pl.when(i % tiles_per_batch == 0)
    def _():
        carry_ref[...] = jnp.zeros_like(carry_ref)

    half = d_model // 2
    pair_w = 2 * HEAD_DIM
    first_head = lax.broadcasted_iota(jnp.int32, (1, pair_w), 1) < HEAD_DIM

    def rows_program(rows):
        hm = rows.stop - rows.start
        hb = (_norm_rows(x_ref[rows, :], ADALN_EPS) * (1.0 + sc) + sh).astype(BF16)
        row0 = lax.broadcasted_iota(jnp.int32, (hm, 1), 0) == 0
        yield

        def shifted(lo, width):
            cols = slice(lo, lo + width)
            z = _dot(hb, win_ref[:, cols])
            prev = jnp.where(row0, carry_ref[:, cols], pltpu.roll(z, 1, 0))
            carry_ref[:, cols] = z[hm - 1:hm, :]
            return z + mu_ref[:, cols] * (prev - z)

        def store_queries(z):
            zq = (z * (HEAD_DIM ** -0.5)).astype(BF16)
            zero = jnp.zeros((), BF16)
            for p in range(d_sb // pair_w):
                qp = zq[:, p * pair_w:(p + 1) * pair_w]
                q_ref[rows, (2 * p) * pair_w:(2 * p + 1) * pair_w] = jnp.where(first_head, qp, zero)
                q_ref[rows, (2 * p + 1) * pair_w:(2 * p + 2) * pair_w] = jnp.where(first_head, zero, qp)

        def plain(ref, cols, fn=lambda z: z):
            def store(z):
                ref[rows, cols] = fn(z).astype(BF16)
            return store

        all_cols = slice(0, d_sb)
        rest = [(store_queries, d_sb), (plain(ks_ref, all_cols), d_sb), (plain(vs_ref, all_cols), d_sb),
                (plain(ga_ref, slice(0, half), _sigmoid), half),
                (plain(ga_ref, slice(half, d_model), _sigmoid), half),
                (plain(gb_ref, slice(0, half), _sigmoid), half),
                (plain(gb_ref, slice(half, d_model), _sigmoid), half)]
        rest_col = [d_shift]

        def project_next():
            store, width = rest.pop(0)
            store(_dot(hb, win_ref[:, rest_col[0]:rest_col[0] + width]))
            rest_col[0] += width

        zl = shifted(3 * d_rwkv, d_decay + d_aaa + d_gate)
        zw = zl[:, :d_decay]
        za = zl[:, d_decay:d_decay + d_aaa]
        zg = zl[:, d_decay + d_aaa:]
        k = shifted(d_rwkv, d_rwkv)
        yield
        ww = w0_ref[...] + _dot(jnp.tanh(zw).astype(BF16), w2_ref[...])
        w_log = -_softplus(-ww) - 0.5
        lw_ref[rows, :] = -jnp.exp(w_log)
        project_next()
        yield
        arate = _sigmoid(a0_ref[...] + _dot(za.astype(BF16), a2_ref[...]))
        g_ref[rows, :] = _dot(_sigmoid(zg).astype(BF16), g2_ref[...])
        r_ref[rows, :] = shifted(0, d_rwkv)
        yield
        kkraw = k * kkw_ref[...]
        ssq = _head_sums(kkraw * kkraw, hsum_ref[...])
        kk = kkraw / jnp.maximum(jnp.sqrt(ssq), 1e-12)
        v_ref[rows, :] = shifted(2 * d_rwkv, d_rwkv)
        k_ref[rows, :] = k * (1.0 + (arate - 1.0) * kaw_ref[...])
        yield
        project_next()
        kk_ref[rows, :] = kk
        b_ref[rows, :] = kk * arate
        yield
        while rest:
            project_next()
            yield

    hm = tm // 2
    _run_skewed([rows_program(slice(0, hm)), rows_program(slice(hm, tm))], lag=2)


def _inproj(x2, ada3, win_bf, mu_shift, rw_w0, rw_w2, rw_a0, rw_a2, rw_g2, rw_kk, rw_ka, hsum,
            *, seq, tm):
    m, d_model = x2.shape
    d_rwkv = rw_w0.shape[-1]
    d_decay, d_aaa, d_gate = rw_w2.shape[0], rw_a2.shape[0], rw_g2.shape[0]
    d_shift = 3 * d_rwkv + d_decay + d_aaa + d_gate
    d_sb = (win_bf.shape[1] - d_shift - 2 * d_model) // 3
    tiles_per_batch = seq // tm
    row = lambda i: (i, 0)
    const = lambda i: (0, 0)
    full = lambda a: pl.BlockSpec(a.shape, const, pipeline_mode=pl.Buffered(1))
    vec = lambda a: a.reshape(1, -1)
    ins = [x2, ada3, win_bf, vec(mu_shift), vec(rw_w0), rw_w2.astype(BF16), vec(rw_a0),
           rw_a2.astype(BF16), rw_g2.astype(BF16), vec(rw_kk), vec(rw_ka), hsum]
    in_specs = [pl.BlockSpec((tm, d_model), row),
                pl.BlockSpec((1,) + ada3.shape[1:], lambda i: (i // tiles_per_batch, 0, 0))]
    in_specs += [full(a) for a in ins[2:]]
    widths = [d_rwkv] * 7 + [2 * d_sb, d_sb, d_sb] + [d_model] * 2
    dtypes = [F32] * 7 + [BF16] * 5
    out_shape = [jax.ShapeDtypeStruct((m, w), dt) for w, dt in zip(widths, dtypes)]
    out_specs = [pl.BlockSpec((tm, w), row) for w in widths]
    kern = functools.partial(_inproj_kernel, tiles_per_batch=tiles_per_batch, d_rwkv=d_rwkv,
                             d_decay=d_decay, d_aaa=d_aaa, d_gate=d_gate, d_sb=d_sb, d_model=d_model)
    return pl.pallas_call(
        kern,
        grid=(m // tm,),
        in_specs=in_specs,
        out_specs=out_specs,
        out_shape=out_shape,
        scratch_shapes=[pltpu.VMEM((1, d_shift), F32)],
        compiler_params=pltpu.CompilerParams(dimension_semantics=("arbitrary",),
                                             vmem_limit_bytes=VMEM_LIMIT),
        name="inproj",
    )(*ins)


RWKV_GROUP_HEADS = 4


def _block_diag(q, head_masks):
    zero = jnp.zeros((), q.dtype)
    return jnp.concatenate([jnp.where(m, q, zero) for m in head_masks], axis=0)


def _packed_dot1(lhs, q, head_masks, dims=_NN):
    return _dot(lhs.astype(BF16), _block_diag(q.astype(BF16), head_masks), dims)


def _packed_dot_wide_lhs(lhs, q, head_masks, dims=_NN):
    m = lhs.shape[0]
    lh, ll = _split2(lhs)
    both = _dot(jnp.concatenate([lh, ll], axis=0), _block_diag(q.astype(BF16), head_masks), dims)
    return both[:m] + both[m:]


def _rwkv_kernel(r_ref, k_ref, v_ref, lw_ref, kk_ref, b_ref, g_ref, rk_ref, gng_ref, gnb_ref, hsum_ref,
                 o_ref, state_ref, *, chunk):
    c = pl.program_id(1)
    n = HEAD_DIM
    rows, d = r_ref.shape[1], r_ref.shape[2]
    gw = RWKV_GROUP_HEADS * n
    n_groups = d // gw

    @pl.when(c == 0)
    def _():
        state_ref[...] = jnp.zeros_like(state_ref)

    t_idx = lax.broadcasted_iota(jnp.int32, (chunk, gw), 0)
    lane = lax.broadcasted_iota(jnp.int32, (chunk, gw), 1)
    i_idx = lane & (n - 1)
    strict = t_idx > i_idx
    incl = t_idx >= i_idx
    lane1 = lax.broadcasted_iota(jnp.int32, (1, gw), 1)
    hm = [(lane1 >> (n.bit_length() - 1)) == h for h in range(RWKV_GROUP_HEADS)]
    same = lambda s: (t_idx >> s) == (i_idx >> s)

    half_w = gw // 2
    lane_half = [(lane1 >> (half_w.bit_length() - 1)) == j for j in range(2)]
    second_in_half = ((lane >> (n.bit_length() - 1)) & 1) == 1

    def halves_rows(x):
        return jnp.concatenate([x[:, :half_w], x[:, half_w:]], axis=0).astype(BF16)

    def halves_masked(w):
        return _block_diag(w.astype(BF16), lane_half)

    def same_head(out):
        return jnp.where(second_in_half, out[n:], out[:n])

    lane_h = lax.broadcasted_iota(jnp.int32, (1, half_w), 1)
    in_half = [(lane_h >> (n.bit_length() - 1)) == j for j in range(half_w // n)]

    def two_rhs(lhs, x, y, dims=_NN):
        lb, xb, yb = lhs.astype(BF16), x.astype(BF16), y.astype(BF16)
        out_x, out_y = [], []
        for j in range(gw // half_w):
            cols = slice(j * half_w, (j + 1) * half_w)
            w = jnp.concatenate([_block_diag(xb[:, cols], in_half), _block_diag(yb[:, cols], in_half)],
                                axis=1 if dims == _NN else 0)
            o = _dot(lb[:, cols], w, dims)
            out_x.append(o[:, :half_w])
            out_y.append(o[:, half_w:])
        return jnp.concatenate(out_x, axis=1), jnp.concatenate(out_y, axis=1)

    sec_rows = rows // RWKV_SECTIONS
    sec_chunks = sec_rows // chunk
    rr = lax.broadcasted_iota(jnp.int32, (sec_rows, sec_rows), 0)
    cc = lax.broadcasted_iota(jnp.int32, (sec_rows, sec_rows), 1)
    cs = chunk.bit_length() - 1
    ltri = jnp.where((rr >= cc) & ((rr >> cs) == (cc >> cs)), 1.0, 0.0).astype(BF16)
    eye = jnp.where(t_idx == i_idx, 1.0, 0.0)
    chains = [(ci, gi) for ci in range(sec_chunks) for gi in range(n_groups)]
    win = lambda x, ci, gi: x[ci * chunk:(ci + 1) * chunk, gi * gw:(gi + 1) * gw]

    def section(row0):
        rs = slice(row0, row0 + sec_rows)
        r, k, v, lw = r_ref[0, rs, :], k_ref[0, rs, :], v_ref[0, rs, :], lw_ref[0, rs, :]
        kk, b = kk_ref[0, rs, :], b_ref[0, rs, :]
        cum = _dot_ones_lhs(ltri, lw)
        a_t = -kk * jnp.exp(cum - lw)
        r_t = r * jnp.exp(cum)
        e_neg = jnp.exp(-cum)
        b_t = b * e_neg
        k_t = k * e_neg
        tots = [cum[(ci + 1) * chunk - 1:(ci + 1) * chunk, :] for ci in range(sec_chunks)]
        e_bars = [jnp.exp(tots[ci] - cum[ci * chunk:(ci + 1) * chunk, :]) for ci in range(sec_chunks)]
        b_bar = [win(b, ci, gi) * e_bars[ci][:, gi * gw:(gi + 1) * gw] for ci, gi in chains]
        k_bar = [win(k, ci, gi) * e_bars[ci][:, gi * gw:(gi + 1) * gw] for ci, gi in chains]
        at = [win(a_t, ci, gi) for ci, gi in chains]
        rt = [win(r_t, ci, gi) for ci, gi in chains]
        vv = [win(v, ci, gi) for ci, gi in chains]
        yield

        ar = [jnp.concatenate([a_, r_], axis=0) for a_, r_ in zip(at, rt)]
        grams = [two_rhs(x, win(b_t, ci, gi), win(k_t, ci, gi), _NT) for x, (ci, gi) in zip(ar, chains)]
        gram_b = [x[0] for x in grams]
        gram_k = [x[1] for x in grams]
        yield
        m_ab = [jnp.where(strict, g_[:chunk], 0.0) for g_ in gram_b]
        p_rb = [jnp.where(incl, g_[chunk:], 0.0) for g_ in gram_b]
        m_ak = [jnp.where(strict, g_[:chunk], 0.0) for g_ in gram_k]
        p_rk = [jnp.where(incl, g_[chunk:], 0.0) for g_ in gram_k]
        yield

        ts = [eye + jnp.where(same(1), m, 0.0) for m in m_ab]
        s = 1
        while (1 << s) < chunk:
            level = same(s + 1) & jnp.logical_not(same(s))
            offs = [jnp.where(level, m, 0.0) for m in m_ab]
            mids = [_packed_dot1(t, o, hm) for t, o in zip(ts, offs)]
            yield
            ts = [t + _packed_dot1(md, t, hm) for t, md in zip(ts, mids)]
            yield
            s += 1

        both = [_packed_dot1(jnp.concatenate([m, pk], axis=0), x, hm) for m, pk, x in zip(m_ak, p_rk, vv)]
        mv = [x[:chunk] for x in both]
        pkv = [x[chunk:] for x in both]
        yield
        hat = [two_rhs(t, x, y) for t, x, y in zip(ts, at, mv)]
        a_hat = [x[0] for x in hat]
        u0 = [x[1] for x in hat]
        yield
        pb = [two_rhs(p, a, u) for p, a, u in zip(p_rb, a_hat, u0)]
        r_hat = [x + y[0] for x, y in zip(rt, pb)]
        y0 = [x + y[1] for x, y in zip(pkv, pb)]
        yield
        g_mat = [same_head(_dot(halves_rows(a), halves_masked(bb), _TN))
                 for a, bb in zip(a_hat, b_bar)]
        h_mat = [same_head(_dot(jnp.concatenate([halves_rows(u), halves_rows(x)], axis=0),
                                jnp.concatenate([halves_masked(bb), halves_masked(kb)], axis=0), _TN))
                 for u, x, bb, kb in zip(u0, vv, b_bar, k_bar)]
        yield

        y_rows = []
        for ci in range(sec_chunks):
            y_groups = []
            for gi in range(n_groups):
                j = ci * n_groups + gi
                s0 = state_ref[gi]
                y_groups.append(_packed_dot1(r_hat[j], s0, hm, _NT) + y0[j])
                w_tot = jnp.exp(tots[ci][:, gi * gw:(gi + 1) * gw])
                state_ref[gi] = s0 * w_tot + _packed_dot_wide_lhs(s0, g_mat[j], hm) + h_mat[j]
            y_rows.append(jnp.concatenate(y_groups, axis=1))
            yield
        y = jnp.concatenate(y_rows, axis=0)

        hsum = hsum_ref[...]
        inv_n = 1.0 / n
        mu = _head_sums(y, hsum) * inv_n
        yc = y - mu
        var = _head_sums(yc * yc, hsum) * inv_n
        yn = yc * lax.rsqrt(var + GN_EPS) * gng_ref[...] + gnb_ref[...]
        bonus = _head_sums(r * k * rk_ref[...], hsum)
        o_ref[0, rs, :] = ((yn + bonus * v) * g_ref[0, rs, :]).astype(o_ref.dtype)

    assert RWKV_SECTION_LAG >= sec_chunks
    _run_skewed([section(si * sec_rows) for si in range(RWKV_SECTIONS)], lag=RWKV_SECTION_LAG)


def _rwkv(r, k, v, lw, kk, b, g, rw_rk, gn_g, gn_b, hsum, *, chunk, rows):
    bsz, seq, d = r.shape
    n_groups = d // (RWKV_GROUP_HEADS * HEAD_DIM)
    blk = pl.BlockSpec((1, rows, d), lambda bi, ci: (bi, ci, 0))
    const = lambda bi, ci: (0, 0)
    vec = lambda a: a.reshape(1, d)
    return pl.pallas_call(
        functools.partial(_rwkv_kernel, chunk=chunk),
        grid=(bsz, seq // rows),
        in_specs=[blk] * 7 + [pl.BlockSpec((1, d), const)] * 3 + [pl.BlockSpec(hsum.shape, const)],
        out_specs=blk,
        out_shape=jax.ShapeDtypeStruct((bsz, seq, d), BF16),
        scratch_shapes=[pltpu.VMEM((n_groups, HEAD_DIM, RWKV_GROUP_HEADS * HEAD_DIM), F32)],
        compiler_params=pltpu.CompilerParams(dimension_semantics=("arbitrary", "arbitrary"),
                                             vmem_limit_bytes=VMEM_LIMIT),
        name="rwkv",
    )(r, k, v, lw, kk, b, g, vec(rw_rk), vec(gn_g), vec(gn_b), hsum)


SB_LOG_UNDERFLOW = 105.0


def _sb_kernel(q_ref, k_ref, v_ref, o_ref, acc_ref, run_ref, *, blk):
    qi = pl.program_id(1)
    pairs = k_ref.shape[2] // (2 * HEAD_DIM)
    row = lax.broadcasted_iota(jnp.int32, (blk, blk), 0)
    col = lax.broadcasted_iota(jnp.int32, (blk, blk), 1)
    causal = col < row
    jj = lax.broadcasted_iota(jnp.int32, (blk, 2 * blk), 0)
    ss = lax.broadcasted_iota(jnp.int32, (blk, 2 * blk), 1)
    suffix = jnp.where((jj >= ss) | (ss >= blk), 1.0, 0.0).astype(BF16)
    first_head = lax.broadcasted_iota(jnp.int32, (1, 2 * HEAD_DIM), 1) < HEAD_DIM
    zero = jnp.zeros((), BF16)

    lanes = [slice(p * 2 * HEAD_DIM, (p + 1) * 2 * HEAD_DIM) for p in range(pairs)]

    pw = 2 * HEAD_DIM

    def tile(kb, diagonal):
        start = pl.multiple_of(kb * blk, blk)
        zs = []
        for p in range(pairs):
            q2 = jnp.concatenate([q_ref[0, :, (2 * p + hh) * pw:(2 * p + hh + 1) * pw] for hh in range(2)],
                                 axis=0)
            z2 = _dot(q2, k_ref[0, pl.ds(start, blk), lanes[p]], _NT)
            zs += [z2[:blk], z2[blk:]]
        sps = [_softplus(z) for z in zs]
        if diagonal:
            sps = [jnp.where(causal, sp, 0.0) for sp in sps]
        css = []
        for p in range(pairs):
            c2 = _dot(jnp.concatenate(sps[2 * p:2 * p + 2], axis=0).astype(BF16), suffix)
            css += [c2[:blk], c2[blk:]]
        atts = []
        low = None
        for n_, (z, cs) in enumerate(zip(zs, css)):
            logw = z - cs[:, :blk]
            if diagonal:
                att = jnp.where(causal, jnp.exp(logw), 0.0)
                run = cs[:, blk:]
            else:
                prev = run_ref[n_]
                att = jnp.exp(logw - prev)
                run = prev + cs[:, blk:]
            run_ref[n_] = run
            low = run if low is None else jnp.minimum(low, run)
            atts.append(att.astype(BF16))
        done = (jnp.min(low) > SB_LOG_UNDERFLOW).astype(jnp.int32)
        for p in range(pairs):
            vp = v_ref[0, pl.ds(start, blk), lanes[p]]
            vv = jnp.concatenate([jnp.where(first_head, vp, zero),
                                  jnp.where(first_head, zero, vp)], axis=0)
            out = _dot(jnp.concatenate(atts[2 * p:2 * p + 2], axis=1), vv)
            if diagonal:
                acc_ref[:, lanes[p]] = out
            else:
                acc_ref[:, lanes[p]] += out
        return done

    done0 = tile(qi, True)

    def cond(c):
        i, done = c
        return jnp.logical_and(i <= qi, done == 0)

    def body(c):
        i, _ = c
        return i + 1, tile(qi - i, False)

    lax.while_loop(cond, body, (jnp.int32(1), done0))
    o_ref[0] = acc_ref[...].astype(o_ref.dtype)


def _sb_attention(q, k, v, *, blk):
    bsz, seq, d = k.shape
    heads = d // HEAD_DIM
    qspec = pl.BlockSpec((1, blk, 2 * d), lambda b, i: (b, i, 0))
    ospec = pl.BlockSpec((1, blk, d), lambda b, i: (b, i, 0))
    kvspec = pl.BlockSpec((1, seq, d), lambda b, i: (b, 0, 0), pipeline_mode=pl.Buffered(1))
    return pl.pallas_call(
        functools.partial(_sb_kernel, blk=blk),
        grid=(bsz, seq // blk),
        in_specs=[qspec, kvspec, kvspec],
        out_specs=ospec,
        out_shape=jax.ShapeDtypeStruct((bsz, seq, d), BF16),
        scratch_shapes=[pltpu.VMEM((blk, d), F32), pltpu.VMEM((heads, blk, blk), F32)],
        compiler_params=pltpu.CompilerParams(dimension_semantics=("arbitrary", "arbitrary"),
                                             vmem_limit_bytes=VMEM_LIMIT),
        name="sbattn",
    )(q, k, v)


def _merge_kernel(x_ref, ada_ref, ya_ref, ob_ref, ga_ref, gb_ref, wa_ref, wb_ref, wo_ref,
                  lng_ref, lnb_ref, o_ref, *, alpha):
    g1 = ada_ref[0, 2:3, :]
    half = x_ref.shape[0] // 2

    def rows_program(rows):
        y_a = _dot(ya_ref[rows, :], wa_ref[...])
        y_b = _dot(ob_ref[rows, :], wb_ref[...])
        yield
        merged = ga_ref[rows, :].astype(F32) * y_a + gb_ref[rows, :].astype(F32) * y_b
        mix = _dot(merged.astype(BF16), wo_ref[...])
        yield
        o_ref[rows, :] = _norm_rows(alpha * x_ref[rows, :] + g1 * mix, LN_EPS) * lng_ref[...] + lnb_ref[...]

    _run_skewed([rows_program(slice(0, half)), rows_program(slice(half, 2 * half))], lag=1)


def _merge(x2, ada3, ya, ob, ga, gb, wa_bf, wb_bf, wo_bf, ln_g, ln_b, *, seq, tm, alpha):
    m, d = x2.shape
    tiles_per_batch = seq // tm
    row = lambda i: (i, 0)
    const = lambda i: (0, 0)
    rows = lambda a: pl.BlockSpec((tm, a.shape[1]), row)
    full = lambda a: pl.BlockSpec(a.shape, const, pipeline_mode=pl.Buffered(1))
    vec = lambda a: a.reshape(1, -1)
    ins = [x2, ada3, ya, ob, ga, gb, wa_bf, wb_bf, wo_bf, vec(ln_g), vec(ln_b)]
    in_specs = [rows(x2), pl.BlockSpec((1,) + ada3.shape[1:], lambda i: (i // tiles_per_batch, 0, 0)),
                rows(ya), rows(ob), rows(ga), rows(gb)] + [full(a) for a in ins[6:]]
    return pl.pallas_call(
        functools.partial(_merge_kernel, alpha=alpha),
        grid=(m // tm,),
        in_specs=in_specs,
        out_specs=pl.BlockSpec((tm, d), row),
        out_shape=jax.ShapeDtypeStruct((m, d), F32),
        compiler_params=pltpu.CompilerParams(dimension_semantics=("arbitrary",),
                                             vmem_limit_bytes=VMEM_LIMIT),
        name="merge",
    )(*ins)


def _ffn_kernel(x_ref, ada_ref, w1_ref, b1_ref, w2_ref, b2_ref, lng_ref, lnb_ref, o_ref, *, alpha, fchunk):
    sh = ada_ref[0, 3:4, :]
    sc = ada_ref[0, 4:5, :]
    g2 = ada_ref[0, 5:6, :]
    d_ff = w1_ref.shape[1]
    n_chunks = d_ff // fchunk
    half = x_ref.shape[0] // 2

    def rows_program(rows):
        x = x_ref[rows, :]
        hb = (_norm_rows(x, ADALN_EPS) * (1.0 + sc) + sh).astype(BF16)
        yield
        ff = None
        for j in range(n_chunks):
            sl = slice(j * fchunk, (j + 1) * fchunk)
            t = jnp.maximum(_dot(hb, w1_ref[:, sl]) + b1_ref[:, sl], 0.0)
            part = _dot((t * t).astype(BF16), w2_ref[sl, :])
            ff = part if ff is None else ff + part
            yield
        ff = ff + b2_ref[...]
        o_ref[rows, :] = _norm_rows(alpha * x + g2 * ff, LN_EPS) * lng_ref[...] + lnb_ref[...]

    _run_skewed([rows_program(slice(0, half)), rows_program(slice(half, 2 * half))], lag=2)


def _ffn(x1, ada3, w1_bf, b1, w2_bf, b2, ln_g, ln_b, *, seq, tm, alpha, fchunk):
    m, d = x1.shape
    tiles_per_batch = seq // tm
    row = lambda i: (i, 0)
    const = lambda i: (0, 0)
    full = lambda a: pl.BlockSpec(a.shape, const, pipeline_mode=pl.Buffered(1))
    vec = lambda a: a.reshape(1, -1)
    ins = [x1, ada3, w1_bf, vec(b1), w2_bf, vec(b2), vec(ln_g), vec(ln_b)]
    in_specs = [pl.BlockSpec((tm, d), row),
                pl.BlockSpec((1,) + ada3.shape[1:], lambda i: (i // tiles_per_batch, 0, 0))]
    in_specs += [full(a) for a in ins[2:]]
    return pl.pallas_call(
        functools.partial(_ffn_kernel, alpha=alpha, fchunk=fchunk),
        grid=(m // tm,),
        in_specs=in_specs,
        out_specs=pl.BlockSpec((tm, d), row),
        out_shape=jax.ShapeDtypeStruct((m, d), F32),
        compiler_params=pltpu.CompilerParams(dimension_semantics=("arbitrary",),
                                             vmem_limit_bytes=VMEM_LIMIT),
        name="ffn",
    )(*ins)


def _branches(proj, rw_rk, rw_gn_g, rw_gn_b, hsum, bsz, seq):
    r, k, v, lw, kk, b, g, q, ks, vs = proj
    seq3 = lambda a: a.reshape(bsz, seq, -1)
    ya = _rwkv(seq3(r), seq3(k), seq3(v), seq3(lw), seq3(kk), seq3(b), seq3(g),
               rw_rk, rw_gn_g, rw_gn_b, hsum, chunk=min(RWKV_CHUNK, seq),
               rows=min(RWKV_CHUNKS_PER_STEP * RWKV_CHUNK, seq))
    o = _sb_attention(seq3(q), seq3(ks), seq3(vs), blk=min(SB_BLOCK, seq))
    return ya.reshape(bsz * seq, -1), o.reshape(bsz * seq, -1)


def _layer(x, c, w_ada, b_ada, w_in, mu_shift, rw_w0, rw_w2, rw_a0, rw_a2, rw_g2, rw_kk, rw_ka,
           rw_rk, rw_gn_g, rw_gn_b, w_branch_a, w_branch_b, w_out, ln1_g, ln1_b,
           w_ff1, b_ff1, w_ff2, b_ff2, ln2_g, ln2_b, *, alpha):
    bsz, seq, d = x.shape
    d_rwkv = rw_w0.shape[-1]
    tm = min(INPROJ_ROWS, seq)
    tm2 = min(MLP_ROWS, seq)
    rwkv_rows = min(RWKV_CHUNKS_PER_STEP * RWKV_CHUNK, seq)
    group_w = RWKV_GROUP_HEADS * HEAD_DIM
    assert seq % tm == 0 and seq % tm2 == 0 and seq % rwkv_rows == 0 and seq % min(SB_BLOCK, seq) == 0, seq
    assert rwkv_rows % (RWKV_SECTIONS * RWKV_CHUNK) == 0, (rwkv_rows, RWKV_SECTIONS)
    assert d_rwkv % group_w == 0 and w_branch_b.shape[0] % (2 * HEAD_DIM) == 0, (d_rwkv, w_branch_b.shape)
    assert HEAD_DIM & (HEAD_DIM - 1) == 0 and RWKV_CHUNK & (RWKV_CHUNK - 1) == 0

    ada3 = _ada(c, w_ada, b_ada).reshape(bsz, 6, d)
    lane = jnp.arange(RWKV_GROUP_HEADS * HEAD_DIM) // HEAD_DIM
    hsum = (lane[:, None] == lane[None, :]).astype(BF16)

    x2 = x.reshape(bsz * seq, d)
    (r, k, v, lw, kk, b, g, q, ks, vs, ga, gb) = _inproj(
        x2, ada3, w_in.astype(BF16), mu_shift, rw_w0, rw_w2, rw_a0, rw_a2, rw_g2, rw_kk, rw_ka, hsum,
        seq=seq, tm=tm)

    ya, ob = _branches((r, k, v, lw, kk, b, g, q, ks, vs), rw_rk, rw_gn_g, rw_gn_b, hsum, bsz, seq)

    x1 = _merge(x2, ada3, ya, ob, ga, gb,
                w_branch_a.astype(BF16), w_branch_b.astype(BF16), w_out.astype(BF16),
                ln1_g, ln1_b, seq=seq, tm=tm2, alpha=alpha)
    out = _ffn(x1, ada3, w_ff1.astype(BF16), b_ff1, w_ff2.astype(BF16), b_ff2, ln2_g, ln2_b,
               seq=seq, tm=tm2, alpha=alpha, fchunk=min(1024, w_ff1.shape[-1]))
    return out.reshape(bsz, seq, d)


def kernel(x, c, w_ada, b_ada, w_in, mu_shift, rw_w0, rw_w2, rw_a0, rw_a2, rw_g2, rw_kk, rw_ka, rw_rk,
           rw_gn_g, rw_gn_b, w_branch_a, w_branch_b, w_out, ln1_g, ln1_b, w_ff1, b_ff1, w_ff2, b_ff2,
           ln2_g, ln2_b):
    in_dtype = x.dtype
    layer_params = (w_ada, b_ada, w_in, mu_shift, rw_w0, rw_w2, rw_a0, rw_a2, rw_g2, rw_kk, rw_ka,
                    rw_rk, rw_gn_g, rw_gn_b, w_branch_a, w_branch_b, w_out, ln1_g, ln1_b,
                    w_ff1, b_ff1, w_ff2, b_ff2, ln2_g, ln2_b)
    depth = w_ada.shape[0]
    alpha = (2.0 * depth) ** 0.25
    for l in range(depth):
        x = _layer(x, c, *[p[l] for p in layer_params], alpha=alpha)
    return x.astype(in_dtype)
```

```python
import functools

import jax
import jax.numpy as jnp
from jax import lax
from jax.experimental import pallas as pl
from jax.experimental.pallas import tpu as pltpu

F32 = jnp.float32
BF16 = jnp.bfloat16

HEAD_DIM = 64
LN_EPS = 1e-5
ADALN_EPS = 1e-6
GN_EPS = 64e-5
RWKV_CHUNK = 64
RWKV_SECTIONS = 2
RWKV_SECTION_CHUNKS = 4
RWKV_SECTION_LAG = 12
RWKV_CHUNKS_PER_STEP = RWKV_SECTIONS * RWKV_SECTION_CHUNKS
SB_BLOCK = 256
INPROJ_ROWS = 512
MLP_ROWS = 1024
VMEM_LIMIT = 56 * 1024 * 1024


def _split2(a):
    hi = a.astype(BF16)
    lo = (a - hi.astype(F32)).astype(BF16)
    return hi, lo


_NN = (((1,), (0,)), ((), ()))
_NT = (((1,), (1,)), ((), ()))
_TN = (((0,), (0,)), ((), ()))


def _dot(a, b, dims=_NN):
    return lax.dot_general(a, b, dims, preferred_element_type=F32)


def _head_sums(x, ones_blk):
    m, w = x.shape[0], ones_blk.shape[0]
    hi, lo = _split2(x)
    outs = []
    for g in range(x.shape[1] // w):
        cols = slice(g * w, (g + 1) * w)
        both = _dot(jnp.concatenate([hi[:, cols], lo[:, cols]], axis=0), ones_blk)
        outs.append(both[:m] + both[m:])
    return jnp.concatenate(outs, axis=1)


def _dot_ones_lhs(a_bf, b):
    hi, lo = _split2(b)
    return _dot(a_bf, hi) + _dot(a_bf, lo)


_DONE = object()


def _run_skewed(programs, lag):
    live = [True] * len(programs)
    tick = 0
    while any(live):
        for i, prog in enumerate(programs):
            if live[i] and tick >= i * lag:
                live[i] = next(prog, _DONE) is not _DONE
        tick += 1


_NEG_LOG2_E = -1.4426950408889634


def _softplus(y):
    return jnp.maximum(y, 0.0) + jnp.log(1.0 + jnp.exp2(jnp.abs(y) * _NEG_LOG2_E))


def _sigmoid(y):
    return 1.0 / (1.0 + jnp.exp(-y))


def _norm_rows(x, eps):
    mu = jnp.mean(x, axis=-1, keepdims=True)
    xc = x - mu
    var = jnp.mean(xc * xc, axis=-1, keepdims=True)
    return xc * lax.rsqrt(var + eps)


def _ada_kernel(c_ref, w_ref, b_ref, o_ref):
    c = c_ref[...]
    s = c * _sigmoid(c)
    m = s.shape[0]
    sh, sl = _split2(s)
    wh, wl = _split2(w_ref[...])
    top = _dot(jnp.concatenate([sh, sl], axis=0), wh)
    o_ref[...] = top[:m] + (top[m:] + _dot(sh, wl)) + b_ref[...]


def _ada(c, w_ada, b_ada):
    bsz, d = c.shape
    n = w_ada.shape[1]
    wide = 2 * d if n % (2 * d) == 0 else d
    return pl.pallas_call(
        _ada_kernel,
        grid=(n // wide,),
        in_specs=[pl.BlockSpec((bsz, d), lambda j: (0, 0)),
                  pl.BlockSpec((d, wide), lambda j: (0, j)),
                  pl.BlockSpec((1, wide), lambda j: (0, j))],
        out_specs=pl.BlockSpec((bsz, wide), lambda j: (0, j)),
        out_shape=jax.ShapeDtypeStruct((bsz, n), F32),
        name="ada",
    )(c, w_ada, b_ada.reshape(1, n))


def _inproj_kernel(x_ref, ada_ref, win_ref, mu_ref, w0_ref, w2_ref, a0_ref, a2_ref, g2_ref,
                   kkw_ref, kaw_ref, hsum_ref,
                   r_ref, k_ref, v_ref, lw_ref, kk_ref, b_ref, g_ref,
                   q_ref, ks_ref, vs_ref, ga_ref, gb_ref,
                   carry_ref, *, tiles_per_batch, d_rwkv, d_decay, d_aaa, d_gate, d_sb, d_model):
    i = pl.program_id(0)
    tm = x_ref.shape[0]
    d_shift = 3 * d_rwkv + d_decay + d_aaa + d_gate

    sh = ada_ref[0, 0:1, :]
    sc = ada_ref[0, 1:2, :]

    @pl.when(i % tiles_per_batch == 0)
    def _():
        carry_ref[...] = jnp.zeros_like(carry_ref)

    half = d_model // 2
    pair_w = 2 * HEAD_DIM
    first_head = lax.broadcasted_iota(jnp.int32, (1, pair_w), 1) < HEAD_DIM

    def rows_program(rows):
        hm = rows.stop - rows.start
        hb = (_norm_rows(x_ref[rows, :], ADALN_EPS) * (1.0 + sc) + sh).astype(BF16)
        row0 = lax.broadcasted_iota(jnp.int32, (hm, 1), 0) == 0
        yield

        def shifted(lo, width):
            cols = slice(lo, lo + width)
            z = _dot(hb, win_ref[:, cols])
            prev = jnp.where(row0, carry_ref[:, cols], pltpu.roll(z, 1, 0))
            carry_ref[:, cols] = z[hm - 1:hm, :]
            return z + mu_ref[:, cols] * (prev - z)

        def store_queries(z):
            zq = (z * (HEAD_DIM ** -0.5)).astype(BF16)
            zero = jnp.zeros((), BF16)
            for p in range(d_sb // pair_w):
                qp = zq[:, p * pair_w:(p + 1) * pair_w]
                q_ref[rows, (2 * p) * pair_w:(2 * p + 1) * pair_w] = jnp.where(first_head, qp, zero)
                q_ref[rows, (2 * p + 1) * pair_w:(2 * p + 2) * pair_w] = jnp.where(first_head, zero, qp)

        def plain(ref, cols, fn=lambda z: z):
            def store(z):
                ref[rows, cols] = fn(z).astype(BF16)
            return store

        all_cols = slice(0, d_sb)
        rest = [(store_queries, d_sb), (plain(ks_ref, all_cols), d_sb), (plain(vs_ref, all_cols), d_sb),
                (plain(ga_ref, slice(0, half), _sigmoid), half),
                (plain(ga_ref, slice(half, d_model), _sigmoid), half),
                (plain(gb_ref, slice(0, half), _sigmoid), half),
                (plain(gb_ref, slice(half, d_model), _sigmoid), half)]
        rest_col = [d_shift]

        def project_next():
            store, width = rest.pop(0)
            store(_dot(hb, win_ref[:, rest_col[0]:rest_col[0] + width]))
            rest_col[0] += width

        zl = shifted(3 * d_rwkv, d_decay + d_aaa + d_gate)
        zw = zl[:, :d_decay]
        za = zl[:, d_decay:d_decay + d_aaa]
        zg = zl[:, d_decay + d_aaa:]
        k = shifted(d_rwkv, d_rwkv)
        yield
        ww = w0_ref[...] + _dot(jnp.tanh(zw).astype(BF16), w2_ref[...])
        w_log = -_softplus(-ww) - 0.5
        lw_ref[rows, :] = -jnp.exp(w_log)
        project_next()
        yield
        arate = _sigmoid(a0_ref[...] + _dot(za.astype(BF16), a2_ref[...]))
        g_ref[rows, :] = _dot(_sigmoid(zg).astype(BF16), g2_ref[...])
        r_ref[rows, :] = shifted(0, d_rwkv)
        yield
        kkraw = k * kkw_ref[...]
        ssq = _head_sums(kkraw * kkraw, hsum_ref[...])
        kk = kkraw / jnp.maximum(jnp.sqrt(ssq), 1e-12)
        v_ref[rows, :] = shifted(2 * d_rwkv, d_rwkv)
        k_ref[rows, :] = k * (1.0 + (arate - 1.0) * kaw_ref[...])
        yield
        project_next()
        kk_ref[rows, :] = kk
        b_ref[rows, :] = kk * arate
        yield
        while rest:
            project_next()
            yield

    hm = tm // 2
    _run_skewed([rows_program(slice(0, hm)), rows_program(slice(hm, tm))], lag=2)


def _inproj(x2, ada3, win_bf, mu_shift, rw_w0, rw_w2, rw_a0, rw_a2, rw_g2, rw_kk, rw_ka, hsum,
            *, seq, tm):
    m, d_model = x2.shape
    d_rwkv = rw_w0.shape[-1]
    d_decay, d_aaa, d_gate = rw_w2.shape[0], rw_a2.shape[0], rw_g2.shape[0]
    d_shift = 3 * d_rwkv + d_decay + d_aaa + d_gate
    d_sb = (win_bf.shape[1] - d_shift - 2 * d_model) // 3
    tiles_per_batch = seq // tm
    row = lambda i: (i, 0)
    const = lambda i: (0, 0)
    full = lambda a: pl.BlockSpec(a.shape, const, pipeline_mode=pl.Buffered(1))
    vec = lambda a: a.reshape(1, -1)
    ins = [x2, ada3, win_bf, vec(mu_shift), vec(rw_w0), rw_w2.astype(BF16), vec(rw_a0),
           rw_a2.astype(BF16), rw_g2.astype(BF16), vec(rw_kk), vec(rw_ka), hsum]
    in_specs = [pl.BlockSpec((tm, d_model), row),
                pl.BlockSpec((1,) + ada3.shape[1:], lambda i: (i // tiles_per_batch, 0, 0))]
    in_specs += [full(a) for a in ins[2:]]
    widths = [d_rwkv] * 7 + [2 * d_sb, d_sb, d_sb] + [d_model] * 2
    dtypes = [F32] * 7 + [BF16] * 5
    out_shape = [jax.ShapeDtypeStruct((m, w), dt) for w, dt in zip(widths, dtypes)]
    out_specs = [pl.BlockSpec((tm, w), row) for w in widths]
    kern = functools.partial(_inproj_kernel, tiles_per_batch=tiles_per_batch, d_rwkv=d_rwkv,
                             d_decay=d_decay, d_aaa=d_aaa, d_gate=d_gate, d_sb=d_sb, d_model=d_model)
    return pl.pallas_call(
        kern,
        grid=(m // tm,),
        in_specs=in_specs,
        out_specs=out_specs,
        out_shape=out_shape,
        scratch_shapes=[pltpu.VMEM((1, d_shift), F32)],
        compiler_params=pltpu.CompilerParams(dimension_semantics=("arbitrary",),
                                             vmem_limit_bytes=VMEM_LIMIT),
        name="inproj",
    )(*ins)


RWKV_GROUP_HEADS = 4


def _block_diag(q, head_masks):
    zero = jnp.zeros((), q.dtype)
    return jnp.concatenate([jnp.where(m, q, zero) for m in head_masks], axis=0)


def _packed_dot1(lhs, q, head_masks, dims=_NN):
    return _dot(lhs.astype(BF16), _block_diag(q.astype(BF16), head_masks), dims)


def _packed_dot_wide_lhs(lhs, q, head_masks, dims=_NN):
    m = lhs.shape[0]
    lh, ll = _split2(lhs)
    both = _dot(jnp.concatenate([lh, ll], axis=0), _block_diag(q.astype(BF16), head_masks), dims)
    return both[:m] + both[m:]


def _rwkv_kernel(r_ref, k_ref, v_ref, lw_ref, kk_ref, b_ref, g_ref, rk_ref, gng_ref, gnb_ref, hsum_ref,
                 o_ref, state_ref, *, chunk):
    c = pl.program_id(1)
    n = HEAD_DIM
    rows, d = r_ref.shape[1], r_ref.shape[2]
    gw = RWKV_GROUP_HEADS * n
    n_groups = d // gw

    @pl.when(c == 0)
    def _():
        state_ref[...] = jnp.zeros_like(state_ref)

    t_idx = lax.broadcasted_iota(jnp.int32, (chunk, gw), 0)
    lane = lax.broadcasted_iota(jnp.int32, (chunk, gw), 1)
    i_idx = lane & (n - 1)
    strict = t_idx > i_idx
    incl = t_idx >= i_idx
    lane1 = lax.broadcasted_iota(jnp.int32, (1, gw), 1)
    hm = [(lane1 >> (n.bit_length() - 1)) == h for h in range(RWKV_GROUP_HEADS)]
    same = lambda s: (t_idx >> s) == (i_idx >> s)

    half_w = gw // 2
    lane_half = [(lane1 >> (half_w.bit_length() - 1)) == j for j in range(2)]
    second_in_half = ((lane >> (n.bit_length() - 1)) & 1) == 1

    def halves_rows(x):
        return jnp.concatenate([x[:, :half_w], x[:, half_w:]], axis=0).astype(BF16)

    def halves_masked(w):
        return _block_diag(w.astype(BF16), lane_half)

    def same_head(out):
        return jnp.where(second_in_half, out[n:], out[:n])

    lane_h = lax.broadcasted_iota(jnp.int32, (1, half_w), 1)
    in_half = [(lane_h >> (n.bit_length() - 1)) == j for j in range(half_w // n)]

    def two_rhs(lhs, x, y, dims=_NN):
        lb, xb, yb = lhs.astype(BF16), x.astype(BF16), y.astype(BF16)
        out_x, out_y = [], []
        for j in range(gw // half_w):
            cols = slice(j * half_w, (j + 1) * half_w)
            w = jnp.concatenate([_block_diag(xb[:, cols], in_half), _block_diag(yb[:, cols], in_half)],
                                axis=1 if dims == _NN else 0)
            o = _dot(lb[:, cols], w, dims)
            out_x.append(o[:, :half_w])
            out_y.append(o[:, half_w:])
        return jnp.concatenate(out_x, axis=1), jnp.concatenate(out_y, axis=1)

    sec_rows = rows // RWKV_SECTIONS
    sec_chunks = sec_rows // chunk
    rr = lax.broadcasted_iota(jnp.int32, (sec_rows, sec_rows), 0)
    cc = lax.broadcasted_iota(jnp.int32, (sec_rows, sec_rows), 1)
    cs = chunk.bit_length() - 1
    ltri = jnp.where((rr >= cc) & ((rr >> cs) == (cc >> cs)), 1.0, 0.0).astype(BF16)
    eye = jnp.where(t_idx == i_idx, 1.0, 0.0)
    chains = [(ci, gi) for ci in range(sec_chunks) for gi in range(n_groups)]
    win = lambda x, ci, gi: x[ci * chunk:(ci + 1) * chunk, gi * gw:(gi + 1) * gw]

    def section(row0):
        rs = slice(row0, row0 + sec_rows)
        r, k, v, lw = r_ref[0, rs, :], k_ref[0, rs, :], v_ref[0, rs, :], lw_ref[0, rs, :]
        kk, b = kk_ref[0, rs, :], b_ref[0, rs, :]
        cum = _dot_ones_lhs(ltri, lw)
        a_t = -kk * jnp.exp(cum - lw)
        r_t = r * jnp.exp(cum)
        e_neg = jnp.exp(-cum)
        b_t = b * e_neg
        k_t = k * e_neg
        tots = [cum[(ci + 1) * chunk - 1:(ci + 1) * chunk, :] for ci in range(sec_chunks)]
        e_bars = [jnp.exp(tots[ci] - cum[ci * chunk:(ci + 1) * chunk, :]) for ci in range(sec_chunks)]
        b_bar = [win(b, ci, gi) * e_bars[ci][:, gi * gw:(gi + 1) * gw] for ci, gi in chains]
        k_bar = [win(k, ci, gi) * e_bars[ci][:, gi * gw:(gi + 1) * gw] for ci, gi in chains]
        at = [win(a_t, ci, gi) for ci, gi in chains]
        rt = [win(r_t, ci, gi) for ci, gi in chains]
        vv = [win(v, ci, gi) for ci, gi in chains]
        yield

        ar = [jnp.concatenate([a_, r_], axis=0) for a_, r_ in zip(at, rt)]
        grams = [two_rhs(x, win(b_t, ci, gi), win(k_t, ci, gi), _NT) for x, (ci, gi) in zip(ar, chains)]
        gram_b = [x[0] for x in grams]
        gram_k = [x[1] for x in grams]
        yield
        m_ab = [jnp.where(strict, g_[:chunk], 0.0) for g_ in gram_b]
        p_rb = [jnp.where(incl, g_[chunk:], 0.0) for g_ in gram_b]
        m_ak = [jnp.where(strict, g_[:chunk], 0.0) for g_ in gram_k]
        p_rk = [jnp.where(incl, g_[chunk:], 0.0) for g_ in gram_k]
        yield

        ts = [eye + jnp.where(same(1), m, 0.0) for m in m_ab]
        s = 1
        while (1 << s) < chunk:
            level = same(s + 1) & jnp.logical_not(same(s))
            offs = [jnp.where(level, m, 0.0) for m in m_ab]
            mids = [_packed_dot1(t, o, hm) for t, o in zip(ts, offs)]
            yield
            ts = [t + _packed_dot1(md, t, hm) for t, md in zip(ts, mids)]
            yield
            s += 1

        both = [_packed_dot1(jnp.concatenate([m, pk], axis=0), x, hm) for m, pk, x in zip(m_ak, p_rk, vv)]
        mv = [x[:chunk] for x in both]
        pkv = [x[chunk:] for x in both]
        yield
        hat = [two_rhs(t, x, y) for t, x, y in zip(ts, at, mv)]
        a_hat = [x[0] for x in hat]
        u0 = [x[1] for x in hat]
        yield
        pb = [two_rhs(p, a, u) for p, a, u in zip(p_rb, a_hat, u0)]
        r_hat = [x + y[0] for x, y in zip(rt, pb)]
        y0 = [x + y[1] for x, y in zip(pkv, pb)]
        yield
        g_mat = [same_head(_dot(halves_rows(a), halves_masked(bb), _TN))
                 for a, bb in zip(a_hat, b_bar)]
        h_mat = [same_head(_dot(jnp.concatenate([halves_rows(u), halves_rows(x)], axis=0),
                                jnp.concatenate([halves_masked(bb), halves_masked(kb)], axis=0), _TN))
                 for u, x, bb, kb in zip(u0, vv, b_bar, k_bar)]
        yield

        y_rows = []
        for ci in range(sec_chunks):
            y_groups = []
            for gi in range(n_groups):
                j = ci * n_groups + gi
                s0 = state_ref[gi]
                y_groups.append(_packed_dot1(r_hat[j], s0, hm, _NT) + y0[j])
                w_tot = jnp.exp(tots[ci][:, gi * gw:(gi + 1) * gw])
                state_ref[gi] = s0 * w_tot + _packed_dot_wide_lhs(s0, g_mat[j], hm) + h_mat[j]
            y_rows.append(jnp.concatenate(y_groups, axis=1))
            yield
        y = jnp.concatenate(y_rows, axis=0)

        hsum = hsum_ref[...]
        inv_n = 1.0 / n
        mu = _head_sums(y, hsum) * inv_n
        yc = y - mu
        var = _head_sums(yc * yc, hsum) * inv_n
        yn = yc * lax.rsqrt(var + GN_EPS) * gng_ref[...] + gnb_ref[...]
        bonus = _head_sums(r * k * rk_ref[...], hsum)
        o_ref[0, rs, :] = ((yn + bonus * v) * g_ref[0, rs, :]).astype(o_ref.dtype)

    assert RWKV_SECTION_LAG >= sec_chunks
    _run_skewed([section(si * sec_rows) for si in range(RWKV_SECTIONS)], lag=RWKV_SECTION_LAG)


def _rwkv(r, k, v, lw, kk, b, g, rw_rk, gn_g, gn_b, hsum, *, chunk, rows):
    bsz, seq, d = r.shape
    n_groups = d // (RWKV_GROUP_HEADS * HEAD_DIM)
    blk = pl.BlockSpec((1, rows, d), lambda bi, ci: (bi, ci, 0))
    const = lambda bi, ci: (0, 0)
    vec = lambda a: a.reshape(1, d)
    return pl.pallas_call(
        functools.partial(_rwkv_kernel, chunk=chunk),
        grid=(bsz, seq // rows),
        in_specs=[blk] * 7 + [pl.BlockSpec((1, d), const)] * 3 + [pl.BlockSpec(hsum.shape, const)],
        out_specs=blk,
        out_shape=jax.ShapeDtypeStruct((bsz, seq, d), BF16),
        scratch_shapes=[pltpu.VMEM((n_groups, HEAD_DIM, RWKV_GROUP_HEADS * HEAD_DIM), F32)],
        compiler_params=pltpu.CompilerParams(dimension_semantics=("arbitrary", "arbitrary"),
                                             vmem_limit_bytes=VMEM_LIMIT),
        name="rwkv",
    )(r, k, v, lw, kk, b, g, vec(rw_rk), vec(gn_g), vec(gn_b), hsum)


SB_LOG_UNDERFLOW = 105.0


def _sb_kernel(q_ref, k_ref, v_ref, o_ref, acc_ref, run_ref, *, blk):
    qi = pl.program_id(1)
    pairs = k_ref.shape[2] // (2 * HEAD_DIM)
    row = lax.broadcasted_iota(jnp.int32, (blk, blk), 0)
    col = lax.broadcasted_iota(jnp.int32, (blk, blk), 1)
    causal = col < row
    jj = lax.broadcasted_iota(jnp.int32, (blk, 2 * blk), 0)
    ss = lax.broadcasted_iota(jnp.int32, (blk, 2 * blk), 1)
    suffix = jnp.where((jj >= ss) | (ss >= blk), 1.0, 0.0).astype(BF16)
    first_head = lax.broadcasted_iota(jnp.int32, (1, 2 * HEAD_DIM), 1) < HEAD_DIM
    zero = jnp.zeros((), BF16)

    lanes = [slice(p * 2 * HEAD_DIM, (p + 1) * 2 * HEAD_DIM) for p in range(pairs)]

    pw = 2 * HEAD_DIM

    def tile(kb, diagonal):
        start = pl.multiple_of(kb * blk, blk)
        zs = []
        for p in range(pairs):
            q2 = jnp.concatenate([q_ref[0, :, (2 * p + hh) * pw:(2 * p + hh + 1) * pw] for hh in range(2)],
                                 axis=0)
            z2 = _dot(q2, k_ref[0, pl.ds(start, blk), lanes[p]], _NT)
            zs += [z2[:blk], z2[blk:]]
        sps = [_softplus(z) for z in zs]
        if diagonal:
            sps = [jnp.where(causal, sp, 0.0) for sp in sps]
        css = []
        for p in range(pairs):
            c2 = _dot(jnp.concatenate(sps[2 * p:2 * p + 2], axis=0).astype(BF16), suffix)
            css += [c2[:blk], c2[blk:]]
        atts = []
        low = None
        for n_, (z, cs) in enumerate(zip(zs, css)):
            logw = z - cs[:, :blk]
            if diagonal:
                att = jnp.where(causal, jnp.exp(logw), 0.0)
                run = cs[:, blk:]
            else:
                prev = run_ref[n_]
                att = jnp.exp(logw - prev)
                run = prev + cs[:, blk:]
            run_ref[n_] = run
            low = run if low is None else jnp.minimum(low, run)
            atts.append(att.astype(BF16))
        done = (jnp.min(low) > SB_LOG_UNDERFLOW).astype(jnp.int32)
        for p in range(pairs):
            vp = v_ref[0, pl.ds(start, blk), lanes[p]]
            vv = jnp.concatenate([jnp.where(first_head, vp, zero),
                                  jnp.where(first_head, zero, vp)], axis=0)
            out = _dot(jnp.concatenate(atts[2 * p:2 * p + 2], axis=1), vv)
            if diagonal:
                acc_ref[:, lanes[p]] = out
            else:
                acc_ref[:, lanes[p]] += out
        return done

    done0 = tile(qi, True)

    def cond(c):
        i, done = c
        return jnp.logical_and(i <= qi, done == 0)

    def body(c):
        i, _ = c
        return i + 1, tile(qi - i, False)

    lax.while_loop(cond, body, (jnp.int32(1), done0))
    o_ref[0] = acc_ref[...].astype(o_ref.dtype)


def _sb_attention(q, k, v, *, blk):
    bsz, seq, d = k.shape
    heads = d // HEAD_DIM
    qspec = pl.BlockSpec((1, blk, 2 * d), lambda b, i: (b, i, 0))
    ospec = pl.BlockSpec((1, blk, d), lambda b, i: (b, i, 0))
    kvspec = pl.BlockSpec((1, seq, d), lambda b, i: (b, 0, 0))
    return pl.pallas_call(
        functools.partial(_sb_kernel, blk=blk),
        grid=(bsz, seq // blk),
        in_specs=[qspec, kvspec, kvspec],
        out_specs=ospec,
        out_shape=jax.ShapeDtypeStruct((bsz, seq, d), BF16),
        scratch_shapes=[pltpu.VMEM((blk, d), F32), pltpu.VMEM((heads, blk, blk), F32)],
        compiler_params=pltpu.CompilerParams(dimension_semantics=("arbitrary", "arbitrary"),
                                             vmem_limit_bytes=VMEM_LIMIT),
        name="sbattn",
    )(q, k, v)


def _merge_kernel(x_ref, ada_ref, ya_ref, ob_ref, ga_ref, gb_ref, wa_ref, wb_ref, wo_ref,
                  lng_ref, lnb_ref, o_ref, *, alpha):
    g1 = ada_ref[0, 2:3, :]
    half = x_ref.shape[0] // 2

    def rows_program(rows):
        y_a = _dot(ya_ref[rows, :], wa_ref[...])
        y_b = _dot(ob_ref[rows, :], wb_ref[...])
        yield
        merged = ga_ref[rows, :].astype(F32) * y_a + gb_ref[rows, :].astype(F32) * y_b
        mix = _dot(merged.astype(BF16), wo_ref[...])
        yield
        o_ref[rows, :] = _norm_rows(alpha * x_ref[rows, :] + g1 * mix, LN_EPS) * lng_ref[...] + lnb_ref[...]

    _run_skewed([rows_program(slice(0, half)), rows_program(slice(half, 2 * half))], lag=1)


def _merge(x2, ada3, ya, ob, ga, gb, wa_bf, wb_bf, wo_bf, ln_g, ln_b, *, seq, tm, alpha):
    m, d = x2.shape
    tiles_per_batch = seq // tm
    row = lambda i: (i, 0)
    const = lambda i: (0, 0)
    rows = lambda a: pl.BlockSpec((tm, a.shape[1]), row)
    full = lambda a: pl.BlockSpec(a.shape, const, pipeline_mode=pl.Buffered(1))
    vec = lambda a: a.reshape(1, -1)
    ins = [x2, ada3, ya, ob, ga, gb, wa_bf, wb_bf, wo_bf, vec(ln_g), vec(ln_b)]
    in_specs = [rows(x2), pl.BlockSpec((1,) + ada3.shape[1:], lambda i: (i // tiles_per_batch, 0, 0)),
                rows(ya), rows(ob), rows(ga), rows(gb)] + [full(a) for a in ins[6:]]
    return pl.pallas_call(
        functools.partial(_merge_kernel, alpha=alpha),
        grid=(m // tm,),
        in_specs=in_specs,
        out_specs=pl.BlockSpec((tm, d), row),
        out_shape=jax.ShapeDtypeStruct((m, d), F32),
        compiler_params=pltpu.CompilerParams(dimension_semantics=("arbitrary",),
                                             vmem_limit_bytes=VMEM_LIMIT),
        name="merge",
    )(*ins)


def _ffn_kernel(x_ref, ada_ref, w1_ref, b1_ref, w2_ref, b2_ref, lng_ref, lnb_ref, o_ref, *, alpha, fchunk):
    sh = ada_ref[0, 3:4, :]
    sc = ada_ref[0, 4:5, :]
    g2 = ada_ref[0, 5:6, :]
    d_ff = w1_ref.shape[1]
    n_chunks = d_ff // fchunk
    half = x_ref.shape[0] // 2

    def rows_program(rows):
        x = x_ref[rows, :]
        hb = (_norm_rows(x, ADALN_EPS) * (1.0 + sc) + sh).astype(BF16)
        yield
        ff = None
        for j in range(n_chunks):
            sl = slice(j * fchunk, (j + 1) * fchunk)
            t = jnp.maximum(_dot(hb, w1_ref[:, sl]) + b1_ref[:, sl], 0.0)
            part = _dot((t * t).astype(BF16), w2_ref[sl, :])
            ff = part if ff is None else ff + part
            yield
        ff = ff + b2_ref[...]
        o_ref[rows, :] = _norm_rows(alpha * x + g2 * ff, LN_EPS) * lng_ref[...] + lnb_ref[...]

    _run_skewed([rows_program(slice(0, half)), rows_program(slice(half, 2 * half))], lag=2)


def _ffn(x1, ada3, w1_bf, b1, w2_bf, b2, ln_g, ln_b, *, seq, tm, alpha, fchunk):
    m, d = x1.shape
    tiles_per_batch = seq // tm
    row = lambda i: (i, 0)
    const = lambda i: (0, 0)
    full = lambda a: pl.BlockSpec(a.shape, const, pipeline_mode=pl.Buffered(1))
    vec = lambda a: a.reshape(1, -1)
    ins = [x1, ada3, w1_bf, vec(b1), w2_bf, vec(b2), vec(ln_g), vec(ln_b)]
    in_specs = [pl.BlockSpec((tm, d), row),
                pl.BlockSpec((1,) + ada3.shape[1:], lambda i: (i // tiles_per_batch, 0, 0))]
    in_specs += [full(a) for a in ins[2:]]
    return pl.pallas_call(
        functools.partial(_ffn_kernel, alpha=alpha, fchunk=fchunk),
        grid=(m // tm,),
        in_specs=in_specs,
        out_specs=pl.BlockSpec((tm, d), row),
        out_shape=jax.ShapeDtypeStruct((m, d), F32),
        compiler_params=pltpu.CompilerParams(dimension_semantics=("arbitrary",),
                                             vmem_limit_bytes=VMEM_LIMIT),
        name="ffn",
    )(*ins)


def _branches(proj, rw_rk, rw_gn_g, rw_gn_b, hsum, bsz, seq):
    r, k, v, lw, kk, b, g, q, ks, vs = proj
    seq3 = lambda a: a.reshape(bsz, seq, -1)
    ya = _rwkv(seq3(r), seq3(k), seq3(v), seq3(lw), seq3(kk), seq3(b), seq3(g),
               rw_rk, rw_gn_g, rw_gn_b, hsum, chunk=min(RWKV_CHUNK, seq),
               rows=min(RWKV_CHUNKS_PER_STEP * RWKV_CHUNK, seq))
    o = _sb_attention(seq3(q), seq3(ks), seq3(vs), blk=min(SB_BLOCK, seq))
    return ya.reshape(bsz * seq, -1), o.reshape(bsz * seq, -1)


def _layer(x, c, w_ada, b_ada, w_in, mu_shift, rw_w0, rw_w2, rw_a0, rw_a2, rw_g2, rw_kk, rw_ka,
           rw_rk, rw_gn_g, rw_gn_b, w_branch_a, w_branch_b, w_out, ln1_g, ln1_b,
           w_ff1, b_ff1, w_ff2, b_ff2, ln2_g, ln2_b, *, alpha):
    bsz, seq, d = x.shape
    d_rwkv = rw_w0.shape[-1]
    tm = min(INPROJ_ROWS, seq)
    tm2 = min(MLP_ROWS, seq)
    rwkv_rows = min(RWKV_CHUNKS_PER_STEP * RWKV_CHUNK, seq)
    group_w = RWKV_GROUP_HEADS * HEAD_DIM
    assert seq % tm == 0 and seq % tm2 == 0 and seq % rwkv_rows == 0 and seq % min(SB_BLOCK, seq) == 0, seq
    assert rwkv_rows % (RWKV_SECTIONS * RWKV_CHUNK) == 0, (rwkv_rows, RWKV_SECTIONS)
    assert d_rwkv % group_w == 0 and w_branch_b.shape[0] % (2 * HEAD_DIM) == 0, (d_rwkv, w_branch_b.shape)
    assert HEAD_DIM & (HEAD_DIM - 1) == 0 and RWKV_CHUNK & (RWKV_CHUNK - 1) == 0

    ada3 = _ada(c, w_ada, b_ada).reshape(bsz, 6, d)
    lane = jnp.arange(RWKV_GROUP_HEADS * HEAD_DIM) // HEAD_DIM
    hsum = (lane[:, None] == lane[None, :]).astype(BF16)

    x2 = x.reshape(bsz * seq, d)
    (r, k, v, lw, kk, b, g, q, ks, vs, ga, gb) = _inproj(
        x2, ada3, w_in.astype(BF16), mu_shift, rw_w0, rw_w2, rw_a0, rw_a2, rw_g2, rw_kk, rw_ka, hsum,
        seq=seq, tm=tm)

    ya, ob = _branches((r, k, v, lw, kk, b, g, q, ks, vs), rw_rk, rw_gn_g, rw_gn_b, hsum, bsz, seq)

    x1 = _merge(x2, ada3, ya, ob, ga, gb,
                w_branch_a.astype(BF16), w_branch_b.astype(BF16), w_out.astype(BF16),
                ln1_g, ln1_b, seq=seq, tm=tm2, alpha=alpha)
    out = _ffn(x1, ada3, w_ff1.astype(BF16), b_ff1, w_ff2.astype(BF16), b_ff2, ln2_g, ln2_b,
               seq=seq, tm=tm2, alpha=alpha, fchunk=min(1024, w_ff1.shape[-1]))
    return out.reshape(bsz, seq, d)


def kernel(x, c, w_ada, b_ada, w_in, mu_shift, rw_w0, rw_w2, rw_a0, rw_a2, rw_g2, rw_kk, rw_ka, rw_rk,
           rw_gn_g, rw_gn_b, w_branch_a, w_branch_b, w_out, ln1_g, ln1_b, w_ff1, b_ff1, w_ff2, b_ff2,
           ln2_g, ln2_b):
    in_dtype = x.dtype
    layer_params = (w_ada, b_ada, w_in, mu_shift, rw_w0, rw_w2, rw_a0, rw_a2, rw_g2, rw_kk, rw_ka,
                    rw_rk, rw_gn_g, rw_gn_b, w_branch_a, w_branch_b, w_out, ln1_g, ln1_b,
                    w_ff1, b_ff1, w_ff2, b_ff2, ln2_g, ln2_b)
    depth = w_ada.shape[0]
    alpha = (2.0 * depth) ** 0.25
    for l in range(depth):
        x = _layer(x, c, *[p[l] for p in layer_params], alpha=alpha)
    return x.astype(in_dtype)
```

```python
import functools

import jax
import jax.numpy as jnp
from jax import lax
from jax.experimental import pallas as pl
from jax.experimental.pallas import tpu as pltpu

F32 = jnp.float32
BF16 = jnp.bfloat16

HEAD_DIM = 64
LN_EPS = 1e-5
ADALN_EPS = 1e-6
GN_EPS = 64e-5
RWKV_CHUNK = 64
RWKV_SECTIONS = 2
RWKV_SECTION_CHUNKS = 4
RWKV_SECTION_LAG = 12
RWKV_CHUNKS_PER_STEP = RWKV_SECTIONS * RWKV_SECTION_CHUNKS
SB_BLOCK = 256
INPROJ_ROWS = 512
MLP_ROWS = 1024
VMEM_LIMIT = 56 * 1024 * 1024


def _split2(a):
    hi = a.astype(BF16)
    lo = (a - hi.astype(F32)).astype(BF16)
    return hi, lo


_NN = (((1,), (0,)), ((), ()))
_NT = (((1,), (1,)), ((), ()))
_TN = (((0,), (0,)), ((), ()))


def _dot(a, b, dims=_NN):
    return lax.dot_general(a, b, dims, preferred_element_type=F32)


def _head_sums(x, ones_blk):
    m, w = x.shape[0], ones_blk.shape[0]
    hi, lo = _split2(x)
    outs = []
    for g in range(x.shape[1] // w):
        cols = slice(g * w, (g + 1) * w)
        both = _dot(jnp.concatenate([hi[:, cols], lo[:, cols]], axis=0), ones_blk)
        outs.append(both[:m] + both[m:])
    return jnp.concatenate(outs, axis=1)


def _dot_ones_lhs(a_bf, b):
    hi, lo = _split2(b)
    return _dot(a_bf, hi) + _dot(a_bf, lo)


_DONE = object()


def _run_skewed(programs, lag):
    live = [True] * len(programs)
    tick = 0
    while any(live):
        for i, prog in enumerate(programs):
            if live[i] and tick >= i * lag:
                live[i] = next(prog, _DONE) is not _DONE
        tick += 1


_NEG_LOG2_E = -1.4426950408889634


def _softplus(y):
    return jnp.maximum(y, 0.0) + jnp.log(1.0 + jnp.exp2(jnp.abs(y) * _NEG_LOG2_E))


def _sigmoid(y):
    return 1.0 / (1.0 + jnp.exp(-y))


def _norm_rows(x, eps):
    mu = jnp.mean(x, axis=-1, keepdims=True)
    xc = x - mu
    var = jnp.mean(xc * xc, axis=-1, keepdims=True)
    return xc * lax.rsqrt(var + eps)


def _ada_kernel(c_ref, w_ref, b_ref, o_ref):
    c = c_ref[...]
    s = c * _sigmoid(c)
    m = s.shape[0]
    sh, sl = _split2(s)
    wh, wl = _split2(w_ref[...])
    top = _dot(jnp.concatenate([sh, sl], axis=0), wh)
    o_ref[...] = top[:m] + (top[m:] + _dot(sh, wl)) + b_ref[...]


def _ada(c, w_ada, b_ada):
    bsz, d = c.shape
    n = w_ada.shape[1]
    return pl.pallas_call(
        _ada_kernel,
        grid=(n // d,),
        in_specs=[pl.BlockSpec((bsz, d), lambda j: (0, 0)),
                  pl.BlockSpec((d, d), lambda j: (0, j)),
                  pl.BlockSpec((1, d), lambda j: (0, j))],
        out_specs=pl.BlockSpec((bsz, d), lambda j: (0, j)),
        out_shape=jax.ShapeDtypeStruct((bsz, n), F32),
        name="ada",
    )(c, w_ada, b_ada.reshape(1, n))


def _inproj_kernel(x_ref, ada_ref, win_ref, mu_ref, w0_ref, w2_ref, a0_ref, a2_ref, g2_ref,
                   kkw_ref, kaw_ref, hsum_ref,
                   r_ref, k_ref, v_ref, lw_ref, kk_ref, b_ref, g_ref,
                   q_ref, ks_ref, vs_ref, ga_ref, gb_ref,
                   carry_ref, *, tiles_per_batch, d_rwkv, d_decay, d_aaa, d_gate, d_sb, d_model):
    i = pl.program_id(0)
    tm = x_ref.shape[0]
    d_shift = 3 * d_rwkv + d_decay + d_aaa + d_gate

    sh = ada_ref[0, 0:1, :]
    sc = ada_ref[0, 1:2, :]

    @pl.when(i % tiles_per_batch == 0)
    def _():
        carry_ref[...] = jnp.zeros_like(carry_ref)

    half = d_model // 2
    pair_w = 2 * HEAD_DIM
    first_head = lax.broadcasted_iota(jnp.int32, (1, pair_w), 1) < HEAD_DIM

    def rows_program(rows):
        hm = rows.stop - rows.start
        hb = (_norm_rows(x_ref[rows, :], ADALN_EPS) * (1.0 + sc) + sh).astype(BF16)
        row0 = lax.broadcasted_iota(jnp.int32, (hm, 1), 0) == 0
        yield

        def shifted(lo, width):
            cols = slice(lo, lo + width)
            z = _dot(hb, win_ref[:, cols])
            prev = jnp.where(row0, carry_ref[:, cols], pltpu.roll(z, 1, 0))
            carry_ref[:, cols] = z[hm - 1:hm, :]
            return z + mu_ref[:, cols] * (prev - z)

        def store_queries(z):
            zq = (z * (HEAD_DIM ** -0.5)).astype(BF16)
            zero = jnp.zeros((), BF16)
            for p in range(d_sb // pair_w):
                qp = zq[:, p * pair_w:(p + 1) * pair_w]
                q_ref[rows, (2 * p) * pair_w:(2 * p + 1) * pair_w] = jnp.where(first_head, qp, zero)
                q_ref[rows, (2 * p + 1) * pair_w:(2 * p + 2) * pair_w] = jnp.where(first_head, zero, qp)

        def plain(ref, cols, fn=lambda z: z):
            def store(z):
                ref[rows, cols] = fn(z).astype(BF16)
            return store

        all_cols = slice(0, d_sb)
        rest = [(store_queries, d_sb), (plain(ks_ref, all_cols), d_sb), (plain(vs_ref, all_cols), d_sb),
                (plain(ga_ref, slice(0, half), _sigmoid), half),
                (plain(ga_ref, slice(half, d_model), _sigmoid), half),
                (plain(gb_ref, slice(0, half), _sigmoid), half),
                (plain(gb_ref, slice(half, d_model), _sigmoid), half)]
        rest_col = [d_shift]

        def project_next():
            store, width = rest.pop(0)
            store(_dot(hb, win_ref[:, rest_col[0]:rest_col[0] + width]))
            rest_col[0] += width

        zl = shifted(3 * d_rwkv, d_decay + d_aaa + d_gate)
        zw = zl[:, :d_decay]
        za = zl[:, d_decay:d_decay + d_aaa]
        zg = zl[:, d_decay + d_aaa:]
        k = shifted(d_rwkv, d_rwkv)
        yield
        ww = w0_ref[...] + _dot(jnp.tanh(zw).astype(BF16), w2_ref[...])
        w_log = -_softplus(-ww) - 0.5
        lw_ref[rows, :] = -jnp.exp(w_log)
        project_next()
        yield
        arate = _sigmoid(a0_ref[...] + _dot(za.astype(BF16), a2_ref[...]))
        g_ref[rows, :] = _dot(_sigmoid(zg).astype(BF16), g2_ref[...])
        r_ref[rows, :] = shifted(0, d_rwkv)
        yield
        kkraw = k * kkw_ref[...]
        ssq = _head_sums(kkraw * kkraw, hsum_ref[...])
        kk = kkraw / jnp.maximum(jnp.sqrt(ssq), 1e-12)
        v_ref[rows, :] = shifted(2 * d_rwkv, d_rwkv)
        k_ref[rows, :] = k * (1.0 + (arate - 1.0) * kaw_ref[...])
        yield
        project_next()
        kk_ref[rows, :] = kk
        b_ref[rows, :] = kk * arate
        yield
        while rest:
            project_next()
            yield

    hm = tm // 2
    _run_skewed([rows_program(slice(0, hm)), rows_program(slice(hm, tm))], lag=2)


def _inproj(x2, ada3, win_bf, mu_shift, rw_w0, rw_w2, rw_a0, rw_a2, rw_g2, rw_kk, rw_ka, hsum,
            *, seq, tm):
    m, d_model = x2.shape
    d_rwkv = rw_w0.shape[-1]
    d_decay, d_aaa, d_gate = rw_w2.shape[0], rw_a2.shape[0], rw_g2.shape[0]
    d_shift = 3 * d_rwkv + d_decay + d_aaa + d_gate
    d_sb = (win_bf.shape[1] - d_shift - 2 * d_model) // 3
    tiles_per_batch = seq // tm
    row = lambda i: (i, 0)
    const = lambda i: (0, 0)
    full = lambda a: pl.BlockSpec(a.shape, const, pipeline_mode=pl.Buffered(1))
    vec = lambda a: a.reshape(1, -1)
    ins = [x2, ada3, win_bf, vec(mu_shift), vec(rw_w0), rw_w2.astype(BF16), vec(rw_a0),
           rw_a2.astype(BF16), rw_g2.astype(BF16), vec(rw_kk), vec(rw_ka), hsum]
    in_specs = [pl.BlockSpec((tm, d_model), row),
                pl.BlockSpec((1,) + ada3.shape[1:], lambda i: (i // tiles_per_batch, 0, 0))]
    in_specs += [full(a) for a in ins[2:]]
    widths = [d_rwkv] * 7 + [2 * d_sb, d_sb, d_sb] + [d_model] * 2
    dtypes = [F32] * 7 + [BF16] * 5
    out_shape = [jax.ShapeDtypeStruct((m, w), dt) for w, dt in zip(widths, dtypes)]
    out_specs = [pl.BlockSpec((tm, w), row) for w in widths]
    kern = functools.partial(_inproj_kernel, tiles_per_batch=tiles_per_batch, d_rwkv=d_rwkv,
                             d_decay=d_decay, d_aaa=d_aaa, d_gate=d_gate, d_sb=d_sb, d_model=d_model)
    return pl.pallas_call(
        kern,
        grid=(m // tm,),
        in_specs=in_specs,
        out_specs=out_specs,
        out_shape=out_shape,
        scratch_shapes=[pltpu.VMEM((1, d_shift), F32)],
        compiler_params=pltpu.CompilerParams(dimension_semantics=("arbitrary",),
                                             vmem_limit_bytes=VMEM_LIMIT),
        name="inproj",
    )(*ins)


RWKV_GROUP_HEADS = 4


def _block_diag(q, head_masks):
    zero = jnp.zeros((), q.dtype)
    return jnp.concatenate([jnp.where(m, q, zero) for m in head_masks], axis=0)


def _packed_dot1(lhs, q, head_masks, dims=_NN):
    return _dot(lhs.astype(BF16), _block_diag(q.astype(BF16), head_masks), dims)


def _packed_dot_wide_lhs(lhs, q, head_masks, dims=_NN):
    m = lhs.shape[0]
    lh, ll = _split2(lhs)
    both = _dot(jnp.concatenate([lh, ll], axis=0), _block_diag(q.astype(BF16), head_masks), dims)
    return both[:m] + both[m:]


def _rwkv_kernel(r_ref, k_ref, v_ref, lw_ref, kk_ref, b_ref, g_ref, rk_ref, gng_ref, gnb_ref, hsum_ref,
                 o_ref, state_ref, *, chunk):
    c = pl.program_id(1)
    n = HEAD_DIM
    rows, d = r_ref.shape[1], r_ref.shape[2]
    gw = RWKV_GROUP_HEADS * n
    n_groups = d // gw

    @pl.when(c == 0)
    def _():
        state_ref[...] = jnp.zeros_like(state_ref)

    t_idx = lax.broadcasted_iota(jnp.int32, (chunk, gw), 0)
    lane = lax.broadcasted_iota(jnp.int32, (chunk, gw), 1)
    i_idx = lane & (n - 1)
    strict = t_idx > i_idx
    incl = t_idx >= i_idx
    lane1 = lax.broadcasted_iota(jnp.int32, (1, gw), 1)
    hm = [(lane1 >> (n.bit_length() - 1)) == h for h in range(RWKV_GROUP_HEADS)]
    same = lambda s: (t_idx >> s) == (i_idx >> s)

    half_w = gw // 2
    lane_half = [(lane1 >> (half_w.bit_length() - 1)) == j for j in range(2)]
    second_in_half = ((lane >> (n.bit_length() - 1)) & 1) == 1

    def halves_rows(x):
        return jnp.concatenate([x[:, :half_w], x[:, half_w:]], axis=0).astype(BF16)

    def halves_masked(w):
        return _block_diag(w.astype(BF16), lane_half)

    def same_head(out):
        return jnp.where(second_in_half, out[n:], out[:n])

    lane_h = lax.broadcasted_iota(jnp.int32, (1, half_w), 1)
    in_half = [(lane_h >> (n.bit_length() - 1)) == j for j in range(half_w // n)]

    def two_rhs(lhs, x, y, dims=_NN):
        lb, xb, yb = lhs.astype(BF16), x.astype(BF16), y.astype(BF16)
        out_x, out_y = [], []
        for j in range(gw // half_w):
            cols = slice(j * half_w, (j + 1) * half_w)
            w = jnp.concatenate([_block_diag(xb[:, cols], in_half), _block_diag(yb[:, cols], in_half)],
                                axis=1 if dims == _NN else 0)
            o = _dot(lb[:, cols], w, dims)
            out_x.append(o[:, :half_w])
            out_y.append(o[:, half_w:])
        return jnp.concatenate(out_x, axis=1), jnp.concatenate(out_y, axis=1)

    sec_rows = rows // RWKV_SECTIONS
    sec_chunks = sec_rows // chunk
    rr = lax.broadcasted_iota(jnp.int32, (sec_rows, sec_rows), 0)
    cc = lax.broadcasted_iota(jnp.int32, (sec_rows, sec_rows), 1)
    cs = chunk.bit_length() - 1
    ltri = jnp.where((rr >= cc) & ((rr >> cs) == (cc >> cs)), 1.0, 0.0).astype(BF16)
    eye = jnp.where(t_idx == i_idx, 1.0, 0.0)
    chains = [(ci, gi) for ci in range(sec_chunks) for gi in range(n_groups)]
    win = lambda x, ci, gi: x[ci * chunk:(ci + 1) * chunk, gi * gw:(gi + 1) * gw]

    def section(row0):
        rs = slice(row0, row0 + sec_rows)
        r, k, v, lw = r_ref[0, rs, :], k_ref[0, rs, :], v_ref[0, rs, :], lw_ref[0, rs, :]
        kk, b = kk_ref[0, rs, :], b_ref[0, rs, :]
        cum = _dot_ones_lhs(ltri, lw)
        a_t = -kk * jnp.exp(cum - lw)
        r_t = r * jnp.exp(cum)
        e_neg = jnp.exp(-cum)
        b_t = b * e_neg
        k_t = k * e_neg
        tots = [cum[(ci + 1) * chunk - 1:(ci + 1) * chunk, :] for ci in range(sec_chunks)]
        e_bars = [jnp.exp(tots[ci] - cum[ci * chunk:(ci + 1) * chunk, :]) for ci in range(sec_chunks)]
        b_bar = [win(b, ci, gi) * e_bars[ci][:, gi * gw:(gi + 1) * gw] for ci, gi in chains]
        k_bar = [win(k, ci, gi) * e_bars[ci][:, gi * gw:(gi + 1) * gw] for ci, gi in chains]
        at = [win(a_t, ci, gi) for ci, gi in chains]
        rt = [win(r_t, ci, gi) for ci, gi in chains]
        vv = [win(v, ci, gi) for ci, gi in chains]
        yield

        ar = [jnp.concatenate([a_, r_], axis=0) for a_, r_ in zip(at, rt)]
        grams = [two_rhs(x, win(b_t, ci, gi), win(k_t, ci, gi), _NT) for x, (ci, gi) in zip(ar, chains)]
        gram_b = [x[0] for x in grams]
        gram_k = [x[1] for x in grams]
        yield
        m_ab = [jnp.where(strict, g_[:chunk], 0.0) for g_ in gram_b]
        p_rb = [jnp.where(incl, g_[chunk:], 0.0) for g_ in gram_b]
        m_ak = [jnp.where(strict, g_[:chunk], 0.0) for g_ in gram_k]
        p_rk = [jnp.where(incl, g_[chunk:], 0.0) for g_ in gram_k]
        yield

        ts = [eye + jnp.where(same(1), m, 0.0) for m in m_ab]
        s = 1
        while (1 << s) < chunk:
            level = same(s + 1) & jnp.logical_not(same(s))
            offs = [jnp.where(level, m, 0.0) for m in m_ab]
            mids = [_packed_dot1(t, o, hm) for t, o in zip(ts, offs)]
            yield
            ts = [t + _packed_dot1(md, t, hm) for t, md in zip(ts, mids)]
            yield
            s += 1

        both = [_packed_dot1(jnp.concatenate([m, pk], axis=0), x, hm) for m, pk, x in zip(m_ak, p_rk, vv)]
        mv = [x[:chunk] for x in both]
        pkv = [x[chunk:] for x in both]
        yield
        hat = [two_rhs(t, x, y) for t, x, y in zip(ts, at, mv)]
        a_hat = [x[0] for x in hat]
        u0 = [x[1] for x in hat]
        yield
        pb = [two_rhs(p, a, u) for p, a, u in zip(p_rb, a_hat, u0)]
        r_hat = [x + y[0] for x, y in zip(rt, pb)]
        y0 = [x + y[1] for x, y in zip(pkv, pb)]
        yield
        g_mat = [same_head(_dot(halves_rows(a), halves_masked(bb), _TN))
                 for a, bb in zip(a_hat, b_bar)]
        h_mat = [same_head(_dot(jnp.concatenate([halves_rows(u), halves_rows(x)], axis=0),
                                jnp.concatenate([halves_masked(bb), halves_masked(kb)], axis=0), _TN))
                 for u, x, bb, kb in zip(u0, vv, b_bar, k_bar)]
        yield

        y_rows = []
        for ci in range(sec_chunks):
            y_groups = []
            for gi in range(n_groups):
                j = ci * n_groups + gi
                s0 = state_ref[gi]
                y_groups.append(_packed_dot1(r_hat[j], s0, hm, _NT) + y0[j])
                w_tot = jnp.exp(tots[ci][:, gi * gw:(gi + 1) * gw])
                state_ref[gi] = s0 * w_tot + _packed_dot_wide_lhs(s0, g_mat[j], hm) + h_mat[j]
            y_rows.append(jnp.concatenate(y_groups, axis=1))
            yield
        y = jnp.concatenate(y_rows, axis=0)

        hsum = hsum_ref[...]
        inv_n = 1.0 / n
        mu = _head_sums(y, hsum) * inv_n
        yc = y - mu
        var = _head_sums(yc * yc, hsum) * inv_n
        yn = yc * lax.rsqrt(var + GN_EPS) * gng_ref[...] + gnb_ref[...]
        bonus = _head_sums(r * k * rk_ref[...], hsum)
        o_ref[0, rs, :] = ((yn + bonus * v) * g_ref[0, rs, :]).astype(o_ref.dtype)

    assert RWKV_SECTION_LAG >= sec_chunks
    _run_skewed([section(si * sec_rows) for si in range(RWKV_SECTIONS)], lag=RWKV_SECTION_LAG)


def _rwkv(r, k, v, lw, kk, b, g, rw_rk, gn_g, gn_b, hsum, *, chunk, rows):
    bsz, seq, d = r.shape
    n_groups = d // (RWKV_GROUP_HEADS * HEAD_DIM)
    blk = pl.BlockSpec((1, rows, d), lambda bi, ci: (bi, ci, 0))
    const = lambda bi, ci: (0, 0)
    vec = lambda a: a.reshape(1, d)
    return pl.pallas_call(
        functools.partial(_rwkv_kernel, chunk=chunk),
        grid=(bsz, seq // rows),
        in_specs=[blk] * 7 + [pl.BlockSpec((1, d), const)] * 3 + [pl.BlockSpec(hsum.shape, const)],
        out_specs=blk,
        out_shape=jax.ShapeDtypeStruct((bsz, seq, d), BF16),
        scratch_shapes=[pltpu.VMEM((n_groups, HEAD_DIM, RWKV_GROUP_HEADS * HEAD_DIM), F32)],
        compiler_params=pltpu.CompilerParams(dimension_semantics=("arbitrary", "arbitrary"),
                                             vmem_limit_bytes=VMEM_LIMIT),
        name="rwkv",
    )(r, k, v, lw, kk, b, g, vec(rw_rk), vec(gn_g), vec(gn_b), hsum)


SB_LOG_UNDERFLOW = 105.0


def _sb_kernel(q_ref, k_ref, v_ref, o_ref, acc_ref, run_ref, *, blk):
    qi = pl.program_id(1)
    pairs = k_ref.shape[2] // (2 * HEAD_DIM)
    row = lax.broadcasted_iota(jnp.int32, (blk, blk), 0)
    col = lax.broadcasted_iota(jnp.int32, (blk, blk), 1)
    causal = col < row
    jj = lax.broadcasted_iota(jnp.int32, (blk, 2 * blk), 0)
    ss = lax.broadcasted_iota(jnp.int32, (blk, 2 * blk), 1)
    suffix = jnp.where((jj >= ss) | (ss >= blk), 1.0, 0.0).astype(BF16)
    first_head = lax.broadcasted_iota(jnp.int32, (1, 2 * HEAD_DIM), 1) < HEAD_DIM
    zero = jnp.zeros((), BF16)

    lanes = [slice(p * 2 * HEAD_DIM, (p + 1) * 2 * HEAD_DIM) for p in range(pairs)]

    pw = 2 * HEAD_DIM

    def tile(kb, diagonal):
        start = pl.multiple_of(kb * blk, blk)
        zs = []
        for p in range(pairs):
            q2 = jnp.concatenate([q_ref[0, :, (2 * p + hh) * pw:(2 * p + hh + 1) * pw] for hh in range(2)],
                                 axis=0)
            z2 = _dot(q2, k_ref[0, pl.ds(start, blk), lanes[p]], _NT)
            zs += [z2[:blk], z2[blk:]]
        sps = [_softplus(z) for z in zs]
        if diagonal:
            sps = [jnp.where(causal, sp, 0.0) for sp in sps]
        css = []
        for p in range(pairs):
            c2 = _dot(jnp.concatenate(sps[2 * p:2 * p + 2], axis=0).astype(BF16), suffix)
            css += [c2[:blk], c2[blk:]]
        atts = []
        low = None
        for n_, (z, cs) in enumerate(zip(zs, css)):
            logw = z - cs[:, :blk]
            if diagonal:
                att = jnp.where(causal, jnp.exp(logw), 0.0)
                run = cs[:, blk:]
            else:
                prev = run_ref[n_]
                att = jnp.exp(logw - prev)
                run = prev + cs[:, blk:]
            run_ref[n_] = run
            low = run if low is None else jnp.minimum(low, run)
            atts.append(att.astype(BF16))
        done = (jnp.min(low) > SB_LOG_UNDERFLOW).astype(jnp.int32)
        for p in range(pairs):
            vp = v_ref[0, pl.ds(start, blk), lanes[p]]
            vv = jnp.concatenate([jnp.where(first_head, vp, zero),
                                  jnp.where(first_head, zero, vp)], axis=0)
            out = _dot(jnp.concatenate(atts[2 * p:2 * p + 2], axis=1), vv)
            if diagonal:
                acc_ref[:, lanes[p]] = out
            else:
                acc_ref[:, lanes[p]] += out
        return done

    done0 = tile(qi, True)

    def cond(c):
        i, done = c
        return jnp.logical_and(i <= qi, done == 0)

    def body(c):
        i, _ = c
        return i + 1, tile(qi - i, False)

    lax.while_loop(cond, body, (jnp.int32(1), done0))
    o_ref[0] = acc_ref[...].astype(o_ref.dtype)


def _sb_attention(q, k, v, *, blk):
    bsz, seq, d = k.shape
    heads = d // HEAD_DIM
    qspec = pl.BlockSpec((1, blk, 2 * d), lambda b, i: (b, i, 0))
    ospec = pl.BlockSpec((1, blk, d), lambda b, i: (b, i, 0))
    kvspec = pl.BlockSpec((1, seq, d), lambda b, i: (b, 0, 0))
    return pl.pallas_call(
        functools.partial(_sb_kernel, blk=blk),
        grid=(bsz, seq // blk),
        in_specs=[qspec, kvspec, kvspec],
        out_specs=ospec,
        out_shape=jax.ShapeDtypeStruct((bsz, seq, d), BF16),
        scratch_shapes=[pltpu.VMEM((blk, d), F32), pltpu.VMEM((heads, blk, blk), F32)],
        compiler_params=pltpu.CompilerParams(dimension_semantics=("arbitrary", "arbitrary"),
                                             vmem_limit_bytes=VMEM_LIMIT),
        name="sbattn",
    )(q, k, v)


def _merge_kernel(x_ref, ada_ref, ya_ref, ob_ref, ga_ref, gb_ref, wa_ref, wb_ref, wo_ref,
                  lng_ref, lnb_ref, o_ref, *, alpha):
    g1 = ada_ref[0, 2:3, :]
    half = x_ref.shape[0] // 2

    def rows_program(rows):
        y_a = _dot(ya_ref[rows, :], wa_ref[...])
        y_b = _dot(ob_ref[rows, :], wb_ref[...])
        yield
        merged = ga_ref[rows, :].astype(F32) * y_a + gb_ref[rows, :].astype(F32) * y_b
        mix = _dot(merged.astype(BF16), wo_ref[...])
        yield
        o_ref[rows, :] = _norm_rows(alpha * x_ref[rows, :] + g1 * mix, LN_EPS) * lng_ref[...] + lnb_ref[...]

    _run_skewed([rows_program(slice(0, half)), rows_program(slice(half, 2 * half))], lag=1)


def _merge(x2, ada3, ya, ob, ga, gb, wa_bf, wb_bf, wo_bf, ln_g, ln_b, *, seq, tm, alpha):
    m, d = x2.shape
    tiles_per_batch = seq // tm
    row = lambda i: (i, 0)
    const = lambda i: (0, 0)
    rows = lambda a: pl.BlockSpec((tm, a.shape[1]), row)
    full = lambda a: pl.BlockSpec(a.shape, const, pipeline_mode=pl.Buffered(1))
    vec = lambda a: a.reshape(1, -1)
    ins = [x2, ada3, ya, ob, ga, gb, wa_bf, wb_bf, wo_bf, vec(ln_g), vec(ln_b)]
    in_specs = [rows(x2), pl.BlockSpec((1,) + ada3.shape[1:], lambda i: (i // tiles_per_batch, 0, 0)),
                rows(ya), rows(ob), rows(ga), rows(gb)] + [full(a) for a in ins[6:]]
    return pl.pallas_call(
        functools.partial(_merge_kernel, alpha=alpha),
        grid=(m // tm,),
        in_specs=in_specs,
        out_specs=pl.BlockSpec((tm, d), row),
        out_shape=jax.ShapeDtypeStruct((m, d), F32),
        compiler_params=pltpu.CompilerParams(dimension_semantics=("arbitrary",),
                                             vmem_limit_bytes=VMEM_LIMIT),
        name="merge",
    )(*ins)


def _ffn_kernel(x_ref, ada_ref, w1_ref, b1_ref, w2_ref, b2_ref, lng_ref, lnb_ref, o_ref, *, alpha, fchunk):
    sh = ada_ref[0, 3:4, :]
    sc = ada_ref[0, 4:5, :]
    g2 = ada_ref[0, 5:6, :]
    d_ff = w1_ref.shape[1]
    n_chunks = d_ff // fchunk
    half = x_ref.shape[0] // 2

    def rows_program(rows):
        x = x_ref[rows, :]
        hb = (_norm_rows(x, ADALN_EPS) * (1.0 + sc) + sh).astype(BF16)
        yield
        ff = None
        for j in range(n_chunks):
            sl = slice(j * fchunk, (j + 1) * fchunk)
            t = jnp.maximum(_dot(hb, w1_ref[:, sl]) + b1_ref[:, sl], 0.0)
            part = _dot((t * t).astype(BF16), w2_ref[sl, :])
            ff = part if ff is None else ff + part
            yield
        ff = ff + b2_ref[...]
        o_ref[rows, :] = _norm_rows(alpha * x + g2 * ff, LN_EPS) * lng_ref[...] + lnb_ref[...]

    _run_skewed([rows_program(slice(0, half)), rows_program(slice(half, 2 * half))], lag=2)


def _ffn(x1, ada3, w1_bf, b1, w2_bf, b2, ln_g, ln_b, *, seq, tm, alpha, fchunk):
    m, d = x1.shape
    tiles_per_batch = seq // tm
    row = lambda i: (i, 0)
    const = lambda i: (0, 0)
    full = lambda a: pl.BlockSpec(a.shape, const, pipeline_mode=pl.Buffered(1))
    vec = lambda a: a.reshape(1, -1)
    ins = [x1, ada3, w1_bf, vec(b1), w2_bf, vec(b2), vec(ln_g), vec(ln_b)]
    in_specs = [pl.BlockSpec((tm, d), row),
                pl.BlockSpec((1,) + ada3.shape[1:], lambda i: (i // tiles_per_batch, 0, 0))]
    in_specs += [full(a) for a in ins[2:]]
    return pl.pallas_call(
        functools.partial(_ffn_kernel, alpha=alpha, fchunk=fchunk),
        grid=(m // tm,),
        in_specs=in_specs,
        out_specs=pl.BlockSpec((tm, d), row),
        out_shape=jax.ShapeDtypeStruct((m, d), F32),
        compiler_params=pltpu.CompilerParams(dimension_semantics=("arbitrary",),
                                             vmem_limit_bytes=VMEM_LIMIT),
        name="ffn",
    )(*ins)


def _branches(proj, rw_rk, rw_gn_g, rw_gn_b, hsum, bsz, seq):
    r, k, v, lw, kk, b, g, q, ks, vs = proj
    seq3 = lambda a: a.reshape(bsz, seq, -1)
    ya = _rwkv(seq3(r), seq3(k), seq3(v), seq3(lw), seq3(kk), seq3(b), seq3(g),
               rw_rk, rw_gn_g, rw_gn_b, hsum, chunk=min(RWKV_CHUNK, seq),
               rows=min(RWKV_CHUNKS_PER_STEP * RWKV_CHUNK, seq))
    o = _sb_attention(seq3(q), seq3(ks), seq3(vs), blk=min(SB_BLOCK, seq))
    return ya.reshape(bsz * seq, -1), o.reshape(bsz * seq, -1)


def _layer(x, c, w_ada, b_ada, w_in, mu_shift, rw_w0, rw_w2, rw_a0, rw_a2, rw_g2, rw_kk, rw_ka,
           rw_rk, rw_gn_g, rw_gn_b, w_branch_a, w_branch_b, w_out, ln1_g, ln1_b,
           w_ff1, b_ff1, w_ff2, b_ff2, ln2_g, ln2_b, *, alpha):
    bsz, seq, d = x.shape
    d_rwkv = rw_w0.shape[-1]
    tm = min(INPROJ_ROWS, seq)
    tm2 = min(MLP_ROWS, seq)
    rwkv_rows = min(RWKV_CHUNKS_PER_STEP * RWKV_CHUNK, seq)
    group_w = RWKV_GROUP_HEADS * HEAD_DIM
    assert seq % tm == 0 and seq % tm2 == 0 and seq % rwkv_rows == 0 and seq % min(SB_BLOCK, seq) == 0, seq
    assert rwkv_rows % (RWKV_SECTIONS * RWKV_CHUNK) == 0, (rwkv_rows, RWKV_SECTIONS)
    assert d_rwkv % group_w == 0 and w_branch_b.shape[0] % (2 * HEAD_DIM) == 0, (d_rwkv, w_branch_b.shape)
    assert HEAD_DIM & (HEAD_DIM - 1) == 0 and RWKV_CHUNK & (RWKV_CHUNK - 1) == 0

    ada3 = _ada(c, w_ada, b_ada).reshape(bsz, 6, d)
    lane = jnp.arange(RWKV_GROUP_HEADS * HEAD_DIM) // HEAD_DIM
    hsum = (lane[:, None] == lane[None, :]).astype(BF16)

    x2 = x.reshape(bsz * seq, d)
    (r, k, v, lw, kk, b, g, q, ks, vs, ga, gb) = _inproj(
        x2, ada3, w_in.astype(BF16), mu_shift, rw_w0, rw_w2, rw_a0, rw_a2, rw_g2, rw_kk, rw_ka, hsum,
        seq=seq, tm=tm)

    ya, ob = _branches((r, k, v, lw, kk, b, g, q, ks, vs), rw_rk, rw_gn_g, rw_gn_b, hsum, bsz, seq)

    x1 = _merge(x2, ada3, ya, ob, ga, gb,
                w_branch_a.astype(BF16), w_branch_b.astype(BF16), w_out.astype(BF16),
                ln1_g, ln1_b, seq=seq, tm=tm2, alpha=alpha)
    out = _ffn(x1, ada3, w_ff1.astype(BF16), b_ff1, w_ff2.astype(BF16), b_ff2, ln2_g, ln2_b,
               seq=seq, tm=tm2, alpha=alpha, fchunk=min(1024, w_ff1.shape[-1]))
    return out.reshape(bsz, seq, d)


def kernel(x, c, w_ada, b_ada, w_in, mu_shift, rw_w0, rw_w2, rw_a0, rw_a2, rw_g2, rw_kk, rw_ka, rw_rk,
           rw_gn_g, rw_gn_b, w_branch_a, w_branch_b, w_out, ln1_g, ln1_b, w_ff1, b_ff1, w_ff2, b_ff2,
           ln2_g, ln2_b):
    in_dtype = x.dtype
    layer_params = (w_ada, b_ada, w_in, mu_shift, rw_w0, rw_w2, rw_a0, rw_a2, rw_g2, rw_kk, rw_ka,
                    rw_rk, rw_gn_g, rw_gn_b, w_branch_a, w_branch_b, w_out, ln1_g, ln1_b,
                    w_ff1, b_ff1, w_ff2, b_ff2, ln2_g, ln2_b)
    depth = w_ada.shape[0]
    alpha = (2.0 * depth) ** 0.25
    for l in range(depth):
        x = _layer(x, c, *[p[l] for p in layer_params], alpha=alpha)
    return x.astype(in_dtype)
```

```python
import functools

import jax
import jax.numpy as jnp
from jax import lax
from jax.experimental import pallas as pl
from jax.experimental.pallas import tpu as pltpu

F32 = jnp.float32
BF16 = jnp.bfloat16

HEAD_DIM = 64
LN_EPS = 1e-5
ADALN_EPS = 1e-6
GN_EPS = 64e-5
RWKV_CHUNK = 64
RWKV_SECTIONS = 2
RWKV_SECTION_CHUNKS = 4
RWKV_SECTION_LAG = 12
RWKV_CHUNKS_PER_STEP = RWKV_SECTIONS * RWKV_SECTION_CHUNKS
SB_BLOCK = 256
SB_BLOCKS_PER_STEP = 2
INPROJ_ROWS = 512
MLP_ROWS = 1024
VMEM_LIMIT = 56 * 1024 * 1024


def _split2(a):
    hi = a.astype(BF16)
    lo = (a - hi.astype(F32)).astype(BF16)
    return hi, lo


_NN = (((1,), (0,)), ((), ()))
_NT = (((1,), (1,)), ((), ()))
_TN = (((0,), (0,)), ((), ()))


def _dot(a, b, dims=_NN):
    return lax.dot_general(a, b, dims, preferred_element_type=F32)


def _head_sums(x, ones_blk):
    m, w = x.shape[0], ones_blk.shape[0]
    hi, lo = _split2(x)
    outs = []
    for g in range(x.shape[1] // w):
        cols = slice(g * w, (g + 1) * w)
        both = _dot(jnp.concatenate([hi[:, cols], lo[:, cols]], axis=0), ones_blk)
        outs.append(both[:m] + both[m:])
    return jnp.concatenate(outs, axis=1)


def _dot_ones_lhs(a_bf, b):
    hi, lo = _split2(b)
    return _dot(a_bf, hi) + _dot(a_bf, lo)


_DONE = object()


def _run_skewed(programs, lag):
    live = [True] * len(programs)
    tick = 0
    while any(live):
        for i, prog in enumerate(programs):
            if live[i] and tick >= i * lag:
                live[i] = next(prog, _DONE) is not _DONE
        tick += 1


_NEG_LOG2_E = -1.4426950408889634


def _softplus(y):
    return jnp.maximum(y, 0.0) + jnp.log(1.0 + jnp.exp2(jnp.abs(y) * _NEG_LOG2_E))


def _sigmoid(y):
    return 1.0 / (1.0 + jnp.exp(-y))


def _norm_rows(x, eps):
    mu = jnp.mean(x, axis=-1, keepdims=True)
    xc = x - mu
    var = jnp.mean(xc * xc, axis=-1, keepdims=True)
    return xc * lax.rsqrt(var + eps)


def _ada_kernel(c_ref, w_ref, b_ref, o_ref):
    c = c_ref[...]
    s = c * _sigmoid(c)
    m = s.shape[0]
    sh, sl = _split2(s)
    wh, wl = _split2(w_ref[...])
    top = _dot(jnp.concatenate([sh, sl], axis=0), wh)
    o_ref[...] = top[:m] + (top[m:] + _dot(sh, wl)) + b_ref[...]


def _ada(c, w_ada, b_ada):
    bsz, d = c.shape
    n = w_ada.shape[1]
    wide = 2 * d if n % (2 * d) == 0 else d
    return pl.pallas_call(
        _ada_kernel,
        grid=(n // wide,),
        in_specs=[pl.BlockSpec((bsz, d), lambda j: (0, 0)),
                  pl.BlockSpec((d, wide), lambda j: (0, j)),
                  pl.BlockSpec((1, wide), lambda j: (0, j))],
        out_specs=pl.BlockSpec((bsz, wide), lambda j: (0, j)),
        out_shape=jax.ShapeDtypeStruct((bsz, n), F32),
        name="ada",
    )(c, w_ada, b_ada.reshape(1, n))


def _inproj_kernel(x_ref, ada_ref, win_ref, mu_ref, w0_ref, w2_ref, a0_ref, a2_ref, g2_ref,
                   kkw_ref, kaw_ref, hsum_ref,
                   r_ref, k_ref, v_ref, lw_ref, kk_ref, b_ref, g_ref,
                   q_ref, ks_ref, vs_ref, ga_ref, gb_ref,
                   carry_ref, *, tiles_per_batch, d_rwkv, d_decay, d_aaa, d_gate, d_sb, d_model):
    i = pl.program_id(0)
    tm = x_ref.shape[0]
    d_shift = 3 * d_rwkv + d_decay + d_aaa + d_gate

    sh = ada_ref[0, 0:1, :]
    sc = ada_ref[0, 1:2, :]

    @pl.when(i % tiles_per_batch == 0)
    def _():
        carry_ref[...] = jnp.zeros_like(carry_ref)

    half = d_model // 2
    pair_w = 2 * HEAD_DIM
    first_head = lax.broadcasted_iota(jnp.int32, (1, pair_w), 1) < HEAD_DIM

    def rows_program(rows):
        hm = rows.stop - rows.start
        hb = (_norm_rows(x_ref[rows, :], ADALN_EPS) * (1.0 + sc) + sh).astype(BF16)
        row0 = lax.broadcasted_iota(jnp.int32, (hm, 1), 0) == 0
        yield

        def shifted(lo, width):
            cols = slice(lo, lo + width)
            z = _dot(hb, win_ref[:, cols])
            prev = jnp.where(row0, carry_ref[:, cols], pltpu.roll(z, 1, 0))
            carry_ref[:, cols] = z[hm - 1:hm, :]
            return z + mu_ref[:, cols] * (prev - z)

        def store_queries(z):
            zq = (z * (HEAD_DIM ** -0.5)).astype(BF16)
            zero = jnp.zeros((), BF16)
            for p in range(d_sb // pair_w):
                qp = zq[:, p * pair_w:(p + 1) * pair_w]
                q_ref[rows, (2 * p) * pair_w:(2 * p + 1) * pair_w] = jnp.where(first_head, qp, zero)
                q_ref[rows, (2 * p + 1) * pair_w:(2 * p + 2) * pair_w] = jnp.where(first_head, zero, qp)

        def plain(ref, cols, fn=lambda z: z):
            def store(z):
                ref[rows, cols] = fn(z).astype(BF16)
            return store

        all_cols = slice(0, d_sb)
        rest = [(store_queries, d_sb), (plain(ks_ref, all_cols), d_sb), (plain(vs_ref, all_cols), d_sb),
                (plain(ga_ref, slice(0, half), _sigmoid), half),
                (plain(ga_ref, slice(half, d_model), _sigmoid), half),
                (plain(gb_ref, slice(0, half), _sigmoid), half),
                (plain(gb_ref, slice(half, d_model), _sigmoid), half)]
        rest_col = [d_shift]

        def project_next():
            store, width = rest.pop(0)
            store(_dot(hb, win_ref[:, rest_col[0]:rest_col[0] + width]))
            rest_col[0] += width

        zl = shifted(3 * d_rwkv, d_decay + d_aaa + d_gate)
        zw = zl[:, :d_decay]
        za = zl[:, d_decay:d_decay + d_aaa]
        zg = zl[:, d_decay + d_aaa:]
        k = shifted(d_rwkv, d_rwkv)
        yield
        ww = w0_ref[...] + _dot(jnp.tanh(zw).astype(BF16), w2_ref[...])
        w_log = -_softplus(-ww) - 0.5
        lw_ref[rows, :] = -jnp.exp(w_log)
        project_next()
        yield
        arate = _sigmoid(a0_ref[...] + _dot(za.astype(BF16), a2_ref[...]))
        g_ref[rows, :] = _dot(_sigmoid(zg).astype(BF16), g2_ref[...])
        r_ref[rows, :] = shifted(0, d_rwkv)
        yield
        kkraw = k * kkw_ref[...]
        ssq = _head_sums(kkraw * kkraw, hsum_ref[...])
        kk = kkraw / jnp.maximum(jnp.sqrt(ssq), 1e-12)
        v_ref[rows, :] = shifted(2 * d_rwkv, d_rwkv)
        k_ref[rows, :] = k * (1.0 + (arate - 1.0) * kaw_ref[...])
        yield
        project_next()
        kk_ref[rows, :] = kk
        b_ref[rows, :] = kk * arate
        yield
        while rest:
            project_next()
            yield

    hm = tm // 2
    _run_skewed([rows_program(slice(0, hm)), rows_program(slice(hm, tm))], lag=2)


def _inproj(x2, ada3, win_bf, mu_shift, rw_w0, rw_w2, rw_a0, rw_a2, rw_g2, rw_kk, rw_ka, hsum,
            *, seq, tm):
    m, d_model = x2.shape
    d_rwkv = rw_w0.shape[-1]
    d_decay, d_aaa, d_gate = rw_w2.shape[0], rw_a2.shape[0], rw_g2.shape[0]
    d_shift = 3 * d_rwkv + d_decay + d_aaa + d_gate
    d_sb = (win_bf.shape[1] - d_shift - 2 * d_model) // 3
    tiles_per_batch = seq // tm
    row = lambda i: (i, 0)
    const = lambda i: (0, 0)
    full = lambda a: pl.BlockSpec(a.shape, const, pipeline_mode=pl.Buffered(1))
    vec = lambda a: a.reshape(1, -1)
    ins = [x2, ada3, win_bf, vec(mu_shift), vec(rw_w0), rw_w2.astype(BF16), vec(rw_a0),
           rw_a2.astype(BF16), rw_g2.astype(BF16), vec(rw_kk), vec(rw_ka), hsum]
    in_specs = [pl.BlockSpec((tm, d_model), row),
                pl.BlockSpec((1,) + ada3.shape[1:], lambda i: (i // tiles_per_batch, 0, 0))]
    in_specs += [full(a) for a in ins[2:]]
    widths = [d_rwkv] * 7 + [2 * d_sb, d_sb, d_sb] + [d_model] * 2
    dtypes = [F32] * 7 + [BF16] * 5
    out_shape = [jax.ShapeDtypeStruct((m, w), dt) for w, dt in zip(widths, dtypes)]
    out_specs = [pl.BlockSpec((tm, w), row) for w in widths]
    kern = functools.partial(_inproj_kernel, tiles_per_batch=tiles_per_batch, d_rwkv=d_rwkv,
                             d_decay=d_decay, d_aaa=d_aaa, d_gate=d_gate, d_sb=d_sb, d_model=d_model)
    return pl.pallas_call(
        kern,
        grid=(m // tm,),
        in_specs=in_specs,
        out_specs=out_specs,
        out_shape=out_shape,
        scratch_shapes=[pltpu.VMEM((1, d_shift), F32)],
        compiler_params=pltpu.CompilerParams(dimension_semantics=("arbitrary",),
                                             vmem_limit_bytes=VMEM_LIMIT),
        name="inproj",
    )(*ins)


RWKV_GROUP_HEADS = 4


def _block_diag(q, head_masks):
    zero = jnp.zeros((), q.dtype)
    return jnp.concatenate([jnp.where(m, q, zero) for m in head_masks], axis=0)


def _packed_dot1(lhs, q, head_masks, dims=_NN):
    return _dot(lhs.astype(BF16), _block_diag(q.astype(BF16), head_masks), dims)


def _packed_dot_wide_lhs(lhs, q, head_masks, dims=_NN):
    m = lhs.shape[0]
    lh, ll = _split2(lhs)
    both = _dot(jnp.concatenate([lh, ll], axis=0), _block_diag(q.astype(BF16), head_masks), dims)
    return both[:m] + both[m:]


def _rwkv_kernel(r_ref, k_ref, v_ref, lw_ref, kk_ref, b_ref, g_ref, rk_ref, gng_ref, gnb_ref, hsum_ref,
                 o_ref, state_ref, *, chunk):
    c = pl.program_id(1)
    n = HEAD_DIM
    rows, d = r_ref.shape[1], r_ref.shape[2]
    gw = RWKV_GROUP_HEADS * n
    n_groups = d // gw

    @pl.when(c == 0)
    def _():
        state_ref[...] = jnp.zeros_like(state_ref)

    t_idx = lax.broadcasted_iota(jnp.int32, (chunk, gw), 0)
    lane = lax.broadcasted_iota(jnp.int32, (chunk, gw), 1)
    i_idx = lane & (n - 1)
    strict = t_idx > i_idx
    incl = t_idx >= i_idx
    lane1 = lax.broadcasted_iota(jnp.int32, (1, gw), 1)
    hm = [(lane1 >> (n.bit_length() - 1)) == h for h in range(RWKV_GROUP_HEADS)]
    same = lambda s: (t_idx >> s) == (i_idx >> s)

    half_w = gw // 2
    lane_half = [(lane1 >> (half_w.bit_length() - 1)) == j for j in range(2)]
    second_in_half = ((lane >> (n.bit_length() - 1)) & 1) == 1

    def halves_rows(x):
        return jnp.concatenate([x[:, :half_w], x[:, half_w:]], axis=0).astype(BF16)

    def halves_masked(w):
        return _block_diag(w.astype(BF16), lane_half)

    def same_head(out):
        return jnp.where(second_in_half, out[n:], out[:n])

    lane_h = lax.broadcasted_iota(jnp.int32, (1, half_w), 1)
    in_half = [(lane_h >> (n.bit_length() - 1)) == j for j in range(half_w // n)]

    def two_rhs(lhs, x, y, dims=_NN):
        lb, xb, yb = lhs.astype(BF16), x.astype(BF16), y.astype(BF16)
        out_x, out_y = [], []
        for j in range(gw // half_w):
            cols = slice(j * half_w, (j + 1) * half_w)
            w = jnp.concatenate([_block_diag(xb[:, cols], in_half), _block_diag(yb[:, cols], in_half)],
                                axis=1 if dims == _NN else 0)
            o = _dot(lb[:, cols], w, dims)
            out_x.append(o[:, :half_w])
            out_y.append(o[:, half_w:])
        return jnp.concatenate(out_x, axis=1), jnp.concatenate(out_y, axis=1)

    sec_rows = rows // RWKV_SECTIONS
    sec_chunks = sec_rows // chunk
    rr = lax.broadcasted_iota(jnp.int32, (sec_rows, sec_rows), 0)
    cc = lax.broadcasted_iota(jnp.int32, (sec_rows, sec_rows), 1)
    cs = chunk.bit_length() - 1
    ltri = jnp.where((rr >= cc) & ((rr >> cs) == (cc >> cs)), 1.0, 0.0).astype(BF16)
    eye = jnp.where(t_idx == i_idx, 1.0, 0.0)
    chains = [(ci, gi) for ci in range(sec_chunks) for gi in range(n_groups)]
    win = lambda x, ci, gi: x[ci * chunk:(ci + 1) * chunk, gi * gw:(gi + 1) * gw]

    def section(row0):
        rs = slice(row0, row0 + sec_rows)
        r, k, v, lw = r_ref[0, rs, :], k_ref[0, rs, :], v_ref[0, rs, :], lw_ref[0, rs, :]
        kk, b = kk_ref[0, rs, :], b_ref[0, rs, :]
        cum = _dot_ones_lhs(ltri, lw)
        a_t = -kk * jnp.exp(cum - lw)
        r_t = r * jnp.exp(cum)
        e_neg = jnp.exp(-cum)
        b_t = b * e_neg
        k_t = k * e_neg
        tots = [cum[(ci + 1) * chunk - 1:(ci + 1) * chunk, :] for ci in range(sec_chunks)]
        e_bars = [jnp.exp(tots[ci] - cum[ci * chunk:(ci + 1) * chunk, :]) for ci in range(sec_chunks)]
        b_bar = [win(b, ci, gi) * e_bars[ci][:, gi * gw:(gi + 1) * gw] for ci, gi in chains]
        k_bar = [win(k, ci, gi) * e_bars[ci][:, gi * gw:(gi + 1) * gw] for ci, gi in chains]
        at = [win(a_t, ci, gi) for ci, gi in chains]
        rt = [win(r_t, ci, gi) for ci, gi in chains]
        vv = [win(v, ci, gi) for ci, gi in chains]
        yield

        ar = [jnp.concatenate([a_, r_], axis=0) for a_, r_ in zip(at, rt)]
        grams = [two_rhs(x, win(b_t, ci, gi), win(k_t, ci, gi), _NT) for x, (ci, gi) in zip(ar, chains)]
        gram_b = [x[0] for x in grams]
        gram_k = [x[1] for x in grams]
        yield
        m_ab = [jnp.where(strict, g_[:chunk], 0.0) for g_ in gram_b]
        p_rb = [jnp.where(incl, g_[chunk:], 0.0) for g_ in gram_b]
        m_ak = [jnp.where(strict, g_[:chunk], 0.0) for g_ in gram_k]
        p_rk = [jnp.where(incl, g_[chunk:], 0.0) for g_ in gram_k]
        yield

        ts = [eye + jnp.where(same(1), m, 0.0) for m in m_ab]
        s = 1
        while (1 << s) < chunk:
            level = same(s + 1) & jnp.logical_not(same(s))
            offs = [jnp.where(level, m, 0.0) for m in m_ab]
            mids = [_packed_dot1(t, o, hm) for t, o in zip(ts, offs)]
            yield
            ts = [t + _packed_dot1(md, t, hm) for t, md in zip(ts, mids)]
            yield
            s += 1

        both = [_packed_dot1(jnp.concatenate([m, pk], axis=0), x, hm) for m, pk, x in zip(m_ak, p_rk, vv)]
        mv = [x[:chunk] for x in both]
        pkv = [x[chunk:] for x in both]
        yield
        hat = [two_rhs(t, x, y) for t, x, y in zip(ts, at, mv)]
        a_hat = [x[0] for x in hat]
        u0 = [x[1] for x in hat]
        yield
        pb = [two_rhs(p, a, u) for p, a, u in zip(p_rb, a_hat, u0)]
        r_hat = [x + y[0] for x, y in zip(rt, pb)]
        y0 = [x + y[1] for x, y in zip(pkv, pb)]
        yield
        g_mat = [same_head(_dot(halves_rows(a), halves_masked(bb), _TN))
                 for a, bb in zip(a_hat, b_bar)]
        h_mat = [same_head(_dot(jnp.concatenate([halves_rows(u), halves_rows(x)], axis=0),
                                jnp.concatenate([halves_masked(bb), halves_masked(kb)], axis=0), _TN))
                 for u, x, bb, kb in zip(u0, vv, b_bar, k_bar)]
        yield

        y_rows = []
        for ci in range(sec_chunks):
            y_groups = []
            for gi in range(n_groups):
                j = ci * n_groups + gi
                s0 = state_ref[gi]
                y_groups.append(_packed_dot1(r_hat[j], s0, hm, _NT) + y0[j])
                w_tot = jnp.exp(tots[ci][:, gi * gw:(gi + 1) * gw])
                state_ref[gi] = s0 * w_tot + _packed_dot_wide_lhs(s0, g_mat[j], hm) + h_mat[j]
            y_rows.append(jnp.concatenate(y_groups, axis=1))
            yield
        y = jnp.concatenate(y_rows, axis=0)

        hsum = hsum_ref[...]
        inv_n = 1.0 / n
        mu = _head_sums(y, hsum) * inv_n
        yc = y - mu
        var = _head_sums(yc * yc, hsum) * inv_n
        yn = yc * lax.rsqrt(var + GN_EPS) * gng_ref[...] + gnb_ref[...]
        bonus = _head_sums(r * k * rk_ref[...], hsum)
        o_ref[0, rs, :] = ((yn + bonus * v) * g_ref[0, rs, :]).astype(o_ref.dtype)

    assert RWKV_SECTION_LAG >= sec_chunks
    _run_skewed([section(si * sec_rows) for si in range(RWKV_SECTIONS)], lag=RWKV_SECTION_LAG)


def _rwkv(r, k, v, lw, kk, b, g, rw_rk, gn_g, gn_b, hsum, *, chunk, rows):
    bsz, seq, d = r.shape
    n_groups = d // (RWKV_GROUP_HEADS * HEAD_DIM)
    blk = pl.BlockSpec((1, rows, d), lambda bi, ci: (bi, ci, 0))
    const = lambda bi, ci: (0, 0)
    vec = lambda a: a.reshape(1, d)
    return pl.pallas_call(
        functools.partial(_rwkv_kernel, chunk=chunk),
        grid=(bsz, seq // rows),
        in_specs=[blk] * 7 + [pl.BlockSpec((1, d), const)] * 3 + [pl.BlockSpec(hsum.shape, const)],
        out_specs=blk,
        out_shape=jax.ShapeDtypeStruct((bsz, seq, d), BF16),
        scratch_shapes=[pltpu.VMEM((n_groups, HEAD_DIM, RWKV_GROUP_HEADS * HEAD_DIM), F32)],
        compiler_params=pltpu.CompilerParams(dimension_semantics=("arbitrary", "arbitrary"),
                                             vmem_limit_bytes=VMEM_LIMIT),
        name="rwkv",
    )(r, k, v, lw, kk, b, g, vec(rw_rk), vec(gn_g), vec(gn_b), hsum)


SB_LOG_UNDERFLOW = 105.0


def _sb_kernel(q_ref, k_ref, v_ref, o_ref, acc_ref, run_ref, *, blk):
    n_sub = q_ref.shape[1] // blk
    cur = {}
    pairs = k_ref.shape[2] // (2 * HEAD_DIM)
    row = lax.broadcasted_iota(jnp.int32, (blk, blk), 0)
    col = lax.broadcasted_iota(jnp.int32, (blk, blk), 1)
    causal = col < row
    jj = lax.broadcasted_iota(jnp.int32, (blk, 2 * blk), 0)
    ss = lax.broadcasted_iota(jnp.int32, (blk, 2 * blk), 1)
    suffix = jnp.where((jj >= ss) | (ss >= blk), 1.0, 0.0).astype(BF16)
    first_head = lax.broadcasted_iota(jnp.int32, (1, 2 * HEAD_DIM), 1) < HEAD_DIM
    zero = jnp.zeros((), BF16)

    lanes = [slice(p * 2 * HEAD_DIM, (p + 1) * 2 * HEAD_DIM) for p in range(pairs)]

    pw = 2 * HEAD_DIM

    def tile(kb, diagonal):
        start = pl.multiple_of(kb * blk, blk)
        zs = []
        for p in range(pairs):
            q2 = jnp.concatenate([q_ref[0, cur["rows"], (2 * p + hh) * pw:(2 * p + hh + 1) * pw] for hh in range(2)],
                                 axis=0)
            z2 = _dot(q2, k_ref[0, pl.ds(start, blk), lanes[p]], _NT)
            zs += [z2[:blk], z2[blk:]]
        sps = [_softplus(z) for z in zs]
        if diagonal:
            sps = [jnp.where(causal, sp, 0.0) for sp in sps]
        css = []
        for p in range(pairs):
            c2 = _dot(jnp.concatenate(sps[2 * p:2 * p + 2], axis=0).astype(BF16), suffix)
            css += [c2[:blk], c2[blk:]]
        atts = []
        low = None
        for n_, (z, cs) in enumerate(zip(zs, css)):
            logw = z - cs[:, :blk]
            if diagonal:
                att = jnp.where(causal, jnp.exp(logw), 0.0)
                run = cs[:, blk:]
            else:
                prev = run_ref[n_]
                att = jnp.exp(logw - prev)
                run = prev + cs[:, blk:]
            run_ref[n_] = run
            low = run if low is None else jnp.minimum(low, run)
            atts.append(att.astype(BF16))
        done = (jnp.min(low) > SB_LOG_UNDERFLOW).astype(jnp.int32)
        for p in range(pairs):
            vp = v_ref[0, pl.ds(start, blk), lanes[p]]
            vv = jnp.concatenate([jnp.where(first_head, vp, zero),
                                  jnp.where(first_head, zero, vp)], axis=0)
            out = _dot(jnp.concatenate(atts[2 * p:2 * p + 2], axis=1), vv)
            if diagonal:
                acc_ref[:, lanes[p]] = out
            else:
                acc_ref[:, lanes[p]] += out
        return done

    for sub in range(n_sub):
        qi = pl.program_id(1) * n_sub + sub
        cur["rows"] = slice(sub * blk, (sub + 1) * blk)
        done0 = tile(qi, True)

        def cond(c, qi=qi):
            i, done = c
            return jnp.logical_and(i <= qi, done == 0)

        def body(c, qi=qi):
            i, _ = c
            return i + 1, tile(qi - i, False)

        lax.while_loop(cond, body, (jnp.int32(1), done0))
        o_ref[0, cur["rows"], :] = acc_ref[...].astype(o_ref.dtype)


def _sb_attention(q, k, v, *, blk):
    bsz, seq, d = k.shape
    heads = d // HEAD_DIM
    rows = min(SB_BLOCKS_PER_STEP * blk, seq)
    assert seq % rows == 0 and rows % blk == 0, (seq, rows, blk)
    qspec = pl.BlockSpec((1, rows, 2 * d), lambda b, i: (b, i, 0))
    ospec = pl.BlockSpec((1, rows, d), lambda b, i: (b, i, 0))
    kvspec = pl.BlockSpec((1, seq, d), lambda b, i: (b, 0, 0))
    return pl.pallas_call(
        functools.partial(_sb_kernel, blk=blk),
        grid=(bsz, seq // rows),
        in_specs=[qspec, kvspec, kvspec],
        out_specs=ospec,
        out_shape=jax.ShapeDtypeStruct((bsz, seq, d), BF16),
        scratch_shapes=[pltpu.VMEM((blk, d), F32), pltpu.VMEM((heads, blk, blk), F32)],
        compiler_params=pltpu.CompilerParams(dimension_semantics=("arbitrary", "arbitrary"),
                                             vmem_limit_bytes=VMEM_LIMIT),
        name="sbattn",
    )(q, k, v)


def _merge_kernel(x_ref, ada_ref, ya_ref, ob_ref, ga_ref, gb_ref, wa_ref, wb_ref, wo_ref,
                  lng_ref, lnb_ref, o_ref, *, alpha):
    g1 = ada_ref[0, 2:3, :]
    half = x_ref.shape[0] // 2

    def rows_program(rows):
        y_a = _dot(ya_ref[rows, :], wa_ref[...])
        y_b = _dot(ob_ref[rows, :], wb_ref[...])
        yield
        merged = ga_ref[rows, :].astype(F32) * y_a + gb_ref[rows, :].astype(F32) * y_b
        mix = _dot(merged.astype(BF16), wo_ref[...])
        yield
        o_ref[rows, :] = _norm_rows(alpha * x_ref[rows, :] + g1 * mix, LN_EPS) * lng_ref[...] + lnb_ref[...]

    _run_skewed([rows_program(slice(0, half)), rows_program(slice(half, 2 * half))], lag=1)


def _merge(x2, ada3, ya, ob, ga, gb, wa_bf, wb_bf, wo_bf, ln_g, ln_b, *, seq, tm, alpha):
    m, d = x2.shape
    tiles_per_batch = seq // tm
    row = lambda i: (i, 0)
    const = lambda i: (0, 0)
    rows = lambda a: pl.BlockSpec((tm, a.shape[1]), row)
    full = lambda a: pl.BlockSpec(a.shape, const, pipeline_mode=pl.Buffered(1))
    vec = lambda a: a.reshape(1, -1)
    ins = [x2, ada3, ya, ob, ga, gb, wa_bf, wb_bf, wo_bf, vec(ln_g), vec(ln_b)]
    in_specs = [rows(x2), pl.BlockSpec((1,) + ada3.shape[1:], lambda i: (i // tiles_per_batch, 0, 0)),
                rows(ya), rows(ob), rows(ga), rows(gb)] + [full(a) for a in ins[6:]]
    return pl.pallas_call(
        functools.partial(_merge_kernel, alpha=alpha),
        grid=(m // tm,),
        in_specs=in_specs,
        out_specs=pl.BlockSpec((tm, d), row),
        out_shape=jax.ShapeDtypeStruct((m, d), F32),
        compiler_params=pltpu.CompilerParams(dimension_semantics=("arbitrary",),
                                             vmem_limit_bytes=VMEM_LIMIT),
        name="merge",
    )(*ins)


def _ffn_kernel(x_ref, ada_ref, w1_ref, b1_ref, w2_ref, b2_ref, lng_ref, lnb_ref, o_ref, *, alpha, fchunk):
    sh = ada_ref[0, 3:4, :]
    sc = ada_ref[0, 4:5, :]
    g2 = ada_ref[0, 5:6, :]
    d_ff = w1_ref.shape[1]
    n_chunks = d_ff // fchunk
    half = x_ref.shape[0] // 2

    def rows_program(rows):
        x = x_ref[rows, :]
        hb = (_norm_rows(x, ADALN_EPS) * (1.0 + sc) + sh).astype(BF16)
        yield
        ff = None
        for j in range(n_chunks):
            sl = slice(j * fchunk, (j + 1) * fchunk)
            t = jnp.maximum(_dot(hb, w1_ref[:, sl]) + b1_ref[:, sl], 0.0)
            part = _dot((t * t).astype(BF16), w2_ref[sl, :])
            ff = part if ff is None else ff + part
            yield
        ff = ff + b2_ref[...]
        o_ref[rows, :] = _norm_rows(alpha * x + g2 * ff, LN_EPS) * lng_ref[...] + lnb_ref[...]

    _run_skewed([rows_program(slice(0, half)), rows_program(slice(half, 2 * half))], lag=2)


def _ffn(x1, ada3, w1_bf, b1, w2_bf, b2, ln_g, ln_b, *, seq, tm, alpha, fchunk):
    m, d = x1.shape
    tiles_per_batch = seq // tm
    row = lambda i: (i, 0)
    const = lambda i: (0, 0)
    full = lambda a: pl.BlockSpec(a.shape, const, pipeline_mode=pl.Buffered(1))
    vec = lambda a: a.reshape(1, -1)
    ins = [x1, ada3, w1_bf, vec(b1), w2_bf, vec(b2), vec(ln_g), vec(ln_b)]
    in_specs = [pl.BlockSpec((tm, d), row),
                pl.BlockSpec((1,) + ada3.shape[1:], lambda i: (i // tiles_per_batch, 0, 0))]
    in_specs += [full(a) for a in ins[2:]]
    return pl.pallas_call(
        functools.partial(_ffn_kernel, alpha=alpha, fchunk=fchunk),
        grid=(m // tm,),
        in_specs=in_specs,
        out_specs=pl.BlockSpec((tm, d), row),
        out_shape=jax.ShapeDtypeStruct((m, d), F32),
        compiler_params=pltpu.CompilerParams(dimension_semantics=("arbitrary",),
                                             vmem_limit_bytes=VMEM_LIMIT),
        name="ffn",
    )(*ins)


def _branches(proj, rw_rk, rw_gn_g, rw_gn_b, hsum, bsz, seq):
    r, k, v, lw, kk, b, g, q, ks, vs = proj
    seq3 = lambda a: a.reshape(bsz, seq, -1)
    ya = _rwkv(seq3(r), seq3(k), seq3(v), seq3(lw), seq3(kk), seq3(b), seq3(g),
               rw_rk, rw_gn_g, rw_gn_b, hsum, chunk=min(RWKV_CHUNK, seq),
               rows=min(RWKV_CHUNKS_PER_STEP * RWKV_CHUNK, seq))
    o = _sb_attention(seq3(q), seq3(ks), seq3(vs), blk=min(SB_BLOCK, seq))
    return ya.reshape(bsz * seq, -1), o.reshape(bsz * seq, -1)


def _layer(x, c, w_ada, b_ada, w_in, mu_shift, rw_w0, rw_w2, rw_a0, rw_a2, rw_g2, rw_kk, rw_ka,
           rw_rk, rw_gn_g, rw_gn_b, w_branch_a, w_branch_b, w_out, ln1_g, ln1_b,
           w_ff1, b_ff1, w_ff2, b_ff2, ln2_g, ln2_b, *, alpha):
    bsz, seq, d = x.shape
    d_rwkv = rw_w0.shape[-1]
    tm = min(INPROJ_ROWS, seq)
    tm2 = min(MLP_ROWS, seq)
    rwkv_rows = min(RWKV_CHUNKS_PER_STEP * RWKV_CHUNK, seq)
    group_w = RWKV_GROUP_HEADS * HEAD_DIM
    assert seq % tm == 0 and seq % tm2 == 0 and seq % rwkv_rows == 0 and seq % min(SB_BLOCK, seq) == 0, seq
    assert rwkv_rows % (RWKV_SECTIONS * RWKV_CHUNK) == 0, (rwkv_rows, RWKV_SECTIONS)
    assert d_rwkv % group_w == 0 and w_branch_b.shape[0] % (2 * HEAD_DIM) == 0, (d_rwkv, w_branch_b.shape)
    assert HEAD_DIM & (HEAD_DIM - 1) == 0 and RWKV_CHUNK & (RWKV_CHUNK - 1) == 0

    ada3 = _ada(c, w_ada, b_ada).reshape(bsz, 6, d)
    lane = jnp.arange(RWKV_GROUP_HEADS * HEAD_DIM) // HEAD_DIM
    hsum = (lane[:, None] == lane[None, :]).astype(BF16)

    x2 = x.reshape(bsz * seq, d)
    (r, k, v, lw, kk, b, g, q, ks, vs, ga, gb) = _inproj(
        x2, ada3, w_in.astype(BF16), mu_shift, rw_w0, rw_w2, rw_a0, rw_a2, rw_g2, rw_kk, rw_ka, hsum,
        seq=seq, tm=tm)

    ya, ob = _branches((r, k, v, lw, kk, b, g, q, ks, vs), rw_rk, rw_gn_g, rw_gn_b, hsum, bsz, seq)

    x1 = _merge(x2, ada3, ya, ob, ga, gb,
                w_branch_a.astype(BF16), w_branch_b.astype(BF16), w_out.astype(BF16),
                ln1_g, ln1_b, seq=seq, tm=tm2, alpha=alpha)
    out = _ffn(x1, ada3, w_ff1.astype(BF16), b_ff1, w_ff2.astype(BF16), b_ff2, ln2_g, ln2_b,
               seq=seq, tm=tm2, alpha=alpha, fchunk=min(1024, w_ff1.shape[-1]))
    return out.reshape(bsz, seq, d)


def kernel(x, c, w_ada, b_ada, w_in, mu_shift, rw_w0, rw_w2, rw_a0, rw_a2, rw_g2, rw_kk, rw_ka, rw_rk,
           rw_gn_g, rw_gn_b, w_branch_a, w_branch_b, w_out, ln1_g, ln1_b, w_ff1, b_ff1, w_ff2, b_ff2,
           ln2_g, ln2_b):
    in_dtype = x.dtype
    layer_params = (w_ada, b_ada, w_in, mu_shift, rw_w0, rw_w2, rw_a0, rw_a2, rw_g2, rw_kk, rw_ka,
                    rw_rk, rw_gn_g, rw_gn_b, w_branch_a, w_branch_b, w_out, ln1_g, ln1_b,
                    w_ff1, b_ff1, w_ff2, b_ff2, ln2_g, ln2_b)
    depth = w_ada.shape[0]
    alpha = (2.0 * depth) ** 0.25
    for l in range(depth):
        x = _layer(x, c, *[p[l] for p in layer_params], alpha=alpha)
    return x.astype(in_dtype)
```

```python
import functools

import jax
import jax.numpy as jnp
from jax import lax
from jax.experimental import pallas as pl
from jax.experimental.pallas import tpu as pltpu

F32 = jnp.float32
BF16 = jnp.bfloat16

HEAD_DIM = 64
LN_EPS = 1e-5
ADALN_EPS = 1e-6
GN_EPS = 64e-5
RWKV_CHUNK = 64
RWKV_SECTIONS = 2
RWKV_SECTION_CHUNKS = 4
RWKV_SECTION_LAG = 12
RWKV_CHUNKS_PER_STEP = RWKV_SECTIONS * RWKV_SECTION_CHUNKS
SB_BLOCK = 256
INPROJ_ROWS = 512
MLP_ROWS = 1024
VMEM_LIMIT = 56 * 1024 * 1024


def _split2(a):
    hi = a.astype(BF16)
    lo = (a - hi.astype(F32)).astype(BF16)
    return hi, lo


_NN = (((1,), (0,)), ((), ()))
_NT = (((1,), (1,)), ((), ()))
_TN = (((0,), (0,)), ((), ()))


def _dot(a, b, dims=_NN):
    return lax.dot_general(a, b, dims, preferred_element_type=F32)


def _head_sums(x, ones_blk):
    m, w = x.shape[0], ones_blk.shape[0]
    hi, lo = _split2(x)
    outs = []
    for g in range(x.shape[1] // w):
        cols = slice(g * w, (g + 1) * w)
        both = _dot(jnp.concatenate([hi[:, cols], lo[:, cols]], axis=0), ones_blk)
        outs.append(both[:m] + both[m:])
    return jnp.concatenate(outs, axis=1)


def _dot_ones_lhs(a_bf, b):
    hi, lo = _split2(b)
    return _dot(a_bf, hi) + _dot(a_bf, lo)


_DONE = object()


def _run_skewed(programs, lag):
    live = [True] * len(programs)
    tick = 0
    while any(live):
        for i, prog in enumerate(programs):
            if live[i] and tick >= i * lag:
                live[i] = next(prog, _DONE) is not _DONE
        tick += 1


_NEG_LOG2_E = -1.4426950408889634


def _softplus(y):
    return jnp.maximum(y, 0.0) + jnp.log(1.0 + jnp.exp2(jnp.abs(y) * _NEG_LOG2_E))


def _sigmoid(y):
    return 1.0 / (1.0 + jnp.exp(-y))


def _norm_rows(x, eps):
    mu = jnp.mean(x, axis=-1, keepdims=True)
    xc = x - mu
    var = jnp.mean(xc * xc, axis=-1, keepdims=True)
    return xc * lax.rsqrt(var + eps)


def _ada_kernel(c_ref, w_ref, b_ref, o_ref):
    c = c_ref[...]
    s = c * _sigmoid(c)
    m = s.shape[0]
    sh, sl = _split2(s)
    wh, wl = _split2(w_ref[...])
    top = _dot(jnp.concatenate([sh, sl], axis=0), wh)
    o_ref[...] = top[:m] + (top[m:] + _dot(sh, wl)) + b_ref[...]


def _ada(c, w_ada, b_ada):
    bsz, d = c.shape
    n = w_ada.shape[1]
    wide = 2 * d if n % (2 * d) == 0 else d
    return pl.pallas_call(
        _ada_kernel,
        grid=(n // wide,),
        in_specs=[pl.BlockSpec((bsz, d), lambda j: (0, 0)),
                  pl.BlockSpec((d, wide), lambda j: (0, j)),
                  pl.BlockSpec((1, wide), lambda j: (0, j))],
        out_specs=pl.BlockSpec((bsz, wide), lambda j: (0, j)),
        out_shape=jax.ShapeDtypeStruct((bsz, n), F32),
        name="ada",
    )(c, w_ada, b_ada.reshape(1, n))


def _inproj_kernel(x_ref, ada_ref, win_ref, mu_ref, w0_ref, w2_ref, a0_ref, a2_ref, g2_ref,
                   kkw_ref, kaw_ref, hsum_ref,
                   r_ref, k_ref, v_ref, lw_ref, kk_ref, b_ref, g_ref,
                   q_ref, ks_ref, vs_ref, ga_ref, gb_ref,
                   carry_ref, *, tiles_per_batch, d_rwkv, d_decay, d_aaa, d_gate, d_sb, d_model):
    i = pl.program_id(0)
    tm = x_ref.shape[0]
    d_shift = 3 * d_rwkv + d_decay + d_aaa + d_gate

    sh = ada_ref[0, 0:1, :]
    sc = ada_ref[0, 1:2, :]

    @pl.when(i % tiles_per_batch == 0)
    def _():
        carry_ref[...] = jnp.zeros_like(carry_ref)

    half = d_model // 2
    pair_w = 2 * HEAD_DIM
    first_head = lax.broadcasted_iota(jnp.int32, (1, pair_w), 1) < HEAD_DIM

    def rows_program(rows):
        hm = rows.stop - rows.start
        hb = (_norm_rows(x_ref[rows, :], ADALN_EPS) * (1.0 + sc) + sh).astype(BF16)
        row0 = lax.broadcasted_iota(jnp.int32, (hm, 1), 0) == 0
        yield

        def shifted(lo, width):
            cols = slice(lo, lo + width)
            z = _dot(hb, win_ref[:, cols])
            prev = jnp.where(row0, carry_ref[:, cols], pltpu.roll(z, 1, 0))
            carry_ref[:, cols] = z[hm - 1:hm, :]
            return z + mu_ref[:, cols] * (prev - z)

        def store_queries(z):
            zq = (z * (HEAD_DIM ** -0.5)).astype(BF16)
            zero = jnp.zeros((), BF16)
            for p in range(d_sb // pair_w):
                qp = zq[:, p * pair_w:(p + 1) * pair_w]
                q_ref[rows, (2 * p) * pair_w:(2 * p + 1) * pair_w] = jnp.where(first_head, qp, zero)
                q_ref[rows, (2 * p + 1) * pair_w:(2 * p + 2) * pair_w] = jnp.where(first_head, zero, qp)

        def plain(ref, cols, fn=lambda z: z):
            def store(z):
                ref[rows, cols] = fn(z).astype(BF16)
            return store

        all_cols = slice(0, d_sb)
        rest = [(store_queries, d_sb), (plain(ks_ref, all_cols), d_sb), (plain(vs_ref, all_cols), d_sb),
                (plain(ga_ref, slice(0, half), _sigmoid), half),
                (plain(ga_ref, slice(half, d_model), _sigmoid), half),
                (plain(gb_ref, slice(0, half), _sigmoid), half),
                (plain(gb_ref, slice(half, d_model), _sigmoid), half)]
        rest_col = [d_shift]

        def project_next():
            store, width = rest.pop(0)
            store(_dot(hb, win_ref[:, rest_col[0]:rest_col[0] + width]))
            rest_col[0] += width

        zl = shifted(3 * d_rwkv, d_decay + d_aaa + d_gate)
        zw = zl[:, :d_decay]
        za = zl[:, d_decay:d_decay + d_aaa]
        zg = zl[:, d_decay + d_aaa:]
        k = shifted(d_rwkv, d_rwkv)
        yield
        ww = w0_ref[...] + _dot(jnp.tanh(zw).astype(BF16), w2_ref[...])
        w_log = -_softplus(-ww) - 0.5
        lw_ref[rows, :] = -jnp.exp(w_log)
        project_next()
        yield
        arate = _sigmoid(a0_ref[...] + _dot(za.astype(BF16), a2_ref[...]))
        g_ref[rows, :] = _dot(_sigmoid(zg).astype(BF16), g2_ref[...])
        r_ref[rows, :] = shifted(0, d_rwkv)
        yield
        kkraw = k * kkw_ref[...]
        ssq = _head_sums(kkraw * kkraw, hsum_ref[...])
        kk = kkraw / jnp.maximum(jnp.sqrt(ssq), 1e-12)
        v_ref[rows, :] = shifted(2 * d_rwkv, d_rwkv)
        k_ref[rows, :] = k * (1.0 + (arate - 1.0) * kaw_ref[...])
        yield
        project_next()
        kk_ref[rows, :] = kk
        b_ref[rows, :] = kk * arate
        yield
        while rest:
            project_next()
            yield

    hm = tm // 2
    _run_skewed([rows_program(slice(0, hm)), rows_program(slice(hm, tm))], lag=2)


def _inproj(x2, ada3, win_bf, mu_shift, rw_w0, rw_w2, rw_a0, rw_a2, rw_g2, rw_kk, rw_ka, hsum,
            *, seq, tm):
    m, d_model = x2.shape
    d_rwkv = rw_w0.shape[-1]
    d_decay, d_aaa, d_gate = rw_w2.shape[0], rw_a2.shape[0], rw_g2.shape[0]
    d_shift = 3 * d_rwkv + d_decay + d_aaa + d_gate
    d_sb = (win_bf.shape[1] - d_shift - 2 * d_model) // 3
    tiles_per_batch = seq // tm
    row = lambda i: (i, 0)
    const = lambda i: (0, 0)
    full = lambda a: pl.BlockSpec(a.shape, const, pipeline_mode=pl.Buffered(1))
    vec = lambda a: a.reshape(1, -1)
    ins = [x2, ada3, win_bf, vec(mu_shift), vec(rw_w0), rw_w2.astype(BF16), vec(rw_a0),
           rw_a2.astype(BF16), rw_g2.astype(BF16), vec(rw_kk), vec(rw_ka), hsum]
    in_specs = [pl.BlockSpec((tm, d_model), row),
                pl.BlockSpec((1,) + ada3.shape[1:], lambda i: (i // tiles_per_batch, 0, 0))]
    in_specs += [full(a) for a in ins[2:]]
    widths = [d_rwkv] * 7 + [2 * d_sb, d_sb, d_sb] + [d_model] * 2
    dtypes = [F32] * 7 + [BF16] * 5
    out_shape = [jax.ShapeDtypeStruct((m, w), dt) for w, dt in zip(widths, dtypes)]
    out_specs = [pl.BlockSpec((tm, w), row) for w in widths]
    kern = functools.partial(_inproj_kernel, tiles_per_batch=tiles_per_batch, d_rwkv=d_rwkv,
                             d_decay=d_decay, d_aaa=d_aaa, d_gate=d_gate, d_sb=d_sb, d_model=d_model)
    return pl.pallas_call(
        kern,
        grid=(m // tm,),
        in_specs=in_specs,
        out_specs=out_specs,
        out_shape=out_shape,
        scratch_shapes=[pltpu.VMEM((1, d_shift), F32)],
        compiler_params=pltpu.CompilerParams(dimension_semantics=("arbitrary",),
                                             vmem_limit_bytes=VMEM_LIMIT),
        name="inproj",
    )(*ins)


RWKV_GROUP_HEADS = 4


def _block_diag(q, head_masks):
    zero = jnp.zeros((), q.dtype)
    return jnp.concatenate([jnp.where(m, q, zero) for m in head_masks], axis=0)


def _packed_dot1(lhs, q, head_masks, dims=_NN):
    return _dot(lhs.astype(BF16), _block_diag(q.astype(BF16), head_masks), dims)


def _packed_dot_wide_lhs(lhs, q, head_masks, dims=_NN):
    m = lhs.shape[0]
    lh, ll = _split2(lhs)
    both = _dot(jnp.concatenate([lh, ll], axis=0), _block_diag(q.astype(BF16), head_masks), dims)
    return both[:m] + both[m:]


def _rwkv_kernel(r_ref, k_ref, v_ref, lw_ref, kk_ref, b_ref, g_ref, rk_ref, gng_ref, gnb_ref, hsum_ref,
                 o_ref, state_ref, *, chunk):
    c = pl.program_id(1)
    n = HEAD_DIM
    rows, d = r_ref.shape[1], r_ref.shape[2]
    gw = RWKV_GROUP_HEADS * n
    n_groups = d // gw

    @pl.when(c == 0)
    def _():
        state_ref[...] = jnp.zeros_like(state_ref)

    t_idx = lax.broadcasted_iota(jnp.int32, (chunk, gw), 0)
    lane = lax.broadcasted_iota(jnp.int32, (chunk, gw), 1)
    i_idx = lane & (n - 1)
    strict = t_idx > i_idx
    incl = t_idx >= i_idx
    lane1 = lax.broadcasted_iota(jnp.int32, (1, gw), 1)
    hm = [(lane1 >> (n.bit_length() - 1)) == h for h in range(RWKV_GROUP_HEADS)]
    same = lambda s: (t_idx >> s) == (i_idx >> s)

    half_w = gw // 2
    lane_half = [(lane1 >> (half_w.bit_length() - 1)) == j for j in range(2)]
    second_in_half = ((lane >> (n.bit_length() - 1)) & 1) == 1

    def halves_rows(x):
        return jnp.concatenate([x[:, :half_w], x[:, half_w:]], axis=0).astype(BF16)

    def halves_masked(w):
        return _block_diag(w.astype(BF16), lane_half)

    def same_head(out):
        return jnp.where(second_in_half, out[n:], out[:n])

    lane_h = lax.broadcasted_iota(jnp.int32, (1, half_w), 1)
    in_half = [(lane_h >> (n.bit_length() - 1)) == j for j in range(half_w // n)]

    def two_rhs(lhs, x, y, dims=_NN):
        lb, xb, yb = lhs.astype(BF16), x.astype(BF16), y.astype(BF16)
        out_x, out_y = [], []
        for j in range(gw // half_w):
            cols = slice(j * half_w, (j + 1) * half_w)
            w = jnp.concatenate([_block_diag(xb[:, cols], in_half), _block_diag(yb[:, cols], in_half)],
                                axis=1 if dims == _NN else 0)
            o = _dot(lb[:, cols], w, dims)
            out_x.append(o[:, :half_w])
            out_y.append(o[:, half_w:])
        return jnp.concatenate(out_x, axis=1), jnp.concatenate(out_y, axis=1)

    sec_rows = rows // RWKV_SECTIONS
    sec_chunks = sec_rows // chunk
    rr = lax.broadcasted_iota(jnp.int32, (sec_rows, sec_rows), 0)
    cc = lax.broadcasted_iota(jnp.int32, (sec_rows, sec_rows), 1)
    cs = chunk.bit_length() - 1
    ltri = jnp.where((rr >= cc) & ((rr >> cs) == (cc >> cs)), 1.0, 0.0).astype(BF16)
    eye = jnp.where(t_idx == i_idx, 1.0, 0.0)
    chains = [(ci, gi) for ci in range(sec_chunks) for gi in range(n_groups)]
    win = lambda x, ci, gi: x[ci * chunk:(ci + 1) * chunk, gi * gw:(gi + 1) * gw]

    def section(row0):
        rs = slice(row0, row0 + sec_rows)
        r, k, v, lw = r_ref[0, rs, :], k_ref[0, rs, :], v_ref[0, rs, :], lw_ref[0, rs, :]
        kk, b = kk_ref[0, rs, :], b_ref[0, rs, :]
        cum = _dot_ones_lhs(ltri, lw)
        a_t = -kk * jnp.exp(cum - lw)
        r_t = r * jnp.exp(cum)
        e_neg = jnp.exp(-cum)
        b_t = b * e_neg
        k_t = k * e_neg
        tots = [cum[(ci + 1) * chunk - 1:(ci + 1) * chunk, :] for ci in range(sec_chunks)]
        e_bars = [jnp.exp(tots[ci] - cum[ci * chunk:(ci + 1) * chunk, :]) for ci in range(sec_chunks)]
        b_bar = [win(b, ci, gi) * e_bars[ci][:, gi * gw:(gi + 1) * gw] for ci, gi in chains]
        k_bar = [win(k, ci, gi) * e_bars[ci][:, gi * gw:(gi + 1) * gw] for ci, gi in chains]
        at = [win(a_t, ci, gi) for ci, gi in chains]
        rt = [win(r_t, ci, gi) for ci, gi in chains]
        vv = [win(v, ci, gi) for ci, gi in chains]
        yield

        ar = [jnp.concatenate([a_, r_], axis=0) for a_, r_ in zip(at, rt)]
        grams = [two_rhs(x, win(b_t, ci, gi), win(k_t, ci, gi), _NT) for x, (ci, gi) in zip(ar, chains)]
        gram_b = [x[0] for x in grams]
        gram_k = [x[1] for x in grams]
        yield
        m_ab = [jnp.where(strict, g_[:chunk], 0.0) for g_ in gram_b]
        p_rb = [jnp.where(incl, g_[chunk:], 0.0) for g_ in gram_b]
        m_ak = [jnp.where(strict, g_[:chunk], 0.0) for g_ in gram_k]
        p_rk = [jnp.where(incl, g_[chunk:], 0.0) for g_ in gram_k]
        yield

        ts = [eye + jnp.where(same(1), m, 0.0) for m in m_ab]
        s = 1
        while (1 << s) < chunk:
            level = same(s + 1) & jnp.logical_not(same(s))
            offs = [jnp.where(level, m, 0.0) for m in m_ab]
            mids = [_packed_dot1(t, o, hm) for t, o in zip(ts, offs)]
            yield
            ts = [t + _packed_dot1(md, t, hm) for t, md in zip(ts, mids)]
            yield
            s += 1

        both = [_packed_dot1(jnp.concatenate([m, pk], axis=0), x, hm) for m, pk, x in zip(m_ak, p_rk, vv)]
        mv = [x[:chunk] for x in both]
        pkv = [x[chunk:] for x in both]
        yield
        hat = [two_rhs(t, x, y) for t, x, y in zip(ts, at, mv)]
        a_hat = [x[0] for x in hat]
        u0 = [x[1] for x in hat]
        yield
        pb = [two_rhs(p, a, u) for p, a, u in zip(p_rb, a_hat, u0)]
        r_hat = [x + y[0] for x, y in zip(rt, pb)]
        y0 = [x + y[1] for x, y in zip(pkv, pb)]
        yield
        g_mat = [same_head(_dot(halves_rows(a), halves_masked(bb), _TN))
                 for a, bb in zip(a_hat, b_bar)]
        h_mat = [same_head(_dot(jnp.concatenate([halves_rows(u), halves_rows(x)], axis=0),
                                jnp.concatenate([halves_masked(bb), halves_masked(kb)], axis=0), _TN))
                 for u, x, bb, kb in zip(u0, vv, b_bar, k_bar)]
        yield

        y_rows = []
        for ci in range(sec_chunks):
            y_groups = []
            for gi in range(n_groups):
                j = ci * n_groups + gi
                s0 = state_ref[gi]
                y_groups.append(_packed_dot1(r_hat[j], s0, hm, _NT) + y0[j])
                w_tot = jnp.exp(tots[ci][:, gi * gw:(gi + 1) * gw])
                state_ref[gi] = s0 * w_tot + _packed_dot_wide_lhs(s0, g_mat[j], hm) + h_mat[j]
            y_rows.append(jnp.concatenate(y_groups, axis=1))
            yield
        y = jnp.concatenate(y_rows, axis=0)

        hsum = hsum_ref[...]
        inv_n = 1.0 / n
        mu = _head_sums(y, hsum) * inv_n
        yc = y - mu
        var = _head_sums(yc * yc, hsum) * inv_n
        yn = yc * lax.rsqrt(var + GN_EPS) * gng_ref[...] + gnb_ref[...]
        bonus = _head_sums(r * k * rk_ref[...], hsum)
        o_ref[0, rs, :] = ((yn + bonus * v) * g_ref[0, rs, :]).astype(o_ref.dtype)

    assert RWKV_SECTION_LAG >= sec_chunks
    _run_skewed([section(si * sec_rows) for si in range(RWKV_SECTIONS)], lag=RWKV_SECTION_LAG)


def _rwkv(r, k, v, lw, kk, b, g, rw_rk, gn_g, gn_b, hsum, *, chunk, rows):
    bsz, seq, d = r.shape
    n_groups = d // (RWKV_GROUP_HEADS * HEAD_DIM)
    blk = pl.BlockSpec((1, rows, d), lambda bi, ci: (bi, ci, 0))
    const = lambda bi, ci: (0, 0)
    vec = lambda a: a.reshape(1, d)
    return pl.pallas_call(
        functools.partial(_rwkv_kernel, chunk=chunk),
        grid=(bsz, seq // rows),
        in_specs=[blk] * 7 + [pl.BlockSpec((1, d), const)] * 3 + [pl.BlockSpec(hsum.shape, const)],
        out_specs=blk,
        out_shape=jax.ShapeDtypeStruct((bsz, seq, d), BF16),
        scratch_shapes=[pltpu.VMEM((n_groups, HEAD_DIM, RWKV_GROUP_HEADS * HEAD_DIM), F32)],
        compiler_params=pltpu.CompilerParams(dimension_semantics=("arbitrary", "arbitrary"),
                                             vmem_limit_bytes=VMEM_LIMIT),
        name="rwkv",
    )(r, k, v, lw, kk, b, g, vec(rw_rk), vec(gn_g), vec(gn_b), hsum)


SB_LOG_UNDERFLOW = 105.0


def _sb_kernel(q_ref, k_ref, v_ref, o_ref, acc_ref, run_ref, *, blk):
    qi = pl.program_id(1)
    pairs = k_ref.shape[2] // (2 * HEAD_DIM)
    row = lax.broadcasted_iota(jnp.int32, (blk, blk), 0)
    col = lax.broadcasted_iota(jnp.int32, (blk, blk), 1)
    causal = col < row
    jj = lax.broadcasted_iota(jnp.int32, (blk, 2 * blk), 0)
    ss = lax.broadcasted_iota(jnp.int32, (blk, 2 * blk), 1)
    suffix = jnp.where((jj >= ss) | (ss >= blk), 1.0, 0.0).astype(BF16)
    first_head = lax.broadcasted_iota(jnp.int32, (1, 2 * HEAD_DIM), 1) < HEAD_DIM
    zero = jnp.zeros((), BF16)

    lanes = [slice(p * 2 * HEAD_DIM, (p + 1) * 2 * HEAD_DIM) for p in range(pairs)]

    pw = 2 * HEAD_DIM

    def tile(kb, diagonal):
        start = pl.multiple_of(kb * blk, blk)
        zs = []
        for p in range(pairs):
            q2 = jnp.concatenate([q_ref[0, :, (2 * p + hh) * pw:(2 * p + hh + 1) * pw] for hh in range(2)],
                                 axis=0)
            z2 = _dot(q2, k_ref[0, pl.ds(start, blk), lanes[p]], _NT)
            zs += [z2[:blk], z2[blk:]]
        sps = [_softplus(z) for z in zs]
        if diagonal:
            sps = [jnp.where(causal, sp, 0.0) for sp in sps]
        css = []
        for p in range(pairs):
            c2 = _dot(jnp.concatenate(sps[2 * p:2 * p + 2], axis=0).astype(BF16), suffix)
            css += [c2[:blk], c2[blk:]]
        atts = []
        low = None
        for n_, (z, cs) in enumerate(zip(zs, css)):
            logw = z - cs[:, :blk]
            if diagonal:
                att = jnp.where(causal, jnp.exp(logw), 0.0)
                run = cs[:, blk:]
            else:
                prev = run_ref[n_]
                att = jnp.exp(logw - prev)
                run = prev + cs[:, blk:]
            run_ref[n_] = run
            low = run if low is None else jnp.minimum(low, run)
            atts.append(att.astype(BF16))
        done = (jnp.min(low) > SB_LOG_UNDERFLOW).astype(jnp.int32)
        for p in range(pairs):
            vp = v_ref[0, pl.ds(start, blk), lanes[p]]
            vv = jnp.concatenate([jnp.where(first_head, vp, zero),
                                  jnp.where(first_head, zero, vp)], axis=0)
            out = _dot(jnp.concatenate(atts[2 * p:2 * p + 2], axis=1), vv)
            if diagonal:
                acc_ref[:, lanes[p]] = out
            else:
                acc_ref[:, lanes[p]] += out
        return done

    done0 = tile(qi, True)

    def cond(c):
        i, done = c
        return jnp.logical_and(i <= qi, done == 0)

    def body(c):
        i, _ = c
        return i + 1, tile(qi - i, False)

    lax.while_loop(cond, body, (jnp.int32(1), done0))
    o_ref[0] = acc_ref[...].astype(o_ref.dtype)


def _sb_attention(q, k, v, *, blk):
    bsz, seq, d = k.shape
    heads = d // HEAD_DIM
    qspec = pl.BlockSpec((1, blk, 2 * d), lambda b, i: (b, i, 0))
    ospec = pl.BlockSpec((1, blk, d), lambda b, i: (b, i, 0))
    kvspec = pl.BlockSpec((1, seq, d), lambda b, i: (b, 0, 0))
    return pl.pallas_call(
        functools.partial(_sb_kernel, blk=blk),
        grid=(bsz, seq // blk),
        in_specs=[qspec, kvspec, kvspec],
        out_specs=ospec,
        out_shape=jax.ShapeDtypeStruct((bsz, seq, d), BF16),
        scratch_shapes=[pltpu.VMEM((blk, d), F32), pltpu.VMEM((heads, blk, blk), F32)],
        compiler_params=pltpu.CompilerParams(dimension_semantics=("arbitrary", "arbitrary"),
                                             vmem_limit_bytes=VMEM_LIMIT),
        name="sbattn",
    )(q, k, v)


def _merge_kernel(x_ref, ada_ref, ya_ref, ob_ref, ga_ref, gb_ref, wa_ref, wb_ref, wo_ref,
                  lng_ref, lnb_ref, o_ref, *, alpha):
    g1 = ada_ref[0, 2:3, :]
    half = x_ref.shape[0] // 2

    def rows_program(rows):
        y_a = _dot(ya_ref[rows, :], wa_ref[...])
        y_b = _dot(ob_ref[rows, :], wb_ref[...])
        yield
        merged = ga_ref[rows, :].astype(F32) * y_a + gb_ref[rows, :].astype(F32) * y_b
        mix = _dot(merged.astype(BF16), wo_ref[...])
        yield
        o_ref[rows, :] = _norm_rows(alpha * x_ref[rows, :] + g1 * mix, LN_EPS) * lng_ref[...] + lnb_ref[...]

    _run_skewed([rows_program(slice(0, half)), rows_program(slice(half, 2 * half))], lag=1)


def _merge(x2, ada3, ya, ob, ga, gb, wa_bf, wb_bf, wo_bf, ln_g, ln_b, *, seq, tm, alpha):
    m, d = x2.shape
    tiles_per_batch = seq // tm
    row = lambda i: (i, 0)
    const = lambda i: (0, 0)
    rows = lambda a: pl.BlockSpec((tm, a.shape[1]), row)
    full = lambda a: pl.BlockSpec(a.shape, const, pipeline_mode=pl.Buffered(1))
    vec = lambda a: a.reshape(1, -1)
    ins = [x2, ada3, ya, ob, ga, gb, wa_bf, wb_bf, wo_bf, vec(ln_g), vec(ln_b)]
    in_specs = [rows(x2), pl.BlockSpec((1,) + ada3.shape[1:], lambda i: (i // tiles_per_batch, 0, 0)),
                rows(ya), rows(ob), rows(ga), rows(gb)] + [full(a) for a in ins[6:]]
    return pl.pallas_call(
        functools.partial(_merge_kernel, alpha=alpha),
        grid=(m // tm,),
        in_specs=in_specs,
        out_specs=pl.BlockSpec((tm, d), row),
        out_shape=jax.ShapeDtypeStruct((m, d), F32),
        compiler_params=pltpu.CompilerParams(dimension_semantics=("arbitrary",),
                                             vmem_limit_bytes=VMEM_LIMIT),
        name="merge",
    )(*ins)


def _ffn_kernel(x_ref, ada_ref, w1_ref, b1_ref, w2_ref, b2_ref, lng_ref, lnb_ref, o_ref, *, alpha, fchunk):
    sh = ada_ref[0, 3:4, :]
    sc = ada_ref[0, 4:5, :]
    g2 = ada_ref[0, 5:6, :]
    d_ff = w1_ref.shape[1]
    n_chunks = d_ff // fchunk
    half = x_ref.shape[0] // 2

    def rows_program(rows):
        x = x_ref[rows, :]
        hb = (_norm_rows(x, ADALN_EPS) * (1.0 + sc) + sh).astype(BF16)
        yield
        ff = None
        for j in range(n_chunks):
            sl = slice(j * fchunk, (j + 1) * fchunk)
            t = jnp.maximum(_dot(hb, w1_ref[:, sl]) + b1_ref[:, sl], 0.0)
            part = _dot((t * t).astype(BF16), w2_ref[sl, :])
            ff = part if ff is None else ff + part
            yield
        ff = ff + b2_ref[...]
        o_ref[rows, :] = _norm_rows(alpha * x + g2 * ff, LN_EPS) * lng_ref[...] + lnb_ref[...]

    _run_skewed([rows_program(slice(0, half)), rows_program(slice(half, 2 * half))], lag=2)


def _ffn(x1, ada3, w1_bf, b1, w2_bf, b2, ln_g, ln_b, *, seq, tm, alpha, fchunk):
    m, d = x1.shape
    tiles_per_batch = seq // tm
    row = lambda i: (i, 0)
    const = lambda i: (0, 0)
    full = lambda a: pl.BlockSpec(a.shape, const, pipeline_mode=pl.Buffered(1))
    vec = lambda a: a.reshape(1, -1)
    ins = [x1, ada3, w1_bf, vec(b1), w2_bf, vec(b2), vec(ln_g), vec(ln_b)]
    in_specs = [pl.BlockSpec((tm, d), row),
                pl.BlockSpec((1,) + ada3.shape[1:], lambda i: (i // tiles_per_batch, 0, 0))]
    in_specs += [full(a) for a in ins[2:]]
    return pl.pallas_call(
        functools.partial(_ffn_kernel, alpha=alpha, fchunk=fchunk),
        grid=(m // tm,),
        in_specs=in_specs,
        out_specs=pl.BlockSpec((tm, d), row),
        out_shape=jax.ShapeDtypeStruct((m, d), F32),
        compiler_params=pltpu.CompilerParams(dimension_semantics=("arbitrary",),
                                             vmem_limit_bytes=VMEM_LIMIT),
        name="ffn",
    )(*ins)


def _mixmlp_kernel(x_ref, ada_ref, ya_ref, ob_ref, ga_ref, gb_ref, wa_ref, wb_ref, wo_ref, l1g_ref, l1b_ref,
                   w1_ref, b1_ref, w2_ref, b2_ref, l2g_ref, l2b_ref, o_ref, *, alpha, fchunk):
    g1 = ada_ref[0, 2:3, :]
    sh = ada_ref[0, 3:4, :]
    sc = ada_ref[0, 4:5, :]
    g2 = ada_ref[0, 5:6, :]
    n_chunks = w1_ref.shape[1] // fchunk
    half = x_ref.shape[0] // 2

    def rows_program(rows):
        y_a = _dot(ya_ref[rows, :], wa_ref[...])
        y_b = _dot(ob_ref[rows, :], wb_ref[...])
        yield
        merged = ga_ref[rows, :].astype(F32) * y_a + gb_ref[rows, :].astype(F32) * y_b
        mix = _dot(merged.astype(BF16), wo_ref[...])
        yield
        x1 = _norm_rows(alpha * x_ref[rows, :] + g1 * mix, LN_EPS) * l1g_ref[...] + l1b_ref[...]
        hb = (_norm_rows(x1, ADALN_EPS) * (1.0 + sc) + sh).astype(BF16)
        yield
        ff = None
        for j in range(n_chunks):
            sl = slice(j * fchunk, (j + 1) * fchunk)
            t = jnp.maximum(_dot(hb, w1_ref[:, sl]) + b1_ref[:, sl], 0.0)
            part = _dot((t * t).astype(BF16), w2_ref[sl, :])
            ff = part if ff is None else ff + part
            yield
        ff = ff + b2_ref[...]
        o_ref[rows, :] = _norm_rows(alpha * x1 + g2 * ff, LN_EPS) * l2g_ref[...] + l2b_ref[...]

    _run_skewed([rows_program(slice(0, half)), rows_program(slice(half, 2 * half))], lag=2)


def _mixmlp(x2, ada3, ya, ob, ga, gb, wa_bf, wb_bf, wo_bf, ln1_g, ln1_b, w1_bf, b1, w2_bf, b2, ln2_g, ln2_b,
            *, seq, tm, alpha, fchunk):
    m, d = x2.shape
    tiles_per_batch = seq // tm
    row = lambda i: (i, 0)
    const = lambda i: (0, 0)
    rows = lambda a: pl.BlockSpec((tm, a.shape[1]), row)
    full = lambda a: pl.BlockSpec(a.shape, const, pipeline_mode=pl.Buffered(1))
    vec = lambda a: a.reshape(1, -1)
    ins = [x2, ada3, ya, ob, ga, gb, wa_bf, wb_bf, wo_bf, vec(ln1_g), vec(ln1_b),
           w1_bf, vec(b1), w2_bf, vec(b2), vec(ln2_g), vec(ln2_b)]
    in_specs = [rows(x2), pl.BlockSpec((1,) + ada3.shape[1:], lambda i: (i // tiles_per_batch, 0, 0)),
                rows(ya), rows(ob), rows(ga), rows(gb)] + [full(a) for a in ins[6:]]
    return pl.pallas_call(
        functools.partial(_mixmlp_kernel, alpha=alpha, fchunk=fchunk),
        grid=(m // tm,),
        in_specs=in_specs,
        out_specs=pl.BlockSpec((tm, d), row),
        out_shape=jax.ShapeDtypeStruct((m, d), F32),
        compiler_params=pltpu.CompilerParams(dimension_semantics=("arbitrary",),
                                             vmem_limit_bytes=VMEM_LIMIT),
        name="mixmlp",
    )(*ins)


def _branches(proj, rw_rk, rw_gn_g, rw_gn_b, hsum, bsz, seq):
    r, k, v, lw, kk, b, g, q, ks, vs = proj
    seq3 = lambda a: a.reshape(bsz, seq, -1)
    ya = _rwkv(seq3(r), seq3(k), seq3(v), seq3(lw), seq3(kk), seq3(b), seq3(g),
               rw_rk, rw_gn_g, rw_gn_b, hsum, chunk=min(RWKV_CHUNK, seq),
               rows=min(RWKV_CHUNKS_PER_STEP * RWKV_CHUNK, seq))
    o = _sb_attention(seq3(q), seq3(ks), seq3(vs), blk=min(SB_BLOCK, seq))
    return ya.reshape(bsz * seq, -1), o.reshape(bsz * seq, -1)


def _layer(x, c, w_ada, b_ada, w_in, mu_shift, rw_w0, rw_w2, rw_a0, rw_a2, rw_g2, rw_kk, rw_ka,
           rw_rk, rw_gn_g, rw_gn_b, w_branch_a, w_branch_b, w_out, ln1_g, ln1_b,
           w_ff1, b_ff1, w_ff2, b_ff2, ln2_g, ln2_b, *, alpha):
    bsz, seq, d = x.shape
    d_rwkv = rw_w0.shape[-1]
    tm = min(INPROJ_ROWS, seq)
    tm2 = min(MLP_ROWS, seq)
    rwkv_rows = min(RWKV_CHUNKS_PER_STEP * RWKV_CHUNK, seq)
    group_w = RWKV_GROUP_HEADS * HEAD_DIM
    assert seq % tm == 0 and seq % tm2 == 0 and seq % rwkv_rows == 0 and seq % min(SB_BLOCK, seq) == 0, seq
    assert rwkv_rows % (RWKV_SECTIONS * RWKV_CHUNK) == 0, (rwkv_rows, RWKV_SECTIONS)
    assert d_rwkv % group_w == 0 and w_branch_b.shape[0] % (2 * HEAD_DIM) == 0, (d_rwkv, w_branch_b.shape)
    assert HEAD_DIM & (HEAD_DIM - 1) == 0 and RWKV_CHUNK & (RWKV_CHUNK - 1) == 0

    ada3 = _ada(c, w_ada, b_ada).reshape(bsz, 6, d)
    lane = jnp.arange(RWKV_GROUP_HEADS * HEAD_DIM) // HEAD_DIM
    hsum = (lane[:, None] == lane[None, :]).astype(BF16)

    x2 = x.reshape(bsz * seq, d)
    (r, k, v, lw, kk, b, g, q, ks, vs, ga, gb) = _inproj(
        x2, ada3, w_in.astype(BF16), mu_shift, rw_w0, rw_w2, rw_a0, rw_a2, rw_g2, rw_kk, rw_ka, hsum,
        seq=seq, tm=tm)

    ya, ob = _branches((r, k, v, lw, kk, b, g, q, ks, vs), rw_rk, rw_gn_g, rw_gn_b, hsum, bsz, seq)

    out = _mixmlp(x2, ada3, ya, ob, ga, gb,
                  w_branch_a.astype(BF16), w_branch_b.astype(BF16), w_out.astype(BF16), ln1_g, ln1_b,
                  w_ff1.astype(BF16), b_ff1, w_ff2.astype(BF16), b_ff2, ln2_g, ln2_b,
                  seq=seq, tm=min(512, seq), alpha=alpha, fchunk=min(1024, w_ff1.shape[-1]))
    return out.reshape(bsz, seq, d)


def kernel(x, c, w_ada, b_ada, w_in, mu_shift, rw_w0, rw_w2, rw_a0, rw_a2, rw_g2, rw_kk, rw_ka, rw_rk,
           rw_gn_g, rw_gn_b, w_branch_a, w_branch_b, w_out, ln1_g, ln1_b, w_ff1, b_ff1, w_ff2, b_ff2,
           ln2_g, ln2_b):
    in_dtype = x.dtype
    layer_params = (w_ada, b_ada, w_in, mu_shift, rw_w0, rw_w2, rw_a0, rw_a2, rw_g2, rw_kk, rw_ka,
                    rw_rk, rw_gn_g, rw_gn_b, w_branch_a, w_branch_b, w_out, ln1_g, ln1_b,
                    w_ff1, b_ff1, w_ff2, b_ff2, ln2_g, ln2_b)
    depth = w_ada.shape[0]
    alpha = (2.0 * depth) ** 0.25
    for l in range(depth):
        x = _layer(x, c, *[p[l] for p in layer_params], alpha=alpha)
    return x.astype(in_dtype)
```

```python
import functools

import jax
import jax.numpy as jnp
from jax import lax
from jax.experimental import pallas as pl
from jax.experimental.pallas import tpu as pltpu

F32 = jnp.float32
BF16 = jnp.bfloat16

HEAD_DIM = 64
LN_EPS = 1e-5
ADALN_EPS = 1e-6
GN_EPS = 64e-5
RWKV_CHUNK = 64
RWKV_SECTIONS = 2
RWKV_SECTION_CHUNKS = 4
RWKV_SECTION_LAG = 12
RWKV_CHUNKS_PER_STEP = RWKV_SECTIONS * RWKV_SECTION_CHUNKS
SB_BLOCK = 256
INPROJ_ROWS = 512
MLP_ROWS = 512
VMEM_LIMIT = 56 * 1024 * 1024


def _split2(a):
    hi = a.astype(BF16)
    lo = (a - hi.astype(F32)).astype(BF16)
    return hi, lo


_NN = (((1,), (0,)), ((), ()))
_NT = (((1,), (1,)), ((), ()))
_TN = (((0,), (0,)), ((), ()))


def _dot(a, b, dims=_NN):
    return lax.dot_general(a, b, dims, preferred_element_type=F32)


def _head_sums(x, ones_blk):
    m, w = x.shape[0], ones_blk.shape[0]
    hi, lo = _split2(x)
    outs = []
    for g in range(x.shape[1] // w):
        cols = slice(g * w, (g + 1) * w)
        both = _dot(jnp.concatenate([hi[:, cols], lo[:, cols]], axis=0), ones_blk)
        outs.append(both[:m] + both[m:])
    return jnp.concatenate(outs, axis=1)


def _dot_ones_lhs(a_bf, b):
    hi, lo = _split2(b)
    return _dot(a_bf, hi) + _dot(a_bf, lo)


_DONE = object()


def _run_skewed(programs, lag):
    live = [True] * len(programs)
    tick = 0
    while any(live):
        for i, prog in enumerate(programs):
            if live[i] and tick >= i * lag:
                live[i] = next(prog, _DONE) is not _DONE
        tick += 1


_NEG_LOG2_E = -1.4426950408889634


def _softplus(y):
    return jnp.maximum(y, 0.0) + jnp.log(1.0 + jnp.exp2(jnp.abs(y) * _NEG_LOG2_E))


def _sigmoid(y):
    return 1.0 / (1.0 + jnp.exp(-y))


def _norm_rows(x, eps):
    mu = jnp.mean(x, axis=-1, keepdims=True)
    xc = x - mu
    var = jnp.mean(xc * xc, axis=-1, keepdims=True)
    return xc * lax.rsqrt(var + eps)


def _ada_kernel(c_ref, w_ref, b_ref, o_ref):
    c = c_ref[...]
    s = c * _sigmoid(c)
    m = s.shape[0]
    sh, sl = _split2(s)
    wh, wl = _split2(w_ref[...])
    top = _dot(jnp.concatenate([sh, sl], axis=0), wh)
    o_ref[...] = top[:m] + (top[m:] + _dot(sh, wl)) + b_ref[...]


def _ada(c, w_ada, b_ada):
    bsz, d = c.shape
    n = w_ada.shape[1]
    wide = 2 * d if n % (2 * d) == 0 else d
    return pl.pallas_call(
        _ada_kernel,
        grid=(n // wide,),
        in_specs=[pl.BlockSpec((bsz, d), lambda j: (0, 0)),
                  pl.BlockSpec((d, wide), lambda j: (0, j)),
                  pl.BlockSpec((1, wide), lambda j: (0, j))],
        out_specs=pl.BlockSpec((bsz, wide), lambda j: (0, j)),
        out_shape=jax.ShapeDtypeStruct((bsz, n), F32),
        name="ada",
    )(c, w_ada, b_ada.reshape(1, n))


def _inproj_kernel(x_ref, ada_ref, win_ref, mu_ref, w0_ref, w2_ref, a0_ref, a2_ref, g2_ref,
                   kkw_ref, kaw_ref, hsum_ref,
                   r_ref, k_ref, v_ref, lw_ref, kk_ref, b_ref, g_ref,
                   q_ref, ks_ref, vs_ref, ga_ref, gb_ref,
                   carry_ref, *, tiles_per_batch, d_rwkv, d_decay, d_aaa, d_gate, d_sb, d_model):
    i = pl.program_id(0)
    tm = x_ref.shape[0]
    d_shift = 3 * d_rwkv + d_decay + d_aaa + d_gate

    sh = ada_ref[0, 0:1, :]
    sc = ada_ref[0, 1:2, :]

    @pl.when(i % tiles_per_batch == 0)
    def _():
        carry_ref[...] = jnp.zeros_like(carry_ref)

    half = d_model // 2
    pair_w = 2 * HEAD_DIM
    first_head = lax.broadcasted_iota(jnp.int32, (1, pair_w), 1) < HEAD_DIM

    def rows_program(rows):
        hm = rows.stop - rows.start
        hb = (_norm_rows(x_ref[rows, :], ADALN_EPS) * (1.0 + sc) + sh).astype(BF16)
        row0 = lax.broadcasted_iota(jnp.int32, (hm, 1), 0) == 0
        yield

        def shifted(lo, width):
            cols = slice(lo, lo + width)
            z = _dot(hb, win_ref[:, cols])
            prev = jnp.where(row0, carry_ref[:, cols], pltpu.roll(z, 1, 0))
            carry_ref[:, cols] = z[hm - 1:hm, :]
            return z + mu_ref[:, cols] * (prev - z)

        def store_queries(z):
            zq = (z * (HEAD_DIM ** -0.5)).astype(BF16)
            zero = jnp.zeros((), BF16)
            for p in range(d_sb // pair_w):
                qp = zq[:, p * pair_w:(p + 1) * pair_w]
                q_ref[rows, (2 * p) * pair_w:(2 * p + 1) * pair_w] = jnp.where(first_head, qp, zero)
                q_ref[rows, (2 * p + 1) * pair_w:(2 * p + 2) * pair_w] = jnp.where(first_head, zero, qp)

        def plain(ref, cols, fn=lambda z: z):
            def store(z):
                ref[rows, cols] = fn(z).astype(BF16)
            return store

        all_cols = slice(0, d_sb)
        rest = [(store_queries, d_sb), (plain(ks_ref, all_cols), d_sb), (plain(vs_ref, all_cols), d_sb),
                (plain(ga_ref, slice(0, half), _sigmoid), half),
                (plain(ga_ref, slice(half, d_model), _sigmoid), half),
                (plain(gb_ref, slice(0, half), _sigmoid), half),
                (plain(gb_ref, slice(half, d_model), _sigmoid), half)]
        rest_col = [d_shift]

        def project_next():
            store, width = rest.pop(0)
            store(_dot(hb, win_ref[:, rest_col[0]:rest_col[0] + width]))
            rest_col[0] += width

        zl = shifted(3 * d_rwkv, d_decay + d_aaa + d_gate)
        zw = zl[:, :d_decay]
        za = zl[:, d_decay:d_decay + d_aaa]
        zg = zl[:, d_decay + d_aaa:]
        k = shifted(d_rwkv, d_rwkv)
        yield
        ww = w0_ref[...] + _dot(jnp.tanh(zw).astype(BF16), w2_ref[...])
        w_log = -_softplus(-ww) - 0.5
        lw_ref[rows, :] = -jnp.exp(w_log)
        project_next()
        yield
        arate = _sigmoid(a0_ref[...] + _dot(za.astype(BF16), a2_ref[...]))
        g_ref[rows, :] = _dot(_sigmoid(zg).astype(BF16), g2_ref[...])
        r_ref[rows, :] = shifted(0, d_rwkv)
        yield
        kkraw = k * kkw_ref[...]
        ssq = _head_sums(kkraw * kkraw, hsum_ref[...])
        kk = kkraw / jnp.maximum(jnp.sqrt(ssq), 1e-12)
        v_ref[rows, :] = shifted(2 * d_rwkv, d_rwkv)
        k_ref[rows, :] = k * (1.0 + (arate - 1.0) * kaw_ref[...])
        yield
        project_next()
        kk_ref[rows, :] = kk
        b_ref[rows, :] = kk * arate
        yield
        while rest:
            project_next()
            yield

    hm = tm // 2
    _run_skewed([rows_program(slice(0, hm)), rows_program(slice(hm, tm))], lag=2)


def _inproj(x2, ada3, win_bf, mu_shift, rw_w0, rw_w2, rw_a0, rw_a2, rw_g2, rw_kk, rw_ka, hsum,
            *, seq, tm):
    m, d_model = x2.shape
    d_rwkv = rw_w0.shape[-1]
    d_decay, d_aaa, d_gate = rw_w2.shape[0], rw_a2.shape[0], rw_g2.shape[0]
    d_shift = 3 * d_rwkv + d_decay + d_aaa + d_gate
    d_sb = (win_bf.shape[1] - d_shift - 2 * d_model) // 3
    tiles_per_batch = seq // tm
    row = lambda i: (i, 0)
    const = lambda i: (0, 0)
    full = lambda a: pl.BlockSpec(a.shape, const, pipeline_mode=pl.Buffered(1))
    vec = lambda a: a.reshape(1, -1)
    ins = [x2, ada3, win_bf, vec(mu_shift), vec(rw_w0), rw_w2.astype(BF16), vec(rw_a0),
           rw_a2.astype(BF16), rw_g2.astype(BF16), vec(rw_kk), vec(rw_ka), hsum]
    in_specs = [pl.BlockSpec((tm, d_model), row),
                pl.BlockSpec((1,) + ada3.shape[1:], lambda i: (i // tiles_per_batch, 0, 0))]
    in_specs += [full(a) for a in ins[2:]]
    widths = [d_rwkv] * 7 + [2 * d_sb, d_sb, d_sb] + [d_model] * 2
    dtypes = [F32] * 7 + [BF16] * 5
    out_shape = [jax.ShapeDtypeStruct((m, w), dt) for w, dt in zip(widths, dtypes)]
    out_specs = [pl.BlockSpec((tm, w), row) for w in widths]
    kern = functools.partial(_inproj_kernel, tiles_per_batch=tiles_per_batch, d_rwkv=d_rwkv,
                             d_decay=d_decay, d_aaa=d_aaa, d_gate=d_gate, d_sb=d_sb, d_model=d_model)
    return pl.pallas_call(
        kern,
        grid=(m // tm,),
        in_specs=in_specs,
        out_specs=out_specs,
        out_shape=out_shape,
        scratch_shapes=[pltpu.VMEM((1, d_shift), F32)],
        compiler_params=pltpu.CompilerParams(dimension_semantics=("arbitrary",),
                                             vmem_limit_bytes=VMEM_LIMIT),
        name="inproj",
    )(*ins)


RWKV_GROUP_HEADS = 4


def _block_diag(q, head_masks):
    zero = jnp.zeros((), q.dtype)
    return jnp.concatenate([jnp.where(m, q, zero) for m in head_masks], axis=0)


def _packed_dot1(lhs, q, head_masks, dims=_NN):
    return _dot(lhs.astype(BF16), _block_diag(q.astype(BF16), head_masks), dims)


def _packed_dot_wide_lhs(lhs, q, head_masks, dims=_NN):
    m = lhs.shape[0]
    lh, ll = _split2(lhs)
    both = _dot(jnp.concatenate([lh, ll], axis=0), _block_diag(q.astype(BF16), head_masks), dims)
    return both[:m] + both[m:]


def _rwkv_kernel(r_ref, k_ref, v_ref, lw_ref, kk_ref, b_ref, g_ref, rk_ref, gng_ref, gnb_ref, hsum_ref,
                 o_ref, state_ref, *, chunk):
    c = pl.program_id(1)
    n = HEAD_DIM
    rows, d = r_ref.shape[1], r_ref.shape[2]
    gw = RWKV_GROUP_HEADS * n
    n_groups = d // gw

    @pl.when(c == 0)
    def _():
        state_ref[...] = jnp.zeros_like(state_ref)

    t_idx = lax.broadcasted_iota(jnp.int32, (chunk, gw), 0)
    lane = lax.broadcasted_iota(jnp.int32, (chunk, gw), 1)
    i_idx = lane & (n - 1)
    strict = t_idx > i_idx
    incl = t_idx >= i_idx
    lane1 = lax.broadcasted_iota(jnp.int32, (1, gw), 1)
    hm = [(lane1 >> (n.bit_length() - 1)) == h for h in range(RWKV_GROUP_HEADS)]
    same = lambda s: (t_idx >> s) == (i_idx >> s)

    half_w = gw // 2
    lane_half = [(lane1 >> (half_w.bit_length() - 1)) == j for j in range(2)]
    second_in_half = ((lane >> (n.bit_length() - 1)) & 1) == 1

    def halves_rows(x):
        return jnp.concatenate([x[:, :half_w], x[:, half_w:]], axis=0).astype(BF16)

    def halves_masked(w):
        return _block_diag(w.astype(BF16), lane_half)

    def same_head(out):
        return jnp.where(second_in_half, out[n:], out[:n])

    lane_h = lax.broadcasted_iota(jnp.int32, (1, half_w), 1)
    in_half = [(lane_h >> (n.bit_length() - 1)) == j for j in range(half_w // n)]

    def two_rhs(lhs, x, y, dims=_NN):
        lb, xb, yb = lhs.astype(BF16), x.astype(BF16), y.astype(BF16)
        out_x, out_y = [], []
        for j in range(gw // half_w):
            cols = slice(j * half_w, (j + 1) * half_w)
            w = jnp.concatenate([_block_diag(xb[:, cols], in_half), _block_diag(yb[:, cols], in_half)],
                                axis=1 if dims == _NN else 0)
            o = _dot(lb[:, cols], w, dims)
            out_x.append(o[:, :half_w])
            out_y.append(o[:, half_w:])
        return jnp.concatenate(out_x, axis=1), jnp.concatenate(out_y, axis=1)

    sec_rows = rows // RWKV_SECTIONS
    sec_chunks = sec_rows // chunk
    rr = lax.broadcasted_iota(jnp.int32, (sec_rows, sec_rows), 0)
    cc = lax.broadcasted_iota(jnp.int32, (sec_rows, sec_rows), 1)
    cs = chunk.bit_length() - 1
    ltri = jnp.where((rr >= cc) & ((rr >> cs) == (cc >> cs)), 1.0, 0.0).astype(BF16)
    eye = jnp.where(t_idx == i_idx, 1.0, 0.0)
    chains = [(ci, gi) for ci in range(sec_chunks) for gi in range(n_groups)]
    win = lambda x, ci, gi: x[ci * chunk:(ci + 1) * chunk, gi * gw:(gi + 1) * gw]

    def section(row0):
        rs = slice(row0, row0 + sec_rows)
        r, k, v, lw = r_ref[0, rs, :], k_ref[0, rs, :], v_ref[0, rs, :], lw_ref[0, rs, :]
        kk, b = kk_ref[0, rs, :], b_ref[0, rs, :]
        cum = _dot_ones_lhs(ltri, lw)
        a_t = -kk * jnp.exp(cum - lw)
        r_t = r * jnp.exp(cum)
        e_neg = jnp.exp(-cum)
        b_t = b * e_neg
        k_t = k * e_neg
        tots = [cum[(ci + 1) * chunk - 1:(ci + 1) * chunk, :] for ci in range(sec_chunks)]
        e_bars = [jnp.exp(tots[ci] - cum[ci * chunk:(ci + 1) * chunk, :]) for ci in range(sec_chunks)]
        b_bar = [win(b, ci, gi) * e_bars[ci][:, gi * gw:(gi + 1) * gw] for ci, gi in chains]
        k_bar = [win(k, ci, gi) * e_bars[ci][:, gi * gw:(gi + 1) * gw] for ci, gi in chains]
        at = [win(a_t, ci, gi) for ci, gi in chains]
        rt = [win(r_t, ci, gi) for ci, gi in chains]
        vv = [win(v, ci, gi) for ci, gi in chains]
        yield

        ar = [jnp.concatenate([a_, r_], axis=0) for a_, r_ in zip(at, rt)]
        grams = [two_rhs(x, win(b_t, ci, gi), win(k_t, ci, gi), _NT) for x, (ci, gi) in zip(ar, chains)]
        gram_b = [x[0] for x in grams]
        gram_k = [x[1] for x in grams]
        yield
        m_ab = [jnp.where(strict, g_[:chunk], 0.0) for g_ in gram_b]
        p_rb = [jnp.where(incl, g_[chunk:], 0.0) for g_ in gram_b]
        m_ak = [jnp.where(strict, g_[:chunk], 0.0) for g_ in gram_k]
        p_rk = [jnp.where(incl, g_[chunk:], 0.0) for g_ in gram_k]
        yield

        ts = [eye + jnp.where(same(1), m, 0.0) for m in m_ab]
        s = 1
        while (1 << s) < chunk:
            level = same(s + 1) & jnp.logical_not(same(s))
            offs = [jnp.where(level, m, 0.0) for m in m_ab]
            mids = [_packed_dot1(t, o, hm) for t, o in zip(ts, offs)]
            yield
            ts = [t + _packed_dot1(md, t, hm) for t, md in zip(ts, mids)]
            yield
            s += 1

        both = [_packed_dot1(jnp.concatenate([m, pk], axis=0), x, hm) for m, pk, x in zip(m_ak, p_rk, vv)]
        mv = [x[:chunk] for x in both]
        pkv = [x[chunk:] for x in both]
        yield
        hat = [two_rhs(t, x, y) for t, x, y in zip(ts, at, mv)]
        a_hat = [x[0] for x in hat]
        u0 = [x[1] for x in hat]
        yield
        pb = [two_rhs(p, a, u) for p, a, u in zip(p_rb, a_hat, u0)]
        r_hat = [x + y[0] for x, y in zip(rt, pb)]
        y0 = [x + y[1] for x, y in zip(pkv, pb)]
        yield
        g_mat = [same_head(_dot(halves_rows(a), halves_masked(bb), _TN))
                 for a, bb in zip(a_hat, b_bar)]
        h_mat = [same_head(_dot(jnp.concatenate([halves_rows(u), halves_rows(x)], axis=0),
                                jnp.concatenate([halves_masked(bb), halves_masked(kb)], axis=0), _TN))
                 for u, x, bb, kb in zip(u0, vv, b_bar, k_bar)]
        yield

        y_rows = []
        for ci in range(sec_chunks):
            y_groups = []
            for gi in range(n_groups):
                j = ci * n_groups + gi
                s0 = state_ref[gi]
                y_groups.append(_packed_dot1(r_hat[j], s0, hm, _NT) + y0[j])
                w_tot = jnp.exp(tots[ci][:, gi * gw:(gi + 1) * gw])
                state_ref[gi] = s0 * w_tot + _packed_dot_wide_lhs(s0, g_mat[j], hm) + h_mat[j]
            y_rows.append(jnp.concatenate(y_groups, axis=1))
            yield
        y = jnp.concatenate(y_rows, axis=0)

        hsum = hsum_ref[...]
        inv_n = 1.0 / n
        mu = _head_sums(y, hsum) * inv_n
        yc = y - mu
        var = _head_sums(yc * yc, hsum) * inv_n
        yn = yc * lax.rsqrt(var + GN_EPS) * gng_ref[...] + gnb_ref[...]
        bonus = _head_sums(r * k * rk_ref[...], hsum)
        o_ref[0, rs, :] = ((yn + bonus * v) * g_ref[0, rs, :]).astype(o_ref.dtype)

    assert RWKV_SECTION_LAG >= sec_chunks
    _run_skewed([section(si * sec_rows) for si in range(RWKV_SECTIONS)], lag=RWKV_SECTION_LAG)


def _rwkv(r, k, v, lw, kk, b, g, rw_rk, gn_g, gn_b, hsum, *, chunk, rows):
    bsz, seq, d = r.shape
    n_groups = d // (RWKV_GROUP_HEADS * HEAD_DIM)
    blk = pl.BlockSpec((1, rows, d), lambda bi, ci: (bi, ci, 0))
    const = lambda bi, ci: (0, 0)
    vec = lambda a: a.reshape(1, d)
    return pl.pallas_call(
        functools.partial(_rwkv_kernel, chunk=chunk),
        grid=(bsz, seq // rows),
        in_specs=[blk] * 7 + [pl.BlockSpec((1, d), const)] * 3 + [pl.BlockSpec(hsum.shape, const)],
        out_specs=blk,
        out_shape=jax.ShapeDtypeStruct((bsz, seq, d), BF16),
        scratch_shapes=[pltpu.VMEM((n_groups, HEAD_DIM, RWKV_GROUP_HEADS * HEAD_DIM), F32)],
        compiler_params=pltpu.CompilerParams(dimension_semantics=("arbitrary", "arbitrary"),
                                             vmem_limit_bytes=VMEM_LIMIT),
        name="rwkv",
    )(r, k, v, lw, kk, b, g, vec(rw_rk), vec(gn_g), vec(gn_b), hsum)


SB_LOG_UNDERFLOW = 105.0


def _sb_kernel(q_ref, k_ref, v_ref, o_ref, acc_ref, run_ref, *, blk):
    qi = pl.program_id(1)
    pairs = k_ref.shape[2] // (2 * HEAD_DIM)
    row = lax.broadcasted_iota(jnp.int32, (blk, blk), 0)
    col = lax.broadcasted_iota(jnp.int32, (blk, blk), 1)
    causal = col < row
    jj = lax.broadcasted_iota(jnp.int32, (blk, 2 * blk), 0)
    ss = lax.broadcasted_iota(jnp.int32, (blk, 2 * blk), 1)
    suffix = jnp.where((jj >= ss) | (ss >= blk), 1.0, 0.0).astype(BF16)
    first_head = lax.broadcasted_iota(jnp.int32, (1, 2 * HEAD_DIM), 1) < HEAD_DIM
    zero = jnp.zeros((), BF16)

    lanes = [slice(p * 2 * HEAD_DIM, (p + 1) * 2 * HEAD_DIM) for p in range(pairs)]

    pw = 2 * HEAD_DIM

    def tile(kb, diagonal):
        start = pl.multiple_of(kb * blk, blk)
        zs = []
        for p in range(pairs):
            q2 = jnp.concatenate([q_ref[0, :, (2 * p + hh) * pw:(2 * p + hh + 1) * pw] for hh in range(2)],
                                 axis=0)
            z2 = _dot(q2, k_ref[0, pl.ds(start, blk), lanes[p]], _NT)
            zs += [z2[:blk], z2[blk:]]
        sps = [_softplus(z) for z in zs]
        if diagonal:
            sps = [jnp.where(causal, sp, 0.0) for sp in sps]
        css = []
        for p in range(pairs):
            c2 = _dot(jnp.concatenate(sps[2 * p:2 * p + 2], axis=0).astype(BF16), suffix)
            css += [c2[:blk], c2[blk:]]
        atts = []
        low = None
        for n_, (z, cs) in enumerate(zip(zs, css)):
            logw = z - cs[:, :blk]
            if diagonal:
                att = jnp.where(causal, jnp.exp(logw), 0.0)
                run = cs[:, blk:]
            else:
                prev = run_ref[n_]
                att = jnp.exp(logw - prev)
                run = prev + cs[:, blk:]
            run_ref[n_] = run
            low = run if low is None else jnp.minimum(low, run)
            atts.append(att.astype(BF16))
        done = (jnp.min(low) > SB_LOG_UNDERFLOW).astype(jnp.int32)
        for p in range(pairs):
            vp = v_ref[0, pl.ds(start, blk), lanes[p]]
            vv = jnp.concatenate([jnp.where(first_head, vp, zero),
                                  jnp.where(first_head, zero, vp)], axis=0)
            out = _dot(jnp.concatenate(atts[2 * p:2 * p + 2], axis=1), vv)
            if diagonal:
                acc_ref[:, lanes[p]] = out
            else:
                acc_ref[:, lanes[p]] += out
        return done

    done0 = tile(qi, True)

    def cond(c):
        i, done = c
        return jnp.logical_and(i <= qi, done == 0)

    def body(c):
        i, _ = c
        return i + 1, tile(qi - i, False)

    lax.while_loop(cond, body, (jnp.int32(1), done0))
    o_ref[0] = acc_ref[...].astype(o_ref.dtype)


def _sb_attention(q, k, v, *, blk):
    bsz, seq, d = k.shape
    heads = d // HEAD_DIM
    qspec = pl.BlockSpec((1, blk, 2 * d), lambda b, i: (b, i, 0))
    ospec = pl.BlockSpec((1, blk, d), lambda b, i: (b, i, 0))
    kvspec = pl.BlockSpec((1, seq, d), lambda b, i: (b, 0, 0))
    return pl.pallas_call(
        functools.partial(_sb_kernel, blk=blk),
        grid=(bsz, seq // blk),
        in_specs=[qspec, kvspec, kvspec],
        out_specs=ospec,
        out_shape=jax.ShapeDtypeStruct((bsz, seq, d), BF16),
        scratch_shapes=[pltpu.VMEM((blk, d), F32), pltpu.VMEM((heads, blk, blk), F32)],
        compiler_params=pltpu.CompilerParams(dimension_semantics=("arbitrary", "arbitrary"),
                                             vmem_limit_bytes=VMEM_LIMIT),
        name="sbattn",
    )(q, k, v)


def _mixmlp_kernel(x_ref, ada_ref, ya_ref, ob_ref, ga_ref, gb_ref, wa_ref, wb_ref, wo_ref, l1g_ref, l1b_ref,
                   w1_ref, b1_ref, w2_ref, b2_ref, l2g_ref, l2b_ref, o_ref, *, alpha, fchunk):
    g1 = ada_ref[0, 2:3, :]
    sh = ada_ref[0, 3:4, :]
    sc = ada_ref[0, 4:5, :]
    g2 = ada_ref[0, 5:6, :]
    n_chunks = w1_ref.shape[1] // fchunk
    half = x_ref.shape[0] // 2

    def rows_program(rows):
        y_a = _dot(ya_ref[rows, :], wa_ref[...])
        y_b = _dot(ob_ref[rows, :], wb_ref[...])
        yield
        merged = ga_ref[rows, :].astype(F32) * y_a + gb_ref[rows, :].astype(F32) * y_b
        mix = _dot(merged.astype(BF16), wo_ref[...])
        yield
        x1 = _norm_rows(alpha * x_ref[rows, :] + g1 * mix, LN_EPS) * l1g_ref[...] + l1b_ref[...]
        hb = (_norm_rows(x1, ADALN_EPS) * (1.0 + sc) + sh).astype(BF16)
        yield
        ff = None
        for j in range(n_chunks):
            sl = slice(j * fchunk, (j + 1) * fchunk)
            t = jnp.maximum(_dot(hb, w1_ref[:, sl]) + b1_ref[:, sl], 0.0)
            part = _dot((t * t).astype(BF16), w2_ref[sl, :])
            ff = part if ff is None else ff + part
            yield
        ff = ff + b2_ref[...]
        o_ref[rows, :] = _norm_rows(alpha * x1 + g2 * ff, LN_EPS) * l2g_ref[...] + l2b_ref[...]

    _run_skewed([rows_program(slice(0, half)), rows_program(slice(half, 2 * half))], lag=3)


def _mixmlp(x2, ada3, ya, ob, ga, gb, wa_bf, wb_bf, wo_bf, ln1_g, ln1_b, w1_bf, b1, w2_bf, b2, ln2_g, ln2_b,
            *, seq, tm, alpha, fchunk):
    m, d = x2.shape
    tiles_per_batch = seq // tm
    row = lambda i: (i, 0)
    const = lambda i: (0, 0)
    rows = lambda a: pl.BlockSpec((tm, a.shape[1]), row)
    full = lambda a: pl.BlockSpec(a.shape, const, pipeline_mode=pl.Buffered(1))
    vec = lambda a: a.reshape(1, -1)
    ins = [x2, ada3, ya, ob, ga, gb, wa_bf, wb_bf, wo_bf, vec(ln1_g), vec(ln1_b),
           w1_bf, vec(b1), w2_bf, vec(b2), vec(ln2_g), vec(ln2_b)]
    in_specs = [rows(x2), pl.BlockSpec((1,) + ada3.shape[1:], lambda i: (i // tiles_per_batch, 0, 0)),
                rows(ya), rows(ob), rows(ga), rows(gb)] + [full(a) for a in ins[6:]]
    return pl.pallas_call(
        functools.partial(_mixmlp_kernel, alpha=alpha, fchunk=fchunk),
        grid=(m // tm,),
        in_specs=in_specs,
        out_specs=pl.BlockSpec((tm, d), row),
        out_shape=jax.ShapeDtypeStruct((m, d), F32),
        compiler_params=pltpu.CompilerParams(dimension_semantics=("arbitrary",),
                                             vmem_limit_bytes=VMEM_LIMIT),
        name="mixmlp",
    )(*ins)


def _branches(proj, rw_rk, rw_gn_g, rw_gn_b, hsum, bsz, seq):
    r, k, v, lw, kk, b, g, q, ks, vs = proj
    seq3 = lambda a: a.reshape(bsz, seq, -1)
    ya = _rwkv(seq3(r), seq3(k), seq3(v), seq3(lw), seq3(kk), seq3(b), seq3(g),
               rw_rk, rw_gn_g, rw_gn_b, hsum, chunk=min(RWKV_CHUNK, seq),
               rows=min(RWKV_CHUNKS_PER_STEP * RWKV_CHUNK, seq))
    o = _sb_attention(seq3(q), seq3(ks), seq3(vs), blk=min(SB_BLOCK, seq))
    return ya.reshape(bsz * seq, -1), o.reshape(bsz * seq, -1)


def _layer(x, c, w_ada, b_ada, w_in, mu_shift, rw_w0, rw_w2, rw_a0, rw_a2, rw_g2, rw_kk, rw_ka,
           rw_rk, rw_gn_g, rw_gn_b, w_branch_a, w_branch_b, w_out, ln1_g, ln1_b,
           w_ff1, b_ff1, w_ff2, b_ff2, ln2_g, ln2_b, *, alpha):
    bsz, seq, d = x.shape
    d_rwkv = rw_w0.shape[-1]
    tm = min(INPROJ_ROWS, seq)
    tm2 = min(MLP_ROWS, seq)
    rwkv_rows = min(RWKV_CHUNKS_PER_STEP * RWKV_CHUNK, seq)
    group_w = RWKV_GROUP_HEADS * HEAD_DIM
    assert seq % tm == 0 and seq % tm2 == 0 and seq % rwkv_rows == 0 and seq % min(SB_BLOCK, seq) == 0, seq
    assert rwkv_rows % (RWKV_SECTIONS * RWKV_CHUNK) == 0, (rwkv_rows, RWKV_SECTIONS)
    assert d_rwkv % group_w == 0 and w_branch_b.shape[0] % (2 * HEAD_DIM) == 0, (d_rwkv, w_branch_b.shape)
    assert HEAD_DIM & (HEAD_DIM - 1) == 0 and RWKV_CHUNK & (RWKV_CHUNK - 1) == 0

    ada3 = _ada(c, w_ada, b_ada).reshape(bsz, 6, d)
    lane = jnp.arange(RWKV_GROUP_HEADS * HEAD_DIM) // HEAD_DIM
    hsum = (lane[:, None] == lane[None, :]).astype(BF16)

    x2 = x.reshape(bsz * seq, d)
    (r, k, v, lw, kk, b, g, q, ks, vs, ga, gb) = _inproj(
        x2, ada3, w_in.astype(BF16), mu_shift, rw_w0, rw_w2, rw_a0, rw_a2, rw_g2, rw_kk, rw_ka, hsum,
        seq=seq, tm=tm)

    ya, ob = _branches((r, k, v, lw, kk, b, g, q, ks, vs), rw_rk, rw_gn_g, rw_gn_b, hsum, bsz, seq)

    out = _mixmlp(x2, ada3, ya, ob, ga, gb,
                  w_branch_a.astype(BF16), w_branch_b.astype(BF16), w_out.astype(BF16), ln1_g, ln1_b,
                  w_ff1.astype(BF16), b_ff1, w_ff2.astype(BF16), b_ff2, ln2_g, ln2_b,
                  seq=seq, tm=tm2, alpha=alpha, fchunk=min(1024, w_ff1.shape[-1]))
    return out.reshape(bsz, seq, d)


def kernel(x, c, w_ada, b_ada, w_in, mu_shift, rw_w0, rw_w2, rw_a0, rw_a2, rw_g2, rw_kk, rw_ka, rw_rk,
           rw_gn_g, rw_gn_b, w_branch_a, w_branch_b, w_out, ln1_g, ln1_b, w_ff1, b_ff1, w_ff2, b_ff2,
           ln2_g, ln2_b):
    in_dtype = x.dtype
    layer_params = (w_ada, b_ada, w_in, mu_shift, rw_w0, rw_w2, rw_a0, rw_a2, rw_g2, rw_kk, rw_ka,
                    rw_rk, rw_gn_g, rw_gn_b, w_branch_a, w_branch_b, w_out, ln1_g, ln1_b,
                    w_ff1, b_ff1, w_ff2, b_ff2, ln2_g, ln2_b)
    depth = w_ada.shape[0]
    alpha = (2.0 * depth) ** 0.25
    for l in range(depth):
        x = _layer(x, c, *[p[l] for p in layer_params], alpha=alpha)
    return x.astype(in_dtype)
```

```python
import functools

import jax
import jax.numpy as jnp
from jax import lax
from jax.experimental import pallas as pl
from jax.experimental.pallas import tpu as pltpu

F32 = jnp.float32
BF16 = jnp.bfloat16

HEAD_DIM = 64
LN_EPS = 1e-5
ADALN_EPS = 1e-6
GN_EPS = 64e-5
RWKV_CHUNK = 64
RWKV_SECTIONS = 2
RWKV_SECTION_CHUNKS = 4
RWKV_SECTION_LAG = 12
RWKV_CHUNKS_PER_STEP = RWKV_SECTIONS * RWKV_SECTION_CHUNKS
SB_BLOCK = 256
INPROJ_ROWS = 512
MLP_ROWS = 512
VMEM_LIMIT = 56 * 1024 * 1024


def _split2(a):
    hi = a.astype(BF16)
    lo = (a - hi.astype(F32)).astype(BF16)
    return hi, lo


_NN = (((1,), (0,)), ((), ()))
_NT = (((1,), (1,)), ((), ()))
_TN = (((0,), (0,)), ((), ()))


def _dot(a, b, dims=_NN):
    return lax.dot_general(a, b, dims, preferred_element_type=F32)


def _head_sums(x, ones_blk):
    m, w = x.shape[0], ones_blk.shape[0]
    hi, lo = _split2(x)
    outs = []
    for g in range(x.shape[1] // w):
        cols = slice(g * w, (g + 1) * w)
        both = _dot(jnp.concatenate([hi[:, cols], lo[:, cols]], axis=0), ones_blk)
        outs.append(both[:m] + both[m:])
    return jnp.concatenate(outs, axis=1)


def _dot_ones_lhs(a_bf, b):
    hi, lo = _split2(b)
    return _dot(a_bf, hi) + _dot(a_bf, lo)


_DONE = object()


def _run_skewed(programs, lag):
    live = [True] * len(programs)
    tick = 0
    while any(live):
        for i, prog in enumerate(programs):
            if live[i] and tick >= i * lag:
                live[i] = next(prog, _DONE) is not _DONE
        tick += 1


_NEG_LOG2_E = -1.4426950408889634


def _softplus(y):
    return jnp.maximum(y, 0.0) + jnp.log(1.0 + jnp.exp2(jnp.abs(y) * _NEG_LOG2_E))


def _sigmoid(y):
    return 1.0 / (1.0 + jnp.exp(-y))


def _norm_rows(x, eps):
    mu = jnp.mean(x, axis=-1, keepdims=True)
    xc = x - mu
    var = jnp.mean(xc * xc, axis=-1, keepdims=True)
    return xc * lax.rsqrt(var + eps)


def _ada_kernel(c_ref, w_ref, b_ref, o_ref):
    c = c_ref[...]
    s = c * _sigmoid(c)
    m = s.shape[0]
    sh, sl = _split2(s)
    wh, wl = _split2(w_ref[...])
    top = _dot(jnp.concatenate([sh, sl], axis=0), wh)
    o_ref[...] = top[:m] + (top[m:] + _dot(sh, wl)) + b_ref[...]


def _ada(c, w_ada, b_ada):
    bsz, d = c.shape
    n = w_ada.shape[1]
    wide = 2 * d if n % (2 * d) == 0 else d
    return pl.pallas_call(
        _ada_kernel,
        grid=(n // wide,),
        in_specs=[pl.BlockSpec((bsz, d), lambda j: (0, 0)),
                  pl.BlockSpec((d, wide), lambda j: (0, j)),
                  pl.BlockSpec((1, wide), lambda j: (0, j))],
        out_specs=pl.BlockSpec((bsz, wide), lambda j: (0, j)),
        out_shape=jax.ShapeDtypeStruct((bsz, n), F32),
        name="ada",
    )(c, w_ada, b_ada.reshape(1, n))


def _inproj_kernel(x_ref, ada_ref, win_ref, mu_ref, w0_ref, w2_ref, a0_ref, a2_ref, g2_ref,
                   kkw_ref, kaw_ref, hsum_ref,
                   r_ref, k_ref, v_ref, lw_ref, kk_ref, b_ref, g_ref,
                   q_ref, ks_ref, vs_ref, ga_ref, gb_ref,
                   carry_ref, *, tiles_per_batch, d_rwkv, d_decay, d_aaa, d_gate, d_sb, d_model):
    i = pl.program_id(0)
    tm = x_ref.shape[0]
    d_shift = 3 * d_rwkv + d_decay + d_aaa + d_gate

    sh = ada_ref[0, 0:1, :]
    sc = ada_ref[0, 1:2, :]

    @pl.when(i % tiles_per_batch == 0)
    def _():
        carry_ref[...] = jnp.zeros_like(carry_ref)

    half = d_model // 2
    pair_w = 2 * HEAD_DIM
    first_head = lax.broadcasted_iota(jnp.int32, (1, pair_w), 1) < HEAD_DIM

    def rows_program(rows):
        hm = rows.stop - rows.start
        hb = (_norm_rows(x_ref[rows, :], ADALN_EPS) * (1.0 + sc) + sh).astype(BF16)
        row0 = lax.broadcasted_iota(jnp.int32, (hm, 1), 0) == 0
        yield

        def shifted(lo, width):
            cols = slice(lo, lo + width)
            z = _dot(hb, win_ref[:, cols])
            prev = jnp.where(row0, carry_ref[:, cols], pltpu.roll(z, 1, 0))
            carry_ref[:, cols] = z[hm - 1:hm, :]
            return z + mu_ref[:, cols] * (prev - z)

        def store_queries(z):
            zq = (z * (HEAD_DIM ** -0.5)).astype(BF16)
            zero = jnp.zeros((), BF16)
            for p in range(d_sb // pair_w):
                qp = zq[:, p * pair_w:(p + 1) * pair_w]
                q_ref[rows, (2 * p) * pair_w:(2 * p + 1) * pair_w] = jnp.where(first_head, qp, zero)
                q_ref[rows, (2 * p + 1) * pair_w:(2 * p + 2) * pair_w] = jnp.where(first_head, zero, qp)

        def plain(ref, cols, fn=lambda z: z):
            def store(z):
                ref[rows, cols] = fn(z).astype(BF16)
            return store

        all_cols = slice(0, d_sb)
        rest = [(store_queries, d_sb), (plain(ks_ref, all_cols), d_sb), (plain(vs_ref, all_cols), d_sb),
                (plain(ga_ref, slice(0, half), _sigmoid), half),
                (plain(ga_ref, slice(half, d_model), _sigmoid), half),
                (plain(gb_ref, slice(0, half), _sigmoid), half),
                (plain(gb_ref, slice(half, d_model), _sigmoid), half)]
        rest_col = [d_shift]

        def project_next():
            store, width = rest.pop(0)
            store(_dot(hb, win_ref[:, rest_col[0]:rest_col[0] + width]))
            rest_col[0] += width

        zl = shifted(3 * d_rwkv, d_decay + d_aaa + d_gate)
        zw = zl[:, :d_decay]
        za = zl[:, d_decay:d_decay + d_aaa]
        zg = zl[:, d_decay + d_aaa:]
        k = shifted(d_rwkv, d_rwkv)
        yield
        ww = w0_ref[...] + _dot(jnp.tanh(zw).astype(BF16), w2_ref[...])
        w_log = -_softplus(-ww) - 0.5
        lw_ref[rows, :] = -jnp.exp(w_log)
        project_next()
        yield
        arate = _sigmoid(a0_ref[...] + _dot(za.astype(BF16), a2_ref[...]))
        g_ref[rows, :] = _dot(_sigmoid(zg).astype(BF16), g2_ref[...])
        r_ref[rows, :] = shifted(0, d_rwkv)
        yield
        kkraw = k * kkw_ref[...]
        ssq = _head_sums(kkraw * kkraw, hsum_ref[...])
        kk = kkraw / jnp.maximum(jnp.sqrt(ssq), 1e-12)
        v_ref[rows, :] = shifted(2 * d_rwkv, d_rwkv)
        k_ref[rows, :] = k * (1.0 + (arate - 1.0) * kaw_ref[...])
        yield
        project_next()
        kk_ref[rows, :] = kk
        b_ref[rows, :] = kk * arate
        yield
        while rest:
            project_next()
            yield

    hm = tm // 2
    _run_skewed([rows_program(slice(0, hm)), rows_program(slice(hm, tm))], lag=2)


def _inproj(x2, ada3, win_bf, mu_shift, rw_w0, rw_w2, rw_a0, rw_a2, rw_g2, rw_kk, rw_ka, hsum,
            *, seq, tm):
    m, d_model = x2.shape
    d_rwkv = rw_w0.shape[-1]
    d_decay, d_aaa, d_gate = rw_w2.shape[0], rw_a2.shape[0], rw_g2.shape[0]
    d_shift = 3 * d_rwkv + d_decay + d_aaa + d_gate
    d_sb = (win_bf.shape[1] - d_shift - 2 * d_model) // 3
    tiles_per_batch = seq // tm
    row = lambda i: (i, 0)
    const = lambda i: (0, 0)
    full = lambda a: pl.BlockSpec(a.shape, const, pipeline_mode=pl.Buffered(1))
    vec = lambda a: a.reshape(1, -1)
    ins = [x2, ada3, win_bf, vec(mu_shift), vec(rw_w0), rw_w2.astype(BF16), vec(rw_a0),
           rw_a2.astype(BF16), rw_g2.astype(BF16), vec(rw_kk), vec(rw_ka), hsum]
    in_specs = [pl.BlockSpec((tm, d_model), row),
                pl.BlockSpec((1,) + ada3.shape[1:], lambda i: (i // tiles_per_batch, 0, 0))]
    in_specs += [full(a) for a in ins[2:]]
    widths = [d_rwkv] * 7 + [2 * d_sb, d_sb, d_sb] + [d_model] * 2
    dtypes = [F32] * 7 + [BF16] * 5
    out_shape = [jax.ShapeDtypeStruct((m, w), dt) for w, dt in zip(widths, dtypes)]
    out_specs = [pl.BlockSpec((tm, w), row) for w in widths]
    kern = functools.partial(_inproj_kernel, tiles_per_batch=tiles_per_batch, d_rwkv=d_rwkv,
                             d_decay=d_decay, d_aaa=d_aaa, d_gate=d_gate, d_sb=d_sb, d_model=d_model)
    return pl.pallas_call(
        kern,
        grid=(m // tm,),
        in_specs=in_specs,
        out_specs=out_specs,
        out_shape=out_shape,
        scratch_shapes=[pltpu.VMEM((1, d_shift), F32)],
        compiler_params=pltpu.CompilerParams(dimension_semantics=("arbitrary",),
                                             vmem_limit_bytes=VMEM_LIMIT),
        name="inproj",
    )(*ins)


RWKV_GROUP_HEADS = 4


def _block_diag(q, head_masks):
    zero = jnp.zeros((), q.dtype)
    return jnp.concatenate([jnp.where(m, q, zero) for m in head_masks], axis=0)


def _packed_dot1(lhs, q, head_masks, dims=_NN):
    return _dot(lhs.astype(BF16), _block_diag(q.astype(BF16), head_masks), dims)


def _packed_dot_wide_lhs(lhs, q, head_masks, dims=_NN):
    m = lhs.shape[0]
    lh, ll = _split2(lhs)
    both = _dot(jnp.concatenate([lh, ll], axis=0), _block_diag(q.astype(BF16), head_masks), dims)
    return both[:m] + both[m:]


def _rwkv_kernel(r_ref, k_ref, v_ref, lw_ref, kk_ref, b_ref, g_ref, rk_ref, gng_ref, gnb_ref, hsum_ref,
                 o_ref, state_ref, *, chunk):
    c = pl.program_id(1)
    n = HEAD_DIM
    rows, d = r_ref.shape[1], r_ref.shape[2]
    gw = RWKV_GROUP_HEADS * n
    n_groups = d // gw

    @pl.when(c == 0)
    def _():
        state_ref[...] = jnp.zeros_like(state_ref)

    t_idx = lax.broadcasted_iota(jnp.int32, (chunk, gw), 0)
    lane = lax.broadcasted_iota(jnp.int32, (chunk, gw), 1)
    i_idx = lane & (n - 1)
    strict = t_idx > i_idx
    incl = t_idx >= i_idx
    lane1 = lax.broadcasted_iota(jnp.int32, (1, gw), 1)
    hm = [(lane1 >> (n.bit_length() - 1)) == h for h in range(RWKV_GROUP_HEADS)]
    same = lambda s: (t_idx >> s) == (i_idx >> s)

    half_w = gw // 2
    lane_half = [(lane1 >> (half_w.bit_length() - 1)) == j for j in range(2)]
    second_in_half = ((lane >> (n.bit_length() - 1)) & 1) == 1

    def halves_rows(x):
        return jnp.concatenate([x[:, :half_w], x[:, half_w:]], axis=0).astype(BF16)

    def halves_masked(w):
        return _block_diag(w.astype(BF16), lane_half)

    def same_head(out):
        return jnp.where(second_in_half, out[n:], out[:n])

    lane_h = lax.broadcasted_iota(jnp.int32, (1, half_w), 1)
    in_half = [(lane_h >> (n.bit_length() - 1)) == j for j in range(half_w // n)]

    def two_rhs(lhs, x, y, dims=_NN):
        lb, xb, yb = lhs.astype(BF16), x.astype(BF16), y.astype(BF16)
        out_x, out_y = [], []
        for j in range(gw // half_w):
            cols = slice(j * half_w, (j + 1) * half_w)
            w = jnp.concatenate([_block_diag(xb[:, cols], in_half), _block_diag(yb[:, cols], in_half)],
                                axis=1 if dims == _NN else 0)
            o = _dot(lb[:, cols], w, dims)
            out_x.append(o[:, :half_w])
            out_y.append(o[:, half_w:])
        return jnp.concatenate(out_x, axis=1), jnp.concatenate(out_y, axis=1)

    sec_rows = rows // RWKV_SECTIONS
    sec_chunks = sec_rows // chunk
    rr = lax.broadcasted_iota(jnp.int32, (sec_rows, sec_rows), 0)
    cc = lax.broadcasted_iota(jnp.int32, (sec_rows, sec_rows), 1)
    cs = chunk.bit_length() - 1
    ltri = jnp.where((rr >= cc) & ((rr >> cs) == (cc >> cs)), 1.0, 0.0).astype(BF16)
    eye = jnp.where(t_idx == i_idx, 1.0, 0.0)
    chains = [(ci, gi) for ci in range(sec_chunks) for gi in range(n_groups)]
    win = lambda x, ci, gi: x[ci * chunk:(ci + 1) * chunk, gi * gw:(gi + 1) * gw]

    def section(row0):
        rs = slice(row0, row0 + sec_rows)
        r, k, v, lw = r_ref[0, rs, :], k_ref[0, rs, :], v_ref[0, rs, :], lw_ref[0, rs, :]
        kk, b = kk_ref[0, rs, :], b_ref[0, rs, :]
        cum = _dot_ones_lhs(ltri, lw)
        a_t = -kk * jnp.exp(cum - lw)
        r_t = r * jnp.exp(cum)
        e_neg = jnp.exp(-cum)
        b_t = b * e_neg
        k_t = k * e_neg
        tots = [cum[(ci + 1) * chunk - 1:(ci + 1) * chunk, :] for ci in range(sec_chunks)]
        e_bars = [jnp.exp(tots[ci] - cum[ci * chunk:(ci + 1) * chunk, :]) for ci in range(sec_chunks)]
        b_bar = [win(b, ci, gi) * e_bars[ci][:, gi * gw:(gi + 1) * gw] for ci, gi in chains]
        k_bar = [win(k, ci, gi) * e_bars[ci][:, gi * gw:(gi + 1) * gw] for ci, gi in chains]
        at = [win(a_t, ci, gi) for ci, gi in chains]
        rt = [win(r_t, ci, gi) for ci, gi in chains]
        vv = [win(v, ci, gi) for ci, gi in chains]
        yield

        ar = [jnp.concatenate([a_, r_], axis=0) for a_, r_ in zip(at, rt)]
        grams = [two_rhs(x, win(b_t, ci, gi), win(k_t, ci, gi), _NT) for x, (ci, gi) in zip(ar, chains)]
        gram_b = [x[0] for x in grams]
        gram_k = [x[1] for x in grams]
        yield
        m_ab = [jnp.where(strict, g_[:chunk], 0.0) for g_ in gram_b]
        p_rb = [jnp.where(incl, g_[chunk:], 0.0) for g_ in gram_b]
        m_ak = [jnp.where(strict, g_[:chunk], 0.0) for g_ in gram_k]
        p_rk = [jnp.where(incl, g_[chunk:], 0.0) for g_ in gram_k]
        yield

        ts = [eye + jnp.where(same(1), m, 0.0) for m in m_ab]
        s = 1
        while (1 << s) < chunk:
            level = same(s + 1) & jnp.logical_not(same(s))
            offs = [jnp.where(level, m, 0.0) for m in m_ab]
            mids = [_packed_dot1(t, o, hm) for t, o in zip(ts, offs)]
            yield
            ts = [t + _packed_dot1(md, t, hm) for t, md in zip(ts, mids)]
            yield
            s += 1

        both = [_packed_dot1(jnp.concatenate([m, pk], axis=0), x, hm) for m, pk, x in zip(m_ak, p_rk, vv)]
        mv = [x[:chunk] for x in both]
        pkv = [x[chunk:] for x in both]
        yield
        hat = [two_rhs(t, x, y) for t, x, y in zip(ts, at, mv)]
        a_hat = [x[0] for x in hat]
        u0 = [x[1] for x in hat]
        yield
        pb = [two_rhs(p, a, u) for p, a, u in zip(p_rb, a_hat, u0)]
        r_hat = [x + y[0] for x, y in zip(rt, pb)]
        y0 = [x + y[1] for x, y in zip(pkv, pb)]
        yield
        g_mat = [same_head(_dot(halves_rows(a), halves_masked(bb), _TN))
                 for a, bb in zip(a_hat, b_bar)]
        h_mat = [same_head(_dot(jnp.concatenate([halves_rows(u), halves_rows(x)], axis=0),
                                jnp.concatenate([halves_masked(bb), halves_masked(kb)], axis=0), _TN))
                 for u, x, bb, kb in zip(u0, vv, b_bar, k_bar)]
        yield

        y_rows = []
        for ci in range(sec_chunks):
            y_groups = []
            for gi in range(n_groups):
                j = ci * n_groups + gi
                s0 = state_ref[gi]
                y_groups.append(_packed_dot1(r_hat[j], s0, hm, _NT) + y0[j])
                w_tot = jnp.exp(tots[ci][:, gi * gw:(gi + 1) * gw])
                state_ref[gi] = s0 * w_tot + _packed_dot_wide_lhs(s0, g_mat[j], hm) + h_mat[j]
            y_rows.append(jnp.concatenate(y_groups, axis=1))
            yield
        y = jnp.concatenate(y_rows, axis=0)

        hsum = hsum_ref[...]
        inv_n = 1.0 / n
        mu = _head_sums(y, hsum) * inv_n
        yc = y - mu
        var = _head_sums(yc * yc, hsum) * inv_n
        yn = yc * lax.rsqrt(var + GN_EPS) * gng_ref[...] + gnb_ref[...]
        bonus = _head_sums(r * k * rk_ref[...], hsum)
        o_ref[0, rs, :] = ((yn + bonus * v) * g_ref[0, rs, :]).astype(o_ref.dtype)

    assert RWKV_SECTION_LAG >= sec_chunks
    _run_skewed([section(si * sec_rows) for si in range(RWKV_SECTIONS)], lag=RWKV_SECTION_LAG)


def _rwkv(r, k, v, lw, kk, b, g, rw_rk, gn_g, gn_b, hsum, *, chunk, rows):
    bsz, seq, d = r.shape
    n_groups = d // (RWKV_GROUP_HEADS * HEAD_DIM)
    blk = pl.BlockSpec((1, rows, d), lambda bi, ci: (bi, ci, 0))
    const = lambda bi, ci: (0, 0)
    vec = lambda a: a.reshape(1, d)
    return pl.pallas_call(
        functools.partial(_rwkv_kernel, chunk=chunk),
        grid=(bsz, seq // rows),
        in_specs=[blk] * 7 + [pl.BlockSpec((1, d), const)] * 3 + [pl.BlockSpec(hsum.shape, const)],
        out_specs=blk,
        out_shape=jax.ShapeDtypeStruct((bsz, seq, d), BF16),
        scratch_shapes=[pltpu.VMEM((n_groups, HEAD_DIM, RWKV_GROUP_HEADS * HEAD_DIM), F32)],
        compiler_params=pltpu.CompilerParams(dimension_semantics=("arbitrary", "arbitrary"),
                                             vmem_limit_bytes=VMEM_LIMIT),
        name="rwkv",
    )(r, k, v, lw, kk, b, g, vec(rw_rk), vec(gn_g), vec(gn_b), hsum)


SB_LOG_UNDERFLOW = 105.0


def _sb_kernel(q_ref, k_ref, v_ref, o_ref, acc_ref, run_ref, *, blk):
    qi = pl.program_id(1)
    pairs = k_ref.shape[2] // (2 * HEAD_DIM)
    row = lax.broadcasted_iota(jnp.int32, (blk, blk), 0)
    col = lax.broadcasted_iota(jnp.int32, (blk, blk), 1)
    causal = col < row
    jj = lax.broadcasted_iota(jnp.int32, (blk, 2 * blk), 0)
    ss = lax.broadcasted_iota(jnp.int32, (blk, 2 * blk), 1)
    suffix = jnp.where((jj >= ss) | (ss >= blk), 1.0, 0.0).astype(BF16)
    first_head = lax.broadcasted_iota(jnp.int32, (1, 2 * HEAD_DIM), 1) < HEAD_DIM
    zero = jnp.zeros((), BF16)

    lanes = [slice(p * 2 * HEAD_DIM, (p + 1) * 2 * HEAD_DIM) for p in range(pairs)]

    pw = 2 * HEAD_DIM

    def tile(kb, diagonal):
        start = pl.multiple_of(kb * blk, blk)
        zs = []
        for p in range(pairs):
            q2 = jnp.concatenate([q_ref[0, :, (2 * p + hh) * pw:(2 * p + hh + 1) * pw] for hh in range(2)],
                                 axis=0)
            z2 = _dot(q2, k_ref[0, pl.ds(start, blk), lanes[p]], _NT)
            zs += [z2[:blk], z2[blk:]]
        sps = [_softplus(z) for z in zs]
        if diagonal:
            sps = [jnp.where(causal, sp, 0.0) for sp in sps]
        css = []
        for p in range(pairs):
            c2 = _dot(jnp.concatenate(sps[2 * p:2 * p + 2], axis=0).astype(BF16), suffix)
            css += [c2[:blk], c2[blk:]]
        atts = []
        low = None
        for n_, (z, cs) in enumerate(zip(zs, css)):
            logw = z - cs[:, :blk]
            if diagonal:
                att = jnp.where(causal, jnp.exp(logw), 0.0)
                run = cs[:, blk:]
            else:
                prev = run_ref[n_]
                att = jnp.exp(logw - prev)
                run = prev + cs[:, blk:]
            run_ref[n_] = run
            low = run if low is None else jnp.minimum(low, run)
            atts.append(att.astype(BF16))
        done = (jnp.min(low) > SB_LOG_UNDERFLOW).astype(jnp.int32)
        for p in range(pairs):
            vp = v_ref[0, pl.ds(start, blk), lanes[p]]
            vv = jnp.concatenate([jnp.where(first_head, vp, zero),
                                  jnp.where(first_head, zero, vp)], axis=0)
            out = _dot(jnp.concatenate(atts[2 * p:2 * p + 2], axis=1), vv)
            if diagonal:
                acc_ref[:, lanes[p]] = out
            else:
                acc_ref[:, lanes[p]] += out
        return done

    done0 = tile(qi, True)

    def cond(c):
        i, done = c
        return jnp.logical_and(i <= qi, done == 0)

    def body(c):
        i, _ = c
        return i + 1, tile(qi - i, False)

    lax.while_loop(cond, body, (jnp.int32(1), done0))
    o_ref[0] = acc_ref[...].astype(o_ref.dtype)


def _sb_attention(q, k, v, *, blk):
    bsz, seq, d = k.shape
    heads = d // HEAD_DIM
    qspec = pl.BlockSpec((1, blk, 2 * d), lambda b, i: (b, i, 0))
    ospec = pl.BlockSpec((1, blk, d), lambda b, i: (b, i, 0))
    kvspec = pl.BlockSpec((1, seq, d), lambda b, i: (b, 0, 0))
    return pl.pallas_call(
        functools.partial(_sb_kernel, blk=blk),
        grid=(bsz, seq // blk),
        in_specs=[qspec, kvspec, kvspec],
        out_specs=ospec,
        out_shape=jax.ShapeDtypeStruct((bsz, seq, d), BF16),
        scratch_shapes=[pltpu.VMEM((blk, d), F32), pltpu.VMEM((heads, blk, blk), F32)],
        compiler_params=pltpu.CompilerParams(dimension_semantics=("arbitrary", "arbitrary"),
                                             vmem_limit_bytes=VMEM_LIMIT),
        name="sbattn",
    )(q, k, v)


def _mixmlp_kernel(x_ref, ada_ref, ya_ref, ob_ref, ga_ref, gb_ref, wa_ref, wb_ref, wo_ref, l1g_ref, l1b_ref,
                   w1_ref, b1_ref, w2_ref, b2_ref, l2g_ref, l2b_ref, o_ref, *, alpha, fchunk):
    g1 = ada_ref[0, 2:3, :]
    sh = ada_ref[0, 3:4, :]
    sc = ada_ref[0, 4:5, :]
    g2 = ada_ref[0, 5:6, :]
    n_chunks = w1_ref.shape[1] // fchunk
    half = x_ref.shape[0] // 2

    def rows_program(rows):
        y_a = _dot(ya_ref[rows, :], wa_ref[...])
        y_b = _dot(ob_ref[rows, :], wb_ref[...])
        yield
        merged = ga_ref[rows, :].astype(F32) * y_a + gb_ref[rows, :].astype(F32) * y_b
        mix = _dot(merged.astype(BF16), wo_ref[...])
        yield
        x1 = _norm_rows(alpha * x_ref[rows, :] + g1 * mix, LN_EPS) * l1g_ref[...] + l1b_ref[...]
        hb = (_norm_rows(x1, ADALN_EPS) * (1.0 + sc) + sh).astype(BF16)
        yield
        ff = None
        for j in range(n_chunks):
            sl = slice(j * fchunk, (j + 1) * fchunk)
            t = jnp.maximum(_dot(hb, w1_ref[:, sl]) + b1_ref[:, sl], 0.0)
            part = _dot((t * t).astype(BF16), w2_ref[sl, :])
            ff = part if ff is None else ff + part
            yield
        ff = ff + b2_ref[...]
        o_ref[rows, :] = _norm_rows(alpha * x1 + g2 * ff, LN_EPS) * l2g_ref[...] + l2b_ref[...]

    _run_skewed([rows_program(slice(0, half)), rows_program(slice(half, 2 * half))], lag=1)


def _mixmlp(x2, ada3, ya, ob, ga, gb, wa_bf, wb_bf, wo_bf, ln1_g, ln1_b, w1_bf, b1, w2_bf, b2, ln2_g, ln2_b,
            *, seq, tm, alpha, fchunk):
    m, d = x2.shape
    tiles_per_batch = seq // tm
    row = lambda i: (i, 0)
    const = lambda i: (0, 0)
    rows = lambda a: pl.BlockSpec((tm, a.shape[1]), row)
    full = lambda a: pl.BlockSpec(a.shape, const, pipeline_mode=pl.Buffered(1))
    vec = lambda a: a.reshape(1, -1)
    ins = [x2, ada3, ya, ob, ga, gb, wa_bf, wb_bf, wo_bf, vec(ln1_g), vec(ln1_b),
           w1_bf, vec(b1), w2_bf, vec(b2), vec(ln2_g), vec(ln2_b)]
    in_specs = [rows(x2), pl.BlockSpec((1,) + ada3.shape[1:], lambda i: (i // tiles_per_batch, 0, 0)),
                rows(ya), rows(ob), rows(ga), rows(gb)] + [full(a) for a in ins[6:]]
    return pl.pallas_call(
        functools.partial(_mixmlp_kernel, alpha=alpha, fchunk=fchunk),
        grid=(m // tm,),
        in_specs=in_specs,
        out_specs=pl.BlockSpec((tm, d), row),
        out_shape=jax.ShapeDtypeStruct((m, d), F32),
        compiler_params=pltpu.CompilerParams(dimension_semantics=("arbitrary",),
                                             vmem_limit_bytes=VMEM_LIMIT),
        name="mixmlp",
    )(*ins)


def _branches(proj, rw_rk, rw_gn_g, rw_gn_b, hsum, bsz, seq):
    r, k, v, lw, kk, b, g, q, ks, vs = proj
    seq3 = lambda a: a.reshape(bsz, seq, -1)
    ya = _rwkv(seq3(r), seq3(k), seq3(v), seq3(lw), seq3(kk), seq3(b), seq3(g),
               rw_rk, rw_gn_g, rw_gn_b, hsum, chunk=min(RWKV_CHUNK, seq),
               rows=min(RWKV_CHUNKS_PER_STEP * RWKV_CHUNK, seq))
    o = _sb_attention(seq3(q), seq3(ks), seq3(vs), blk=min(SB_BLOCK, seq))
    return ya.reshape(bsz * seq, -1), o.reshape(bsz * seq, -1)


def _layer(x, c, w_ada, b_ada, w_in, mu_shift, rw_w0, rw_w2, rw_a0, rw_a2, rw_g2, rw_kk, rw_ka,
           rw_rk, rw_gn_g, rw_gn_b, w_branch_a, w_branch_b, w_out, ln1_g, ln1_b,
           w_ff1, b_ff1, w_ff2, b_ff2, ln2_g, ln2_b, *, alpha):
    bsz, seq, d = x.shape
    d_rwkv = rw_w0.shape[-1]
    tm = min(INPROJ_ROWS, seq)
    tm2 = min(MLP_ROWS, seq)
    rwkv_rows = min(RWKV_CHUNKS_PER_STEP * RWKV_CHUNK, seq)
    group_w = RWKV_GROUP_HEADS * HEAD_DIM
    assert seq % tm == 0 and seq % tm2 == 0 and seq % rwkv_rows == 0 and seq % min(SB_BLOCK, seq) == 0, seq
    assert rwkv_rows % (RWKV_SECTIONS * RWKV_CHUNK) == 0, (rwkv_rows, RWKV_SECTIONS)
    assert d_rwkv % group_w == 0 and w_branch_b.shape[0] % (2 * HEAD_DIM) == 0, (d_rwkv, w_branch_b.shape)
    assert HEAD_DIM & (HEAD_DIM - 1) == 0 and RWKV_CHUNK & (RWKV_CHUNK - 1) == 0

    ada3 = _ada(c, w_ada, b_ada).reshape(bsz, 6, d)
    lane = jnp.arange(RWKV_GROUP_HEADS * HEAD_DIM) // HEAD_DIM
    hsum = (lane[:, None] == lane[None, :]).astype(BF16)

    x2 = x.reshape(bsz * seq, d)
    (r, k, v, lw, kk, b, g, q, ks, vs, ga, gb) = _inproj(
        x2, ada3, w_in.astype(BF16), mu_shift, rw_w0, rw_w2, rw_a0, rw_a2, rw_g2, rw_kk, rw_ka, hsum,
        seq=seq, tm=tm)

    ya, ob = _branches((r, k, v, lw, kk, b, g, q, ks, vs), rw_rk, rw_gn_g, rw_gn_b, hsum, bsz, seq)

    out = _mixmlp(x2, ada3, ya, ob, ga, gb,
                  w_branch_a.astype(BF16), w_branch_b.astype(BF16), w_out.astype(BF16), ln1_g, ln1_b,
                  w_ff1.astype(BF16), b_ff1, w_ff2.astype(BF16), b_ff2, ln2_g, ln2_b,
                  seq=seq, tm=tm2, alpha=alpha, fchunk=min(1024, w_ff1.shape[-1]))
    return out.reshape(bsz, seq, d)


def kernel(x, c, w_ada, b_ada, w_in, mu_shift, rw_w0, rw_w2, rw_a0, rw_a2, rw_g2, rw_kk, rw_ka, rw_rk,
           rw_gn_g, rw_gn_b, w_branch_a, w_branch_b, w_out, ln1_g, ln1_b, w_ff1, b_ff1, w_ff2, b_ff2,
           ln2_g, ln2_b):
    in_dtype = x.dtype
    layer_params = (w_ada, b_ada, w_in, mu_shift, rw_w0, rw_w2, rw_a0, rw_a2, rw_g2, rw_kk, rw_ka,
                    rw_rk, rw_gn_g, rw_gn_b, w_branch_a, w_branch_b, w_out, ln1_g, ln1_b,
                    w_ff1, b_ff1, w_ff2, b_ff2, ln2_g, ln2_b)
    depth = w_ada.shape[0]
    alpha = (2.0 * depth) ** 0.25
    for l in range(depth):
        x = _layer(x, c, *[p[l] for p in layer_params], alpha=alpha)
    return x.astype(in_dtype)
```
